```python
import math
import jax
import jax.numpy as jnp
from jax import lax
import numpy as np

D_MODEL = 1024
BATCH = 2
SEQ = 16384
DEPTH = 2

CTX_LEN = 256
GRID_W = 64
N_MOD = 6
NORM_EPS = 1e-6

FOURIER_GROUPS = 4
FOURIER_GD = 64
FOURIER_W = FOURIER_GROUPS * FOURIER_GD
DA_HEADS = 6
DA_DH = 64
DA_VD = 2 * DA_DH
DA_W = DA_HEADS * DA_VD
EVEN_IN = FOURIER_W + 3 * DA_W
EVEN_MIX = FOURIER_W + DA_W
Q_BLOCK = 128
ROPE_BASE = 10000.0
ROPE_AXIS_DIM = DA_DH // 2
SUBLN_EPS = 1e-5

GDN_HK = 8
GDN_HV = 16
GDN_DK = 128
GDN_DV = 128
GDN_QK_W = GDN_HK * GDN_DK
GDN_V_W = GDN_HV * GDN_DV
GDN_QKV_W = 2 * GDN_QK_W + GDN_V_W
GDN_CONV = 5
GDN_CHUNK = 64
ODD_IN = GDN_QKV_W + GDN_V_W + 4 * GDN_HV

N_EXPERTS = 32
TOP_K = 4
D_FF = 1024
SWIGLU_LIMIT = 7.0
SWIGLU_ALPHA = 1.702
MOE_BLOCK = 512

N_EVEN = (DEPTH + 1) // 2
N_ODD = DEPTH // 2

kernel_name = "hybrid_fourier_diffattn_gdn_moe_dit"


def rms_norm(x, g, eps=NORM_EPS):
    xf = x.astype(jnp.float32)
    y = xf * lax.rsqrt(jnp.mean(xf * xf, axis=-1, keepdims=True) + eps)
    return y.astype(x.dtype) * g


def modulate(h, shift, scale):
    return h * (1 + scale) + shift


def l2_normalize(x, eps=1e-6):
    return x * lax.rsqrt(jnp.sum(x * x, axis=-1, keepdims=True) + eps)


def diff_lambda_init(layer_idx):
    return 0.8 - 0.6 * math.exp(-0.3 * layer_idx)


def axial_rope_tables(rows):
    t = jnp.arange(rows * GRID_W, dtype=jnp.int32)
    row = (t // GRID_W).astype(jnp.float32)
    col = (t % GRID_W).astype(jnp.float32)
    inv = ROPE_BASE ** (-jnp.arange(0, ROPE_AXIS_DIM, 2, dtype=jnp.float32) / ROPE_AXIS_DIM)
    ang = jnp.concatenate([row[:, None] * inv, col[:, None] * inv], axis=-1)
    return jnp.cos(ang), jnp.sin(ang)


def rope2d(x, cos, sin):
    half = x.shape[-1] // 2
    xf = x.astype(jnp.float32)
    x1, x2 = xf[..., :half], xf[..., half:]
    cb = cos[None, :, None, None, :]
    sb = sin[None, :, None, None, :]
    return jnp.concatenate([x1 * cb - x2 * sb, x2 * cb + x1 * sb], axis=-1).astype(x.dtype)


def fourier_mix(f):
    ff = jnp.moveaxis(f.astype(jnp.float32), 2, 1)
    out = jnp.real(jnp.fft.fft2(ff, norm='ortho'))
    return jnp.moveaxis(out, 1, 2).astype(f.dtype)


def _diff_attend(q, k, v, lam):
    s = jnp.einsum('bqhmd,bkhmd->bhmqk', q, k, preferred_element_type=jnp.float32) * (DA_DH ** -0.5)
    p = jax.nn.softmax(s, axis=-1)
    a = (p[:, :, 0] - lam * p[:, :, 1]).astype(v.dtype)
    return jnp.einsum('bhqk,bkhe->bqhe', a, v)


def _even_split(p):
    b, n, _ = p.shape
    f = p[..., :FOURIER_W].reshape(b, n, FOURIER_GROUPS, FOURIER_GD)
    q, k, v = jnp.split(p[..., FOURIER_W:], 3, axis=-1)
    q = q.reshape(b, n, DA_HEADS, 2, DA_DH)
    k = k.reshape(b, n, DA_HEADS, 2, DA_DH)
    v = v.reshape(b, n, DA_HEADS, DA_VD)
    return f, q, k, v


def _even_merge(f, o, subln_g, lam_init, w_out):
    b, n = f.shape[:2]
    o = rms_norm(o, subln_g, SUBLN_EPS) * (1.0 - lam_init)
    mixed = jnp.concatenate([fourier_mix(f).reshape(b, n, FOURIER_W), o.reshape(b, n, DA_W)], axis=-1)
    return mixed @ w_out


def fourier_diff_mixer(h, hc, w_in, w_out, lam_p, subln_g, lam_init, cos, sin, need_ctx):
    b, n, _ = h.shape
    f, q, k, v = _even_split(h @ w_in)
    fc, qc, kc, vc = _even_split(hc @ w_in)
    q = rope2d(q, cos, sin)
    k = rope2d(k, cos, sin)
    lp = lam_p.astype(jnp.float32)
    lam = jnp.exp(jnp.sum(lp[0] * lp[1])) - jnp.exp(jnp.sum(lp[2] * lp[3])) + lam_init
    k_all = jnp.concatenate([k, kc], axis=1)
    v_all = jnp.concatenate([v, vc], axis=1)
    nb = n // Q_BLOCK
    qb = jnp.moveaxis(q.reshape(b, nb, Q_BLOCK, DA_HEADS, 2, DA_DH), 1, 0)
    o = lax.map(lambda blk: _diff_attend(blk, k_all, v_all, lam), qb)
    o = jnp.moveaxis(o, 0, 1).reshape(b, n, DA_HEADS, DA_VD)
    y = _even_merge(f, o, subln_g, lam_init, w_out)
    if not need_ctx:
        return y, None
    oc = _diff_attend(qc, kc, vc, lam)
    yc = _even_merge(fc, oc, subln_g, lam_init, w_out)
    return y, yc


def centred_depthwise_conv(x, w):
    width = w.shape[0]
    return lax.conv_general_dilated(
        x, w[:, None, :].astype(x.dtype), window_strides=(1,),
        padding=[((width - 1) // 2, width // 2)],
        dimension_numbers=('NWC', 'WIO', 'NWC'),
        feature_group_count=x.shape[-1])


def chunk_gated_delta(q, k, v, g, beta, s0):
    b, t, nh, _ = q.shape
    n = t // GDN_CHUNK

    def chunks(a):
        a = a.reshape((b, n, GDN_CHUNK, nh) + a.shape[3:])
        return jnp.moveaxis(a, (1, 3), (0, 2))

    qc, kc, vc, gc, bc = chunks(q), chunks(k), chunks(v), chunks(g), chunks(beta)
    G = jnp.cumsum(gc, axis=-1)
    idx = jnp.arange(GDN_CHUNK)
    incl = idx[:, None] >= idx[None, :]
    strict = idx[:, None] > idx[None, :]
    diff = G[..., :, None] - G[..., None, :]
    decay = jnp.where(incl, jnp.exp(jnp.where(incl, diff, 0.0)), 0.0)
    kb = kc * bc[..., None]
    lmat = jnp.where(strict, jnp.einsum('...id,...jd->...ij', kb, kc) * decay, 0.0)
    eye = jnp.broadcast_to(jnp.eye(GDN_CHUNK, dtype=jnp.float32), lmat.shape)
    tmat = lax.linalg.triangular_solve(lmat + eye, eye, left_side=True, lower=True)
    u = tmat @ (vc * bc[..., None])
    w = tmat @ (kb * jnp.exp(G)[..., None])
    qg = qc * jnp.exp(G)[..., None]
    intra = jnp.where(incl, jnp.einsum('...id,...jd->...ij', qc, kc) * decay, 0.0)
    g_end = G[..., -1]
    k_tail = kc * jnp.exp(g_end[..., None] - G)[..., None]

    def step(state, xs):
        qg_i, u_i, w_i, intra_i, kt_i, ge_i = xs
        v_new = u_i - w_i @ state
        o_i = qg_i @ state + intra_i @ v_new
        state = state * jnp.exp(ge_i)[..., None, None] + jnp.swapaxes(kt_i, -1, -2) @ v_new
        return state, o_i

    s_fin, o = lax.scan(step, s0, (qg, u, w, intra, k_tail, g_end))
    o = jnp.moveaxis(o, (0, 2), (1, 3)).reshape(b, t, nh, -1)
    return o, s_fin


def _gdn_prepare(p, conv_w, a_log, dt_bias):
    b, n, _ = p.shape
    qkv = jax.nn.silu(centred_depthwise_conv(p[..., :GDN_QKV_W], conv_w)).astype(jnp.float32)
    q = qkv[..., :GDN_QK_W].reshape(b, n, GDN_HK, GDN_DK)
    k = qkv[..., GDN_QK_W:2 * GDN_QK_W].reshape(b, n, GDN_HK, GDN_DK)
    v = qkv[..., 2 * GDN_QK_W:].reshape(b, n, GDN_HV, GDN_DV)
    rep = GDN_HV // GDN_HK
    q = jnp.repeat(l2_normalize(q) * (GDN_DK ** -0.5), rep, axis=2)
    k = jnp.repeat(l2_normalize(k), rep, axis=2)
    z = p[..., GDN_QKV_W:GDN_QKV_W + GDN_V_W].reshape(b, n, GDN_HV, GDN_DV)
    ab = p[..., GDN_QKV_W + GDN_V_W:].astype(jnp.float32).reshape(b, n, 2, 2, GDN_HV)
    a, bb = ab[:, :, 0], ab[:, :, 1]
    g = -jnp.exp(a_log.astype(jnp.float32)) * jax.nn.softplus(a + dt_bias.astype(jnp.float32))
    beta = jax.nn.sigmoid(bb)
    return q, k, v, z, g, beta


def _gdn_readout(o, z, norm_g, w_out):
    b, n = o.shape[:2]
    y = rms_norm(o, norm_g) * jax.nn.silu(z.astype(jnp.float32))
    return y.astype(z.dtype).reshape(b, n, GDN_V_W) @ w_out


def gated_deltanet_mixer(h, hc, w_in, conv_w, a_log, dt_bias, norm_g, w_out, need_ctx):
    q, k, v, z, g, beta = _gdn_prepare(h @ w_in, conv_w, a_log, dt_bias)
    qc, kc, vc, zc, gc, bc = _gdn_prepare(hc @ w_in, conv_w, a_log, dt_bias)
    s0 = jnp.zeros((h.shape[0], GDN_HV, GDN_DK, GDN_DV), jnp.float32)
    rev = lambda a: jnp.flip(a, axis=1)
    oc_f, sc_f = chunk_gated_delta(qc, kc, vc, gc[:, :, 0], bc[:, :, 0], s0)
    o_f, _ = chunk_gated_delta(q, k, v, g[:, :, 0], beta[:, :, 0], sc_f)
    oc_b, sc_b = chunk_gated_delta(rev(qc), rev(kc), rev(vc), rev(gc[:, :, 1]), rev(bc[:, :, 1]), s0)
    o_b, _ = chunk_gated_delta(rev(q), rev(k), rev(v), rev(g[:, :, 1]), rev(beta[:, :, 1]), sc_b)
    y = _gdn_readout(o_f + rev(o_b), z, norm_g, w_out)
    if not need_ctx:
        return y, None
    yc = _gdn_readout(oc_f + rev(oc_b), zc, norm_g, w_out)
    return y, yc


def moe_ffn(h, w_router, b_router, w1, b1, w2, b2):
    n_tok, d = h.shape
    logits = (h @ w_router + b_router).astype(jnp.float32)
    top_val, top_idx = lax.top_k(logits, TOP_K)
    gates = jax.nn.softmax(top_val, axis=-1).astype(h.dtype)
    n_assign = n_tok * TOP_K
    e_flat = top_idx.reshape(n_assign).astype(jnp.int32)
    order = jnp.argsort(e_flat).astype(jnp.int32)
    e_sorted = e_flat[order]
    tok_sorted = order // TOP_K
    gate_sorted = gates.reshape(n_assign)[order]
    counts = jnp.zeros((N_EXPERTS,), jnp.int32).at[e_flat].add(1)
    starts = jnp.cumsum(counts) - counts
    padded = (counts + MOE_BLOCK - 1) // MOE_BLOCK * MOE_BLOCK
    pad_ends = jnp.cumsum(padded)
    pad_starts = pad_ends - padded
    dest = pad_starts[e_sorted] + jnp.arange(n_assign, dtype=jnp.int32) - starts[e_sorted]
    n_blocks = -(-(n_assign + N_EXPERTS * (MOE_BLOCK - 1)) // MOE_BLOCK)
    n_rows = n_blocks * MOE_BLOCK
    row_tok = jnp.full((n_rows,), n_tok, jnp.int32).at[dest].set(tok_sorted)
    row_gate = jnp.zeros((n_rows,), h.dtype).at[dest].set(gate_sorted)
    block_start = jnp.arange(n_blocks, dtype=jnp.int32) * MOE_BLOCK
    block_expert = jnp.minimum(jnp.searchsorted(pad_ends, block_start, side='right'), N_EXPERTS - 1)
    h_pad = jnp.concatenate([h, jnp.zeros((1, d), h.dtype)], axis=0)
    xb = h_pad[row_tok].reshape(n_blocks, MOE_BLOCK, d)

    def expert_block(args):
        xblk, e = args
        u = xblk @ w1[e] + b1[e]
        glu = jnp.minimum(u[:, 0::2], SWIGLU_LIMIT)
        lin = jnp.clip(u[:, 1::2], -SWIGLU_LIMIT, SWIGLU_LIMIT)
        act = glu * jax.nn.sigmoid(SWIGLU_ALPHA * glu) * (lin + 1)
        return act @ w2[e] + b2[e]

    yb = lax.map(expert_block, (xb, block_expert)).reshape(n_rows, d)
    out = jnp.zeros((n_tok + 1, d), yb.dtype).at[row_tok].add(yb * row_gate[:, None])
    return out[:n_tok]


def setup_inputs(seed: int = 0) -> dict:
    key = jax.random.key(seed)
    ks = iter(jax.random.split(key, 32))
    f32 = jnp.float32

    def normal(shape, scale=1.0):
        return jax.random.normal(next(ks), shape, f32) * scale

    def gain(shape):
        return 1.0 + normal(shape, 0.05)

    def log_uniform_dt(shape):
        dt = jnp.exp(jax.random.uniform(next(ks), shape, f32, math.log(1e-3), math.log(1e-1)))
        return dt + jnp.log(-jnp.expm1(-dt))

    return {
        'x': normal((BATCH, SEQ, D_MODEL)),
        'c': normal((BATCH, D_MODEL)),
        'ctx': normal((BATCH, CTX_LEN, D_MODEL)),
        'c_ctx': normal((D_MODEL,)),
        'norm1_g': gain((DEPTH, D_MODEL)),
        'norm2_g': gain((DEPTH, D_MODEL)),
        'w_mod': normal((DEPTH, D_MODEL, N_MOD * D_MODEL), 0.5 * D_MODEL ** -0.5),
        'b_mod': normal((DEPTH, N_MOD * D_MODEL), 0.02),
        'ev_w_in': normal((N_EVEN, D_MODEL, EVEN_IN), D_MODEL ** -0.5),
        'ev_w_out': normal((N_EVEN, EVEN_MIX, D_MODEL), EVEN_MIX ** -0.5),
        'ev_lam': normal((N_EVEN, 4, DA_DH), 0.1),
        'ev_subln_g': gain((N_EVEN, DA_VD)),
        'od_w_in': normal((N_ODD, D_MODEL, ODD_IN), D_MODEL ** -0.5),
        'od_conv_w': normal((N_ODD, GDN_CONV, GDN_QKV_W), GDN_CONV ** -0.5),
        'od_a_log': jnp.log(jax.random.uniform(next(ks), (N_ODD, 2, GDN_HV), f32, 1.0, 16.0)),
        'od_dt_bias': log_uniform_dt((N_ODD, 2, GDN_HV)),
        'od_norm_g': gain((N_ODD, GDN_DV)),
        'od_w_out': normal((N_ODD, GDN_V_W, D_MODEL), GDN_V_W ** -0.5),
        'moe_w_router': normal((DEPTH, D_MODEL, N_EXPERTS), D_MODEL ** -0.5),
        'moe_b_router': normal((DEPTH, N_EXPERTS), 0.01),
        'moe_w1': normal((DEPTH, N_EXPERTS, D_MODEL, 2 * D_FF), D_MODEL ** -0.5),
        'moe_b1': normal((DEPTH, N_EXPERTS, 2 * D_FF), 0.02),
        'moe_w2': normal((DEPTH, N_EXPERTS, D_FF, D_MODEL), D_FF ** -0.5),
        'moe_b2': normal((DEPTH, N_EXPERTS, D_MODEL), 0.02),
        'final_g': gain((D_MODEL,)),
    }


def reference(x, c, ctx, c_ctx, norm1_g, norm2_g, w_mod, b_mod,
              ev_w_in, ev_w_out, ev_lam, ev_subln_g,
              od_w_in, od_conv_w, od_a_log, od_dt_bias, od_norm_g, od_w_out,
              moe_w_router, moe_b_router, moe_w1, moe_b1, moe_w2, moe_b2, final_g):
    bsz, n_lat, d = x.shape
    n_ctx = ctx.shape[1]
    rows = n_lat // GRID_W
    cos, sin = axial_rope_tables(rows)
    s_lat = jax.nn.silu(c)
    s_ctx = jax.nn.silu(c_ctx)
    xc = ctx
    for l in range(DEPTH):
        last = l == DEPTH - 1
        mod = jnp.split((s_lat @ w_mod[l] + b_mod[l])[:, None, :], N_MOD, axis=-1)
        mod_c = jnp.split((s_ctx @ w_mod[l] + b_mod[l])[None, None, :], N_MOD, axis=-1)
        h = modulate(rms_norm(x, norm1_g[l]), mod[0], mod[1])
        hc = modulate(rms_norm(xc, norm1_g[l]), mod_c[0], mod_c[1])
        i = l // 2
        if l % 2 == 0:
            y, yc = fourier_diff_mixer(h, hc, ev_w_in[i], ev_w_out[i], ev_lam[i], ev_subln_g[i],
                                       diff_lambda_init(l), cos, sin, not last)
        else:
            y, yc = gated_deltanet_mixer(h, hc, od_w_in[i], od_conv_w[i], od_a_log[i], od_dt_bias[i],
                                         od_norm_g[i], od_w_out[i], not last)
        x = x + mod[2] * y
        h2 = modulate(rms_norm(x, norm2_g[l]), mod[3], mod[4]).reshape(bsz * n_lat, d)
        moe_args = (moe_w_router[l], moe_b_router[l], moe_w1[l], moe_b1[l], moe_w2[l], moe_b2[l])
        if last:
            x = x + mod[5] * moe_ffn(h2, *moe_args).reshape(bsz, n_lat, d)
        else:
            xc = xc + mod_c[2] * yc
            h2c = modulate(rms_norm(xc, norm2_g[l]), mod_c[3], mod_c[4]).reshape(bsz * n_ctx, d)
            out = moe_ffn(jnp.concatenate([h2, h2c], axis=0), *moe_args)
            x = x + mod[5] * out[:bsz * n_lat].reshape(bsz, n_lat, d)
            xc = xc + mod_c[5] * out[bsz * n_lat:].reshape(bsz, n_ctx, d)
    return rms_norm(x, final_g)
```

```python
import functools
import math

import jax
import jax.numpy as jnp
from jax import lax
from jax.experimental import pallas as pl
from jax.experimental.pallas import tpu as pltpu

D_MODEL = 1024
N_MOD = 6
NORM_EPS = 1e-6
GRID_W = 64

FOURIER_GROUPS = 4
FOURIER_GD = 64
FOURIER_W = FOURIER_GROUPS * FOURIER_GD
DA_HEADS = 6
DA_DH = 64
DA_VD = 2 * DA_DH
DA_W = DA_HEADS * DA_VD
Q_BLOCK = 128
ROPE_BASE = 10000.0
ROPE_AXIS_DIM = DA_DH // 2
SUBLN_EPS = 1e-5

GDN_HK = 8
GDN_HV = 16
GDN_DK = 128
GDN_DV = 128
GDN_QK_W = GDN_HK * GDN_DK
GDN_V_W = GDN_HV * GDN_DV
GDN_QKV_W = 2 * GDN_QK_W + GDN_V_W
GDN_CHUNK = 64

N_EXPERTS = 32
TOP_K = 4
D_FF = 1024
SWIGLU_LIMIT = 7.0
SWIGLU_ALPHA = 1.702
MOE_BLOCK = 512


def _rms_norm(x, g, eps=NORM_EPS):
    xf = x.astype(jnp.float32)
    y = xf * lax.rsqrt(jnp.mean(xf * xf, axis=-1, keepdims=True) + eps)
    return y.astype(x.dtype) * g


def _modulate(h, shift, scale):
    return h * (1 + scale) + shift


def _l2_normalize(x, eps=1e-6):
    return x * lax.rsqrt(jnp.sum(x * x, axis=-1, keepdims=True) + eps)


def _diff_lambda_init(layer_idx):
    return 0.8 - 0.6 * math.exp(-0.3 * layer_idx)


def _axial_rope_tables(rows):
    t = jnp.arange(rows * GRID_W, dtype=jnp.int32)
    row = (t // GRID_W).astype(jnp.float32)
    col = (t % GRID_W).astype(jnp.float32)
    inv = ROPE_BASE ** (-jnp.arange(0, ROPE_AXIS_DIM, 2, dtype=jnp.float32) / ROPE_AXIS_DIM)
    ang = jnp.concatenate([row[:, None] * inv, col[:, None] * inv], axis=-1)
    return jnp.cos(ang), jnp.sin(ang)


def _rope2d(x, cos, sin):
    half = x.shape[-1] // 2
    xf = x.astype(jnp.float32)
    x1, x2 = xf[..., :half], xf[..., half:]
    cb = cos[None, :, None, None, :]
    sb = sin[None, :, None, None, :]
    return jnp.concatenate([x1 * cb - x2 * sb, x2 * cb + x1 * sb], axis=-1).astype(x.dtype)


def _fourier_mix(f):
    ff = jnp.moveaxis(f.astype(jnp.float32), 2, 1)
    out = jnp.real(jnp.fft.fft2(ff, norm='ortho'))
    return jnp.moveaxis(out, 1, 2).astype(f.dtype)


def _diff_attend(q, k, v, lam):
    s = jnp.einsum('bqhmd,bkhmd->bhmqk', q, k, preferred_element_type=jnp.float32) * (DA_DH ** -0.5)
    p = jax.nn.softmax(s, axis=-1)
    a = (p[:, :, 0] - lam * p[:, :, 1]).astype(v.dtype)
    return jnp.einsum('bhqk,bkhe->bqhe', a, v)


def _even_split(p):
    b, n, _ = p.shape
    f = p[..., :FOURIER_W].reshape(b, n, FOURIER_GROUPS, FOURIER_GD)
    q, k, v = jnp.split(p[..., FOURIER_W:], 3, axis=-1)
    q = q.reshape(b, n, DA_HEADS, 2, DA_DH)
    k = k.reshape(b, n, DA_HEADS, 2, DA_DH)
    v = v.reshape(b, n, DA_HEADS, DA_VD)
    return f, q, k, v


def _even_merge(f, o, subln_g, lam_init, w_out):
    b, n = f.shape[:2]
    o = _rms_norm(o, subln_g, SUBLN_EPS) * (1.0 - lam_init)
    mixed = jnp.concatenate([_fourier_mix(f).reshape(b, n, FOURIER_W), o.reshape(b, n, DA_W)], axis=-1)
    return mixed @ w_out


def _fourier_diff_mixer(h, hc, w_in, w_out, lam_p, subln_g, lam_init, cos, sin, need_ctx):
    b, n, _ = h.shape
    f, q, k, v = _even_split(h @ w_in)
    fc, qc, kc, vc = _even_split(hc @ w_in)
    q = _rope2d(q, cos, sin)
    k = _rope2d(k, cos, sin)
    lp = lam_p.astype(jnp.float32)
    lam = jnp.exp(jnp.sum(lp[0] * lp[1])) - jnp.exp(jnp.sum(lp[2] * lp[3])) + lam_init
    k_all = jnp.concatenate([k, kc], axis=1)
    v_all = jnp.concatenate([v, vc], axis=1)
    nb = n // Q_BLOCK
    qb = jnp.moveaxis(q.reshape(b, nb, Q_BLOCK, DA_HEADS, 2, DA_DH), 1, 0)
    o = lax.map(lambda blk: _diff_attend(blk, k_all, v_all, lam), qb)
    o = jnp.moveaxis(o, 0, 1).reshape(b, n, DA_HEADS, DA_VD)
    y = _even_merge(f, o, subln_g, lam_init, w_out)
    if not need_ctx:
        return y, None
    oc = _diff_attend(qc, kc, vc, lam)
    yc = _even_merge(fc, oc, subln_g, lam_init, w_out)
    return y, yc


def _centred_depthwise_conv(x, w):
    width = w.shape[0]
    return lax.conv_general_dilated(
        x, w[:, None, :].astype(x.dtype), window_strides=(1,),
        padding=[((width - 1) // 2, width // 2)],
        dimension_numbers=('NWC', 'WIO', 'NWC'),
        feature_group_count=x.shape[-1])


def _chunk_gated_delta(q, k, v, g, beta, s0):
    b, t, nh, _ = q.shape
    n = t // GDN_CHUNK

    def chunks(a):
        a = a.reshape((b, n, GDN_CHUNK, nh) + a.shape[3:])
        return jnp.moveaxis(a, (1, 3), (0, 2))

    qc, kc, vc, gc, bc = chunks(q), chunks(k), chunks(v), chunks(g), chunks(beta)
    G = jnp.cumsum(gc, axis=-1)
    idx = jnp.arange(GDN_CHUNK)
    incl = idx[:, None] >= idx[None, :]
    strict = idx[:, None] > idx[None, :]
    diff = G[..., :, None] - G[..., None, :]
    decay = jnp.where(incl, jnp.exp(jnp.where(incl, diff, 0.0)), 0.0)
    kb = kc * bc[..., None]
    lmat = jnp.where(strict, jnp.einsum('...id,...jd->...ij', kb, kc) * decay, 0.0)
    eye = jnp.broadcast_to(jnp.eye(GDN_CHUNK, dtype=jnp.float32), lmat.shape)
    tmat = lax.linalg.triangular_solve(lmat + eye, eye, left_side=True, lower=True)
    u = tmat @ (vc * bc[..., None])
    w = tmat @ (kb * jnp.exp(G)[..., None])
    qg = qc * jnp.exp(G)[..., None]
    intra = jnp.where(incl, jnp.einsum('...id,...jd->...ij', qc, kc) * decay, 0.0)
    g_end = G[..., -1]
    k_tail = kc * jnp.exp(g_end[..., None] - G)[..., None]

    def step(state, xs):
        qg_i, u_i, w_i, intra_i, kt_i, ge_i = xs
        v_new = u_i - w_i @ state
        o_i = qg_i @ state + intra_i @ v_new
        state = state * jnp.exp(ge_i)[..., None, None] + jnp.swapaxes(kt_i, -1, -2) @ v_new
        return state, o_i

    s_fin, o = lax.scan(step, s0, (qg, u, w, intra, k_tail, g_end))
    o = jnp.moveaxis(o, (0, 2), (1, 3)).reshape(b, t, nh, -1)
    return o, s_fin


def _gdn_prepare(p, conv_w, a_log, dt_bias):
    b, n, _ = p.shape
    qkv = jax.nn.silu(_centred_depthwise_conv(p[..., :GDN_QKV_W], conv_w)).astype(jnp.float32)
    q = qkv[..., :GDN_QK_W].reshape(b, n, GDN_HK, GDN_DK)
    k = qkv[..., GDN_QK_W:2 * GDN_QK_W].reshape(b, n, GDN_HK, GDN_DK)
    v = qkv[..., 2 * GDN_QK_W:].reshape(b, n, GDN_HV, GDN_DV)
    rep = GDN_HV // GDN_HK
    q = jnp.repeat(_l2_normalize(q) * (GDN_DK ** -0.5), rep, axis=2)
    k = jnp.repeat(_l2_normalize(k), rep, axis=2)
    z = p[..., GDN_QKV_W:GDN_QKV_W + GDN_V_W].reshape(b, n, GDN_HV, GDN_DV)
    ab = p[..., GDN_QKV_W + GDN_V_W:].astype(jnp.float32).reshape(b, n, 2, 2, GDN_HV)
    a, bb = ab[:, :, 0], ab[:, :, 1]
    g = -jnp.exp(a_log.astype(jnp.float32)) * jax.nn.softplus(a + dt_bias.astype(jnp.float32))
    beta = jax.nn.sigmoid(bb)
    return q, k, v, z, g, beta


def _gdn_readout(o, z, norm_g, w_out):
    b, n = o.shape[:2]
    y = _rms_norm(o, norm_g) * jax.nn.silu(z.astype(jnp.float32))
    return y.astype(z.dtype).reshape(b, n, GDN_V_W) @ w_out


def _gated_deltanet_mixer(h, hc, w_in, conv_w, a_log, dt_bias, norm_g, w_out, need_ctx):
    q, k, v, z, g, beta = _gdn_prepare(h @ w_in, conv_w, a_log, dt_bias)
    qc, kc, vc, zc, gc, bc = _gdn_prepare(hc @ w_in, conv_w, a_log, dt_bias)
    s0 = jnp.zeros((h.shape[0], GDN_HV, GDN_DK, GDN_DV), jnp.float32)
    rev = lambda a: jnp.flip(a, axis=1)
    oc_f, sc_f = _chunk_gated_delta(qc, kc, vc, gc[:, :, 0], bc[:, :, 0], s0)
    o_f, _ = _chunk_gated_delta(q, k, v, g[:, :, 0], beta[:, :, 0], sc_f)
    oc_b, sc_b = _chunk_gated_delta(rev(qc), rev(kc), rev(vc), rev(gc[:, :, 1]), rev(bc[:, :, 1]), s0)
    o_b, _ = _chunk_gated_delta(rev(q), rev(k), rev(v), rev(g[:, :, 1]), rev(beta[:, :, 1]), sc_b)
    y = _gdn_readout(o_f + rev(o_b), z, norm_g, w_out)
    if not need_ctx:
        return y, None
    yc = _gdn_readout(oc_f + rev(oc_b), zc, norm_g, w_out)
    return y, yc


def _moe_ffn(h, w_router, b_router, w1, b1, w2, b2):
    n_tok, d = h.shape
    logits = (h @ w_router + b_router).astype(jnp.float32)
    top_val, top_idx = lax.top_k(logits, TOP_K)
    gates = jax.nn.softmax(top_val, axis=-1).astype(h.dtype)
    n_assign = n_tok * TOP_K
    e_flat = top_idx.reshape(n_assign).astype(jnp.int32)
    order = jnp.argsort(e_flat).astype(jnp.int32)
    e_sorted = e_flat[order]
    tok_sorted = order // TOP_K
    gate_sorted = gates.reshape(n_assign)[order]
    counts = jnp.zeros((N_EXPERTS,), jnp.int32).at[e_flat].add(1)
    starts = jnp.cumsum(counts) - counts
    padded = (counts + MOE_BLOCK - 1) // MOE_BLOCK * MOE_BLOCK
    pad_ends = jnp.cumsum(padded)
    pad_starts = pad_ends - padded
    dest = pad_starts[e_sorted] + jnp.arange(n_assign, dtype=jnp.int32) - starts[e_sorted]
    n_blocks = -(-(n_assign + N_EXPERTS * (MOE_BLOCK - 1)) // MOE_BLOCK)
    n_rows = n_blocks * MOE_BLOCK
    row_tok = jnp.full((n_rows,), n_tok, jnp.int32).at[dest].set(tok_sorted)
    row_gate = jnp.zeros((n_rows,), h.dtype).at[dest].set(gate_sorted)
    block_start = jnp.arange(n_blocks, dtype=jnp.int32) * MOE_BLOCK
    block_expert = jnp.minimum(jnp.searchsorted(pad_ends, block_start, side='right'), N_EXPERTS - 1)
    h_pad = jnp.concatenate([h, jnp.zeros((1, d), h.dtype)], axis=0)
    xb = h_pad[row_tok].reshape(n_blocks, MOE_BLOCK, d)

    def expert_block(args):
        xblk, e = args
        u = xblk @ w1[e] + b1[e]
        glu = jnp.minimum(u[:, 0::2], SWIGLU_LIMIT)
        lin = jnp.clip(u[:, 1::2], -SWIGLU_LIMIT, SWIGLU_LIMIT)
        act = glu * jax.nn.sigmoid(SWIGLU_ALPHA * glu) * (lin + 1)
        return act @ w2[e] + b2[e]

    yb = lax.map(expert_block, (xb, block_expert)).reshape(n_rows, d)
    out = jnp.zeros((n_tok + 1, d), yb.dtype).at[row_tok].add(yb * row_gate[:, None])
    return out[:n_tok]


def _final_norm_kernel(x_ref, g_ref, o_ref):
    x = x_ref[...]
    ms = jnp.mean(x * x, axis=-1, keepdims=True)
    o_ref[...] = x * lax.rsqrt(ms + NORM_EPS) * g_ref[...]


def _final_norm(x2d, g):
    t, d = x2d.shape
    tm = 1024
    return pl.pallas_call(
        _final_norm_kernel,
        grid=(t // tm,),
        in_specs=[pl.BlockSpec((tm, d), lambda i: (i, 0)),
                  pl.BlockSpec((1, d), lambda i: (0, 0))],
        out_specs=pl.BlockSpec((tm, d), lambda i: (i, 0)),
        out_shape=jax.ShapeDtypeStruct((t, d), x2d.dtype),
        name="final_norm",
    )(x2d, g.reshape(1, d))


def kernel(x, c, ctx, c_ctx, norm1_g, norm2_g, w_mod, b_mod, ev_w_in, ev_w_out, ev_lam, ev_subln_g,
           od_w_in, od_conv_w, od_a_log, od_dt_bias, od_norm_g, od_w_out,
           moe_w_router, moe_b_router, moe_w1, moe_b1, moe_w2, moe_b2, final_g):
    bsz, n_lat, d = x.shape
    n_ctx = ctx.shape[1]
    depth = w_mod.shape[0]
    rows = n_lat // GRID_W
    cos, sin = _axial_rope_tables(rows)
    s_lat = jax.nn.silu(c)
    s_ctx = jax.nn.silu(c_ctx)
    xc = ctx
    for l in range(depth):
        last = l == depth - 1
        mod = jnp.split((s_lat @ w_mod[l] + b_mod[l])[:, None, :], N_MOD, axis=-1)
        mod_c = jnp.split((s_ctx @ w_mod[l] + b_mod[l])[None, None, :], N_MOD, axis=-1)
        h = _modulate(_rms_norm(x, norm1_g[l]), mod[0], mod[1])
        hc = _modulate(_rms_norm(xc, norm1_g[l]), mod_c[0], mod_c[1])
        i = l // 2
        if l % 2 == 0:
            y, yc = _fourier_diff_mixer(h, hc, ev_w_in[i], ev_w_out[i], ev_lam[i], ev_subln_g[i],
                                        _diff_lambda_init(l), cos, sin, not last)
        else:
            y, yc = _gated_deltanet_mixer(h, hc, od_w_in[i], od_conv_w[i], od_a_log[i], od_dt_bias[i],
                                          od_norm_g[i], od_w_out[i], not last)
        x = x + mod[2] * y
        h2 = _modulate(_rms_norm(x, norm2_g[l]), mod[3], mod[4]).reshape(bsz * n_lat, d)
        moe_args = (moe_w_router[l], moe_b_router[l], moe_w1[l], moe_b1[l], moe_w2[l], moe_b2[l])
        if last:
            x = x + mod[5] * _moe_ffn(h2, *moe_args).reshape(bsz, n_lat, d)
        else:
            xc = xc + mod_c[2] * yc
            h2c = _modulate(_rms_norm(xc, norm2_g[l]), mod_c[3], mod_c[4]).reshape(bsz * n_ctx, d)
            out = _moe_ffn(jnp.concatenate([h2, h2c], axis=0), *moe_args)
            x = x + mod[5] * out[:bsz * n_lat].reshape(bsz, n_lat, d)
            xc = xc + mod_c[5] * out[bsz * n_lat:].reshape(bsz, n_ctx, d)
    return _final_norm(x.reshape(bsz * n_lat, d), final_g).reshape(bsz, n_lat, d)
```

```python
import functools
import math

import jax
import jax.numpy as jnp
from jax import lax
from jax.experimental import pallas as pl
from jax.experimental.pallas import tpu as pltpu

D_MODEL = 1024
N_MOD = 6
NORM_EPS = 1e-6
GRID_W = 64

FOURIER_GROUPS = 4
FOURIER_GD = 64
FOURIER_W = FOURIER_GROUPS * FOURIER_GD
DA_HEADS = 6
DA_DH = 64
DA_VD = 2 * DA_DH
DA_W = DA_HEADS * DA_VD
ROPE_BASE = 10000.0
ROPE_AXIS_DIM = DA_DH // 2
SUBLN_EPS = 1e-5

GDN_HK = 8
GDN_HV = 16
GDN_DK = 128
GDN_DV = 128
GDN_QK_W = GDN_HK * GDN_DK
GDN_V_W = GDN_HV * GDN_DV
GDN_QKV_W = 2 * GDN_QK_W + GDN_V_W
GDN_CHUNK = 64

N_EXPERTS = 32
TOP_K = 4
D_FF = 1024
SWIGLU_LIMIT = 7.0
SWIGLU_ALPHA = 1.702
MOE_BLOCK = 512

LANES = 128
VMEM_LIMIT = 56 * 1024 * 1024
BF16 = jnp.bfloat16
F32 = jnp.float32


def _rms_norm(x, g, eps=NORM_EPS):
    xf = x.astype(jnp.float32)
    y = xf * lax.rsqrt(jnp.mean(xf * xf, axis=-1, keepdims=True) + eps)
    return y.astype(x.dtype) * g


def _modulate(h, shift, scale):
    return h * (1 + scale) + shift


def _l2_normalize(x, eps=1e-6):
    return x * lax.rsqrt(jnp.sum(x * x, axis=-1, keepdims=True) + eps)


def _diff_lambda_init(layer_idx):
    return 0.8 - 0.6 * math.exp(-0.3 * layer_idx)


def _axial_rope_tables(rows):
    t = jnp.arange(rows * GRID_W, dtype=jnp.int32)
    row = (t // GRID_W).astype(jnp.float32)
    col = (t % GRID_W).astype(jnp.float32)
    inv = ROPE_BASE ** (-jnp.arange(0, ROPE_AXIS_DIM, 2, dtype=jnp.float32) / ROPE_AXIS_DIM)
    ang = jnp.concatenate([row[:, None] * inv, col[:, None] * inv], axis=-1)
    return jnp.cos(ang), jnp.sin(ang)


def _rope2d(x, cos, sin):
    half = x.shape[-1] // 2
    xf = x.astype(jnp.float32)
    x1, x2 = xf[..., :half], xf[..., half:]
    cb = cos[None, :, None, None, :]
    sb = sin[None, :, None, None, :]
    return jnp.concatenate([x1 * cb - x2 * sb, x2 * cb + x1 * sb], axis=-1).astype(x.dtype)


def _fourier_mix(f):
    ff = jnp.moveaxis(f.astype(jnp.float32), 2, 1)
    out = jnp.real(jnp.fft.fft2(ff, norm='ortho'))
    return jnp.moveaxis(out, 1, 2).astype(f.dtype)


def _diff_attend(q, k, v, lam):
    s = jnp.einsum('bqhmd,bkhmd->bhmqk', q, k, preferred_element_type=jnp.float32) * (DA_DH ** -0.5)
    p = jax.nn.softmax(s, axis=-1)
    a = (p[:, :, 0] - lam * p[:, :, 1]).astype(v.dtype)
    return jnp.einsum('bhqk,bkhe->bqhe', a, v)


def _even_split(p):
    b, n, _ = p.shape
    f = p[..., :FOURIER_W].reshape(b, n, FOURIER_GROUPS, FOURIER_GD)
    q, k, v = jnp.split(p[..., FOURIER_W:], 3, axis=-1)
    q = q.reshape(b, n, DA_HEADS, 2, DA_DH)
    k = k.reshape(b, n, DA_HEADS, 2, DA_DH)
    v = v.reshape(b, n, DA_HEADS, DA_VD)
    return f, q, k, v


def _centred_depthwise_conv(x, w):
    width = w.shape[0]
    return lax.conv_general_dilated(
        x, w[:, None, :].astype(x.dtype), window_strides=(1,),
        padding=[((width - 1) // 2, width // 2)],
        dimension_numbers=('NWC', 'WIO', 'NWC'),
        feature_group_count=x.shape[-1])


def _chunk_gated_delta(q, k, v, g, beta, s0):
    b, t, nh, _ = q.shape
    n = t // GDN_CHUNK

    def chunks(a):
        a = a.reshape((b, n, GDN_CHUNK, nh) + a.shape[3:])
        return jnp.moveaxis(a, (1, 3), (0, 2))

    qc, kc, vc, gc, bc = chunks(q), chunks(k), chunks(v), chunks(g), chunks(beta)
    G = jnp.cumsum(gc, axis=-1)
    idx = jnp.arange(GDN_CHUNK)
    incl = idx[:, None] >= idx[None, :]
    strict = idx[:, None] > idx[None, :]
    diff = G[..., :, None] - G[..., None, :]
    decay = jnp.where(incl, jnp.exp(jnp.where(incl, diff, 0.0)), 0.0)
    kb = kc * bc[..., None]
    lmat = jnp.where(strict, jnp.einsum('...id,...jd->...ij', kb, kc) * decay, 0.0)
    eye = jnp.broadcast_to(jnp.eye(GDN_CHUNK, dtype=jnp.float32), lmat.shape)
    tmat = lax.linalg.triangular_solve(lmat + eye, eye, left_side=True, lower=True)
    u = tmat @ (vc * bc[..., None])
    w = tmat @ (kb * jnp.exp(G)[..., None])
    qg = qc * jnp.exp(G)[..., None]
    intra = jnp.where(incl, jnp.einsum('...id,...jd->...ij', qc, kc) * decay, 0.0)
    g_end = G[..., -1]
    k_tail = kc * jnp.exp(g_end[..., None] - G)[..., None]

    def step(state, xs):
        qg_i, u_i, w_i, intra_i, kt_i, ge_i = xs
        v_new = u_i - w_i @ state
        o_i = qg_i @ state + intra_i @ v_new
        state = state * jnp.exp(ge_i)[..., None, None] + jnp.swapaxes(kt_i, -1, -2) @ v_new
        return state, o_i

    s_fin, o = lax.scan(step, s0, (qg, u, w, intra, k_tail, g_end))
    o = jnp.moveaxis(o, (0, 2), (1, 3)).reshape(b, t, nh, -1)
    return o, s_fin


def _gdn_prepare(p_qkv, p_z, p_ab, conv_w, a_log, dt_bias):
    b, n, _ = p_qkv.shape
    qkv = jax.nn.silu(_centred_depthwise_conv(p_qkv, conv_w)).astype(jnp.float32)
    q = qkv[..., :GDN_QK_W].reshape(b, n, GDN_HK, GDN_DK)
    k = qkv[..., GDN_QK_W:2 * GDN_QK_W].reshape(b, n, GDN_HK, GDN_DK)
    v = qkv[..., 2 * GDN_QK_W:].reshape(b, n, GDN_HV, GDN_DV)
    rep = GDN_HV // GDN_HK
    q = jnp.repeat(_l2_normalize(q) * (GDN_DK ** -0.5), rep, axis=2)
    k = jnp.repeat(_l2_normalize(k), rep, axis=2)
    z = p_z.reshape(b, n, GDN_HV, GDN_DV)
    ab = p_ab.astype(jnp.float32).reshape(b, n, 2, 2, GDN_HV)
    a, bb = ab[:, :, 0], ab[:, :, 1]
    g = -jnp.exp(a_log.astype(jnp.float32)) * jax.nn.softplus(a + dt_bias.astype(jnp.float32))
    beta = jax.nn.sigmoid(bb)
    return q, k, v, z, g, beta


def _norm_mod_matmul_kernel(x_ref, g_ref, shift_ref, scale_ref, *refs, n_w):
    x = x_ref[...]
    ms = jnp.mean(x * x, axis=-1, keepdims=True)
    h = x * lax.rsqrt(ms + NORM_EPS) * g_ref[...]
    h = (h * (1.0 + scale_ref[0]) + shift_ref[0]).astype(BF16)
    for w_ref, o_ref in zip(refs[:n_w], refs[n_w:]):
        o_ref[...] = jnp.dot(h, w_ref[...], preferred_element_type=F32).astype(o_ref.dtype)


def _norm_mod_matmul(x2d, g, shift, scale, ws, rows_per_batch, tm=512):
    t, d = x2d.shape
    bpb = rows_per_batch // tm
    in_specs = [pl.BlockSpec((tm, d), lambda i: (i, 0)),
                pl.BlockSpec((1, d), lambda i: (0, 0)),
                pl.BlockSpec((1, 1, d), lambda i: (i // bpb, 0, 0)),
                pl.BlockSpec((1, 1, d), lambda i: (i // bpb, 0, 0))]
    in_specs += [pl.BlockSpec(w.shape, lambda i: (0, 0)) for w in ws]
    out_specs = [pl.BlockSpec((tm, w.shape[1]), lambda i: (i, 0)) for w in ws]
    out_shape = [jax.ShapeDtypeStruct((t, w.shape[1]), F32) for w in ws]
    return pl.pallas_call(
        functools.partial(_norm_mod_matmul_kernel, n_w=len(ws)),
        grid=(t // tm,),
        in_specs=in_specs, out_specs=out_specs, out_shape=out_shape,
        compiler_params=pltpu.CompilerParams(vmem_limit_bytes=VMEM_LIMIT),
        name="norm_mod_matmul",
    )(x2d, g.reshape(1, d), shift, scale, *ws)


def _diff_attn_kernel(lam_ref, q_ref, k_ref, v_ref, g_ref, o_ref, m_ref, acc_ref, *, tk, out_scale):
    tq = q_ref.shape[1]
    n_kv = k_ref.shape[1] // tk
    q = q_ref[0]
    lane = lax.broadcasted_iota(jnp.int32, q.shape, 1)
    zero = jnp.zeros_like(q)
    qs = jnp.concatenate([jnp.where(lane < DA_DH, q, zero), jnp.where(lane >= DA_DH, q, zero)], axis=0)
    m_ref[...] = jnp.full(m_ref.shape, -1e30, F32)
    acc_ref[...] = jnp.zeros(acc_ref.shape, F32)
    ones = jnp.ones((tk, LANES), BF16)

    def body(i, carry):
        off = pl.multiple_of(i * tk, tk)
        k = k_ref[0, pl.ds(off, tk), :]
        v = v_ref[0, pl.ds(off, tk), :]
        s = lax.dot_general(qs, k, (((1,), (1,)), ((), ())), preferred_element_type=F32)
        m_prev = m_ref[...]
        m_new = jnp.maximum(m_prev, jnp.max(s, axis=1, keepdims=True))
        alpha = jnp.exp(m_prev - m_new)
        p = jnp.exp(s - jnp.tile(m_new, (1, tk // LANES)))
        v_ext = jnp.concatenate([v, ones], axis=1)
        pv = jnp.dot(p.astype(BF16), v_ext, preferred_element_type=F32)
        acc_ref[...] = acc_ref[...] * jnp.tile(alpha, (1, 2)) + pv
        m_ref[...] = m_new
        return carry

    lax.fori_loop(0, n_kv, body, 0)
    acc = acc_ref[...]
    o1 = acc[:tq, :LANES] / acc[:tq, LANES:]
    o2 = acc[tq:, :LANES] / acc[tq:, LANES:]
    o = o1 - lam_ref[0] * o2
    ms = jnp.mean(o * o, axis=-1, keepdims=True)
    o = o * lax.rsqrt(ms + SUBLN_EPS) * g_ref[...] * out_scale
    o_ref[0] = o.astype(o_ref.dtype)


def _diff_attention(lam, q, k_all, v_all, subln_g, out_scale, tq=512, tk=640):
    b, n, _ = q.shape
    nk = k_all.shape[1]
    assert n % tq == 0 and nk % tk == 0 and tk % LANES == 0
    grid_spec = pltpu.PrefetchScalarGridSpec(
        num_scalar_prefetch=1,
        grid=(b, DA_HEADS, n // tq),
        in_specs=[pl.BlockSpec((1, tq, DA_VD), lambda bi, hi, qi, lam_r: (bi, qi, hi)),
                  pl.BlockSpec((1, nk, DA_VD), lambda bi, hi, qi, lam_r: (bi, 0, hi)),
                  pl.BlockSpec((1, nk, DA_VD), lambda bi, hi, qi, lam_r: (bi, 0, hi)),
                  pl.BlockSpec((1, DA_VD), lambda bi, hi, qi, lam_r: (0, 0))],
        out_specs=pl.BlockSpec((1, tq, DA_VD), lambda bi, hi, qi, lam_r: (bi, qi, hi)),
        scratch_shapes=[pltpu.VMEM((2 * tq, LANES), F32), pltpu.VMEM((2 * tq, 2 * LANES), F32)],
    )
    return pl.pallas_call(
        functools.partial(_diff_attn_kernel, tk=tk, out_scale=out_scale),
        grid_spec=grid_spec,
        out_shape=jax.ShapeDtypeStruct((b, n, DA_W), BF16),
        compiler_params=pltpu.CompilerParams(vmem_limit_bytes=VMEM_LIMIT),
        name="diff_attention",
    )(lam.reshape(1), q, k_all, v_all, subln_g.reshape(1, DA_VD))


def _proj_residual_kernel(a_ref, w_ref, x_ref, gate_ref, g_ref, shift_ref, scale_ref, wr_ref, br_ref,
                          xo_ref, h_ref, lg_ref):
    y = jnp.dot(a_ref[...], w_ref[...], preferred_element_type=F32)
    xn = x_ref[...] + gate_ref[0] * y
    xo_ref[...] = xn
    ms = jnp.mean(xn * xn, axis=-1, keepdims=True)
    h = xn * lax.rsqrt(ms + NORM_EPS) * g_ref[...]
    h = (h * (1.0 + scale_ref[0]) + shift_ref[0]).astype(BF16)
    h_ref[...] = h
    lg_ref[...] = jnp.dot(h, wr_ref[...], preferred_element_type=F32) + br_ref[...]


def _proj_residual(a, w, x2d, gate, g, shift, scale, w_router, b_router, rows_per_batch, tm=512):
    t, d = x2d.shape
    kdim = a.shape[1]
    bpb = rows_per_batch // tm
    wr = jnp.zeros((d, LANES), BF16).at[:, :N_EXPERTS].set(w_router.astype(BF16))
    br = jnp.zeros((1, LANES), F32).at[0, :N_EXPERTS].set(b_router)
    vec = pl.BlockSpec((1, 1, d), lambda i: (i // bpb, 0, 0))
    row = lambda n: pl.BlockSpec((tm, n), lambda i: (i, 0))
    full = lambda s: pl.BlockSpec(s, lambda i: (0, 0))
    return pl.pallas_call(
        _proj_residual_kernel,
        grid=(t // tm,),
        in_specs=[row(kdim), full((kdim, d)), row(d), vec, full((1, d)), vec, vec,
                  full((d, LANES)), full((1, LANES))],
        out_specs=[row(d), row(d), row(LANES)],
        out_shape=[jax.ShapeDtypeStruct((t, d), F32), jax.ShapeDtypeStruct((t, d), BF16),
                   jax.ShapeDtypeStruct((t, LANES), F32)],
        compiler_params=pltpu.CompilerParams(vmem_limit_bytes=VMEM_LIMIT),
        name="proj_residual",
    )(a, w, x2d, gate, g.reshape(1, d), shift, scale, wr, br)


def _moe_kernel(be_ref, x_ref, w1g_ref, w1l_ref, b1g_ref, b1l_ref, w2_ref, b2_ref, o_ref):
    x = x_ref[...]
    ug = jnp.dot(x, w1g_ref[0], preferred_element_type=F32) + b1g_ref[0]
    ul = jnp.dot(x, w1l_ref[0], preferred_element_type=F32) + b1l_ref[0]
    glu = jnp.minimum(ug, SWIGLU_LIMIT)
    lin = jnp.clip(ul, -SWIGLU_LIMIT, SWIGLU_LIMIT)
    act = glu * jax.nn.sigmoid(SWIGLU_ALPHA * glu) * (lin + 1.0)
    o_ref[...] = jnp.dot(act.astype(BF16), w2_ref[0], preferred_element_type=F32) + b2_ref[0]


def _moe_experts(block_expert, xb, w1g, w1l, b1g, b1l, w2, b2):
    n_rows, d = xb.shape
    n_blocks = n_rows // MOE_BLOCK
    wspec = lambda s: pl.BlockSpec((1,) + s, lambda i, be: (be[i], 0, 0))
    grid_spec = pltpu.PrefetchScalarGridSpec(
        num_scalar_prefetch=1,
        grid=(n_blocks,),
        in_specs=[pl.BlockSpec((MOE_BLOCK, d), lambda i, be: (i, 0)),
                  wspec((d, D_FF)), wspec((d, D_FF)), wspec((1, D_FF)), wspec((1, D_FF)),
                  wspec((D_FF, d)), wspec((1, d))],
        out_specs=pl.BlockSpec((MOE_BLOCK, d), lambda i, be: (i, 0)),
    )
    return pl.pallas_call(
        _moe_kernel,
        grid_spec=grid_spec,
        out_shape=jax.ShapeDtypeStruct((n_rows, d), F32),
        compiler_params=pltpu.CompilerParams(vmem_limit_bytes=VMEM_LIMIT),
        name="moe_experts",
    )(block_expert, xb, w1g, w1l, b1g, b1l, w2, b2)


def _moe_ffn(h, logits, w1, b1, w2, b2):
    n_tok, d = h.shape
    top_val, top_idx = lax.top_k(logits, TOP_K)
    gates = jax.nn.softmax(top_val, axis=-1)
    n_assign = n_tok * TOP_K
    e_flat = top_idx.reshape(n_assign).astype(jnp.int32)
    order = jnp.argsort(e_flat).astype(jnp.int32)
    e_sorted = e_flat[order]
    tok_sorted = order // TOP_K
    counts = jnp.zeros((N_EXPERTS,), jnp.int32).at[e_flat].add(1)
    starts = jnp.cumsum(counts) - counts
    padded = (counts + MOE_BLOCK - 1) // MOE_BLOCK * MOE_BLOCK
    pad_ends = jnp.cumsum(padded)
    pad_starts = pad_ends - padded
    dest = pad_starts[e_sorted] + jnp.arange(n_assign, dtype=jnp.int32) - starts[e_sorted]
    n_blocks = -(-(n_assign + N_EXPERTS * (MOE_BLOCK - 1)) // MOE_BLOCK)
    n_rows = n_blocks * MOE_BLOCK
    row_tok = jnp.zeros((n_rows,), jnp.int32).at[dest].set(tok_sorted)
    block_start = jnp.arange(n_blocks, dtype=jnp.int32) * MOE_BLOCK
    block_expert = jnp.minimum(jnp.searchsorted(pad_ends, block_start, side='right'),
                               N_EXPERTS - 1).astype(jnp.int32)
    xb = h[row_tok]
    w1g = w1[:, :, 0::2].astype(BF16)
    w1l = w1[:, :, 1::2].astype(BF16)
    b1g = b1[:, None, 0::2]
    b1l = b1[:, None, 1::2]
    yb = _moe_experts(block_expert, xb, w1g, w1l, b1g, b1l, w2.astype(BF16), b2[:, None, :])
    pos = jnp.zeros((n_assign,), jnp.int32).at[order].set(dest).reshape(n_tok, TOP_K)
    return jnp.sum(yb[pos] * gates[..., None], axis=1)


def _final_norm_kernel(x_ref, g_ref, o_ref):
    x = x_ref[...]
    ms = jnp.mean(x * x, axis=-1, keepdims=True)
    o_ref[...] = x * lax.rsqrt(ms + NORM_EPS) * g_ref[...]


def _final_norm(x2d, g, tm=1024):
    t, d = x2d.shape
    return pl.pallas_call(
        _final_norm_kernel,
        grid=(t // tm,),
        in_specs=[pl.BlockSpec((tm, d), lambda i: (i, 0)),
                  pl.BlockSpec((1, d), lambda i: (0, 0))],
        out_specs=pl.BlockSpec((tm, d), lambda i: (i, 0)),
        out_shape=jax.ShapeDtypeStruct((t, d), x2d.dtype),
        name="final_norm",
    )(x2d, g.reshape(1, d))


def _even_layer(x2d, xc, mod, mod_c, norm1_g, norm2_g, w_in, w_out, lam_p, subln_g, lam_init, cos, sin,
                w_router, b_router, bsz, n_lat):
    d = x2d.shape[1]
    w_in_b = w_in.astype(BF16)
    w_out_b = w_out.astype(BF16)
    (p,) = _norm_mod_matmul(x2d, norm1_g, mod[0], mod[1], [w_in_b], n_lat)
    f, q, k, v = _even_split(p.reshape(bsz, n_lat, -1))
    hc = _modulate(_rms_norm(xc, norm1_g), mod_c[0], mod_c[1])
    fc, qc, kc, vc = _even_split(hc @ w_in)
    q = _rope2d(q, cos, sin) * (DA_DH ** -0.5)
    k = _rope2d(k, cos, sin)
    lp = lam_p.astype(jnp.float32)
    lam = jnp.exp(jnp.sum(lp[0] * lp[1])) - jnp.exp(jnp.sum(lp[2] * lp[3])) + lam_init
    n_ctx = xc.shape[1]
    k_all = jnp.concatenate([k, kc], axis=1).astype(BF16).reshape(bsz, n_lat + n_ctx, DA_W)
    v_all = jnp.concatenate([v, vc], axis=1).astype(BF16).reshape(bsz, n_lat + n_ctx, DA_W)
    o = _diff_attention(lam, q.astype(BF16).reshape(bsz, n_lat, DA_W), k_all, v_all, subln_g, 1.0 - lam_init)
    fm = _fourier_mix(f).reshape(bsz, n_lat, FOURIER_W).astype(BF16)
    mixed = jnp.concatenate([fm, o], axis=-1).reshape(bsz * n_lat, -1)
    x2d, h2, logits = _proj_residual(mixed, w_out_b, x2d, mod[2], norm2_g, mod[3], mod[4],
                                     w_router, b_router, n_lat)
    oc = _diff_attend(qc, kc, vc, lam)
    oc = _rms_norm(oc, subln_g, SUBLN_EPS) * (1.0 - lam_init)
    mixed_c = jnp.concatenate([_fourier_mix(fc).reshape(bsz, n_ctx, FOURIER_W), oc.reshape(bsz, n_ctx, DA_W)], axis=-1)
    xc = xc + mod_c[2] * (mixed_c @ w_out)
    return x2d, h2, logits, xc


def _odd_layer(x2d, xc, mod, mod_c, norm1_g, norm2_g, w_in, conv_w, a_log, dt_bias, norm_g, w_out,
               w_router, b_router, bsz, n_lat):
    d = x2d.shape[1]
    n_main = GDN_QKV_W + GDN_V_W
    w_main = w_in[:, :n_main].astype(BF16)
    w_ab = jnp.zeros((d, LANES), BF16).at[:, :4 * GDN_HV].set(w_in[:, n_main:].astype(BF16))
    p_main, p_ab = _norm_mod_matmul(x2d, norm1_g, mod[0], mod[1], [w_main, w_ab], n_lat)
    p_main = p_main.reshape(bsz, n_lat, n_main)
    p_ab = p_ab.reshape(bsz, n_lat, LANES)[..., :4 * GDN_HV]
    q, k, v, z, g, beta = _gdn_prepare(p_main[..., :GDN_QKV_W], p_main[..., GDN_QKV_W:], p_ab,
                                       conv_w, a_log, dt_bias)
    hc = _modulate(_rms_norm(xc, norm1_g), mod_c[0], mod_c[1])
    pc = hc @ w_in
    qc, kc, vc, zc, gc, bc = _gdn_prepare(pc[..., :GDN_QKV_W], pc[..., GDN_QKV_W:n_main], pc[..., n_main:],
                                          conv_w, a_log, dt_bias)
    s0 = jnp.zeros((bsz, GDN_HV, GDN_DK, GDN_DV), jnp.float32)
    rev = lambda a: jnp.flip(a, axis=1)
    _, sc_f = _chunk_gated_delta(qc, kc, vc, gc[:, :, 0], bc[:, :, 0], s0)
    o_f, _ = _chunk_gated_delta(q, k, v, g[:, :, 0], beta[:, :, 0], sc_f)
    _, sc_b = _chunk_gated_delta(rev(qc), rev(kc), rev(vc), rev(gc[:, :, 1]), rev(bc[:, :, 1]), s0)
    o_b, _ = _chunk_gated_delta(rev(q), rev(k), rev(v), rev(g[:, :, 1]), rev(beta[:, :, 1]), sc_b)
    y = _rms_norm(o_f + rev(o_b), norm_g) * jax.nn.silu(z.astype(jnp.float32))
    y = y.astype(BF16).reshape(bsz * n_lat, GDN_V_W)
    return _proj_residual(y, w_out.astype(BF16), x2d, mod[2], norm2_g, mod[3], mod[4],
                          w_router, b_router, n_lat)


def kernel(x, c, ctx, c_ctx, norm1_g, norm2_g, w_mod, b_mod, ev_w_in, ev_w_out, ev_lam, ev_subln_g,
           od_w_in, od_conv_w, od_a_log, od_dt_bias, od_norm_g, od_w_out,
           moe_w_router, moe_b_router, moe_w1, moe_b1, moe_w2, moe_b2, final_g):
    bsz, n_lat, d = x.shape
    n_ctx = ctx.shape[1]
    depth = w_mod.shape[0]
    assert depth == 2, "kernel is written for one even (attention) and one odd (DeltaNet) layer"
    cos, sin = _axial_rope_tables(n_lat // GRID_W)
    s_lat = jax.nn.silu(c)
    s_ctx = jax.nn.silu(c_ctx)
    x2d = x.reshape(bsz * n_lat, d)
    xc = ctx

    mod = jnp.split((s_lat @ w_mod[0] + b_mod[0])[:, None, :], N_MOD, axis=-1)
    mod_c = jnp.split((s_ctx @ w_mod[0] + b_mod[0])[None, None, :], N_MOD, axis=-1)
    x2d, h2, logits, xc = _even_layer(x2d, xc, mod, mod_c, norm1_g[0], norm2_g[0], ev_w_in[0], ev_w_out[0],
                                      ev_lam[0], ev_subln_g[0], _diff_lambda_init(0), cos, sin,
                                      moe_w_router[0], moe_b_router[0], bsz, n_lat)
    h2c = _modulate(_rms_norm(xc, norm2_g[0]), mod_c[3], mod_c[4]).reshape(bsz * n_ctx, d)
    logits_c = (h2c @ moe_w_router[0] + moe_b_router[0]).astype(jnp.float32)
    out = _moe_ffn(jnp.concatenate([h2, h2c.astype(BF16)], axis=0),
                   jnp.concatenate([logits[:, :N_EXPERTS], logits_c], axis=0),
                   moe_w1[0], moe_b1[0], moe_w2[0], moe_b2[0])
    gate5 = jnp.broadcast_to(mod[5], (bsz, n_lat, d)).reshape(bsz * n_lat, d)
    x2d = x2d + gate5 * out[:bsz * n_lat]
    xc = xc + mod_c[5] * out[bsz * n_lat:].reshape(bsz, n_ctx, d)

    mod = jnp.split((s_lat @ w_mod[1] + b_mod[1])[:, None, :], N_MOD, axis=-1)
    mod_c = jnp.split((s_ctx @ w_mod[1] + b_mod[1])[None, None, :], N_MOD, axis=-1)
    x2d, h2, logits = _odd_layer(x2d, xc, mod, mod_c, norm1_g[1], norm2_g[1], od_w_in[0], od_conv_w[0],
                                 od_a_log[0], od_dt_bias[0], od_norm_g[0], od_w_out[0],
                                 moe_w_router[1], moe_b_router[1], bsz, n_lat)
    out = _moe_ffn(h2, logits[:, :N_EXPERTS], moe_w1[1], moe_b1[1], moe_w2[1], moe_b2[1])
    gate5 = jnp.broadcast_to(mod[5], (bsz, n_lat, d)).reshape(bsz * n_lat, d)
    x2d = x2d + gate5 * out
    return _final_norm(x2d, final_g).reshape(bsz, n_lat, d)
```

```python
import functools
import math

import jax
import jax.numpy as jnp
from jax import lax
from jax.experimental import pallas as pl
from jax.experimental.pallas import tpu as pltpu

D_MODEL = 1024
N_MOD = 6
NORM_EPS = 1e-6
GRID_W = 64

FOURIER_GROUPS = 4
FOURIER_GD = 64
FOURIER_W = FOURIER_GROUPS * FOURIER_GD
DA_HEADS = 6
DA_DH = 64
DA_VD = 2 * DA_DH
DA_W = DA_HEADS * DA_VD
ROPE_BASE = 10000.0
ROPE_AXIS_DIM = DA_DH // 2
SUBLN_EPS = 1e-5

GDN_HK = 8
GDN_HV = 16
GDN_DK = 128
GDN_DV = 128
GDN_QK_W = GDN_HK * GDN_DK
GDN_V_W = GDN_HV * GDN_DV
GDN_QKV_W = 2 * GDN_QK_W + GDN_V_W
GDN_CHUNK = 64

N_EXPERTS = 32
TOP_K = 4
D_FF = 1024
SWIGLU_LIMIT = 7.0
SWIGLU_ALPHA = 1.702
MOE_BLOCK = 512

LANES = 128
VMEM_LIMIT = 56 * 1024 * 1024
BF16 = jnp.bfloat16
F32 = jnp.float32


def _rms_norm(x, g, eps=NORM_EPS):
    xf = x.astype(jnp.float32)
    y = xf * lax.rsqrt(jnp.mean(xf * xf, axis=-1, keepdims=True) + eps)
    return y.astype(x.dtype) * g


def _modulate(h, shift, scale):
    return h * (1 + scale) + shift


def _l2_normalize(x, eps=1e-6):
    return x * lax.rsqrt(jnp.sum(x * x, axis=-1, keepdims=True) + eps)


def _diff_lambda_init(layer_idx):
    return 0.8 - 0.6 * math.exp(-0.3 * layer_idx)


def _axial_rope_tables(rows):
    t = jnp.arange(rows * GRID_W, dtype=jnp.int32)
    row = (t // GRID_W).astype(jnp.float32)
    col = (t % GRID_W).astype(jnp.float32)
    inv = ROPE_BASE ** (-jnp.arange(0, ROPE_AXIS_DIM, 2, dtype=jnp.float32) / ROPE_AXIS_DIM)
    ang = jnp.concatenate([row[:, None] * inv, col[:, None] * inv], axis=-1)
    return jnp.cos(ang), jnp.sin(ang)


def _rope2d(x, cos, sin):
    half = x.shape[-1] // 2
    xf = x.astype(jnp.float32)
    x1, x2 = xf[..., :half], xf[..., half:]
    cb = cos[None, :, None, None, :]
    sb = sin[None, :, None, None, :]
    return jnp.concatenate([x1 * cb - x2 * sb, x2 * cb + x1 * sb], axis=-1).astype(x.dtype)


def _fourier_mix(f):
    ff = jnp.moveaxis(f.astype(jnp.float32), 2, 1)
    out = jnp.real(jnp.fft.fft2(ff, norm='ortho'))
    return jnp.moveaxis(out, 1, 2).astype(f.dtype)


def _diff_attend(q, k, v, lam):
    s = jnp.einsum('bqhmd,bkhmd->bhmqk', q, k, preferred_element_type=jnp.float32) * (DA_DH ** -0.5)
    p = jax.nn.softmax(s, axis=-1)
    a = (p[:, :, 0] - lam * p[:, :, 1]).astype(v.dtype)
    return jnp.einsum('bhqk,bkhe->bqhe', a, v)


def _even_split(p):
    b, n, _ = p.shape
    f = p[..., :FOURIER_W].reshape(b, n, FOURIER_GROUPS, FOURIER_GD)
    q, k, v = jnp.split(p[..., FOURIER_W:], 3, axis=-1)
    q = q.reshape(b, n, DA_HEADS, 2, DA_DH)
    k = k.reshape(b, n, DA_HEADS, 2, DA_DH)
    v = v.reshape(b, n, DA_HEADS, DA_VD)
    return f, q, k, v


def _centred_depthwise_conv(x, w):
    width = w.shape[0]
    return lax.conv_general_dilated(
        x, w[:, None, :].astype(x.dtype), window_strides=(1,),
        padding=[((width - 1) // 2, width // 2)],
        dimension_numbers=('NWC', 'WIO', 'NWC'),
        feature_group_count=x.shape[-1])


def _chunk_gated_delta(q, k, v, g, beta, s0):
    b, t, nh, _ = q.shape
    n = t // GDN_CHUNK

    def chunks(a):
        a = a.reshape((b, n, GDN_CHUNK, nh) + a.shape[3:])
        return jnp.moveaxis(a, (1, 3), (0, 2))

    qc, kc, vc, gc, bc = chunks(q), chunks(k), chunks(v), chunks(g), chunks(beta)
    G = jnp.cumsum(gc, axis=-1)
    idx = jnp.arange(GDN_CHUNK)
    incl = idx[:, None] >= idx[None, :]
    strict = idx[:, None] > idx[None, :]
    diff = G[..., :, None] - G[..., None, :]
    decay = jnp.where(incl, jnp.exp(jnp.where(incl, diff, 0.0)), 0.0)
    kb = kc * bc[..., None]
    lmat = jnp.where(strict, jnp.einsum('...id,...jd->...ij', kb, kc) * decay, 0.0)
    eye = jnp.broadcast_to(jnp.eye(GDN_CHUNK, dtype=jnp.float32), lmat.shape)
    tmat = lax.linalg.triangular_solve(lmat + eye, eye, left_side=True, lower=True)
    u = tmat @ (vc * bc[..., None])
    w = tmat @ (kb * jnp.exp(G)[..., None])
    qg = qc * jnp.exp(G)[..., None]
    intra = jnp.where(incl, jnp.einsum('...id,...jd->...ij', qc, kc) * decay, 0.0)
    g_end = G[..., -1]
    k_tail = kc * jnp.exp(g_end[..., None] - G)[..., None]

    def step(state, xs):
        qg_i, u_i, w_i, intra_i, kt_i, ge_i = xs
        v_new = u_i - w_i @ state
        o_i = qg_i @ state + intra_i @ v_new
        state = state * jnp.exp(ge_i)[..., None, None] + jnp.swapaxes(kt_i, -1, -2) @ v_new
        return state, o_i

    s_fin, o = lax.scan(step, s0, (qg, u, w, intra, k_tail, g_end))
    o = jnp.moveaxis(o, (0, 2), (1, 3)).reshape(b, t, nh, -1)
    return o, s_fin


def _gdn_prepare(p_qkv, p_z, p_ab, conv_w, a_log, dt_bias):
    b, n, _ = p_qkv.shape
    qkv = jax.nn.silu(_centred_depthwise_conv(p_qkv, conv_w)).astype(jnp.float32)
    q = qkv[..., :GDN_QK_W].reshape(b, n, GDN_HK, GDN_DK)
    k = qkv[..., GDN_QK_W:2 * GDN_QK_W].reshape(b, n, GDN_HK, GDN_DK)
    v = qkv[..., 2 * GDN_QK_W:].reshape(b, n, GDN_HV, GDN_DV)
    rep = GDN_HV // GDN_HK
    q = jnp.repeat(_l2_normalize(q) * (GDN_DK ** -0.5), rep, axis=2)
    k = jnp.repeat(_l2_normalize(k), rep, axis=2)
    z = p_z.reshape(b, n, GDN_HV, GDN_DV)
    ab = p_ab.astype(jnp.float32).reshape(b, n, 2, 2, GDN_HV)
    a, bb = ab[:, :, 0], ab[:, :, 1]
    g = -jnp.exp(a_log.astype(jnp.float32)) * jax.nn.softplus(a + dt_bias.astype(jnp.float32))
    beta = jax.nn.sigmoid(bb)
    return q, k, v, z, g, beta


def _norm_mod_matmul_kernel(x_ref, g_ref, shift_ref, scale_ref, *refs, n_w):
    x = x_ref[...]
    ms = jnp.mean(x * x, axis=-1, keepdims=True)
    h = x * lax.rsqrt(ms + NORM_EPS) * g_ref[...]
    h = (h * (1.0 + scale_ref[0]) + shift_ref[0]).astype(BF16)
    for w_ref, o_ref in zip(refs[:n_w], refs[n_w:]):
        o_ref[...] = jnp.dot(h, w_ref[...], preferred_element_type=F32).astype(o_ref.dtype)


def _norm_mod_matmul(x2d, g, shift, scale, ws, rows_per_batch, tm=512):
    t, d = x2d.shape
    bpb = rows_per_batch // tm
    in_specs = [pl.BlockSpec((tm, d), lambda i: (i, 0)),
                pl.BlockSpec((1, d), lambda i: (0, 0)),
                pl.BlockSpec((1, 1, d), lambda i: (i // bpb, 0, 0)),
                pl.BlockSpec((1, 1, d), lambda i: (i // bpb, 0, 0))]
    in_specs += [pl.BlockSpec(w.shape, lambda i: (0, 0)) for w in ws]
    out_specs = [pl.BlockSpec((tm, w.shape[1]), lambda i: (i, 0)) for w in ws]
    out_shape = [jax.ShapeDtypeStruct((t, w.shape[1]), F32) for w in ws]
    return pl.pallas_call(
        functools.partial(_norm_mod_matmul_kernel, n_w=len(ws)),
        grid=(t // tm,),
        in_specs=in_specs, out_specs=out_specs, out_shape=out_shape,
        compiler_params=pltpu.CompilerParams(vmem_limit_bytes=VMEM_LIMIT),
        name="norm_mod_matmul",
    )(x2d, g.reshape(1, d), shift, scale, *ws)


def _diff_attn_kernel(lam_ref, q_ref, k_ref, v_ref, g_ref, o_ref, qs_ref, s_ref, m_ref, acc_ref, *,
                      tk, n_sub, out_scale):
    tq = q_ref.shape[1]
    n_kv = k_ref.shape[1] // tk
    rb = 2 * tq // n_sub
    q = q_ref[0]
    lane = lax.broadcasted_iota(jnp.int32, q.shape, 1)
    zero = jnp.zeros_like(q)
    qs_ref[:tq] = jnp.where(lane < DA_DH, q, zero)
    qs_ref[tq:] = jnp.where(lane >= DA_DH, q, zero)
    m_ref[...] = jnp.full(m_ref.shape, -1e30, F32)
    acc_ref[...] = jnp.zeros(acc_ref.shape, F32)
    ones = jnp.ones((tk, LANES), BF16)

    def scores(i, slot):
        off = pl.multiple_of(i * tk, tk)
        k = k_ref[0, pl.ds(off, tk), :]
        s_ref[slot] = lax.dot_general(qs_ref[...], k, (((1,), (1,)), ((), ())), preferred_element_type=F32)

    def consume(i, slot):
        off = pl.multiple_of(i * tk, tk)
        v_ext = jnp.concatenate([v_ref[0, pl.ds(off, tk), :], ones], axis=1)
        for r in range(n_sub):
            rows = pl.ds(r * rb, rb)
            s = s_ref[slot, rows, :]
            m_prev = m_ref[rows, :]
            m_new = jnp.maximum(m_prev, jnp.max(s, axis=1, keepdims=True))
            alpha = jnp.exp2(m_prev - m_new)
            p = jnp.exp2(s - jnp.tile(m_new, (1, tk // LANES)))
            pv = jnp.dot(p.astype(BF16), v_ext, preferred_element_type=F32)
            acc_ref[rows, :] = acc_ref[rows, :] * jnp.tile(alpha, (1, 2)) + pv
            m_ref[rows, :] = m_new

    scores(0, 0)

    def body(j, carry):
        scores(2 * j + 1, 1)
        consume(2 * j, 0)
        scores(2 * j + 2, 0)
        consume(2 * j + 1, 1)
        return carry

    lax.fori_loop(0, (n_kv - 1) // 2, body, 0)
    consume(n_kv - 1, 0)
    acc = acc_ref[...]
    o1 = acc[:tq, :LANES] / acc[:tq, LANES:]
    o2 = acc[tq:, :LANES] / acc[tq:, LANES:]
    o = o1 - lam_ref[0] * o2
    ms = jnp.mean(o * o, axis=-1, keepdims=True)
    o = o * lax.rsqrt(ms + SUBLN_EPS) * g_ref[...] * out_scale
    o_ref[0] = o.astype(o_ref.dtype)


def _diff_attention(lam, q, k_all, v_all, subln_g, out_scale, tq=512, tk=1280, n_sub=2):
    b, n, _ = q.shape
    nk = k_all.shape[1]
    assert n % tq == 0 and nk % tk == 0 and tk % (2 * LANES) == 0 and (nk // tk) % 2 == 1
    grid_spec = pltpu.PrefetchScalarGridSpec(
        num_scalar_prefetch=1,
        grid=(b, DA_HEADS, n // tq),
        in_specs=[pl.BlockSpec((1, tq, DA_VD), lambda bi, hi, qi, lam_r: (bi, qi, hi)),
                  pl.BlockSpec((1, nk, DA_VD), lambda bi, hi, qi, lam_r: (bi, 0, hi)),
                  pl.BlockSpec((1, nk, DA_VD), lambda bi, hi, qi, lam_r: (bi, 0, hi)),
                  pl.BlockSpec((1, DA_VD), lambda bi, hi, qi, lam_r: (0, 0))],
        out_specs=pl.BlockSpec((1, tq, DA_VD), lambda bi, hi, qi, lam_r: (bi, qi, hi)),
        scratch_shapes=[pltpu.VMEM((2 * tq, LANES), BF16), pltpu.VMEM((2, 2 * tq, tk), F32),
                        pltpu.VMEM((2 * tq, LANES), F32), pltpu.VMEM((2 * tq, 2 * LANES), F32)],
    )
    return pl.pallas_call(
        functools.partial(_diff_attn_kernel, tk=tk, n_sub=n_sub, out_scale=out_scale),
        grid_spec=grid_spec,
        out_shape=jax.ShapeDtypeStruct((b, n, DA_W), BF16),
        compiler_params=pltpu.CompilerParams(vmem_limit_bytes=VMEM_LIMIT),
        name="diff_attention",
    )(lam.reshape(1), q, k_all, v_all, subln_g.reshape(1, DA_VD))


def _proj_residual_kernel(a_ref, w_ref, x_ref, gate_ref, g_ref, shift_ref, scale_ref, wr_ref, br_ref,
                          xo_ref, h_ref, lg_ref):
    y = jnp.dot(a_ref[...], w_ref[...], preferred_element_type=F32)
    xn = x_ref[...] + gate_ref[0] * y
    xo_ref[...] = xn
    ms = jnp.mean(xn * xn, axis=-1, keepdims=True)
    h = xn * lax.rsqrt(ms + NORM_EPS) * g_ref[...]
    h = (h * (1.0 + scale_ref[0]) + shift_ref[0]).astype(BF16)
    h_ref[...] = h
    lg_ref[...] = jnp.dot(h, wr_ref[...], preferred_element_type=F32) + br_ref[...]


def _proj_residual(a, w, x2d, gate, g, shift, scale, w_router, b_router, rows_per_batch, tm=512):
    t, d = x2d.shape
    kdim = a.shape[1]
    bpb = rows_per_batch // tm
    wr = jnp.zeros((d, LANES), BF16).at[:, :N_EXPERTS].set(w_router.astype(BF16))
    br = jnp.zeros((1, LANES), F32).at[0, :N_EXPERTS].set(b_router)
    vec = pl.BlockSpec((1, 1, d), lambda i: (i // bpb, 0, 0))
    row = lambda n: pl.BlockSpec((tm, n), lambda i: (i, 0))
    full = lambda s: pl.BlockSpec(s, lambda i: (0, 0))
    return pl.pallas_call(
        _proj_residual_kernel,
        grid=(t // tm,),
        in_specs=[row(kdim), full((kdim, d)), row(d), vec, full((1, d)), vec, vec,
                  full((d, LANES)), full((1, LANES))],
        out_specs=[row(d), row(d), row(LANES)],
        out_shape=[jax.ShapeDtypeStruct((t, d), F32), jax.ShapeDtypeStruct((t, d), BF16),
                   jax.ShapeDtypeStruct((t, LANES), F32)],
        compiler_params=pltpu.CompilerParams(vmem_limit_bytes=VMEM_LIMIT),
        name="proj_residual",
    )(a, w, x2d, gate, g.reshape(1, d), shift, scale, wr, br)


MXU_DIM = 256


def _deinterleave_kernel(w_ref, perm_ref, g_ref, l_ref):
    w = w_ref[0].astype(BF16)
    for c in range(w.shape[1] // MXU_DIM):
        blk = jnp.dot(w[:, c * MXU_DIM:(c + 1) * MXU_DIM], perm_ref[...], preferred_element_type=F32)
        g_ref[0, :, c * LANES:(c + 1) * LANES] = blk[:, :LANES].astype(BF16)
        l_ref[0, :, c * LANES:(c + 1) * LANES] = blk[:, LANES:].astype(BF16)


def _deinterleave_w1(w1, tm=512):
    e, d, f2 = w1.shape
    src = jnp.arange(MXU_DIM)[:, None]
    dst = jnp.arange(MXU_DIM)[None, :]
    perm = (src == jnp.where(dst < LANES, 2 * dst, 2 * (dst - LANES) + 1)).astype(BF16)
    out = jax.ShapeDtypeStruct((e, d, f2 // 2), BF16)
    return pl.pallas_call(
        _deinterleave_kernel,
        grid=(e, d // tm),
        in_specs=[pl.BlockSpec((1, tm, f2), lambda i, j: (i, j, 0)),
                  pl.BlockSpec((MXU_DIM, MXU_DIM), lambda i, j: (0, 0))],
        out_specs=[pl.BlockSpec((1, tm, f2 // 2), lambda i, j: (i, j, 0))] * 2,
        out_shape=[out, out],
        compiler_params=pltpu.CompilerParams(vmem_limit_bytes=VMEM_LIMIT),
        name="deinterleave_w1",
    )(w1, perm)


def _moe_kernel(be_ref, x_ref, w1g_ref, w1l_ref, b1g_ref, b1l_ref, w2_ref, b2_ref, o_ref):
    x = x_ref[...]
    ug = jnp.dot(x, w1g_ref[0], preferred_element_type=F32) + b1g_ref[0]
    ul = jnp.dot(x, w1l_ref[0], preferred_element_type=F32) + b1l_ref[0]
    glu = jnp.minimum(ug, SWIGLU_LIMIT)
    lin = jnp.clip(ul, -SWIGLU_LIMIT, SWIGLU_LIMIT)
    act = glu * jax.nn.sigmoid(SWIGLU_ALPHA * glu) * (lin + 1.0)
    o_ref[...] = jnp.dot(act.astype(BF16), w2_ref[0], preferred_element_type=F32) + b2_ref[0]


def _moe_experts(block_expert, xb, w1g, w1l, b1g, b1l, w2, b2):
    n_rows, d = xb.shape
    n_blocks = n_rows // MOE_BLOCK
    wspec = lambda s: pl.BlockSpec((1,) + s, lambda i, be: (be[i], 0, 0))
    grid_spec = pltpu.PrefetchScalarGridSpec(
        num_scalar_prefetch=1,
        grid=(n_blocks,),
        in_specs=[pl.BlockSpec((MOE_BLOCK, d), lambda i, be: (i, 0)),
                  wspec((d, D_FF)), wspec((d, D_FF)), wspec((1, D_FF)), wspec((1, D_FF)),
                  wspec((D_FF, d)), wspec((1, d))],
        out_specs=pl.BlockSpec((MOE_BLOCK, d), lambda i, be: (i, 0)),
    )
    return pl.pallas_call(
        _moe_kernel,
        grid_spec=grid_spec,
        out_shape=jax.ShapeDtypeStruct((n_rows, d), F32),
        compiler_params=pltpu.CompilerParams(vmem_limit_bytes=VMEM_LIMIT),
        name="moe_experts",
    )(block_expert, xb, w1g, w1l, b1g, b1l, w2, b2)


def _moe_ffn(h, logits, w1, b1, w2, b2):
    n_tok, d = h.shape
    top_val, top_idx = lax.top_k(logits, TOP_K)
    gates = jax.nn.softmax(top_val, axis=-1)
    n_assign = n_tok * TOP_K
    e_flat = top_idx.reshape(n_assign).astype(jnp.int32)
    order = jnp.argsort(e_flat).astype(jnp.int32)
    e_sorted = e_flat[order]
    tok_sorted = order // TOP_K
    counts = jnp.zeros((N_EXPERTS,), jnp.int32).at[e_flat].add(1)
    starts = jnp.cumsum(counts) - counts
    padded = (counts + MOE_BLOCK - 1) // MOE_BLOCK * MOE_BLOCK
    pad_ends = jnp.cumsum(padded)
    pad_starts = pad_ends - padded
    dest = pad_starts[e_sorted] + jnp.arange(n_assign, dtype=jnp.int32) - starts[e_sorted]
    n_blocks = -(-(n_assign + N_EXPERTS * (MOE_BLOCK - 1)) // MOE_BLOCK)
    n_rows = n_blocks * MOE_BLOCK
    row_tok = jnp.zeros((n_rows,), jnp.int32).at[dest].set(tok_sorted)
    block_start = jnp.arange(n_blocks, dtype=jnp.int32) * MOE_BLOCK
    block_expert = jnp.minimum(jnp.searchsorted(pad_ends, block_start, side='right'),
                               N_EXPERTS - 1).astype(jnp.int32)
    xb = h[row_tok]
    w1g, w1l = _deinterleave_w1(w1)
    b1g = b1[:, None, 0::2]
    b1l = b1[:, None, 1::2]
    yb = _moe_experts(block_expert, xb, w1g, w1l, b1g, b1l, w2.astype(BF16), b2[:, None, :])
    pos = jnp.zeros((n_assign,), jnp.int32).at[order].set(dest).reshape(n_tok, TOP_K)
    return jnp.sum(yb[pos] * gates[..., None], axis=1)


def _final_norm_kernel(x_ref, g_ref, o_ref):
    x = x_ref[...]
    ms = jnp.mean(x * x, axis=-1, keepdims=True)
    o_ref[...] = x * lax.rsqrt(ms + NORM_EPS) * g_ref[...]


def _final_norm(x2d, g, tm=1024):
    t, d = x2d.shape
    return pl.pallas_call(
        _final_norm_kernel,
        grid=(t // tm,),
        in_specs=[pl.BlockSpec((tm, d), lambda i: (i, 0)),
                  pl.BlockSpec((1, d), lambda i: (0, 0))],
        out_specs=pl.BlockSpec((tm, d), lambda i: (i, 0)),
        out_shape=jax.ShapeDtypeStruct((t, d), x2d.dtype),
        name="final_norm",
    )(x2d, g.reshape(1, d))


def _even_layer(x2d, xc, mod, mod_c, norm1_g, norm2_g, w_in, w_out, lam_p, subln_g, lam_init, cos, sin,
                w_router, b_router, bsz, n_lat):
    d = x2d.shape[1]
    w_in_b = w_in.astype(BF16)
    w_out_b = w_out.astype(BF16)
    (p,) = _norm_mod_matmul(x2d, norm1_g, mod[0], mod[1], [w_in_b], n_lat)
    f, q, k, v = _even_split(p.reshape(bsz, n_lat, -1))
    hc = _modulate(_rms_norm(xc, norm1_g), mod_c[0], mod_c[1])
    fc, qc, kc, vc = _even_split(hc @ w_in)
    q = _rope2d(q, cos, sin) * (DA_DH ** -0.5 * math.log2(math.e))
    k = _rope2d(k, cos, sin)
    lp = lam_p.astype(jnp.float32)
    lam = jnp.exp(jnp.sum(lp[0] * lp[1])) - jnp.exp(jnp.sum(lp[2] * lp[3])) + lam_init
    n_ctx = xc.shape[1]
    k_all = jnp.concatenate([k, kc], axis=1).astype(BF16).reshape(bsz, n_lat + n_ctx, DA_W)
    v_all = jnp.concatenate([v, vc], axis=1).astype(BF16).reshape(bsz, n_lat + n_ctx, DA_W)
    o = _diff_attention(lam, q.astype(BF16).reshape(bsz, n_lat, DA_W), k_all, v_all, subln_g, 1.0 - lam_init)
    fm = _fourier_mix(f).reshape(bsz, n_lat, FOURIER_W).astype(BF16)
    mixed = jnp.concatenate([fm, o], axis=-1).reshape(bsz * n_lat, -1)
    x2d, h2, logits = _proj_residual(mixed, w_out_b, x2d, mod[2], norm2_g, mod[3], mod[4],
                                     w_router, b_router, n_lat)
    oc = _diff_attend(qc, kc, vc, lam)
    oc = _rms_norm(oc, subln_g, SUBLN_EPS) * (1.0 - lam_init)
    mixed_c = jnp.concatenate([_fourier_mix(fc).reshape(bsz, n_ctx, FOURIER_W), oc.reshape(bsz, n_ctx, DA_W)], axis=-1)
    xc = xc + mod_c[2] * (mixed_c @ w_out)
    return x2d, h2, logits, xc


def _odd_layer(x2d, xc, mod, mod_c, norm1_g, norm2_g, w_in, conv_w, a_log, dt_bias, norm_g, w_out,
               w_router, b_router, bsz, n_lat):
    d = x2d.shape[1]
    n_main = GDN_QKV_W + GDN_V_W
    w_main = w_in[:, :n_main].astype(BF16)
    w_ab = jnp.zeros((d, LANES), BF16).at[:, :4 * GDN_HV].set(w_in[:, n_main:].astype(BF16))
    p_main, p_ab = _norm_mod_matmul(x2d, norm1_g, mod[0], mod[1], [w_main, w_ab], n_lat)
    p_main = p_main.reshape(bsz, n_lat, n_main)
    p_ab = p_ab.reshape(bsz, n_lat, LANES)[..., :4 * GDN_HV]
    q, k, v, z, g, beta = _gdn_prepare(p_main[..., :GDN_QKV_W], p_main[..., GDN_QKV_W:], p_ab,
                                       conv_w, a_log, dt_bias)
    hc = _modulate(_rms_norm(xc, norm1_g), mod_c[0], mod_c[1])
    pc = hc @ w_in
    qc, kc, vc, zc, gc, bc = _gdn_prepare(pc[..., :GDN_QKV_W], pc[..., GDN_QKV_W:n_main], pc[..., n_main:],
                                          conv_w, a_log, dt_bias)
    s0 = jnp.zeros((bsz, GDN_HV, GDN_DK, GDN_DV), jnp.float32)
    rev = lambda a: jnp.flip(a, axis=1)
    _, sc_f = _chunk_gated_delta(qc, kc, vc, gc[:, :, 0], bc[:, :, 0], s0)
    o_f, _ = _chunk_gated_delta(q, k, v, g[:, :, 0], beta[:, :, 0], sc_f)
    _, sc_b = _chunk_gated_delta(rev(qc), rev(kc), rev(vc), rev(gc[:, :, 1]), rev(bc[:, :, 1]), s0)
    o_b, _ = _chunk_gated_delta(rev(q), rev(k), rev(v), rev(g[:, :, 1]), rev(beta[:, :, 1]), sc_b)
    y = _rms_norm(o_f + rev(o_b), norm_g) * jax.nn.silu(z.astype(jnp.float32))
    y = y.astype(BF16).reshape(bsz * n_lat, GDN_V_W)
    return _proj_residual(y, w_out.astype(BF16), x2d, mod[2], norm2_g, mod[3], mod[4],
                          w_router, b_router, n_lat)


def kernel(x, c, ctx, c_ctx, norm1_g, norm2_g, w_mod, b_mod, ev_w_in, ev_w_out, ev_lam, ev_subln_g,
           od_w_in, od_conv_w, od_a_log, od_dt_bias, od_norm_g, od_w_out,
           moe_w_router, moe_b_router, moe_w1, moe_b1, moe_w2, moe_b2, final_g):
    bsz, n_lat, d = x.shape
    n_ctx = ctx.shape[1]
    depth = w_mod.shape[0]
    assert depth == 2, "kernel is written for one even (attention) and one odd (DeltaNet) layer"
    cos, sin = _axial_rope_tables(n_lat // GRID_W)
    s_lat = jax.nn.silu(c)
    s_ctx = jax.nn.silu(c_ctx)
    x2d = x.reshape(bsz * n_lat, d)
    xc = ctx

    mod = jnp.split((s_lat @ w_mod[0] + b_mod[0])[:, None, :], N_MOD, axis=-1)
    mod_c = jnp.split((s_ctx @ w_mod[0] + b_mod[0])[None, None, :], N_MOD, axis=-1)
    x2d, h2, logits, xc = _even_layer(x2d, xc, mod, mod_c, norm1_g[0], norm2_g[0], ev_w_in[0], ev_w_out[0],
                                      ev_lam[0], ev_subln_g[0], _diff_lambda_init(0), cos, sin,
                                      moe_w_router[0], moe_b_router[0], bsz, n_lat)
    h2c = _modulate(_rms_norm(xc, norm2_g[0]), mod_c[3], mod_c[4]).reshape(bsz * n_ctx, d)
    logits_c = (h2c @ moe_w_router[0] + moe_b_router[0]).astype(jnp.float32)
    out = _moe_ffn(jnp.concatenate([h2, h2c.astype(BF16)], axis=0),
                   jnp.concatenate([logits[:, :N_EXPERTS], logits_c], axis=0),
                   moe_w1[0], moe_b1[0], moe_w2[0], moe_b2[0])
    gate5 = jnp.broadcast_to(mod[5], (bsz, n_lat, d)).reshape(bsz * n_lat, d)
    x2d = x2d + gate5 * out[:bsz * n_lat]
    xc = xc + mod_c[5] * out[bsz * n_lat:].reshape(bsz, n_ctx, d)

    mod = jnp.split((s_lat @ w_mod[1] + b_mod[1])[:, None, :], N_MOD, axis=-1)
    mod_c = jnp.split((s_ctx @ w_mod[1] + b_mod[1])[None, None, :], N_MOD, axis=-1)
    x2d, h2, logits = _odd_layer(x2d, xc, mod, mod_c, norm1_g[1], norm2_g[1], od_w_in[0], od_conv_w[0],
                                 od_a_log[0], od_dt_bias[0], od_norm_g[0], od_w_out[0],
                                 moe_w_router[1], moe_b_router[1], bsz, n_lat)
    out = _moe_ffn(h2, logits[:, :N_EXPERTS], moe_w1[1], moe_b1[1], moe_w2[1], moe_b2[1])
    gate5 = jnp.broadcast_to(mod[5], (bsz, n_lat, d)).reshape(bsz * n_lat, d)
    x2d = x2d + gate5 * out
    return _final_norm(x2d, final_g).reshape(bsz, n_lat, d)
```

```python
import functools
import math

import jax
import jax.numpy as jnp
from jax import lax
from jax.experimental import pallas as pl
from jax.experimental.pallas import tpu as pltpu

D_MODEL = 1024
N_MOD = 6
NORM_EPS = 1e-6
GRID_W = 64

FOURIER_GROUPS = 4
FOURIER_GD = 64
FOURIER_W = FOURIER_GROUPS * FOURIER_GD
DA_HEADS = 6
DA_DH = 64
DA_VD = 2 * DA_DH
DA_W = DA_HEADS * DA_VD
ROPE_BASE = 10000.0
ROPE_AXIS_DIM = DA_DH // 2
SUBLN_EPS = 1e-5

GDN_HK = 8
GDN_HV = 16
GDN_DK = 128
GDN_DV = 128
GDN_QK_W = GDN_HK * GDN_DK
GDN_V_W = GDN_HV * GDN_DV
GDN_QKV_W = 2 * GDN_QK_W + GDN_V_W
GDN_CHUNK = 64

N_EXPERTS = 32
TOP_K = 4
D_FF = 1024
SWIGLU_LIMIT = 7.0
SWIGLU_ALPHA = 1.702
MOE_BLOCK = 512

LANES = 128
VMEM_LIMIT = 56 * 1024 * 1024
BF16 = jnp.bfloat16
F32 = jnp.float32


def _rms_norm(x, g, eps=NORM_EPS):
    xf = x.astype(jnp.float32)
    y = xf * lax.rsqrt(jnp.mean(xf * xf, axis=-1, keepdims=True) + eps)
    return y.astype(x.dtype) * g


def _modulate(h, shift, scale):
    return h * (1 + scale) + shift


def _l2_normalize(x, eps=1e-6):
    return x * lax.rsqrt(jnp.sum(x * x, axis=-1, keepdims=True) + eps)


def _diff_lambda_init(layer_idx):
    return 0.8 - 0.6 * math.exp(-0.3 * layer_idx)


def _axial_rope_tables(rows):
    t = jnp.arange(rows * GRID_W, dtype=jnp.int32)
    row = (t // GRID_W).astype(jnp.float32)
    col = (t % GRID_W).astype(jnp.float32)
    inv = ROPE_BASE ** (-jnp.arange(0, ROPE_AXIS_DIM, 2, dtype=jnp.float32) / ROPE_AXIS_DIM)
    ang = jnp.concatenate([row[:, None] * inv, col[:, None] * inv], axis=-1)
    return jnp.cos(ang), jnp.sin(ang)


def _rope2d(x, cos, sin):
    half = x.shape[-1] // 2
    xf = x.astype(jnp.float32)
    x1, x2 = xf[..., :half], xf[..., half:]
    cb = cos[None, :, None, None, :]
    sb = sin[None, :, None, None, :]
    return jnp.concatenate([x1 * cb - x2 * sb, x2 * cb + x1 * sb], axis=-1).astype(x.dtype)


def _fourier_mix(f):
    ff = jnp.moveaxis(f.astype(jnp.float32), 2, 1)
    out = jnp.real(jnp.fft.fft2(ff, norm='ortho'))
    return jnp.moveaxis(out, 1, 2).astype(f.dtype)


def _diff_attend(q, k, v, lam):
    s = jnp.einsum('bqhmd,bkhmd->bhmqk', q, k, preferred_element_type=jnp.float32) * (DA_DH ** -0.5)
    p = jax.nn.softmax(s, axis=-1)
    a = (p[:, :, 0] - lam * p[:, :, 1]).astype(v.dtype)
    return jnp.einsum('bhqk,bkhe->bqhe', a, v)


def _even_split(p):
    b, n, _ = p.shape
    f = p[..., :FOURIER_W].reshape(b, n, FOURIER_GROUPS, FOURIER_GD)
    q, k, v = jnp.split(p[..., FOURIER_W:], 3, axis=-1)
    q = q.reshape(b, n, DA_HEADS, 2, DA_DH)
    k = k.reshape(b, n, DA_HEADS, 2, DA_DH)
    v = v.reshape(b, n, DA_HEADS, DA_VD)
    return f, q, k, v


def _centred_depthwise_conv(x, w):
    width = w.shape[0]
    return lax.conv_general_dilated(
        x, w[:, None, :].astype(x.dtype), window_strides=(1,),
        padding=[((width - 1) // 2, width // 2)],
        dimension_numbers=('NWC', 'WIO', 'NWC'),
        feature_group_count=x.shape[-1])


def _chunk_gated_delta(q, k, v, g, beta, s0):
    b, t, nh, _ = q.shape
    n = t // GDN_CHUNK

    def chunks(a):
        a = a.reshape((b, n, GDN_CHUNK, nh) + a.shape[3:])
        return jnp.moveaxis(a, (1, 3), (0, 2))

    qc, kc, vc, gc, bc = chunks(q), chunks(k), chunks(v), chunks(g), chunks(beta)
    G = jnp.cumsum(gc, axis=-1)
    idx = jnp.arange(GDN_CHUNK)
    incl = idx[:, None] >= idx[None, :]
    strict = idx[:, None] > idx[None, :]
    diff = G[..., :, None] - G[..., None, :]
    decay = jnp.where(incl, jnp.exp(jnp.where(incl, diff, 0.0)), 0.0)
    kb = kc * bc[..., None]
    lmat = jnp.where(strict, jnp.einsum('...id,...jd->...ij', kb, kc) * decay, 0.0)
    eye = jnp.broadcast_to(jnp.eye(GDN_CHUNK, dtype=jnp.float32), lmat.shape)
    tmat = lax.linalg.triangular_solve(lmat + eye, eye, left_side=True, lower=True)
    u = tmat @ (vc * bc[..., None])
    w = tmat @ (kb * jnp.exp(G)[..., None])
    qg = qc * jnp.exp(G)[..., None]
    intra = jnp.where(incl, jnp.einsum('...id,...jd->...ij', qc, kc) * decay, 0.0)
    g_end = G[..., -1]
    k_tail = kc * jnp.exp(g_end[..., None] - G)[..., None]

    def step(state, xs):
        qg_i, u_i, w_i, intra_i, kt_i, ge_i = xs
        v_new = u_i - w_i @ state
        o_i = qg_i @ state + intra_i @ v_new
        state = state * jnp.exp(ge_i)[..., None, None] + jnp.swapaxes(kt_i, -1, -2) @ v_new
        return state, o_i

    s_fin, o = lax.scan(step, s0, (qg, u, w, intra, k_tail, g_end))
    o = jnp.moveaxis(o, (0, 2), (1, 3)).reshape(b, t, nh, -1)
    return o, s_fin


def _gdn_prepare(p_qkv, p_z, p_ab, conv_w, a_log, dt_bias):
    b, n, _ = p_qkv.shape
    qkv = jax.nn.silu(_centred_depthwise_conv(p_qkv, conv_w)).astype(jnp.float32)
    q = qkv[..., :GDN_QK_W].reshape(b, n, GDN_HK, GDN_DK)
    k = qkv[..., GDN_QK_W:2 * GDN_QK_W].reshape(b, n, GDN_HK, GDN_DK)
    v = qkv[..., 2 * GDN_QK_W:].reshape(b, n, GDN_HV, GDN_DV)
    rep = GDN_HV // GDN_HK
    q = jnp.repeat(_l2_normalize(q) * (GDN_DK ** -0.5), rep, axis=2)
    k = jnp.repeat(_l2_normalize(k), rep, axis=2)
    z = p_z.reshape(b, n, GDN_HV, GDN_DV)
    ab = p_ab.astype(jnp.float32).reshape(b, n, 2, 2, GDN_HV)
    a, bb = ab[:, :, 0], ab[:, :, 1]
    g = -jnp.exp(a_log.astype(jnp.float32)) * jax.nn.softplus(a + dt_bias.astype(jnp.float32))
    beta = jax.nn.sigmoid(bb)
    return q, k, v, z, g, beta


def _norm_mod_matmul_kernel(x_ref, g_ref, shift_ref, scale_ref, *refs, n_w):
    x = x_ref[...]
    ms = jnp.mean(x * x, axis=-1, keepdims=True)
    h = x * lax.rsqrt(ms + NORM_EPS) * g_ref[...]
    h = (h * (1.0 + scale_ref[0]) + shift_ref[0]).astype(BF16)
    for w_ref, o_ref in zip(refs[:n_w], refs[n_w:]):
        o_ref[...] = jnp.dot(h, w_ref[...], preferred_element_type=F32).astype(o_ref.dtype)


def _norm_mod_matmul(x2d, g, shift, scale, ws, rows_per_batch, tm=512):
    t, d = x2d.shape
    bpb = rows_per_batch // tm
    in_specs = [pl.BlockSpec((tm, d), lambda i: (i, 0)),
                pl.BlockSpec((1, d), lambda i: (0, 0)),
                pl.BlockSpec((1, 1, d), lambda i: (i // bpb, 0, 0)),
                pl.BlockSpec((1, 1, d), lambda i: (i // bpb, 0, 0))]
    in_specs += [pl.BlockSpec(w.shape, lambda i: (0, 0)) for w in ws]
    out_specs = [pl.BlockSpec((tm, w.shape[1]), lambda i: (i, 0)) for w in ws]
    out_shape = [jax.ShapeDtypeStruct((t, w.shape[1]), F32) for w in ws]
    return pl.pallas_call(
        functools.partial(_norm_mod_matmul_kernel, n_w=len(ws)),
        grid=(t // tm,),
        in_specs=in_specs, out_specs=out_specs, out_shape=out_shape,
        compiler_params=pltpu.CompilerParams(vmem_limit_bytes=VMEM_LIMIT),
        name="norm_mod_matmul",
    )(x2d, g.reshape(1, d), shift, scale, *ws)


def _diff_attn_kernel(lam_ref, q_ref, k_ref, v_ref, g_ref, o_ref, qs_ref, s_ref, m_ref, acc_ref, *,
                      tk, n_sub, out_scale):
    tq = q_ref.shape[1]
    n_kv = k_ref.shape[1] // tk
    rb = 2 * tq // n_sub
    q = q_ref[0]
    lane = lax.broadcasted_iota(jnp.int32, q.shape, 1)
    zero = jnp.zeros_like(q)
    qs_ref[:tq] = jnp.where(lane < DA_DH, q, zero)
    qs_ref[tq:] = jnp.where(lane >= DA_DH, q, zero)
    m_ref[...] = jnp.full(m_ref.shape, -1e30, F32)
    acc_ref[...] = jnp.zeros(acc_ref.shape, F32)
    ones = jnp.ones((tk, LANES), BF16)

    def scores(i, slot):
        off = pl.multiple_of(i * tk, tk)
        k = k_ref[0, pl.ds(off, tk), :]
        s_ref[slot] = lax.dot_general(qs_ref[...], k, (((1,), (1,)), ((), ())), preferred_element_type=F32)

    def consume(i, slot):
        off = pl.multiple_of(i * tk, tk)
        v_ext = jnp.concatenate([v_ref[0, pl.ds(off, tk), :], ones], axis=1)
        for r in range(n_sub):
            rows = pl.ds(r * rb, rb)
            s = s_ref[slot, rows, :]
            m_prev = m_ref[rows, :]
            m_new = jnp.maximum(m_prev, jnp.max(s, axis=1, keepdims=True))
            alpha = jnp.exp2(m_prev - m_new)
            p = jnp.exp2(s - jnp.tile(m_new, (1, tk // LANES)))
            pv = jnp.dot(p.astype(BF16), v_ext, preferred_element_type=F32)
            acc_ref[rows, :] = acc_ref[rows, :] * jnp.tile(alpha, (1, 2)) + pv
            m_ref[rows, :] = m_new

    scores(0, 0)

    def body(j, carry):
        scores(2 * j + 1, 1)
        consume(2 * j, 0)
        scores(2 * j + 2, 0)
        consume(2 * j + 1, 1)
        return carry

    lax.fori_loop(0, (n_kv - 1) // 2, body, 0)
    consume(n_kv - 1, 0)
    acc = acc_ref[...]
    o1 = acc[:tq, :LANES] / acc[:tq, LANES:]
    o2 = acc[tq:, :LANES] / acc[tq:, LANES:]
    o = o1 - lam_ref[0] * o2
    ms = jnp.mean(o * o, axis=-1, keepdims=True)
    o = o * lax.rsqrt(ms + SUBLN_EPS) * g_ref[...] * out_scale
    o_ref[0] = o.astype(o_ref.dtype)


def _diff_attention(lam, q, k_all, v_all, subln_g, out_scale, tq=512, tk=1280, n_sub=2):
    b, n, _ = q.shape
    nk = k_all.shape[1]
    assert n % tq == 0 and nk % tk == 0 and tk % (2 * LANES) == 0 and (nk // tk) % 2 == 1
    grid_spec = pltpu.PrefetchScalarGridSpec(
        num_scalar_prefetch=1,
        grid=(b, DA_HEADS, n // tq),
        in_specs=[pl.BlockSpec((1, tq, DA_VD), lambda bi, hi, qi, lam_r: (bi, qi, hi)),
                  pl.BlockSpec((1, nk, DA_VD), lambda bi, hi, qi, lam_r: (bi, 0, hi)),
                  pl.BlockSpec((1, nk, DA_VD), lambda bi, hi, qi, lam_r: (bi, 0, hi)),
                  pl.BlockSpec((1, DA_VD), lambda bi, hi, qi, lam_r: (0, 0))],
        out_specs=pl.BlockSpec((1, tq, DA_VD), lambda bi, hi, qi, lam_r: (bi, qi, hi)),
        scratch_shapes=[pltpu.VMEM((2 * tq, LANES), BF16), pltpu.VMEM((2, 2 * tq, tk), F32),
                        pltpu.VMEM((2 * tq, LANES), F32), pltpu.VMEM((2 * tq, 2 * LANES), F32)],
    )
    return pl.pallas_call(
        functools.partial(_diff_attn_kernel, tk=tk, n_sub=n_sub, out_scale=out_scale),
        grid_spec=grid_spec,
        out_shape=jax.ShapeDtypeStruct((b, n, DA_W), BF16),
        compiler_params=pltpu.CompilerParams(vmem_limit_bytes=VMEM_LIMIT),
        name="diff_attention",
    )(lam.reshape(1), q, k_all, v_all, subln_g.reshape(1, DA_VD))


def _proj_residual_kernel(a_ref, w_ref, x_ref, gate_ref, g_ref, shift_ref, scale_ref, wr_ref, br_ref,
                          xo_ref, h_ref, lg_ref):
    y = jnp.dot(a_ref[...], w_ref[...], preferred_element_type=F32)
    xn = x_ref[...] + gate_ref[0] * y
    xo_ref[...] = xn
    ms = jnp.mean(xn * xn, axis=-1, keepdims=True)
    h = xn * lax.rsqrt(ms + NORM_EPS) * g_ref[...]
    h = (h * (1.0 + scale_ref[0]) + shift_ref[0]).astype(BF16)
    h_ref[...] = h
    lg_ref[...] = jnp.dot(h, wr_ref[...], preferred_element_type=F32) + br_ref[...]


def _proj_residual(a, w, x2d, gate, g, shift, scale, w_router, b_router, rows_per_batch, tm=512):
    t, d = x2d.shape
    kdim = a.shape[1]
    bpb = rows_per_batch // tm
    wr = jnp.zeros((d, LANES), BF16).at[:, :N_EXPERTS].set(w_router.astype(BF16))
    br = jnp.zeros((1, LANES), F32).at[0, :N_EXPERTS].set(b_router)
    vec = pl.BlockSpec((1, 1, d), lambda i: (i // bpb, 0, 0))
    row = lambda n: pl.BlockSpec((tm, n), lambda i: (i, 0))
    full = lambda s: pl.BlockSpec(s, lambda i: (0, 0))
    return pl.pallas_call(
        _proj_residual_kernel,
        grid=(t // tm,),
        in_specs=[row(kdim), full((kdim, d)), row(d), vec, full((1, d)), vec, vec,
                  full((d, LANES)), full((1, LANES))],
        out_specs=[row(d), row(d), row(LANES)],
        out_shape=[jax.ShapeDtypeStruct((t, d), F32), jax.ShapeDtypeStruct((t, d), BF16),
                   jax.ShapeDtypeStruct((t, LANES), F32)],
        compiler_params=pltpu.CompilerParams(vmem_limit_bytes=VMEM_LIMIT),
        name="proj_residual",
    )(a, w, x2d, gate, g.reshape(1, d), shift, scale, wr, br)


MXU_DIM = 256


def _deinterleave_kernel(w_ref, perm_ref, g_ref, l_ref):
    w = w_ref[0].astype(BF16)
    for c in range(w.shape[1] // MXU_DIM):
        blk = jnp.dot(w[:, c * MXU_DIM:(c + 1) * MXU_DIM], perm_ref[...], preferred_element_type=F32)
        g_ref[0, :, c * LANES:(c + 1) * LANES] = blk[:, :LANES].astype(BF16)
        l_ref[0, :, c * LANES:(c + 1) * LANES] = blk[:, LANES:].astype(BF16)


def _deinterleave_w1(w1, tm=512):
    e, d, f2 = w1.shape
    src = jnp.arange(MXU_DIM)[:, None]
    dst = jnp.arange(MXU_DIM)[None, :]
    perm = (src == jnp.where(dst < LANES, 2 * dst, 2 * (dst - LANES) + 1)).astype(BF16)
    out = jax.ShapeDtypeStruct((e, d, f2 // 2), BF16)
    return pl.pallas_call(
        _deinterleave_kernel,
        grid=(e, d // tm),
        in_specs=[pl.BlockSpec((1, tm, f2), lambda i, j: (i, j, 0)),
                  pl.BlockSpec((MXU_DIM, MXU_DIM), lambda i, j: (0, 0))],
        out_specs=[pl.BlockSpec((1, tm, f2 // 2), lambda i, j: (i, j, 0))] * 2,
        out_shape=[out, out],
        compiler_params=pltpu.CompilerParams(vmem_limit_bytes=VMEM_LIMIT),
        name="deinterleave_w1",
    )(w1, perm)


def _moe_kernel(be_ref, x_ref, w1g_ref, w1l_ref, b1g_ref, b1l_ref, w2_ref, b2_ref, o_ref):
    x = x_ref[...]
    ug = jnp.dot(x, w1g_ref[0], preferred_element_type=F32) + b1g_ref[0]
    ul = jnp.dot(x, w1l_ref[0], preferred_element_type=F32) + b1l_ref[0]
    glu = jnp.minimum(ug, SWIGLU_LIMIT)
    lin = jnp.clip(ul, -SWIGLU_LIMIT, SWIGLU_LIMIT)
    act = glu * jax.nn.sigmoid(SWIGLU_ALPHA * glu) * (lin + 1.0)
    o_ref[...] = jnp.dot(act.astype(BF16), w2_ref[0], preferred_element_type=F32) + b2_ref[0]


def _moe_experts(block_expert, xb, w1g, w1l, b1g, b1l, w2, b2):
    n_rows, d = xb.shape
    n_blocks = n_rows // MOE_BLOCK
    wspec = lambda s: pl.BlockSpec((1,) + s, lambda i, be: (be[i], 0, 0))
    grid_spec = pltpu.PrefetchScalarGridSpec(
        num_scalar_prefetch=1,
        grid=(n_blocks,),
        in_specs=[pl.BlockSpec((MOE_BLOCK, d), lambda i, be: (i, 0)),
                  wspec((d, D_FF)), wspec((d, D_FF)), wspec((1, D_FF)), wspec((1, D_FF)),
                  wspec((D_FF, d)), wspec((1, d))],
        out_specs=pl.BlockSpec((MOE_BLOCK, d), lambda i, be: (i, 0)),
    )
    return pl.pallas_call(
        _moe_kernel,
        grid_spec=grid_spec,
        out_shape=jax.ShapeDtypeStruct((n_rows, d), F32),
        compiler_params=pltpu.CompilerParams(vmem_limit_bytes=VMEM_LIMIT),
        name="moe_experts",
    )(block_expert, xb, w1g, w1l, b1g, b1l, w2, b2)


def _moe_ffn(h, logits, w1, b1, w2, b2):
    n_tok, d = h.shape
    top_val, top_idx = lax.top_k(logits, TOP_K)
    gates = jax.nn.softmax(top_val, axis=-1)
    n_assign = n_tok * TOP_K
    e_flat = top_idx.reshape(n_assign).astype(jnp.int32)
    order = jnp.argsort(e_flat).astype(jnp.int32)
    e_sorted = e_flat[order]
    tok_sorted = order // TOP_K
    counts = jnp.zeros((N_EXPERTS,), jnp.int32).at[e_flat].add(1)
    starts = jnp.cumsum(counts) - counts
    padded = (counts + MOE_BLOCK - 1) // MOE_BLOCK * MOE_BLOCK
    pad_ends = jnp.cumsum(padded)
    pad_starts = pad_ends - padded
    dest = pad_starts[e_sorted] + jnp.arange(n_assign, dtype=jnp.int32) - starts[e_sorted]
    n_blocks = -(-(n_assign + N_EXPERTS * (MOE_BLOCK - 1)) // MOE_BLOCK)
    n_rows = n_blocks * MOE_BLOCK
    row_tok = jnp.zeros((n_rows,), jnp.int32).at[dest].set(tok_sorted)
    block_start = jnp.arange(n_blocks, dtype=jnp.int32) * MOE_BLOCK
    block_expert = jnp.minimum(jnp.searchsorted(pad_ends, block_start, side='right'),
                               N_EXPERTS - 1).astype(jnp.int32)
    xb = h[row_tok]
    w1g, w1l = _deinterleave_w1(w1)
    b1g = b1[:, None, 0::2]
    b1l = b1[:, None, 1::2]
    yb = _moe_experts(block_expert, xb, w1g, w1l, b1g, b1l, w2.astype(BF16), b2[:, None, :])
    pos = jnp.zeros((n_assign,), jnp.int32).at[order].set(dest).reshape(n_tok, TOP_K)
    return jnp.sum(yb[pos] * gates[..., None], axis=1)


def _final_norm_kernel(x_ref, g_ref, o_ref):
    x = x_ref[...]
    ms = jnp.mean(x * x, axis=-1, keepdims=True)
    o_ref[...] = x * lax.rsqrt(ms + NORM_EPS) * g_ref[...]


def _final_norm(x2d, g, tm=1024):
    t, d = x2d.shape
    return pl.pallas_call(
        _final_norm_kernel,
        grid=(t // tm,),
        in_specs=[pl.BlockSpec((tm, d), lambda i: (i, 0)),
                  pl.BlockSpec((1, d), lambda i: (0, 0))],
        out_specs=pl.BlockSpec((tm, d), lambda i: (i, 0)),
        out_shape=jax.ShapeDtypeStruct((t, d), x2d.dtype),
        name="final_norm",
    )(x2d, g.reshape(1, d))


GDN_CONV = 5
HALO = 8


def _gdn_prep_kernel(pm_ref, pp_ref, pn_ref, ab_ref, cw_ref, alog_ref, dtb_ref, q_ref, k_ref, v_ref, gb_ref):
    i = pl.program_id(1)
    n_i = pl.num_programs(1)
    tm = pm_ref.shape[1]
    half = GDN_CONV // 2
    keep_prev = (i > 0).astype(F32)
    keep_next = (i < n_i - 1).astype(F32)
    for cg in range(GDN_QKV_W // LANES):
        cols = slice(cg * LANES, (cg + 1) * LANES)
        main = pm_ref[0, :, cols]
        ext = jnp.concatenate([pp_ref[0, :, cols] * keep_prev, main, pn_ref[0, :, cols] * keep_next], axis=0)
        acc = main * cw_ref[half:half + 1, cols]
        for j in range(GDN_CONV):
            if j != half:
                sh = pltpu.roll(ext, (half - j) % (tm + 2 * HALO), 0)[HALO:HALO + tm]
                acc = acc + sh * cw_ref[j:j + 1, cols]
        y = acc * jax.nn.sigmoid(acc)
        if cg < 2 * GDN_HK:
            y = y * lax.rsqrt(jnp.sum(y * y, axis=-1, keepdims=True) + 1e-6)
            if cg < GDN_HK:
                q_ref[0, :, cols] = y * (GDN_DK ** -0.5)
            else:
                k_ref[0, :, (cg - GDN_HK) * LANES:(cg - GDN_HK + 1) * LANES] = y
        else:
            v_ref[0, :, (cg - 2 * GDN_HK) * LANES:(cg - 2 * GDN_HK + 1) * LANES] = y
    ab = ab_ref[0]
    xa = ab + dtb_ref[...]
    softplus = jnp.maximum(xa, 0.0) + jnp.log1p(jnp.exp(-jnp.abs(xa)))
    lane = lax.broadcasted_iota(jnp.int32, ab.shape, 1)
    gb_ref[0] = jnp.where(lane < 2 * GDN_HV, -jnp.exp(alog_ref[...]) * softplus, jax.nn.sigmoid(ab))


def _gdn_prep(p_main, p_ab, conv_w, a_log, dt_bias, tm=256):
    b, t, _ = p_main.shape
    tm = min(tm, t)
    nb = tm // HALO
    last = t // HALO - 1
    pad = lambda a: jnp.zeros((1, LANES), F32).at[0, :2 * GDN_HV].set(a.reshape(-1))
    f = lambda n: jax.ShapeDtypeStruct((b, t, n), F32)
    return pl.pallas_call(
        _gdn_prep_kernel,
        grid=(b, t // tm),
        in_specs=[pl.BlockSpec((1, tm, GDN_QKV_W), lambda bi, i: (bi, i, 0)),
                  pl.BlockSpec((1, HALO, GDN_QKV_W), lambda bi, i: (bi, jnp.maximum(i * nb - 1, 0), 0)),
                  pl.BlockSpec((1, HALO, GDN_QKV_W), lambda bi, i: (bi, jnp.minimum((i + 1) * nb, last), 0)),
                  pl.BlockSpec((1, tm, LANES), lambda bi, i: (bi, i, 0)),
                  pl.BlockSpec((GDN_CONV, GDN_QKV_W), lambda bi, i: (0, 0)),
                  pl.BlockSpec((1, LANES), lambda bi, i: (0, 0)),
                  pl.BlockSpec((1, LANES), lambda bi, i: (0, 0))],
        out_specs=[pl.BlockSpec((1, tm, GDN_QK_W), lambda bi, i: (bi, i, 0)),
                   pl.BlockSpec((1, tm, GDN_QK_W), lambda bi, i: (bi, i, 0)),
                   pl.BlockSpec((1, tm, GDN_V_W), lambda bi, i: (bi, i, 0)),
                   pl.BlockSpec((1, tm, LANES), lambda bi, i: (bi, i, 0))],
        out_shape=[f(GDN_QK_W), f(GDN_QK_W), f(GDN_V_W), f(LANES)],
        compiler_params=pltpu.CompilerParams(vmem_limit_bytes=VMEM_LIMIT),
        name="gdn_prep",
    )(p_main, p_main, p_main, p_ab, conv_w, pad(a_log), pad(dt_bias))


def _gdn_chunk_kernel(q_ref, k_ref, v_ref, gb_ref, s0_ref, o_ref, sfin_ref, s_ref, *, reverse, g_lane, b_lane):
    c = pl.program_id(1)
    n_c = pl.num_programs(1)
    cs = GDN_CHUNK
    rep = GDN_HV // GDN_HK

    @pl.when(c == 0)
    def _():
        s_ref[...] = s0_ref[0]

    row = lax.broadcasted_iota(jnp.int32, (cs, cs), 0)
    col = lax.broadcasted_iota(jnp.int32, (cs, cs), 1)
    incl = (row <= col) if reverse else (row >= col)
    strict = (row < col) if reverse else (row > col)
    eye = (row == col).astype(F32)
    incl_b = incl.astype(BF16)
    last = 0 if reverse else cs - 1

    gb = gb_ref[0]
    gb_hi = gb.astype(BF16)
    gb_lo = (gb - gb_hi.astype(F32)).astype(BF16)
    g_col = (jnp.dot(incl_b, gb_hi, preferred_element_type=F32)
             + jnp.dot(incl_b, gb_lo, preferred_element_type=F32))
    tn = (((0,), (1,)), ((), ()))
    g_row = (lax.dot_general(gb_hi, incl_b, tn, preferred_element_type=F32)
             + lax.dot_general(gb_lo, incl_b, tn, preferred_element_type=F32))

    heads = range(GDN_HV)
    gc = jnp.stack([g_col[:, g_lane + h:g_lane + h + 1] for h in heads])
    gr = jnp.stack([g_row[g_lane + h:g_lane + h + 1, :] for h in heads])
    bc = jnp.stack([gb[:, b_lane + h:b_lane + h + 1] for h in heads])
    ge = gc[:, last:last + 1, :]
    qh = jnp.stack([q_ref[0, :, h * GDN_DK:(h + 1) * GDN_DK] for h in range(GDN_HK)])
    kh = jnp.stack([k_ref[0, :, h * GDN_DK:(h + 1) * GDN_DK] for h in range(GDN_HK)])
    vh = jnp.stack([v_ref[0, :, h * GDN_DV:(h + 1) * GDN_DV] for h in heads])
    kh_b = kh.astype(BF16)
    bnt = (((2,), (2,)), ((0,), (0,)))
    kk = lax.dot_general(kh_b, kh_b, bnt, preferred_element_type=F32)
    qk = lax.dot_general(qh.astype(BF16), kh_b, bnt, preferred_element_type=F32)
    kk, qk = jnp.repeat(kk, rep, axis=0), jnp.repeat(qk, rep, axis=0)
    qv, kv = jnp.repeat(qh, rep, axis=0), jnp.repeat(kh, rep, axis=0)

    decay = jnp.where(incl, jnp.exp(jnp.where(incl, gc - gr, 0.0)), 0.0)
    lm = jnp.where(strict, bc * kk * decay, 0.0)
    bmm = lambda a, b: jnp.einsum('hij,hjk->hik', a.astype(BF16), b.astype(BF16), preferred_element_type=F32)
    p = bmm(lm, lm)
    x = eye - lm
    for _ in range(4):
        xp = bmm(jnp.concatenate([x, p], axis=1), p)
        x = x + xp[:, :cs]
        p = xp[:, cs:]
    x = x + bmm(x, p)
    eg = jnp.exp(gc)
    uw = bmm(x, jnp.concatenate([vh * bc, kv * (bc * eg)], axis=2))
    u = uw[:, :, :GDN_DV]
    w = uw[:, :, GDN_DV:]
    qg = qv * eg
    intra = jnp.where(incl, qk * decay, 0.0)
    kt = kv * jnp.exp(ge - gc)
    s = s_ref[...]
    wq = bmm(jnp.concatenate([w, qg], axis=1), s)
    v_new = u - wq[:, :cs]
    o = wq[:, cs:] + bmm(intra, v_new)
    for h in heads:
        o_ref[0, :, h * GDN_DV:(h + 1) * GDN_DV] = o[h]
    s_ref[...] = s * jnp.exp(ge) + jnp.einsum('hck,hcv->hkv', kt.astype(BF16), v_new.astype(BF16),
                                              preferred_element_type=F32)

    @pl.when(c == n_c - 1)
    def _():
        sfin_ref[0] = s_ref[...]


def _gdn_scan(q, k, v, gb, s0, reverse):
    b, t, _ = q.shape
    n_c = t // GDN_CHUNK
    cm = (lambda bi, ci: (bi, n_c - 1 - ci, 0)) if reverse else (lambda bi, ci: (bi, ci, 0))
    d = 1 if reverse else 0
    smap = lambda bi, ci: (bi, 0, 0, 0)
    return pl.pallas_call(
        functools.partial(_gdn_chunk_kernel, reverse=reverse, g_lane=d * GDN_HV, b_lane=(2 + d) * GDN_HV),
        grid=(b, n_c),
        in_specs=[pl.BlockSpec((1, GDN_CHUNK, GDN_QK_W), cm), pl.BlockSpec((1, GDN_CHUNK, GDN_QK_W), cm),
                  pl.BlockSpec((1, GDN_CHUNK, GDN_V_W), cm), pl.BlockSpec((1, GDN_CHUNK, LANES), cm),
                  pl.BlockSpec((1, GDN_HV, GDN_DK, GDN_DV), smap)],
        out_specs=[pl.BlockSpec((1, GDN_CHUNK, GDN_V_W), cm), pl.BlockSpec((1, GDN_HV, GDN_DK, GDN_DV), smap)],
        out_shape=[jax.ShapeDtypeStruct((b, t, GDN_V_W), F32),
                   jax.ShapeDtypeStruct((b, GDN_HV, GDN_DK, GDN_DV), F32)],
        scratch_shapes=[pltpu.VMEM((GDN_HV, GDN_DK, GDN_DV), F32)],
        compiler_params=pltpu.CompilerParams(vmem_limit_bytes=VMEM_LIMIT),
        name="gdn_scan_bwd" if reverse else "gdn_scan_fwd",
    )(q, k, v, gb, s0)


def _gdn_out_kernel(of_ref, ob_ref, z_ref, ng_ref, w_ref, x_ref, gate_ref, g_ref, shift_ref, scale_ref,
                    wr_ref, br_ref, xo_ref, h_ref, lg_ref):
    parts = []
    for h in range(GDN_HV):
        cols = slice(h * GDN_DV, (h + 1) * GDN_DV)
        o = of_ref[:, cols] + ob_ref[:, cols]
        z = z_ref[:, cols]
        o = o * lax.rsqrt(jnp.mean(o * o, axis=-1, keepdims=True) + NORM_EPS) * ng_ref[...]
        parts.append((o * (z * jax.nn.sigmoid(z))).astype(BF16))
    y = jnp.dot(jnp.concatenate(parts, axis=1), w_ref[...], preferred_element_type=F32)
    xn = x_ref[...] + gate_ref[0] * y
    xo_ref[...] = xn
    ms = jnp.mean(xn * xn, axis=-1, keepdims=True)
    hh = xn * lax.rsqrt(ms + NORM_EPS) * g_ref[...]
    hh = (hh * (1.0 + scale_ref[0]) + shift_ref[0]).astype(BF16)
    h_ref[...] = hh
    lg_ref[...] = jnp.dot(hh, wr_ref[...], preferred_element_type=F32) + br_ref[...]


def _gdn_out_proj(o_f, o_b, p_main, norm_g, w, x2d, gate, g, shift, scale, w_router, b_router, rows_per_batch,
                  tm=256):
    t, d = x2d.shape
    bpb = rows_per_batch // tm
    wr = jnp.zeros((d, LANES), BF16).at[:, :N_EXPERTS].set(w_router.astype(BF16))
    br = jnp.zeros((1, LANES), F32).at[0, :N_EXPERTS].set(b_router)
    vec = pl.BlockSpec((1, 1, d), lambda i: (i // bpb, 0, 0))
    row = lambda n: pl.BlockSpec((tm, n), lambda i: (i, 0))
    full = lambda s: pl.BlockSpec(s, lambda i: (0, 0))
    return pl.pallas_call(
        _gdn_out_kernel,
        grid=(t // tm,),
        in_specs=[row(GDN_V_W), row(GDN_V_W), pl.BlockSpec((tm, GDN_V_W), lambda i: (i, GDN_QKV_W // GDN_V_W)),
                  full((1, GDN_DV)), full((GDN_V_W, d)), row(d), vec, full((1, d)), vec, vec,
                  full((d, LANES)), full((1, LANES))],
        out_specs=[row(d), row(d), row(LANES)],
        out_shape=[jax.ShapeDtypeStruct((t, d), F32), jax.ShapeDtypeStruct((t, d), BF16),
                   jax.ShapeDtypeStruct((t, LANES), F32)],
        compiler_params=pltpu.CompilerParams(vmem_limit_bytes=VMEM_LIMIT),
        name="gdn_out_proj",
    )(o_f, o_b, p_main, norm_g.reshape(1, GDN_DV), w, x2d, gate, g.reshape(1, d), shift, scale, wr, br)


def _even_layer(x2d, xc, mod, mod_c, norm1_g, norm2_g, w_in, w_out, lam_p, subln_g, lam_init, cos, sin,
                w_router, b_router, bsz, n_lat):
    d = x2d.shape[1]
    w_in_b = w_in.astype(BF16)
    w_out_b = w_out.astype(BF16)
    (p,) = _norm_mod_matmul(x2d, norm1_g, mod[0], mod[1], [w_in_b], n_lat)
    f, q, k, v = _even_split(p.reshape(bsz, n_lat, -1))
    hc = _modulate(_rms_norm(xc, norm1_g), mod_c[0], mod_c[1])
    fc, qc, kc, vc = _even_split(hc @ w_in)
    q = _rope2d(q, cos, sin) * (DA_DH ** -0.5 * math.log2(math.e))
    k = _rope2d(k, cos, sin)
    lp = lam_p.astype(jnp.float32)
    lam = jnp.exp(jnp.sum(lp[0] * lp[1])) - jnp.exp(jnp.sum(lp[2] * lp[3])) + lam_init
    n_ctx = xc.shape[1]
    k_all = jnp.concatenate([k, kc], axis=1).astype(BF16).reshape(bsz, n_lat + n_ctx, DA_W)
    v_all = jnp.concatenate([v, vc], axis=1).astype(BF16).reshape(bsz, n_lat + n_ctx, DA_W)
    o = _diff_attention(lam, q.astype(BF16).reshape(bsz, n_lat, DA_W), k_all, v_all, subln_g, 1.0 - lam_init)
    fm = _fourier_mix(f).reshape(bsz, n_lat, FOURIER_W).astype(BF16)
    mixed = jnp.concatenate([fm, o], axis=-1).reshape(bsz * n_lat, -1)
    x2d, h2, logits = _proj_residual(mixed, w_out_b, x2d, mod[2], norm2_g, mod[3], mod[4],
                                     w_router, b_router, n_lat)
    oc = _diff_attend(qc, kc, vc, lam)
    oc = _rms_norm(oc, subln_g, SUBLN_EPS) * (1.0 - lam_init)
    mixed_c = jnp.concatenate([_fourier_mix(fc).reshape(bsz, n_ctx, FOURIER_W), oc.reshape(bsz, n_ctx, DA_W)], axis=-1)
    xc = xc + mod_c[2] * (mixed_c @ w_out)
    return x2d, h2, logits, xc


def _odd_layer(x2d, xc, mod, mod_c, norm1_g, norm2_g, w_in, conv_w, a_log, dt_bias, norm_g, w_out,
               w_router, b_router, bsz, n_lat):
    d = x2d.shape[1]
    n_main = GDN_QKV_W + GDN_V_W
    w_main = w_in[:, :n_main].astype(BF16)
    w_ab = jnp.zeros((d, LANES), BF16).at[:, :4 * GDN_HV].set(w_in[:, n_main:].astype(BF16))
    n_ctx = xc.shape[1]
    p_main, p_ab = _norm_mod_matmul(x2d, norm1_g, mod[0], mod[1], [w_main, w_ab], n_lat)
    q, k, v, gb = _gdn_prep(p_main.reshape(bsz, n_lat, n_main), p_ab.reshape(bsz, n_lat, LANES),
                            conv_w, a_log, dt_bias)
    pc_main, pc_ab = _norm_mod_matmul(xc.reshape(bsz * n_ctx, d), norm1_g, mod_c[0], mod_c[1], [w_main, w_ab],
                                      bsz * n_ctx, tm=n_ctx)
    qc, kc, vc, gbc = _gdn_prep(pc_main.reshape(bsz, n_ctx, n_main), pc_ab.reshape(bsz, n_ctx, LANES),
                                conv_w, a_log, dt_bias)
    s0 = jnp.zeros((bsz, GDN_HV, GDN_DK, GDN_DV), jnp.float32)
    _, sc_f = _gdn_scan(qc, kc, vc, gbc, s0, False)
    o_f, _ = _gdn_scan(q, k, v, gb, sc_f, False)
    _, sc_b = _gdn_scan(qc, kc, vc, gbc, s0, True)
    o_b, _ = _gdn_scan(q, k, v, gb, sc_b, True)
    return _gdn_out_proj(o_f.reshape(bsz * n_lat, GDN_V_W), o_b.reshape(bsz * n_lat, GDN_V_W), p_main, norm_g,
                         w_out.astype(BF16), x2d, mod[2], norm2_g, mod[3], mod[4], w_router, b_router, n_lat)


def kernel(x, c, ctx, c_ctx, norm1_g, norm2_g, w_mod, b_mod, ev_w_in, ev_w_out, ev_lam, ev_subln_g,
           od_w_in, od_conv_w, od_a_log, od_dt_bias, od_norm_g, od_w_out,
           moe_w_router, moe_b_router, moe_w1, moe_b1, moe_w2, moe_b2, final_g):
    bsz, n_lat, d = x.shape
    n_ctx = ctx.shape[1]
    depth = w_mod.shape[0]
    assert depth == 2, "kernel is written for one even (attention) and one odd (DeltaNet) layer"
    cos, sin = _axial_rope_tables(n_lat // GRID_W)
    s_lat = jax.nn.silu(c)
    s_ctx = jax.nn.silu(c_ctx)
    x2d = x.reshape(bsz * n_lat, d)
    xc = ctx

    mod = jnp.split((s_lat @ w_mod[0] + b_mod[0])[:, None, :], N_MOD, axis=-1)
    mod_c = jnp.split((s_ctx @ w_mod[0] + b_mod[0])[None, None, :], N_MOD, axis=-1)
    x2d, h2, logits, xc = _even_layer(x2d, xc, mod, mod_c, norm1_g[0], norm2_g[0], ev_w_in[0], ev_w_out[0],
                                      ev_lam[0], ev_subln_g[0], _diff_lambda_init(0), cos, sin,
                                      moe_w_router[0], moe_b_router[0], bsz, n_lat)
    h2c = _modulate(_rms_norm(xc, norm2_g[0]), mod_c[3], mod_c[4]).reshape(bsz * n_ctx, d)
    logits_c = (h2c @ moe_w_router[0] + moe_b_router[0]).astype(jnp.float32)
    out = _moe_ffn(jnp.concatenate([h2, h2c.astype(BF16)], axis=0),
                   jnp.concatenate([logits[:, :N_EXPERTS], logits_c], axis=0),
                   moe_w1[0], moe_b1[0], moe_w2[0], moe_b2[0])
    gate5 = jnp.broadcast_to(mod[5], (bsz, n_lat, d)).reshape(bsz * n_lat, d)
    x2d = x2d + gate5 * out[:bsz * n_lat]
    xc = xc + mod_c[5] * out[bsz * n_lat:].reshape(bsz, n_ctx, d)

    mod = jnp.split((s_lat @ w_mod[1] + b_mod[1])[:, None, :], N_MOD, axis=-1)
    mod_c = jnp.split((s_ctx @ w_mod[1] + b_mod[1])[None, None, :], N_MOD, axis=-1)
    x2d, h2, logits = _odd_layer(x2d, xc, mod, mod_c, norm1_g[1], norm2_g[1], od_w_in[0], od_conv_w[0],
                                 od_a_log[0], od_dt_bias[0], od_norm_g[0], od_w_out[0],
                                 moe_w_router[1], moe_b_router[1], bsz, n_lat)
    out = _moe_ffn(h2, logits[:, :N_EXPERTS], moe_w1[1], moe_b1[1], moe_w2[1], moe_b2[1])
    gate5 = jnp.broadcast_to(mod[5], (bsz, n_lat, d)).reshape(bsz * n_lat, d)
    x2d = x2d + gate5 * out
    return _final_norm(x2d, final_g).reshape(bsz, n_lat, d)
```

```python
import functools
import math

import jax
import jax.numpy as jnp
from jax import lax
from jax.experimental import pallas as pl
from jax.experimental.pallas import tpu as pltpu

D_MODEL = 1024
N_MOD = 6
NORM_EPS = 1e-6
GRID_W = 64

FOURIER_GROUPS = 4
FOURIER_GD = 64
FOURIER_W = FOURIER_GROUPS * FOURIER_GD
DA_HEADS = 6
DA_DH = 64
DA_VD = 2 * DA_DH
DA_W = DA_HEADS * DA_VD
ROPE_BASE = 10000.0
ROPE_AXIS_DIM = DA_DH // 2
SUBLN_EPS = 1e-5

GDN_HK = 8
GDN_HV = 16
GDN_DK = 128
GDN_DV = 128
GDN_QK_W = GDN_HK * GDN_DK
GDN_V_W = GDN_HV * GDN_DV
GDN_QKV_W = 2 * GDN_QK_W + GDN_V_W
GDN_CHUNK = 64

N_EXPERTS = 32
TOP_K = 4
D_FF = 1024
SWIGLU_LIMIT = 7.0
SWIGLU_ALPHA = 1.702
MOE_BLOCK = 512

LANES = 128
VMEM_LIMIT = 56 * 1024 * 1024
BF16 = jnp.bfloat16
F32 = jnp.float32


def _rms_norm(x, g, eps=NORM_EPS):
    xf = x.astype(jnp.float32)
    y = xf * lax.rsqrt(jnp.mean(xf * xf, axis=-1, keepdims=True) + eps)
    return y.astype(x.dtype) * g


def _modulate(h, shift, scale):
    return h * (1 + scale) + shift


def _l2_normalize(x, eps=1e-6):
    return x * lax.rsqrt(jnp.sum(x * x, axis=-1, keepdims=True) + eps)


def _diff_lambda_init(layer_idx):
    return 0.8 - 0.6 * math.exp(-0.3 * layer_idx)


def _axial_rope_tables(rows):
    t = jnp.arange(rows * GRID_W, dtype=jnp.int32)
    row = (t // GRID_W).astype(jnp.float32)
    col = (t % GRID_W).astype(jnp.float32)
    inv = ROPE_BASE ** (-jnp.arange(0, ROPE_AXIS_DIM, 2, dtype=jnp.float32) / ROPE_AXIS_DIM)
    ang = jnp.concatenate([row[:, None] * inv, col[:, None] * inv], axis=-1)
    return jnp.cos(ang), jnp.sin(ang)


def _rope2d(x, cos, sin):
    half = x.shape[-1] // 2
    xf = x.astype(jnp.float32)
    x1, x2 = xf[..., :half], xf[..., half:]
    cb = cos[None, :, None, None, :]
    sb = sin[None, :, None, None, :]
    return jnp.concatenate([x1 * cb - x2 * sb, x2 * cb + x1 * sb], axis=-1).astype(x.dtype)


def _fourier_mix(f):
    ff = jnp.moveaxis(f.astype(jnp.float32), 2, 1)
    out = jnp.real(jnp.fft.fft2(ff, norm='ortho'))
    return jnp.moveaxis(out, 1, 2).astype(f.dtype)


def _diff_attend(q, k, v, lam):
    s = jnp.einsum('bqhmd,bkhmd->bhmqk', q, k, preferred_element_type=jnp.float32) * (DA_DH ** -0.5)
    p = jax.nn.softmax(s, axis=-1)
    a = (p[:, :, 0] - lam * p[:, :, 1]).astype(v.dtype)
    return jnp.einsum('bhqk,bkhe->bqhe', a, v)


def _even_split(p):
    b, n, _ = p.shape
    f = p[..., :FOURIER_W].reshape(b, n, FOURIER_GROUPS, FOURIER_GD)
    q, k, v = jnp.split(p[..., FOURIER_W:], 3, axis=-1)
    q = q.reshape(b, n, DA_HEADS, 2, DA_DH)
    k = k.reshape(b, n, DA_HEADS, 2, DA_DH)
    v = v.reshape(b, n, DA_HEADS, DA_VD)
    return f, q, k, v


def _centred_depthwise_conv(x, w):
    width = w.shape[0]
    return lax.conv_general_dilated(
        x, w[:, None, :].astype(x.dtype), window_strides=(1,),
        padding=[((width - 1) // 2, width // 2)],
        dimension_numbers=('NWC', 'WIO', 'NWC'),
        feature_group_count=x.shape[-1])


def _chunk_gated_delta(q, k, v, g, beta, s0):
    b, t, nh, _ = q.shape
    n = t // GDN_CHUNK

    def chunks(a):
        a = a.reshape((b, n, GDN_CHUNK, nh) + a.shape[3:])
        return jnp.moveaxis(a, (1, 3), (0, 2))

    qc, kc, vc, gc, bc = chunks(q), chunks(k), chunks(v), chunks(g), chunks(beta)
    G = jnp.cumsum(gc, axis=-1)
    idx = jnp.arange(GDN_CHUNK)
    incl = idx[:, None] >= idx[None, :]
    strict = idx[:, None] > idx[None, :]
    diff = G[..., :, None] - G[..., None, :]
    decay = jnp.where(incl, jnp.exp(jnp.where(incl, diff, 0.0)), 0.0)
    kb = kc * bc[..., None]
    lmat = jnp.where(strict, jnp.einsum('...id,...jd->...ij', kb, kc) * decay, 0.0)
    eye = jnp.broadcast_to(jnp.eye(GDN_CHUNK, dtype=jnp.float32), lmat.shape)
    tmat = lax.linalg.triangular_solve(lmat + eye, eye, left_side=True, lower=True)
    u = tmat @ (vc * bc[..., None])
    w = tmat @ (kb * jnp.exp(G)[..., None])
    qg = qc * jnp.exp(G)[..., None]
    intra = jnp.where(incl, jnp.einsum('...id,...jd->...ij', qc, kc) * decay, 0.0)
    g_end = G[..., -1]
    k_tail = kc * jnp.exp(g_end[..., None] - G)[..., None]

    def step(state, xs):
        qg_i, u_i, w_i, intra_i, kt_i, ge_i = xs
        v_new = u_i - w_i @ state
        o_i = qg_i @ state + intra_i @ v_new
        state = state * jnp.exp(ge_i)[..., None, None] + jnp.swapaxes(kt_i, -1, -2) @ v_new
        return state, o_i

    s_fin, o = lax.scan(step, s0, (qg, u, w, intra, k_tail, g_end))
    o = jnp.moveaxis(o, (0, 2), (1, 3)).reshape(b, t, nh, -1)
    return o, s_fin


def _gdn_prepare(p_qkv, p_z, p_ab, conv_w, a_log, dt_bias):
    b, n, _ = p_qkv.shape
    qkv = jax.nn.silu(_centred_depthwise_conv(p_qkv, conv_w)).astype(jnp.float32)
    q = qkv[..., :GDN_QK_W].reshape(b, n, GDN_HK, GDN_DK)
    k = qkv[..., GDN_QK_W:2 * GDN_QK_W].reshape(b, n, GDN_HK, GDN_DK)
    v = qkv[..., 2 * GDN_QK_W:].reshape(b, n, GDN_HV, GDN_DV)
    rep = GDN_HV // GDN_HK
    q = jnp.repeat(_l2_normalize(q) * (GDN_DK ** -0.5), rep, axis=2)
    k = jnp.repeat(_l2_normalize(k), rep, axis=2)
    z = p_z.reshape(b, n, GDN_HV, GDN_DV)
    ab = p_ab.astype(jnp.float32).reshape(b, n, 2, 2, GDN_HV)
    a, bb = ab[:, :, 0], ab[:, :, 1]
    g = -jnp.exp(a_log.astype(jnp.float32)) * jax.nn.softplus(a + dt_bias.astype(jnp.float32))
    beta = jax.nn.sigmoid(bb)
    return q, k, v, z, g, beta


def _norm_mod_matmul_kernel(x_ref, g_ref, shift_ref, scale_ref, *refs, n_w):
    x = x_ref[...]
    ms = jnp.mean(x * x, axis=-1, keepdims=True)
    h = x * lax.rsqrt(ms + NORM_EPS) * g_ref[...]
    h = (h * (1.0 + scale_ref[0]) + shift_ref[0]).astype(BF16)
    for w_ref, o_ref in zip(refs[:n_w], refs[n_w:]):
        o_ref[...] = jnp.dot(h, w_ref[...], preferred_element_type=F32).astype(o_ref.dtype)


def _norm_mod_matmul(x2d, g, shift, scale, ws, rows_per_batch, tm=512):
    t, d = x2d.shape
    bpb = rows_per_batch // tm
    in_specs = [pl.BlockSpec((tm, d), lambda i: (i, 0)),
                pl.BlockSpec((1, d), lambda i: (0, 0)),
                pl.BlockSpec((1, 1, d), lambda i: (i // bpb, 0, 0)),
                pl.BlockSpec((1, 1, d), lambda i: (i // bpb, 0, 0))]
    in_specs += [pl.BlockSpec(w.shape, lambda i: (0, 0)) for w in ws]
    out_specs = [pl.BlockSpec((tm, w.shape[1]), lambda i: (i, 0)) for w in ws]
    out_shape = [jax.ShapeDtypeStruct((t, w.shape[1]), F32) for w in ws]
    return pl.pallas_call(
        functools.partial(_norm_mod_matmul_kernel, n_w=len(ws)),
        grid=(t // tm,),
        in_specs=in_specs, out_specs=out_specs, out_shape=out_shape,
        compiler_params=pltpu.CompilerParams(vmem_limit_bytes=VMEM_LIMIT),
        name="norm_mod_matmul",
    )(x2d, g.reshape(1, d), shift, scale, *ws)


def _diff_attn_kernel(lam_ref, q_ref, k_ref, v_ref, g_ref, o_ref, qs_ref, s_ref, m_ref, acc_ref, *,
                      tk, n_sub, out_scale):
    tq = q_ref.shape[1]
    n_kv = k_ref.shape[1] // tk
    rb = 2 * tq // n_sub
    q = q_ref[0]
    lane = lax.broadcasted_iota(jnp.int32, q.shape, 1)
    zero = jnp.zeros_like(q)
    qs_ref[:tq] = jnp.where(lane < DA_DH, q, zero)
    qs_ref[tq:] = jnp.where(lane >= DA_DH, q, zero)
    m_ref[...] = jnp.full(m_ref.shape, -1e30, F32)
    acc_ref[...] = jnp.zeros(acc_ref.shape, F32)
    ones = jnp.ones((tk, LANES), BF16)

    def scores(i, slot):
        off = pl.multiple_of(i * tk, tk)
        k = k_ref[0, pl.ds(off, tk), :]
        s_ref[slot] = lax.dot_general(qs_ref[...], k, (((1,), (1,)), ((), ())), preferred_element_type=F32)

    def consume(i, slot):
        off = pl.multiple_of(i * tk, tk)
        v_ext = jnp.concatenate([v_ref[0, pl.ds(off, tk), :], ones], axis=1)
        for r in range(n_sub):
            rows = pl.ds(r * rb, rb)
            s = s_ref[slot, rows, :]
            m_prev = m_ref[rows, :]
            m_new = jnp.maximum(m_prev, jnp.max(s, axis=1, keepdims=True))
            alpha = jnp.exp2(m_prev - m_new)
            p = jnp.exp2(s - jnp.tile(m_new, (1, tk // LANES)))
            pv = jnp.dot(p.astype(BF16), v_ext, preferred_element_type=F32)
            acc_ref[rows, :] = acc_ref[rows, :] * jnp.tile(alpha, (1, 2)) + pv
            m_ref[rows, :] = m_new

    scores(0, 0)

    def body(j, carry):
        scores(2 * j + 1, 1)
        consume(2 * j, 0)
        scores(2 * j + 2, 0)
        consume(2 * j + 1, 1)
        return carry

    lax.fori_loop(0, (n_kv - 1) // 2, body, 0)
    consume(n_kv - 1, 0)
    acc = acc_ref[...]
    o1 = acc[:tq, :LANES] / acc[:tq, LANES:]
    o2 = acc[tq:, :LANES] / acc[tq:, LANES:]
    o = o1 - lam_ref[0] * o2
    ms = jnp.mean(o * o, axis=-1, keepdims=True)
    o = o * lax.rsqrt(ms + SUBLN_EPS) * g_ref[...] * out_scale
    o_ref[0] = o.astype(o_ref.dtype)


def _diff_attention(lam, q, k_all, v_all, subln_g, out_scale, tq=512, tk=1280, n_sub=2):
    b, n, _ = q.shape
    nk = k_all.shape[1]
    assert n % tq == 0 and nk % tk == 0 and tk % (2 * LANES) == 0 and (nk // tk) % 2 == 1
    grid_spec = pltpu.PrefetchScalarGridSpec(
        num_scalar_prefetch=1,
        grid=(b, DA_HEADS, n // tq),
        in_specs=[pl.BlockSpec((1, tq, DA_VD), lambda bi, hi, qi, lam_r: (bi, qi, hi)),
                  pl.BlockSpec((1, nk, DA_VD), lambda bi, hi, qi, lam_r: (bi, 0, hi)),
                  pl.BlockSpec((1, nk, DA_VD), lambda bi, hi, qi, lam_r: (bi, 0, hi)),
                  pl.BlockSpec((1, DA_VD), lambda bi, hi, qi, lam_r: (0, 0))],
        out_specs=pl.BlockSpec((1, tq, DA_VD), lambda bi, hi, qi, lam_r: (bi, qi, hi)),
        scratch_shapes=[pltpu.VMEM((2 * tq, LANES), BF16), pltpu.VMEM((2, 2 * tq, tk), F32),
                        pltpu.VMEM((2 * tq, LANES), F32), pltpu.VMEM((2 * tq, 2 * LANES), F32)],
    )
    return pl.pallas_call(
        functools.partial(_diff_attn_kernel, tk=tk, n_sub=n_sub, out_scale=out_scale),
        grid_spec=grid_spec,
        out_shape=jax.ShapeDtypeStruct((b, n, DA_W), BF16),
        compiler_params=pltpu.CompilerParams(vmem_limit_bytes=VMEM_LIMIT),
        name="diff_attention",
    )(lam.reshape(1), q, k_all, v_all, subln_g.reshape(1, DA_VD))


def _proj_residual_kernel(a_ref, w_ref, x_ref, gate_ref, g_ref, shift_ref, scale_ref, wr_ref, br_ref,
                          xo_ref, h_ref, lg_ref):
    y = jnp.dot(a_ref[...], w_ref[...], preferred_element_type=F32)
    xn = x_ref[...] + gate_ref[0] * y
    xo_ref[...] = xn
    ms = jnp.mean(xn * xn, axis=-1, keepdims=True)
    h = xn * lax.rsqrt(ms + NORM_EPS) * g_ref[...]
    h = (h * (1.0 + scale_ref[0]) + shift_ref[0]).astype(BF16)
    h_ref[...] = h
    lg_ref[...] = jnp.dot(h, wr_ref[...], preferred_element_type=F32) + br_ref[...]


def _proj_residual(a, w, x2d, gate, g, shift, scale, w_router, b_router, rows_per_batch, tm=512):
    t, d = x2d.shape
    kdim = a.shape[1]
    bpb = rows_per_batch // tm
    wr = jnp.zeros((d, LANES), BF16).at[:, :N_EXPERTS].set(w_router.astype(BF16))
    br = jnp.zeros((1, LANES), F32).at[0, :N_EXPERTS].set(b_router)
    vec = pl.BlockSpec((1, 1, d), lambda i: (i // bpb, 0, 0))
    row = lambda n: pl.BlockSpec((tm, n), lambda i: (i, 0))
    full = lambda s: pl.BlockSpec(s, lambda i: (0, 0))
    return pl.pallas_call(
        _proj_residual_kernel,
        grid=(t // tm,),
        in_specs=[row(kdim), full((kdim, d)), row(d), vec, full((1, d)), vec, vec,
                  full((d, LANES)), full((1, LANES))],
        out_specs=[row(d), row(d), row(LANES)],
        out_shape=[jax.ShapeDtypeStruct((t, d), F32), jax.ShapeDtypeStruct((t, d), BF16),
                   jax.ShapeDtypeStruct((t, LANES), F32)],
        compiler_params=pltpu.CompilerParams(vmem_limit_bytes=VMEM_LIMIT),
        name="proj_residual",
    )(a, w, x2d, gate, g.reshape(1, d), shift, scale, wr, br)


MXU_DIM = 256


def _deinterleave_kernel(w_ref, perm_ref, g_ref, l_ref):
    w = w_ref[0, 0].astype(BF16)
    for c in range(w.shape[1] // MXU_DIM):
        blk = jnp.dot(w[:, c * MXU_DIM:(c + 1) * MXU_DIM], perm_ref[...], preferred_element_type=F32)
        g_ref[0, :, c * LANES:(c + 1) * LANES] = blk[:, :LANES].astype(BF16)
        l_ref[0, :, c * LANES:(c + 1) * LANES] = blk[:, LANES:].astype(BF16)


def _deinterleave_w1(w1_all, layer, tm=512):
    _, e, d, f2 = w1_all.shape
    src = jnp.arange(MXU_DIM)[:, None]
    dst = jnp.arange(MXU_DIM)[None, :]
    perm = (src == jnp.where(dst < LANES, 2 * dst, 2 * (dst - LANES) + 1)).astype(BF16)
    out = jax.ShapeDtypeStruct((e, d, f2 // 2), BF16)
    return pl.pallas_call(
        _deinterleave_kernel,
        grid=(e, d // tm),
        in_specs=[pl.BlockSpec((1, 1, tm, f2), lambda i, j: (layer, i, j, 0)),
                  pl.BlockSpec((MXU_DIM, MXU_DIM), lambda i, j: (0, 0))],
        out_specs=[pl.BlockSpec((1, tm, f2 // 2), lambda i, j: (i, j, 0))] * 2,
        out_shape=[out, out],
        compiler_params=pltpu.CompilerParams(vmem_limit_bytes=VMEM_LIMIT),
        name="deinterleave_w1",
    )(w1_all, perm)


def _moe_kernel(be_ref, x_ref, w1g_ref, w1l_ref, b1g_ref, b1l_ref, w2_ref, b2_ref, o_ref):
    x = x_ref[...]
    ug = jnp.dot(x, w1g_ref[0], preferred_element_type=F32) + b1g_ref[0]
    ul = jnp.dot(x, w1l_ref[0], preferred_element_type=F32) + b1l_ref[0]
    glu = jnp.minimum(ug, SWIGLU_LIMIT)
    lin = jnp.clip(ul, -SWIGLU_LIMIT, SWIGLU_LIMIT)
    act = glu * jax.nn.sigmoid(SWIGLU_ALPHA * glu) * (lin + 1.0)
    y = jnp.dot(act.astype(BF16), w2_ref[0, 0].astype(BF16), preferred_element_type=F32) + b2_ref[0]
    o_ref[...] = y.astype(o_ref.dtype)


def _moe_experts(block_expert, xb, w1g, w1l, b1g, b1l, w2_all, layer, b2):
    n_rows, d = xb.shape
    n_blocks = n_rows // MOE_BLOCK
    wspec = lambda s: pl.BlockSpec((1,) + s, lambda i, be: (be[i], 0, 0))
    grid_spec = pltpu.PrefetchScalarGridSpec(
        num_scalar_prefetch=1,
        grid=(n_blocks,),
        in_specs=[pl.BlockSpec((MOE_BLOCK, d), lambda i, be: (i, 0)),
                  wspec((d, D_FF)), wspec((d, D_FF)), wspec((1, D_FF)), wspec((1, D_FF)),
                  pl.BlockSpec((1, 1, D_FF, d), lambda i, be: (layer, be[i], 0, 0)), wspec((1, d))],
        out_specs=pl.BlockSpec((MOE_BLOCK, d), lambda i, be: (i, 0)),
    )
    return pl.pallas_call(
        _moe_kernel,
        grid_spec=grid_spec,
        out_shape=jax.ShapeDtypeStruct((n_rows, d), BF16),
        compiler_params=pltpu.CompilerParams(vmem_limit_bytes=VMEM_LIMIT),
        name="moe_experts",
    )(block_expert, xb, w1g, w1l, b1g, b1l, w2_all, b2)


def _moe_ffn(h, logits, w1_all, b1, w2_all, b2, layer):
    n_tok, d = h.shape
    top_val, top_idx = lax.top_k(logits, TOP_K)
    gates = jax.nn.softmax(top_val, axis=-1)
    n_assign = n_tok * TOP_K
    e_flat = top_idx.reshape(n_assign).astype(jnp.int32)
    order = jnp.argsort(e_flat).astype(jnp.int32)
    rank = jnp.argsort(order).astype(jnp.int32)
    counts = jnp.sum(e_flat[:, None] == jnp.arange(N_EXPERTS, dtype=jnp.int32)[None, :], axis=0, dtype=jnp.int32)
    starts = jnp.cumsum(counts) - counts
    padded = (counts + MOE_BLOCK - 1) // MOE_BLOCK * MOE_BLOCK
    pad_ends = jnp.cumsum(padded)
    pad_starts = pad_ends - padded
    n_blocks = -(-(n_assign + N_EXPERTS * (MOE_BLOCK - 1)) // MOE_BLOCK)
    n_rows = n_blocks * MOE_BLOCK
    block_start = jnp.arange(n_blocks, dtype=jnp.int32) * MOE_BLOCK
    block_expert = jnp.minimum(jnp.searchsorted(pad_ends, block_start, side='right'),
                               N_EXPERTS - 1).astype(jnp.int32)
    shift = pad_starts - starts
    pos = (shift[e_flat] + rank).reshape(n_tok, TOP_K)
    r = jnp.arange(n_rows, dtype=jnp.int32)
    src = jnp.clip(r - jnp.repeat(shift[block_expert], MOE_BLOCK), 0, n_assign - 1)
    row_tok = order[src] // TOP_K
    xb = h[row_tok]
    w1g, w1l = _deinterleave_w1(w1_all, layer)
    yb = _moe_experts(block_expert, xb, w1g, w1l, b1[:, None, 0::2], b1[:, None, 1::2], w2_all, layer,
                      b2[:, None, :])
    return yb, pos, gates


def _combine_kernel(*refs, final):
    y_refs = refs[:TOP_K]
    gt_ref, x_ref, g5_ref = refs[TOP_K:TOP_K + 3]
    o_ref = refs[-1]
    gt = gt_ref[...]
    acc = y_refs[0][...].astype(F32) * gt[:, 0:1]
    for j in range(1, TOP_K):
        acc = acc + y_refs[j][...].astype(F32) * gt[:, j:j + 1]
    xn = x_ref[...] + g5_ref[0] * acc
    if final:
        fg_ref = refs[TOP_K + 3]
        xn = xn * lax.rsqrt(jnp.mean(xn * xn, axis=-1, keepdims=True) + NORM_EPS) * fg_ref[...]
    o_ref[...] = xn


def _moe_combine(yb, pos, gates, x2d, gate5, rows_per_batch, final_g=None, tm=512):
    t, d = x2d.shape
    bpb = rows_per_batch // tm
    ys = [yb[pos[:t, j]] for j in range(TOP_K)]
    gt = jnp.zeros((t, LANES), F32).at[:, :TOP_K].set(gates[:t])
    row = lambda n: pl.BlockSpec((tm, n), lambda i: (i, 0))
    in_specs = [row(d)] * TOP_K + [row(LANES), row(d), pl.BlockSpec((1, 1, d), lambda i: (i // bpb, 0, 0))]
    args = ys + [gt, x2d, gate5]
    if final_g is not None:
        in_specs.append(pl.BlockSpec((1, d), lambda i: (0, 0)))
        args.append(final_g.reshape(1, d))
    return pl.pallas_call(
        functools.partial(_combine_kernel, final=final_g is not None),
        grid=(t // tm,),
        in_specs=in_specs,
        out_specs=row(d),
        out_shape=jax.ShapeDtypeStruct((t, d), F32),
        compiler_params=pltpu.CompilerParams(vmem_limit_bytes=VMEM_LIMIT),
        name="moe_combine",
    )(*args)


GDN_CONV = 5
HALO = 8


def _gdn_prep_kernel(pm_ref, pp_ref, pn_ref, ab_ref, cw_ref, alog_ref, dtb_ref, q_ref, k_ref, v_ref, gb_ref):
    i = pl.program_id(1)
    n_i = pl.num_programs(1)
    tm = pm_ref.shape[1]
    half = GDN_CONV // 2
    keep_prev = (i > 0).astype(F32)
    keep_next = (i < n_i - 1).astype(F32)
    for cg in range(GDN_QKV_W // LANES):
        cols = slice(cg * LANES, (cg + 1) * LANES)
        main = pm_ref[0, :, cols]
        ext = jnp.concatenate([pp_ref[0, :, cols] * keep_prev, main, pn_ref[0, :, cols] * keep_next], axis=0)
        acc = main * cw_ref[half:half + 1, cols]
        for j in range(GDN_CONV):
            if j != half:
                sh = pltpu.roll(ext, (half - j) % (tm + 2 * HALO), 0)[HALO:HALO + tm]
                acc = acc + sh * cw_ref[j:j + 1, cols]
        y = acc * jax.nn.sigmoid(acc)
        if cg < 2 * GDN_HK:
            y = y * lax.rsqrt(jnp.sum(y * y, axis=-1, keepdims=True) + 1e-6)
            if cg < GDN_HK:
                q_ref[0, :, cols] = y * (GDN_DK ** -0.5)
            else:
                k_ref[0, :, (cg - GDN_HK) * LANES:(cg - GDN_HK + 1) * LANES] = y
        else:
            v_ref[0, :, (cg - 2 * GDN_HK) * LANES:(cg - 2 * GDN_HK + 1) * LANES] = y
    ab = ab_ref[0]
    xa = ab + dtb_ref[...]
    softplus = jnp.maximum(xa, 0.0) + jnp.log1p(jnp.exp(-jnp.abs(xa)))
    lane = lax.broadcasted_iota(jnp.int32, ab.shape, 1)
    gb_ref[0] = jnp.where(lane < 2 * GDN_HV, -jnp.exp(alog_ref[...]) * softplus, jax.nn.sigmoid(ab))


def _gdn_prep(p_main, p_ab, conv_w, a_log, dt_bias, tm=256):
    b, t, _ = p_main.shape
    tm = min(tm, t)
    nb = tm // HALO
    last = t // HALO - 1
    pad = lambda a: jnp.zeros((1, LANES), F32).at[0, :2 * GDN_HV].set(a.reshape(-1))
    f = lambda n: jax.ShapeDtypeStruct((b, t, n), F32)
    return pl.pallas_call(
        _gdn_prep_kernel,
        grid=(b, t // tm),
        in_specs=[pl.BlockSpec((1, tm, GDN_QKV_W), lambda bi, i: (bi, i, 0)),
                  pl.BlockSpec((1, HALO, GDN_QKV_W), lambda bi, i: (bi, jnp.maximum(i * nb - 1, 0), 0)),
                  pl.BlockSpec((1, HALO, GDN_QKV_W), lambda bi, i: (bi, jnp.minimum((i + 1) * nb, last), 0)),
                  pl.BlockSpec((1, tm, LANES), lambda bi, i: (bi, i, 0)),
                  pl.BlockSpec((GDN_CONV, GDN_QKV_W), lambda bi, i: (0, 0)),
                  pl.BlockSpec((1, LANES), lambda bi, i: (0, 0)),
                  pl.BlockSpec((1, LANES), lambda bi, i: (0, 0))],
        out_specs=[pl.BlockSpec((1, tm, GDN_QK_W), lambda bi, i: (bi, i, 0)),
                   pl.BlockSpec((1, tm, GDN_QK_W), lambda bi, i: (bi, i, 0)),
                   pl.BlockSpec((1, tm, GDN_V_W), lambda bi, i: (bi, i, 0)),
                   pl.BlockSpec((1, tm, LANES), lambda bi, i: (bi, i, 0))],
        out_shape=[f(GDN_QK_W), f(GDN_QK_W), f(GDN_V_W), f(LANES)],
        compiler_params=pltpu.CompilerParams(vmem_limit_bytes=VMEM_LIMIT),
        name="gdn_prep",
    )(p_main, p_main, p_main, p_ab, conv_w, pad(a_log), pad(dt_bias))


def _gdn_chunk_kernel(q_ref, k_ref, v_ref, gb_ref, s0_ref, o_ref, sfin_ref, s_ref, *, reverse, g_lane, b_lane):
    c = pl.program_id(1)
    n_c = pl.num_programs(1)
    cs = GDN_CHUNK
    rep = GDN_HV // GDN_HK

    @pl.when(c == 0)
    def _():
        s_ref[...] = s0_ref[0]

    row = lax.broadcasted_iota(jnp.int32, (cs, cs), 0)
    col = lax.broadcasted_iota(jnp.int32, (cs, cs), 1)
    incl = (row <= col) if reverse else (row >= col)
    strict = (row < col) if reverse else (row > col)
    eye = (row == col).astype(F32)
    incl_b = incl.astype(BF16)
    last = 0 if reverse else cs - 1

    gb = gb_ref[0]
    gb_hi = gb.astype(BF16)
    gb_lo = (gb - gb_hi.astype(F32)).astype(BF16)
    g_col = (jnp.dot(incl_b, gb_hi, preferred_element_type=F32)
             + jnp.dot(incl_b, gb_lo, preferred_element_type=F32))
    tn = (((0,), (1,)), ((), ()))
    g_row = (lax.dot_general(gb_hi, incl_b, tn, preferred_element_type=F32)
             + lax.dot_general(gb_lo, incl_b, tn, preferred_element_type=F32))

    heads = range(GDN_HV)
    gc = jnp.stack([g_col[:, g_lane + h:g_lane + h + 1] for h in heads])
    gr = jnp.stack([g_row[g_lane + h:g_lane + h + 1, :] for h in heads])
    bc = jnp.stack([gb[:, b_lane + h:b_lane + h + 1] for h in heads])
    ge = gc[:, last:last + 1, :]
    qh = jnp.stack([q_ref[0, :, h * GDN_DK:(h + 1) * GDN_DK] for h in range(GDN_HK)])
    kh = jnp.stack([k_ref[0, :, h * GDN_DK:(h + 1) * GDN_DK] for h in range(GDN_HK)])
    vh = jnp.stack([v_ref[0, :, h * GDN_DV:(h + 1) * GDN_DV] for h in heads])
    kh_b = kh.astype(BF16)
    bnt = (((2,), (2,)), ((0,), (0,)))
    kk = lax.dot_general(kh_b, kh_b, bnt, preferred_element_type=F32)
    qk = lax.dot_general(qh.astype(BF16), kh_b, bnt, preferred_element_type=F32)
    kk, qk = jnp.repeat(kk, rep, axis=0), jnp.repeat(qk, rep, axis=0)
    qv, kv = jnp.repeat(qh, rep, axis=0), jnp.repeat(kh, rep, axis=0)

    decay = jnp.where(incl, jnp.exp(jnp.where(incl, gc - gr, 0.0)), 0.0)
    lm = jnp.where(strict, bc * kk * decay, 0.0)
    bmm = lambda a, b: jnp.einsum('hij,hjk->hik', a.astype(BF16), b.astype(BF16), preferred_element_type=F32)
    p = bmm(lm, lm)
    x = eye - lm
    for _ in range(4):
        xp = bmm(jnp.concatenate([x, p], axis=1), p)
        x = x + xp[:, :cs]
        p = xp[:, cs:]
    x = x + bmm(x, p)
    eg = jnp.exp(gc)
    uw = bmm(x, jnp.concatenate([vh * bc, kv * (bc * eg)], axis=2))
    u = uw[:, :, :GDN_DV]
    w = uw[:, :, GDN_DV:]
    qg = qv * eg
    intra = jnp.where(incl, qk * decay, 0.0)
    kt = kv * jnp.exp(ge - gc)
    s = s_ref[...]
    wq = bmm(jnp.concatenate([w, qg], axis=1), s)
    v_new = u - wq[:, :cs]
    o = wq[:, cs:] + bmm(intra, v_new)
    for h in heads:
        o_ref[0, :, h * GDN_DV:(h + 1) * GDN_DV] = o[h]
    s_ref[...] = s * jnp.exp(ge) + jnp.einsum('hck,hcv->hkv', kt.astype(BF16), v_new.astype(BF16),
                                              preferred_element_type=F32)

    @pl.when(c == n_c - 1)
    def _():
        sfin_ref[0] = s_ref[...]


def _gdn_scan(q, k, v, gb, s0, reverse):
    b, t, _ = q.shape
    n_c = t // GDN_CHUNK
    cm = (lambda bi, ci: (bi, n_c - 1 - ci, 0)) if reverse else (lambda bi, ci: (bi, ci, 0))
    d = 1 if reverse else 0
    smap = lambda bi, ci: (bi, 0, 0, 0)
    return pl.pallas_call(
        functools.partial(_gdn_chunk_kernel, reverse=reverse, g_lane=d * GDN_HV, b_lane=(2 + d) * GDN_HV),
        grid=(b, n_c),
        in_specs=[pl.BlockSpec((1, GDN_CHUNK, GDN_QK_W), cm), pl.BlockSpec((1, GDN_CHUNK, GDN_QK_W), cm),
                  pl.BlockSpec((1, GDN_CHUNK, GDN_V_W), cm), pl.BlockSpec((1, GDN_CHUNK, LANES), cm),
                  pl.BlockSpec((1, GDN_HV, GDN_DK, GDN_DV), smap)],
        out_specs=[pl.BlockSpec((1, GDN_CHUNK, GDN_V_W), cm), pl.BlockSpec((1, GDN_HV, GDN_DK, GDN_DV), smap)],
        out_shape=[jax.ShapeDtypeStruct((b, t, GDN_V_W), F32),
                   jax.ShapeDtypeStruct((b, GDN_HV, GDN_DK, GDN_DV), F32)],
        scratch_shapes=[pltpu.VMEM((GDN_HV, GDN_DK, GDN_DV), F32)],
        compiler_params=pltpu.CompilerParams(vmem_limit_bytes=VMEM_LIMIT),
        name="gdn_scan_bwd" if reverse else "gdn_scan_fwd",
    )(q, k, v, gb, s0)


def _gdn_out_kernel(of_ref, ob_ref, z_ref, ng_ref, w_ref, x_ref, gate_ref, g_ref, shift_ref, scale_ref,
                    wr_ref, br_ref, xo_ref, h_ref, lg_ref):
    parts = []
    for h in range(GDN_HV):
        cols = slice(h * GDN_DV, (h + 1) * GDN_DV)
        o = of_ref[:, cols] + ob_ref[:, cols]
        z = z_ref[:, cols]
        o = o * lax.rsqrt(jnp.mean(o * o, axis=-1, keepdims=True) + NORM_EPS) * ng_ref[...]
        parts.append((o * (z * jax.nn.sigmoid(z))).astype(BF16))
    y = jnp.dot(jnp.concatenate(parts, axis=1), w_ref[...], preferred_element_type=F32)
    xn = x_ref[...] + gate_ref[0] * y
    xo_ref[...] = xn
    ms = jnp.mean(xn * xn, axis=-1, keepdims=True)
    hh = xn * lax.rsqrt(ms + NORM_EPS) * g_ref[...]
    hh = (hh * (1.0 + scale_ref[0]) + shift_ref[0]).astype(BF16)
    h_ref[...] = hh
    lg_ref[...] = jnp.dot(hh, wr_ref[...], preferred_element_type=F32) + br_ref[...]


def _gdn_out_proj(o_f, o_b, p_main, norm_g, w, x2d, gate, g, shift, scale, w_router, b_router, rows_per_batch,
                  tm=256):
    t, d = x2d.shape
    bpb = rows_per_batch // tm
    wr = jnp.zeros((d, LANES), BF16).at[:, :N_EXPERTS].set(w_router.astype(BF16))
    br = jnp.zeros((1, LANES), F32).at[0, :N_EXPERTS].set(b_router)
    vec = pl.BlockSpec((1, 1, d), lambda i: (i // bpb, 0, 0))
    row = lambda n: pl.BlockSpec((tm, n), lambda i: (i, 0))
    full = lambda s: pl.BlockSpec(s, lambda i: (0, 0))
    return pl.pallas_call(
        _gdn_out_kernel,
        grid=(t // tm,),
        in_specs=[row(GDN_V_W), row(GDN_V_W), pl.BlockSpec((tm, GDN_V_W), lambda i: (i, GDN_QKV_W // GDN_V_W)),
                  full((1, GDN_DV)), full((GDN_V_W, d)), row(d), vec, full((1, d)), vec, vec,
                  full((d, LANES)), full((1, LANES))],
        out_specs=[row(d), row(d), row(LANES)],
        out_shape=[jax.ShapeDtypeStruct((t, d), F32), jax.ShapeDtypeStruct((t, d), BF16),
                   jax.ShapeDtypeStruct((t, LANES), F32)],
        compiler_params=pltpu.CompilerParams(vmem_limit_bytes=VMEM_LIMIT),
        name="gdn_out_proj",
    )(o_f, o_b, p_main, norm_g.reshape(1, GDN_DV), w, x2d, gate, g.reshape(1, d), shift, scale, wr, br)


def _even_layer(x2d, xc, mod, mod_c, norm1_g, norm2_g, w_in, w_out, lam_p, subln_g, lam_init, cos, sin,
                w_router, b_router, bsz, n_lat):
    d = x2d.shape[1]
    w_in_b = w_in.astype(BF16)
    w_out_b = w_out.astype(BF16)
    (p,) = _norm_mod_matmul(x2d, norm1_g, mod[0], mod[1], [w_in_b], n_lat)
    f, q, k, v = _even_split(p.reshape(bsz, n_lat, -1))
    hc = _modulate(_rms_norm(xc, norm1_g), mod_c[0], mod_c[1])
    fc, qc, kc, vc = _even_split(hc @ w_in)
    q = _rope2d(q, cos, sin) * (DA_DH ** -0.5 * math.log2(math.e))
    k = _rope2d(k, cos, sin)
    lp = lam_p.astype(jnp.float32)
    lam = jnp.exp(jnp.sum(lp[0] * lp[1])) - jnp.exp(jnp.sum(lp[2] * lp[3])) + lam_init
    n_ctx = xc.shape[1]
    k_all = jnp.concatenate([k, kc], axis=1).astype(BF16).reshape(bsz, n_lat + n_ctx, DA_W)
    v_all = jnp.concatenate([v, vc], axis=1).astype(BF16).reshape(bsz, n_lat + n_ctx, DA_W)
    o = _diff_attention(lam, q.astype(BF16).reshape(bsz, n_lat, DA_W), k_all, v_all, subln_g, 1.0 - lam_init)
    fm = _fourier_mix(f).reshape(bsz, n_lat, FOURIER_W).astype(BF16)
    mixed = jnp.concatenate([fm, o], axis=-1).reshape(bsz * n_lat, -1)
    x2d, h2, logits = _proj_residual(mixed, w_out_b, x2d, mod[2], norm2_g, mod[3], mod[4],
                                     w_router, b_router, n_lat)
    oc = _diff_attend(qc, kc, vc, lam)
    oc = _rms_norm(oc, subln_g, SUBLN_EPS) * (1.0 - lam_init)
    mixed_c = jnp.concatenate([_fourier_mix(fc).reshape(bsz, n_ctx, FOURIER_W), oc.reshape(bsz, n_ctx, DA_W)], axis=-1)
    xc = xc + mod_c[2] * (mixed_c @ w_out)
    return x2d, h2, logits, xc


def _odd_layer(x2d, xc, mod, mod_c, norm1_g, norm2_g, w_in, conv_w, a_log, dt_bias, norm_g, w_out,
               w_router, b_router, bsz, n_lat):
    d = x2d.shape[1]
    n_main = GDN_QKV_W + GDN_V_W
    w_main = w_in[:, :n_main].astype(BF16)
    w_ab = jnp.zeros((d, LANES), BF16).at[:, :4 * GDN_HV].set(w_in[:, n_main:].astype(BF16))
    n_ctx = xc.shape[1]
    p_main, p_ab = _norm_mod_matmul(x2d, norm1_g, mod[0], mod[1], [w_main, w_ab], n_lat)
    q, k, v, gb = _gdn_prep(p_main.reshape(bsz, n_lat, n_main), p_ab.reshape(bsz, n_lat, LANES),
                            conv_w, a_log, dt_bias)
    pc_main, pc_ab = _norm_mod_matmul(xc.reshape(bsz * n_ctx, d), norm1_g, mod_c[0], mod_c[1], [w_main, w_ab],
                                      bsz * n_ctx, tm=n_ctx)
    qc, kc, vc, gbc = _gdn_prep(pc_main.reshape(bsz, n_ctx, n_main), pc_ab.reshape(bsz, n_ctx, LANES),
                                conv_w, a_log, dt_bias)
    s0 = jnp.zeros((bsz, GDN_HV, GDN_DK, GDN_DV), jnp.float32)
    _, sc_f = _gdn_scan(qc, kc, vc, gbc, s0, False)
    o_f, _ = _gdn_scan(q, k, v, gb, sc_f, False)
    _, sc_b = _gdn_scan(qc, kc, vc, gbc, s0, True)
    o_b, _ = _gdn_scan(q, k, v, gb, sc_b, True)
    return _gdn_out_proj(o_f.reshape(bsz * n_lat, GDN_V_W), o_b.reshape(bsz * n_lat, GDN_V_W), p_main, norm_g,
                         w_out.astype(BF16), x2d, mod[2], norm2_g, mod[3], mod[4], w_router, b_router, n_lat)


def kernel(x, c, ctx, c_ctx, norm1_g, norm2_g, w_mod, b_mod, ev_w_in, ev_w_out, ev_lam, ev_subln_g,
           od_w_in, od_conv_w, od_a_log, od_dt_bias, od_norm_g, od_w_out,
           moe_w_router, moe_b_router, moe_w1, moe_b1, moe_w2, moe_b2, final_g):
    bsz, n_lat, d = x.shape
    n_ctx = ctx.shape[1]
    depth = w_mod.shape[0]
    assert depth == 2, "kernel is written for one even (attention) and one odd (DeltaNet) layer"
    cos, sin = _axial_rope_tables(n_lat // GRID_W)
    s_lat = jax.nn.silu(c)
    s_ctx = jax.nn.silu(c_ctx)
    x2d = x.reshape(bsz * n_lat, d)
    xc = ctx

    mod = jnp.split((s_lat @ w_mod[0] + b_mod[0])[:, None, :], N_MOD, axis=-1)
    mod_c = jnp.split((s_ctx @ w_mod[0] + b_mod[0])[None, None, :], N_MOD, axis=-1)
    x2d, h2, logits, xc = _even_layer(x2d, xc, mod, mod_c, norm1_g[0], norm2_g[0], ev_w_in[0], ev_w_out[0],
                                      ev_lam[0], ev_subln_g[0], _diff_lambda_init(0), cos, sin,
                                      moe_w_router[0], moe_b_router[0], bsz, n_lat)
    h2c = _modulate(_rms_norm(xc, norm2_g[0]), mod_c[3], mod_c[4]).reshape(bsz * n_ctx, d)
    logits_c = (h2c @ moe_w_router[0] + moe_b_router[0]).astype(jnp.float32)
    yb, pos, gates = _moe_ffn(jnp.concatenate([h2, h2c.astype(BF16)], axis=0),
                              jnp.concatenate([logits[:, :N_EXPERTS], logits_c], axis=0),
                              moe_w1, moe_b1[0], moe_w2, moe_b2[0], 0)
    x2d = _moe_combine(yb, pos, gates, x2d, mod[5], n_lat)
    out_c = jnp.sum(yb[pos[bsz * n_lat:]].astype(F32) * gates[bsz * n_lat:, :, None], axis=1)
    xc = xc + mod_c[5] * out_c.reshape(bsz, n_ctx, d)

    mod = jnp.split((s_lat @ w_mod[1] + b_mod[1])[:, None, :], N_MOD, axis=-1)
    mod_c = jnp.split((s_ctx @ w_mod[1] + b_mod[1])[None, None, :], N_MOD, axis=-1)
    x2d, h2, logits = _odd_layer(x2d, xc, mod, mod_c, norm1_g[1], norm2_g[1], od_w_in[0], od_conv_w[0],
                                 od_a_log[0], od_dt_bias[0], od_norm_g[0], od_w_out[0],
                                 moe_w_router[1], moe_b_router[1], bsz, n_lat)
    yb, pos, gates = _moe_ffn(h2, logits[:, :N_EXPERTS], moe_w1, moe_b1[1], moe_w2, moe_b2[1], 1)
    return _moe_combine(yb, pos, gates, x2d, mod[5], n_lat, final_g=final_g).reshape(bsz, n_lat, d)
```

```python
import functools
import math

import jax
import jax.numpy as jnp
from jax import lax
from jax.experimental import pallas as pl
from jax.experimental.pallas import tpu as pltpu

D_MODEL = 1024
N_MOD = 6
NORM_EPS = 1e-6
GRID_W = 64

FOURIER_GROUPS = 4
FOURIER_GD = 64
FOURIER_W = FOURIER_GROUPS * FOURIER_GD
DA_HEADS = 6
DA_DH = 64
DA_VD = 2 * DA_DH
DA_W = DA_HEADS * DA_VD
ROPE_BASE = 10000.0
ROPE_AXIS_DIM = DA_DH // 2
SUBLN_EPS = 1e-5

GDN_HK = 8
GDN_HV = 16
GDN_DK = 128
GDN_DV = 128
GDN_QK_W = GDN_HK * GDN_DK
GDN_V_W = GDN_HV * GDN_DV
GDN_QKV_W = 2 * GDN_QK_W + GDN_V_W
GDN_CHUNK = 64

N_EXPERTS = 32
TOP_K = 4
D_FF = 1024
SWIGLU_LIMIT = 7.0
SWIGLU_ALPHA = 1.702
MOE_BLOCK = 512

LANES = 128
VMEM_LIMIT = 56 * 1024 * 1024
BF16 = jnp.bfloat16
F32 = jnp.float32


def _rms_norm(x, g, eps=NORM_EPS):
    xf = x.astype(jnp.float32)
    y = xf * lax.rsqrt(jnp.mean(xf * xf, axis=-1, keepdims=True) + eps)
    return y.astype(x.dtype) * g


def _modulate(h, shift, scale):
    return h * (1 + scale) + shift


def _l2_normalize(x, eps=1e-6):
    return x * lax.rsqrt(jnp.sum(x * x, axis=-1, keepdims=True) + eps)


def _diff_lambda_init(layer_idx):
    return 0.8 - 0.6 * math.exp(-0.3 * layer_idx)


def _axial_rope_tables(rows):
    t = jnp.arange(rows * GRID_W, dtype=jnp.int32)
    row = (t // GRID_W).astype(jnp.float32)
    col = (t % GRID_W).astype(jnp.float32)
    inv = ROPE_BASE ** (-jnp.arange(0, ROPE_AXIS_DIM, 2, dtype=jnp.float32) / ROPE_AXIS_DIM)
    ang = jnp.concatenate([row[:, None] * inv, col[:, None] * inv], axis=-1)
    return jnp.cos(ang), jnp.sin(ang)


def _rope2d(x, cos, sin):
    half = x.shape[-1] // 2
    xf = x.astype(jnp.float32)
    x1, x2 = xf[..., :half], xf[..., half:]
    cb = cos[None, :, None, None, :]
    sb = sin[None, :, None, None, :]
    return jnp.concatenate([x1 * cb - x2 * sb, x2 * cb + x1 * sb], axis=-1).astype(x.dtype)


def _fourier_mix(f):
    ff = jnp.moveaxis(f.astype(jnp.float32), 2, 1)
    out = jnp.real(jnp.fft.fft2(ff, norm='ortho'))
    return jnp.moveaxis(out, 1, 2).astype(f.dtype)


def _diff_attend(q, k, v, lam):
    s = jnp.einsum('bqhmd,bkhmd->bhmqk', q, k, preferred_element_type=jnp.float32) * (DA_DH ** -0.5)
    p = jax.nn.softmax(s, axis=-1)
    a = (p[:, :, 0] - lam * p[:, :, 1]).astype(v.dtype)
    return jnp.einsum('bhqk,bkhe->bqhe', a, v)


def _even_split(p):
    b, n, _ = p.shape
    f = p[..., :FOURIER_W].reshape(b, n, FOURIER_GROUPS, FOURIER_GD)
    q, k, v = jnp.split(p[..., FOURIER_W:], 3, axis=-1)
    q = q.reshape(b, n, DA_HEADS, 2, DA_DH)
    k = k.reshape(b, n, DA_HEADS, 2, DA_DH)
    v = v.reshape(b, n, DA_HEADS, DA_VD)
    return f, q, k, v


def _centred_depthwise_conv(x, w):
    width = w.shape[0]
    return lax.conv_general_dilated(
        x, w[:, None, :].astype(x.dtype), window_strides=(1,),
        padding=[((width - 1) // 2, width // 2)],
        dimension_numbers=('NWC', 'WIO', 'NWC'),
        feature_group_count=x.shape[-1])


def _chunk_gated_delta(q, k, v, g, beta, s0):
    b, t, nh, _ = q.shape
    n = t // GDN_CHUNK

    def chunks(a):
        a = a.reshape((b, n, GDN_CHUNK, nh) + a.shape[3:])
        return jnp.moveaxis(a, (1, 3), (0, 2))

    qc, kc, vc, gc, bc = chunks(q), chunks(k), chunks(v), chunks(g), chunks(beta)
    G = jnp.cumsum(gc, axis=-1)
    idx = jnp.arange(GDN_CHUNK)
    incl = idx[:, None] >= idx[None, :]
    strict = idx[:, None] > idx[None, :]
    diff = G[..., :, None] - G[..., None, :]
    decay = jnp.where(incl, jnp.exp(jnp.where(incl, diff, 0.0)), 0.0)
    kb = kc * bc[..., None]
    lmat = jnp.where(strict, jnp.einsum('...id,...jd->...ij', kb, kc) * decay, 0.0)
    eye = jnp.broadcast_to(jnp.eye(GDN_CHUNK, dtype=jnp.float32), lmat.shape)
    tmat = lax.linalg.triangular_solve(lmat + eye, eye, left_side=True, lower=True)
    u = tmat @ (vc * bc[..., None])
    w = tmat @ (kb * jnp.exp(G)[..., None])
    qg = qc * jnp.exp(G)[..., None]
    intra = jnp.where(incl, jnp.einsum('...id,...jd->...ij', qc, kc) * decay, 0.0)
    g_end = G[..., -1]
    k_tail = kc * jnp.exp(g_end[..., None] - G)[..., None]

    def step(state, xs):
        qg_i, u_i, w_i, intra_i, kt_i, ge_i = xs
        v_new = u_i - w_i @ state
        o_i = qg_i @ state + intra_i @ v_new
        state = state * jnp.exp(ge_i)[..., None, None] + jnp.swapaxes(kt_i, -1, -2) @ v_new
        return state, o_i

    s_fin, o = lax.scan(step, s0, (qg, u, w, intra, k_tail, g_end))
    o = jnp.moveaxis(o, (0, 2), (1, 3)).reshape(b, t, nh, -1)
    return o, s_fin


def _gdn_prepare(p_qkv, p_z, p_ab, conv_w, a_log, dt_bias):
    b, n, _ = p_qkv.shape
    qkv = jax.nn.silu(_centred_depthwise_conv(p_qkv, conv_w)).astype(jnp.float32)
    q = qkv[..., :GDN_QK_W].reshape(b, n, GDN_HK, GDN_DK)
    k = qkv[..., GDN_QK_W:2 * GDN_QK_W].reshape(b, n, GDN_HK, GDN_DK)
    v = qkv[..., 2 * GDN_QK_W:].reshape(b, n, GDN_HV, GDN_DV)
    rep = GDN_HV // GDN_HK
    q = jnp.repeat(_l2_normalize(q) * (GDN_DK ** -0.5), rep, axis=2)
    k = jnp.repeat(_l2_normalize(k), rep, axis=2)
    z = p_z.reshape(b, n, GDN_HV, GDN_DV)
    ab = p_ab.astype(jnp.float32).reshape(b, n, 2, 2, GDN_HV)
    a, bb = ab[:, :, 0], ab[:, :, 1]
    g = -jnp.exp(a_log.astype(jnp.float32)) * jax.nn.softplus(a + dt_bias.astype(jnp.float32))
    beta = jax.nn.sigmoid(bb)
    return q, k, v, z, g, beta


def _norm_mod_matmul_kernel(x_ref, g_ref, shift_ref, scale_ref, *refs, n_w):
    x = x_ref[...]
    ms = jnp.mean(x * x, axis=-1, keepdims=True)
    h = x * lax.rsqrt(ms + NORM_EPS) * g_ref[...]
    h = (h * (1.0 + scale_ref[0]) + shift_ref[0]).astype(BF16)
    for w_ref, o_ref in zip(refs[:n_w], refs[n_w:]):
        o_ref[...] = jnp.dot(h, w_ref[...], preferred_element_type=F32).astype(o_ref.dtype)


def _norm_mod_matmul(x2d, g, shift, scale, ws, rows_per_batch, tm=512):
    t, d = x2d.shape
    bpb = rows_per_batch // tm
    in_specs = [pl.BlockSpec((tm, d), lambda i: (i, 0)),
                pl.BlockSpec((1, d), lambda i: (0, 0)),
                pl.BlockSpec((1, 1, d), lambda i: (i // bpb, 0, 0)),
                pl.BlockSpec((1, 1, d), lambda i: (i // bpb, 0, 0))]
    in_specs += [pl.BlockSpec(w.shape, lambda i: (0, 0)) for w in ws]
    out_specs = [pl.BlockSpec((tm, w.shape[1]), lambda i: (i, 0)) for w in ws]
    out_shape = [jax.ShapeDtypeStruct((t, w.shape[1]), F32) for w in ws]
    return pl.pallas_call(
        functools.partial(_norm_mod_matmul_kernel, n_w=len(ws)),
        grid=(t // tm,),
        in_specs=in_specs, out_specs=out_specs, out_shape=out_shape,
        compiler_params=pltpu.CompilerParams(vmem_limit_bytes=VMEM_LIMIT),
        name="norm_mod_matmul",
    )(x2d, g.reshape(1, d), shift, scale, *ws)


def _diff_attn_kernel(lam_ref, q_ref, k_ref, v_ref, g_ref, o_ref, qs_ref, s_ref, m_ref, acc_ref, *,
                      tk, n_sub, out_scale):
    tq = q_ref.shape[1]
    n_kv = k_ref.shape[1] // tk
    rb = 2 * tq // n_sub
    q = q_ref[0]
    lane = lax.broadcasted_iota(jnp.int32, q.shape, 1)
    zero = jnp.zeros_like(q)
    qs_ref[:tq] = jnp.where(lane < DA_DH, q, zero)
    qs_ref[tq:] = jnp.where(lane >= DA_DH, q, zero)
    m_ref[...] = jnp.full(m_ref.shape, -1e30, F32)
    acc_ref[...] = jnp.zeros(acc_ref.shape, F32)
    ones = jnp.ones((tk, LANES), BF16)

    def scores(i, slot):
        off = pl.multiple_of(i * tk, tk)
        k = k_ref[0, pl.ds(off, tk), :]
        s_ref[slot] = lax.dot_general(qs_ref[...], k, (((1,), (1,)), ((), ())), preferred_element_type=F32)

    def consume(i, slot):
        off = pl.multiple_of(i * tk, tk)
        v_ext = jnp.concatenate([v_ref[0, pl.ds(off, tk), :], ones], axis=1)
        for r in range(n_sub):
            rows = pl.ds(r * rb, rb)
            s = s_ref[slot, rows, :]
            m_prev = m_ref[rows, :]
            m_new = jnp.maximum(m_prev, jnp.max(s, axis=1, keepdims=True))
            alpha = jnp.exp2(m_prev - m_new)
            p = jnp.exp2(s - jnp.tile(m_new, (1, tk // LANES)))
            pv = jnp.dot(p.astype(BF16), v_ext, preferred_element_type=F32)
            acc_ref[rows, :] = acc_ref[rows, :] * jnp.tile(alpha, (1, 2)) + pv
            m_ref[rows, :] = m_new

    scores(0, 0)

    def body(j, carry):
        scores(2 * j + 1, 1)
        consume(2 * j, 0)
        scores(2 * j + 2, 0)
        consume(2 * j + 1, 1)
        return carry

    lax.fori_loop(0, (n_kv - 1) // 2, body, 0)
    consume(n_kv - 1, 0)
    acc = acc_ref[...]
    o1 = acc[:tq, :LANES] / acc[:tq, LANES:]
    o2 = acc[tq:, :LANES] / acc[tq:, LANES:]
    o = o1 - lam_ref[0] * o2
    ms = jnp.mean(o * o, axis=-1, keepdims=True)
    o = o * lax.rsqrt(ms + SUBLN_EPS) * g_ref[...] * out_scale
    o_ref[0] = o.astype(o_ref.dtype)


def _diff_attention(lam, q, k_all, v_all, subln_g, out_scale, tq=512, tk=1280, n_sub=2):
    b, n, _ = q.shape
    nk = k_all.shape[1]
    assert n % tq == 0 and nk % tk == 0 and tk % (2 * LANES) == 0 and (nk // tk) % 2 == 1
    grid_spec = pltpu.PrefetchScalarGridSpec(
        num_scalar_prefetch=1,
        grid=(b, DA_HEADS, n // tq),
        in_specs=[pl.BlockSpec((1, tq, DA_VD), lambda bi, hi, qi, lam_r: (bi, qi, hi)),
                  pl.BlockSpec((1, nk, DA_VD), lambda bi, hi, qi, lam_r: (bi, 0, hi)),
                  pl.BlockSpec((1, nk, DA_VD), lambda bi, hi, qi, lam_r: (bi, 0, hi)),
                  pl.BlockSpec((1, DA_VD), lambda bi, hi, qi, lam_r: (0, 0))],
        out_specs=pl.BlockSpec((1, tq, DA_VD), lambda bi, hi, qi, lam_r: (bi, qi, hi)),
        scratch_shapes=[pltpu.VMEM((2 * tq, LANES), BF16), pltpu.VMEM((2, 2 * tq, tk), F32),
                        pltpu.VMEM((2 * tq, LANES), F32), pltpu.VMEM((2 * tq, 2 * LANES), F32)],
    )
    return pl.pallas_call(
        functools.partial(_diff_attn_kernel, tk=tk, n_sub=n_sub, out_scale=out_scale),
        grid_spec=grid_spec,
        out_shape=jax.ShapeDtypeStruct((b, n, DA_W), BF16),
        compiler_params=pltpu.CompilerParams(vmem_limit_bytes=VMEM_LIMIT),
        name="diff_attention",
    )(lam.reshape(1), q, k_all, v_all, subln_g.reshape(1, DA_VD))


def _proj_residual_kernel(fm_ref, o_ref, w_ref, x_ref, gate_ref, g_ref, shift_ref, scale_ref, wr_ref, br_ref,
                          xo_ref, h_ref, lg_ref):
    y = (jnp.dot(fm_ref[...].astype(BF16), w_ref[:FOURIER_W], preferred_element_type=F32)
         + jnp.dot(o_ref[...], w_ref[FOURIER_W:], preferred_element_type=F32))
    xn = x_ref[...] + gate_ref[0] * y
    xo_ref[...] = xn
    ms = jnp.mean(xn * xn, axis=-1, keepdims=True)
    h = xn * lax.rsqrt(ms + NORM_EPS) * g_ref[...]
    h = (h * (1.0 + scale_ref[0]) + shift_ref[0]).astype(BF16)
    h_ref[...] = h
    lg_ref[...] = jnp.dot(h, wr_ref[...], preferred_element_type=F32) + br_ref[...]


def _proj_residual(fm, o, w, x2d, gate, g, shift, scale, w_router, b_router, rows_per_batch, tm=512):
    t, d = x2d.shape
    kdim = w.shape[0]
    bpb = rows_per_batch // tm
    wr = jnp.zeros((d, LANES), BF16).at[:, :N_EXPERTS].set(w_router.astype(BF16))
    br = jnp.zeros((1, LANES), F32).at[0, :N_EXPERTS].set(b_router)
    vec = pl.BlockSpec((1, 1, d), lambda i: (i // bpb, 0, 0))
    row = lambda n: pl.BlockSpec((tm, n), lambda i: (i, 0))
    full = lambda s: pl.BlockSpec(s, lambda i: (0, 0))
    return pl.pallas_call(
        _proj_residual_kernel,
        grid=(t // tm,),
        in_specs=[row(fm.shape[1]), row(o.shape[1]), full((kdim, d)), row(d), vec, full((1, d)), vec, vec,
                  full((d, LANES)), full((1, LANES))],
        out_specs=[row(d), row(d), row(LANES)],
        out_shape=[jax.ShapeDtypeStruct((t, d), F32), jax.ShapeDtypeStruct((t, d), BF16),
                   jax.ShapeDtypeStruct((t, LANES), F32)],
        compiler_params=pltpu.CompilerParams(vmem_limit_bytes=VMEM_LIMIT),
        name="proj_residual",
    )(fm, o, w, x2d, gate, g.reshape(1, d), shift, scale, wr, br)


MXU_DIM = 256


def _deinterleave_kernel(w_ref, perm_ref, g_ref, l_ref):
    w = w_ref[0, 0].astype(BF16)
    for c in range(w.shape[1] // MXU_DIM):
        blk = jnp.dot(w[:, c * MXU_DIM:(c + 1) * MXU_DIM], perm_ref[...], preferred_element_type=F32)
        g_ref[0, :, c * LANES:(c + 1) * LANES] = blk[:, :LANES].astype(BF16)
        l_ref[0, :, c * LANES:(c + 1) * LANES] = blk[:, LANES:].astype(BF16)


def _deinterleave_w1(w1_all, layer, tm=512):
    _, e, d, f2 = w1_all.shape
    src = jnp.arange(MXU_DIM)[:, None]
    dst = jnp.arange(MXU_DIM)[None, :]
    perm = (src == jnp.where(dst < LANES, 2 * dst, 2 * (dst - LANES) + 1)).astype(BF16)
    out = jax.ShapeDtypeStruct((e, d, f2 // 2), BF16)
    return pl.pallas_call(
        _deinterleave_kernel,
        grid=(e, d // tm),
        in_specs=[pl.BlockSpec((1, 1, tm, f2), lambda i, j: (layer, i, j, 0)),
                  pl.BlockSpec((MXU_DIM, MXU_DIM), lambda i, j: (0, 0))],
        out_specs=[pl.BlockSpec((1, tm, f2 // 2), lambda i, j: (i, j, 0))] * 2,
        out_shape=[out, out],
        compiler_params=pltpu.CompilerParams(vmem_limit_bytes=VMEM_LIMIT),
        name="deinterleave_w1",
    )(w1_all, perm)


def _moe_kernel(be_ref, x_ref, w1g_ref, w1l_ref, b1g_ref, b1l_ref, w2_ref, b2_ref, o_ref):
    x = x_ref[...]
    ug = jnp.dot(x, w1g_ref[0], preferred_element_type=F32) + b1g_ref[0]
    ul = jnp.dot(x, w1l_ref[0], preferred_element_type=F32) + b1l_ref[0]
    glu = jnp.minimum(ug, SWIGLU_LIMIT)
    lin = jnp.clip(ul, -SWIGLU_LIMIT, SWIGLU_LIMIT)
    act = glu * jax.nn.sigmoid(SWIGLU_ALPHA * glu) * (lin + 1.0)
    y = jnp.dot(act.astype(BF16), w2_ref[0, 0].astype(BF16), preferred_element_type=F32) + b2_ref[0]
    o_ref[...] = y.astype(o_ref.dtype)


def _moe_experts(block_expert, xb, w1g, w1l, b1g, b1l, w2_all, layer, b2):
    n_rows, d = xb.shape
    n_blocks = n_rows // MOE_BLOCK
    wspec = lambda s: pl.BlockSpec((1,) + s, lambda i, be: (be[i], 0, 0))
    grid_spec = pltpu.PrefetchScalarGridSpec(
        num_scalar_prefetch=1,
        grid=(n_blocks,),
        in_specs=[pl.BlockSpec((MOE_BLOCK, d), lambda i, be: (i, 0)),
                  wspec((d, D_FF)), wspec((d, D_FF)), wspec((1, D_FF)), wspec((1, D_FF)),
                  pl.BlockSpec((1, 1, D_FF, d), lambda i, be: (layer, be[i], 0, 0)), wspec((1, d))],
        out_specs=pl.BlockSpec((MOE_BLOCK, d), lambda i, be: (i, 0)),
    )
    return pl.pallas_call(
        _moe_kernel,
        grid_spec=grid_spec,
        out_shape=jax.ShapeDtypeStruct((n_rows, d), BF16),
        compiler_params=pltpu.CompilerParams(vmem_limit_bytes=VMEM_LIMIT),
        name="moe_experts",
    )(block_expert, xb, w1g, w1l, b1g, b1l, w2_all, b2)


def _moe_ffn(h, logits, w1_all, b1, w2_all, b2, layer):
    n_tok, d = h.shape
    top_val, top_idx = lax.top_k(logits, TOP_K)
    gates = jax.nn.softmax(top_val, axis=-1)
    n_assign = n_tok * TOP_K
    e_flat = top_idx.reshape(n_assign).astype(jnp.int32)
    order = jnp.argsort(e_flat).astype(jnp.int32)
    rank = jnp.argsort(order).astype(jnp.int32)
    counts = jnp.sum(e_flat[:, None] == jnp.arange(N_EXPERTS, dtype=jnp.int32)[None, :], axis=0, dtype=jnp.int32)
    starts = jnp.cumsum(counts) - counts
    padded = (counts + MOE_BLOCK - 1) // MOE_BLOCK * MOE_BLOCK
    pad_ends = jnp.cumsum(padded)
    pad_starts = pad_ends - padded
    n_blocks = -(-(n_assign + N_EXPERTS * (MOE_BLOCK - 1)) // MOE_BLOCK)
    n_rows = n_blocks * MOE_BLOCK
    block_start = jnp.arange(n_blocks, dtype=jnp.int32) * MOE_BLOCK
    block_expert = jnp.minimum(jnp.sum(pad_ends[None, :] <= block_start[:, None], axis=1, dtype=jnp.int32),
                               N_EXPERTS - 1)
    shift = pad_starts - starts
    pos = (shift[e_flat] + rank).reshape(n_tok, TOP_K)
    r = jnp.arange(n_rows, dtype=jnp.int32)
    src = jnp.clip(r - jnp.repeat(shift[block_expert], MOE_BLOCK), 0, n_assign - 1)
    row_tok = order[src] // TOP_K
    xb = h[row_tok]
    w1g, w1l = _deinterleave_w1(w1_all, layer)
    yb = _moe_experts(block_expert, xb, w1g, w1l, b1[:, None, 0::2], b1[:, None, 1::2], w2_all, layer,
                      b2[:, None, :])
    return yb, pos, gates


def _combine_kernel(*refs, final):
    y_refs = refs[:TOP_K]
    gt_ref, x_ref, g5_ref = refs[TOP_K:TOP_K + 3]
    o_ref = refs[-1]
    gt = gt_ref[...]
    acc = y_refs[0][...].astype(F32) * gt[:, 0:1]
    for j in range(1, TOP_K):
        acc = acc + y_refs[j][...].astype(F32) * gt[:, j:j + 1]
    xn = x_ref[...] + g5_ref[0] * acc
    if final:
        fg_ref = refs[TOP_K + 3]
        xn = xn * lax.rsqrt(jnp.mean(xn * xn, axis=-1, keepdims=True) + NORM_EPS) * fg_ref[...]
    o_ref[...] = xn


def _moe_combine(yb, pos, gates, x2d, gate5, rows_per_batch, final_g=None, tm=512):
    t, d = x2d.shape
    bpb = rows_per_batch // tm
    ys = [yb[pos[:t, j]] for j in range(TOP_K)]
    gt = jnp.zeros((t, LANES), F32).at[:, :TOP_K].set(gates[:t])
    row = lambda n: pl.BlockSpec((tm, n), lambda i: (i, 0))
    in_specs = [row(d)] * TOP_K + [row(LANES), row(d), pl.BlockSpec((1, 1, d), lambda i: (i // bpb, 0, 0))]
    args = ys + [gt, x2d, gate5]
    if final_g is not None:
        in_specs.append(pl.BlockSpec((1, d), lambda i: (0, 0)))
        args.append(final_g.reshape(1, d))
    return pl.pallas_call(
        functools.partial(_combine_kernel, final=final_g is not None),
        grid=(t // tm,),
        in_specs=in_specs,
        out_specs=row(d),
        out_shape=jax.ShapeDtypeStruct((t, d), F32),
        compiler_params=pltpu.CompilerParams(vmem_limit_bytes=VMEM_LIMIT),
        name="moe_combine",
    )(*args)


GDN_CONV = 5
HALO = 8


def _gdn_prep_kernel(pm_ref, pp_ref, pn_ref, ab_ref, cw_ref, alog_ref, dtb_ref, q_ref, k_ref, v_ref, gb_ref):
    i = pl.program_id(1)
    n_i = pl.num_programs(1)
    tm = pm_ref.shape[1]
    half = GDN_CONV // 2
    keep_prev = (i > 0).astype(F32)
    keep_next = (i < n_i - 1).astype(F32)
    for cg in range(GDN_QKV_W // LANES):
        cols = slice(cg * LANES, (cg + 1) * LANES)
        main = pm_ref[0, :, cols]
        ext = jnp.concatenate([pp_ref[0, :, cols] * keep_prev, main, pn_ref[0, :, cols] * keep_next], axis=0)
        acc = main * cw_ref[half:half + 1, cols]
        for j in range(GDN_CONV):
            if j != half:
                sh = pltpu.roll(ext, (half - j) % (tm + 2 * HALO), 0)[HALO:HALO + tm]
                acc = acc + sh * cw_ref[j:j + 1, cols]
        y = acc * jax.nn.sigmoid(acc)
        if cg < 2 * GDN_HK:
            y = y * lax.rsqrt(jnp.sum(y * y, axis=-1, keepdims=True) + 1e-6)
            if cg < GDN_HK:
                q_ref[0, :, cols] = y * (GDN_DK ** -0.5)
            else:
                k_ref[0, :, (cg - GDN_HK) * LANES:(cg - GDN_HK + 1) * LANES] = y
        else:
            v_ref[0, :, (cg - 2 * GDN_HK) * LANES:(cg - 2 * GDN_HK + 1) * LANES] = y
    ab = ab_ref[0]
    xa = ab + dtb_ref[...]
    softplus = jnp.maximum(xa, 0.0) + jnp.log1p(jnp.exp(-jnp.abs(xa)))
    lane = lax.broadcasted_iota(jnp.int32, ab.shape, 1)
    gb_ref[0] = jnp.where(lane < 2 * GDN_HV, -jnp.exp(alog_ref[...]) * softplus, jax.nn.sigmoid(ab))


def _gdn_prep(p_main, p_ab, conv_w, a_log, dt_bias, tm=256):
    b, t, _ = p_main.shape
    tm = min(tm, t)
    nb = tm // HALO
    last = t // HALO - 1
    pad = lambda a: jnp.zeros((1, LANES), F32).at[0, :2 * GDN_HV].set(a.reshape(-1))
    f = lambda n: jax.ShapeDtypeStruct((b, t, n), F32)
    return pl.pallas_call(
        _gdn_prep_kernel,
        grid=(b, t // tm),
        in_specs=[pl.BlockSpec((1, tm, GDN_QKV_W), lambda bi, i: (bi, i, 0)),
                  pl.BlockSpec((1, HALO, GDN_QKV_W), lambda bi, i: (bi, jnp.maximum(i * nb - 1, 0), 0)),
                  pl.BlockSpec((1, HALO, GDN_QKV_W), lambda bi, i: (bi, jnp.minimum((i + 1) * nb, last), 0)),
                  pl.BlockSpec((1, tm, LANES), lambda bi, i: (bi, i, 0)),
                  pl.BlockSpec((GDN_CONV, GDN_QKV_W), lambda bi, i: (0, 0)),
                  pl.BlockSpec((1, LANES), lambda bi, i: (0, 0)),
                  pl.BlockSpec((1, LANES), lambda bi, i: (0, 0))],
        out_specs=[pl.BlockSpec((1, tm, GDN_QK_W), lambda bi, i: (bi, i, 0)),
                   pl.BlockSpec((1, tm, GDN_QK_W), lambda bi, i: (bi, i, 0)),
                   pl.BlockSpec((1, tm, GDN_V_W), lambda bi, i: (bi, i, 0)),
                   pl.BlockSpec((1, tm, LANES), lambda bi, i: (bi, i, 0))],
        out_shape=[f(GDN_QK_W), f(GDN_QK_W), f(GDN_V_W), f(LANES)],
        compiler_params=pltpu.CompilerParams(vmem_limit_bytes=VMEM_LIMIT),
        name="gdn_prep",
    )(p_main, p_main, p_main, p_ab, conv_w, pad(a_log), pad(dt_bias))


def _gdn_chunk_kernel(q_ref, k_ref, v_ref, gb_ref, s0_ref, o_ref, sfin_ref, s_ref, *, reverse, g_lane, b_lane):
    c = pl.program_id(0)
    n_c = pl.num_programs(0)
    cs = GDN_CHUNK
    rep = GDN_HV // GDN_HK
    nb = q_ref.shape[0]

    @pl.when(c == 0)
    def _():
        s_ref[...] = s0_ref[...].reshape(s_ref.shape)

    row = lax.broadcasted_iota(jnp.int32, (cs, cs), 0)
    col = lax.broadcasted_iota(jnp.int32, (cs, cs), 1)
    incl = (row <= col) if reverse else (row >= col)
    strict = (row < col) if reverse else (row > col)
    eye = (row == col).astype(F32)
    incl_b = incl.astype(BF16)
    last = 0 if reverse else cs - 1
    tn = (((0,), (1,)), ((), ()))

    heads = range(GDN_HV)
    gc, gr, bc, qh, kh, vh = [], [], [], [], [], []
    for b in range(nb):
        gb = gb_ref[b]
        gb_hi = gb.astype(BF16)
        gb_lo = (gb - gb_hi.astype(F32)).astype(BF16)
        g_col = (jnp.dot(incl_b, gb_hi, preferred_element_type=F32)
                 + jnp.dot(incl_b, gb_lo, preferred_element_type=F32))
        g_row = (lax.dot_general(gb_hi, incl_b, tn, preferred_element_type=F32)
                 + lax.dot_general(gb_lo, incl_b, tn, preferred_element_type=F32))
        gc += [g_col[:, g_lane + h:g_lane + h + 1] for h in heads]
        gr += [g_row[g_lane + h:g_lane + h + 1, :] for h in heads]
        bc += [gb[:, b_lane + h:b_lane + h + 1] for h in heads]
        qh += [q_ref[b, :, h * GDN_DK:(h + 1) * GDN_DK] for h in range(GDN_HK)]
        kh += [k_ref[b, :, h * GDN_DK:(h + 1) * GDN_DK] for h in range(GDN_HK)]
        vh += [v_ref[b, :, h * GDN_DV:(h + 1) * GDN_DV] for h in heads]
    gc, gr, bc = jnp.stack(gc), jnp.stack(gr), jnp.stack(bc)
    qh, kh, vh = jnp.stack(qh), jnp.stack(kh), jnp.stack(vh)
    ge = gc[:, last:last + 1, :]
    kh_b = kh.astype(BF16)
    bnt = (((2,), (2,)), ((0,), (0,)))
    kk = lax.dot_general(kh_b, kh_b, bnt, preferred_element_type=F32)
    qk = lax.dot_general(qh.astype(BF16), kh_b, bnt, preferred_element_type=F32)
    kk, qk = jnp.repeat(kk, rep, axis=0), jnp.repeat(qk, rep, axis=0)
    qv, kv = jnp.repeat(qh, rep, axis=0), jnp.repeat(kh, rep, axis=0)

    decay = jnp.where(incl, jnp.exp(jnp.where(incl, gc - gr, 0.0)), 0.0)
    lm = jnp.where(strict, bc * kk * decay, 0.0)
    bmm = lambda a, b: jnp.einsum('hij,hjk->hik', a.astype(BF16), b.astype(BF16), preferred_element_type=F32)
    blk = lambda n: (row // n) == (col // n)
    l0 = jnp.where(blk(16), lm, 0.0)
    p = bmm(l0, l0)
    x = eye - l0
    for _ in range(2):
        xp = bmm(jnp.concatenate([x, p], axis=1), p)
        x = x + xp[:, :cs]
        p = xp[:, cs:]
    x = x + bmm(x, p)
    for n in (32, 64):
        off = jnp.where(blk(n) & ~blk(n // 2), lm, 0.0)
        x = x - bmm(bmm(x, off), x)
    eg = jnp.exp(gc)
    uw = bmm(x, jnp.concatenate([vh * bc, kv * (bc * eg)], axis=2))
    u = uw[:, :, :GDN_DV]
    w = uw[:, :, GDN_DV:]
    qg = qv * eg
    intra = jnp.where(incl, qk * decay, 0.0)
    kt = kv * jnp.exp(ge - gc)
    s = s_ref[...]
    wq = bmm(jnp.concatenate([w, qg], axis=1), s)
    v_new = u - wq[:, :cs]
    o = wq[:, cs:] + bmm(intra, v_new)
    for b in range(nb):
        for h in heads:
            o_ref[b, :, h * GDN_DV:(h + 1) * GDN_DV] = o[b * GDN_HV + h]
    s_ref[...] = s * jnp.exp(ge) + jnp.einsum('hck,hcv->hkv', kt.astype(BF16), v_new.astype(BF16),
                                              preferred_element_type=F32)

    @pl.when(c == n_c - 1)
    def _():
        sfin_ref[...] = s_ref[...].reshape(sfin_ref.shape)


def _gdn_scan(q, k, v, gb, s0, reverse):
    b, t, _ = q.shape
    n_c = t // GDN_CHUNK
    cm = (lambda ci: (0, n_c - 1 - ci, 0)) if reverse else (lambda ci: (0, ci, 0))
    d = 1 if reverse else 0
    smap = lambda ci: (0, 0, 0, 0)
    return pl.pallas_call(
        functools.partial(_gdn_chunk_kernel, reverse=reverse, g_lane=d * GDN_HV, b_lane=(2 + d) * GDN_HV),
        grid=(n_c,),
        in_specs=[pl.BlockSpec((b, GDN_CHUNK, GDN_QK_W), cm), pl.BlockSpec((b, GDN_CHUNK, GDN_QK_W), cm),
                  pl.BlockSpec((b, GDN_CHUNK, GDN_V_W), cm), pl.BlockSpec((b, GDN_CHUNK, LANES), cm),
                  pl.BlockSpec((b, GDN_HV, GDN_DK, GDN_DV), smap)],
        out_specs=[pl.BlockSpec((b, GDN_CHUNK, GDN_V_W), cm), pl.BlockSpec((b, GDN_HV, GDN_DK, GDN_DV), smap)],
        out_shape=[jax.ShapeDtypeStruct((b, t, GDN_V_W), F32),
                   jax.ShapeDtypeStruct((b, GDN_HV, GDN_DK, GDN_DV), F32)],
        scratch_shapes=[pltpu.VMEM((b * GDN_HV, GDN_DK, GDN_DV), F32)],
        compiler_params=pltpu.CompilerParams(vmem_limit_bytes=VMEM_LIMIT),
        name="gdn_scan_bwd" if reverse else "gdn_scan_fwd",
    )(q, k, v, gb, s0)


def _gdn_out_kernel(of_ref, ob_ref, z_ref, ng_ref, w_ref, x_ref, gate_ref, g_ref, shift_ref, scale_ref,
                    wr_ref, br_ref, xo_ref, h_ref, lg_ref):
    parts = []
    for h in range(GDN_HV):
        cols = slice(h * GDN_DV, (h + 1) * GDN_DV)
        o = of_ref[:, cols] + ob_ref[:, cols]
        z = z_ref[:, cols]
        o = o * lax.rsqrt(jnp.mean(o * o, axis=-1, keepdims=True) + NORM_EPS) * ng_ref[...]
        parts.append((o * (z * jax.nn.sigmoid(z))).astype(BF16))
    y = jnp.dot(jnp.concatenate(parts, axis=1), w_ref[...], preferred_element_type=F32)
    xn = x_ref[...] + gate_ref[0] * y
    xo_ref[...] = xn
    ms = jnp.mean(xn * xn, axis=-1, keepdims=True)
    hh = xn * lax.rsqrt(ms + NORM_EPS) * g_ref[...]
    hh = (hh * (1.0 + scale_ref[0]) + shift_ref[0]).astype(BF16)
    h_ref[...] = hh
    lg_ref[...] = jnp.dot(hh, wr_ref[...], preferred_element_type=F32) + br_ref[...]


def _gdn_out_proj(o_f, o_b, p_main, norm_g, w, x2d, gate, g, shift, scale, w_router, b_router, rows_per_batch,
                  tm=256):
    t, d = x2d.shape
    bpb = rows_per_batch // tm
    wr = jnp.zeros((d, LANES), BF16).at[:, :N_EXPERTS].set(w_router.astype(BF16))
    br = jnp.zeros((1, LANES), F32).at[0, :N_EXPERTS].set(b_router)
    vec = pl.BlockSpec((1, 1, d), lambda i: (i // bpb, 0, 0))
    row = lambda n: pl.BlockSpec((tm, n), lambda i: (i, 0))
    full = lambda s: pl.BlockSpec(s, lambda i: (0, 0))
    return pl.pallas_call(
        _gdn_out_kernel,
        grid=(t // tm,),
        in_specs=[row(GDN_V_W), row(GDN_V_W), pl.BlockSpec((tm, GDN_V_W), lambda i: (i, GDN_QKV_W // GDN_V_W)),
                  full((1, GDN_DV)), full((GDN_V_W, d)), row(d), vec, full((1, d)), vec, vec,
                  full((d, LANES)), full((1, LANES))],
        out_specs=[row(d), row(d), row(LANES)],
        out_shape=[jax.ShapeDtypeStruct((t, d), F32), jax.ShapeDtypeStruct((t, d), BF16),
                   jax.ShapeDtypeStruct((t, LANES), F32)],
        compiler_params=pltpu.CompilerParams(vmem_limit_bytes=VMEM_LIMIT),
        name="gdn_out_proj",
    )(o_f, o_b, p_main, norm_g.reshape(1, GDN_DV), w, x2d, gate, g.reshape(1, d), shift, scale, wr, br)


FFT_R = 128
HI = lax.Precision.HIGHEST


def _fft_stage1_kernel(x_ref, fch_ref, f1_ref, tc_ref, ts_ref, yr_ref, yi_ref, *, nb):
    w = FOURIER_W
    for bl in range(nb):
        cols = slice(bl * w, (bl + 1) * w)
        ab = jnp.dot(x_ref[0, :, cols], fch_ref[...], preferred_element_type=F32, precision=HI)
        z = jnp.concatenate([ab[:, :w], ab[:, w:]], axis=0)
        y = jnp.dot(f1_ref[...], z, preferred_element_type=F32, precision=HI)
        yr, yi = y[:FFT_R], y[FFT_R:]
        tc, ts = tc_ref[:, cols], ts_ref[:, cols]
        yr_ref[0, :, cols] = yr * tc + yi * ts
        yi_ref[0, :, cols] = yi * tc - yr * ts


def _fft_stage2_kernel(yr_ref, yi_ref, f2_ref, o_ref, *, nc):
    w = FOURIER_W
    for cl in range(nc):
        y = jnp.concatenate([yr_ref[0, cl], yi_ref[0, cl]], axis=0)
        o_ref[0, :, cl * w:(cl + 1) * w] = jnp.dot(f2_ref[...], y, preferred_element_type=F32, precision=HI)


def _fourier_mix_pallas(f, nb=8):
    bsz, n, w = f.shape
    assert n == FFT_R * FFT_R and w == FOURIER_W
    two_pi = 2.0 * math.pi
    k = jnp.arange(FFT_R, dtype=jnp.int32)
    ang = two_pi * ((k[:, None] * k[None, :]) % FFT_R).astype(F32) / FFT_R
    c1, s1 = jnp.cos(ang), jnp.sin(ang)
    kc = jnp.arange(FOURIER_GD, dtype=jnp.int32)
    angc = two_pi * ((kc[:, None] * kc[None, :]) % FOURIER_GD).astype(F32) / FOURIER_GD
    eye_g = jnp.eye(FOURIER_GROUPS, dtype=F32)
    fch = jnp.concatenate([jnp.kron(eye_g, jnp.cos(angc)), jnp.kron(eye_g, jnp.sin(angc))], axis=1)
    f1 = jnp.concatenate([jnp.concatenate([c1, -s1], axis=1), jnp.concatenate([-s1, -c1], axis=1)], axis=0)
    scale = 1.0 / math.sqrt(n * FOURIER_GD)
    f2 = jnp.concatenate([c1, s1], axis=1) * scale
    angt = two_pi * ((k[:, None] * k[None, :]) % n).astype(F32) / n
    tc = jnp.repeat(jnp.cos(angt), w, axis=1)
    ts = jnp.repeat(jnp.sin(angt), w, axis=1)
    xv = f.reshape(bsz, FFT_R, FFT_R * w)
    blk = pl.BlockSpec((1, FFT_R, nb * w), lambda bi, j: (bi, 0, j))
    tab = pl.BlockSpec((FFT_R, nb * w), lambda bi, j: (0, j))
    full = lambda a: pl.BlockSpec(a.shape, lambda bi, j: (0, 0))
    yshape = jax.ShapeDtypeStruct((bsz, FFT_R, FFT_R * w), F32)
    yr, yi = pl.pallas_call(
        functools.partial(_fft_stage1_kernel, nb=nb),
        grid=(bsz, FFT_R // nb),
        in_specs=[blk, full(fch), full(f1), tab, tab],
        out_specs=[blk, blk],
        out_shape=[yshape, yshape],
        compiler_params=pltpu.CompilerParams(vmem_limit_bytes=VMEM_LIMIT),
        name="fft_stage1",
    )(xv, fch, f1, tc, ts)
    y4 = lambda a: a.reshape(bsz, FFT_R, FFT_R, w)
    yblk = pl.BlockSpec((1, nb, FFT_R, w), lambda bi, j: (bi, j, 0, 0))
    out = pl.pallas_call(
        functools.partial(_fft_stage2_kernel, nc=nb),
        grid=(bsz, FFT_R // nb),
        in_specs=[yblk, yblk, full(f2)],
        out_specs=blk,
        out_shape=yshape,
        compiler_params=pltpu.CompilerParams(vmem_limit_bytes=VMEM_LIMIT),
        name="fft_stage2",
    )(y4(yr), y4(yi), f2)
    return out.reshape(bsz, n, w)


def _even_in_kernel(x_ref, g_ref, shift_ref, scale_ref, w_ref, rc_ref, rs_ref, f_ref, q_ref, k_ref, v_ref, *, q_scale):
    x = x_ref[...]
    ms = jnp.mean(x * x, axis=-1, keepdims=True)
    h = x * lax.rsqrt(ms + NORM_EPS) * g_ref[...]
    h = (h * (1.0 + scale_ref[0]) + shift_ref[0]).astype(BF16)
    p = jnp.dot(h, w_ref[...], preferred_element_type=F32)
    f_ref[...] = p[:, :FOURIER_W]
    rc, rs = rc_ref[...], rs_ref[...]
    lane = lax.broadcasted_iota(jnp.int32, rc.shape, 1)
    first = (lane % DA_DH) < ROPE_AXIS_DIM

    def rope(t):
        partner = jnp.where(first, pltpu.roll(t, LANES - ROPE_AXIS_DIM, 1), pltpu.roll(t, ROPE_AXIS_DIM, 1))
        return t * rc + partner * rs

    for hd in range(DA_HEADS):
        cq = slice(FOURIER_W + hd * DA_VD, FOURIER_W + (hd + 1) * DA_VD)
        ck = slice(FOURIER_W + DA_W + hd * DA_VD, FOURIER_W + DA_W + (hd + 1) * DA_VD)
        q_ref[0, :, hd * DA_VD:(hd + 1) * DA_VD] = (rope(p[:, cq]) * q_scale).astype(BF16)
        k_ref[0, :, hd * DA_VD:(hd + 1) * DA_VD] = rope(p[:, ck]).astype(BF16)
    v_ref[0] = p[:, FOURIER_W + 2 * DA_W:].astype(BF16)


def _even_in_proj(x2d, g, shift, scale, w, cos, sin, bsz, n_lat, n_keys, q_scale, tm=512):
    t, d = x2d.shape
    bpb = n_lat // tm
    rc = jnp.tile(cos, (1, LANES // ROPE_AXIS_DIM))
    rs = jnp.tile(jnp.concatenate([-sin, sin], axis=1), (1, LANES // DA_DH))
    vec = pl.BlockSpec((1, 1, d), lambda i: (i // bpb, 0, 0))
    tab = pl.BlockSpec((tm, LANES), lambda i: (i % bpb, 0))
    seq = pl.BlockSpec((1, tm, DA_W), lambda i: (i // bpb, i % bpb, 0))
    return pl.pallas_call(
        functools.partial(_even_in_kernel, q_scale=q_scale),
        grid=(t // tm,),
        in_specs=[pl.BlockSpec((tm, d), lambda i: (i, 0)), pl.BlockSpec((1, d), lambda i: (0, 0)), vec, vec,
                  pl.BlockSpec(w.shape, lambda i: (0, 0)), tab, tab],
        out_specs=[pl.BlockSpec((tm, FOURIER_W), lambda i: (i, 0)), seq, seq, seq],
        out_shape=[jax.ShapeDtypeStruct((t, FOURIER_W), F32), jax.ShapeDtypeStruct((bsz, n_lat, DA_W), BF16),
                   jax.ShapeDtypeStruct((bsz, n_keys, DA_W), BF16), jax.ShapeDtypeStruct((bsz, n_keys, DA_W), BF16)],
        compiler_params=pltpu.CompilerParams(vmem_limit_bytes=VMEM_LIMIT),
        name="even_in_proj",
    )(x2d, g.reshape(1, d), shift, scale, w, rc, rs)


def _even_layer(x2d, xc, mod, mod_c, norm1_g, norm2_g, w_in, w_out, lam_p, subln_g, lam_init, cos, sin,
                w_router, b_router, bsz, n_lat):
    d = x2d.shape[1]
    w_in_b = w_in.astype(BF16)
    w_out_b = w_out.astype(BF16)
    n_ctx = xc.shape[1]
    q_scale = DA_DH ** -0.5 * math.log2(math.e)
    f, q, k_all, v_all = _even_in_proj(x2d, norm1_g, mod[0], mod[1], w_in_b, cos, sin, bsz, n_lat, n_lat + n_ctx,
                                       q_scale)
    hc = _modulate(_rms_norm(xc, norm1_g), mod_c[0], mod_c[1])
    fc, qc, kc, vc = _even_split(hc @ w_in)
    lp = lam_p.astype(jnp.float32)
    lam = jnp.exp(jnp.sum(lp[0] * lp[1])) - jnp.exp(jnp.sum(lp[2] * lp[3])) + lam_init
    k_all = lax.dynamic_update_slice(k_all, kc.astype(BF16).reshape(bsz, n_ctx, DA_W), (0, n_lat, 0))
    v_all = lax.dynamic_update_slice(v_all, vc.astype(BF16).reshape(bsz, n_ctx, DA_W), (0, n_lat, 0))
    o = _diff_attention(lam, q, k_all, v_all, subln_g, 1.0 - lam_init)
    fm = _fourier_mix_pallas(f.reshape(bsz, n_lat, FOURIER_W))
    x2d, h2, logits = _proj_residual(fm.reshape(bsz * n_lat, FOURIER_W), o.reshape(bsz * n_lat, DA_W), w_out_b,
                                     x2d, mod[2], norm2_g, mod[3], mod[4], w_router, b_router, n_lat)
    oc = _diff_attend(qc, kc, vc, lam)
    oc = _rms_norm(oc, subln_g, SUBLN_EPS) * (1.0 - lam_init)
    mixed_c = jnp.concatenate([_fourier_mix(fc).reshape(bsz, n_ctx, FOURIER_W), oc.reshape(bsz, n_ctx, DA_W)], axis=-1)
    xc = xc + mod_c[2] * (mixed_c @ w_out)
    return x2d, h2, logits, xc


def _odd_layer(x2d, xc, mod, mod_c, norm1_g, norm2_g, w_in, conv_w, a_log, dt_bias, norm_g, w_out,
               w_router, b_router, bsz, n_lat):
    d = x2d.shape[1]
    n_main = GDN_QKV_W + GDN_V_W
    w_main = w_in[:, :n_main].astype(BF16)
    w_ab = jnp.zeros((d, LANES), BF16).at[:, :4 * GDN_HV].set(w_in[:, n_main:].astype(BF16))
    n_ctx = xc.shape[1]
    p_main, p_ab = _norm_mod_matmul(x2d, norm1_g, mod[0], mod[1], [w_main, w_ab], n_lat)
    q, k, v, gb = _gdn_prep(p_main.reshape(bsz, n_lat, n_main), p_ab.reshape(bsz, n_lat, LANES),
                            conv_w, a_log, dt_bias)
    pc_main, pc_ab = _norm_mod_matmul(xc.reshape(bsz * n_ctx, d), norm1_g, mod_c[0], mod_c[1], [w_main, w_ab],
                                      bsz * n_ctx, tm=n_ctx)
    qc, kc, vc, gbc = _gdn_prep(pc_main.reshape(bsz, n_ctx, n_main), pc_ab.reshape(bsz, n_ctx, LANES),
                                conv_w, a_log, dt_bias)
    s0 = jnp.zeros((bsz, GDN_HV, GDN_DK, GDN_DV), jnp.float32)
    _, sc_f = _gdn_scan(qc, kc, vc, gbc, s0, False)
    o_f, _ = _gdn_scan(q, k, v, gb, sc_f, False)
    _, sc_b = _gdn_scan(qc, kc, vc, gbc, s0, True)
    o_b, _ = _gdn_scan(q, k, v, gb, sc_b, True)
    return _gdn_out_proj(o_f.reshape(bsz * n_lat, GDN_V_W), o_b.reshape(bsz * n_lat, GDN_V_W), p_main, norm_g,
                         w_out.astype(BF16), x2d, mod[2], norm2_g, mod[3], mod[4], w_router, b_router, n_lat)


def kernel(x, c, ctx, c_ctx, norm1_g, norm2_g, w_mod, b_mod, ev_w_in, ev_w_out, ev_lam, ev_subln_g,
           od_w_in, od_conv_w, od_a_log, od_dt_bias, od_norm_g, od_w_out,
           moe_w_router, moe_b_router, moe_w1, moe_b1, moe_w2, moe_b2, final_g):
    bsz, n_lat, d = x.shape
    n_ctx = ctx.shape[1]
    depth = w_mod.shape[0]
    assert depth == 2, "kernel is written for one even (attention) and one odd (DeltaNet) layer"
    cos, sin = _axial_rope_tables(n_lat // GRID_W)
    s_lat = jax.nn.silu(c)
    s_ctx = jax.nn.silu(c_ctx)
    x2d = x.reshape(bsz * n_lat, d)
    xc = ctx

    mod = jnp.split((s_lat @ w_mod[0] + b_mod[0])[:, None, :], N_MOD, axis=-1)
    mod_c = jnp.split((s_ctx @ w_mod[0] + b_mod[0])[None, None, :], N_MOD, axis=-1)
    x2d, h2, logits, xc = _even_layer(x2d, xc, mod, mod_c, norm1_g[0], norm2_g[0], ev_w_in[0], ev_w_out[0],
                                      ev_lam[0], ev_subln_g[0], _diff_lambda_init(0), cos, sin,
                                      moe_w_router[0], moe_b_router[0], bsz, n_lat)
    h2c = _modulate(_rms_norm(xc, norm2_g[0]), mod_c[3], mod_c[4]).reshape(bsz * n_ctx, d)
    logits_c = (h2c @ moe_w_router[0] + moe_b_router[0]).astype(jnp.float32)
    yb, pos, gates = _moe_ffn(jnp.concatenate([h2, h2c.astype(BF16)], axis=0),
                              jnp.concatenate([logits[:, :N_EXPERTS], logits_c], axis=0),
                              moe_w1, moe_b1[0], moe_w2, moe_b2[0], 0)
    x2d = _moe_combine(yb, pos, gates, x2d, mod[5], n_lat)
    out_c = jnp.sum(yb[pos[bsz * n_lat:]].astype(F32) * gates[bsz * n_lat:, :, None], axis=1)
    xc = xc + mod_c[5] * out_c.reshape(bsz, n_ctx, d)

    mod = jnp.split((s_lat @ w_mod[1] + b_mod[1])[:, None, :], N_MOD, axis=-1)
    mod_c = jnp.split((s_ctx @ w_mod[1] + b_mod[1])[None, None, :], N_MOD, axis=-1)
    x2d, h2, logits = _odd_layer(x2d, xc, mod, mod_c, norm1_g[1], norm2_g[1], od_w_in[0], od_conv_w[0],
                                 od_a_log[0], od_dt_bias[0], od_norm_g[0], od_w_out[0],
                                 moe_w_router[1], moe_b_router[1], bsz, n_lat)
    yb, pos, gates = _moe_ffn(h2, logits[:, :N_EXPERTS], moe_w1, moe_b1[1], moe_w2, moe_b2[1], 1)
    return _moe_combine(yb, pos, gates, x2d, mod[5], n_lat, final_g=final_g).reshape(bsz, n_lat, d)
```

```python
import functools
import math

import jax
import jax.numpy as jnp
from jax import lax
from jax.experimental import pallas as pl
from jax.experimental.pallas import tpu as pltpu

D_MODEL = 1024
N_MOD = 6
NORM_EPS = 1e-6
GRID_W = 64

FOURIER_GROUPS = 4
FOURIER_GD = 64
FOURIER_W = FOURIER_GROUPS * FOURIER_GD
DA_HEADS = 6
DA_DH = 64
DA_VD = 2 * DA_DH
DA_W = DA_HEADS * DA_VD
ROPE_BASE = 10000.0
ROPE_AXIS_DIM = DA_DH // 2
SUBLN_EPS = 1e-5

GDN_HK = 8
GDN_HV = 16
GDN_DK = 128
GDN_DV = 128
GDN_QK_W = GDN_HK * GDN_DK
GDN_V_W = GDN_HV * GDN_DV
GDN_QKV_W = 2 * GDN_QK_W + GDN_V_W
GDN_CHUNK = 64

N_EXPERTS = 32
TOP_K = 4
D_FF = 1024
SWIGLU_LIMIT = 7.0
SWIGLU_ALPHA = 1.702
MOE_BLOCK = 512

LANES = 128
VMEM_LIMIT = 56 * 1024 * 1024
BF16 = jnp.bfloat16
F32 = jnp.float32


def _rms_norm(x, g, eps=NORM_EPS):
    xf = x.astype(jnp.float32)
    y = xf * lax.rsqrt(jnp.mean(xf * xf, axis=-1, keepdims=True) + eps)
    return y.astype(x.dtype) * g


def _modulate(h, shift, scale):
    return h * (1 + scale) + shift


def _l2_normalize(x, eps=1e-6):
    return x * lax.rsqrt(jnp.sum(x * x, axis=-1, keepdims=True) + eps)


def _diff_lambda_init(layer_idx):
    return 0.8 - 0.6 * math.exp(-0.3 * layer_idx)


def _axial_rope_tables(rows):
    t = jnp.arange(rows * GRID_W, dtype=jnp.int32)
    row = (t // GRID_W).astype(jnp.float32)
    col = (t % GRID_W).astype(jnp.float32)
    inv = ROPE_BASE ** (-jnp.arange(0, ROPE_AXIS_DIM, 2, dtype=jnp.float32) / ROPE_AXIS_DIM)
    ang = jnp.concatenate([row[:, None] * inv, col[:, None] * inv], axis=-1)
    return jnp.cos(ang), jnp.sin(ang)


def _rope2d(x, cos, sin):
    half = x.shape[-1] // 2
    xf = x.astype(jnp.float32)
    x1, x2 = xf[..., :half], xf[..., half:]
    cb = cos[None, :, None, None, :]
    sb = sin[None, :, None, None, :]
    return jnp.concatenate([x1 * cb - x2 * sb, x2 * cb + x1 * sb], axis=-1).astype(x.dtype)


def _fourier_mix(f):
    ff = jnp.moveaxis(f.astype(jnp.float32), 2, 1)
    out = jnp.real(jnp.fft.fft2(ff, norm='ortho'))
    return jnp.moveaxis(out, 1, 2).astype(f.dtype)


def _diff_attend(q, k, v, lam):
    s = jnp.einsum('bqhmd,bkhmd->bhmqk', q, k, preferred_element_type=jnp.float32) * (DA_DH ** -0.5)
    p = jax.nn.softmax(s, axis=-1)
    a = (p[:, :, 0] - lam * p[:, :, 1]).astype(v.dtype)
    return jnp.einsum('bhqk,bkhe->bqhe', a, v)


def _even_split(p):
    b, n, _ = p.shape
    f = p[..., :FOURIER_W].reshape(b, n, FOURIER_GROUPS, FOURIER_GD)
    q, k, v = jnp.split(p[..., FOURIER_W:], 3, axis=-1)
    q = q.reshape(b, n, DA_HEADS, 2, DA_DH)
    k = k.reshape(b, n, DA_HEADS, 2, DA_DH)
    v = v.reshape(b, n, DA_HEADS, DA_VD)
    return f, q, k, v


def _centred_depthwise_conv(x, w):
    width = w.shape[0]
    return lax.conv_general_dilated(
        x, w[:, None, :].astype(x.dtype), window_strides=(1,),
        padding=[((width - 1) // 2, width // 2)],
        dimension_numbers=('NWC', 'WIO', 'NWC'),
        feature_group_count=x.shape[-1])


def _chunk_gated_delta(q, k, v, g, beta, s0):
    b, t, nh, _ = q.shape
    n = t // GDN_CHUNK

    def chunks(a):
        a = a.reshape((b, n, GDN_CHUNK, nh) + a.shape[3:])
        return jnp.moveaxis(a, (1, 3), (0, 2))

    qc, kc, vc, gc, bc = chunks(q), chunks(k), chunks(v), chunks(g), chunks(beta)
    G = jnp.cumsum(gc, axis=-1)
    idx = jnp.arange(GDN_CHUNK)
    incl = idx[:, None] >= idx[None, :]
    strict = idx[:, None] > idx[None, :]
    diff = G[..., :, None] - G[..., None, :]
    decay = jnp.where(incl, jnp.exp(jnp.where(incl, diff, 0.0)), 0.0)
    kb = kc * bc[..., None]
    lmat = jnp.where(strict, jnp.einsum('...id,...jd->...ij', kb, kc) * decay, 0.0)
    eye = jnp.broadcast_to(jnp.eye(GDN_CHUNK, dtype=jnp.float32), lmat.shape)
    tmat = lax.linalg.triangular_solve(lmat + eye, eye, left_side=True, lower=True)
    u = tmat @ (vc * bc[..., None])
    w = tmat @ (kb * jnp.exp(G)[..., None])
    qg = qc * jnp.exp(G)[..., None]
    intra = jnp.where(incl, jnp.einsum('...id,...jd->...ij', qc, kc) * decay, 0.0)
    g_end = G[..., -1]
    k_tail = kc * jnp.exp(g_end[..., None] - G)[..., None]

    def step(state, xs):
        qg_i, u_i, w_i, intra_i, kt_i, ge_i = xs
        v_new = u_i - w_i @ state
        o_i = qg_i @ state + intra_i @ v_new
        state = state * jnp.exp(ge_i)[..., None, None] + jnp.swapaxes(kt_i, -1, -2) @ v_new
        return state, o_i

    s_fin, o = lax.scan(step, s0, (qg, u, w, intra, k_tail, g_end))
    o = jnp.moveaxis(o, (0, 2), (1, 3)).reshape(b, t, nh, -1)
    return o, s_fin


def _gdn_prepare(p_qkv, p_z, p_ab, conv_w, a_log, dt_bias):
    b, n, _ = p_qkv.shape
    qkv = jax.nn.silu(_centred_depthwise_conv(p_qkv, conv_w)).astype(jnp.float32)
    q = qkv[..., :GDN_QK_W].reshape(b, n, GDN_HK, GDN_DK)
    k = qkv[..., GDN_QK_W:2 * GDN_QK_W].reshape(b, n, GDN_HK, GDN_DK)
    v = qkv[..., 2 * GDN_QK_W:].reshape(b, n, GDN_HV, GDN_DV)
    rep = GDN_HV // GDN_HK
    q = jnp.repeat(_l2_normalize(q) * (GDN_DK ** -0.5), rep, axis=2)
    k = jnp.repeat(_l2_normalize(k), rep, axis=2)
    z = p_z.reshape(b, n, GDN_HV, GDN_DV)
    ab = p_ab.astype(jnp.float32).reshape(b, n, 2, 2, GDN_HV)
    a, bb = ab[:, :, 0], ab[:, :, 1]
    g = -jnp.exp(a_log.astype(jnp.float32)) * jax.nn.softplus(a + dt_bias.astype(jnp.float32))
    beta = jax.nn.sigmoid(bb)
    return q, k, v, z, g, beta


def _norm_mod_matmul_kernel(x_ref, g_ref, shift_ref, scale_ref, *refs, n_w):
    x = x_ref[...]
    ms = jnp.mean(x * x, axis=-1, keepdims=True)
    h = x * lax.rsqrt(ms + NORM_EPS) * g_ref[...]
    h = (h * (1.0 + scale_ref[0]) + shift_ref[0]).astype(BF16)
    for w_ref, o_ref in zip(refs[:n_w], refs[n_w:]):
        o_ref[...] = jnp.dot(h, w_ref[...], preferred_element_type=F32).astype(o_ref.dtype)


def _norm_mod_matmul(x2d, g, shift, scale, ws, out_dtypes, rows_per_batch, tm=512):
    t, d = x2d.shape
    bpb = rows_per_batch // tm
    in_specs = [pl.BlockSpec((tm, d), lambda i: (i, 0)),
                pl.BlockSpec((1, d), lambda i: (0, 0)),
                pl.BlockSpec((1, 1, d), lambda i: (i // bpb, 0, 0)),
                pl.BlockSpec((1, 1, d), lambda i: (i // bpb, 0, 0))]
    in_specs += [pl.BlockSpec(w.shape, lambda i: (0, 0)) for w in ws]
    out_specs = [pl.BlockSpec((tm, w.shape[1]), lambda i: (i, 0)) for w in ws]
    out_shape = [jax.ShapeDtypeStruct((t, w.shape[1]), dt) for w, dt in zip(ws, out_dtypes)]
    return pl.pallas_call(
        functools.partial(_norm_mod_matmul_kernel, n_w=len(ws)),
        grid=(t // tm,),
        in_specs=in_specs, out_specs=out_specs, out_shape=out_shape,
        compiler_params=pltpu.CompilerParams(vmem_limit_bytes=VMEM_LIMIT),
        name="norm_mod_matmul",
    )(x2d, g.reshape(1, d), shift, scale, *ws)


def _diff_attn_kernel(lam_ref, q_ref, k_ref, v_ref, g_ref, o_ref, qs_ref, s_ref, m_ref, acc_ref, *,
                      tk, n_sub, out_scale):
    tq = q_ref.shape[1]
    n_kv = k_ref.shape[1] // tk
    rb = 2 * tq // n_sub
    q = q_ref[0]
    lane = lax.broadcasted_iota(jnp.int32, q.shape, 1)
    zero = jnp.zeros_like(q)
    qs_ref[:tq] = jnp.where(lane < DA_DH, q, zero)
    qs_ref[tq:] = jnp.where(lane >= DA_DH, q, zero)
    m_ref[...] = jnp.full(m_ref.shape, -1e30, F32)
    acc_ref[...] = jnp.zeros(acc_ref.shape, F32)
    ones = jnp.ones((tk, LANES), BF16)

    def scores(i, slot):
        off = pl.multiple_of(i * tk, tk)
        k = k_ref[0, pl.ds(off, tk), :]
        s_ref[slot] = lax.dot_general(qs_ref[...], k, (((1,), (1,)), ((), ())), preferred_element_type=F32)

    def consume(i, slot):
        off = pl.multiple_of(i * tk, tk)
        v_ext = jnp.concatenate([v_ref[0, pl.ds(off, tk), :], ones], axis=1)
        for r in range(n_sub):
            rows = pl.ds(r * rb, rb)
            s = s_ref[slot, rows, :]
            m_prev = m_ref[rows, :]
            m_new = jnp.maximum(m_prev, jnp.max(s, axis=1, keepdims=True))
            alpha = jnp.exp2(m_prev - m_new)
            p = jnp.exp2(s - jnp.tile(m_new, (1, tk // LANES)))
            pv = jnp.dot(p.astype(BF16), v_ext, preferred_element_type=F32)
            acc_ref[rows, :] = acc_ref[rows, :] * jnp.tile(alpha, (1, 2)) + pv
            m_ref[rows, :] = m_new

    scores(0, 0)

    def body(j, carry):
        scores(2 * j + 1, 1)
        consume(2 * j, 0)
        scores(2 * j + 2, 0)
        consume(2 * j + 1, 1)
        return carry

    lax.fori_loop(0, (n_kv - 1) // 2, body, 0)
    consume(n_kv - 1, 0)
    acc = acc_ref[...]
    o1 = acc[:tq, :LANES] / acc[:tq, LANES:]
    o2 = acc[tq:, :LANES] / acc[tq:, LANES:]
    o = o1 - lam_ref[0] * o2
    ms = jnp.mean(o * o, axis=-1, keepdims=True)
    o = o * lax.rsqrt(ms + SUBLN_EPS) * g_ref[...] * out_scale
    o_ref[0] = o.astype(o_ref.dtype)


def _diff_attention(lam, q, k_all, v_all, subln_g, out_scale, tq=512, tk=1280, n_sub=2):
    b, n, _ = q.shape
    nk = k_all.shape[1]
    assert n % tq == 0 and nk % tk == 0 and tk % (2 * LANES) == 0 and (nk // tk) % 2 == 1
    grid_spec = pltpu.PrefetchScalarGridSpec(
        num_scalar_prefetch=1,
        grid=(b, DA_HEADS, n // tq),
        in_specs=[pl.BlockSpec((1, tq, DA_VD), lambda bi, hi, qi, lam_r: (bi, qi, hi)),
                  pl.BlockSpec((1, nk, DA_VD), lambda bi, hi, qi, lam_r: (bi, 0, hi)),
                  pl.BlockSpec((1, nk, DA_VD), lambda bi, hi, qi, lam_r: (bi, 0, hi)),
                  pl.BlockSpec((1, DA_VD), lambda bi, hi, qi, lam_r: (0, 0))],
        out_specs=pl.BlockSpec((1, tq, DA_VD), lambda bi, hi, qi, lam_r: (bi, qi, hi)),
        scratch_shapes=[pltpu.VMEM((2 * tq, LANES), BF16), pltpu.VMEM((2, 2 * tq, tk), F32),
                        pltpu.VMEM((2 * tq, LANES), F32), pltpu.VMEM((2 * tq, 2 * LANES), F32)],
    )
    return pl.pallas_call(
        functools.partial(_diff_attn_kernel, tk=tk, n_sub=n_sub, out_scale=out_scale),
        grid_spec=grid_spec,
        out_shape=jax.ShapeDtypeStruct((b, n, DA_W), BF16),
        compiler_params=pltpu.CompilerParams(vmem_limit_bytes=VMEM_LIMIT),
        name="diff_attention",
    )(lam.reshape(1), q, k_all, v_all, subln_g.reshape(1, DA_VD))


def _proj_residual_kernel(fm_ref, o_ref, w_ref, x_ref, gate_ref, g_ref, shift_ref, scale_ref, wr_ref, br_ref,
                          xo_ref, h_ref, lg_ref):
    y = (jnp.dot(fm_ref[...].astype(BF16), w_ref[:FOURIER_W], preferred_element_type=F32)
         + jnp.dot(o_ref[...], w_ref[FOURIER_W:], preferred_element_type=F32))
    xn = x_ref[...] + gate_ref[0] * y
    xo_ref[...] = xn
    ms = jnp.mean(xn * xn, axis=-1, keepdims=True)
    h = xn * lax.rsqrt(ms + NORM_EPS) * g_ref[...]
    h = (h * (1.0 + scale_ref[0]) + shift_ref[0]).astype(BF16)
    h_ref[...] = h
    lg_ref[...] = jnp.dot(h, wr_ref[...], preferred_element_type=F32) + br_ref[...]


def _proj_residual(fm, o, w, x2d, gate, g, shift, scale, w_router, b_router, rows_per_batch, tm=512):
    t, d = x2d.shape
    kdim = w.shape[0]
    bpb = rows_per_batch // tm
    wr = jnp.zeros((d, LANES), BF16).at[:, :N_EXPERTS].set(w_router.astype(BF16))
    br = jnp.zeros((1, LANES), F32).at[0, :N_EXPERTS].set(b_router)
    vec = pl.BlockSpec((1, 1, d), lambda i: (i // bpb, 0, 0))
    row = lambda n: pl.BlockSpec((tm, n), lambda i: (i, 0))
    full = lambda s: pl.BlockSpec(s, lambda i: (0, 0))
    return pl.pallas_call(
        _proj_residual_kernel,
        grid=(t // tm,),
        in_specs=[row(fm.shape[1]), row(o.shape[1]), full((kdim, d)), row(d), vec, full((1, d)), vec, vec,
                  full((d, LANES)), full((1, LANES))],
        out_specs=[row(d), row(d), row(LANES)],
        out_shape=[jax.ShapeDtypeStruct((t, d), F32), jax.ShapeDtypeStruct((t, d), BF16),
                   jax.ShapeDtypeStruct((t, LANES), F32)],
        compiler_params=pltpu.CompilerParams(vmem_limit_bytes=VMEM_LIMIT),
        name="proj_residual",
    )(fm, o, w, x2d, gate, g.reshape(1, d), shift, scale, wr, br)


MXU_DIM = 256


def _moe_kernel(be_ref, x_ref, w1_ref, perm_ref, b1g_ref, b1l_ref, w2_ref, b2_ref, o_ref, w1g_s, w1l_s, w2_s):
    i = pl.program_id(0)

    @pl.when((i == 0) | (be_ref[i] != be_ref[jnp.maximum(i - 1, 0)]))
    def _():
        for c in range(w1_ref.shape[3] // MXU_DIM):
            blk = jnp.dot(w1_ref[0, 0, :, c * MXU_DIM:(c + 1) * MXU_DIM].astype(BF16), perm_ref[...],
                          preferred_element_type=F32)
            w1g_s[:, c * LANES:(c + 1) * LANES] = blk[:, :LANES].astype(BF16)
            w1l_s[:, c * LANES:(c + 1) * LANES] = blk[:, LANES:].astype(BF16)
        w2_s[...] = w2_ref[0, 0].astype(BF16)

    x = x_ref[...]
    ug = jnp.dot(x, w1g_s[...], preferred_element_type=F32) + b1g_ref[0]
    ul = jnp.dot(x, w1l_s[...], preferred_element_type=F32) + b1l_ref[0]
    glu = jnp.minimum(ug, SWIGLU_LIMIT)
    lin = jnp.clip(ul, -SWIGLU_LIMIT, SWIGLU_LIMIT)
    act = glu * jax.nn.sigmoid(SWIGLU_ALPHA * glu) * (lin + 1.0)
    y = jnp.dot(act.astype(BF16), w2_s[...], preferred_element_type=F32) + b2_ref[0]
    o_ref[...] = y.astype(o_ref.dtype)


def _moe_experts(block_expert, xb, w1_all, b1g, b1l, w2_all, layer, b2):
    n_rows, d = xb.shape
    n_blocks = n_rows // MOE_BLOCK
    src = jnp.arange(MXU_DIM)[:, None]
    dst = jnp.arange(MXU_DIM)[None, :]
    perm = (src == jnp.where(dst < LANES, 2 * dst, 2 * (dst - LANES) + 1)).astype(BF16)
    bspec = lambda s: pl.BlockSpec((1,) + s, lambda i, be: (be[i], 0, 0))
    wspec = lambda s: pl.BlockSpec((1, 1) + s, lambda i, be: (layer, be[i], 0, 0))
    grid_spec = pltpu.PrefetchScalarGridSpec(
        num_scalar_prefetch=1,
        grid=(n_blocks,),
        in_specs=[pl.BlockSpec((MOE_BLOCK, d), lambda i, be: (i, 0)),
                  wspec((d, 2 * D_FF)), pl.BlockSpec((MXU_DIM, MXU_DIM), lambda i, be: (0, 0)),
                  bspec((1, D_FF)), bspec((1, D_FF)), wspec((D_FF, d)), bspec((1, d))],
        out_specs=pl.BlockSpec((MOE_BLOCK, d), lambda i, be: (i, 0)),
        scratch_shapes=[pltpu.VMEM((d, D_FF), BF16), pltpu.VMEM((d, D_FF), BF16), pltpu.VMEM((D_FF, d), BF16)],
    )
    return pl.pallas_call(
        _moe_kernel,
        grid_spec=grid_spec,
        out_shape=jax.ShapeDtypeStruct((n_rows, d), BF16),
        compiler_params=pltpu.CompilerParams(vmem_limit_bytes=VMEM_LIMIT),
        name="moe_experts",
    )(block_expert, xb, w1_all, perm, b1g, b1l, w2_all, b2)


def _moe_ffn(h, logits, w1_all, b1, w2_all, b2, layer):
    n_tok, d = h.shape
    top_val, top_idx = lax.top_k(logits, TOP_K)
    gates = jax.nn.softmax(top_val, axis=-1)
    n_assign = n_tok * TOP_K
    e_flat = top_idx.reshape(n_assign).astype(jnp.int32)
    order = jnp.argsort(e_flat).astype(jnp.int32)
    rank = jnp.argsort(order).astype(jnp.int32)
    counts = jnp.sum(e_flat[:, None] == jnp.arange(N_EXPERTS, dtype=jnp.int32)[None, :], axis=0, dtype=jnp.int32)
    starts = jnp.cumsum(counts) - counts
    padded = (counts + MOE_BLOCK - 1) // MOE_BLOCK * MOE_BLOCK
    pad_ends = jnp.cumsum(padded)
    pad_starts = pad_ends - padded
    n_blocks = -(-(n_assign + N_EXPERTS * (MOE_BLOCK - 1)) // MOE_BLOCK)
    n_rows = n_blocks * MOE_BLOCK
    block_start = jnp.arange(n_blocks, dtype=jnp.int32) * MOE_BLOCK
    block_expert = jnp.minimum(jnp.sum(pad_ends[None, :] <= block_start[:, None], axis=1, dtype=jnp.int32),
                               N_EXPERTS - 1)
    shift = pad_starts - starts
    pos = (shift[e_flat] + rank).reshape(n_tok, TOP_K)
    r = jnp.arange(n_rows, dtype=jnp.int32)
    src = jnp.clip(r - jnp.repeat(shift[block_expert], MOE_BLOCK), 0, n_assign - 1)
    row_tok = order[src] // TOP_K
    xb = h[row_tok]
    yb = _moe_experts(block_expert, xb, w1_all, b1[:, None, 0::2], b1[:, None, 1::2], w2_all, layer,
                      b2[:, None, :])
    return yb, pos, gates


def _combine_kernel(*refs, final):
    y_refs = refs[:TOP_K]
    gt_ref, x_ref, g5_ref = refs[TOP_K:TOP_K + 3]
    o_ref = refs[-1]
    gt = gt_ref[...]
    acc = y_refs[0][...].astype(F32) * gt[:, 0:1]
    for j in range(1, TOP_K):
        acc = acc + y_refs[j][...].astype(F32) * gt[:, j:j + 1]
    xn = x_ref[...] + g5_ref[0] * acc
    if final:
        fg_ref = refs[TOP_K + 3]
        xn = xn * lax.rsqrt(jnp.mean(xn * xn, axis=-1, keepdims=True) + NORM_EPS) * fg_ref[...]
    o_ref[...] = xn


def _moe_combine(yb, pos, gates, x2d, gate5, rows_per_batch, final_g=None, tm=512):
    t, d = x2d.shape
    bpb = rows_per_batch // tm
    ys = [yb[pos[:t, j]] for j in range(TOP_K)]
    gt = jnp.zeros((t, LANES), F32).at[:, :TOP_K].set(gates[:t])
    row = lambda n: pl.BlockSpec((tm, n), lambda i: (i, 0))
    in_specs = [row(d)] * TOP_K + [row(LANES), row(d), pl.BlockSpec((1, 1, d), lambda i: (i // bpb, 0, 0))]
    args = ys + [gt, x2d, gate5]
    if final_g is not None:
        in_specs.append(pl.BlockSpec((1, d), lambda i: (0, 0)))
        args.append(final_g.reshape(1, d))
    return pl.pallas_call(
        functools.partial(_combine_kernel, final=final_g is not None),
        grid=(t // tm,),
        in_specs=in_specs,
        out_specs=row(d),
        out_shape=jax.ShapeDtypeStruct((t, d), F32),
        compiler_params=pltpu.CompilerParams(vmem_limit_bytes=VMEM_LIMIT),
        name="moe_combine",
    )(*args)


GDN_CONV = 5
HALO = 16


def _gdn_prep_kernel(pm_ref, pp_ref, pn_ref, ab_ref, cw_ref, alog_ref, dtb_ref, q_ref, k_ref, v_ref, gb_ref):
    i = pl.program_id(1)
    n_i = pl.num_programs(1)
    tm = pm_ref.shape[1]
    half = GDN_CONV // 2
    keep_prev = (i > 0).astype(F32)
    keep_next = (i < n_i - 1).astype(F32)
    for cg in range(GDN_QKV_W // LANES):
        cols = slice(cg * LANES, (cg + 1) * LANES)
        main = pm_ref[0, :, cols].astype(F32)
        ext = jnp.concatenate([pp_ref[0, :, cols].astype(F32) * keep_prev, main,
                               pn_ref[0, :, cols].astype(F32) * keep_next], axis=0)
        acc = main * cw_ref[half:half + 1, cols]
        for j in range(GDN_CONV):
            if j != half:
                sh = pltpu.roll(ext, (half - j) % (tm + 2 * HALO), 0)[HALO:HALO + tm]
                acc = acc + sh * cw_ref[j:j + 1, cols]
        y = acc * jax.nn.sigmoid(acc)
        if cg < 2 * GDN_HK:
            y = y * lax.rsqrt(jnp.sum(y * y, axis=-1, keepdims=True) + 1e-6)
            if cg < GDN_HK:
                q_ref[0, :, cols] = y * (GDN_DK ** -0.5)
            else:
                k_ref[0, :, (cg - GDN_HK) * LANES:(cg - GDN_HK + 1) * LANES] = y
        else:
            v_ref[0, :, (cg - 2 * GDN_HK) * LANES:(cg - 2 * GDN_HK + 1) * LANES] = y
    ab = ab_ref[0]
    xa = ab + dtb_ref[...]
    softplus = jnp.maximum(xa, 0.0) + jnp.log1p(jnp.exp(-jnp.abs(xa)))
    lane = lax.broadcasted_iota(jnp.int32, ab.shape, 1)
    gb_ref[0] = jnp.where(lane < 2 * GDN_HV, -jnp.exp(alog_ref[...]) * softplus, jax.nn.sigmoid(ab))


def _gdn_prep(p_main, p_ab, conv_w, a_log, dt_bias, tm=256):
    b, t, _ = p_main.shape
    tm = min(tm, t)
    nb = tm // HALO
    last = t // HALO - 1
    pad = lambda a: jnp.zeros((1, LANES), F32).at[0, :2 * GDN_HV].set(a.reshape(-1))
    f = lambda n: jax.ShapeDtypeStruct((b, t, n), F32)
    return pl.pallas_call(
        _gdn_prep_kernel,
        grid=(b, t // tm),
        in_specs=[pl.BlockSpec((1, tm, GDN_QKV_W), lambda bi, i: (bi, i, 0)),
                  pl.BlockSpec((1, HALO, GDN_QKV_W), lambda bi, i: (bi, jnp.maximum(i * nb - 1, 0), 0)),
                  pl.BlockSpec((1, HALO, GDN_QKV_W), lambda bi, i: (bi, jnp.minimum((i + 1) * nb, last), 0)),
                  pl.BlockSpec((1, tm, LANES), lambda bi, i: (bi, i, 0)),
                  pl.BlockSpec((GDN_CONV, GDN_QKV_W), lambda bi, i: (0, 0)),
                  pl.BlockSpec((1, LANES), lambda bi, i: (0, 0)),
                  pl.BlockSpec((1, LANES), lambda bi, i: (0, 0))],
        out_specs=[pl.BlockSpec((1, tm, GDN_QK_W), lambda bi, i: (bi, i, 0)),
                   pl.BlockSpec((1, tm, GDN_QK_W), lambda bi, i: (bi, i, 0)),
                   pl.BlockSpec((1, tm, GDN_V_W), lambda bi, i: (bi, i, 0)),
                   pl.BlockSpec((1, tm, LANES), lambda bi, i: (bi, i, 0))],
        out_shape=[f(GDN_QK_W), f(GDN_QK_W), f(GDN_V_W), f(LANES)],
        compiler_params=pltpu.CompilerParams(vmem_limit_bytes=VMEM_LIMIT),
        name="gdn_prep",
    )(p_main, p_main, p_main, p_ab, conv_w, pad(a_log), pad(dt_bias))


def _gdn_chunk_kernel(q_ref, k_ref, v_ref, gb_ref, s0_ref, o_ref, sfin_ref, s_ref, *, reverse, g_lane, b_lane):
    c = pl.program_id(0)
    n_c = pl.num_programs(0)
    cs = GDN_CHUNK
    rep = GDN_HV // GDN_HK
    nb = q_ref.shape[0]

    @pl.when(c == 0)
    def _():
        s_ref[...] = s0_ref[...].reshape(s_ref.shape)

    row = lax.broadcasted_iota(jnp.int32, (cs, cs), 0)
    col = lax.broadcasted_iota(jnp.int32, (cs, cs), 1)
    incl = (row <= col) if reverse else (row >= col)
    strict = (row < col) if reverse else (row > col)
    eye = (row == col).astype(F32)
    incl_b = incl.astype(BF16)
    last = 0 if reverse else cs - 1
    tn = (((0,), (1,)), ((), ()))

    heads = range(GDN_HV)
    gc, gr, bc, qh, kh, vh = [], [], [], [], [], []
    for b in range(nb):
        gb = gb_ref[b]
        gb_hi = gb.astype(BF16)
        gb_lo = (gb - gb_hi.astype(F32)).astype(BF16)
        g_col = (jnp.dot(incl_b, gb_hi, preferred_element_type=F32)
                 + jnp.dot(incl_b, gb_lo, preferred_element_type=F32))
        g_row = (lax.dot_general(gb_hi, incl_b, tn, preferred_element_type=F32)
                 + lax.dot_general(gb_lo, incl_b, tn, preferred_element_type=F32))
        gc += [g_col[:, g_lane + h:g_lane + h + 1] for h in heads]
        gr += [g_row[g_lane + h:g_lane + h + 1, :] for h in heads]
        bc += [gb[:, b_lane + h:b_lane + h + 1] for h in heads]
        qh += [q_ref[b, :, h * GDN_DK:(h + 1) * GDN_DK] for h in range(GDN_HK)]
        kh += [k_ref[b, :, h * GDN_DK:(h + 1) * GDN_DK] for h in range(GDN_HK)]
        vh += [v_ref[b, :, h * GDN_DV:(h + 1) * GDN_DV] for h in heads]
    gc, gr, bc = jnp.stack(gc), jnp.stack(gr), jnp.stack(bc)
    qh, kh, vh = jnp.stack(qh), jnp.stack(kh), jnp.stack(vh)
    ge = gc[:, last:last + 1, :]
    kh_b = kh.astype(BF16)
    bnt = (((2,), (2,)), ((0,), (0,)))
    kk = lax.dot_general(kh_b, kh_b, bnt, preferred_element_type=F32)
    qk = lax.dot_general(qh.astype(BF16), kh_b, bnt, preferred_element_type=F32)
    kk, qk = jnp.repeat(kk, rep, axis=0), jnp.repeat(qk, rep, axis=0)
    qv, kv = jnp.repeat(qh, rep, axis=0), jnp.repeat(kh, rep, axis=0)

    decay = jnp.where(incl, jnp.exp(jnp.where(incl, gc - gr, 0.0)), 0.0)
    lm = jnp.where(strict, bc * kk * decay, 0.0)
    bmm = lambda a, b: jnp.einsum('hij,hjk->hik', a.astype(BF16), b.astype(BF16), preferred_element_type=F32)
    blk = lambda n: (row // n) == (col // n)
    l0 = jnp.where(blk(16), lm, 0.0)
    p = bmm(l0, l0)
    x = eye - l0
    for _ in range(2):
        xp = bmm(jnp.concatenate([x, p], axis=1), p)
        x = x + xp[:, :cs]
        p = xp[:, cs:]
    x = x + bmm(x, p)
    for n in (32, 64):
        off = jnp.where(blk(n) & ~blk(n // 2), lm, 0.0)
        x = x - bmm(bmm(x, off), x)
    eg = jnp.exp(gc)
    uw = bmm(x, jnp.concatenate([vh * bc, kv * (bc * eg)], axis=2))
    u = uw[:, :, :GDN_DV]
    w = uw[:, :, GDN_DV:]
    qg = qv * eg
    intra = jnp.where(incl, qk * decay, 0.0)
    kt = kv * jnp.exp(ge - gc)
    s = s_ref[...]
    wq = bmm(jnp.concatenate([w, qg], axis=1), s)
    v_new = u - wq[:, :cs]
    o = wq[:, cs:] + bmm(intra, v_new)
    for b in range(nb):
        for h in heads:
            o_ref[b, :, h * GDN_DV:(h + 1) * GDN_DV] = o[b * GDN_HV + h]
    s_ref[...] = s * jnp.exp(ge) + jnp.einsum('hck,hcv->hkv', kt.astype(BF16), v_new.astype(BF16),
                                              preferred_element_type=F32)

    @pl.when(c == n_c - 1)
    def _():
        sfin_ref[...] = s_ref[...].reshape(sfin_ref.shape)


def _gdn_scan(q, k, v, gb, s0, reverse):
    b, t, _ = q.shape
    n_c = t // GDN_CHUNK
    cm = (lambda ci: (0, n_c - 1 - ci, 0)) if reverse else (lambda ci: (0, ci, 0))
    d = 1 if reverse else 0
    smap = lambda ci: (0, 0, 0, 0)
    return pl.pallas_call(
        functools.partial(_gdn_chunk_kernel, reverse=reverse, g_lane=d * GDN_HV, b_lane=(2 + d) * GDN_HV),
        grid=(n_c,),
        in_specs=[pl.BlockSpec((b, GDN_CHUNK, GDN_QK_W), cm), pl.BlockSpec((b, GDN_CHUNK, GDN_QK_W), cm),
                  pl.BlockSpec((b, GDN_CHUNK, GDN_V_W), cm), pl.BlockSpec((b, GDN_CHUNK, LANES), cm),
                  pl.BlockSpec((b, GDN_HV, GDN_DK, GDN_DV), smap)],
        out_specs=[pl.BlockSpec((b, GDN_CHUNK, GDN_V_W), cm), pl.BlockSpec((b, GDN_HV, GDN_DK, GDN_DV), smap)],
        out_shape=[jax.ShapeDtypeStruct((b, t, GDN_V_W), F32),
                   jax.ShapeDtypeStruct((b, GDN_HV, GDN_DK, GDN_DV), F32)],
        scratch_shapes=[pltpu.VMEM((b * GDN_HV, GDN_DK, GDN_DV), F32)],
        compiler_params=pltpu.CompilerParams(vmem_limit_bytes=VMEM_LIMIT),
        name="gdn_scan_bwd" if reverse else "gdn_scan_fwd",
    )(q, k, v, gb, s0)


def _gdn_out_kernel(of_ref, ob_ref, z_ref, ng_ref, w_ref, x_ref, gate_ref, g_ref, shift_ref, scale_ref,
                    wr_ref, br_ref, xo_ref, h_ref, lg_ref):
    parts = []
    for h in range(GDN_HV):
        cols = slice(h * GDN_DV, (h + 1) * GDN_DV)
        o = of_ref[:, cols] + ob_ref[:, cols]
        z = z_ref[:, cols].astype(F32)
        o = o * lax.rsqrt(jnp.mean(o * o, axis=-1, keepdims=True) + NORM_EPS) * ng_ref[...]
        parts.append((o * (z * jax.nn.sigmoid(z))).astype(BF16))
    y = jnp.dot(jnp.concatenate(parts, axis=1), w_ref[...], preferred_element_type=F32)
    xn = x_ref[...] + gate_ref[0] * y
    xo_ref[...] = xn
    ms = jnp.mean(xn * xn, axis=-1, keepdims=True)
    hh = xn * lax.rsqrt(ms + NORM_EPS) * g_ref[...]
    hh = (hh * (1.0 + scale_ref[0]) + shift_ref[0]).astype(BF16)
    h_ref[...] = hh
    lg_ref[...] = jnp.dot(hh, wr_ref[...], preferred_element_type=F32) + br_ref[...]


def _gdn_out_proj(o_f, o_b, p_main, norm_g, w, x2d, gate, g, shift, scale, w_router, b_router, rows_per_batch,
                  tm=256):
    t, d = x2d.shape
    bpb = rows_per_batch // tm
    wr = jnp.zeros((d, LANES), BF16).at[:, :N_EXPERTS].set(w_router.astype(BF16))
    br = jnp.zeros((1, LANES), F32).at[0, :N_EXPERTS].set(b_router)
    vec = pl.BlockSpec((1, 1, d), lambda i: (i // bpb, 0, 0))
    row = lambda n: pl.BlockSpec((tm, n), lambda i: (i, 0))
    full = lambda s: pl.BlockSpec(s, lambda i: (0, 0))
    return pl.pallas_call(
        _gdn_out_kernel,
        grid=(t // tm,),
        in_specs=[row(GDN_V_W), row(GDN_V_W), pl.BlockSpec((tm, GDN_V_W), lambda i: (i, GDN_QKV_W // GDN_V_W)),
                  full((1, GDN_DV)), full((GDN_V_W, d)), row(d), vec, full((1, d)), vec, vec,
                  full((d, LANES)), full((1, LANES))],
        out_specs=[row(d), row(d), row(LANES)],
        out_shape=[jax.ShapeDtypeStruct((t, d), F32), jax.ShapeDtypeStruct((t, d), BF16),
                   jax.ShapeDtypeStruct((t, LANES), F32)],
        compiler_params=pltpu.CompilerParams(vmem_limit_bytes=VMEM_LIMIT),
        name="gdn_out_proj",
    )(o_f, o_b, p_main, norm_g.reshape(1, GDN_DV), w, x2d, gate, g.reshape(1, d), shift, scale, wr, br)


FFT_R = 128
HI = lax.Precision.HIGHEST


def _fft_stage1_kernel(x_ref, fch_ref, f1_ref, tc_ref, ts_ref, yr_ref, yi_ref, *, nb):
    w = FOURIER_W
    for bl in range(nb):
        cols = slice(bl * w, (bl + 1) * w)
        ab = jnp.dot(x_ref[0, :, cols], fch_ref[...], preferred_element_type=F32, precision=HI)
        z = jnp.concatenate([ab[:, :w], ab[:, w:]], axis=0)
        y = jnp.dot(f1_ref[...], z, preferred_element_type=F32, precision=HI)
        yr, yi = y[:FFT_R], y[FFT_R:]
        tc, ts = tc_ref[:, cols], ts_ref[:, cols]
        yr_ref[0, :, cols] = yr * tc + yi * ts
        yi_ref[0, :, cols] = yi * tc - yr * ts


def _fft_stage2_kernel(yr_ref, yi_ref, f2_ref, o_ref, *, nc):
    w = FOURIER_W
    for cl in range(nc):
        y = jnp.concatenate([yr_ref[0, cl], yi_ref[0, cl]], axis=0)
        o_ref[0, :, cl * w:(cl + 1) * w] = jnp.dot(f2_ref[...], y, preferred_element_type=F32, precision=HI)


def _fourier_mix_pallas(f, nb=8):
    bsz, n, w = f.shape
    assert n == FFT_R * FFT_R and w == FOURIER_W
    two_pi = 2.0 * math.pi
    k = jnp.arange(FFT_R, dtype=jnp.int32)
    ang = two_pi * ((k[:, None] * k[None, :]) % FFT_R).astype(F32) / FFT_R
    c1, s1 = jnp.cos(ang), jnp.sin(ang)
    kc = jnp.arange(FOURIER_GD, dtype=jnp.int32)
    angc = two_pi * ((kc[:, None] * kc[None, :]) % FOURIER_GD).astype(F32) / FOURIER_GD
    eye_g = jnp.eye(FOURIER_GROUPS, dtype=F32)
    fch = jnp.concatenate([jnp.kron(eye_g, jnp.cos(angc)), jnp.kron(eye_g, jnp.sin(angc))], axis=1)
    f1 = jnp.concatenate([jnp.concatenate([c1, -s1], axis=1), jnp.concatenate([-s1, -c1], axis=1)], axis=0)
    scale = 1.0 / math.sqrt(n * FOURIER_GD)
    f2 = jnp.concatenate([c1, s1], axis=1) * scale
    angt = two_pi * ((k[:, None] * k[None, :]) % n).astype(F32) / n
    tc = jnp.repeat(jnp.cos(angt), w, axis=1)
    ts = jnp.repeat(jnp.sin(angt), w, axis=1)
    xv = f.reshape(bsz, FFT_R, FFT_R * w)
    blk = pl.BlockSpec((1, FFT_R, nb * w), lambda bi, j: (bi, 0, j))
    tab = pl.BlockSpec((FFT_R, nb * w), lambda bi, j: (0, j))
    full = lambda a: pl.BlockSpec(a.shape, lambda bi, j: (0, 0))
    yshape = jax.ShapeDtypeStruct((bsz, FFT_R, FFT_R * w), F32)
    yr, yi = pl.pallas_call(
        functools.partial(_fft_stage1_kernel, nb=nb),
        grid=(bsz, FFT_R // nb),
        in_specs=[blk, full(fch), full(f1), tab, tab],
        out_specs=[blk, blk],
        out_shape=[yshape, yshape],
        compiler_params=pltpu.CompilerParams(vmem_limit_bytes=VMEM_LIMIT),
        name="fft_stage1",
    )(xv, fch, f1, tc, ts)
    y4 = lambda a: a.reshape(bsz, FFT_R, FFT_R, w)
    yblk = pl.BlockSpec((1, nb, FFT_R, w), lambda bi, j: (bi, j, 0, 0))
    out = pl.pallas_call(
        functools.partial(_fft_stage2_kernel, nc=nb),
        grid=(bsz, FFT_R // nb),
        in_specs=[yblk, yblk, full(f2)],
        out_specs=blk,
        out_shape=yshape,
        compiler_params=pltpu.CompilerParams(vmem_limit_bytes=VMEM_LIMIT),
        name="fft_stage2",
    )(y4(yr), y4(yi), f2)
    return out.reshape(bsz, n, w)


def _even_in_kernel(x_ref, g_ref, shift_ref, scale_ref, w_ref, rc_ref, rs_ref, f_ref, q_ref, k_ref, v_ref, *, q_scale):
    x = x_ref[...]
    ms = jnp.mean(x * x, axis=-1, keepdims=True)
    h = x * lax.rsqrt(ms + NORM_EPS) * g_ref[...]
    h = (h * (1.0 + scale_ref[0]) + shift_ref[0]).astype(BF16)
    p = jnp.dot(h, w_ref[...], preferred_element_type=F32)
    f_ref[...] = p[:, :FOURIER_W]
    rc, rs = rc_ref[...], rs_ref[...]
    lane = lax.broadcasted_iota(jnp.int32, rc.shape, 1)
    first = (lane % DA_DH) < ROPE_AXIS_DIM

    def rope(t):
        partner = jnp.where(first, pltpu.roll(t, LANES - ROPE_AXIS_DIM, 1), pltpu.roll(t, ROPE_AXIS_DIM, 1))
        return t * rc + partner * rs

    for hd in range(DA_HEADS):
        cq = slice(FOURIER_W + hd * DA_VD, FOURIER_W + (hd + 1) * DA_VD)
        ck = slice(FOURIER_W + DA_W + hd * DA_VD, FOURIER_W + DA_W + (hd + 1) * DA_VD)
        q_ref[0, :, hd * DA_VD:(hd + 1) * DA_VD] = (rope(p[:, cq]) * q_scale).astype(BF16)
        k_ref[0, :, hd * DA_VD:(hd + 1) * DA_VD] = rope(p[:, ck]).astype(BF16)
    v_ref[0] = p[:, FOURIER_W + 2 * DA_W:].astype(BF16)


def _even_in_proj(x2d, g, shift, scale, w, cos, sin, bsz, n_lat, n_keys, q_scale, tm=512):
    t, d = x2d.shape
    bpb = n_lat // tm
    rc = jnp.tile(cos, (1, LANES // ROPE_AXIS_DIM))
    rs = jnp.tile(jnp.concatenate([-sin, sin], axis=1), (1, LANES // DA_DH))
    vec = pl.BlockSpec((1, 1, d), lambda i: (i // bpb, 0, 0))
    tab = pl.BlockSpec((tm, LANES), lambda i: (i % bpb, 0))
    seq = pl.BlockSpec((1, tm, DA_W), lambda i: (i // bpb, i % bpb, 0))
    return pl.pallas_call(
        functools.partial(_even_in_kernel, q_scale=q_scale),
        grid=(t // tm,),
        in_specs=[pl.BlockSpec((tm, d), lambda i: (i, 0)), pl.BlockSpec((1, d), lambda i: (0, 0)), vec, vec,
                  pl.BlockSpec(w.shape, lambda i: (0, 0)), tab, tab],
        out_specs=[pl.BlockSpec((tm, FOURIER_W), lambda i: (i, 0)), seq, seq, seq],
        out_shape=[jax.ShapeDtypeStruct((t, FOURIER_W), F32), jax.ShapeDtypeStruct((bsz, n_lat, DA_W), BF16),
                   jax.ShapeDtypeStruct((bsz, n_keys, DA_W), BF16), jax.ShapeDtypeStruct((bsz, n_keys, DA_W), BF16)],
        compiler_params=pltpu.CompilerParams(vmem_limit_bytes=VMEM_LIMIT),
        name="even_in_proj",
    )(x2d, g.reshape(1, d), shift, scale, w, rc, rs)


def _even_layer(x2d, xc, mod, mod_c, norm1_g, norm2_g, w_in, w_out, lam_p, subln_g, lam_init, cos, sin,
                w_router, b_router, bsz, n_lat):
    d = x2d.shape[1]
    w_in_b = w_in.astype(BF16)
    w_out_b = w_out.astype(BF16)
    n_ctx = xc.shape[1]
    q_scale = DA_DH ** -0.5 * math.log2(math.e)
    f, q, k_all, v_all = _even_in_proj(x2d, norm1_g, mod[0], mod[1], w_in_b, cos, sin, bsz, n_lat, n_lat + n_ctx,
                                       q_scale)
    hc = _modulate(_rms_norm(xc, norm1_g), mod_c[0], mod_c[1])
    fc, qc, kc, vc = _even_split(hc @ w_in)
    lp = lam_p.astype(jnp.float32)
    lam = jnp.exp(jnp.sum(lp[0] * lp[1])) - jnp.exp(jnp.sum(lp[2] * lp[3])) + lam_init
    k_all = lax.dynamic_update_slice(k_all, kc.astype(BF16).reshape(bsz, n_ctx, DA_W), (0, n_lat, 0))
    v_all = lax.dynamic_update_slice(v_all, vc.astype(BF16).reshape(bsz, n_ctx, DA_W), (0, n_lat, 0))
    o = _diff_attention(lam, q, k_all, v_all, subln_g, 1.0 - lam_init)
    fm = _fourier_mix_pallas(f.reshape(bsz, n_lat, FOURIER_W))
    x2d, h2, logits = _proj_residual(fm.reshape(bsz * n_lat, FOURIER_W), o.reshape(bsz * n_lat, DA_W), w_out_b,
                                     x2d, mod[2], norm2_g, mod[3], mod[4], w_router, b_router, n_lat)
    oc = _diff_attend(qc, kc, vc, lam)
    oc = _rms_norm(oc, subln_g, SUBLN_EPS) * (1.0 - lam_init)
    mixed_c = jnp.concatenate([_fourier_mix(fc).reshape(bsz, n_ctx, FOURIER_W), oc.reshape(bsz, n_ctx, DA_W)], axis=-1)
    xc = xc + mod_c[2] * (mixed_c @ w_out)
    return x2d, h2, logits, xc


def _odd_layer(x2d, xc, mod, mod_c, norm1_g, norm2_g, w_in, conv_w, a_log, dt_bias, norm_g, w_out,
               w_router, b_router, bsz, n_lat):
    d = x2d.shape[1]
    n_main = GDN_QKV_W + GDN_V_W
    w_main = w_in[:, :n_main].astype(BF16)
    w_ab = jnp.zeros((d, LANES), BF16).at[:, :4 * GDN_HV].set(w_in[:, n_main:].astype(BF16))
    n_ctx = xc.shape[1]
    p_main, p_ab = _norm_mod_matmul(x2d, norm1_g, mod[0], mod[1], [w_main, w_ab], [BF16, F32], n_lat)
    q, k, v, gb = _gdn_prep(p_main.reshape(bsz, n_lat, n_main), p_ab.reshape(bsz, n_lat, LANES),
                            conv_w, a_log, dt_bias)
    pc_main, pc_ab = _norm_mod_matmul(xc.reshape(bsz * n_ctx, d), norm1_g, mod_c[0], mod_c[1], [w_main, w_ab],
                                      [BF16, F32], bsz * n_ctx, tm=n_ctx)
    qc, kc, vc, gbc = _gdn_prep(pc_main.reshape(bsz, n_ctx, n_main), pc_ab.reshape(bsz, n_ctx, LANES),
                                conv_w, a_log, dt_bias)
    s0 = jnp.zeros((bsz, GDN_HV, GDN_DK, GDN_DV), jnp.float32)
    _, sc_f = _gdn_scan(qc, kc, vc, gbc, s0, False)
    o_f, _ = _gdn_scan(q, k, v, gb, sc_f, False)
    _, sc_b = _gdn_scan(qc, kc, vc, gbc, s0, True)
    o_b, _ = _gdn_scan(q, k, v, gb, sc_b, True)
    return _gdn_out_proj(o_f.reshape(bsz * n_lat, GDN_V_W), o_b.reshape(bsz * n_lat, GDN_V_W), p_main, norm_g,
                         w_out.astype(BF16), x2d, mod[2], norm2_g, mod[3], mod[4], w_router, b_router, n_lat)


def kernel(x, c, ctx, c_ctx, norm1_g, norm2_g, w_mod, b_mod, ev_w_in, ev_w_out, ev_lam, ev_subln_g,
           od_w_in, od_conv_w, od_a_log, od_dt_bias, od_norm_g, od_w_out,
           moe_w_router, moe_b_router, moe_w1, moe_b1, moe_w2, moe_b2, final_g):
    bsz, n_lat, d = x.shape
    n_ctx = ctx.shape[1]
    depth = w_mod.shape[0]
    assert depth == 2, "kernel is written for one even (attention) and one odd (DeltaNet) layer"
    cos, sin = _axial_rope_tables(n_lat // GRID_W)
    s_lat = jax.nn.silu(c)
    s_ctx = jax.nn.silu(c_ctx)
    x2d = x.reshape(bsz * n_lat, d)
    xc = ctx

    mod = jnp.split((s_lat @ w_mod[0] + b_mod[0])[:, None, :], N_MOD, axis=-1)
    mod_c = jnp.split((s_ctx @ w_mod[0] + b_mod[0])[None, None, :], N_MOD, axis=-1)
    x2d, h2, logits, xc = _even_layer(x2d, xc, mod, mod_c, norm1_g[0], norm2_g[0], ev_w_in[0], ev_w_out[0],
                                      ev_lam[0], ev_subln_g[0], _diff_lambda_init(0), cos, sin,
                                      moe_w_router[0], moe_b_router[0], bsz, n_lat)
    h2c = _modulate(_rms_norm(xc, norm2_g[0]), mod_c[3], mod_c[4]).reshape(bsz * n_ctx, d)
    logits_c = (h2c @ moe_w_router[0] + moe_b_router[0]).astype(jnp.float32)
    yb, pos, gates = _moe_ffn(jnp.concatenate([h2, h2c.astype(BF16)], axis=0),
                              jnp.concatenate([logits[:, :N_EXPERTS], logits_c], axis=0),
                              moe_w1, moe_b1[0], moe_w2, moe_b2[0], 0)
    x2d = _moe_combine(yb, pos, gates, x2d, mod[5], n_lat)
    out_c = jnp.sum(yb[pos[bsz * n_lat:]].astype(F32) * gates[bsz * n_lat:, :, None], axis=1)
    xc = xc + mod_c[5] * out_c.reshape(bsz, n_ctx, d)

    mod = jnp.split((s_lat @ w_mod[1] + b_mod[1])[:, None, :], N_MOD, axis=-1)
    mod_c = jnp.split((s_ctx @ w_mod[1] + b_mod[1])[None, None, :], N_MOD, axis=-1)
    x2d, h2, logits = _odd_layer(x2d, xc, mod, mod_c, norm1_g[1], norm2_g[1], od_w_in[0], od_conv_w[0],
                                 od_a_log[0], od_dt_bias[0], od_norm_g[0], od_w_out[0],
                                 moe_w_router[1], moe_b_router[1], bsz, n_lat)
    yb, pos, gates = _moe_ffn(h2, logits[:, :N_EXPERTS], moe_w1, moe_b1[1], moe_w2, moe_b2[1], 1)
    return _moe_combine(yb, pos, gates, x2d, mod[5], n_lat, final_g=final_g).reshape(bsz, n_lat, d)
```

```python
import functools
import math

import jax
import jax.numpy as jnp
from jax import lax
from jax.experimental import pallas as pl
from jax.experimental.pallas import tpu as pltpu

D_MODEL = 1024
N_MOD = 6
NORM_EPS = 1e-6
GRID_W = 64

FOURIER_GROUPS = 4
FOURIER_GD = 64
FOURIER_W = FOURIER_GROUPS * FOURIER_GD
DA_HEADS = 6
DA_DH = 64
DA_VD = 2 * DA_DH
DA_W = DA_HEADS * DA_VD
ROPE_BASE = 10000.0
ROPE_AXIS_DIM = DA_DH // 2
SUBLN_EPS = 1e-5

GDN_HK = 8
GDN_HV = 16
GDN_DK = 128
GDN_DV = 128
GDN_QK_W = GDN_HK * GDN_DK
GDN_V_W = GDN_HV * GDN_DV
GDN_QKV_W = 2 * GDN_QK_W + GDN_V_W
GDN_MAIN_W = GDN_QKV_W + GDN_V_W
GDN_CONV = 5
GDN_CHUNK = 64

N_EXPERTS = 32
TOP_K = 4
D_FF = 1024
SWIGLU_LIMIT = 7.0
SWIGLU_ALPHA = 1.702
MOE_BLOCK = 512

LANES = 128
SUBLANES_BF16 = 16
MXU_DIM = 256
VMEM_LIMIT = 56 * 1024 * 1024
BF16 = jnp.bfloat16
F32 = jnp.float32
HI = lax.Precision.HIGHEST

TM_PROJ = 512
TM_GDN = 256
TQ_ATTN, TK_ATTN = 512, 1280

_CP = pltpu.CompilerParams(vmem_limit_bytes=VMEM_LIMIT)


def _diff_lambda_init(layer_idx):
    return 0.8 - 0.6 * math.exp(-0.3 * layer_idx)


def _axial_rope_tables(rows):
    t = jnp.arange(rows * GRID_W, dtype=jnp.int32)
    row = (t // GRID_W).astype(F32)
    col = (t % GRID_W).astype(F32)
    inv = ROPE_BASE ** (-jnp.arange(0, ROPE_AXIS_DIM, 2, dtype=F32) / ROPE_AXIS_DIM)
    ang = jnp.concatenate([row[:, None] * inv, col[:, None] * inv], axis=-1)
    return jnp.cos(ang), jnp.sin(ang)


def _norm_mod(x, g_ref, shift_ref, scale_ref):
    h = x * lax.rsqrt(jnp.mean(x * x, axis=-1, keepdims=True) + NORM_EPS) * g_ref[...]
    return (h * (1.0 + scale_ref[0]) + shift_ref[0]).astype(BF16)


def _route(logits):
    lane = lax.broadcasted_iota(jnp.int32, logits.shape, 1)
    lane_f = lane.astype(F32)
    neg = jnp.float32(-jnp.inf)
    rest = jnp.where(lane < N_EXPERTS, logits, neg)
    idx = jnp.zeros(logits.shape, F32)
    val = jnp.full(logits.shape, neg, F32)
    for j in range(TOP_K):
        m = jnp.max(rest, axis=-1, keepdims=True)
        sel = jnp.min(jnp.where(rest == m, lane_f, float(LANES)), axis=-1, keepdims=True)
        idx = jnp.where(lane == j, sel, idx)
        val = jnp.where(lane == j, m, val)
        rest = jnp.where(lane_f == sel, neg, rest)
    e = jnp.exp(val - val[:, 0:1])
    return idx, e / jnp.sum(e, axis=-1, keepdims=True)


def _residual_norm_route(y, x_ref, gate_ref, g_ref, shift_ref, scale_ref, wr_ref, br_ref,
                         xo_ref, h_ref, idx_ref, gates_ref):
    xn = x_ref[...] + gate_ref[0] * y
    xo_ref[...] = xn
    h = _norm_mod(xn, g_ref, shift_ref, scale_ref)
    h_ref[...] = h
    idx, gates = _route(jnp.dot(h, wr_ref[...], preferred_element_type=F32) + br_ref[...])
    idx_ref[...] = idx.astype(jnp.int32)
    gates_ref[...] = gates


def _row(tm, n):
    return pl.BlockSpec((tm, n), lambda i: (i, 0))


def _full(shape):
    return pl.BlockSpec(shape, lambda i: (0,) * len(shape))


def _vec(d, rows_per_vec, tm):
    per = rows_per_vec // tm
    return pl.BlockSpec((1, 1, d), lambda i: (i // per, 0, 0))


def _router_operands(w_router, b_router):
    d = w_router.shape[0]
    wr = jnp.zeros((d, LANES), BF16).at[:, :N_EXPERTS].set(w_router.astype(BF16))
    br = jnp.zeros((1, LANES), F32).at[0, :N_EXPERTS].set(b_router)
    return wr, br


def _route_out(t, d, tm):
    specs = [_row(tm, d), _row(tm, d), _row(tm, LANES), _row(tm, LANES)]
    shapes = [jax.ShapeDtypeStruct((t, d), F32), jax.ShapeDtypeStruct((t, d), BF16),
              jax.ShapeDtypeStruct((t, LANES), jnp.int32), jax.ShapeDtypeStruct((t, LANES), F32)]
    return specs, shapes


def _mod_kernel(c_ref, w_ref, b_ref, o_ref):
    c = c_ref[...]
    s = (c * jax.nn.sigmoid(c)).astype(BF16)
    o_ref[...] = jnp.dot(s, w_ref[0].astype(BF16), preferred_element_type=F32) + b_ref[0]


def _mod_vectors(c_rows, w_mod, b_mod, layer):
    d = c_rows.shape[1]
    return pl.pallas_call(
        _mod_kernel,
        grid=(N_MOD,),
        in_specs=[pl.BlockSpec(c_rows.shape, lambda n: (0, 0)),
                  pl.BlockSpec((1, d, d), lambda n: (layer, 0, n)),
                  pl.BlockSpec((1, 1, d), lambda n: (layer, 0, n))],
        out_specs=pl.BlockSpec((c_rows.shape[0], d), lambda n: (0, n)),
        out_shape=jax.ShapeDtypeStruct((c_rows.shape[0], N_MOD * d), F32),
        compiler_params=_CP,
        name="mod_vectors",
    )(c_rows, w_mod, b_mod.reshape(b_mod.shape[0], 1, -1))


def _even_in_kernel(x_ref, g_ref, shift_ref, scale_ref, w_ref, rc_ref, rs_ref, f_ref, q_ref, k_ref, v_ref, *, q_scale):
    h = _norm_mod(x_ref[...], g_ref, shift_ref, scale_ref)
    p = jnp.dot(h, w_ref[...], preferred_element_type=F32)
    f_ref[...] = p[:, :FOURIER_W]
    rc, rs = rc_ref[...], rs_ref[...]
    lane = lax.broadcasted_iota(jnp.int32, rc.shape, 1)
    first = (lane % DA_DH) < ROPE_AXIS_DIM

    def rope(t):
        partner = jnp.where(first, pltpu.roll(t, LANES - ROPE_AXIS_DIM, 1), pltpu.roll(t, ROPE_AXIS_DIM, 1))
        return t * rc + partner * rs

    for hd in range(DA_HEADS):
        cq = slice(FOURIER_W + hd * DA_VD, FOURIER_W + (hd + 1) * DA_VD)
        ck = slice(FOURIER_W + DA_W + hd * DA_VD, FOURIER_W + DA_W + (hd + 1) * DA_VD)
        q_ref[0, :, hd * DA_VD:(hd + 1) * DA_VD] = (rope(p[:, cq]) * q_scale).astype(BF16)
        k_ref[0, :, hd * DA_VD:(hd + 1) * DA_VD] = rope(p[:, ck]).astype(BF16)
    v_ref[0] = p[:, FOURIER_W + 2 * DA_W:].astype(BF16)


def _even_in_proj(x2d, g, shift, scale, w, cos, sin, bsz, n_seq, n_keys, q_scale, tm):
    t, d = x2d.shape
    bpb = n_seq // tm
    rc = jnp.tile(cos, (1, LANES // ROPE_AXIS_DIM))
    rs = jnp.tile(jnp.concatenate([-sin, sin], axis=1), (1, LANES // DA_DH))
    tab = pl.BlockSpec((tm, LANES), lambda i: (i % bpb, 0))
    seq = pl.BlockSpec((1, tm, DA_W), lambda i: (i // bpb, i % bpb, 0))
    vec = _vec(d, n_seq * (bsz // shift.shape[0]), tm)
    return pl.pallas_call(
        functools.partial(_even_in_kernel, q_scale=q_scale),
        grid=(t // tm,),
        in_specs=[_row(tm, d), _full((1, d)), vec, vec, _full(w.shape), tab, tab],
        out_specs=[_row(tm, FOURIER_W), seq, seq, seq],
        out_shape=[jax.ShapeDtypeStruct((t, FOURIER_W), F32), jax.ShapeDtypeStruct((bsz, n_seq, DA_W), BF16),
                   jax.ShapeDtypeStruct((bsz, n_keys, DA_W), BF16), jax.ShapeDtypeStruct((bsz, n_keys, DA_W), BF16)],
        compiler_params=_CP,
        name="even_in_proj",
    )(x2d, g.reshape(1, d), shift, scale, w, rc, rs)


def _diff_attn_kernel(lam_ref, q_ref, k_ref, v_ref, g_ref, o_ref, qs_ref, s_ref, m_ref, acc_ref, *,
                      tk, n_sub, out_scale):
    tq = q_ref.shape[1]
    n_kv = k_ref.shape[1] // tk
    rb = 2 * tq // n_sub
    q = q_ref[0]
    lane = lax.broadcasted_iota(jnp.int32, q.shape, 1)
    zero = jnp.zeros_like(q)
    qs_ref[:tq] = jnp.where(lane < DA_DH, q, zero)
    qs_ref[tq:] = jnp.where(lane >= DA_DH, q, zero)
    m_ref[...] = jnp.full(m_ref.shape, -1e30, F32)
    acc_ref[...] = jnp.zeros(acc_ref.shape, F32)
    ones = jnp.ones((tk, LANES), BF16)

    def scores(i, slot):
        off = pl.multiple_of(i * tk, tk)
        k = k_ref[0, pl.ds(off, tk), :]
        s_ref[slot] = lax.dot_general(qs_ref[...], k, (((1,), (1,)), ((), ())), preferred_element_type=F32)

    def consume(i, slot):
        off = pl.multiple_of(i * tk, tk)
        v_ext = jnp.concatenate([v_ref[0, pl.ds(off, tk), :], ones], axis=1)
        for r in range(n_sub):
            rows = pl.ds(r * rb, rb)
            s = s_ref[slot, rows, :]
            m_prev = m_ref[rows, :]
            m_new = jnp.maximum(m_prev, jnp.max(s, axis=1, keepdims=True))
            alpha = jnp.exp2(m_prev - m_new)
            p = jnp.exp2(s - jnp.tile(m_new, (1, tk // LANES)))
            pv = jnp.dot(p.astype(BF16), v_ext, preferred_element_type=F32)
            acc_ref[rows, :] = acc_ref[rows, :] * jnp.tile(alpha, (1, 2)) + pv
            m_ref[rows, :] = m_new

    scores(0, 0)

    def body(j, carry):
        scores(2 * j + 1, 1)
        consume(2 * j, 0)
        scores(2 * j + 2, 0)
        consume(2 * j + 1, 1)
        return carry

    lax.fori_loop(0, (n_kv - 1) // 2, body, 0)
    consume(n_kv - 1, 0)
    acc = acc_ref[...]
    o1 = acc[:tq, :LANES] / acc[:tq, LANES:]
    o2 = acc[tq:, :LANES] / acc[tq:, LANES:]
    o = o1 - lam_ref[0] * o2
    ms = jnp.mean(o * o, axis=-1, keepdims=True)
    o = o * lax.rsqrt(ms + SUBLN_EPS) * g_ref[...] * out_scale
    o_ref[0] = o.astype(o_ref.dtype)


def _diff_attention(lam, q, k_all, v_all, subln_g, out_scale, tq, tk, n_sub=2):
    b, n, _ = q.shape
    nk = k_all.shape[1]
    assert n % tq == 0 and nk % tk == 0 and tk % MXU_DIM == 0 and (nk // tk) % 2 == 1
    grid_spec = pltpu.PrefetchScalarGridSpec(
        num_scalar_prefetch=1,
        grid=(b, DA_HEADS, n // tq),
        in_specs=[pl.BlockSpec((1, tq, DA_VD), lambda bi, hi, qi, lam_r: (bi, qi, hi)),
                  pl.BlockSpec((1, nk, DA_VD), lambda bi, hi, qi, lam_r: (bi, 0, hi)),
                  pl.BlockSpec((1, nk, DA_VD), lambda bi, hi, qi, lam_r: (bi, 0, hi)),
                  pl.BlockSpec((1, DA_VD), lambda bi, hi, qi, lam_r: (0, 0))],
        out_specs=pl.BlockSpec((1, tq, DA_VD), lambda bi, hi, qi, lam_r: (bi, qi, hi)),
        scratch_shapes=[pltpu.VMEM((2 * tq, LANES), BF16), pltpu.VMEM((2, 2 * tq, tk), F32),
                        pltpu.VMEM((2 * tq, LANES), F32), pltpu.VMEM((2 * tq, 2 * LANES), F32)],
    )
    return pl.pallas_call(
        functools.partial(_diff_attn_kernel, tk=tk, n_sub=n_sub, out_scale=out_scale),
        grid_spec=grid_spec,
        out_shape=jax.ShapeDtypeStruct((b, n, DA_W), BF16),
        compiler_params=_CP,
        name="diff_attention",
    )(lam.reshape(1), q, k_all, v_all, subln_g.reshape(1, DA_VD))


def _fft_stage1_kernel(x_ref, fch_ref, f1_ref, tc_ref, ts_ref, yr_ref, yi_ref, *, nb):
    w = FOURIER_W
    r = x_ref.shape[1]
    for bl in range(nb):
        cols = slice(bl * w, (bl + 1) * w)
        ab = jnp.dot(x_ref[0, :, cols], fch_ref[...], preferred_element_type=F32, precision=HI)
        z = jnp.concatenate([ab[:, :w], ab[:, w:]], axis=0)
        y = jnp.dot(f1_ref[...], z, preferred_element_type=F32, precision=HI)
        yr, yi = y[:r], y[r:]
        tc, ts = tc_ref[:, cols], ts_ref[:, cols]
        yr_ref[0, :, cols] = yr * tc + yi * ts
        yi_ref[0, :, cols] = yi * tc - yr * ts


def _fft_stage2_kernel(yr_ref, yi_ref, f2_ref, o_ref, *, nc):
    w = FOURIER_W
    for cl in range(nc):
        y = jnp.concatenate([yr_ref[0, cl], yi_ref[0, cl]], axis=0)
        o_ref[0, :, cl * w:(cl + 1) * w] = jnp.dot(f2_ref[...], y, preferred_element_type=F32, precision=HI)


def _fourier_mix(f, nb=8):
    bsz, n, w = f.shape
    r = math.isqrt(n)
    assert r * r == n and r % nb == 0 and w == FOURIER_W
    two_pi = 2.0 * math.pi
    k = jnp.arange(r, dtype=jnp.int32)
    ang = two_pi * ((k[:, None] * k[None, :]) % r).astype(F32) / r
    c1, s1 = jnp.cos(ang), jnp.sin(ang)
    kc = jnp.arange(FOURIER_GD, dtype=jnp.int32)
    angc = two_pi * ((kc[:, None] * kc[None, :]) % FOURIER_GD).astype(F32) / FOURIER_GD
    eye_g = jnp.eye(FOURIER_GROUPS, dtype=F32)
    fch = jnp.concatenate([jnp.kron(eye_g, jnp.cos(angc)), jnp.kron(eye_g, jnp.sin(angc))], axis=1)
    f1 = jnp.concatenate([jnp.concatenate([c1, -s1], axis=1), jnp.concatenate([-s1, -c1], axis=1)], axis=0)
    f2 = jnp.concatenate([c1, s1], axis=1) * (1.0 / math.sqrt(n * FOURIER_GD))
    angt = two_pi * ((k[:, None] * k[None, :]) % n).astype(F32) / n
    tc = jnp.repeat(jnp.cos(angt), w, axis=1)
    ts = jnp.repeat(jnp.sin(angt), w, axis=1)
    xv = f.reshape(bsz, r, r * w)
    blk = pl.BlockSpec((1, r, nb * w), lambda bi, j: (bi, 0, j))
    tab = pl.BlockSpec((r, nb * w), lambda bi, j: (0, j))
    full = lambda a: pl.BlockSpec(a.shape, lambda bi, j: (0, 0))
    yshape = jax.ShapeDtypeStruct((bsz, r, r * w), F32)
    yr, yi = pl.pallas_call(
        functools.partial(_fft_stage1_kernel, nb=nb),
        grid=(bsz, r // nb),
        in_specs=[blk, full(fch), full(f1), tab, tab],
        out_specs=[blk, blk],
        out_shape=[yshape, yshape],
        compiler_params=_CP,
        name="fft_stage1",
    )(xv, fch, f1, tc, ts)
    y4 = lambda a: a.reshape(bsz, r, r, w)
    yblk = pl.BlockSpec((1, nb, r, w), lambda bi, j: (bi, j, 0, 0))
    out = pl.pallas_call(
        functools.partial(_fft_stage2_kernel, nc=nb),
        grid=(bsz, r // nb),
        in_specs=[yblk, yblk, full(f2)],
        out_specs=blk,
        out_shape=yshape,
        compiler_params=_CP,
        name="fft_stage2",
    )(y4(yr), y4(yi), f2)
    return out.reshape(bsz, n, w)


def _even_out_kernel(fm_ref, o_ref, w_ref, x_ref, gate_ref, g_ref, shift_ref, scale_ref, wr_ref, br_ref,
                     xo_ref, h_ref, idx_ref, gates_ref):
    y = (jnp.dot(fm_ref[...].astype(BF16), w_ref[:FOURIER_W], preferred_element_type=F32)
         + jnp.dot(o_ref[...], w_ref[FOURIER_W:], preferred_element_type=F32))
    _residual_norm_route(y, x_ref, gate_ref, g_ref, shift_ref, scale_ref, wr_ref, br_ref,
                         xo_ref, h_ref, idx_ref, gates_ref)


def _even_out_proj(fm, o, w, x2d, gate, g, shift, scale, w_router, b_router, rows_per_vec, tm):
    t, d = x2d.shape
    wr, br = _router_operands(w_router, b_router)
    vec = _vec(d, rows_per_vec, tm)
    out_specs, out_shape = _route_out(t, d, tm)
    return pl.pallas_call(
        _even_out_kernel,
        grid=(t // tm,),
        in_specs=[_row(tm, FOURIER_W), _row(tm, DA_W), _full(w.shape), _row(tm, d), vec, _full((1, d)), vec, vec,
                  _full((d, LANES)), _full((1, LANES))],
        out_specs=out_specs, out_shape=out_shape,
        compiler_params=_CP,
        name="even_out_proj",
    )(fm, o, w, x2d, gate, g.reshape(1, d), shift, scale, wr, br)


def _moe_kernel(be_ref, x_ref, w1_ref, perm_ref, b1g_ref, b1l_ref, w2_ref, b2_ref, o_ref, w1g_s, w1l_s, w2_s):
    i = pl.program_id(0)

    @pl.when((i == 0) | (be_ref[i] != be_ref[jnp.maximum(i - 1, 0)]))
    def _():
        for c in range(w1_ref.shape[3] // MXU_DIM):
            blk = jnp.dot(w1_ref[0, 0, :, c * MXU_DIM:(c + 1) * MXU_DIM].astype(BF16), perm_ref[...],
                          preferred_element_type=F32)
            w1g_s[:, c * LANES:(c + 1) * LANES] = blk[:, :LANES].astype(BF16)
            w1l_s[:, c * LANES:(c + 1) * LANES] = blk[:, LANES:].astype(BF16)
        w2_s[...] = w2_ref[0, 0].astype(BF16)

    x = x_ref[...]
    ug = jnp.dot(x, w1g_s[...], preferred_element_type=F32) + b1g_ref[0]
    ul = jnp.dot(x, w1l_s[...], preferred_element_type=F32) + b1l_ref[0]
    glu = jnp.minimum(ug, SWIGLU_LIMIT)
    lin = jnp.clip(ul, -SWIGLU_LIMIT, SWIGLU_LIMIT)
    act = glu * jax.nn.sigmoid(SWIGLU_ALPHA * glu) * (lin + 1.0)
    y = jnp.dot(act.astype(BF16), w2_s[...], preferred_element_type=F32) + b2_ref[0]
    o_ref[...] = y.astype(o_ref.dtype)


def _moe_experts(block_expert, xb, w1_all, b1g, b1l, w2_all, layer, b2):
    n_rows, d = xb.shape
    n_blocks = n_rows // MOE_BLOCK
    src = jnp.arange(MXU_DIM)[:, None]
    dst = jnp.arange(MXU_DIM)[None, :]
    perm = (src == jnp.where(dst < LANES, 2 * dst, 2 * (dst - LANES) + 1)).astype(BF16)
    bspec = lambda s: pl.BlockSpec((1,) + s, lambda i, be: (be[i], 0, 0))
    wspec = lambda s: pl.BlockSpec((1, 1) + s, lambda i, be: (layer, be[i], 0, 0))
    grid_spec = pltpu.PrefetchScalarGridSpec(
        num_scalar_prefetch=1,
        grid=(n_blocks,),
        in_specs=[pl.BlockSpec((MOE_BLOCK, d), lambda i, be: (i, 0)),
                  wspec((d, 2 * D_FF)), pl.BlockSpec((MXU_DIM, MXU_DIM), lambda i, be: (0, 0)),
                  bspec((1, D_FF)), bspec((1, D_FF)), wspec((D_FF, d)), bspec((1, d))],
        out_specs=pl.BlockSpec((MOE_BLOCK, d), lambda i, be: (i, 0)),
        scratch_shapes=[pltpu.VMEM((d, D_FF), BF16), pltpu.VMEM((d, D_FF), BF16), pltpu.VMEM((D_FF, d), BF16)],
    )
    return pl.pallas_call(
        _moe_kernel,
        grid_spec=grid_spec,
        out_shape=jax.ShapeDtypeStruct((n_rows, d), BF16),
        compiler_params=_CP,
        name="moe_experts",
    )(block_expert, xb, w1_all, perm, b1g, b1l, w2_all, b2)


def _moe_ffn(h, top_idx, w1_all, b1, w2_all, b2, layer):
    n_tok, d = h.shape
    n_assign = n_tok * TOP_K
    e_flat = top_idx.reshape(n_assign)
    order = jnp.argsort(e_flat).astype(jnp.int32)
    rank = jnp.argsort(order).astype(jnp.int32)
    counts = jnp.sum(e_flat[:, None] == jnp.arange(N_EXPERTS, dtype=jnp.int32)[None, :], axis=0, dtype=jnp.int32)
    starts = jnp.cumsum(counts) - counts
    padded = (counts + MOE_BLOCK - 1) // MOE_BLOCK * MOE_BLOCK
    pad_ends = jnp.cumsum(padded)
    pad_starts = pad_ends - padded
    n_blocks = -(-(n_assign + N_EXPERTS * (MOE_BLOCK - 1)) // MOE_BLOCK)
    n_rows = n_blocks * MOE_BLOCK
    block_start = jnp.arange(n_blocks, dtype=jnp.int32) * MOE_BLOCK
    block_expert = jnp.minimum(jnp.sum(pad_ends[None, :] <= block_start[:, None], axis=1, dtype=jnp.int32),
                               N_EXPERTS - 1)
    shift = pad_starts - starts
    pos = (shift[e_flat] + rank).reshape(n_tok, TOP_K)
    r = jnp.arange(n_rows, dtype=jnp.int32)
    src = jnp.clip(r - jnp.repeat(shift[block_expert], MOE_BLOCK), 0, n_assign - 1)
    row_tok = order[src] // TOP_K
    xb = h[row_tok]
    yb = _moe_experts(block_expert, xb, w1_all, b1[:, None, 0::2], b1[:, None, 1::2], w2_all, layer,
                      b2[:, None, :])
    return yb, pos


def _combine_kernel(*refs, final):
    y_refs = refs[:TOP_K]
    gt_ref, x_ref, g5_ref = refs[TOP_K:TOP_K + 3]
    o_ref = refs[-1]
    gt = gt_ref[...]
    acc = y_refs[0][...].astype(F32) * gt[:, 0:1]
    for j in range(1, TOP_K):
        acc = acc + y_refs[j][...].astype(F32) * gt[:, j:j + 1]
    xn = x_ref[...] + g5_ref[0] * acc
    if final:
        fg_ref = refs[TOP_K + 3]
        xn = xn * lax.rsqrt(jnp.mean(xn * xn, axis=-1, keepdims=True) + NORM_EPS) * fg_ref[...]
    o_ref[...] = xn


def _moe_combine(yb, pos, gates, x2d, gate5, rows_per_vec, tm, final_g=None):
    t, d = x2d.shape
    ys = [yb[pos[:, j]] for j in range(TOP_K)]
    in_specs = [_row(tm, d)] * TOP_K + [_row(tm, LANES), _row(tm, d), _vec(d, rows_per_vec, tm)]
    args = ys + [gates, x2d, gate5]
    if final_g is not None:
        in_specs.append(_full((1, d)))
        args.append(final_g.reshape(1, d))
    return pl.pallas_call(
        functools.partial(_combine_kernel, final=final_g is not None),
        grid=(t // tm,),
        in_specs=in_specs,
        out_specs=_row(tm, d),
        out_shape=jax.ShapeDtypeStruct((t, d), F32),
        compiler_params=_CP,
        name="moe_combine",
    )(*args)


def _odd_in_kernel(x_ref, g_ref, shift_ref, scale_ref, wm_ref, wab_ref, pm_ref, pab_ref):
    h = _norm_mod(x_ref[...], g_ref, shift_ref, scale_ref)
    pm_ref[...] = jnp.dot(h, wm_ref[...], preferred_element_type=F32).astype(pm_ref.dtype)
    pab_ref[...] = jnp.dot(h, wab_ref[...], preferred_element_type=F32)


def _odd_in_proj(x2d, g, shift, scale, w_main, w_ab, rows_per_vec, tm):
    t, d = x2d.shape
    vec = _vec(d, rows_per_vec, tm)
    return pl.pallas_call(
        _odd_in_kernel,
        grid=(t // tm,),
        in_specs=[_row(tm, d), _full((1, d)), vec, vec, _full(w_main.shape), _full(w_ab.shape)],
        out_specs=[_row(tm, GDN_MAIN_W), _row(tm, LANES)],
        out_shape=[jax.ShapeDtypeStruct((t, GDN_MAIN_W), BF16), jax.ShapeDtypeStruct((t, LANES), F32)],
        compiler_params=_CP,
        name="odd_in_proj",
    )(x2d, g.reshape(1, d), shift, scale, w_main, w_ab)


HALO = SUBLANES_BF16


def _gdn_prep_kernel(pm_ref, pp_ref, pn_ref, ab_ref, cw_ref, alog_ref, dtb_ref, q_ref, k_ref, v_ref, gb_ref):
    i = pl.program_id(1)
    n_i = pl.num_programs(1)
    tm = pm_ref.shape[1]
    half = GDN_CONV // 2
    keep_prev = (i > 0).astype(F32)
    keep_next = (i < n_i - 1).astype(F32)
    for cg in range(GDN_QKV_W // LANES):
        cols = slice(cg * LANES, (cg + 1) * LANES)
        main = pm_ref[0, :, cols].astype(F32)
        ext = jnp.concatenate([pp_ref[0, :, cols].astype(F32) * keep_prev, main,
                               pn_ref[0, :, cols].astype(F32) * keep_next], axis=0)
        acc = main * cw_ref[half:half + 1, cols]
        for j in range(GDN_CONV):
            if j != half:
                sh = pltpu.roll(ext, (half - j) % (tm + 2 * HALO), 0)[HALO:HALO + tm]
                acc = acc + sh * cw_ref[j:j + 1, cols]
        y = acc * jax.nn.sigmoid(acc)
        if cg < 2 * GDN_HK:
            y = y * lax.rsqrt(jnp.sum(y * y, axis=-1, keepdims=True) + 1e-6)
            if cg < GDN_HK:
                q_ref[0, :, cols] = y * (GDN_DK ** -0.5)
            else:
                k_ref[0, :, (cg - GDN_HK) * LANES:(cg - GDN_HK + 1) * LANES] = y
        else:
            v_ref[0, :, (cg - 2 * GDN_HK) * LANES:(cg - 2 * GDN_HK + 1) * LANES] = y
    ab = ab_ref[0]
    xa = ab + dtb_ref[...]
    softplus = jnp.maximum(xa, 0.0) + jnp.log1p(jnp.exp(-jnp.abs(xa)))
    lane = lax.broadcasted_iota(jnp.int32, ab.shape, 1)
    gb_ref[0] = jnp.where(lane < 2 * GDN_HV, -jnp.exp(alog_ref[...]) * softplus, jax.nn.sigmoid(ab))


def _gdn_prep(p_main, p_ab, conv_w, a_log, dt_bias, tm):
    b, t, _ = p_main.shape
    nb = tm // HALO
    last = t // HALO - 1
    pad = lambda a: jnp.zeros((1, LANES), F32).at[0, :2 * GDN_HV].set(a.reshape(-1))
    f = lambda n: jax.ShapeDtypeStruct((b, t, n), F32)
    seq = lambda n: pl.BlockSpec((1, tm, n), lambda bi, i: (bi, i, 0))
    return pl.pallas_call(
        _gdn_prep_kernel,
        grid=(b, t // tm),
        in_specs=[seq(GDN_QKV_W),
                  pl.BlockSpec((1, HALO, GDN_QKV_W), lambda bi, i: (bi, jnp.maximum(i * nb - 1, 0), 0)),
                  pl.BlockSpec((1, HALO, GDN_QKV_W), lambda bi, i: (bi, jnp.minimum((i + 1) * nb, last), 0)),
                  seq(LANES),
                  pl.BlockSpec((GDN_CONV, GDN_QKV_W), lambda bi, i: (0, 0)),
                  pl.BlockSpec((1, LANES), lambda bi, i: (0, 0)),
                  pl.BlockSpec((1, LANES), lambda bi, i: (0, 0))],
        out_specs=[seq(GDN_QK_W), seq(GDN_QK_W), seq(GDN_V_W), seq(LANES)],
        out_shape=[f(GDN_QK_W), f(GDN_QK_W), f(GDN_V_W), f(LANES)],
        compiler_params=_CP,
        name="gdn_prep",
    )(p_main, p_main, p_main, p_ab, conv_w, pad(a_log), pad(dt_bias))


def _gdn_chunk_kernel(q_ref, k_ref, v_ref, gb_ref, s0_ref, o_ref, sfin_ref, s_ref, *, reverse, g_lane, b_lane):
    c = pl.program_id(0)
    n_c = pl.num_programs(0)
    cs = GDN_CHUNK
    rep = GDN_HV // GDN_HK
    nb = q_ref.shape[0]

    @pl.when(c == 0)
    def _():
        s_ref[...] = s0_ref[...].reshape(s_ref.shape)

    row = lax.broadcasted_iota(jnp.int32, (cs, cs), 0)
    col = lax.broadcasted_iota(jnp.int32, (cs, cs), 1)
    incl = (row <= col) if reverse else (row >= col)
    strict = (row < col) if reverse else (row > col)
    eye = (row == col).astype(F32)
    incl_b = incl.astype(BF16)
    last = 0 if reverse else cs - 1
    tn = (((0,), (1,)), ((), ()))

    heads = range(GDN_HV)
    gc, gr, bc, qh, kh, vh = [], [], [], [], [], []
    for b in range(nb):
        gb = gb_ref[b]
        gb_hi = gb.astype(BF16)
        gb_lo = (gb - gb_hi.astype(F32)).astype(BF16)
        g_col = (jnp.dot(incl_b, gb_hi, preferred_element_type=F32)
                 + jnp.dot(incl_b, gb_lo, preferred_element_type=F32))
        g_row = (lax.dot_general(gb_hi, incl_b, tn, preferred_element_type=F32)
                 + lax.dot_general(gb_lo, incl_b, tn, preferred_element_type=F32))
        gc += [g_col[:, g_lane + h:g_lane + h + 1] for h in heads]
        gr += [g_row[g_lane + h:g_lane + h + 1, :] for h in heads]
        bc += [gb[:, b_lane + h:b_lane + h + 1] for h in heads]
        qh += [q_ref[b, :, h * GDN_DK:(h + 1) * GDN_DK] for h in range(GDN_HK)]
        kh += [k_ref[b, :, h * GDN_DK:(h + 1) * GDN_DK] for h in range(GDN_HK)]
        vh += [v_ref[b, :, h * GDN_DV:(h + 1) * GDN_DV] for h in heads]
    gc, gr, bc = jnp.stack(gc), jnp.stack(gr), jnp.stack(bc)
    qh, kh, vh = jnp.stack(qh), jnp.stack(kh), jnp.stack(vh)
    ge = gc[:, last:last + 1, :]
    kh_b = kh.astype(BF16)
    bnt = (((2,), (2,)), ((0,), (0,)))
    kk = lax.dot_general(kh_b, kh_b, bnt, preferred_element_type=F32)
    qk = lax.dot_general(qh.astype(BF16), kh_b, bnt, preferred_element_type=F32)
    kk, qk = jnp.repeat(kk, rep, axis=0), jnp.repeat(qk, rep, axis=0)
    qv, kv = jnp.repeat(qh, rep, axis=0), jnp.repeat(kh, rep, axis=0)

    decay = jnp.where(incl, jnp.exp(jnp.where(incl, gc - gr, 0.0)), 0.0)
    lm = jnp.where(strict, bc * kk * decay, 0.0)
    bmm = lambda a, b: jnp.einsum('hij,hjk->hik', a.astype(BF16), b.astype(BF16), preferred_element_type=F32)
    blk = lambda n: (row // n) == (col // n)
    l0 = jnp.where(blk(16), lm, 0.0)
    p = bmm(l0, l0)
    x = eye - l0
    for _ in range(2):
        xp = bmm(jnp.concatenate([x, p], axis=1), p)
        x = x + xp[:, :cs]
        p = xp[:, cs:]
    x = x + bmm(x, p)
    n = 32
    while n <= cs:
        off = jnp.where(blk(n) & ~blk(n // 2), lm, 0.0)
        x = x - bmm(bmm(x, off), x)
        n *= 2
    eg = jnp.exp(gc)
    uw = bmm(x, jnp.concatenate([vh * bc, kv * (bc * eg)], axis=2))
    u = uw[:, :, :GDN_DV]
    w = uw[:, :, GDN_DV:]
    qg = qv * eg
    intra = jnp.where(incl, qk * decay, 0.0)
    kt = kv * jnp.exp(ge - gc)
    s = s_ref[...]
    wq = bmm(jnp.concatenate([w, qg], axis=1), s)
    v_new = u - wq[:, :cs]
    o = wq[:, cs:] + bmm(intra, v_new)
    for b in range(nb):
        for h in heads:
            o_ref[b, :, h * GDN_DV:(h + 1) * GDN_DV] = o[b * GDN_HV + h]
    s_ref[...] = s * jnp.exp(ge) + jnp.einsum('hck,hcv->hkv', kt.astype(BF16), v_new.astype(BF16),
                                              preferred_element_type=F32)

    @pl.when(c == n_c - 1)
    def _():
        sfin_ref[...] = s_ref[...].reshape(sfin_ref.shape)


def _gdn_scan(q, k, v, gb, s0, reverse):
    b, t, _ = q.shape
    assert t % GDN_CHUNK == 0 and GDN_CHUNK % 32 == 0
    n_c = t // GDN_CHUNK
    cm = (lambda ci: (0, n_c - 1 - ci, 0)) if reverse else (lambda ci: (0, ci, 0))
    d = 1 if reverse else 0
    smap = lambda ci: (0, 0, 0, 0)
    return pl.pallas_call(
        functools.partial(_gdn_chunk_kernel, reverse=reverse, g_lane=d * GDN_HV, b_lane=(2 + d) * GDN_HV),
        grid=(n_c,),
        in_specs=[pl.BlockSpec((b, GDN_CHUNK, GDN_QK_W), cm), pl.BlockSpec((b, GDN_CHUNK, GDN_QK_W), cm),
                  pl.BlockSpec((b, GDN_CHUNK, GDN_V_W), cm), pl.BlockSpec((b, GDN_CHUNK, LANES), cm),
                  pl.BlockSpec((b, GDN_HV, GDN_DK, GDN_DV), smap)],
        out_specs=[pl.BlockSpec((b, GDN_CHUNK, GDN_V_W), cm), pl.BlockSpec((b, GDN_HV, GDN_DK, GDN_DV), smap)],
        out_shape=[jax.ShapeDtypeStruct((b, t, GDN_V_W), F32),
                   jax.ShapeDtypeStruct((b, GDN_HV, GDN_DK, GDN_DV), F32)],
        scratch_shapes=[pltpu.VMEM((b * GDN_HV, GDN_DK, GDN_DV), F32)],
        compiler_params=_CP,
        name="gdn_scan_bwd" if reverse else "gdn_scan_fwd",
    )(q, k, v, gb, s0)


def _odd_out_kernel(of_ref, ob_ref, z_ref, ng_ref, w_ref, x_ref, gate_ref, g_ref, shift_ref, scale_ref,
                    wr_ref, br_ref, xo_ref, h_ref, idx_ref, gates_ref):
    parts = []
    for h in range(GDN_HV):
        cols = slice(h * GDN_DV, (h + 1) * GDN_DV)
        o = of_ref[:, cols] + ob_ref[:, cols]
        z = z_ref[:, cols].astype(F32)
        o = o * lax.rsqrt(jnp.mean(o * o, axis=-1, keepdims=True) + NORM_EPS) * ng_ref[...]
        parts.append((o * (z * jax.nn.sigmoid(z))).astype(BF16))
    y = jnp.dot(jnp.concatenate(parts, axis=1), w_ref[...], preferred_element_type=F32)
    _residual_norm_route(y, x_ref, gate_ref, g_ref, shift_ref, scale_ref, wr_ref, br_ref,
                         xo_ref, h_ref, idx_ref, gates_ref)


def _odd_out_proj(o_f, o_b, p_main, norm_g, w, x2d, gate, g, shift, scale, w_router, b_router, rows_per_vec, tm):
    t, d = x2d.shape
    wr, br = _router_operands(w_router, b_router)
    vec = _vec(d, rows_per_vec, tm)
    out_specs, out_shape = _route_out(t, d, tm)
    return pl.pallas_call(
        _odd_out_kernel,
        grid=(t // tm,),
        in_specs=[_row(tm, GDN_V_W), _row(tm, GDN_V_W),
                  pl.BlockSpec((tm, GDN_V_W), lambda i: (i, GDN_QKV_W // GDN_V_W)),
                  _full((1, GDN_DV)), _full(w.shape), _row(tm, d), vec, _full((1, d)), vec, vec,
                  _full((d, LANES)), _full((1, LANES))],
        out_specs=out_specs, out_shape=out_shape,
        compiler_params=_CP,
        name="odd_out_proj",
    )(o_f, o_b, p_main, norm_g.reshape(1, GDN_DV), w, x2d, gate, g.reshape(1, d), shift, scale, wr, br)


def _split_mod(mv, bsz):
    d = mv.shape[1] // N_MOD
    lat = [mv[:bsz, j * d:(j + 1) * d][:, None, :] for j in range(N_MOD)]
    ctx = [mv[bsz:bsz + 1, j * d:(j + 1) * d][:, None, :] for j in range(N_MOD)]
    return lat, ctx


def _even_layer(x2d, xc2d, mod, mod_c, norm1_g, norm2_g, w_in, w_out, lam_p, subln_g, lam_init, cos, sin,
                w_router, b_router, bsz, n_lat, n_ctx):
    d = x2d.shape[1]
    w_in_b = w_in.astype(BF16)
    w_out_b = w_out.astype(BF16)
    q_scale = DA_DH ** -0.5 * math.log2(math.e)
    f, q, k_all, v_all = _even_in_proj(x2d, norm1_g, mod[0], mod[1], w_in_b, cos, sin, bsz, n_lat, n_lat + n_ctx,
                                       q_scale, TM_PROJ)
    ones, zeros = jnp.ones((n_ctx, ROPE_AXIS_DIM), F32), jnp.zeros((n_ctx, ROPE_AXIS_DIM), F32)
    fc, qc, kc, vc = _even_in_proj(xc2d, norm1_g, mod_c[0], mod_c[1], w_in_b, ones, zeros, bsz, n_ctx, n_ctx,
                                   q_scale, n_ctx)
    lp = lam_p.astype(F32)
    lam = jnp.exp(jnp.sum(lp[0] * lp[1])) - jnp.exp(jnp.sum(lp[2] * lp[3])) + lam_init
    k_all = lax.dynamic_update_slice(k_all, kc, (0, n_lat, 0))
    v_all = lax.dynamic_update_slice(v_all, vc, (0, n_lat, 0))
    o = _diff_attention(lam, q, k_all, v_all, subln_g, 1.0 - lam_init, TQ_ATTN, TK_ATTN)
    oc = _diff_attention(lam, qc, kc, vc, subln_g, 1.0 - lam_init, n_ctx, n_ctx)
    fm = _fourier_mix(f.reshape(bsz, n_lat, FOURIER_W)).reshape(bsz * n_lat, FOURIER_W)
    fmc = _fourier_mix(fc.reshape(bsz, n_ctx, FOURIER_W)).reshape(bsz * n_ctx, FOURIER_W)
    lat = _even_out_proj(fm, o.reshape(bsz * n_lat, DA_W), w_out_b, x2d, mod[2], norm2_g, mod[3], mod[4],
                         w_router, b_router, n_lat, TM_PROJ)
    ctx = _even_out_proj(fmc, oc.reshape(bsz * n_ctx, DA_W), w_out_b, xc2d, mod_c[2], norm2_g, mod_c[3], mod_c[4],
                         w_router, b_router, bsz * n_ctx, n_ctx)
    return lat, ctx


def _odd_layer(x2d, xc2d, mod, mod_c, norm1_g, norm2_g, w_in, conv_w, a_log, dt_bias, norm_g, w_out,
               w_router, b_router, bsz, n_lat, n_ctx):
    d = x2d.shape[1]
    w_main = w_in[:, :GDN_MAIN_W].astype(BF16)
    w_ab = jnp.zeros((d, LANES), BF16).at[:, :4 * GDN_HV].set(w_in[:, GDN_MAIN_W:].astype(BF16))
    p_main, p_ab = _odd_in_proj(x2d, norm1_g, mod[0], mod[1], w_main, w_ab, n_lat, TM_PROJ)
    q, k, v, gb = _gdn_prep(p_main.reshape(bsz, n_lat, GDN_MAIN_W), p_ab.reshape(bsz, n_lat, LANES),
                            conv_w, a_log, dt_bias, TM_GDN)
    pc_main, pc_ab = _odd_in_proj(xc2d, norm1_g, mod_c[0], mod_c[1], w_main, w_ab, bsz * n_ctx, n_ctx)
    qc, kc, vc, gbc = _gdn_prep(pc_main.reshape(bsz, n_ctx, GDN_MAIN_W), pc_ab.reshape(bsz, n_ctx, LANES),
                                conv_w, a_log, dt_bias, n_ctx)
    s0 = jnp.zeros((bsz, GDN_HV, GDN_DK, GDN_DV), F32)
    _, sc_f = _gdn_scan(qc, kc, vc, gbc, s0, False)
    o_f, _ = _gdn_scan(q, k, v, gb, sc_f, False)
    _, sc_b = _gdn_scan(qc, kc, vc, gbc, s0, True)
    o_b, _ = _gdn_scan(q, k, v, gb, sc_b, True)
    return _odd_out_proj(o_f.reshape(bsz * n_lat, GDN_V_W), o_b.reshape(bsz * n_lat, GDN_V_W), p_main, norm_g,
                         w_out.astype(BF16), x2d, mod[2], norm2_g, mod[3], mod[4], w_router, b_router, n_lat, TM_GDN)


def kernel(x, c, ctx, c_ctx, norm1_g, norm2_g, w_mod, b_mod, ev_w_in, ev_w_out, ev_lam, ev_subln_g,
           od_w_in, od_conv_w, od_a_log, od_dt_bias, od_norm_g, od_w_out,
           moe_w_router, moe_b_router, moe_w1, moe_b1, moe_w2, moe_b2, final_g):
    bsz, n_lat, d = x.shape
    n_ctx = ctx.shape[1]
    assert w_mod.shape[0] == 2, "kernel is written for one even (attention) and one odd (DeltaNet) layer"
    t_lat = bsz * n_lat
    cos, sin = _axial_rope_tables(n_lat // GRID_W)
    c_rows = jnp.zeros((8, d), F32).at[:bsz].set(c).at[bsz].set(c_ctx)
    x2d = x.reshape(t_lat, d)
    xc2d = ctx.reshape(bsz * n_ctx, d)

    mod, mod_c = _split_mod(_mod_vectors(c_rows, w_mod, b_mod, 0), bsz)
    (x2d, h2, idx, gates), (xc2d, h2c, idx_c, gates_c) = _even_layer(
        x2d, xc2d, mod, mod_c, norm1_g[0], norm2_g[0], ev_w_in[0], ev_w_out[0], ev_lam[0], ev_subln_g[0],
        _diff_lambda_init(0), cos, sin, moe_w_router[0], moe_b_router[0], bsz, n_lat, n_ctx)
    top_idx = jnp.concatenate([idx[:, :TOP_K], idx_c[:, :TOP_K]], axis=0)
    yb, pos = _moe_ffn(jnp.concatenate([h2, h2c], axis=0), top_idx, moe_w1, moe_b1[0], moe_w2, moe_b2[0], 0)
    x2d = _moe_combine(yb, pos[:t_lat], gates, x2d, mod[5], n_lat, TM_PROJ)
    xc2d = _moe_combine(yb, pos[t_lat:], gates_c, xc2d, mod_c[5], bsz * n_ctx, n_ctx)

    mod, mod_c = _split_mod(_mod_vectors(c_rows, w_mod, b_mod, 1), bsz)
    x2d, h2, idx, gates = _odd_layer(x2d, xc2d, mod, mod_c, norm1_g[1], norm2_g[1], od_w_in[0], od_conv_w[0],
                                     od_a_log[0], od_dt_bias[0], od_norm_g[0], od_w_out[0],
                                     moe_w_router[1], moe_b_router[1], bsz, n_lat, n_ctx)
    yb, pos = _moe_ffn(h2, idx[:, :TOP_K], moe_w1, moe_b1[1], moe_w2, moe_b2[1], 1)
    return _moe_combine(yb, pos, gates, x2d, mod[5], n_lat, TM_PROJ, final_g=final_g).reshape(bsz, n_lat, d)
```

```python
import functools
import math

import jax
import jax.numpy as jnp
from jax import lax
from jax.experimental import pallas as pl
from jax.experimental.pallas import tpu as pltpu

D_MODEL = 1024
N_MOD = 6
NORM_EPS = 1e-6
GRID_W = 64

FOURIER_GROUPS = 4
FOURIER_GD = 64
FOURIER_W = FOURIER_GROUPS * FOURIER_GD
DA_HEADS = 6
DA_DH = 64
DA_VD = 2 * DA_DH
DA_W = DA_HEADS * DA_VD
ROPE_BASE = 10000.0
ROPE_AXIS_DIM = DA_DH // 2
SUBLN_EPS = 1e-5

GDN_HK = 8
GDN_HV = 16
GDN_DK = 128
GDN_DV = 128
GDN_QK_W = GDN_HK * GDN_DK
GDN_V_W = GDN_HV * GDN_DV
GDN_QKV_W = 2 * GDN_QK_W + GDN_V_W
GDN_MAIN_W = GDN_QKV_W + GDN_V_W
GDN_CONV = 5
GDN_CHUNK = 64

N_EXPERTS = 32
TOP_K = 4
D_FF = 1024
SWIGLU_LIMIT = 7.0
SWIGLU_ALPHA = 1.702
MOE_BLOCK = 512
MOE_PARTS = 4

LANES = 128
SUBLANES_BF16 = 16
MXU_DIM = 256
VMEM_LIMIT = 56 * 1024 * 1024
BF16 = jnp.bfloat16
F32 = jnp.float32
HI = lax.Precision.HIGHEST

TM_PROJ = 512
TM_GDN = 256
TQ_ATTN, TK_ATTN = 512, 1280

_CP = pltpu.CompilerParams(vmem_limit_bytes=VMEM_LIMIT)


def _diff_lambda_init(layer_idx):
    return 0.8 - 0.6 * math.exp(-0.3 * layer_idx)


def _axial_rope_tables(rows):
    t = jnp.arange(rows * GRID_W, dtype=jnp.int32)
    row = (t // GRID_W).astype(F32)
    col = (t % GRID_W).astype(F32)
    inv = ROPE_BASE ** (-jnp.arange(0, ROPE_AXIS_DIM, 2, dtype=F32) / ROPE_AXIS_DIM)
    ang = jnp.concatenate([row[:, None] * inv, col[:, None] * inv], axis=-1)
    return jnp.cos(ang), jnp.sin(ang)


def _norm_mod(x, g_ref, shift_ref, scale_ref):
    h = x * lax.rsqrt(jnp.mean(x * x, axis=-1, keepdims=True) + NORM_EPS) * g_ref[...]
    return (h * (1.0 + scale_ref[0]) + shift_ref[0]).astype(BF16)


def _route(logits):
    lane = lax.broadcasted_iota(jnp.int32, logits.shape, 1)
    lane_f = lane.astype(F32)
    neg = jnp.float32(-jnp.inf)
    rest = jnp.where(lane < N_EXPERTS, logits, neg)
    idx = jnp.zeros(logits.shape, F32)
    val = jnp.full(logits.shape, neg, F32)
    for j in range(TOP_K):
        m = jnp.max(rest, axis=-1, keepdims=True)
        sel = jnp.min(jnp.where(rest == m, lane_f, float(LANES)), axis=-1, keepdims=True)
        idx = jnp.where(lane == j, sel, idx)
        val = jnp.where(lane == j, m, val)
        rest = jnp.where(lane_f == sel, neg, rest)
    e = jnp.exp(val - val[:, 0:1])
    return idx, e / jnp.sum(e, axis=-1, keepdims=True)


def _residual_norm_route(y, x_ref, gate_ref, g_ref, shift_ref, scale_ref, wr_ref, br_ref,
                         xo_ref, h_ref, idx_ref, gates_ref):
    xn = x_ref[...] + gate_ref[0] * y
    xo_ref[...] = xn
    h = _norm_mod(xn, g_ref, shift_ref, scale_ref)
    h_ref[...] = h
    idx, gates = _route(jnp.dot(h, wr_ref[...], preferred_element_type=F32) + br_ref[...])
    idx_ref[...] = idx.astype(jnp.int32)
    gates_ref[...] = gates


def _row(tm, n):
    return pl.BlockSpec((tm, n), lambda i: (i, 0))


def _full(shape):
    return pl.BlockSpec(shape, lambda i: (0,) * len(shape))


def _vec(d, rows_per_vec, tm):
    per = rows_per_vec // tm
    return pl.BlockSpec((1, 1, d), lambda i: (i // per, 0, 0))


def _router_operands(w_router, b_router):
    d = w_router.shape[0]
    wr = jnp.zeros((d, LANES), BF16).at[:, :N_EXPERTS].set(w_router.astype(BF16))
    br = jnp.zeros((1, LANES), F32).at[0, :N_EXPERTS].set(b_router)
    return wr, br


def _route_out(t, d, tm):
    specs = [_row(tm, d), _row(tm, d), _row(tm, LANES), _row(tm, LANES)]
    shapes = [jax.ShapeDtypeStruct((t, d), F32), jax.ShapeDtypeStruct((t, d), BF16),
              jax.ShapeDtypeStruct((t, LANES), jnp.int32), jax.ShapeDtypeStruct((t, LANES), F32)]
    return specs, shapes


def _mod_kernel(c_ref, w_ref, b_ref, o_ref):
    c = c_ref[...]
    s = (c * jax.nn.sigmoid(c)).astype(BF16)
    o_ref[...] = jnp.dot(s, w_ref[0].astype(BF16), preferred_element_type=F32) + b_ref[0]


def _mod_vectors(c_rows, w_mod, b_mod, layer):
    d = c_rows.shape[1]
    return pl.pallas_call(
        _mod_kernel,
        grid=(N_MOD,),
        in_specs=[pl.BlockSpec(c_rows.shape, lambda n: (0, 0)),
                  pl.BlockSpec((1, d, d), lambda n: (layer, 0, n)),
                  pl.BlockSpec((1, 1, d), lambda n: (layer, 0, n))],
        out_specs=pl.BlockSpec((c_rows.shape[0], d), lambda n: (0, n)),
        out_shape=jax.ShapeDtypeStruct((c_rows.shape[0], N_MOD * d), F32),
        compiler_params=_CP,
        name="mod_vectors",
    )(c_rows, w_mod, b_mod.reshape(b_mod.shape[0], 1, -1))


def _even_in_kernel(x_ref, g_ref, shift_ref, scale_ref, w_ref, rc_ref, rs_ref, f_ref, q_ref, k_ref, v_ref, *, q_scale):
    h = _norm_mod(x_ref[...], g_ref, shift_ref, scale_ref)
    p = jnp.dot(h, w_ref[...], preferred_element_type=F32)
    f_ref[...] = p[:, :FOURIER_W]
    rc, rs = rc_ref[...], rs_ref[...]
    lane = lax.broadcasted_iota(jnp.int32, rc.shape, 1)
    first = (lane % DA_DH) < ROPE_AXIS_DIM

    def rope(t):
        partner = jnp.where(first, pltpu.roll(t, LANES - ROPE_AXIS_DIM, 1), pltpu.roll(t, ROPE_AXIS_DIM, 1))
        return t * rc + partner * rs

    for hd in range(DA_HEADS):
        cq = slice(FOURIER_W + hd * DA_VD, FOURIER_W + (hd + 1) * DA_VD)
        ck = slice(FOURIER_W + DA_W + hd * DA_VD, FOURIER_W + DA_W + (hd + 1) * DA_VD)
        q_ref[0, :, hd * DA_VD:(hd + 1) * DA_VD] = (rope(p[:, cq]) * q_scale).astype(BF16)
        k_ref[0, :, hd * DA_VD:(hd + 1) * DA_VD] = rope(p[:, ck]).astype(BF16)
    v_ref[0] = p[:, FOURIER_W + 2 * DA_W:].astype(BF16)


def _even_in_proj(x2d, g, shift, scale, w, cos, sin, bsz, n_seq, n_keys, q_scale, tm):
    t, d = x2d.shape
    bpb = n_seq // tm
    rc = jnp.tile(cos, (1, LANES // ROPE_AXIS_DIM))
    rs = jnp.tile(jnp.concatenate([-sin, sin], axis=1), (1, LANES // DA_DH))
    tab = pl.BlockSpec((tm, LANES), lambda i: (i % bpb, 0))
    seq = pl.BlockSpec((1, tm, DA_W), lambda i: (i // bpb, i % bpb, 0))
    vec = _vec(d, n_seq * (bsz // shift.shape[0]), tm)
    return pl.pallas_call(
        functools.partial(_even_in_kernel, q_scale=q_scale),
        grid=(t // tm,),
        in_specs=[_row(tm, d), _full((1, d)), vec, vec, _full(w.shape), tab, tab],
        out_specs=[_row(tm, FOURIER_W), seq, seq, seq],
        out_shape=[jax.ShapeDtypeStruct((t, FOURIER_W), F32), jax.ShapeDtypeStruct((bsz, n_seq, DA_W), BF16),
                   jax.ShapeDtypeStruct((bsz, n_keys, DA_W), BF16), jax.ShapeDtypeStruct((bsz, n_keys, DA_W), BF16)],
        compiler_params=_CP,
        name="even_in_proj",
    )(x2d, g.reshape(1, d), shift, scale, w, rc, rs)


def _diff_attn_kernel(lam_ref, q_ref, k_ref, v_ref, g_ref, o_ref, qs_ref, s_ref, m_ref, acc_ref, *,
                      tk, n_sub, out_scale):
    tq = q_ref.shape[1]
    n_kv = k_ref.shape[1] // tk
    rb = 2 * tq // n_sub
    q = q_ref[0]
    lane = lax.broadcasted_iota(jnp.int32, q.shape, 1)
    zero = jnp.zeros_like(q)
    qs_ref[:tq] = jnp.where(lane < DA_DH, q, zero)
    qs_ref[tq:] = jnp.where(lane >= DA_DH, q, zero)
    m_ref[...] = jnp.full(m_ref.shape, -1e30, F32)
    acc_ref[...] = jnp.zeros(acc_ref.shape, F32)
    ones = jnp.ones((tk, LANES), BF16)

    def scores(i, slot):
        off = pl.multiple_of(i * tk, tk)
        k = k_ref[0, pl.ds(off, tk), :]
        s_ref[slot] = lax.dot_general(qs_ref[...], k, (((1,), (1,)), ((), ())), preferred_element_type=F32)

    def consume(i, slot):
        off = pl.multiple_of(i * tk, tk)
        v_ext = jnp.concatenate([v_ref[0, pl.ds(off, tk), :], ones], axis=1)
        for r in range(n_sub):
            rows = pl.ds(r * rb, rb)
            s = s_ref[slot, rows, :]
            m_prev = m_ref[rows, :]
            m_new = jnp.maximum(m_prev, jnp.max(s, axis=1, keepdims=True))
            alpha = jnp.exp2(m_prev - m_new)
            p = jnp.exp2(s - jnp.tile(m_new, (1, tk // LANES)))
            pv = jnp.dot(p.astype(BF16), v_ext, preferred_element_type=F32)
            acc_ref[rows, :] = acc_ref[rows, :] * jnp.tile(alpha, (1, 2)) + pv
            m_ref[rows, :] = m_new

    scores(0, 0)

    def body(j, carry):
        scores(2 * j + 1, 1)
        consume(2 * j, 0)
        scores(2 * j + 2, 0)
        consume(2 * j + 1, 1)
        return carry

    lax.fori_loop(0, (n_kv - 1) // 2, body, 0)
    consume(n_kv - 1, 0)
    acc = acc_ref[...]
    o1 = acc[:tq, :LANES] / acc[:tq, LANES:]
    o2 = acc[tq:, :LANES] / acc[tq:, LANES:]
    o = o1 - lam_ref[0] * o2
    ms = jnp.mean(o * o, axis=-1, keepdims=True)
    o = o * lax.rsqrt(ms + SUBLN_EPS) * g_ref[...] * out_scale
    o_ref[0] = o.astype(o_ref.dtype)


def _diff_attention(lam, q, k_all, v_all, subln_g, out_scale, tq, tk, n_sub=2):
    b, n, _ = q.shape
    nk = k_all.shape[1]
    assert n % tq == 0 and nk % tk == 0 and tk % MXU_DIM == 0 and (nk // tk) % 2 == 1
    grid_spec = pltpu.PrefetchScalarGridSpec(
        num_scalar_prefetch=1,
        grid=(b, DA_HEADS, n // tq),
        in_specs=[pl.BlockSpec((1, tq, DA_VD), lambda bi, hi, qi, lam_r: (bi, qi, hi)),
                  pl.BlockSpec((1, nk, DA_VD), lambda bi, hi, qi, lam_r: (bi, 0, hi)),
                  pl.BlockSpec((1, nk, DA_VD), lambda bi, hi, qi, lam_r: (bi, 0, hi)),
                  pl.BlockSpec((1, DA_VD), lambda bi, hi, qi, lam_r: (0, 0))],
        out_specs=pl.BlockSpec((1, tq, DA_VD), lambda bi, hi, qi, lam_r: (bi, qi, hi)),
        scratch_shapes=[pltpu.VMEM((2 * tq, LANES), BF16), pltpu.VMEM((2, 2 * tq, tk), F32),
                        pltpu.VMEM((2 * tq, LANES), F32), pltpu.VMEM((2 * tq, 2 * LANES), F32)],
    )
    return pl.pallas_call(
        functools.partial(_diff_attn_kernel, tk=tk, n_sub=n_sub, out_scale=out_scale),
        grid_spec=grid_spec,
        out_shape=jax.ShapeDtypeStruct((b, n, DA_W), BF16),
        compiler_params=_CP,
        name="diff_attention",
    )(lam.reshape(1), q, k_all, v_all, subln_g.reshape(1, DA_VD))


def _fft_stage1_kernel(x_ref, fch_ref, f1_ref, tc_ref, ts_ref, yr_ref, yi_ref, *, nb):
    w = FOURIER_W
    r = x_ref.shape[1]
    for bl in range(nb):
        cols = slice(bl * w, (bl + 1) * w)
        ab = jnp.dot(x_ref[0, :, cols], fch_ref[...], preferred_element_type=F32, precision=HI)
        z = jnp.concatenate([ab[:, :w], ab[:, w:]], axis=0)
        y = jnp.dot(f1_ref[...], z, preferred_element_type=F32, precision=HI)
        yr, yi = y[:r], y[r:]
        tc, ts = tc_ref[:, cols], ts_ref[:, cols]
        yr_ref[0, :, cols] = yr * tc + yi * ts
        yi_ref[0, :, cols] = yi * tc - yr * ts


def _fft_stage2_kernel(yr_ref, yi_ref, f2_ref, o_ref, *, nc):
    w = FOURIER_W
    for cl in range(nc):
        y = jnp.concatenate([yr_ref[0, cl], yi_ref[0, cl]], axis=0)
        o_ref[0, :, cl * w:(cl + 1) * w] = jnp.dot(f2_ref[...], y, preferred_element_type=F32, precision=HI)


def _fourier_mix(f, nb=8):
    bsz, n, w = f.shape
    r = math.isqrt(n)
    assert r * r == n and r % nb == 0 and w == FOURIER_W
    two_pi = 2.0 * math.pi
    k = jnp.arange(r, dtype=jnp.int32)
    ang = two_pi * ((k[:, None] * k[None, :]) % r).astype(F32) / r
    c1, s1 = jnp.cos(ang), jnp.sin(ang)
    kc = jnp.arange(FOURIER_GD, dtype=jnp.int32)
    angc = two_pi * ((kc[:, None] * kc[None, :]) % FOURIER_GD).astype(F32) / FOURIER_GD
    eye_g = jnp.eye(FOURIER_GROUPS, dtype=F32)
    fch = jnp.concatenate([jnp.kron(eye_g, jnp.cos(angc)), jnp.kron(eye_g, jnp.sin(angc))], axis=1)
    f1 = jnp.concatenate([jnp.concatenate([c1, -s1], axis=1), jnp.concatenate([-s1, -c1], axis=1)], axis=0)
    f2 = jnp.concatenate([c1, s1], axis=1) * (1.0 / math.sqrt(n * FOURIER_GD))
    angt = two_pi * ((k[:, None] * k[None, :]) % n).astype(F32) / n
    tc = jnp.repeat(jnp.cos(angt), w, axis=1)
    ts = jnp.repeat(jnp.sin(angt), w, axis=1)
    xv = f.reshape(bsz, r, r * w)
    blk = pl.BlockSpec((1, r, nb * w), lambda bi, j: (bi, 0, j))
    tab = pl.BlockSpec((r, nb * w), lambda bi, j: (0, j))
    full = lambda a: pl.BlockSpec(a.shape, lambda bi, j: (0, 0))
    yshape = jax.ShapeDtypeStruct((bsz, r, r * w), F32)
    yr, yi = pl.pallas_call(
        functools.partial(_fft_stage1_kernel, nb=nb),
        grid=(bsz, r // nb),
        in_specs=[blk, full(fch), full(f1), tab, tab],
        out_specs=[blk, blk],
        out_shape=[yshape, yshape],
        compiler_params=_CP,
        name="fft_stage1",
    )(xv, fch, f1, tc, ts)
    y4 = lambda a: a.reshape(bsz, r, r, w)
    yblk = pl.BlockSpec((1, nb, r, w), lambda bi, j: (bi, j, 0, 0))
    out = pl.pallas_call(
        functools.partial(_fft_stage2_kernel, nc=nb),
        grid=(bsz, r // nb),
        in_specs=[yblk, yblk, full(f2)],
        out_specs=blk,
        out_shape=yshape,
        compiler_params=_CP,
        name="fft_stage2",
    )(y4(yr), y4(yi), f2)
    return out.reshape(bsz, n, w)


def _even_out_kernel(fm_ref, o_ref, w_ref, x_ref, gate_ref, g_ref, shift_ref, scale_ref, wr_ref, br_ref,
                     xo_ref, h_ref, idx_ref, gates_ref):
    y = (jnp.dot(fm_ref[...].astype(BF16), w_ref[:FOURIER_W], preferred_element_type=F32)
         + jnp.dot(o_ref[...], w_ref[FOURIER_W:], preferred_element_type=F32))
    _residual_norm_route(y, x_ref, gate_ref, g_ref, shift_ref, scale_ref, wr_ref, br_ref,
                         xo_ref, h_ref, idx_ref, gates_ref)


def _even_out_proj(fm, o, w, x2d, gate, g, shift, scale, w_router, b_router, rows_per_vec, tm):
    t, d = x2d.shape
    wr, br = _router_operands(w_router, b_router)
    vec = _vec(d, rows_per_vec, tm)
    out_specs, out_shape = _route_out(t, d, tm)
    return pl.pallas_call(
        _even_out_kernel,
        grid=(t // tm,),
        in_specs=[_row(tm, FOURIER_W), _row(tm, DA_W), _full(w.shape), _row(tm, d), vec, _full((1, d)), vec, vec,
                  _full((d, LANES)), _full((1, LANES))],
        out_specs=out_specs, out_shape=out_shape,
        compiler_params=_CP,
        name="even_out_proj",
    )(fm, o, w, x2d, gate, g.reshape(1, d), shift, scale, wr, br)


def _moe_kernel(be_ref, x_ref, w1_ref, perm_ref, b1g_ref, b1l_ref, w2_ref, b2_ref, *rest):
    o_ref, w1g_s, w1l_s, w2_s = rest[-4:]
    i = pl.program_id(0)

    @pl.when((i == 0) | (be_ref[i] != be_ref[jnp.maximum(i - 1, 0)]))
    def _():
        for c in range(w1_ref.shape[3] // MXU_DIM):
            blk = jnp.dot(w1_ref[0, 0, :, c * MXU_DIM:(c + 1) * MXU_DIM].astype(BF16), perm_ref[...],
                          preferred_element_type=F32)
            w1g_s[:, c * LANES:(c + 1) * LANES] = blk[:, :LANES].astype(BF16)
            w1l_s[:, c * LANES:(c + 1) * LANES] = blk[:, LANES:].astype(BF16)
        w2_s[...] = w2_ref[0, 0].astype(BF16)

    x = x_ref[...]
    ug = jnp.dot(x, w1g_s[...], preferred_element_type=F32) + b1g_ref[0]
    ul = jnp.dot(x, w1l_s[...], preferred_element_type=F32) + b1l_ref[0]
    glu = jnp.minimum(ug, SWIGLU_LIMIT)
    lin = jnp.clip(ul, -SWIGLU_LIMIT, SWIGLU_LIMIT)
    act = glu * jax.nn.sigmoid(SWIGLU_ALPHA * glu) * (lin + 1.0)
    y = jnp.dot(act.astype(BF16), w2_s[...], preferred_element_type=F32) + b2_ref[0]
    o_ref[...] = y.astype(o_ref.dtype)


def _moe_experts(block_expert, xb, w1_all, b1g, b1l, w2_all, layer, b2, yb_prev, first_block, n_blocks_total):
    n_rows, d = xb.shape
    n_blocks = n_rows // MOE_BLOCK
    src = jnp.arange(MXU_DIM)[:, None]
    dst = jnp.arange(MXU_DIM)[None, :]
    perm = (src == jnp.where(dst < LANES, 2 * dst, 2 * (dst - LANES) + 1)).astype(BF16)
    bspec = lambda s: pl.BlockSpec((1,) + s, lambda i, be: (be[i], 0, 0))
    wspec = lambda s: pl.BlockSpec((1, 1) + s, lambda i, be: (layer, be[i], 0, 0))
    in_specs = [pl.BlockSpec((MOE_BLOCK, d), lambda i, be: (i, 0)),
                wspec((d, 2 * D_FF)), pl.BlockSpec((MXU_DIM, MXU_DIM), lambda i, be: (0, 0)),
                bspec((1, D_FF)), bspec((1, D_FF)), wspec((D_FF, d)), bspec((1, d))]
    args = [block_expert, xb, w1_all, perm, b1g, b1l, w2_all, b2]
    aliases = {}
    if yb_prev is not None:
        in_specs.append(pl.BlockSpec(memory_space=pl.ANY))
        aliases = {len(args): 0}
        args.append(yb_prev)
    grid_spec = pltpu.PrefetchScalarGridSpec(
        num_scalar_prefetch=1,
        grid=(n_blocks,),
        in_specs=in_specs,
        out_specs=pl.BlockSpec((MOE_BLOCK, d), lambda i, be: (first_block + i, 0)),
        scratch_shapes=[pltpu.VMEM((d, D_FF), BF16), pltpu.VMEM((d, D_FF), BF16), pltpu.VMEM((D_FF, d), BF16)],
    )
    return pl.pallas_call(
        _moe_kernel,
        grid_spec=grid_spec,
        out_shape=jax.ShapeDtypeStruct((n_blocks_total * MOE_BLOCK, d), BF16),
        input_output_aliases=aliases,
        compiler_params=_CP,
        name="moe_experts",
    )(*args)


def _moe_ffn(h, top_idx, w1_all, b1, w2_all, b2, layer):
    n_tok, d = h.shape
    n_assign = n_tok * TOP_K
    e_flat = top_idx.reshape(n_assign)
    order = jnp.argsort(e_flat).astype(jnp.int32)
    rank = jnp.argsort(order).astype(jnp.int32)
    counts = jnp.sum(e_flat[:, None] == jnp.arange(N_EXPERTS, dtype=jnp.int32)[None, :], axis=0, dtype=jnp.int32)
    starts = jnp.cumsum(counts) - counts
    padded = (counts + MOE_BLOCK - 1) // MOE_BLOCK * MOE_BLOCK
    pad_ends = jnp.cumsum(padded)
    pad_starts = pad_ends - padded
    n_blocks = -(-(n_assign + N_EXPERTS * (MOE_BLOCK - 1)) // MOE_BLOCK)
    n_rows = n_blocks * MOE_BLOCK
    block_start = jnp.arange(n_blocks, dtype=jnp.int32) * MOE_BLOCK
    block_expert = jnp.minimum(jnp.sum(pad_ends[None, :] <= block_start[:, None], axis=1, dtype=jnp.int32),
                               N_EXPERTS - 1)
    shift = pad_starts - starts
    pos = (shift[e_flat] + rank).reshape(n_tok, TOP_K)
    r = jnp.arange(n_rows, dtype=jnp.int32)
    src = jnp.clip(r - jnp.repeat(shift[block_expert], MOE_BLOCK), 0, n_assign - 1)
    row_tok = order[src] // TOP_K
    assert n_blocks % MOE_PARTS == 0
    nbp = n_blocks // MOE_PARTS
    yb = None
    for part in range(MOE_PARTS):
        blocks = slice(part * nbp, (part + 1) * nbp)
        xb = h[row_tok[part * nbp * MOE_BLOCK:(part + 1) * nbp * MOE_BLOCK]]
        yb = _moe_experts(block_expert[blocks], xb, w1_all, b1[:, None, 0::2], b1[:, None, 1::2], w2_all, layer,
                          b2[:, None, :], yb, part * nbp, n_blocks)
    return yb, pos


def _combine_kernel(*refs, final):
    y_refs = refs[:TOP_K]
    gt_ref, x_ref, g5_ref = refs[TOP_K:TOP_K + 3]
    o_ref = refs[-1]
    gt = gt_ref[...]
    acc = y_refs[0][...].astype(F32) * gt[:, 0:1]
    for j in range(1, TOP_K):
        acc = acc + y_refs[j][...].astype(F32) * gt[:, j:j + 1]
    xn = x_ref[...] + g5_ref[0] * acc
    if final:
        fg_ref = refs[TOP_K + 3]
        xn = xn * lax.rsqrt(jnp.mean(xn * xn, axis=-1, keepdims=True) + NORM_EPS) * fg_ref[...]
    o_ref[...] = xn


def _moe_combine(yb, pos, gates, x2d, gate5, rows_per_vec, tm, final_g=None):
    t, d = x2d.shape
    ys = [yb[pos[:, j]] for j in range(TOP_K)]
    in_specs = [_row(tm, d)] * TOP_K + [_row(tm, LANES), _row(tm, d), _vec(d, rows_per_vec, tm)]
    args = ys + [gates, x2d, gate5]
    if final_g is not None:
        in_specs.append(_full((1, d)))
        args.append(final_g.reshape(1, d))
    return pl.pallas_call(
        functools.partial(_combine_kernel, final=final_g is not None),
        grid=(t // tm,),
        in_specs=in_specs,
        out_specs=_row(tm, d),
        out_shape=jax.ShapeDtypeStruct((t, d), F32),
        compiler_params=_CP,
        name="moe_combine",
    )(*args)


def _odd_in_kernel(x_ref, g_ref, shift_ref, scale_ref, wm_ref, wab_ref, pm_ref, pab_ref):
    h = _norm_mod(x_ref[...], g_ref, shift_ref, scale_ref)
    pm_ref[...] = jnp.dot(h, wm_ref[...], preferred_element_type=F32).astype(pm_ref.dtype)
    pab_ref[...] = jnp.dot(h, wab_ref[...], preferred_element_type=F32)


def _odd_in_proj(x2d, g, shift, scale, w_main, w_ab, rows_per_vec, tm):
    t, d = x2d.shape
    vec = _vec(d, rows_per_vec, tm)
    return pl.pallas_call(
        _odd_in_kernel,
        grid=(t // tm,),
        in_specs=[_row(tm, d), _full((1, d)), vec, vec, _full(w_main.shape), _full(w_ab.shape)],
        out_specs=[_row(tm, GDN_MAIN_W), _row(tm, LANES)],
        out_shape=[jax.ShapeDtypeStruct((t, GDN_MAIN_W), BF16), jax.ShapeDtypeStruct((t, LANES), F32)],
        compiler_params=_CP,
        name="odd_in_proj",
    )(x2d, g.reshape(1, d), shift, scale, w_main, w_ab)


HALO = SUBLANES_BF16


def _gdn_prep_kernel(pm_ref, pp_ref, pn_ref, ab_ref, cw_ref, alog_ref, dtb_ref, q_ref, k_ref, v_ref, gb_ref):
    i = pl.program_id(1)
    n_i = pl.num_programs(1)
    tm = pm_ref.shape[1]
    half = GDN_CONV // 2
    keep_prev = (i > 0).astype(F32)
    keep_next = (i < n_i - 1).astype(F32)
    for cg in range(GDN_QKV_W // LANES):
        cols = slice(cg * LANES, (cg + 1) * LANES)
        main = pm_ref[0, :, cols].astype(F32)
        ext = jnp.concatenate([pp_ref[0, :, cols].astype(F32) * keep_prev, main,
                               pn_ref[0, :, cols].astype(F32) * keep_next], axis=0)
        acc = main * cw_ref[half:half + 1, cols]
        for j in range(GDN_CONV):
            if j != half:
                sh = pltpu.roll(ext, (half - j) % (tm + 2 * HALO), 0)[HALO:HALO + tm]
                acc = acc + sh * cw_ref[j:j + 1, cols]
        y = acc * jax.nn.sigmoid(acc)
        if cg < 2 * GDN_HK:
            y = y * lax.rsqrt(jnp.sum(y * y, axis=-1, keepdims=True) + 1e-6)
            if cg < GDN_HK:
                q_ref[0, :, cols] = y * (GDN_DK ** -0.5)
            else:
                k_ref[0, :, (cg - GDN_HK) * LANES:(cg - GDN_HK + 1) * LANES] = y
        else:
            v_ref[0, :, (cg - 2 * GDN_HK) * LANES:(cg - 2 * GDN_HK + 1) * LANES] = y
    ab = ab_ref[0]
    xa = ab + dtb_ref[...]
    softplus = jnp.maximum(xa, 0.0) + jnp.log1p(jnp.exp(-jnp.abs(xa)))
    lane = lax.broadcasted_iota(jnp.int32, ab.shape, 1)
    gb_ref[0] = jnp.where(lane < 2 * GDN_HV, -jnp.exp(alog_ref[...]) * softplus, jax.nn.sigmoid(ab))


def _gdn_prep(p_main, p_ab, conv_w, a_log, dt_bias, tm):
    b, t, _ = p_main.shape
    nb = tm // HALO
    last = t // HALO - 1
    pad = lambda a: jnp.zeros((1, LANES), F32).at[0, :2 * GDN_HV].set(a.reshape(-1))
    f = lambda n: jax.ShapeDtypeStruct((b, t, n), F32)
    seq = lambda n: pl.BlockSpec((1, tm, n), lambda bi, i: (bi, i, 0))
    return pl.pallas_call(
        _gdn_prep_kernel,
        grid=(b, t // tm),
        in_specs=[seq(GDN_QKV_W),
                  pl.BlockSpec((1, HALO, GDN_QKV_W), lambda bi, i: (bi, jnp.maximum(i * nb - 1, 0), 0)),
                  pl.BlockSpec((1, HALO, GDN_QKV_W), lambda bi, i: (bi, jnp.minimum((i + 1) * nb, last), 0)),
                  seq(LANES),
                  pl.BlockSpec((GDN_CONV, GDN_QKV_W), lambda bi, i: (0, 0)),
                  pl.BlockSpec((1, LANES), lambda bi, i: (0, 0)),
                  pl.BlockSpec((1, LANES), lambda bi, i: (0, 0))],
        out_specs=[seq(GDN_QK_W), seq(GDN_QK_W), seq(GDN_V_W), seq(LANES)],
        out_shape=[f(GDN_QK_W), f(GDN_QK_W), f(GDN_V_W), f(LANES)],
        compiler_params=_CP,
        name="gdn_prep",
    )(p_main, p_main, p_main, p_ab, conv_w, pad(a_log), pad(dt_bias))


def _gdn_chunk_kernel(q_ref, k_ref, v_ref, gb_ref, s0_ref, o_ref, sfin_ref, s_ref, *, reverse, g_lane, b_lane):
    c = pl.program_id(0)
    n_c = pl.num_programs(0)
    cs = GDN_CHUNK
    rep = GDN_HV // GDN_HK
    nb = q_ref.shape[0]

    @pl.when(c == 0)
    def _():
        s_ref[...] = s0_ref[...].reshape(s_ref.shape)

    row = lax.broadcasted_iota(jnp.int32, (cs, cs), 0)
    col = lax.broadcasted_iota(jnp.int32, (cs, cs), 1)
    incl = (row <= col) if reverse else (row >= col)
    strict = (row < col) if reverse else (row > col)
    eye = (row == col).astype(F32)
    incl_b = incl.astype(BF16)
    last = 0 if reverse else cs - 1
    tn = (((0,), (1,)), ((), ()))

    heads = range(GDN_HV)
    gc, gr, bc, qh, kh, vh = [], [], [], [], [], []
    for b in range(nb):
        gb = gb_ref[b]
        gb_hi = gb.astype(BF16)
        gb_lo = (gb - gb_hi.astype(F32)).astype(BF16)
        g_col = (jnp.dot(incl_b, gb_hi, preferred_element_type=F32)
                 + jnp.dot(incl_b, gb_lo, preferred_element_type=F32))
        g_row = (lax.dot_general(gb_hi, incl_b, tn, preferred_element_type=F32)
                 + lax.dot_general(gb_lo, incl_b, tn, preferred_element_type=F32))
        gc += [g_col[:, g_lane + h:g_lane + h + 1] for h in heads]
        gr += [g_row[g_lane + h:g_lane + h + 1, :] for h in heads]
        bc += [gb[:, b_lane + h:b_lane + h + 1] for h in heads]
        qh += [q_ref[b, :, h * GDN_DK:(h + 1) * GDN_DK] for h in range(GDN_HK)]
        kh += [k_ref[b, :, h * GDN_DK:(h + 1) * GDN_DK] for h in range(GDN_HK)]
        vh += [v_ref[b, :, h * GDN_DV:(h + 1) * GDN_DV] for h in heads]
    gc, gr, bc = jnp.stack(gc), jnp.stack(gr), jnp.stack(bc)
    qh, kh, vh = jnp.stack(qh), jnp.stack(kh), jnp.stack(vh)
    ge = gc[:, last:last + 1, :]
    kh_b = kh.astype(BF16)
    bnt = (((2,), (2,)), ((0,), (0,)))
    kk = lax.dot_general(kh_b, kh_b, bnt, preferred_element_type=F32)
    qk = lax.dot_general(qh.astype(BF16), kh_b, bnt, preferred_element_type=F32)
    kk, qk = jnp.repeat(kk, rep, axis=0), jnp.repeat(qk, rep, axis=0)
    qv, kv = jnp.repeat(qh, rep, axis=0), jnp.repeat(kh, rep, axis=0)

    decay = jnp.where(incl, jnp.exp(jnp.where(incl, gc - gr, 0.0)), 0.0)
    lm = jnp.where(strict, bc * kk * decay, 0.0)
    bmm = lambda a, b: jnp.einsum('hij,hjk->hik', a.astype(BF16), b.astype(BF16), preferred_element_type=F32)
    blk = lambda n: (row // n) == (col // n)
    l0 = jnp.where(blk(16), lm, 0.0)
    p = bmm(l0, l0)
    x = eye - l0
    for _ in range(2):
        xp = bmm(jnp.concatenate([x, p], axis=1), p)
        x = x + xp[:, :cs]
        p = xp[:, cs:]
    x = x + bmm(x, p)
    n = 32
    while n <= cs:
        off = jnp.where(blk(n) & ~blk(n // 2), lm, 0.0)
        x = x - bmm(bmm(x, off), x)
        n *= 2
    eg = jnp.exp(gc)
    uw = bmm(x, jnp.concatenate([vh * bc, kv * (bc * eg)], axis=2))
    u = uw[:, :, :GDN_DV]
    w = uw[:, :, GDN_DV:]
    qg = qv * eg
    intra = jnp.where(incl, qk * decay, 0.0)
    kt = kv * jnp.exp(ge - gc)
    s = s_ref[...]
    wq = bmm(jnp.concatenate([w, qg], axis=1), s)
    v_new = u - wq[:, :cs]
    o = wq[:, cs:] + bmm(intra, v_new)
    for b in range(nb):
        for h in heads:
            o_ref[b, :, h * GDN_DV:(h + 1) * GDN_DV] = o[b * GDN_HV + h]
    s_ref[...] = s * jnp.exp(ge) + jnp.einsum('hck,hcv->hkv', kt.astype(BF16), v_new.astype(BF16),
                                              preferred_element_type=F32)

    @pl.when(c == n_c - 1)
    def _():
        sfin_ref[...] = s_ref[...].reshape(sfin_ref.shape)


def _gdn_scan(q, k, v, gb, s0, reverse):
    b, t, _ = q.shape
    assert t % GDN_CHUNK == 0 and GDN_CHUNK % 32 == 0
    n_c = t // GDN_CHUNK
    cm = (lambda ci: (0, n_c - 1 - ci, 0)) if reverse else (lambda ci: (0, ci, 0))
    d = 1 if reverse else 0
    smap = lambda ci: (0, 0, 0, 0)
    return pl.pallas_call(
        functools.partial(_gdn_chunk_kernel, reverse=reverse, g_lane=d * GDN_HV, b_lane=(2 + d) * GDN_HV),
        grid=(n_c,),
        in_specs=[pl.BlockSpec((b, GDN_CHUNK, GDN_QK_W), cm), pl.BlockSpec((b, GDN_CHUNK, GDN_QK_W), cm),
                  pl.BlockSpec((b, GDN_CHUNK, GDN_V_W), cm), pl.BlockSpec((b, GDN_CHUNK, LANES), cm),
                  pl.BlockSpec((b, GDN_HV, GDN_DK, GDN_DV), smap)],
        out_specs=[pl.BlockSpec((b, GDN_CHUNK, GDN_V_W), cm), pl.BlockSpec((b, GDN_HV, GDN_DK, GDN_DV), smap)],
        out_shape=[jax.ShapeDtypeStruct((b, t, GDN_V_W), F32),
                   jax.ShapeDtypeStruct((b, GDN_HV, GDN_DK, GDN_DV), F32)],
        scratch_shapes=[pltpu.VMEM((b * GDN_HV, GDN_DK, GDN_DV), F32)],
        compiler_params=_CP,
        name="gdn_scan_bwd" if reverse else "gdn_scan_fwd",
    )(q, k, v, gb, s0)


def _odd_out_kernel(of_ref, ob_ref, z_ref, ng_ref, w_ref, x_ref, gate_ref, g_ref, shift_ref, scale_ref,
                    wr_ref, br_ref, xo_ref, h_ref, idx_ref, gates_ref):
    parts = []
    for h in range(GDN_HV):
        cols = slice(h * GDN_DV, (h + 1) * GDN_DV)
        o = of_ref[:, cols] + ob_ref[:, cols]
        z = z_ref[:, cols].astype(F32)
        o = o * lax.rsqrt(jnp.mean(o * o, axis=-1, keepdims=True) + NORM_EPS) * ng_ref[...]
        parts.append((o * (z * jax.nn.sigmoid(z))).astype(BF16))
    y = jnp.dot(jnp.concatenate(parts, axis=1), w_ref[...], preferred_element_type=F32)
    _residual_norm_route(y, x_ref, gate_ref, g_ref, shift_ref, scale_ref, wr_ref, br_ref,
                         xo_ref, h_ref, idx_ref, gates_ref)


def _odd_out_proj(o_f, o_b, p_main, norm_g, w, x2d, gate, g, shift, scale, w_router, b_router, rows_per_vec, tm):
    t, d = x2d.shape
    wr, br = _router_operands(w_router, b_router)
    vec = _vec(d, rows_per_vec, tm)
    out_specs, out_shape = _route_out(t, d, tm)
    return pl.pallas_call(
        _odd_out_kernel,
        grid=(t // tm,),
        in_specs=[_row(tm, GDN_V_W), _row(tm, GDN_V_W),
                  pl.BlockSpec((tm, GDN_V_W), lambda i: (i, GDN_QKV_W // GDN_V_W)),
                  _full((1, GDN_DV)), _full(w.shape), _row(tm, d), vec, _full((1, d)), vec, vec,
                  _full((d, LANES)), _full((1, LANES))],
        out_specs=out_specs, out_shape=out_shape,
        compiler_params=_CP,
        name="odd_out_proj",
    )(o_f, o_b, p_main, norm_g.reshape(1, GDN_DV), w, x2d, gate, g.reshape(1, d), shift, scale, wr, br)


def _split_mod(mv, bsz):
    d = mv.shape[1] // N_MOD
    lat = [mv[:bsz, j * d:(j + 1) * d][:, None, :] for j in range(N_MOD)]
    ctx = [mv[bsz:bsz + 1, j * d:(j + 1) * d][:, None, :] for j in range(N_MOD)]
    return lat, ctx


def _even_layer(x2d, xc2d, mod, mod_c, norm1_g, norm2_g, w_in, w_out, lam_p, subln_g, lam_init, cos, sin,
                w_router, b_router, bsz, n_lat, n_ctx):
    d = x2d.shape[1]
    w_in_b = w_in.astype(BF16)
    w_out_b = w_out.astype(BF16)
    q_scale = DA_DH ** -0.5 * math.log2(math.e)
    f, q, k_all, v_all = _even_in_proj(x2d, norm1_g, mod[0], mod[1], w_in_b, cos, sin, bsz, n_lat, n_lat + n_ctx,
                                       q_scale, TM_PROJ)
    ones, zeros = jnp.ones((n_ctx, ROPE_AXIS_DIM), F32), jnp.zeros((n_ctx, ROPE_AXIS_DIM), F32)
    fc, qc, kc, vc = _even_in_proj(xc2d, norm1_g, mod_c[0], mod_c[1], w_in_b, ones, zeros, bsz, n_ctx, n_ctx,
                                   q_scale, n_ctx)
    lp = lam_p.astype(F32)
    lam = jnp.exp(jnp.sum(lp[0] * lp[1])) - jnp.exp(jnp.sum(lp[2] * lp[3])) + lam_init
    k_all = lax.dynamic_update_slice(k_all, kc, (0, n_lat, 0))
    v_all = lax.dynamic_update_slice(v_all, vc, (0, n_lat, 0))
    o = _diff_attention(lam, q, k_all, v_all, subln_g, 1.0 - lam_init, TQ_ATTN, TK_ATTN)
    oc = _diff_attention(lam, qc, kc, vc, subln_g, 1.0 - lam_init, n_ctx, n_ctx)
    fm = _fourier_mix(f.reshape(bsz, n_lat, FOURIER_W)).reshape(bsz * n_lat, FOURIER_W)
    fmc = _fourier_mix(fc.reshape(bsz, n_ctx, FOURIER_W)).reshape(bsz * n_ctx, FOURIER_W)
    lat = _even_out_proj(fm, o.reshape(bsz * n_lat, DA_W), w_out_b, x2d, mod[2], norm2_g, mod[3], mod[4],
                         w_router, b_router, n_lat, TM_PROJ)
    ctx = _even_out_proj(fmc, oc.reshape(bsz * n_ctx, DA_W), w_out_b, xc2d, mod_c[2], norm2_g, mod_c[3], mod_c[4],
                         w_router, b_router, bsz * n_ctx, n_ctx)
    return lat, ctx


def _odd_layer(x2d, xc2d, mod, mod_c, norm1_g, norm2_g, w_in, conv_w, a_log, dt_bias, norm_g, w_out,
               w_router, b_router, bsz, n_lat, n_ctx):
    d = x2d.shape[1]
    w_main = w_in[:, :GDN_MAIN_W].astype(BF16)
    w_ab = jnp.zeros((d, LANES), BF16).at[:, :4 * GDN_HV].set(w_in[:, GDN_MAIN_W:].astype(BF16))
    p_main, p_ab = _odd_in_proj(x2d, norm1_g, mod[0], mod[1], w_main, w_ab, n_lat, TM_PROJ)
    q, k, v, gb = _gdn_prep(p_main.reshape(bsz, n_lat, GDN_MAIN_W), p_ab.reshape(bsz, n_lat, LANES),
                            conv_w, a_log, dt_bias, TM_GDN)
    pc_main, pc_ab = _odd_in_proj(xc2d, norm1_g, mod_c[0], mod_c[1], w_main, w_ab, bsz * n_ctx, n_ctx)
    qc, kc, vc, gbc = _gdn_prep(pc_main.reshape(bsz, n_ctx, GDN_MAIN_W), pc_ab.reshape(bsz, n_ctx, LANES),
                                conv_w, a_log, dt_bias, n_ctx)
    s0 = jnp.zeros((bsz, GDN_HV, GDN_DK, GDN_DV), F32)
    _, sc_f = _gdn_scan(qc, kc, vc, gbc, s0, False)
    o_f, _ = _gdn_scan(q, k, v, gb, sc_f, False)
    _, sc_b = _gdn_scan(qc, kc, vc, gbc, s0, True)
    o_b, _ = _gdn_scan(q, k, v, gb, sc_b, True)
    return _odd_out_proj(o_f.reshape(bsz * n_lat, GDN_V_W), o_b.reshape(bsz * n_lat, GDN_V_W), p_main, norm_g,
                         w_out.astype(BF16), x2d, mod[2], norm2_g, mod[3], mod[4], w_router, b_router, n_lat, TM_GDN)


def kernel(x, c, ctx, c_ctx, norm1_g, norm2_g, w_mod, b_mod, ev_w_in, ev_w_out, ev_lam, ev_subln_g,
           od_w_in, od_conv_w, od_a_log, od_dt_bias, od_norm_g, od_w_out,
           moe_w_router, moe_b_router, moe_w1, moe_b1, moe_w2, moe_b2, final_g):
    bsz, n_lat, d = x.shape
    n_ctx = ctx.shape[1]
    assert w_mod.shape[0] == 2, "kernel is written for one even (attention) and one odd (DeltaNet) layer"
    t_lat = bsz * n_lat
    cos, sin = _axial_rope_tables(n_lat // GRID_W)
    c_rows = jnp.zeros((8, d), F32).at[:bsz].set(c).at[bsz].set(c_ctx)
    x2d = x.reshape(t_lat, d)
    xc2d = ctx.reshape(bsz * n_ctx, d)

    mod, mod_c = _split_mod(_mod_vectors(c_rows, w_mod, b_mod, 0), bsz)
    (x2d, h2, idx, gates), (xc2d, h2c, idx_c, gates_c) = _even_layer(
        x2d, xc2d, mod, mod_c, norm1_g[0], norm2_g[0], ev_w_in[0], ev_w_out[0], ev_lam[0], ev_subln_g[0],
        _diff_lambda_init(0), cos, sin, moe_w_router[0], moe_b_router[0], bsz, n_lat, n_ctx)
    top_idx = jnp.concatenate([idx[:, :TOP_K], idx_c[:, :TOP_K]], axis=0)
    yb, pos = _moe_ffn(jnp.concatenate([h2, h2c], axis=0), top_idx, moe_w1, moe_b1[0], moe_w2, moe_b2[0], 0)
    x2d = _moe_combine(yb, pos[:t_lat], gates, x2d, mod[5], n_lat, TM_PROJ)
    xc2d = _moe_combine(yb, pos[t_lat:], gates_c, xc2d, mod_c[5], bsz * n_ctx, n_ctx)

    mod, mod_c = _split_mod(_mod_vectors(c_rows, w_mod, b_mod, 1), bsz)
    x2d, h2, idx, gates = _odd_layer(x2d, xc2d, mod, mod_c, norm1_g[1], norm2_g[1], od_w_in[0], od_conv_w[0],
                                     od_a_log[0], od_dt_bias[0], od_norm_g[0], od_w_out[0],
                                     moe_w_router[1], moe_b_router[1], bsz, n_lat, n_ctx)
    yb, pos = _moe_ffn(h2, idx[:, :TOP_K], moe_w1, moe_b1[1], moe_w2, moe_b2[1], 1)
    return _moe_combine(yb, pos, gates, x2d, mod[5], n_lat, TM_PROJ, final_g=final_g).reshape(bsz, n_lat, d)
```

```python
import functools
import math

import jax
import jax.numpy as jnp
from jax import lax
from jax.experimental import pallas as pl
from jax.experimental.pallas import tpu as pltpu

D_MODEL = 1024
N_MOD = 6
NORM_EPS = 1e-6
GRID_W = 64

FOURIER_GROUPS = 4
FOURIER_GD = 64
FOURIER_W = FOURIER_GROUPS * FOURIER_GD
DA_HEADS = 6
DA_DH = 64
DA_VD = 2 * DA_DH
DA_W = DA_HEADS * DA_VD
ROPE_BASE = 10000.0
ROPE_AXIS_DIM = DA_DH // 2
SUBLN_EPS = 1e-5

GDN_HK = 8
GDN_HV = 16
GDN_DK = 128
GDN_DV = 128
GDN_QK_W = GDN_HK * GDN_DK
GDN_V_W = GDN_HV * GDN_DV
GDN_QKV_W = 2 * GDN_QK_W + GDN_V_W
GDN_MAIN_W = GDN_QKV_W + GDN_V_W
GDN_CONV = 5
GDN_CHUNK = 64

N_EXPERTS = 32
TOP_K = 4
D_FF = 1024
SWIGLU_LIMIT = 7.0
SWIGLU_ALPHA = 1.702
MOE_BLOCK = 512
MOE_PARTS = 4

LANES = 128
SUBLANES_BF16 = 16
MXU_DIM = 256
VMEM_LIMIT = 56 * 1024 * 1024
BF16 = jnp.bfloat16
F32 = jnp.float32
HI = lax.Precision.HIGHEST

TM_PROJ = 512
TM_GDN = 256
TQ_ATTN, TK_ATTN = 512, 1280

_CP = pltpu.CompilerParams(vmem_limit_bytes=VMEM_LIMIT)


def _diff_lambda_init(layer_idx):
    return 0.8 - 0.6 * math.exp(-0.3 * layer_idx)


def _axial_rope_tables(rows):
    t = jnp.arange(rows * GRID_W, dtype=jnp.int32)
    row = (t // GRID_W).astype(F32)
    col = (t % GRID_W).astype(F32)
    inv = ROPE_BASE ** (-jnp.arange(0, ROPE_AXIS_DIM, 2, dtype=F32) / ROPE_AXIS_DIM)
    ang = jnp.concatenate([row[:, None] * inv, col[:, None] * inv], axis=-1)
    return jnp.cos(ang), jnp.sin(ang)


def _norm_mod(x, g_ref, shift_ref, scale_ref):
    h = x * lax.rsqrt(jnp.mean(x * x, axis=-1, keepdims=True) + NORM_EPS) * g_ref[...]
    return (h * (1.0 + scale_ref[0]) + shift_ref[0]).astype(BF16)


def _route(logits):
    lane = lax.broadcasted_iota(jnp.int32, logits.shape, 1)
    lane_f = lane.astype(F32)
    neg = jnp.float32(-jnp.inf)
    rest = jnp.where(lane < N_EXPERTS, logits, neg)
    idx = jnp.zeros(logits.shape, F32)
    val = jnp.full(logits.shape, neg, F32)
    for j in range(TOP_K):
        m = jnp.max(rest, axis=-1, keepdims=True)
        sel = jnp.min(jnp.where(rest == m, lane_f, float(LANES)), axis=-1, keepdims=True)
        idx = jnp.where(lane == j, sel, idx)
        val = jnp.where(lane == j, m, val)
        rest = jnp.where(lane_f == sel, neg, rest)
    e = jnp.exp(val - val[:, 0:1])
    return idx, e / jnp.sum(e, axis=-1, keepdims=True)


def _residual_norm_route(y, x_ref, gate_ref, g_ref, shift_ref, scale_ref, wr_ref, br_ref,
                         xo_ref, h_ref, idx_ref, gates_ref):
    xn = x_ref[...] + gate_ref[0] * y
    xo_ref[...] = xn
    h = _norm_mod(xn, g_ref, shift_ref, scale_ref)
    h_ref[...] = h
    idx, gates = _route(jnp.dot(h, wr_ref[...], preferred_element_type=F32) + br_ref[...])
    idx_ref[...] = idx.astype(jnp.int32)
    gates_ref[...] = gates


def _row(tm, n):
    return pl.BlockSpec((tm, n), lambda i: (i, 0))


def _full(shape):
    return pl.BlockSpec(shape, lambda i: (0,) * len(shape))


def _vec(d, rows_per_vec, tm):
    per = rows_per_vec // tm
    return pl.BlockSpec((1, 1, d), lambda i: (i // per, 0, 0))


def _router_operands(w_router, b_router):
    d = w_router.shape[0]
    wr = jnp.zeros((d, LANES), BF16).at[:, :N_EXPERTS].set(w_router.astype(BF16))
    br = jnp.zeros((1, LANES), F32).at[0, :N_EXPERTS].set(b_router)
    return wr, br


def _route_out(t, d, tm):
    specs = [_row(tm, d), _row(tm, d), _row(tm, LANES), _row(tm, LANES)]
    shapes = [jax.ShapeDtypeStruct((t, d), F32), jax.ShapeDtypeStruct((t, d), BF16),
              jax.ShapeDtypeStruct((t, LANES), jnp.int32), jax.ShapeDtypeStruct((t, LANES), F32)]
    return specs, shapes


def _mod_kernel(c_ref, w_ref, b_ref, o_ref):
    c = c_ref[...]
    s = (c * jax.nn.sigmoid(c)).astype(BF16)
    o_ref[...] = jnp.dot(s, w_ref[0].astype(BF16), preferred_element_type=F32) + b_ref[0]


def _mod_vectors(c_rows, w_mod, b_mod, layer):
    d = c_rows.shape[1]
    return pl.pallas_call(
        _mod_kernel,
        grid=(N_MOD,),
        in_specs=[pl.BlockSpec(c_rows.shape, lambda n: (0, 0)),
                  pl.BlockSpec((1, d, d), lambda n: (layer, 0, n)),
                  pl.BlockSpec((1, 1, d), lambda n: (layer, 0, n))],
        out_specs=pl.BlockSpec((c_rows.shape[0], d), lambda n: (0, n)),
        out_shape=jax.ShapeDtypeStruct((c_rows.shape[0], N_MOD * d), F32),
        compiler_params=_CP,
        name="mod_vectors",
    )(c_rows, w_mod, b_mod.reshape(b_mod.shape[0], 1, -1))


def _even_in_kernel(x_ref, g_ref, shift_ref, scale_ref, w_ref, rc_ref, rs_ref, f_ref, q_ref, k_ref, v_ref, *, q_scale):
    h = _norm_mod(x_ref[...], g_ref, shift_ref, scale_ref)
    p = jnp.dot(h, w_ref[...], preferred_element_type=F32)
    f_ref[...] = p[:, :FOURIER_W]
    rc, rs = rc_ref[...], rs_ref[...]
    lane = lax.broadcasted_iota(jnp.int32, rc.shape, 1)
    first = (lane % DA_DH) < ROPE_AXIS_DIM

    def rope(t):
        partner = jnp.where(first, pltpu.roll(t, LANES - ROPE_AXIS_DIM, 1), pltpu.roll(t, ROPE_AXIS_DIM, 1))
        return t * rc + partner * rs

    for hd in range(DA_HEADS):
        cq = slice(FOURIER_W + hd * DA_VD, FOURIER_W + (hd + 1) * DA_VD)
        ck = slice(FOURIER_W + DA_W + hd * DA_VD, FOURIER_W + DA_W + (hd + 1) * DA_VD)
        q_ref[0, :, hd * DA_VD:(hd + 1) * DA_VD] = (rope(p[:, cq]) * q_scale).astype(BF16)
        k_ref[0, :, hd * DA_VD:(hd + 1) * DA_VD] = rope(p[:, ck]).astype(BF16)
    v_ref[0] = p[:, FOURIER_W + 2 * DA_W:].astype(BF16)


def _even_in_proj(x2d, g, shift, scale, w, cos, sin, bsz, n_seq, n_keys, q_scale, tm):
    t, d = x2d.shape
    bpb = n_seq // tm
    rc = jnp.tile(cos, (1, LANES // ROPE_AXIS_DIM))
    rs = jnp.tile(jnp.concatenate([-sin, sin], axis=1), (1, LANES // DA_DH))
    tab = pl.BlockSpec((tm, LANES), lambda i: (i % bpb, 0))
    seq = pl.BlockSpec((1, tm, DA_W), lambda i: (i // bpb, i % bpb, 0))
    vec = _vec(d, n_seq * (bsz // shift.shape[0]), tm)
    return pl.pallas_call(
        functools.partial(_even_in_kernel, q_scale=q_scale),
        grid=(t // tm,),
        in_specs=[_row(tm, d), _full((1, d)), vec, vec, _full(w.shape), tab, tab],
        out_specs=[_row(tm, FOURIER_W), seq, seq, seq],
        out_shape=[jax.ShapeDtypeStruct((t, FOURIER_W), F32), jax.ShapeDtypeStruct((bsz, n_seq, DA_W), BF16),
                   jax.ShapeDtypeStruct((bsz, n_keys, DA_W), BF16), jax.ShapeDtypeStruct((bsz, n_keys, DA_W), BF16)],
        compiler_params=_CP,
        name="even_in_proj",
    )(x2d, g.reshape(1, d), shift, scale, w, rc, rs)


def _diff_attn_kernel(lam_ref, q_ref, k_ref, v_ref, g_ref, o_ref, qs_ref, s_ref, m_ref, acc_ref, *,
                      tk, n_sub, out_scale):
    tq = q_ref.shape[1]
    n_kv = k_ref.shape[1] // tk
    rb = 2 * tq // n_sub
    q = q_ref[0]
    lane = lax.broadcasted_iota(jnp.int32, q.shape, 1)
    zero = jnp.zeros_like(q)
    qs_ref[:tq] = jnp.where(lane < DA_DH, q, zero)
    qs_ref[tq:] = jnp.where(lane >= DA_DH, q, zero)
    m_ref[...] = jnp.full(m_ref.shape, -1e30, F32)
    acc_ref[...] = jnp.zeros(acc_ref.shape, F32)
    ones = jnp.ones((tk, LANES), BF16)

    def scores(i, slot):
        off = pl.multiple_of(i * tk, tk)
        k = k_ref[0, pl.ds(off, tk), :]
        s_ref[slot] = lax.dot_general(qs_ref[...], k, (((1,), (1,)), ((), ())), preferred_element_type=F32)

    def consume(i, slot):
        off = pl.multiple_of(i * tk, tk)
        v_ext = jnp.concatenate([v_ref[0, pl.ds(off, tk), :], ones], axis=1)
        for r in range(n_sub):
            rows = pl.ds(r * rb, rb)
            s = s_ref[slot, rows, :]
            m_prev = m_ref[rows, :]
            m_new = jnp.maximum(m_prev, jnp.max(s, axis=1, keepdims=True))
            alpha = jnp.exp2(m_prev - m_new)
            p = jnp.exp2(s - jnp.tile(m_new, (1, tk // LANES)))
            pv = jnp.dot(p.astype(BF16), v_ext, preferred_element_type=F32)
            acc_ref[rows, :] = acc_ref[rows, :] * jnp.tile(alpha, (1, 2)) + pv
            m_ref[rows, :] = m_new

    scores(0, 0)

    def body(j, carry):
        scores(2 * j + 1, 1)
        consume(2 * j, 0)
        scores(2 * j + 2, 0)
        consume(2 * j + 1, 1)
        return carry

    lax.fori_loop(0, (n_kv - 1) // 2, body, 0)
    consume(n_kv - 1, 0)
    acc = acc_ref[...]
    o1 = acc[:tq, :LANES] / acc[:tq, LANES:]
    o2 = acc[tq:, :LANES] / acc[tq:, LANES:]
    o = o1 - lam_ref[0] * o2
    ms = jnp.mean(o * o, axis=-1, keepdims=True)
    o = o * lax.rsqrt(ms + SUBLN_EPS) * g_ref[...] * out_scale
    o_ref[0] = o.astype(o_ref.dtype)


def _diff_attention(lam, q, k_all, v_all, subln_g, out_scale, tq, tk, n_sub=2):
    b, n, _ = q.shape
    nk = k_all.shape[1]
    assert n % tq == 0 and nk % tk == 0 and tk % MXU_DIM == 0 and (nk // tk) % 2 == 1
    grid_spec = pltpu.PrefetchScalarGridSpec(
        num_scalar_prefetch=1,
        grid=(b, DA_HEADS, n // tq),
        in_specs=[pl.BlockSpec((1, tq, DA_VD), lambda bi, hi, qi, lam_r: (bi, qi, hi)),
                  pl.BlockSpec((1, nk, DA_VD), lambda bi, hi, qi, lam_r: (bi, 0, hi)),
                  pl.BlockSpec((1, nk, DA_VD), lambda bi, hi, qi, lam_r: (bi, 0, hi)),
                  pl.BlockSpec((1, DA_VD), lambda bi, hi, qi, lam_r: (0, 0))],
        out_specs=pl.BlockSpec((1, tq, DA_VD), lambda bi, hi, qi, lam_r: (bi, qi, hi)),
        scratch_shapes=[pltpu.VMEM((2 * tq, LANES), BF16), pltpu.VMEM((2, 2 * tq, tk), F32),
                        pltpu.VMEM((2 * tq, LANES), F32), pltpu.VMEM((2 * tq, 2 * LANES), F32)],
    )
    return pl.pallas_call(
        functools.partial(_diff_attn_kernel, tk=tk, n_sub=n_sub, out_scale=out_scale),
        grid_spec=grid_spec,
        out_shape=jax.ShapeDtypeStruct((b, n, DA_W), BF16),
        compiler_params=_CP,
        name="diff_attention",
    )(lam.reshape(1), q, k_all, v_all, subln_g.reshape(1, DA_VD))


def _fft_stage1_kernel(x_ref, fch_ref, f1_ref, tc_ref, ts_ref, yr_ref, yi_ref, *, nb):
    w = FOURIER_W
    r = x_ref.shape[1]
    for bl in range(nb):
        cols = slice(bl * w, (bl + 1) * w)
        ab = jnp.dot(x_ref[0, :, cols], fch_ref[...], preferred_element_type=F32, precision=HI)
        z = jnp.concatenate([ab[:, :w], ab[:, w:]], axis=0)
        y = jnp.dot(f1_ref[...], z, preferred_element_type=F32, precision=HI)
        yr, yi = y[:r], y[r:]
        tc, ts = tc_ref[:, cols], ts_ref[:, cols]
        yr_ref[0, :, cols] = yr * tc + yi * ts
        yi_ref[0, :, cols] = yi * tc - yr * ts


def _fft_stage2_kernel(yr_ref, yi_ref, f2_ref, o_ref, *, nc):
    w = FOURIER_W
    for cl in range(nc):
        y = jnp.concatenate([yr_ref[0, cl], yi_ref[0, cl]], axis=0)
        o_ref[0, :, cl * w:(cl + 1) * w] = jnp.dot(f2_ref[...], y, preferred_element_type=F32, precision=HI)


def _fourier_mix(f, nb=8):
    bsz, n, w = f.shape
    r = math.isqrt(n)
    assert r * r == n and r % nb == 0 and w == FOURIER_W
    two_pi = 2.0 * math.pi
    k = jnp.arange(r, dtype=jnp.int32)
    ang = two_pi * ((k[:, None] * k[None, :]) % r).astype(F32) / r
    c1, s1 = jnp.cos(ang), jnp.sin(ang)
    kc = jnp.arange(FOURIER_GD, dtype=jnp.int32)
    angc = two_pi * ((kc[:, None] * kc[None, :]) % FOURIER_GD).astype(F32) / FOURIER_GD
    eye_g = jnp.eye(FOURIER_GROUPS, dtype=F32)
    fch = jnp.concatenate([jnp.kron(eye_g, jnp.cos(angc)), jnp.kron(eye_g, jnp.sin(angc))], axis=1)
    f1 = jnp.concatenate([jnp.concatenate([c1, -s1], axis=1), jnp.concatenate([-s1, -c1], axis=1)], axis=0)
    f2 = jnp.concatenate([c1, s1], axis=1) * (1.0 / math.sqrt(n * FOURIER_GD))
    angt = two_pi * ((k[:, None] * k[None, :]) % n).astype(F32) / n
    tc = jnp.repeat(jnp.cos(angt), w, axis=1)
    ts = jnp.repeat(jnp.sin(angt), w, axis=1)
    xv = f.reshape(bsz, r, r * w)
    blk = pl.BlockSpec((1, r, nb * w), lambda bi, j: (bi, 0, j))
    tab = pl.BlockSpec((r, nb * w), lambda bi, j: (0, j))
    full = lambda a: pl.BlockSpec(a.shape, lambda bi, j: (0, 0))
    yshape = jax.ShapeDtypeStruct((bsz, r, r * w), F32)
    yr, yi = pl.pallas_call(
        functools.partial(_fft_stage1_kernel, nb=nb),
        grid=(bsz, r // nb),
        in_specs=[blk, full(fch), full(f1), tab, tab],
        out_specs=[blk, blk],
        out_shape=[yshape, yshape],
        compiler_params=_CP,
        name="fft_stage1",
    )(xv, fch, f1, tc, ts)
    y4 = lambda a: a.reshape(bsz, r, r, w)
    yblk = pl.BlockSpec((1, nb, r, w), lambda bi, j: (bi, j, 0, 0))
    out = pl.pallas_call(
        functools.partial(_fft_stage2_kernel, nc=nb),
        grid=(bsz, r // nb),
        in_specs=[yblk, yblk, full(f2)],
        out_specs=blk,
        out_shape=yshape,
        compiler_params=_CP,
        name="fft_stage2",
    )(y4(yr), y4(yi), f2)
    return out.reshape(bsz, n, w)


def _even_out_kernel(fm_ref, o_ref, w_ref, x_ref, gate_ref, g_ref, shift_ref, scale_ref, wr_ref, br_ref,
                     xo_ref, h_ref, idx_ref, gates_ref):
    y = (jnp.dot(fm_ref[...].astype(BF16), w_ref[:FOURIER_W], preferred_element_type=F32)
         + jnp.dot(o_ref[...], w_ref[FOURIER_W:], preferred_element_type=F32))
    _residual_norm_route(y, x_ref, gate_ref, g_ref, shift_ref, scale_ref, wr_ref, br_ref,
                         xo_ref, h_ref, idx_ref, gates_ref)


def _even_out_proj(fm, o, w, x2d, gate, g, shift, scale, w_router, b_router, rows_per_vec, tm):
    t, d = x2d.shape
    wr, br = _router_operands(w_router, b_router)
    vec = _vec(d, rows_per_vec, tm)
    out_specs, out_shape = _route_out(t, d, tm)
    return pl.pallas_call(
        _even_out_kernel,
        grid=(t // tm,),
        in_specs=[_row(tm, FOURIER_W), _row(tm, DA_W), _full(w.shape), _row(tm, d), vec, _full((1, d)), vec, vec,
                  _full((d, LANES)), _full((1, LANES))],
        out_specs=out_specs, out_shape=out_shape,
        compiler_params=_CP,
        name="even_out_proj",
    )(fm, o, w, x2d, gate, g.reshape(1, d), shift, scale, wr, br)


def _moe_kernel(be_ref, x_ref, w1_ref, perm_ref, b1g_ref, b1l_ref, w2_ref, b2_ref, *rest):
    o_ref, w1g_s, w1l_s, w2_s = rest[-4:]
    i = pl.program_id(0)

    @pl.when((i == 0) | (be_ref[i] != be_ref[jnp.maximum(i - 1, 0)]))
    def _():
        for c in range(w1_ref.shape[3] // MXU_DIM):
            blk = jnp.dot(w1_ref[0, 0, :, c * MXU_DIM:(c + 1) * MXU_DIM].astype(BF16), perm_ref[...],
                          preferred_element_type=F32)
            w1g_s[:, c * LANES:(c + 1) * LANES] = blk[:, :LANES].astype(BF16)
            w1l_s[:, c * LANES:(c + 1) * LANES] = blk[:, LANES:].astype(BF16)
        w2_s[...] = w2_ref[0, 0].astype(BF16)

    x = x_ref[...]
    ug = jnp.dot(x, w1g_s[...], preferred_element_type=F32) + b1g_ref[0]
    ul = jnp.dot(x, w1l_s[...], preferred_element_type=F32) + b1l_ref[0]
    glu = jnp.minimum(ug, SWIGLU_LIMIT)
    lin = jnp.clip(ul, -SWIGLU_LIMIT, SWIGLU_LIMIT)
    act = glu * jax.nn.sigmoid(SWIGLU_ALPHA * glu) * (lin + 1.0)
    y = jnp.dot(act.astype(BF16), w2_s[...], preferred_element_type=F32) + b2_ref[0]
    o_ref[...] = y.astype(o_ref.dtype)


def _moe_experts(block_expert, xb, w1_all, b1g, b1l, w2_all, layer, b2, yb_prev, first_block, n_blocks_total):
    n_rows, d = xb.shape
    n_blocks = n_rows // MOE_BLOCK
    src = jnp.arange(MXU_DIM)[:, None]
    dst = jnp.arange(MXU_DIM)[None, :]
    perm = (src == jnp.where(dst < LANES, 2 * dst, 2 * (dst - LANES) + 1)).astype(BF16)
    bspec = lambda s: pl.BlockSpec((1,) + s, lambda i, be: (be[i], 0, 0))
    wspec = lambda s: pl.BlockSpec((1, 1) + s, lambda i, be: (layer, be[i], 0, 0))
    in_specs = [pl.BlockSpec((MOE_BLOCK, d), lambda i, be: (i, 0)),
                wspec((d, 2 * D_FF)), pl.BlockSpec((MXU_DIM, MXU_DIM), lambda i, be: (0, 0)),
                bspec((1, D_FF)), bspec((1, D_FF)), wspec((D_FF, d)), bspec((1, d))]
    args = [block_expert, xb, w1_all, perm, b1g, b1l, w2_all, b2]
    aliases = {}
    if yb_prev is not None:
        in_specs.append(pl.BlockSpec(memory_space=pl.ANY))
        aliases = {len(args): 0}
        args.append(yb_prev)
    grid_spec = pltpu.PrefetchScalarGridSpec(
        num_scalar_prefetch=1,
        grid=(n_blocks,),
        in_specs=in_specs,
        out_specs=pl.BlockSpec((MOE_BLOCK, d), lambda i, be: (first_block + i, 0)),
        scratch_shapes=[pltpu.VMEM((d, D_FF), BF16), pltpu.VMEM((d, D_FF), BF16), pltpu.VMEM((D_FF, d), BF16)],
    )
    return pl.pallas_call(
        _moe_kernel,
        grid_spec=grid_spec,
        out_shape=jax.ShapeDtypeStruct((n_blocks_total * MOE_BLOCK, d), BF16),
        input_output_aliases=aliases,
        compiler_params=_CP,
        name="moe_experts",
    )(*args)


def _rank_kernel(idx_ref, rank_ref, cnt_ref, carry_ref):
    i = pl.program_id(0)

    @pl.when(i == 0)
    def _():
        carry_ref[...] = jnp.zeros(carry_ref.shape, F32)

    idx = idx_ref[...]
    tm = idx.shape[0]
    lane = lax.broadcasted_iota(jnp.int32, idx.shape, 1)
    sel = [lane == idx[:, j:j + 1] for j in range(TOP_K)]
    onehot = sel[0].astype(F32)
    for j in range(1, TOP_K):
        onehot = onehot + sel[j].astype(F32)
    row = lax.broadcasted_iota(jnp.int32, (tm, tm), 0)
    col = lax.broadcasted_iota(jnp.int32, (tm, tm), 1)
    before = (row > col).astype(BF16)
    prefix = jnp.dot(before, onehot.astype(BF16), preferred_element_type=F32) + carry_ref[0:1, :]
    rank = jnp.zeros(idx.shape, F32)
    for j in range(TOP_K):
        rank = jnp.where(lane == j, jnp.sum(jnp.where(sel[j], prefix, 0.0), axis=-1, keepdims=True), rank)
    rank_ref[...] = rank.astype(jnp.int32)
    carry_ref[0:1, :] = carry_ref[0:1, :] + jnp.sum(onehot, axis=0, keepdims=True)
    cnt_ref[...] = carry_ref[...]


def _expert_ranks(idx, tm):
    t = idx.shape[0]
    rank, cnt = pl.pallas_call(
        _rank_kernel,
        grid=(t // tm,),
        in_specs=[_row(tm, LANES)],
        out_specs=[_row(tm, LANES), _full((8, LANES))],
        out_shape=[jax.ShapeDtypeStruct((t, LANES), jnp.int32), jax.ShapeDtypeStruct((8, LANES), F32)],
        scratch_shapes=[pltpu.VMEM((8, LANES), F32)],
        compiler_params=_CP,
        name="expert_ranks",
    )(idx)
    return rank[:, :TOP_K], cnt[0, :N_EXPERTS].astype(jnp.int32)


def _moe_ffn(h, idx, w1_all, b1, w2_all, b2, layer, tm):
    n_tok, d = h.shape
    n_assign = n_tok * TOP_K
    e_flat = idx[:, :TOP_K].reshape(n_assign)
    order = jnp.argsort(e_flat).astype(jnp.int32)
    rank, counts = _expert_ranks(idx, tm)
    starts = jnp.cumsum(counts) - counts
    padded = (counts + MOE_BLOCK - 1) // MOE_BLOCK * MOE_BLOCK
    pad_ends = jnp.cumsum(padded)
    pad_starts = pad_ends - padded
    n_blocks = -(-(n_assign + N_EXPERTS * (MOE_BLOCK - 1)) // MOE_BLOCK)
    n_rows = n_blocks * MOE_BLOCK
    block_start = jnp.arange(n_blocks, dtype=jnp.int32) * MOE_BLOCK
    block_expert = jnp.minimum(jnp.sum(pad_ends[None, :] <= block_start[:, None], axis=1, dtype=jnp.int32),
                               N_EXPERTS - 1)
    shift = pad_starts - starts
    pos = pad_starts[e_flat].reshape(n_tok, TOP_K) + rank
    r = jnp.arange(n_rows, dtype=jnp.int32)
    src = jnp.clip(r - jnp.repeat(shift[block_expert], MOE_BLOCK), 0, n_assign - 1)
    row_tok = order[src] // TOP_K
    assert n_blocks % MOE_PARTS == 0
    nbp = n_blocks // MOE_PARTS
    yb = None
    for part in range(MOE_PARTS):
        blocks = slice(part * nbp, (part + 1) * nbp)
        xb = h[row_tok[part * nbp * MOE_BLOCK:(part + 1) * nbp * MOE_BLOCK]]
        yb = _moe_experts(block_expert[blocks], xb, w1_all, b1[:, None, 0::2], b1[:, None, 1::2], w2_all, layer,
                          b2[:, None, :], yb, part * nbp, n_blocks)
    return yb, pos


def _combine_kernel(y_ref, gt_ref, x_ref, g5_ref, *refs, final):
    o_ref = refs[-1]
    d = x_ref.shape[1]
    gt = gt_ref[...]
    acc = y_ref[:, :d].astype(F32) * gt[:, 0:1]
    for j in range(1, TOP_K):
        acc = acc + y_ref[:, j * d:(j + 1) * d].astype(F32) * gt[:, j:j + 1]
    xn = x_ref[...] + g5_ref[0] * acc
    if final:
        fg_ref = refs[0]
        xn = xn * lax.rsqrt(jnp.mean(xn * xn, axis=-1, keepdims=True) + NORM_EPS) * fg_ref[...]
    o_ref[...] = xn


def _moe_combine(yb, pos, gates, x2d, gate5, rows_per_vec, tm, final_g=None):
    t, d = x2d.shape
    ys = yb[pos.reshape(t * TOP_K)].reshape(t, TOP_K * d)
    in_specs = [_row(tm, TOP_K * d), _row(tm, LANES), _row(tm, d), _vec(d, rows_per_vec, tm)]
    args = [ys, gates, x2d, gate5]
    if final_g is not None:
        in_specs.append(_full((1, d)))
        args.append(final_g.reshape(1, d))
    return pl.pallas_call(
        functools.partial(_combine_kernel, final=final_g is not None),
        grid=(t // tm,),
        in_specs=in_specs,
        out_specs=_row(tm, d),
        out_shape=jax.ShapeDtypeStruct((t, d), F32),
        compiler_params=_CP,
        name="moe_combine",
    )(*args)


def _odd_in_kernel(x_ref, g_ref, shift_ref, scale_ref, wm_ref, wab_ref, pm_ref, pab_ref):
    h = _norm_mod(x_ref[...], g_ref, shift_ref, scale_ref)
    pm_ref[...] = jnp.dot(h, wm_ref[...], preferred_element_type=F32).astype(pm_ref.dtype)
    pab_ref[...] = jnp.dot(h, wab_ref[...], preferred_element_type=F32)


def _odd_in_proj(x2d, g, shift, scale, w_main, w_ab, rows_per_vec, tm):
    t, d = x2d.shape
    vec = _vec(d, rows_per_vec, tm)
    return pl.pallas_call(
        _odd_in_kernel,
        grid=(t // tm,),
        in_specs=[_row(tm, d), _full((1, d)), vec, vec, _full(w_main.shape), _full(w_ab.shape)],
        out_specs=[_row(tm, GDN_MAIN_W), _row(tm, LANES)],
        out_shape=[jax.ShapeDtypeStruct((t, GDN_MAIN_W), BF16), jax.ShapeDtypeStruct((t, LANES), F32)],
        compiler_params=_CP,
        name="odd_in_proj",
    )(x2d, g.reshape(1, d), shift, scale, w_main, w_ab)


HALO = SUBLANES_BF16


def _gdn_prep_kernel(pm_ref, pp_ref, pn_ref, ab_ref, cw_ref, alog_ref, dtb_ref, q_ref, k_ref, v_ref, gb_ref):
    i = pl.program_id(1)
    n_i = pl.num_programs(1)
    tm = pm_ref.shape[1]
    half = GDN_CONV // 2
    keep_prev = (i > 0).astype(F32)
    keep_next = (i < n_i - 1).astype(F32)
    for cg in range(GDN_QKV_W // LANES):
        cols = slice(cg * LANES, (cg + 1) * LANES)
        main = pm_ref[0, :, cols].astype(F32)
        ext = jnp.concatenate([pp_ref[0, :, cols].astype(F32) * keep_prev, main,
                               pn_ref[0, :, cols].astype(F32) * keep_next], axis=0)
        acc = main * cw_ref[half:half + 1, cols]
        for j in range(GDN_CONV):
            if j != half:
                sh = pltpu.roll(ext, (half - j) % (tm + 2 * HALO), 0)[HALO:HALO + tm]
                acc = acc + sh * cw_ref[j:j + 1, cols]
        y = acc * jax.nn.sigmoid(acc)
        if cg < 2 * GDN_HK:
            y = y * lax.rsqrt(jnp.sum(y * y, axis=-1, keepdims=True) + 1e-6)
            if cg < GDN_HK:
                q_ref[0, :, cols] = y * (GDN_DK ** -0.5)
            else:
                k_ref[0, :, (cg - GDN_HK) * LANES:(cg - GDN_HK + 1) * LANES] = y
        else:
            v_ref[0, :, (cg - 2 * GDN_HK) * LANES:(cg - 2 * GDN_HK + 1) * LANES] = y
    ab = ab_ref[0]
    xa = ab + dtb_ref[...]
    softplus = jnp.maximum(xa, 0.0) + jnp.log1p(jnp.exp(-jnp.abs(xa)))
    lane = lax.broadcasted_iota(jnp.int32, ab.shape, 1)
    gb_ref[0] = jnp.where(lane < 2 * GDN_HV, -jnp.exp(alog_ref[...]) * softplus, jax.nn.sigmoid(ab))


def _gdn_prep(p_main, p_ab, conv_w, a_log, dt_bias, tm):
    b, t, _ = p_main.shape
    nb = tm // HALO
    last = t // HALO - 1
    pad = lambda a: jnp.zeros((1, LANES), F32).at[0, :2 * GDN_HV].set(a.reshape(-1))
    f = lambda n: jax.ShapeDtypeStruct((b, t, n), F32)
    seq = lambda n: pl.BlockSpec((1, tm, n), lambda bi, i: (bi, i, 0))
    return pl.pallas_call(
        _gdn_prep_kernel,
        grid=(b, t // tm),
        in_specs=[seq(GDN_QKV_W),
                  pl.BlockSpec((1, HALO, GDN_QKV_W), lambda bi, i: (bi, jnp.maximum(i * nb - 1, 0), 0)),
                  pl.BlockSpec((1, HALO, GDN_QKV_W), lambda bi, i: (bi, jnp.minimum((i + 1) * nb, last), 0)),
                  seq(LANES),
                  pl.BlockSpec((GDN_CONV, GDN_QKV_W), lambda bi, i: (0, 0)),
                  pl.BlockSpec((1, LANES), lambda bi, i: (0, 0)),
                  pl.BlockSpec((1, LANES), lambda bi, i: (0, 0))],
        out_specs=[seq(GDN_QK_W), seq(GDN_QK_W), seq(GDN_V_W), seq(LANES)],
        out_shape=[f(GDN_QK_W), f(GDN_QK_W), f(GDN_V_W), f(LANES)],
        compiler_params=_CP,
        name="gdn_prep",
    )(p_main, p_main, p_main, p_ab, conv_w, pad(a_log), pad(dt_bias))


def _gdn_chunk_kernel(q_ref, k_ref, v_ref, gb_ref, s0_ref, o_ref, sfin_ref, s_ref, *, reverse, g_lane, b_lane):
    c = pl.program_id(0)
    n_c = pl.num_programs(0)
    cs = GDN_CHUNK
    rep = GDN_HV // GDN_HK
    nb = q_ref.shape[0]

    @pl.when(c == 0)
    def _():
        s_ref[...] = s0_ref[...].reshape(s_ref.shape)

    row = lax.broadcasted_iota(jnp.int32, (cs, cs), 0)
    col = lax.broadcasted_iota(jnp.int32, (cs, cs), 1)
    incl = (row <= col) if reverse else (row >= col)
    strict = (row < col) if reverse else (row > col)
    eye = (row == col).astype(F32)
    incl_b = incl.astype(BF16)
    last = 0 if reverse else cs - 1
    tn = (((0,), (1,)), ((), ()))

    heads = range(GDN_HV)
    gc, gr, bc, qh, kh, vh = [], [], [], [], [], []
    for b in range(nb):
        gb = gb_ref[b]
        gb_hi = gb.astype(BF16)
        gb_lo = (gb - gb_hi.astype(F32)).astype(BF16)
        g_col = (jnp.dot(incl_b, gb_hi, preferred_element_type=F32)
                 + jnp.dot(incl_b, gb_lo, preferred_element_type=F32))
        g_row = (lax.dot_general(gb_hi, incl_b, tn, preferred_element_type=F32)
                 + lax.dot_general(gb_lo, incl_b, tn, preferred_element_type=F32))
        gc += [g_col[:, g_lane + h:g_lane + h + 1] for h in heads]
        gr += [g_row[g_lane + h:g_lane + h + 1, :] for h in heads]
        bc += [gb[:, b_lane + h:b_lane + h + 1] for h in heads]
        qh += [q_ref[b, :, h * GDN_DK:(h + 1) * GDN_DK] for h in range(GDN_HK)]
        kh += [k_ref[b, :, h * GDN_DK:(h + 1) * GDN_DK] for h in range(GDN_HK)]
        vh += [v_ref[b, :, h * GDN_DV:(h + 1) * GDN_DV] for h in heads]
    gc, gr, bc = jnp.stack(gc), jnp.stack(gr), jnp.stack(bc)
    qh, kh, vh = jnp.stack(qh), jnp.stack(kh), jnp.stack(vh)
    ge = gc[:, last:last + 1, :]
    kh_b = kh.astype(BF16)
    bnt = (((2,), (2,)), ((0,), (0,)))
    kk = lax.dot_general(kh_b, kh_b, bnt, preferred_element_type=F32)
    qk = lax.dot_general(qh.astype(BF16), kh_b, bnt, preferred_element_type=F32)
    kk, qk = jnp.repeat(kk, rep, axis=0), jnp.repeat(qk, rep, axis=0)
    qv, kv = jnp.repeat(qh, rep, axis=0), jnp.repeat(kh, rep, axis=0)

    decay = jnp.where(incl, jnp.exp(jnp.where(incl, gc - gr, 0.0)), 0.0)
    lm = jnp.where(strict, bc * kk * decay, 0.0)
    bmm = lambda a, b: jnp.einsum('hij,hjk->hik', a.astype(BF16), b.astype(BF16), preferred_element_type=F32)
    blk = lambda n: (row // n) == (col // n)
    l0 = jnp.where(blk(16), lm, 0.0)
    p = bmm(l0, l0)
    x = eye - l0
    for _ in range(2):
        xp = bmm(jnp.concatenate([x, p], axis=1), p)
        x = x + xp[:, :cs]
        p = xp[:, cs:]
    x = x + bmm(x, p)
    n = 32
    while n <= cs:
        off = jnp.where(blk(n) & ~blk(n // 2), lm, 0.0)
        x = x - bmm(bmm(x, off), x)
        n *= 2
    eg = jnp.exp(gc)
    uw = bmm(x, jnp.concatenate([vh * bc, kv * (bc * eg)], axis=2))
    u = uw[:, :, :GDN_DV]
    w = uw[:, :, GDN_DV:]
    qg = qv * eg
    intra = jnp.where(incl, qk * decay, 0.0)
    kt = kv * jnp.exp(ge - gc)
    s = s_ref[...]
    wq = bmm(jnp.concatenate([w, qg], axis=1), s)
    v_new = u - wq[:, :cs]
    o = wq[:, cs:] + bmm(intra, v_new)
    for b in range(nb):
        for h in heads:
            o_ref[b, :, h * GDN_DV:(h + 1) * GDN_DV] = o[b * GDN_HV + h]
    s_ref[...] = s * jnp.exp(ge) + jnp.einsum('hck,hcv->hkv', kt.astype(BF16), v_new.astype(BF16),
                                              preferred_element_type=F32)

    @pl.when(c == n_c - 1)
    def _():
        sfin_ref[...] = s_ref[...].reshape(sfin_ref.shape)


def _gdn_scan(q, k, v, gb, s0, reverse):
    b, t, _ = q.shape
    assert t % GDN_CHUNK == 0 and GDN_CHUNK % 32 == 0
    n_c = t // GDN_CHUNK
    cm = (lambda ci: (0, n_c - 1 - ci, 0)) if reverse else (lambda ci: (0, ci, 0))
    d = 1 if reverse else 0
    smap = lambda ci: (0, 0, 0, 0)
    return pl.pallas_call(
        functools.partial(_gdn_chunk_kernel, reverse=reverse, g_lane=d * GDN_HV, b_lane=(2 + d) * GDN_HV),
        grid=(n_c,),
        in_specs=[pl.BlockSpec((b, GDN_CHUNK, GDN_QK_W), cm), pl.BlockSpec((b, GDN_CHUNK, GDN_QK_W), cm),
                  pl.BlockSpec((b, GDN_CHUNK, GDN_V_W), cm), pl.BlockSpec((b, GDN_CHUNK, LANES), cm),
                  pl.BlockSpec((b, GDN_HV, GDN_DK, GDN_DV), smap)],
        out_specs=[pl.BlockSpec((b, GDN_CHUNK, GDN_V_W), cm), pl.BlockSpec((b, GDN_HV, GDN_DK, GDN_DV), smap)],
        out_shape=[jax.ShapeDtypeStruct((b, t, GDN_V_W), F32),
                   jax.ShapeDtypeStruct((b, GDN_HV, GDN_DK, GDN_DV), F32)],
        scratch_shapes=[pltpu.VMEM((b * GDN_HV, GDN_DK, GDN_DV), F32)],
        compiler_params=_CP,
        name="gdn_scan_bwd" if reverse else "gdn_scan_fwd",
    )(q, k, v, gb, s0)


def _odd_out_kernel(of_ref, ob_ref, z_ref, ng_ref, w_ref, x_ref, gate_ref, g_ref, shift_ref, scale_ref,
                    wr_ref, br_ref, xo_ref, h_ref, idx_ref, gates_ref):
    parts = []
    for h in range(GDN_HV):
        cols = slice(h * GDN_DV, (h + 1) * GDN_DV)
        o = of_ref[:, cols] + ob_ref[:, cols]
        z = z_ref[:, cols].astype(F32)
        o = o * lax.rsqrt(jnp.mean(o * o, axis=-1, keepdims=True) + NORM_EPS) * ng_ref[...]
        parts.append((o * (z * jax.nn.sigmoid(z))).astype(BF16))
    y = jnp.dot(jnp.concatenate(parts, axis=1), w_ref[...], preferred_element_type=F32)
    _residual_norm_route(y, x_ref, gate_ref, g_ref, shift_ref, scale_ref, wr_ref, br_ref,
                         xo_ref, h_ref, idx_ref, gates_ref)


def _odd_out_proj(o_f, o_b, p_main, norm_g, w, x2d, gate, g, shift, scale, w_router, b_router, rows_per_vec, tm):
    t, d = x2d.shape
    wr, br = _router_operands(w_router, b_router)
    vec = _vec(d, rows_per_vec, tm)
    out_specs, out_shape = _route_out(t, d, tm)
    return pl.pallas_call(
        _odd_out_kernel,
        grid=(t // tm,),
        in_specs=[_row(tm, GDN_V_W), _row(tm, GDN_V_W),
                  pl.BlockSpec((tm, GDN_V_W), lambda i: (i, GDN_QKV_W // GDN_V_W)),
                  _full((1, GDN_DV)), _full(w.shape), _row(tm, d), vec, _full((1, d)), vec, vec,
                  _full((d, LANES)), _full((1, LANES))],
        out_specs=out_specs, out_shape=out_shape,
        compiler_params=_CP,
        name="odd_out_proj",
    )(o_f, o_b, p_main, norm_g.reshape(1, GDN_DV), w, x2d, gate, g.reshape(1, d), shift, scale, wr, br)


def _split_mod(mv, bsz):
    d = mv.shape[1] // N_MOD
    lat = [mv[:bsz, j * d:(j + 1) * d][:, None, :] for j in range(N_MOD)]
    ctx = [mv[bsz:bsz + 1, j * d:(j + 1) * d][:, None, :] for j in range(N_MOD)]
    return lat, ctx


def _even_layer(x2d, xc2d, mod, mod_c, norm1_g, norm2_g, w_in, w_out, lam_p, subln_g, lam_init, cos, sin,
                w_router, b_router, bsz, n_lat, n_ctx):
    d = x2d.shape[1]
    w_in_b = w_in.astype(BF16)
    w_out_b = w_out.astype(BF16)
    q_scale = DA_DH ** -0.5 * math.log2(math.e)
    f, q, k_all, v_all = _even_in_proj(x2d, norm1_g, mod[0], mod[1], w_in_b, cos, sin, bsz, n_lat, n_lat + n_ctx,
                                       q_scale, TM_PROJ)
    ones, zeros = jnp.ones((n_ctx, ROPE_AXIS_DIM), F32), jnp.zeros((n_ctx, ROPE_AXIS_DIM), F32)
    fc, qc, kc, vc = _even_in_proj(xc2d, norm1_g, mod_c[0], mod_c[1], w_in_b, ones, zeros, bsz, n_ctx, n_ctx,
                                   q_scale, n_ctx)
    lp = lam_p.astype(F32)
    lam = jnp.exp(jnp.sum(lp[0] * lp[1])) - jnp.exp(jnp.sum(lp[2] * lp[3])) + lam_init
    k_all = lax.dynamic_update_slice(k_all, kc, (0, n_lat, 0))
    v_all = lax.dynamic_update_slice(v_all, vc, (0, n_lat, 0))
    o = _diff_attention(lam, q, k_all, v_all, subln_g, 1.0 - lam_init, TQ_ATTN, TK_ATTN)
    oc = _diff_attention(lam, qc, kc, vc, subln_g, 1.0 - lam_init, n_ctx, n_ctx)
    fm = _fourier_mix(f.reshape(bsz, n_lat, FOURIER_W)).reshape(bsz * n_lat, FOURIER_W)
    fmc = _fourier_mix(fc.reshape(bsz, n_ctx, FOURIER_W)).reshape(bsz * n_ctx, FOURIER_W)
    lat = _even_out_proj(fm, o.reshape(bsz * n_lat, DA_W), w_out_b, x2d, mod[2], norm2_g, mod[3], mod[4],
                         w_router, b_router, n_lat, TM_PROJ)
    ctx = _even_out_proj(fmc, oc.reshape(bsz * n_ctx, DA_W), w_out_b, xc2d, mod_c[2], norm2_g, mod_c[3], mod_c[4],
                         w_router, b_router, bsz * n_ctx, n_ctx)
    return lat, ctx


def _odd_layer(x2d, xc2d, mod, mod_c, norm1_g, norm2_g, w_in, conv_w, a_log, dt_bias, norm_g, w_out,
               w_router, b_router, bsz, n_lat, n_ctx):
    d = x2d.shape[1]
    w_main = w_in[:, :GDN_MAIN_W].astype(BF16)
    w_ab = jnp.zeros((d, LANES), BF16).at[:, :4 * GDN_HV].set(w_in[:, GDN_MAIN_W:].astype(BF16))
    p_main, p_ab = _odd_in_proj(x2d, norm1_g, mod[0], mod[1], w_main, w_ab, n_lat, TM_PROJ)
    q, k, v, gb = _gdn_prep(p_main.reshape(bsz, n_lat, GDN_MAIN_W), p_ab.reshape(bsz, n_lat, LANES),
                            conv_w, a_log, dt_bias, TM_GDN)
    pc_main, pc_ab = _odd_in_proj(xc2d, norm1_g, mod_c[0], mod_c[1], w_main, w_ab, bsz * n_ctx, n_ctx)
    qc, kc, vc, gbc = _gdn_prep(pc_main.reshape(bsz, n_ctx, GDN_MAIN_W), pc_ab.reshape(bsz, n_ctx, LANES),
                                conv_w, a_log, dt_bias, n_ctx)
    s0 = jnp.zeros((bsz, GDN_HV, GDN_DK, GDN_DV), F32)
    _, sc_f = _gdn_scan(qc, kc, vc, gbc, s0, False)
    o_f, _ = _gdn_scan(q, k, v, gb, sc_f, False)
    _, sc_b = _gdn_scan(qc, kc, vc, gbc, s0, True)
    o_b, _ = _gdn_scan(q, k, v, gb, sc_b, True)
    return _odd_out_proj(o_f.reshape(bsz * n_lat, GDN_V_W), o_b.reshape(bsz * n_lat, GDN_V_W), p_main, norm_g,
                         w_out.astype(BF16), x2d, mod[2], norm2_g, mod[3], mod[4], w_router, b_router, n_lat, TM_GDN)


def kernel(x, c, ctx, c_ctx, norm1_g, norm2_g, w_mod, b_mod, ev_w_in, ev_w_out, ev_lam, ev_subln_g,
           od_w_in, od_conv_w, od_a_log, od_dt_bias, od_norm_g, od_w_out,
           moe_w_router, moe_b_router, moe_w1, moe_b1, moe_w2, moe_b2, final_g):
    bsz, n_lat, d = x.shape
    n_ctx = ctx.shape[1]
    assert w_mod.shape[0] == 2, "kernel is written for one even (attention) and one odd (DeltaNet) layer"
    t_lat = bsz * n_lat
    cos, sin = _axial_rope_tables(n_lat // GRID_W)
    c_rows = jnp.zeros((8, d), F32).at[:bsz].set(c).at[bsz].set(c_ctx)
    x2d = x.reshape(t_lat, d)
    xc2d = ctx.reshape(bsz * n_ctx, d)

    mod, mod_c = _split_mod(_mod_vectors(c_rows, w_mod, b_mod, 0), bsz)
    (x2d, h2, idx, gates), (xc2d, h2c, idx_c, gates_c) = _even_layer(
        x2d, xc2d, mod, mod_c, norm1_g[0], norm2_g[0], ev_w_in[0], ev_w_out[0], ev_lam[0], ev_subln_g[0],
        _diff_lambda_init(0), cos, sin, moe_w_router[0], moe_b_router[0], bsz, n_lat, n_ctx)
    yb, pos = _moe_ffn(jnp.concatenate([h2, h2c], axis=0), jnp.concatenate([idx, idx_c], axis=0),
                       moe_w1, moe_b1[0], moe_w2, moe_b2[0], 0, TM_PROJ)
    x2d = _moe_combine(yb, pos[:t_lat], gates, x2d, mod[5], n_lat, TM_PROJ)
    xc2d = _moe_combine(yb, pos[t_lat:], gates_c, xc2d, mod_c[5], bsz * n_ctx, n_ctx)

    mod, mod_c = _split_mod(_mod_vectors(c_rows, w_mod, b_mod, 1), bsz)
    x2d, h2, idx, gates = _odd_layer(x2d, xc2d, mod, mod_c, norm1_g[1], norm2_g[1], od_w_in[0], od_conv_w[0],
                                     od_a_log[0], od_dt_bias[0], od_norm_g[0], od_w_out[0],
                                     moe_w_router[1], moe_b_router[1], bsz, n_lat, n_ctx)
    yb, pos = _moe_ffn(h2, idx, moe_w1, moe_b1[1], moe_w2, moe_b2[1], 1, TM_PROJ)
    return _moe_combine(yb, pos, gates, x2d, mod[5], n_lat, TM_PROJ, final_g=final_g).reshape(bsz, n_lat, d)
```

```python
import functools
import math

import jax
import jax.numpy as jnp
from jax import lax
from jax.experimental import pallas as pl
from jax.experimental.pallas import tpu as pltpu

D_MODEL = 1024
N_MOD = 6
NORM_EPS = 1e-6
GRID_W = 64

FOURIER_GROUPS = 4
FOURIER_GD = 64
FOURIER_W = FOURIER_GROUPS * FOURIER_GD
DA_HEADS = 6
DA_DH = 64
DA_VD = 2 * DA_DH
DA_W = DA_HEADS * DA_VD
ROPE_BASE = 10000.0
ROPE_AXIS_DIM = DA_DH // 2
SUBLN_EPS = 1e-5

GDN_HK = 8
GDN_HV = 16
GDN_DK = 128
GDN_DV = 128
GDN_QK_W = GDN_HK * GDN_DK
GDN_V_W = GDN_HV * GDN_DV
GDN_QKV_W = 2 * GDN_QK_W + GDN_V_W
GDN_MAIN_W = GDN_QKV_W + GDN_V_W
GDN_CONV = 5
GDN_CHUNK = 64

N_EXPERTS = 32
TOP_K = 4
D_FF = 1024
SWIGLU_LIMIT = 7.0
SWIGLU_ALPHA = 1.702
MOE_BLOCK = 512
MOE_PARTS = 4

LANES = 128
SUBLANES_BF16 = 16
MXU_DIM = 256
VMEM_LIMIT = 56 * 1024 * 1024
BF16 = jnp.bfloat16
F32 = jnp.float32
HI = lax.Precision.HIGHEST

TM_PROJ = 512
TM_GDN = 256
TQ_ATTN, TK_ATTN = 512, 1280

_CP = pltpu.CompilerParams(vmem_limit_bytes=VMEM_LIMIT)


def _diff_lambda_init(layer_idx):
    return 0.8 - 0.6 * math.exp(-0.3 * layer_idx)


def _axial_rope_tables(rows):
    t = jnp.arange(rows * GRID_W, dtype=jnp.int32)
    row = (t // GRID_W).astype(F32)
    col = (t % GRID_W).astype(F32)
    inv = ROPE_BASE ** (-jnp.arange(0, ROPE_AXIS_DIM, 2, dtype=F32) / ROPE_AXIS_DIM)
    ang = jnp.concatenate([row[:, None] * inv, col[:, None] * inv], axis=-1)
    return jnp.cos(ang), jnp.sin(ang)


def _norm_mod(x, g_ref, shift_ref, scale_ref):
    h = x * lax.rsqrt(jnp.mean(x * x, axis=-1, keepdims=True) + NORM_EPS) * g_ref[...]
    return (h * (1.0 + scale_ref[0]) + shift_ref[0]).astype(BF16)


def _route(logits):
    lane = lax.broadcasted_iota(jnp.int32, logits.shape, 1)
    lane_f = lane.astype(F32)
    neg = jnp.float32(-jnp.inf)
    rest = jnp.where(lane < N_EXPERTS, logits, neg)
    idx = jnp.zeros(logits.shape, F32)
    val = jnp.full(logits.shape, neg, F32)
    for j in range(TOP_K):
        m = jnp.max(rest, axis=-1, keepdims=True)
        sel = jnp.min(jnp.where(rest == m, lane_f, float(LANES)), axis=-1, keepdims=True)
        idx = jnp.where(lane == j, sel, idx)
        val = jnp.where(lane == j, m, val)
        rest = jnp.where(lane_f == sel, neg, rest)
    e = jnp.exp(val - val[:, 0:1])
    return idx, e / jnp.sum(e, axis=-1, keepdims=True)


def _residual_norm_route(y, x_ref, gate_ref, g_ref, shift_ref, scale_ref, wr_ref, br_ref,
                         xo_ref, h_ref, idx_ref, gates_ref):
    xn = x_ref[...] + gate_ref[0] * y
    xo_ref[...] = xn
    h = _norm_mod(xn, g_ref, shift_ref, scale_ref)
    h_ref[...] = h
    idx, gates = _route(jnp.dot(h, wr_ref[...], preferred_element_type=F32) + br_ref[...])
    idx_ref[...] = idx.astype(jnp.int32)
    gates_ref[...] = gates


def _row(tm, n):
    return pl.BlockSpec((tm, n), lambda i: (i, 0))


def _full(shape):
    return pl.BlockSpec(shape, lambda i: (0,) * len(shape))


def _vec(d, rows_per_vec, tm):
    per = rows_per_vec // tm
    return pl.BlockSpec((1, 1, d), lambda i: (i // per, 0, 0))


def _router_operands(w_router, b_router):
    d = w_router.shape[0]
    wr = jnp.zeros((d, LANES), BF16).at[:, :N_EXPERTS].set(w_router.astype(BF16))
    br = jnp.zeros((1, LANES), F32).at[0, :N_EXPERTS].set(b_router)
    return wr, br


def _route_out(t, d, tm):
    specs = [_row(tm, d), _row(tm, d), _row(tm, LANES), _row(tm, LANES)]
    shapes = [jax.ShapeDtypeStruct((t, d), F32), jax.ShapeDtypeStruct((t, d), BF16),
              jax.ShapeDtypeStruct((t, LANES), jnp.int32), jax.ShapeDtypeStruct((t, LANES), F32)]
    return specs, shapes


def _mod_kernel(c_ref, w_ref, b_ref, o_ref):
    c = c_ref[...]
    s = (c * jax.nn.sigmoid(c)).astype(BF16)
    o_ref[...] = jnp.dot(s, w_ref[0].astype(BF16), preferred_element_type=F32) + b_ref[0]


def _mod_vectors(c_rows, w_mod, b_mod, layer):
    d = c_rows.shape[1]
    return pl.pallas_call(
        _mod_kernel,
        grid=(N_MOD,),
        in_specs=[pl.BlockSpec(c_rows.shape, lambda n: (0, 0)),
                  pl.BlockSpec((1, d, d), lambda n: (layer, 0, n)),
                  pl.BlockSpec((1, 1, d), lambda n: (layer, 0, n))],
        out_specs=pl.BlockSpec((c_rows.shape[0], d), lambda n: (0, n)),
        out_shape=jax.ShapeDtypeStruct((c_rows.shape[0], N_MOD * d), F32),
        compiler_params=_CP,
        name="mod_vectors",
    )(c_rows, w_mod, b_mod.reshape(b_mod.shape[0], 1, -1))


def _even_in_kernel(x_ref, g_ref, shift_ref, scale_ref, w_ref, rc_ref, rs_ref, f_ref, q_ref, k_ref, v_ref, *, q_scale):
    h = _norm_mod(x_ref[...], g_ref, shift_ref, scale_ref)
    p = jnp.dot(h, w_ref[...], preferred_element_type=F32)
    f_ref[...] = p[:, :FOURIER_W]
    rc, rs = rc_ref[...], rs_ref[...]
    lane = lax.broadcasted_iota(jnp.int32, rc.shape, 1)
    first = (lane % DA_DH) < ROPE_AXIS_DIM

    def rope(t):
        partner = jnp.where(first, pltpu.roll(t, LANES - ROPE_AXIS_DIM, 1), pltpu.roll(t, ROPE_AXIS_DIM, 1))
        return t * rc + partner * rs

    for hd in range(DA_HEADS):
        cq = slice(FOURIER_W + hd * DA_VD, FOURIER_W + (hd + 1) * DA_VD)
        ck = slice(FOURIER_W + DA_W + hd * DA_VD, FOURIER_W + DA_W + (hd + 1) * DA_VD)
        q_ref[0, :, hd * DA_VD:(hd + 1) * DA_VD] = (rope(p[:, cq]) * q_scale).astype(BF16)
        k_ref[0, :, hd * DA_VD:(hd + 1) * DA_VD] = rope(p[:, ck]).astype(BF16)
    v_ref[0] = p[:, FOURIER_W + 2 * DA_W:].astype(BF16)


def _even_in_proj(x2d, g, shift, scale, w, cos, sin, bsz, n_seq, n_keys, q_scale, tm):
    t, d = x2d.shape
    bpb = n_seq // tm
    rc = jnp.tile(cos, (1, LANES // ROPE_AXIS_DIM))
    rs = jnp.tile(jnp.concatenate([-sin, sin], axis=1), (1, LANES // DA_DH))
    tab = pl.BlockSpec((tm, LANES), lambda i: (i % bpb, 0))
    seq = pl.BlockSpec((1, tm, DA_W), lambda i: (i // bpb, i % bpb, 0))
    vec = _vec(d, n_seq * (bsz // shift.shape[0]), tm)
    return pl.pallas_call(
        functools.partial(_even_in_kernel, q_scale=q_scale),
        grid=(t // tm,),
        in_specs=[_row(tm, d), _full((1, d)), vec, vec, _full(w.shape), tab, tab],
        out_specs=[_row(tm, FOURIER_W), seq, seq, seq],
        out_shape=[jax.ShapeDtypeStruct((t, FOURIER_W), F32), jax.ShapeDtypeStruct((bsz, n_seq, DA_W), BF16),
                   jax.ShapeDtypeStruct((bsz, n_keys, DA_W), BF16), jax.ShapeDtypeStruct((bsz, n_keys, DA_W), BF16)],
        compiler_params=_CP,
        name="even_in_proj",
    )(x2d, g.reshape(1, d), shift, scale, w, rc, rs)


def _diff_attn_kernel(lam_ref, q_ref, k_ref, v_ref, g_ref, o_ref, qs_ref, s_ref, m_ref, acc_ref, *,
                      tk, n_sub, out_scale):
    tq = q_ref.shape[1]
    n_kv = k_ref.shape[1] // tk
    rb = 2 * tq // n_sub
    q = q_ref[0]
    lane = lax.broadcasted_iota(jnp.int32, q.shape, 1)
    zero = jnp.zeros_like(q)
    qs_ref[:tq] = jnp.where(lane < DA_DH, q, zero)
    qs_ref[tq:] = jnp.where(lane >= DA_DH, q, zero)
    m_ref[...] = jnp.full(m_ref.shape, -1e30, F32)
    acc_ref[...] = jnp.zeros(acc_ref.shape, F32)
    ones = jnp.ones((tk, LANES), BF16)

    def scores(i, slot):
        off = pl.multiple_of(i * tk, tk)
        k = k_ref[0, pl.ds(off, tk), :]
        s_ref[slot] = lax.dot_general(qs_ref[...], k, (((1,), (1,)), ((), ())), preferred_element_type=F32)

    def consume(i, slot):
        off = pl.multiple_of(i * tk, tk)
        v_ext = jnp.concatenate([v_ref[0, pl.ds(off, tk), :], ones], axis=1)
        for r in range(n_sub):
            rows = pl.ds(r * rb, rb)
            s = s_ref[slot, rows, :]
            m_prev = m_ref[rows, :]
            m_new = jnp.maximum(m_prev, jnp.max(s, axis=1, keepdims=True))
            alpha = jnp.exp2(m_prev - m_new)
            p = jnp.exp2(s - jnp.tile(m_new, (1, tk // LANES)))
            pv = jnp.dot(p.astype(BF16), v_ext, preferred_element_type=F32)
            acc_ref[rows, :] = acc_ref[rows, :] * jnp.tile(alpha, (1, 2)) + pv
            m_ref[rows, :] = m_new

    scores(0, 0)

    def body(j, carry):
        scores(2 * j + 1, 1)
        consume(2 * j, 0)
        scores(2 * j + 2, 0)
        consume(2 * j + 1, 1)
        return carry

    lax.fori_loop(0, (n_kv - 1) // 2, body, 0)
    consume(n_kv - 1, 0)
    acc = acc_ref[...]
    o1 = acc[:tq, :LANES] / acc[:tq, LANES:]
    o2 = acc[tq:, :LANES] / acc[tq:, LANES:]
    o = o1 - lam_ref[0] * o2
    ms = jnp.mean(o * o, axis=-1, keepdims=True)
    o = o * lax.rsqrt(ms + SUBLN_EPS) * g_ref[...] * out_scale
    o_ref[0] = o.astype(o_ref.dtype)


def _diff_attention(lam, q, k_all, v_all, subln_g, out_scale, tq, tk, n_sub=2):
    b, n, _ = q.shape
    nk = k_all.shape[1]
    assert n % tq == 0 and nk % tk == 0 and tk % MXU_DIM == 0 and (nk // tk) % 2 == 1
    grid_spec = pltpu.PrefetchScalarGridSpec(
        num_scalar_prefetch=1,
        grid=(b, DA_HEADS, n // tq),
        in_specs=[pl.BlockSpec((1, tq, DA_VD), lambda bi, hi, qi, lam_r: (bi, qi, hi)),
                  pl.BlockSpec((1, nk, DA_VD), lambda bi, hi, qi, lam_r: (bi, 0, hi)),
                  pl.BlockSpec((1, nk, DA_VD), lambda bi, hi, qi, lam_r: (bi, 0, hi)),
                  pl.BlockSpec((1, DA_VD), lambda bi, hi, qi, lam_r: (0, 0))],
        out_specs=pl.BlockSpec((1, tq, DA_VD), lambda bi, hi, qi, lam_r: (bi, qi, hi)),
        scratch_shapes=[pltpu.VMEM((2 * tq, LANES), BF16), pltpu.VMEM((2, 2 * tq, tk), F32),
                        pltpu.VMEM((2 * tq, LANES), F32), pltpu.VMEM((2 * tq, 2 * LANES), F32)],
    )
    return pl.pallas_call(
        functools.partial(_diff_attn_kernel, tk=tk, n_sub=n_sub, out_scale=out_scale),
        grid_spec=grid_spec,
        out_shape=jax.ShapeDtypeStruct((b, n, DA_W), BF16),
        compiler_params=_CP,
        name="diff_attention",
    )(lam.reshape(1), q, k_all, v_all, subln_g.reshape(1, DA_VD))


def _fft_stage1_kernel(x_ref, fch_ref, f1_ref, tc_ref, ts_ref, yr_ref, yi_ref, *, nb):
    w = FOURIER_W
    r = x_ref.shape[1]
    for bl in range(nb):
        cols = slice(bl * w, (bl + 1) * w)
        ab = jnp.dot(x_ref[0, :, cols], fch_ref[...], preferred_element_type=F32, precision=HI)
        z = jnp.concatenate([ab[:, :w], ab[:, w:]], axis=0)
        y = jnp.dot(f1_ref[...], z, preferred_element_type=F32, precision=HI)
        yr, yi = y[:r], y[r:]
        tc, ts = tc_ref[:, cols], ts_ref[:, cols]
        yr_ref[0, :, cols] = yr * tc + yi * ts
        yi_ref[0, :, cols] = yi * tc - yr * ts


def _fft_stage2_kernel(yr_ref, yi_ref, f2_ref, o_ref, *, nc):
    w = FOURIER_W
    for cl in range(nc):
        y = jnp.concatenate([yr_ref[0, cl], yi_ref[0, cl]], axis=0)
        o_ref[0, :, cl * w:(cl + 1) * w] = jnp.dot(f2_ref[...], y, preferred_element_type=F32, precision=HI)


def _fourier_mix(f, nb=8):
    bsz, n, w = f.shape
    r = math.isqrt(n)
    assert r * r == n and r % nb == 0 and w == FOURIER_W
    two_pi = 2.0 * math.pi
    k = jnp.arange(r, dtype=jnp.int32)
    ang = two_pi * ((k[:, None] * k[None, :]) % r).astype(F32) / r
    c1, s1 = jnp.cos(ang), jnp.sin(ang)
    kc = jnp.arange(FOURIER_GD, dtype=jnp.int32)
    angc = two_pi * ((kc[:, None] * kc[None, :]) % FOURIER_GD).astype(F32) / FOURIER_GD
    eye_g = jnp.eye(FOURIER_GROUPS, dtype=F32)
    fch = jnp.concatenate([jnp.kron(eye_g, jnp.cos(angc)), jnp.kron(eye_g, jnp.sin(angc))], axis=1)
    f1 = jnp.concatenate([jnp.concatenate([c1, -s1], axis=1), jnp.concatenate([-s1, -c1], axis=1)], axis=0)
    f2 = jnp.concatenate([c1, s1], axis=1) * (1.0 / math.sqrt(n * FOURIER_GD))
    angt = two_pi * ((k[:, None] * k[None, :]) % n).astype(F32) / n
    tc = jnp.repeat(jnp.cos(angt), w, axis=1)
    ts = jnp.repeat(jnp.sin(angt), w, axis=1)
    xv = f.reshape(bsz, r, r * w)
    blk = pl.BlockSpec((1, r, nb * w), lambda bi, j: (bi, 0, j))
    tab = pl.BlockSpec((r, nb * w), lambda bi, j: (0, j))
    full = lambda a: pl.BlockSpec(a.shape, lambda bi, j: (0, 0))
    yshape = jax.ShapeDtypeStruct((bsz, r, r * w), F32)
    yr, yi = pl.pallas_call(
        functools.partial(_fft_stage1_kernel, nb=nb),
        grid=(bsz, r // nb),
        in_specs=[blk, full(fch), full(f1), tab, tab],
        out_specs=[blk, blk],
        out_shape=[yshape, yshape],
        compiler_params=_CP,
        name="fft_stage1",
    )(xv, fch, f1, tc, ts)
    y4 = lambda a: a.reshape(bsz, r, r, w)
    yblk = pl.BlockSpec((1, nb, r, w), lambda bi, j: (bi, j, 0, 0))
    out = pl.pallas_call(
        functools.partial(_fft_stage2_kernel, nc=nb),
        grid=(bsz, r // nb),
        in_specs=[yblk, yblk, full(f2)],
        out_specs=blk,
        out_shape=yshape,
        compiler_params=_CP,
        name="fft_stage2",
    )(y4(yr), y4(yi), f2)
    return out.reshape(bsz, n, w)


def _even_out_kernel(fm_ref, o_ref, w_ref, x_ref, gate_ref, g_ref, shift_ref, scale_ref, wr_ref, br_ref,
                     xo_ref, h_ref, idx_ref, gates_ref):
    y = (jnp.dot(fm_ref[...].astype(BF16), w_ref[:FOURIER_W], preferred_element_type=F32)
         + jnp.dot(o_ref[...], w_ref[FOURIER_W:], preferred_element_type=F32))
    _residual_norm_route(y, x_ref, gate_ref, g_ref, shift_ref, scale_ref, wr_ref, br_ref,
                         xo_ref, h_ref, idx_ref, gates_ref)


def _even_out_proj(fm, o, w, x2d, gate, g, shift, scale, w_router, b_router, rows_per_vec, tm):
    t, d = x2d.shape
    wr, br = _router_operands(w_router, b_router)
    vec = _vec(d, rows_per_vec, tm)
    out_specs, out_shape = _route_out(t, d, tm)
    return pl.pallas_call(
        _even_out_kernel,
        grid=(t // tm,),
        in_specs=[_row(tm, FOURIER_W), _row(tm, DA_W), _full(w.shape), _row(tm, d), vec, _full((1, d)), vec, vec,
                  _full((d, LANES)), _full((1, LANES))],
        out_specs=out_specs, out_shape=out_shape,
        compiler_params=_CP,
        name="even_out_proj",
    )(fm, o, w, x2d, gate, g.reshape(1, d), shift, scale, wr, br)


def _moe_kernel(be_ref, x_ref, w1_ref, perm_ref, b1g_ref, b1l_ref, w2_ref, b2_ref, *rest):
    o_ref, w1g_s, w1l_s, w2_s = rest[-4:]
    i = pl.program_id(0)

    @pl.when((i == 0) | (be_ref[i] != be_ref[jnp.maximum(i - 1, 0)]))
    def _():
        for c in range(w1_ref.shape[3] // MXU_DIM):
            blk = jnp.dot(w1_ref[0, 0, :, c * MXU_DIM:(c + 1) * MXU_DIM].astype(BF16), perm_ref[...],
                          preferred_element_type=F32)
            w1g_s[:, c * LANES:(c + 1) * LANES] = blk[:, :LANES].astype(BF16)
            w1l_s[:, c * LANES:(c + 1) * LANES] = blk[:, LANES:].astype(BF16)
        w2_s[...] = w2_ref[0, 0].astype(BF16)

    x = x_ref[...]
    ug = jnp.dot(x, w1g_s[...], preferred_element_type=F32) + b1g_ref[0]
    ul = jnp.dot(x, w1l_s[...], preferred_element_type=F32) + b1l_ref[0]
    glu = jnp.minimum(ug, SWIGLU_LIMIT)
    lin = jnp.clip(ul, -SWIGLU_LIMIT, SWIGLU_LIMIT)
    act = glu * jax.nn.sigmoid(SWIGLU_ALPHA * glu) * (lin + 1.0)
    y = jnp.dot(act.astype(BF16), w2_s[...], preferred_element_type=F32) + b2_ref[0]
    o_ref[...] = y.astype(o_ref.dtype)


def _moe_experts(block_expert, xb, w1_all, b1g, b1l, w2_all, layer, b2, yb_prev, first_block, n_blocks_total):
    n_rows, d = xb.shape
    n_blocks = n_rows // MOE_BLOCK
    src = jnp.arange(MXU_DIM)[:, None]
    dst = jnp.arange(MXU_DIM)[None, :]
    perm = (src == jnp.where(dst < LANES, 2 * dst, 2 * (dst - LANES) + 1)).astype(BF16)
    bspec = lambda s: pl.BlockSpec((1,) + s, lambda i, be: (be[i], 0, 0))
    wspec = lambda s: pl.BlockSpec((1, 1) + s, lambda i, be: (layer, be[i], 0, 0))
    in_specs = [pl.BlockSpec((MOE_BLOCK, d), lambda i, be: (i, 0)),
                wspec((d, 2 * D_FF)), pl.BlockSpec((MXU_DIM, MXU_DIM), lambda i, be: (0, 0)),
                bspec((1, D_FF)), bspec((1, D_FF)), wspec((D_FF, d)), bspec((1, d))]
    args = [block_expert, xb, w1_all, perm, b1g, b1l, w2_all, b2]
    aliases = {}
    if yb_prev is not None:
        in_specs.append(pl.BlockSpec(memory_space=pl.ANY))
        aliases = {len(args): 0}
        args.append(yb_prev)
    grid_spec = pltpu.PrefetchScalarGridSpec(
        num_scalar_prefetch=1,
        grid=(n_blocks,),
        in_specs=in_specs,
        out_specs=pl.BlockSpec((MOE_BLOCK, d), lambda i, be: (first_block + i, 0)),
        scratch_shapes=[pltpu.VMEM((d, D_FF), BF16), pltpu.VMEM((d, D_FF), BF16), pltpu.VMEM((D_FF, d), BF16)],
    )
    return pl.pallas_call(
        _moe_kernel,
        grid_spec=grid_spec,
        out_shape=jax.ShapeDtypeStruct((n_blocks_total * MOE_BLOCK, d), BF16),
        input_output_aliases=aliases,
        compiler_params=_CP,
        name="moe_experts",
    )(*args)


def _rank_kernel(idx_ref, rank_ref, cnt_ref, carry_ref):
    i = pl.program_id(0)

    @pl.when(i == 0)
    def _():
        carry_ref[...] = jnp.zeros(carry_ref.shape, F32)

    idx = idx_ref[...]
    tm = idx.shape[0]
    lane = lax.broadcasted_iota(jnp.int32, idx.shape, 1)
    sel = [lane == idx[:, j:j + 1] for j in range(TOP_K)]
    onehot = sel[0].astype(F32)
    for j in range(1, TOP_K):
        onehot = onehot + sel[j].astype(F32)
    row = lax.broadcasted_iota(jnp.int32, (tm, tm), 0)
    col = lax.broadcasted_iota(jnp.int32, (tm, tm), 1)
    before = (row > col).astype(BF16)
    prefix = jnp.dot(before, onehot.astype(BF16), preferred_element_type=F32) + carry_ref[0:1, :]
    rank = jnp.zeros(idx.shape, F32)
    for j in range(TOP_K):
        rank = jnp.where(lane == j, jnp.sum(jnp.where(sel[j], prefix, 0.0), axis=-1, keepdims=True), rank)
    rank_ref[...] = rank.astype(jnp.int32)
    carry_ref[0:1, :] = carry_ref[0:1, :] + jnp.sum(onehot, axis=0, keepdims=True)
    cnt_ref[...] = carry_ref[...]


def _expert_ranks(idx, tm):
    t = idx.shape[0]
    rank, cnt = pl.pallas_call(
        _rank_kernel,
        grid=(t // tm,),
        in_specs=[_row(tm, LANES)],
        out_specs=[_row(tm, LANES), _full((8, LANES))],
        out_shape=[jax.ShapeDtypeStruct((t, LANES), jnp.int32), jax.ShapeDtypeStruct((8, LANES), F32)],
        scratch_shapes=[pltpu.VMEM((8, LANES), F32)],
        compiler_params=_CP,
        name="expert_ranks",
    )(idx)
    return rank[:, :TOP_K], cnt[0, :N_EXPERTS].astype(jnp.int32)


def _moe_ffn(h, idx, w1_all, b1, w2_all, b2, layer, tm):
    n_tok, d = h.shape
    n_assign = n_tok * TOP_K
    e_flat = idx[:, :TOP_K].reshape(n_assign)
    order = jnp.argsort(e_flat).astype(jnp.int32)
    rank, counts = _expert_ranks(idx, tm)
    starts = jnp.cumsum(counts) - counts
    padded = (counts + MOE_BLOCK - 1) // MOE_BLOCK * MOE_BLOCK
    pad_ends = jnp.cumsum(padded)
    pad_starts = pad_ends - padded
    n_blocks = -(-(n_assign + N_EXPERTS * (MOE_BLOCK - 1)) // MOE_BLOCK)
    n_rows = n_blocks * MOE_BLOCK
    block_start = jnp.arange(n_blocks, dtype=jnp.int32) * MOE_BLOCK
    block_expert = jnp.minimum(jnp.sum(pad_ends[None, :] <= block_start[:, None], axis=1, dtype=jnp.int32),
                               N_EXPERTS - 1)
    shift = pad_starts - starts
    pos = pad_starts[e_flat].reshape(n_tok, TOP_K) + rank
    r = jnp.arange(n_rows, dtype=jnp.int32)
    src = jnp.clip(r - jnp.repeat(shift[block_expert], MOE_BLOCK), 0, n_assign - 1)
    row_tok = order[src] // TOP_K
    assert n_blocks % MOE_PARTS == 0
    nbp = n_blocks // MOE_PARTS
    yb = None
    for part in range(MOE_PARTS):
        blocks = slice(part * nbp, (part + 1) * nbp)
        xb = h[row_tok[part * nbp * MOE_BLOCK:(part + 1) * nbp * MOE_BLOCK]]
        yb = _moe_experts(block_expert[blocks], xb, w1_all, b1[:, None, 0::2], b1[:, None, 1::2], w2_all, layer,
                          b2[:, None, :], yb, part * nbp, n_blocks)
    return yb, pos


def _combine_kernel(*refs, final):
    y_refs = refs[:TOP_K]
    gt_ref, x_ref, g5_ref = refs[TOP_K:TOP_K + 3]
    o_ref = refs[-1]
    gt = gt_ref[...]
    acc = y_refs[0][...].astype(F32) * gt[:, 0:1]
    for j in range(1, TOP_K):
        acc = acc + y_refs[j][...].astype(F32) * gt[:, j:j + 1]
    xn = x_ref[...] + g5_ref[0] * acc
    if final:
        fg_ref = refs[TOP_K + 3]
        xn = xn * lax.rsqrt(jnp.mean(xn * xn, axis=-1, keepdims=True) + NORM_EPS) * fg_ref[...]
    o_ref[...] = xn


def _moe_combine(yb, pos, gates, x2d, gate5, rows_per_vec, tm, final_g=None):
    t, d = x2d.shape
    ys = [yb[pos[:, j]] for j in range(TOP_K)]
    in_specs = [_row(tm, d)] * TOP_K + [_row(tm, LANES), _row(tm, d), _vec(d, rows_per_vec, tm)]
    args = ys + [gates, x2d, gate5]
    if final_g is not None:
        in_specs.append(_full((1, d)))
        args.append(final_g.reshape(1, d))
    return pl.pallas_call(
        functools.partial(_combine_kernel, final=final_g is not None),
        grid=(t // tm,),
        in_specs=in_specs,
        out_specs=_row(tm, d),
        out_shape=jax.ShapeDtypeStruct((t, d), F32),
        compiler_params=_CP,
        name="moe_combine",
    )(*args)


def _odd_in_kernel(x_ref, g_ref, shift_ref, scale_ref, wm_ref, wab_ref, pm_ref, pab_ref):
    h = _norm_mod(x_ref[...], g_ref, shift_ref, scale_ref)
    pm_ref[...] = jnp.dot(h, wm_ref[...], preferred_element_type=F32).astype(pm_ref.dtype)
    pab_ref[...] = jnp.dot(h, wab_ref[...], preferred_element_type=F32)


def _odd_in_proj(x2d, g, shift, scale, w_main, w_ab, rows_per_vec, tm):
    t, d = x2d.shape
    vec = _vec(d, rows_per_vec, tm)
    return pl.pallas_call(
        _odd_in_kernel,
        grid=(t // tm,),
        in_specs=[_row(tm, d), _full((1, d)), vec, vec, _full(w_main.shape), _full(w_ab.shape)],
        out_specs=[_row(tm, GDN_MAIN_W), _row(tm, LANES)],
        out_shape=[jax.ShapeDtypeStruct((t, GDN_MAIN_W), BF16), jax.ShapeDtypeStruct((t, LANES), F32)],
        compiler_params=_CP,
        name="odd_in_proj",
    )(x2d, g.reshape(1, d), shift, scale, w_main, w_ab)


HALO = SUBLANES_BF16


def _gdn_prep_kernel(pm_ref, pp_ref, pn_ref, ab_ref, cw_ref, alog_ref, dtb_ref, q_ref, k_ref, v_ref, gb_ref):
    i = pl.program_id(1)
    n_i = pl.num_programs(1)
    tm = pm_ref.shape[1]
    half = GDN_CONV // 2
    keep_prev = (i > 0).astype(F32)
    keep_next = (i < n_i - 1).astype(F32)
    for cg in range(GDN_QKV_W // LANES):
        cols = slice(cg * LANES, (cg + 1) * LANES)
        main = pm_ref[0, :, cols].astype(F32)
        ext = jnp.concatenate([pp_ref[0, :, cols].astype(F32) * keep_prev, main,
                               pn_ref[0, :, cols].astype(F32) * keep_next], axis=0)
        acc = main * cw_ref[half:half + 1, cols]
        for j in range(GDN_CONV):
            if j != half:
                sh = pltpu.roll(ext, (half - j) % (tm + 2 * HALO), 0)[HALO:HALO + tm]
                acc = acc + sh * cw_ref[j:j + 1, cols]
        y = acc * jax.nn.sigmoid(acc)
        if cg < 2 * GDN_HK:
            y = y * lax.rsqrt(jnp.sum(y * y, axis=-1, keepdims=True) + 1e-6)
            if cg < GDN_HK:
                q_ref[0, :, cols] = y * (GDN_DK ** -0.5)
            else:
                k_ref[0, :, (cg - GDN_HK) * LANES:(cg - GDN_HK + 1) * LANES] = y
        else:
            v_ref[0, :, (cg - 2 * GDN_HK) * LANES:(cg - 2 * GDN_HK + 1) * LANES] = y
    ab = ab_ref[0]
    xa = ab + dtb_ref[...]
    softplus = jnp.maximum(xa, 0.0) + jnp.log1p(jnp.exp(-jnp.abs(xa)))
    lane = lax.broadcasted_iota(jnp.int32, ab.shape, 1)
    gb_ref[0] = jnp.where(lane < 2 * GDN_HV, -jnp.exp(alog_ref[...]) * softplus, jax.nn.sigmoid(ab))


def _gdn_prep(p_main, p_ab, conv_w, a_log, dt_bias, tm):
    b, t, _ = p_main.shape
    nb = tm // HALO
    last = t // HALO - 1
    pad = lambda a: jnp.zeros((1, LANES), F32).at[0, :2 * GDN_HV].set(a.reshape(-1))
    f = lambda n: jax.ShapeDtypeStruct((b, t, n), F32)
    seq = lambda n: pl.BlockSpec((1, tm, n), lambda bi, i: (bi, i, 0))
    return pl.pallas_call(
        _gdn_prep_kernel,
        grid=(b, t // tm),
        in_specs=[seq(GDN_QKV_W),
                  pl.BlockSpec((1, HALO, GDN_QKV_W), lambda bi, i: (bi, jnp.maximum(i * nb - 1, 0), 0)),
                  pl.BlockSpec((1, HALO, GDN_QKV_W), lambda bi, i: (bi, jnp.minimum((i + 1) * nb, last), 0)),
                  seq(LANES),
                  pl.BlockSpec((GDN_CONV, GDN_QKV_W), lambda bi, i: (0, 0)),
                  pl.BlockSpec((1, LANES), lambda bi, i: (0, 0)),
                  pl.BlockSpec((1, LANES), lambda bi, i: (0, 0))],
        out_specs=[seq(GDN_QK_W), seq(GDN_QK_W), seq(GDN_V_W), seq(LANES)],
        out_shape=[f(GDN_QK_W), f(GDN_QK_W), f(GDN_V_W), f(LANES)],
        compiler_params=_CP,
        name="gdn_prep",
    )(p_main, p_main, p_main, p_ab, conv_w, pad(a_log), pad(dt_bias))


def _gdn_chunk_kernel(q_ref, k_ref, v_ref, gb_ref, s0_ref, o_ref, sfin_ref, s_ref, *, reverse, g_lane, b_lane):
    c = pl.program_id(0)
    n_c = pl.num_programs(0)
    cs = GDN_CHUNK
    rep = GDN_HV // GDN_HK
    nb = q_ref.shape[0]

    @pl.when(c == 0)
    def _():
        s_ref[...] = s0_ref[...].reshape(s_ref.shape)

    row = lax.broadcasted_iota(jnp.int32, (cs, cs), 0)
    col = lax.broadcasted_iota(jnp.int32, (cs, cs), 1)
    incl = (row <= col) if reverse else (row >= col)
    strict = (row < col) if reverse else (row > col)
    eye = (row == col).astype(F32)
    incl_b = incl.astype(BF16)
    last = 0 if reverse else cs - 1
    tn = (((0,), (1,)), ((), ()))

    heads = range(GDN_HV)
    gc, gr, bc, qh, kh, vh = [], [], [], [], [], []
    for b in range(nb):
        gb = gb_ref[b]
        gb_hi = gb.astype(BF16)
        gb_lo = (gb - gb_hi.astype(F32)).astype(BF16)
        g_col = (jnp.dot(incl_b, gb_hi, preferred_element_type=F32)
                 + jnp.dot(incl_b, gb_lo, preferred_element_type=F32))
        g_row = (lax.dot_general(gb_hi, incl_b, tn, preferred_element_type=F32)
                 + lax.dot_general(gb_lo, incl_b, tn, preferred_element_type=F32))
        gc += [g_col[:, g_lane + h:g_lane + h + 1] for h in heads]
        gr += [g_row[g_lane + h:g_lane + h + 1, :] for h in heads]
        bc += [gb[:, b_lane + h:b_lane + h + 1] for h in heads]
        qh += [q_ref[b, :, h * GDN_DK:(h + 1) * GDN_DK] for h in range(GDN_HK)]
        kh += [k_ref[b, :, h * GDN_DK:(h + 1) * GDN_DK] for h in range(GDN_HK)]
        vh += [v_ref[b, :, h * GDN_DV:(h + 1) * GDN_DV] for h in heads]
    gc, gr, bc = jnp.stack(gc), jnp.stack(gr), jnp.stack(bc)
    qh, kh, vh = jnp.stack(qh), jnp.stack(kh), jnp.stack(vh)
    ge = gc[:, last:last + 1, :]
    kh_b = kh.astype(BF16)
    bnt = (((2,), (2,)), ((0,), (0,)))
    kk = lax.dot_general(kh_b, kh_b, bnt, preferred_element_type=F32)
    qk = lax.dot_general(qh.astype(BF16), kh_b, bnt, preferred_element_type=F32)
    kk, qk = jnp.repeat(kk, rep, axis=0), jnp.repeat(qk, rep, axis=0)
    qv, kv = jnp.repeat(qh, rep, axis=0), jnp.repeat(kh, rep, axis=0)

    decay = jnp.where(incl, jnp.exp(jnp.where(incl, gc - gr, 0.0)), 0.0)
    lm = jnp.where(strict, bc * kk * decay, 0.0)
    bmm = lambda a, b: jnp.einsum('hij,hjk->hik', a.astype(BF16), b.astype(BF16), preferred_element_type=F32)
    blk = lambda n: (row // n) == (col // n)
    l0 = jnp.where(blk(16), lm, 0.0)
    p = bmm(l0, l0)
    x = eye - l0
    for _ in range(2):
        xp = bmm(jnp.concatenate([x, p], axis=1), p)
        x = x + xp[:, :cs]
        p = xp[:, cs:]
    x = x + bmm(x, p)
    n = 32
    while n <= cs:
        off = jnp.where(blk(n) & ~blk(n // 2), lm, 0.0)
        x = x - bmm(bmm(x, off), x)
        n *= 2
    eg = jnp.exp(gc)
    uw = bmm(x, jnp.concatenate([vh * bc, kv * (bc * eg)], axis=2))
    u = uw[:, :, :GDN_DV]
    w = uw[:, :, GDN_DV:]
    qg = qv * eg
    intra = jnp.where(incl, qk * decay, 0.0)
    kt = kv * jnp.exp(ge - gc)
    s = s_ref[...]
    wq = bmm(jnp.concatenate([w, qg], axis=1), s)
    v_new = u - wq[:, :cs]
    o = wq[:, cs:] + bmm(intra, v_new)
    for b in range(nb):
        for h in heads:
            o_ref[b, :, h * GDN_DV:(h + 1) * GDN_DV] = o[b * GDN_HV + h]
    s_ref[...] = s * jnp.exp(ge) + jnp.einsum('hck,hcv->hkv', kt.astype(BF16), v_new.astype(BF16),
                                              preferred_element_type=F32)

    @pl.when(c == n_c - 1)
    def _():
        sfin_ref[...] = s_ref[...].reshape(sfin_ref.shape)


def _gdn_scan(q, k, v, gb, s0, reverse):
    b, t, _ = q.shape
    assert t % GDN_CHUNK == 0 and GDN_CHUNK % 32 == 0
    n_c = t // GDN_CHUNK
    cm = (lambda ci: (0, n_c - 1 - ci, 0)) if reverse else (lambda ci: (0, ci, 0))
    d = 1 if reverse else 0
    smap = lambda ci: (0, 0, 0, 0)
    return pl.pallas_call(
        functools.partial(_gdn_chunk_kernel, reverse=reverse, g_lane=d * GDN_HV, b_lane=(2 + d) * GDN_HV),
        grid=(n_c,),
        in_specs=[pl.BlockSpec((b, GDN_CHUNK, GDN_QK_W), cm), pl.BlockSpec((b, GDN_CHUNK, GDN_QK_W), cm),
                  pl.BlockSpec((b, GDN_CHUNK, GDN_V_W), cm), pl.BlockSpec((b, GDN_CHUNK, LANES), cm),
                  pl.BlockSpec((b, GDN_HV, GDN_DK, GDN_DV), smap)],
        out_specs=[pl.BlockSpec((b, GDN_CHUNK, GDN_V_W), cm), pl.BlockSpec((b, GDN_HV, GDN_DK, GDN_DV), smap)],
        out_shape=[jax.ShapeDtypeStruct((b, t, GDN_V_W), F32),
                   jax.ShapeDtypeStruct((b, GDN_HV, GDN_DK, GDN_DV), F32)],
        scratch_shapes=[pltpu.VMEM((b * GDN_HV, GDN_DK, GDN_DV), F32)],
        compiler_params=_CP,
        name="gdn_scan_bwd" if reverse else "gdn_scan_fwd",
    )(q, k, v, gb, s0)


def _odd_out_kernel(of_ref, ob_ref, z_ref, ng_ref, w_ref, x_ref, gate_ref, g_ref, shift_ref, scale_ref,
                    wr_ref, br_ref, xo_ref, h_ref, idx_ref, gates_ref):
    parts = []
    for h in range(GDN_HV):
        cols = slice(h * GDN_DV, (h + 1) * GDN_DV)
        o = of_ref[:, cols] + ob_ref[:, cols]
        z = z_ref[:, cols].astype(F32)
        o = o * lax.rsqrt(jnp.mean(o * o, axis=-1, keepdims=True) + NORM_EPS) * ng_ref[...]
        parts.append((o * (z * jax.nn.sigmoid(z))).astype(BF16))
    y = jnp.dot(jnp.concatenate(parts, axis=1), w_ref[...], preferred_element_type=F32)
    _residual_norm_route(y, x_ref, gate_ref, g_ref, shift_ref, scale_ref, wr_ref, br_ref,
                         xo_ref, h_ref, idx_ref, gates_ref)


def _odd_out_proj(o_f, o_b, p_main, norm_g, w, x2d, gate, g, shift, scale, w_router, b_router, rows_per_vec, tm):
    t, d = x2d.shape
    wr, br = _router_operands(w_router, b_router)
    vec = _vec(d, rows_per_vec, tm)
    out_specs, out_shape = _route_out(t, d, tm)
    return pl.pallas_call(
        _odd_out_kernel,
        grid=(t // tm,),
        in_specs=[_row(tm, GDN_V_W), _row(tm, GDN_V_W),
                  pl.BlockSpec((tm, GDN_V_W), lambda i: (i, GDN_QKV_W // GDN_V_W)),
                  _full((1, GDN_DV)), _full(w.shape), _row(tm, d), vec, _full((1, d)), vec, vec,
                  _full((d, LANES)), _full((1, LANES))],
        out_specs=out_specs, out_shape=out_shape,
        compiler_params=_CP,
        name="odd_out_proj",
    )(o_f, o_b, p_main, norm_g.reshape(1, GDN_DV), w, x2d, gate, g.reshape(1, d), shift, scale, wr, br)


def _split_mod(mv, bsz):
    d = mv.shape[1] // N_MOD
    lat = [mv[:bsz, j * d:(j + 1) * d][:, None, :] for j in range(N_MOD)]
    ctx = [mv[bsz:bsz + 1, j * d:(j + 1) * d][:, None, :] for j in range(N_MOD)]
    return lat, ctx


def _even_layer(x2d, xc2d, mod, mod_c, norm1_g, norm2_g, w_in, w_out, lam_p, subln_g, lam_init, cos, sin,
                w_router, b_router, bsz, n_lat, n_ctx):
    d = x2d.shape[1]
    w_in_b = w_in.astype(BF16)
    w_out_b = w_out.astype(BF16)
    q_scale = DA_DH ** -0.5 * math.log2(math.e)
    f, q, k_all, v_all = _even_in_proj(x2d, norm1_g, mod[0], mod[1], w_in_b, cos, sin, bsz, n_lat, n_lat + n_ctx,
                                       q_scale, TM_PROJ)
    ones, zeros = jnp.ones((n_ctx, ROPE_AXIS_DIM), F32), jnp.zeros((n_ctx, ROPE_AXIS_DIM), F32)
    fc, qc, kc, vc = _even_in_proj(xc2d, norm1_g, mod_c[0], mod_c[1], w_in_b, ones, zeros, bsz, n_ctx, n_ctx,
                                   q_scale, n_ctx)
    lp = lam_p.astype(F32)
    lam = jnp.exp(jnp.sum(lp[0] * lp[1])) - jnp.exp(jnp.sum(lp[2] * lp[3])) + lam_init
    k_all = lax.dynamic_update_slice(k_all, kc, (0, n_lat, 0))
    v_all = lax.dynamic_update_slice(v_all, vc, (0, n_lat, 0))
    o = _diff_attention(lam, q, k_all, v_all, subln_g, 1.0 - lam_init, TQ_ATTN, TK_ATTN)
    oc = _diff_attention(lam, qc, kc, vc, subln_g, 1.0 - lam_init, n_ctx, n_ctx)
    fm = _fourier_mix(f.reshape(bsz, n_lat, FOURIER_W)).reshape(bsz * n_lat, FOURIER_W)
    fmc = _fourier_mix(fc.reshape(bsz, n_ctx, FOURIER_W)).reshape(bsz * n_ctx, FOURIER_W)
    lat = _even_out_proj(fm, o.reshape(bsz * n_lat, DA_W), w_out_b, x2d, mod[2], norm2_g, mod[3], mod[4],
                         w_router, b_router, n_lat, TM_PROJ)
    ctx = _even_out_proj(fmc, oc.reshape(bsz * n_ctx, DA_W), w_out_b, xc2d, mod_c[2], norm2_g, mod_c[3], mod_c[4],
                         w_router, b_router, bsz * n_ctx, n_ctx)
    return lat, ctx


def _odd_layer(x2d, xc2d, mod, mod_c, norm1_g, norm2_g, w_in, conv_w, a_log, dt_bias, norm_g, w_out,
               w_router, b_router, bsz, n_lat, n_ctx):
    d = x2d.shape[1]
    w_main = w_in[:, :GDN_MAIN_W].astype(BF16)
    w_ab = jnp.zeros((d, LANES), BF16).at[:, :4 * GDN_HV].set(w_in[:, GDN_MAIN_W:].astype(BF16))
    p_main, p_ab = _odd_in_proj(x2d, norm1_g, mod[0], mod[1], w_main, w_ab, n_lat, TM_PROJ)
    q, k, v, gb = _gdn_prep(p_main.reshape(bsz, n_lat, GDN_MAIN_W), p_ab.reshape(bsz, n_lat, LANES),
                            conv_w, a_log, dt_bias, TM_GDN)
    pc_main, pc_ab = _odd_in_proj(xc2d, norm1_g, mod_c[0], mod_c[1], w_main, w_ab, bsz * n_ctx, n_ctx)
    qc, kc, vc, gbc = _gdn_prep(pc_main.reshape(bsz, n_ctx, GDN_MAIN_W), pc_ab.reshape(bsz, n_ctx, LANES),
                                conv_w, a_log, dt_bias, n_ctx)
    s0 = jnp.zeros((bsz, GDN_HV, GDN_DK, GDN_DV), F32)
    _, sc_f = _gdn_scan(qc, kc, vc, gbc, s0, False)
    o_f, _ = _gdn_scan(q, k, v, gb, sc_f, False)
    _, sc_b = _gdn_scan(qc, kc, vc, gbc, s0, True)
    o_b, _ = _gdn_scan(q, k, v, gb, sc_b, True)
    return _odd_out_proj(o_f.reshape(bsz * n_lat, GDN_V_W), o_b.reshape(bsz * n_lat, GDN_V_W), p_main, norm_g,
                         w_out.astype(BF16), x2d, mod[2], norm2_g, mod[3], mod[4], w_router, b_router, n_lat, TM_GDN)


def kernel(x, c, ctx, c_ctx, norm1_g, norm2_g, w_mod, b_mod, ev_w_in, ev_w_out, ev_lam, ev_subln_g,
           od_w_in, od_conv_w, od_a_log, od_dt_bias, od_norm_g, od_w_out,
           moe_w_router, moe_b_router, moe_w1, moe_b1, moe_w2, moe_b2, final_g):
    bsz, n_lat, d = x.shape
    n_ctx = ctx.shape[1]
    assert w_mod.shape[0] == 2, "kernel is written for one even (attention) and one odd (DeltaNet) layer"
    t_lat = bsz * n_lat
    cos, sin = _axial_rope_tables(n_lat // GRID_W)
    c_rows = jnp.zeros((8, d), F32).at[:bsz].set(c).at[bsz].set(c_ctx)
    x2d = x.reshape(t_lat, d)
    xc2d = ctx.reshape(bsz * n_ctx, d)

    mod, mod_c = _split_mod(_mod_vectors(c_rows, w_mod, b_mod, 0), bsz)
    (x2d, h2, idx, gates), (xc2d, h2c, idx_c, gates_c) = _even_layer(
        x2d, xc2d, mod, mod_c, norm1_g[0], norm2_g[0], ev_w_in[0], ev_w_out[0], ev_lam[0], ev_subln_g[0],
        _diff_lambda_init(0), cos, sin, moe_w_router[0], moe_b_router[0], bsz, n_lat, n_ctx)
    yb, pos = _moe_ffn(jnp.concatenate([h2, h2c], axis=0), jnp.concatenate([idx, idx_c], axis=0),
                       moe_w1, moe_b1[0], moe_w2, moe_b2[0], 0, TM_PROJ)
    x2d = _moe_combine(yb, pos[:t_lat], gates, x2d, mod[5], n_lat, TM_PROJ)
    xc2d = _moe_combine(yb, pos[t_lat:], gates_c, xc2d, mod_c[5], bsz * n_ctx, n_ctx)

    mod, mod_c = _split_mod(_mod_vectors(c_rows, w_mod, b_mod, 1), bsz)
    x2d, h2, idx, gates = _odd_layer(x2d, xc2d, mod, mod_c, norm1_g[1], norm2_g[1], od_w_in[0], od_conv_w[0],
                                     od_a_log[0], od_dt_bias[0], od_norm_g[0], od_w_out[0],
                                     moe_w_router[1], moe_b_router[1], bsz, n_lat, n_ctx)
    yb, pos = _moe_ffn(h2, idx, moe_w1, moe_b1[1], moe_w2, moe_b2[1], 1, TM_PROJ)
    return _moe_combine(yb, pos, gates, x2d, mod[5], n_lat, TM_PROJ, final_g=final_g).reshape(bsz, n_lat, d)
```

```python
import functools
import math

import jax
import jax.numpy as jnp
from jax import lax
from jax.experimental import pallas as pl
from jax.experimental.pallas import tpu as pltpu

D_MODEL = 1024
N_MOD = 6
NORM_EPS = 1e-6
GRID_W = 64

FOURIER_GROUPS = 4
FOURIER_GD = 64
FOURIER_W = FOURIER_GROUPS * FOURIER_GD
DA_HEADS = 6
DA_DH = 64
DA_VD = 2 * DA_DH
DA_W = DA_HEADS * DA_VD
ROPE_BASE = 10000.0
ROPE_AXIS_DIM = DA_DH // 2
SUBLN_EPS = 1e-5

GDN_HK = 8
GDN_HV = 16
GDN_DK = 128
GDN_DV = 128
GDN_QK_W = GDN_HK * GDN_DK
GDN_V_W = GDN_HV * GDN_DV
GDN_QKV_W = 2 * GDN_QK_W + GDN_V_W
GDN_MAIN_W = GDN_QKV_W + GDN_V_W
GDN_CONV = 5
GDN_CHUNK = 64

N_EXPERTS = 32
TOP_K = 4
D_FF = 1024
SWIGLU_LIMIT = 7.0
SWIGLU_ALPHA = 1.702
MOE_BLOCK = 512
MOE_PARTS = 4

LANES = 128
SUBLANES_BF16 = 16
MXU_DIM = 256
VMEM_LIMIT = 56 * 1024 * 1024
BF16 = jnp.bfloat16
F32 = jnp.float32
HI = lax.Precision.HIGHEST

TM_PROJ = 512
TM_GDN = 256
TQ_ATTN, TK_ATTN = 512, 1280

_CP = pltpu.CompilerParams(vmem_limit_bytes=VMEM_LIMIT)


def _diff_lambda_init(layer_idx):
    return 0.8 - 0.6 * math.exp(-0.3 * layer_idx)


def _axial_rope_tables(rows):
    t = jnp.arange(rows * GRID_W, dtype=jnp.int32)
    row = (t // GRID_W).astype(F32)
    col = (t % GRID_W).astype(F32)
    inv = ROPE_BASE ** (-jnp.arange(0, ROPE_AXIS_DIM, 2, dtype=F32) / ROPE_AXIS_DIM)
    ang = jnp.concatenate([row[:, None] * inv, col[:, None] * inv], axis=-1)
    return jnp.cos(ang), jnp.sin(ang)


def _norm_mod(x, g_ref, shift_ref, scale_ref):
    h = x * lax.rsqrt(jnp.mean(x * x, axis=-1, keepdims=True) + NORM_EPS) * g_ref[...]
    return (h * (1.0 + scale_ref[0]) + shift_ref[0]).astype(BF16)


def _route(logits):
    lane = lax.broadcasted_iota(jnp.int32, logits.shape, 1)
    lane_f = lane.astype(F32)
    neg = jnp.float32(-jnp.inf)
    rest = jnp.where(lane < N_EXPERTS, logits, neg)
    idx = jnp.zeros(logits.shape, F32)
    val = jnp.full(logits.shape, neg, F32)
    for j in range(TOP_K):
        m = jnp.max(rest, axis=-1, keepdims=True)
        sel = jnp.min(jnp.where(rest == m, lane_f, float(LANES)), axis=-1, keepdims=True)
        idx = jnp.where(lane == j, sel, idx)
        val = jnp.where(lane == j, m, val)
        rest = jnp.where(lane_f == sel, neg, rest)
    e = jnp.exp(val - val[:, 0:1])
    return idx, e / jnp.sum(e, axis=-1, keepdims=True)


def _residual_norm_route(y, x_ref, gate_ref, g_ref, shift_ref, scale_ref, wr_ref, br_ref,
                         xo_ref, h_ref, idx_ref, gates_ref):
    xn = x_ref[...] + gate_ref[0] * y
    xo_ref[...] = xn
    h = _norm_mod(xn, g_ref, shift_ref, scale_ref)
    h_ref[...] = h
    idx, gates = _route(jnp.dot(h, wr_ref[...], preferred_element_type=F32) + br_ref[...])
    idx_ref[...] = idx.astype(jnp.int32)
    gates_ref[...] = gates


def _row(tm, n):
    return pl.BlockSpec((tm, n), lambda i: (i, 0))


def _full(shape):
    return pl.BlockSpec(shape, lambda i: (0,) * len(shape))


def _vec(d, rows_per_vec, tm):
    per = rows_per_vec // tm
    return pl.BlockSpec((1, 1, d), lambda i: (i // per, 0, 0))


def _router_operands(w_router, b_router):
    d = w_router.shape[0]
    wr = jnp.zeros((d, LANES), BF16).at[:, :N_EXPERTS].set(w_router.astype(BF16))
    br = jnp.zeros((1, LANES), F32).at[0, :N_EXPERTS].set(b_router)
    return wr, br


def _route_out(t, d, tm):
    specs = [_row(tm, d), _row(tm, d), _row(tm, LANES), _row(tm, LANES)]
    shapes = [jax.ShapeDtypeStruct((t, d), F32), jax.ShapeDtypeStruct((t, d), BF16),
              jax.ShapeDtypeStruct((t, LANES), jnp.int32), jax.ShapeDtypeStruct((t, LANES), F32)]
    return specs, shapes


def _mod_kernel(c_ref, w_ref, b_ref, o_ref):
    c = c_ref[...]
    s = (c * jax.nn.sigmoid(c)).astype(BF16)
    o_ref[...] = jnp.dot(s, w_ref[0].astype(BF16), preferred_element_type=F32) + b_ref[0]


def _mod_vectors(c_rows, w_mod, b_mod, layer):
    d = c_rows.shape[1]
    return pl.pallas_call(
        _mod_kernel,
        grid=(N_MOD,),
        in_specs=[pl.BlockSpec(c_rows.shape, lambda n: (0, 0)),
                  pl.BlockSpec((1, d, d), lambda n: (layer, 0, n)),
                  pl.BlockSpec((1, 1, d), lambda n: (layer, 0, n))],
        out_specs=pl.BlockSpec((c_rows.shape[0], d), lambda n: (0, n)),
        out_shape=jax.ShapeDtypeStruct((c_rows.shape[0], N_MOD * d), F32),
        compiler_params=_CP,
        name="mod_vectors",
    )(c_rows, w_mod, b_mod.reshape(b_mod.shape[0], 1, -1))


def _even_in_kernel(x_ref, g_ref, shift_ref, scale_ref, w_ref, rc_ref, rs_ref, f_ref, q_ref, k_ref, v_ref, *, q_scale):
    h = _norm_mod(x_ref[...], g_ref, shift_ref, scale_ref)
    p = jnp.dot(h, w_ref[...], preferred_element_type=F32)
    f_ref[...] = p[:, :FOURIER_W]
    rc, rs = rc_ref[...], rs_ref[...]
    lane = lax.broadcasted_iota(jnp.int32, rc.shape, 1)
    first = (lane % DA_DH) < ROPE_AXIS_DIM

    def rope(t):
        partner = jnp.where(first, pltpu.roll(t, LANES - ROPE_AXIS_DIM, 1), pltpu.roll(t, ROPE_AXIS_DIM, 1))
        return t * rc + partner * rs

    for hd in range(DA_HEADS):
        cq = slice(FOURIER_W + hd * DA_VD, FOURIER_W + (hd + 1) * DA_VD)
        ck = slice(FOURIER_W + DA_W + hd * DA_VD, FOURIER_W + DA_W + (hd + 1) * DA_VD)
        q_ref[0, :, hd * DA_VD:(hd + 1) * DA_VD] = (rope(p[:, cq]) * q_scale).astype(BF16)
        k_ref[0, :, hd * DA_VD:(hd + 1) * DA_VD] = rope(p[:, ck]).astype(BF16)
    v_ref[0] = p[:, FOURIER_W + 2 * DA_W:].astype(BF16)


def _even_in_proj(x2d, g, shift, scale, w, cos, sin, bsz, n_seq, n_keys, q_scale, tm):
    t, d = x2d.shape
    bpb = n_seq // tm
    rc = jnp.tile(cos, (1, LANES // ROPE_AXIS_DIM))
    rs = jnp.tile(jnp.concatenate([-sin, sin], axis=1), (1, LANES // DA_DH))
    tab = pl.BlockSpec((tm, LANES), lambda i: (i % bpb, 0))
    seq = pl.BlockSpec((1, tm, DA_W), lambda i: (i // bpb, i % bpb, 0))
    vec = _vec(d, n_seq * (bsz // shift.shape[0]), tm)
    return pl.pallas_call(
        functools.partial(_even_in_kernel, q_scale=q_scale),
        grid=(t // tm,),
        in_specs=[_row(tm, d), _full((1, d)), vec, vec, _full(w.shape), tab, tab],
        out_specs=[_row(tm, FOURIER_W), seq, seq, seq],
        out_shape=[jax.ShapeDtypeStruct((t, FOURIER_W), F32), jax.ShapeDtypeStruct((bsz, n_seq, DA_W), BF16),
                   jax.ShapeDtypeStruct((bsz, n_keys, DA_W), BF16), jax.ShapeDtypeStruct((bsz, n_keys, DA_W), BF16)],
        compiler_params=_CP,
        name="even_in_proj",
    )(x2d, g.reshape(1, d), shift, scale, w, rc, rs)


def _diff_attn_kernel(lam_ref, q_ref, k_ref, v_ref, g_ref, o_ref, qs_ref, s_ref, m_ref, acc_ref, *,
                      tk, n_sub, out_scale):
    tq = q_ref.shape[1]
    n_kv = k_ref.shape[1] // tk
    rb = 2 * tq // n_sub
    q = q_ref[0]
    lane = lax.broadcasted_iota(jnp.int32, q.shape, 1)
    zero = jnp.zeros_like(q)
    qs_ref[:tq] = jnp.where(lane < DA_DH, q, zero)
    qs_ref[tq:] = jnp.where(lane >= DA_DH, q, zero)
    m_ref[...] = jnp.full(m_ref.shape, -1e30, F32)
    acc_ref[...] = jnp.zeros(acc_ref.shape, F32)
    ones = jnp.ones((tk, LANES), BF16)

    def scores(i, slot):
        off = pl.multiple_of(i * tk, tk)
        k = k_ref[0, pl.ds(off, tk), :]
        s_ref[slot] = lax.dot_general(qs_ref[...], k, (((1,), (1,)), ((), ())), preferred_element_type=F32)

    def consume(i, slot):
        off = pl.multiple_of(i * tk, tk)
        v_ext = jnp.concatenate([v_ref[0, pl.ds(off, tk), :], ones], axis=1)
        for r in range(n_sub):
            rows = pl.ds(r * rb, rb)
            s = s_ref[slot, rows, :]
            m_prev = m_ref[rows, :]
            m_new = jnp.maximum(m_prev, jnp.max(s, axis=1, keepdims=True))
            alpha = jnp.exp2(m_prev - m_new)
            p = jnp.exp2(s - jnp.tile(m_new, (1, tk // LANES)))
            pv = jnp.dot(p.astype(BF16), v_ext, preferred_element_type=F32)
            acc_ref[rows, :] = acc_ref[rows, :] * jnp.tile(alpha, (1, 2)) + pv
            m_ref[rows, :] = m_new

    scores(0, 0)

    def body(j, carry):
        scores(2 * j + 1, 1)
        consume(2 * j, 0)
        scores(2 * j + 2, 0)
        consume(2 * j + 1, 1)
        return carry

    lax.fori_loop(0, (n_kv - 1) // 2, body, 0)
    consume(n_kv - 1, 0)
    acc = acc_ref[...]
    o1 = acc[:tq, :LANES] / acc[:tq, LANES:]
    o2 = acc[tq:, :LANES] / acc[tq:, LANES:]
    o = o1 - lam_ref[0] * o2
    ms = jnp.mean(o * o, axis=-1, keepdims=True)
    o = o * lax.rsqrt(ms + SUBLN_EPS) * g_ref[...] * out_scale
    o_ref[0] = o.astype(o_ref.dtype)


def _diff_attention(lam, q, k_all, v_all, subln_g, out_scale, tq, tk, n_sub=2):
    b, n, _ = q.shape
    nk = k_all.shape[1]
    assert n % tq == 0 and nk % tk == 0 and tk % MXU_DIM == 0 and (nk // tk) % 2 == 1
    grid_spec = pltpu.PrefetchScalarGridSpec(
        num_scalar_prefetch=1,
        grid=(b, DA_HEADS, n // tq),
        in_specs=[pl.BlockSpec((1, tq, DA_VD), lambda bi, hi, qi, lam_r: (bi, qi, hi)),
                  pl.BlockSpec((1, nk, DA_VD), lambda bi, hi, qi, lam_r: (bi, 0, hi)),
                  pl.BlockSpec((1, nk, DA_VD), lambda bi, hi, qi, lam_r: (bi, 0, hi)),
                  pl.BlockSpec((1, DA_VD), lambda bi, hi, qi, lam_r: (0, 0))],
        out_specs=pl.BlockSpec((1, tq, DA_VD), lambda bi, hi, qi, lam_r: (bi, qi, hi)),
        scratch_shapes=[pltpu.VMEM((2 * tq, LANES), BF16), pltpu.VMEM((2, 2 * tq, tk), F32),
                        pltpu.VMEM((2 * tq, LANES), F32), pltpu.VMEM((2 * tq, 2 * LANES), F32)],
    )
    return pl.pallas_call(
        functools.partial(_diff_attn_kernel, tk=tk, n_sub=n_sub, out_scale=out_scale),
        grid_spec=grid_spec,
        out_shape=jax.ShapeDtypeStruct((b, n, DA_W), BF16),
        compiler_params=_CP,
        name="diff_attention",
    )(lam.reshape(1), q, k_all, v_all, subln_g.reshape(1, DA_VD))


def _fft_stage1_kernel(x_ref, fch_ref, f1_ref, tc_ref, ts_ref, yr_ref, yi_ref, *, nb):
    w = FOURIER_W
    r = x_ref.shape[1]
    for bl in range(nb):
        cols = slice(bl * w, (bl + 1) * w)
        ab = jnp.dot(x_ref[0, :, cols], fch_ref[...], preferred_element_type=F32, precision=HI)
        z = jnp.concatenate([ab[:, :w], ab[:, w:]], axis=0)
        y = jnp.dot(f1_ref[...], z, preferred_element_type=F32, precision=HI)
        yr, yi = y[:r], y[r:]
        tc, ts = tc_ref[:, cols], ts_ref[:, cols]
        yr_ref[0, :, cols] = yr * tc + yi * ts
        yi_ref[0, :, cols] = yi * tc - yr * ts


def _fft_stage2_kernel(yr_ref, yi_ref, f2_ref, o_ref, *, nc):
    w = FOURIER_W
    for cl in range(nc):
        y = jnp.concatenate([yr_ref[0, cl], yi_ref[0, cl]], axis=0)
        o_ref[0, :, cl * w:(cl + 1) * w] = jnp.dot(f2_ref[...], y, preferred_element_type=F32, precision=HI)


def _fourier_mix(f, nb=8):
    bsz, n, w = f.shape
    r = math.isqrt(n)
    assert r * r == n and r % nb == 0 and w == FOURIER_W
    two_pi = 2.0 * math.pi
    k = jnp.arange(r, dtype=jnp.int32)
    ang = two_pi * ((k[:, None] * k[None, :]) % r).astype(F32) / r
    c1, s1 = jnp.cos(ang), jnp.sin(ang)
    kc = jnp.arange(FOURIER_GD, dtype=jnp.int32)
    angc = two_pi * ((kc[:, None] * kc[None, :]) % FOURIER_GD).astype(F32) / FOURIER_GD
    eye_g = jnp.eye(FOURIER_GROUPS, dtype=F32)
    fch = jnp.concatenate([jnp.kron(eye_g, jnp.cos(angc)), jnp.kron(eye_g, jnp.sin(angc))], axis=1)
    f1 = jnp.concatenate([jnp.concatenate([c1, -s1], axis=1), jnp.concatenate([-s1, -c1], axis=1)], axis=0)
    f2 = jnp.concatenate([c1, s1], axis=1) * (1.0 / math.sqrt(n * FOURIER_GD))
    angt = two_pi * ((k[:, None] * k[None, :]) % n).astype(F32) / n
    tc = jnp.repeat(jnp.cos(angt), w, axis=1)
    ts = jnp.repeat(jnp.sin(angt), w, axis=1)
    xv = f.reshape(bsz, r, r * w)
    blk = pl.BlockSpec((1, r, nb * w), lambda bi, j: (bi, 0, j))
    tab = pl.BlockSpec((r, nb * w), lambda bi, j: (0, j))
    full = lambda a: pl.BlockSpec(a.shape, lambda bi, j: (0, 0))
    yshape = jax.ShapeDtypeStruct((bsz, r, r * w), F32)
    yr, yi = pl.pallas_call(
        functools.partial(_fft_stage1_kernel, nb=nb),
        grid=(bsz, r // nb),
        in_specs=[blk, full(fch), full(f1), tab, tab],
        out_specs=[blk, blk],
        out_shape=[yshape, yshape],
        compiler_params=_CP,
        name="fft_stage1",
    )(xv, fch, f1, tc, ts)
    y4 = lambda a: a.reshape(bsz, r, r, w)
    yblk = pl.BlockSpec((1, nb, r, w), lambda bi, j: (bi, j, 0, 0))
    out = pl.pallas_call(
        functools.partial(_fft_stage2_kernel, nc=nb),
        grid=(bsz, r // nb),
        in_specs=[yblk, yblk, full(f2)],
        out_specs=blk,
        out_shape=yshape,
        compiler_params=_CP,
        name="fft_stage2",
    )(y4(yr), y4(yi), f2)
    return out.reshape(bsz, n, w)


def _even_out_kernel(fm_ref, o_ref, w_ref, x_ref, gate_ref, g_ref, shift_ref, scale_ref, wr_ref, br_ref,
                     xo_ref, h_ref, idx_ref, gates_ref):
    y = (jnp.dot(fm_ref[...].astype(BF16), w_ref[:FOURIER_W], preferred_element_type=F32)
         + jnp.dot(o_ref[...], w_ref[FOURIER_W:], preferred_element_type=F32))
    _residual_norm_route(y, x_ref, gate_ref, g_ref, shift_ref, scale_ref, wr_ref, br_ref,
                         xo_ref, h_ref, idx_ref, gates_ref)


def _even_out_proj(fm, o, w, x2d, gate, g, shift, scale, w_router, b_router, rows_per_vec, tm):
    t, d = x2d.shape
    wr, br = _router_operands(w_router, b_router)
    vec = _vec(d, rows_per_vec, tm)
    out_specs, out_shape = _route_out(t, d, tm)
    return pl.pallas_call(
        _even_out_kernel,
        grid=(t // tm,),
        in_specs=[_row(tm, FOURIER_W), _row(tm, DA_W), _full(w.shape), _row(tm, d), vec, _full((1, d)), vec, vec,
                  _full((d, LANES)), _full((1, LANES))],
        out_specs=out_specs, out_shape=out_shape,
        compiler_params=_CP,
        name="even_out_proj",
    )(fm, o, w, x2d, gate, g.reshape(1, d), shift, scale, wr, br)


def _moe_kernel(be_ref, x_ref, w1_ref, perm_ref, b1g_ref, b1l_ref, w2_ref, b2_ref, *rest):
    o_ref, w1g_s, w1l_s, w2_s = rest[-4:]
    i = pl.program_id(0)

    @pl.when((i == 0) | (be_ref[i] != be_ref[jnp.maximum(i - 1, 0)]))
    def _():
        for c in range(w1_ref.shape[3] // MXU_DIM):
            blk = jnp.dot(w1_ref[0, 0, :, c * MXU_DIM:(c + 1) * MXU_DIM].astype(BF16), perm_ref[...],
                          preferred_element_type=F32)
            w1g_s[:, c * LANES:(c + 1) * LANES] = blk[:, :LANES].astype(BF16)
            w1l_s[:, c * LANES:(c + 1) * LANES] = blk[:, LANES:].astype(BF16)
        w2_s[...] = w2_ref[0, 0].astype(BF16)

    x = x_ref[...]
    ug = jnp.dot(x, w1g_s[...], preferred_element_type=F32) + b1g_ref[0]
    ul = jnp.dot(x, w1l_s[...], preferred_element_type=F32) + b1l_ref[0]
    glu = jnp.minimum(ug, SWIGLU_LIMIT)
    lin = jnp.clip(ul, -SWIGLU_LIMIT, SWIGLU_LIMIT)
    act = glu * jax.nn.sigmoid(SWIGLU_ALPHA * glu) * (lin + 1.0)
    y = jnp.dot(act.astype(BF16), w2_s[...], preferred_element_type=F32) + b2_ref[0]
    o_ref[...] = y.astype(o_ref.dtype)


def _moe_experts(block_expert, xb, w1_all, b1g, b1l, w2_all, layer, b2, yb_prev, first_block, n_blocks_total):
    n_rows, d = xb.shape
    n_blocks = n_rows // MOE_BLOCK
    src = jnp.arange(MXU_DIM)[:, None]
    dst = jnp.arange(MXU_DIM)[None, :]
    perm = (src == jnp.where(dst < LANES, 2 * dst, 2 * (dst - LANES) + 1)).astype(BF16)
    bspec = lambda s: pl.BlockSpec((1,) + s, lambda i, be: (be[i], 0, 0))
    wspec = lambda s: pl.BlockSpec((1, 1) + s, lambda i, be: (layer, be[i], 0, 0))
    in_specs = [pl.BlockSpec((MOE_BLOCK, d), lambda i, be: (i, 0)),
                wspec((d, 2 * D_FF)), pl.BlockSpec((MXU_DIM, MXU_DIM), lambda i, be: (0, 0)),
                bspec((1, D_FF)), bspec((1, D_FF)), wspec((D_FF, d)), bspec((1, d))]
    args = [block_expert, xb, w1_all, perm, b1g, b1l, w2_all, b2]
    aliases = {}
    if yb_prev is not None:
        in_specs.append(pl.BlockSpec(memory_space=pl.ANY))
        aliases = {len(args): 0}
        args.append(yb_prev)
    grid_spec = pltpu.PrefetchScalarGridSpec(
        num_scalar_prefetch=1,
        grid=(n_blocks,),
        in_specs=in_specs,
        out_specs=pl.BlockSpec((MOE_BLOCK, d), lambda i, be: (first_block + i, 0)),
        scratch_shapes=[pltpu.VMEM((d, D_FF), BF16), pltpu.VMEM((d, D_FF), BF16), pltpu.VMEM((D_FF, d), BF16)],
    )
    return pl.pallas_call(
        _moe_kernel,
        grid_spec=grid_spec,
        out_shape=jax.ShapeDtypeStruct((n_blocks_total * MOE_BLOCK, d), BF16),
        input_output_aliases=aliases,
        compiler_params=_CP,
        name="moe_experts",
    )(*args)


def _rank_kernel(idx_ref, rank_ref, cnt_ref, carry_ref):
    i = pl.program_id(0)

    @pl.when(i == 0)
    def _():
        carry_ref[...] = jnp.zeros(carry_ref.shape, F32)

    idx = idx_ref[...]
    tm = idx.shape[0]
    lane = lax.broadcasted_iota(jnp.int32, idx.shape, 1)
    sel = [lane == idx[:, j:j + 1] for j in range(TOP_K)]
    onehot = sel[0].astype(F32)
    for j in range(1, TOP_K):
        onehot = onehot + sel[j].astype(F32)
    row = lax.broadcasted_iota(jnp.int32, (tm, tm), 0)
    col = lax.broadcasted_iota(jnp.int32, (tm, tm), 1)
    before = (row > col).astype(BF16)
    prefix = jnp.dot(before, onehot.astype(BF16), preferred_element_type=F32) + carry_ref[0:1, :]
    rank = jnp.zeros(idx.shape, F32)
    for j in range(TOP_K):
        rank = jnp.where(lane == j, jnp.sum(jnp.where(sel[j], prefix, 0.0), axis=-1, keepdims=True), rank)
    rank_ref[...] = rank.astype(jnp.int32)
    carry_ref[0:1, :] = carry_ref[0:1, :] + jnp.sum(onehot, axis=0, keepdims=True)
    cnt_ref[...] = carry_ref[...]


def _expert_ranks(idx, tm):
    t = idx.shape[0]
    rank, cnt = pl.pallas_call(
        _rank_kernel,
        grid=(t // tm,),
        in_specs=[_row(tm, LANES)],
        out_specs=[_row(tm, LANES), _full((8, LANES))],
        out_shape=[jax.ShapeDtypeStruct((t, LANES), jnp.int32), jax.ShapeDtypeStruct((8, LANES), F32)],
        scratch_shapes=[pltpu.VMEM((8, LANES), F32)],
        compiler_params=_CP,
        name="expert_ranks",
    )(idx)
    return rank[:, :TOP_K], cnt[0, :N_EXPERTS].astype(jnp.int32)


def _moe_ffn(h, idx, w1_all, b1, w2_all, b2, layer, tm):
    n_tok, d = h.shape
    n_assign = n_tok * TOP_K
    e_flat = idx[:, :TOP_K].reshape(n_assign)
    order = jnp.argsort(e_flat).astype(jnp.int32)
    rank, counts = _expert_ranks(idx, tm)
    starts = jnp.cumsum(counts) - counts
    padded = (counts + MOE_BLOCK - 1) // MOE_BLOCK * MOE_BLOCK
    pad_ends = jnp.cumsum(padded)
    pad_starts = pad_ends - padded
    n_blocks = -(-(n_assign + N_EXPERTS * (MOE_BLOCK - 1)) // MOE_BLOCK)
    n_rows = n_blocks * MOE_BLOCK
    block_start = jnp.arange(n_blocks, dtype=jnp.int32) * MOE_BLOCK
    block_expert = jnp.minimum(jnp.sum(pad_ends[None, :] <= block_start[:, None], axis=1, dtype=jnp.int32),
                               N_EXPERTS - 1)
    shift = pad_starts - starts
    pos = pad_starts[e_flat].reshape(n_tok, TOP_K) + rank
    r = jnp.arange(n_rows, dtype=jnp.int32)
    src = jnp.clip(r - jnp.repeat(shift[block_expert], MOE_BLOCK), 0, n_assign - 1)
    row_tok = order[src] // TOP_K
    assert n_blocks % MOE_PARTS == 0
    nbp = n_blocks // MOE_PARTS
    yb = None
    for part in range(MOE_PARTS):
        blocks = slice(part * nbp, (part + 1) * nbp)
        xb = h[row_tok[part * nbp * MOE_BLOCK:(part + 1) * nbp * MOE_BLOCK]]
        yb = _moe_experts(block_expert[blocks], xb, w1_all, b1[:, None, 0::2], b1[:, None, 1::2], w2_all, layer,
                          b2[:, None, :], yb, part * nbp, n_blocks)
    return yb, pos


def _combine_kernel(*refs, final):
    y_refs = refs[:TOP_K]
    gt_ref, x_ref, g5_ref = refs[TOP_K:TOP_K + 3]
    o_ref = refs[-1]
    gt = gt_ref[...]
    acc = y_refs[0][...].astype(F32) * gt[:, 0:1]
    for j in range(1, TOP_K):
        acc = acc + y_refs[j][...].astype(F32) * gt[:, j:j + 1]
    xn = x_ref[...] + g5_ref[0] * acc
    if final:
        fg_ref = refs[TOP_K + 3]
        xn = xn * lax.rsqrt(jnp.mean(xn * xn, axis=-1, keepdims=True) + NORM_EPS) * fg_ref[...]
    o_ref[...] = xn


def _moe_combine(yb, pos, gates, x2d, gate5, rows_per_vec, tm, final_g=None):
    t, d = x2d.shape
    ys = [yb[pos[:, j]] for j in range(TOP_K)]
    in_specs = [_row(tm, d)] * TOP_K + [_row(tm, LANES), _row(tm, d), _vec(d, rows_per_vec, tm)]
    args = ys + [gates, x2d, gate5]
    if final_g is not None:
        in_specs.append(_full((1, d)))
        args.append(final_g.reshape(1, d))
    return pl.pallas_call(
        functools.partial(_combine_kernel, final=final_g is not None),
        grid=(t // tm,),
        in_specs=in_specs,
        out_specs=_row(tm, d),
        out_shape=jax.ShapeDtypeStruct((t, d), F32),
        compiler_params=_CP,
        name="moe_combine",
    )(*args)


HALO = SUBLANES_BF16


def _odd_in_kernel(xm_ref, xp_ref, xn_ref, g_ref, shift_ref, scale_ref, wqkv_ref, wz_ref, wab_ref, cw_ref,
                   alog_ref, dtb_ref, q_ref, k_ref, v_ref, z_ref, gb_ref):
    i = pl.program_id(1)
    n_i = pl.num_programs(1)
    tm = xm_ref.shape[1]
    half = GDN_CONV // 2
    keep_prev = (i > 0).astype(BF16)
    keep_next = (i < n_i - 1).astype(BF16)
    h_main = _norm_mod(xm_ref[0], g_ref, shift_ref, scale_ref)
    h_ext = jnp.concatenate([_norm_mod(xp_ref[0], g_ref, shift_ref, scale_ref) * keep_prev, h_main,
                             _norm_mod(xn_ref[0], g_ref, shift_ref, scale_ref) * keep_next], axis=0)
    gw = MXU_DIM
    for cg in range(GDN_QKV_W // gw):
        cols = slice(cg * gw, (cg + 1) * gw)
        ext = jnp.dot(h_ext, wqkv_ref[:, cols], preferred_element_type=F32)
        acc = ext[HALO:HALO + tm] * cw_ref[half:half + 1, cols]
        for j in range(GDN_CONV):
            if j != half:
                sh = pltpu.roll(ext, (half - j) % (tm + 2 * HALO), 0)[HALO:HALO + tm]
                acc = acc + sh * cw_ref[j:j + 1, cols]
        y = acc * jax.nn.sigmoid(acc)
        for sub in range(gw // LANES):
            hd = cg * (gw // LANES) + sub
            yh = y[:, sub * LANES:(sub + 1) * LANES]
            if hd < 2 * GDN_HK:
                yh = yh * lax.rsqrt(jnp.sum(yh * yh, axis=-1, keepdims=True) + 1e-6)
                if hd < GDN_HK:
                    q_ref[0, :, hd * LANES:(hd + 1) * LANES] = yh * (GDN_DK ** -0.5)
                else:
                    k_ref[0, :, (hd - GDN_HK) * LANES:(hd - GDN_HK + 1) * LANES] = yh
            else:
                v_ref[0, :, (hd - 2 * GDN_HK) * LANES:(hd - 2 * GDN_HK + 1) * LANES] = yh
    z_ref[0] = jnp.dot(h_main, wz_ref[...], preferred_element_type=F32).astype(BF16)
    ab = jnp.dot(h_main, wab_ref[...], preferred_element_type=F32)
    xa = ab + dtb_ref[...]
    softplus = jnp.maximum(xa, 0.0) + jnp.log1p(jnp.exp(-jnp.abs(xa)))
    lane = lax.broadcasted_iota(jnp.int32, ab.shape, 1)
    gb_ref[0] = jnp.where(lane < 2 * GDN_HV, -jnp.exp(alog_ref[...]) * softplus, jax.nn.sigmoid(ab))


def _odd_in_stage(x3d, g, shift, scale, w_in, conv_w, a_log, dt_bias, tm):
    b, t, d = x3d.shape
    w_qkv = w_in[:, :GDN_QKV_W].astype(BF16)
    w_z = w_in[:, GDN_QKV_W:GDN_MAIN_W].astype(BF16)
    w_ab = jnp.zeros((d, LANES), BF16).at[:, :4 * GDN_HV].set(w_in[:, GDN_MAIN_W:].astype(BF16))
    nb = tm // HALO
    last = t // HALO - 1
    pad = lambda a: jnp.zeros((1, LANES), F32).at[0, :2 * GDN_HV].set(a.reshape(-1))
    per_vec = b // shift.shape[0]
    vec = pl.BlockSpec((1, 1, d), lambda bi, i: (bi // per_vec, 0, 0))
    seq = lambda n: pl.BlockSpec((1, tm, n), lambda bi, i: (bi, i, 0))
    full = lambda s: pl.BlockSpec(s, lambda bi, i: (0,) * len(s))
    f = lambda n, dt=F32: jax.ShapeDtypeStruct((b, t, n), dt)
    return pl.pallas_call(
        _odd_in_kernel,
        grid=(b, t // tm),
        in_specs=[seq(d),
                  pl.BlockSpec((1, HALO, d), lambda bi, i: (bi, jnp.maximum(i * nb - 1, 0), 0)),
                  pl.BlockSpec((1, HALO, d), lambda bi, i: (bi, jnp.minimum((i + 1) * nb, last), 0)),
                  full((1, d)), vec, vec, full(w_qkv.shape), full(w_z.shape), full(w_ab.shape),
                  full((GDN_CONV, GDN_QKV_W)), full((1, LANES)), full((1, LANES))],
        out_specs=[seq(GDN_QK_W), seq(GDN_QK_W), seq(GDN_V_W), seq(GDN_V_W), seq(LANES)],
        out_shape=[f(GDN_QK_W), f(GDN_QK_W), f(GDN_V_W), f(GDN_V_W, BF16), f(LANES)],
        compiler_params=_CP,
        name="odd_in_stage",
    )(x3d, x3d, x3d, g.reshape(1, d), shift, scale, w_qkv, w_z, w_ab, conv_w, pad(a_log), pad(dt_bias))


def _gdn_chunk_kernel(q_ref, k_ref, v_ref, gb_ref, s0_ref, o_ref, sfin_ref, s_ref, *, reverse, g_lane, b_lane):
    c = pl.program_id(0)
    n_c = pl.num_programs(0)
    cs = GDN_CHUNK
    rep = GDN_HV // GDN_HK
    nb = q_ref.shape[0]

    @pl.when(c == 0)
    def _():
        s_ref[...] = s0_ref[...].reshape(s_ref.shape)

    row = lax.broadcasted_iota(jnp.int32, (cs, cs), 0)
    col = lax.broadcasted_iota(jnp.int32, (cs, cs), 1)
    incl = (row <= col) if reverse else (row >= col)
    strict = (row < col) if reverse else (row > col)
    eye = (row == col).astype(F32)
    incl_b = incl.astype(BF16)
    last = 0 if reverse else cs - 1
    tn = (((0,), (1,)), ((), ()))

    heads = range(GDN_HV)
    gc, gr, bc, qh, kh, vh = [], [], [], [], [], []
    for b in range(nb):
        gb = gb_ref[b]
        gb_hi = gb.astype(BF16)
        gb_lo = (gb - gb_hi.astype(F32)).astype(BF16)
        g_col = (jnp.dot(incl_b, gb_hi, preferred_element_type=F32)
                 + jnp.dot(incl_b, gb_lo, preferred_element_type=F32))
        g_row = (lax.dot_general(gb_hi, incl_b, tn, preferred_element_type=F32)
                 + lax.dot_general(gb_lo, incl_b, tn, preferred_element_type=F32))
        gc += [g_col[:, g_lane + h:g_lane + h + 1] for h in heads]
        gr += [g_row[g_lane + h:g_lane + h + 1, :] for h in heads]
        bc += [gb[:, b_lane + h:b_lane + h + 1] for h in heads]
        qh += [q_ref[b, :, h * GDN_DK:(h + 1) * GDN_DK] for h in range(GDN_HK)]
        kh += [k_ref[b, :, h * GDN_DK:(h + 1) * GDN_DK] for h in range(GDN_HK)]
        vh += [v_ref[b, :, h * GDN_DV:(h + 1) * GDN_DV] for h in heads]
    gc, gr, bc = jnp.stack(gc), jnp.stack(gr), jnp.stack(bc)
    qh, kh, vh = jnp.stack(qh), jnp.stack(kh), jnp.stack(vh)
    ge = gc[:, last:last + 1, :]
    kh_b = kh.astype(BF16)
    bnt = (((2,), (2,)), ((0,), (0,)))
    kk = lax.dot_general(kh_b, kh_b, bnt, preferred_element_type=F32)
    qk = lax.dot_general(qh.astype(BF16), kh_b, bnt, preferred_element_type=F32)
    kk, qk = jnp.repeat(kk, rep, axis=0), jnp.repeat(qk, rep, axis=0)
    qv, kv = jnp.repeat(qh, rep, axis=0), jnp.repeat(kh, rep, axis=0)

    decay = jnp.where(incl, jnp.exp(jnp.where(incl, gc - gr, 0.0)), 0.0)
    lm = jnp.where(strict, bc * kk * decay, 0.0)
    bmm = lambda a, b: jnp.einsum('hij,hjk->hik', a.astype(BF16), b.astype(BF16), preferred_element_type=F32)
    blk = lambda n: (row // n) == (col // n)
    l0 = jnp.where(blk(16), lm, 0.0)
    p = bmm(l0, l0)
    x = eye - l0
    for _ in range(2):
        xp = bmm(jnp.concatenate([x, p], axis=1), p)
        x = x + xp[:, :cs]
        p = xp[:, cs:]
    x = x + bmm(x, p)
    n = 32
    while n <= cs:
        off = jnp.where(blk(n) & ~blk(n // 2), lm, 0.0)
        x = x - bmm(bmm(x, off), x)
        n *= 2
    eg = jnp.exp(gc)
    uw = bmm(x, jnp.concatenate([vh * bc, kv * (bc * eg)], axis=2))
    u = uw[:, :, :GDN_DV]
    w = uw[:, :, GDN_DV:]
    qg = qv * eg
    intra = jnp.where(incl, qk * decay, 0.0)
    kt = kv * jnp.exp(ge - gc)
    s = s_ref[...]
    wq = bmm(jnp.concatenate([w, qg], axis=1), s)
    v_new = u - wq[:, :cs]
    o = wq[:, cs:] + bmm(intra, v_new)
    for b in range(nb):
        for h in heads:
            o_ref[b, :, h * GDN_DV:(h + 1) * GDN_DV] = o[b * GDN_HV + h]
    s_ref[...] = s * jnp.exp(ge) + jnp.einsum('hck,hcv->hkv', kt.astype(BF16), v_new.astype(BF16),
                                              preferred_element_type=F32)

    @pl.when(c == n_c - 1)
    def _():
        sfin_ref[...] = s_ref[...].reshape(sfin_ref.shape)


def _gdn_scan(q, k, v, gb, s0, reverse):
    b, t, _ = q.shape
    assert t % GDN_CHUNK == 0 and GDN_CHUNK % 32 == 0
    n_c = t // GDN_CHUNK
    cm = (lambda ci: (0, n_c - 1 - ci, 0)) if reverse else (lambda ci: (0, ci, 0))
    d = 1 if reverse else 0
    smap = lambda ci: (0, 0, 0, 0)
    return pl.pallas_call(
        functools.partial(_gdn_chunk_kernel, reverse=reverse, g_lane=d * GDN_HV, b_lane=(2 + d) * GDN_HV),
        grid=(n_c,),
        in_specs=[pl.BlockSpec((b, GDN_CHUNK, GDN_QK_W), cm), pl.BlockSpec((b, GDN_CHUNK, GDN_QK_W), cm),
                  pl.BlockSpec((b, GDN_CHUNK, GDN_V_W), cm), pl.BlockSpec((b, GDN_CHUNK, LANES), cm),
                  pl.BlockSpec((b, GDN_HV, GDN_DK, GDN_DV), smap)],
        out_specs=[pl.BlockSpec((b, GDN_CHUNK, GDN_V_W), cm), pl.BlockSpec((b, GDN_HV, GDN_DK, GDN_DV), smap)],
        out_shape=[jax.ShapeDtypeStruct((b, t, GDN_V_W), F32),
                   jax.ShapeDtypeStruct((b, GDN_HV, GDN_DK, GDN_DV), F32)],
        scratch_shapes=[pltpu.VMEM((b * GDN_HV, GDN_DK, GDN_DV), F32)],
        compiler_params=_CP,
        name="gdn_scan_bwd" if reverse else "gdn_scan_fwd",
    )(q, k, v, gb, s0)


def _odd_out_kernel(of_ref, ob_ref, z_ref, ng_ref, w_ref, x_ref, gate_ref, g_ref, shift_ref, scale_ref,
                    wr_ref, br_ref, xo_ref, h_ref, idx_ref, gates_ref):
    parts = []
    for h in range(GDN_HV):
        cols = slice(h * GDN_DV, (h + 1) * GDN_DV)
        o = of_ref[:, cols] + ob_ref[:, cols]
        z = z_ref[:, cols].astype(F32)
        o = o * lax.rsqrt(jnp.mean(o * o, axis=-1, keepdims=True) + NORM_EPS) * ng_ref[...]
        parts.append((o * (z * jax.nn.sigmoid(z))).astype(BF16))
    y = jnp.dot(jnp.concatenate(parts, axis=1), w_ref[...], preferred_element_type=F32)
    _residual_norm_route(y, x_ref, gate_ref, g_ref, shift_ref, scale_ref, wr_ref, br_ref,
                         xo_ref, h_ref, idx_ref, gates_ref)


def _odd_out_proj(o_f, o_b, z, norm_g, w, x2d, gate, g, shift, scale, w_router, b_router, rows_per_vec, tm):
    t, d = x2d.shape
    wr, br = _router_operands(w_router, b_router)
    vec = _vec(d, rows_per_vec, tm)
    out_specs, out_shape = _route_out(t, d, tm)
    return pl.pallas_call(
        _odd_out_kernel,
        grid=(t // tm,),
        in_specs=[_row(tm, GDN_V_W), _row(tm, GDN_V_W), _row(tm, GDN_V_W),
                  _full((1, GDN_DV)), _full(w.shape), _row(tm, d), vec, _full((1, d)), vec, vec,
                  _full((d, LANES)), _full((1, LANES))],
        out_specs=out_specs, out_shape=out_shape,
        compiler_params=_CP,
        name="odd_out_proj",
    )(o_f, o_b, z, norm_g.reshape(1, GDN_DV), w, x2d, gate, g.reshape(1, d), shift, scale, wr, br)


def _split_mod(mv, bsz):
    d = mv.shape[1] // N_MOD
    lat = [mv[:bsz, j * d:(j + 1) * d][:, None, :] for j in range(N_MOD)]
    ctx = [mv[bsz:bsz + 1, j * d:(j + 1) * d][:, None, :] for j in range(N_MOD)]
    return lat, ctx


def _even_layer(x2d, xc2d, mod, mod_c, norm1_g, norm2_g, w_in, w_out, lam_p, subln_g, lam_init, cos, sin,
                w_router, b_router, bsz, n_lat, n_ctx):
    d = x2d.shape[1]
    w_in_b = w_in.astype(BF16)
    w_out_b = w_out.astype(BF16)
    q_scale = DA_DH ** -0.5 * math.log2(math.e)
    f, q, k_all, v_all = _even_in_proj(x2d, norm1_g, mod[0], mod[1], w_in_b, cos, sin, bsz, n_lat, n_lat + n_ctx,
                                       q_scale, TM_PROJ)
    ones, zeros = jnp.ones((n_ctx, ROPE_AXIS_DIM), F32), jnp.zeros((n_ctx, ROPE_AXIS_DIM), F32)
    fc, qc, kc, vc = _even_in_proj(xc2d, norm1_g, mod_c[0], mod_c[1], w_in_b, ones, zeros, bsz, n_ctx, n_ctx,
                                   q_scale, n_ctx)
    lp = lam_p.astype(F32)
    lam = jnp.exp(jnp.sum(lp[0] * lp[1])) - jnp.exp(jnp.sum(lp[2] * lp[3])) + lam_init
    k_all = lax.dynamic_update_slice(k_all, kc, (0, n_lat, 0))
    v_all = lax.dynamic_update_slice(v_all, vc, (0, n_lat, 0))
    o = _diff_attention(lam, q, k_all, v_all, subln_g, 1.0 - lam_init, TQ_ATTN, TK_ATTN)
    oc = _diff_attention(lam, qc, kc, vc, subln_g, 1.0 - lam_init, n_ctx, n_ctx)
    fm = _fourier_mix(f.reshape(bsz, n_lat, FOURIER_W)).reshape(bsz * n_lat, FOURIER_W)
    fmc = _fourier_mix(fc.reshape(bsz, n_ctx, FOURIER_W)).reshape(bsz * n_ctx, FOURIER_W)
    lat = _even_out_proj(fm, o.reshape(bsz * n_lat, DA_W), w_out_b, x2d, mod[2], norm2_g, mod[3], mod[4],
                         w_router, b_router, n_lat, TM_PROJ)
    ctx = _even_out_proj(fmc, oc.reshape(bsz * n_ctx, DA_W), w_out_b, xc2d, mod_c[2], norm2_g, mod_c[3], mod_c[4],
                         w_router, b_router, bsz * n_ctx, n_ctx)
    return lat, ctx


def _odd_layer(x2d, xc2d, mod, mod_c, norm1_g, norm2_g, w_in, conv_w, a_log, dt_bias, norm_g, w_out,
               w_router, b_router, bsz, n_lat, n_ctx):
    d = x2d.shape[1]
    q, k, v, z, gb = _odd_in_stage(x2d.reshape(bsz, n_lat, d), norm1_g, mod[0], mod[1], w_in, conv_w, a_log,
                                   dt_bias, TM_PROJ)
    qc, kc, vc, _, gbc = _odd_in_stage(xc2d.reshape(bsz, n_ctx, d), norm1_g, mod_c[0], mod_c[1], w_in, conv_w,
                                       a_log, dt_bias, n_ctx)
    s0 = jnp.zeros((bsz, GDN_HV, GDN_DK, GDN_DV), F32)
    _, sc_f = _gdn_scan(qc, kc, vc, gbc, s0, False)
    o_f, _ = _gdn_scan(q, k, v, gb, sc_f, False)
    _, sc_b = _gdn_scan(qc, kc, vc, gbc, s0, True)
    o_b, _ = _gdn_scan(q, k, v, gb, sc_b, True)
    return _odd_out_proj(o_f.reshape(bsz * n_lat, GDN_V_W), o_b.reshape(bsz * n_lat, GDN_V_W),
                         z.reshape(bsz * n_lat, GDN_V_W), norm_g, w_out.astype(BF16), x2d, mod[2], norm2_g,
                         mod[3], mod[4], w_router, b_router, n_lat, TM_GDN)


def kernel(x, c, ctx, c_ctx, norm1_g, norm2_g, w_mod, b_mod, ev_w_in, ev_w_out, ev_lam, ev_subln_g,
           od_w_in, od_conv_w, od_a_log, od_dt_bias, od_norm_g, od_w_out,
           moe_w_router, moe_b_router, moe_w1, moe_b1, moe_w2, moe_b2, final_g):
    bsz, n_lat, d = x.shape
    n_ctx = ctx.shape[1]
    assert w_mod.shape[0] == 2, "kernel is written for one even (attention) and one odd (DeltaNet) layer"
    t_lat = bsz * n_lat
    cos, sin = _axial_rope_tables(n_lat // GRID_W)
    c_rows = jnp.zeros((8, d), F32).at[:bsz].set(c).at[bsz].set(c_ctx)
    x2d = x.reshape(t_lat, d)
    xc2d = ctx.reshape(bsz * n_ctx, d)

    mod, mod_c = _split_mod(_mod_vectors(c_rows, w_mod, b_mod, 0), bsz)
    (x2d, h2, idx, gates), (xc2d, h2c, idx_c, gates_c) = _even_layer(
        x2d, xc2d, mod, mod_c, norm1_g[0], norm2_g[0], ev_w_in[0], ev_w_out[0], ev_lam[0], ev_subln_g[0],
        _diff_lambda_init(0), cos, sin, moe_w_router[0], moe_b_router[0], bsz, n_lat, n_ctx)
    yb, pos = _moe_ffn(jnp.concatenate([h2, h2c], axis=0), jnp.concatenate([idx, idx_c], axis=0),
                       moe_w1, moe_b1[0], moe_w2, moe_b2[0], 0, TM_PROJ)
    x2d = _moe_combine(yb, pos[:t_lat], gates, x2d, mod[5], n_lat, TM_PROJ)
    xc2d = _moe_combine(yb, pos[t_lat:], gates_c, xc2d, mod_c[5], bsz * n_ctx, n_ctx)

    mod, mod_c = _split_mod(_mod_vectors(c_rows, w_mod, b_mod, 1), bsz)
    x2d, h2, idx, gates = _odd_layer(x2d, xc2d, mod, mod_c, norm1_g[1], norm2_g[1], od_w_in[0], od_conv_w[0],
                                     od_a_log[0], od_dt_bias[0], od_norm_g[0], od_w_out[0],
                                     moe_w_router[1], moe_b_router[1], bsz, n_lat, n_ctx)
    yb, pos = _moe_ffn(h2, idx, moe_w1, moe_b1[1], moe_w2, moe_b2[1], 1, TM_PROJ)
    return _moe_combine(yb, pos, gates, x2d, mod[5], n_lat, TM_PROJ, final_g=final_g).reshape(bsz, n_lat, d)
```

```python
import functools
import math

import jax
import jax.numpy as jnp
from jax import lax
from jax.experimental import pallas as pl
from jax.experimental.pallas import tpu as pltpu

D_MODEL = 1024
N_MOD = 6
NORM_EPS = 1e-6
GRID_W = 64

FOURIER_GROUPS = 4
FOURIER_GD = 64
FOURIER_W = FOURIER_GROUPS * FOURIER_GD
DA_HEADS = 6
DA_DH = 64
DA_VD = 2 * DA_DH
DA_W = DA_HEADS * DA_VD
ROPE_BASE = 10000.0
ROPE_AXIS_DIM = DA_DH // 2
SUBLN_EPS = 1e-5

GDN_HK = 8
GDN_HV = 16
GDN_DK = 128
GDN_DV = 128
GDN_QK_W = GDN_HK * GDN_DK
GDN_V_W = GDN_HV * GDN_DV
GDN_QKV_W = 2 * GDN_QK_W + GDN_V_W
GDN_MAIN_W = GDN_QKV_W + GDN_V_W
GDN_CONV = 5
GDN_CHUNK = 64

N_EXPERTS = 32
TOP_K = 4
D_FF = 1024
SWIGLU_LIMIT = 7.0
SWIGLU_ALPHA = 1.702
MOE_BLOCK = 512
MOE_PARTS = 4

LANES = 128
SUBLANES_BF16 = 16
MXU_DIM = 256
VMEM_LIMIT = 56 * 1024 * 1024
BF16 = jnp.bfloat16
F32 = jnp.float32
HI = lax.Precision.HIGHEST

TM_PROJ = 512
TQ_ATTN, TK_ATTN = 512, 1280

_CP = pltpu.CompilerParams(vmem_limit_bytes=VMEM_LIMIT)


def _diff_lambda_init(layer_idx):
    return 0.8 - 0.6 * math.exp(-0.3 * layer_idx)


def _axial_rope_tables(rows):
    t = jnp.arange(rows * GRID_W, dtype=jnp.int32)
    row = (t // GRID_W).astype(F32)
    col = (t % GRID_W).astype(F32)
    inv = ROPE_BASE ** (-jnp.arange(0, ROPE_AXIS_DIM, 2, dtype=F32) / ROPE_AXIS_DIM)
    ang = jnp.concatenate([row[:, None] * inv, col[:, None] * inv], axis=-1)
    return jnp.cos(ang), jnp.sin(ang)


def _norm_mod(x, g_ref, shift_ref, scale_ref):
    h = x * lax.rsqrt(jnp.mean(x * x, axis=-1, keepdims=True) + NORM_EPS) * g_ref[...]
    return (h * (1.0 + scale_ref[0]) + shift_ref[0]).astype(BF16)


def _route(logits):
    lane = lax.broadcasted_iota(jnp.int32, logits.shape, 1)
    lane_f = lane.astype(F32)
    neg = jnp.float32(-jnp.inf)
    rest = jnp.where(lane < N_EXPERTS, logits, neg)
    idx = jnp.zeros(logits.shape, F32)
    val = jnp.full(logits.shape, neg, F32)
    for j in range(TOP_K):
        m = jnp.max(rest, axis=-1, keepdims=True)
        sel = jnp.min(jnp.where(rest == m, lane_f, float(LANES)), axis=-1, keepdims=True)
        idx = jnp.where(lane == j, sel, idx)
        val = jnp.where(lane == j, m, val)
        rest = jnp.where(lane_f == sel, neg, rest)
    e = jnp.exp(val - val[:, 0:1])
    return idx, e / jnp.sum(e, axis=-1, keepdims=True)


def _residual_norm_route(y, x_ref, gate_ref, g_ref, shift_ref, scale_ref, wr_ref, br_ref,
                         xo_ref, h_ref, idx_ref, gates_ref):
    xn = x_ref[...] + gate_ref[0] * y
    xo_ref[...] = xn
    h = _norm_mod(xn, g_ref, shift_ref, scale_ref)
    h_ref[...] = h
    idx, gates = _route(jnp.dot(h, wr_ref[...], preferred_element_type=F32) + br_ref[...])
    idx_ref[...] = idx.astype(jnp.int32)
    gates_ref[...] = gates


def _row(tm, n):
    return pl.BlockSpec((tm, n), lambda i: (i, 0))


def _full(shape):
    return pl.BlockSpec(shape, lambda i: (0,) * len(shape))


def _vec(d, rows_per_vec, tm):
    per = rows_per_vec // tm
    return pl.BlockSpec((1, 1, d), lambda i: (i // per, 0, 0))


def _router_operands(w_router, b_router):
    d = w_router.shape[0]
    wr = jnp.zeros((d, LANES), BF16).at[:, :N_EXPERTS].set(w_router.astype(BF16))
    br = jnp.zeros((1, LANES), F32).at[0, :N_EXPERTS].set(b_router)
    return wr, br


def _route_out(t, d, tm):
    specs = [_row(tm, d), _row(tm, d), _row(tm, LANES), _row(tm, LANES)]
    shapes = [jax.ShapeDtypeStruct((t, d), F32), jax.ShapeDtypeStruct((t, d), BF16),
              jax.ShapeDtypeStruct((t, LANES), jnp.int32), jax.ShapeDtypeStruct((t, LANES), F32)]
    return specs, shapes


def _mod_kernel(c_ref, w_ref, b_ref, o_ref):
    c = c_ref[...]
    s = (c * jax.nn.sigmoid(c)).astype(BF16)
    o_ref[...] = jnp.dot(s, w_ref[0].astype(BF16), preferred_element_type=F32) + b_ref[0]


def _mod_vectors(c_rows, w_mod, b_mod, layer):
    d = c_rows.shape[1]
    return pl.pallas_call(
        _mod_kernel,
        grid=(N_MOD,),
        in_specs=[pl.BlockSpec(c_rows.shape, lambda n: (0, 0)),
                  pl.BlockSpec((1, d, d), lambda n: (layer, 0, n)),
                  pl.BlockSpec((1, 1, d), lambda n: (layer, 0, n))],
        out_specs=pl.BlockSpec((c_rows.shape[0], d), lambda n: (0, n)),
        out_shape=jax.ShapeDtypeStruct((c_rows.shape[0], N_MOD * d), F32),
        compiler_params=_CP,
        name="mod_vectors",
    )(c_rows, w_mod, b_mod.reshape(b_mod.shape[0], 1, -1))


def _even_in_kernel(x_ref, g_ref, shift_ref, scale_ref, w_ref, rc_ref, rs_ref, f_ref, q_ref, k_ref, v_ref, *, q_scale):
    h = _norm_mod(x_ref[...], g_ref, shift_ref, scale_ref)
    p = jnp.dot(h, w_ref[...], preferred_element_type=F32)
    f_ref[...] = p[:, :FOURIER_W]
    rc, rs = rc_ref[...], rs_ref[...]
    lane = lax.broadcasted_iota(jnp.int32, rc.shape, 1)
    first = (lane % DA_DH) < ROPE_AXIS_DIM

    def rope(t):
        partner = jnp.where(first, pltpu.roll(t, LANES - ROPE_AXIS_DIM, 1), pltpu.roll(t, ROPE_AXIS_DIM, 1))
        return t * rc + partner * rs

    for hd in range(DA_HEADS):
        cq = slice(FOURIER_W + hd * DA_VD, FOURIER_W + (hd + 1) * DA_VD)
        ck = slice(FOURIER_W + DA_W + hd * DA_VD, FOURIER_W + DA_W + (hd + 1) * DA_VD)
        q_ref[0, :, hd * DA_VD:(hd + 1) * DA_VD] = (rope(p[:, cq]) * q_scale).astype(BF16)
        k_ref[0, :, hd * DA_VD:(hd + 1) * DA_VD] = rope(p[:, ck]).astype(BF16)
    v_ref[0] = p[:, FOURIER_W + 2 * DA_W:].astype(BF16)


def _even_in_proj(x2d, g, shift, scale, w, cos, sin, bsz, n_seq, n_keys, q_scale, tm):
    t, d = x2d.shape
    bpb = n_seq // tm
    rc = jnp.tile(cos, (1, LANES // ROPE_AXIS_DIM))
    rs = jnp.tile(jnp.concatenate([-sin, sin], axis=1), (1, LANES // DA_DH))
    tab = pl.BlockSpec((tm, LANES), lambda i: (i % bpb, 0))
    seq = pl.BlockSpec((1, tm, DA_W), lambda i: (i // bpb, i % bpb, 0))
    vec = _vec(d, n_seq * (bsz // shift.shape[0]), tm)
    return pl.pallas_call(
        functools.partial(_even_in_kernel, q_scale=q_scale),
        grid=(t // tm,),
        in_specs=[_row(tm, d), _full((1, d)), vec, vec, _full(w.shape), tab, tab],
        out_specs=[_row(tm, FOURIER_W), seq, seq, seq],
        out_shape=[jax.ShapeDtypeStruct((t, FOURIER_W), F32), jax.ShapeDtypeStruct((bsz, n_seq, DA_W), BF16),
                   jax.ShapeDtypeStruct((bsz, n_keys, DA_W), BF16), jax.ShapeDtypeStruct((bsz, n_keys, DA_W), BF16)],
        compiler_params=_CP,
        name="even_in_proj",
    )(x2d, g.reshape(1, d), shift, scale, w, rc, rs)


def _diff_attn_kernel(lam_ref, q_ref, k_ref, v_ref, g_ref, o_ref, qs_ref, s_ref, m_ref, acc_ref, *,
                      tk, n_sub, out_scale):
    tq = q_ref.shape[1]
    n_kv = k_ref.shape[1] // tk
    rb = 2 * tq // n_sub
    q = q_ref[0]
    lane = lax.broadcasted_iota(jnp.int32, q.shape, 1)
    zero = jnp.zeros_like(q)
    qs_ref[:tq] = jnp.where(lane < DA_DH, q, zero)
    qs_ref[tq:] = jnp.where(lane >= DA_DH, q, zero)
    m_ref[...] = jnp.full(m_ref.shape, -1e30, F32)
    acc_ref[...] = jnp.zeros(acc_ref.shape, F32)
    ones = jnp.ones((tk, LANES), BF16)

    def scores(i, slot):
        off = pl.multiple_of(i * tk, tk)
        k = k_ref[0, pl.ds(off, tk), :]
        s_ref[slot] = lax.dot_general(qs_ref[...], k, (((1,), (1,)), ((), ())), preferred_element_type=F32)

    def consume(i, slot):
        off = pl.multiple_of(i * tk, tk)
        v_ext = jnp.concatenate([v_ref[0, pl.ds(off, tk), :], ones], axis=1)
        for r in range(n_sub):
            rows = pl.ds(r * rb, rb)
            s = s_ref[slot, rows, :]
            m_prev = m_ref[rows, :]
            m_new = jnp.maximum(m_prev, jnp.max(s, axis=1, keepdims=True))
            alpha = jnp.exp2(m_prev - m_new)
            p = jnp.exp2(s - jnp.tile(m_new, (1, tk // LANES)))
            pv = jnp.dot(p.astype(BF16), v_ext, preferred_element_type=F32)
            acc_ref[rows, :] = acc_ref[rows, :] * jnp.tile(alpha, (1, 2)) + pv
            m_ref[rows, :] = m_new

    scores(0, 0)

    def body(j, carry):
        scores(2 * j + 1, 1)
        consume(2 * j, 0)
        scores(2 * j + 2, 0)
        consume(2 * j + 1, 1)
        return carry

    lax.fori_loop(0, (n_kv - 1) // 2, body, 0)
    consume(n_kv - 1, 0)
    acc = acc_ref[...]
    o1 = acc[:tq, :LANES] / acc[:tq, LANES:]
    o2 = acc[tq:, :LANES] / acc[tq:, LANES:]
    o = o1 - lam_ref[0] * o2
    ms = jnp.mean(o * o, axis=-1, keepdims=True)
    o = o * lax.rsqrt(ms + SUBLN_EPS) * g_ref[...] * out_scale
    o_ref[0] = o.astype(o_ref.dtype)


def _diff_attention(lam, q, k_all, v_all, subln_g, out_scale, tq, tk, n_sub=1):
    b, n, _ = q.shape
    nk = k_all.shape[1]
    assert n % tq == 0 and nk % tk == 0 and tk % MXU_DIM == 0 and (nk // tk) % 2 == 1
    grid_spec = pltpu.PrefetchScalarGridSpec(
        num_scalar_prefetch=1,
        grid=(b, DA_HEADS, n // tq),
        in_specs=[pl.BlockSpec((1, tq, DA_VD), lambda bi, hi, qi, lam_r: (bi, qi, hi)),
                  pl.BlockSpec((1, nk, DA_VD), lambda bi, hi, qi, lam_r: (bi, 0, hi)),
                  pl.BlockSpec((1, nk, DA_VD), lambda bi, hi, qi, lam_r: (bi, 0, hi)),
                  pl.BlockSpec((1, DA_VD), lambda bi, hi, qi, lam_r: (0, 0))],
        out_specs=pl.BlockSpec((1, tq, DA_VD), lambda bi, hi, qi, lam_r: (bi, qi, hi)),
        scratch_shapes=[pltpu.VMEM((2 * tq, LANES), BF16), pltpu.VMEM((2, 2 * tq, tk), F32),
                        pltpu.VMEM((2 * tq, LANES), F32), pltpu.VMEM((2 * tq, 2 * LANES), F32)],
    )
    return pl.pallas_call(
        functools.partial(_diff_attn_kernel, tk=tk, n_sub=n_sub, out_scale=out_scale),
        grid_spec=grid_spec,
        out_shape=jax.ShapeDtypeStruct((b, n, DA_W), BF16),
        compiler_params=_CP,
        name="diff_attention",
    )(lam.reshape(1), q, k_all, v_all, subln_g.reshape(1, DA_VD))


def _fft_stage1_kernel(x_ref, fch_ref, f1_ref, tc_ref, ts_ref, yr_ref, yi_ref, *, nb):
    w = FOURIER_W
    r = x_ref.shape[1]
    for bl in range(nb):
        cols = slice(bl * w, (bl + 1) * w)
        ab = jnp.dot(x_ref[0, :, cols], fch_ref[...], preferred_element_type=F32, precision=HI)
        z = jnp.concatenate([ab[:, :w], ab[:, w:]], axis=0)
        y = jnp.dot(f1_ref[...], z, preferred_element_type=F32, precision=HI)
        yr, yi = y[:r], y[r:]
        tc, ts = tc_ref[:, cols], ts_ref[:, cols]
        yr_ref[0, :, cols] = yr * tc + yi * ts
        yi_ref[0, :, cols] = yi * tc - yr * ts


def _fft_stage2_kernel(yr_ref, yi_ref, f2_ref, o_ref, *, nc):
    w = FOURIER_W
    for cl in range(nc):
        y = jnp.concatenate([yr_ref[0, cl], yi_ref[0, cl]], axis=0)
        o_ref[0, :, cl * w:(cl + 1) * w] = jnp.dot(f2_ref[...], y, preferred_element_type=F32, precision=HI)


def _fourier_mix(f, nb=8):
    bsz, n, w = f.shape
    r = math.isqrt(n)
    assert r * r == n and r % nb == 0 and w == FOURIER_W
    two_pi = 2.0 * math.pi
    k = jnp.arange(r, dtype=jnp.int32)
    ang = two_pi * ((k[:, None] * k[None, :]) % r).astype(F32) / r
    c1, s1 = jnp.cos(ang), jnp.sin(ang)
    kc = jnp.arange(FOURIER_GD, dtype=jnp.int32)
    angc = two_pi * ((kc[:, None] * kc[None, :]) % FOURIER_GD).astype(F32) / FOURIER_GD
    eye_g = jnp.eye(FOURIER_GROUPS, dtype=F32)
    fch = jnp.concatenate([jnp.kron(eye_g, jnp.cos(angc)), jnp.kron(eye_g, jnp.sin(angc))], axis=1)
    f1 = jnp.concatenate([jnp.concatenate([c1, -s1], axis=1), jnp.concatenate([-s1, -c1], axis=1)], axis=0)
    f2 = jnp.concatenate([c1, s1], axis=1) * (1.0 / math.sqrt(n * FOURIER_GD))
    angt = two_pi * ((k[:, None] * k[None, :]) % n).astype(F32) / n
    tc = jnp.repeat(jnp.cos(angt), w, axis=1)
    ts = jnp.repeat(jnp.sin(angt), w, axis=1)
    xv = f.reshape(bsz, r, r * w)
    blk = pl.BlockSpec((1, r, nb * w), lambda bi, j: (bi, 0, j))
    tab = pl.BlockSpec((r, nb * w), lambda bi, j: (0, j))
    full = lambda a: pl.BlockSpec(a.shape, lambda bi, j: (0, 0))
    yshape = jax.ShapeDtypeStruct((bsz, r, r * w), F32)
    yr, yi = pl.pallas_call(
        functools.partial(_fft_stage1_kernel, nb=nb),
        grid=(bsz, r // nb),
        in_specs=[blk, full(fch), full(f1), tab, tab],
        out_specs=[blk, blk],
        out_shape=[yshape, yshape],
        compiler_params=_CP,
        name="fft_stage1",
    )(xv, fch, f1, tc, ts)
    y4 = lambda a: a.reshape(bsz, r, r, w)
    yblk = pl.BlockSpec((1, nb, r, w), lambda bi, j: (bi, j, 0, 0))
    out = pl.pallas_call(
        functools.partial(_fft_stage2_kernel, nc=nb),
        grid=(bsz, r // nb),
        in_specs=[yblk, yblk, full(f2)],
        out_specs=blk,
        out_shape=yshape,
        compiler_params=_CP,
        name="fft_stage2",
    )(y4(yr), y4(yi), f2)
    return out.reshape(bsz, n, w)


def _even_out_kernel(fm_ref, o_ref, w_ref, x_ref, gate_ref, g_ref, shift_ref, scale_ref, wr_ref, br_ref,
                     xo_ref, h_ref, idx_ref, gates_ref):
    y = (jnp.dot(fm_ref[...].astype(BF16), w_ref[:FOURIER_W], preferred_element_type=F32)
         + jnp.dot(o_ref[...], w_ref[FOURIER_W:], preferred_element_type=F32))
    _residual_norm_route(y, x_ref, gate_ref, g_ref, shift_ref, scale_ref, wr_ref, br_ref,
                         xo_ref, h_ref, idx_ref, gates_ref)


def _even_out_proj(fm, o, w, x2d, gate, g, shift, scale, w_router, b_router, rows_per_vec, tm):
    t, d = x2d.shape
    wr, br = _router_operands(w_router, b_router)
    vec = _vec(d, rows_per_vec, tm)
    out_specs, out_shape = _route_out(t, d, tm)
    return pl.pallas_call(
        _even_out_kernel,
        grid=(t // tm,),
        in_specs=[_row(tm, FOURIER_W), _row(tm, DA_W), _full(w.shape), _row(tm, d), vec, _full((1, d)), vec, vec,
                  _full((d, LANES)), _full((1, LANES))],
        out_specs=out_specs, out_shape=out_shape,
        compiler_params=_CP,
        name="even_out_proj",
    )(fm, o, w, x2d, gate, g.reshape(1, d), shift, scale, wr, br)


def _moe_kernel(be_ref, x_ref, w1_ref, perm_ref, b1g_ref, b1l_ref, w2_ref, b2_ref, *rest):
    o_ref, w1g_s, w1l_s, w2_s = rest[-4:]
    i = pl.program_id(0)

    @pl.when((i == 0) | (be_ref[i] != be_ref[jnp.maximum(i - 1, 0)]))
    def _():
        for c in range(w1_ref.shape[3] // MXU_DIM):
            blk = jnp.dot(w1_ref[0, 0, :, c * MXU_DIM:(c + 1) * MXU_DIM].astype(BF16), perm_ref[...],
                          preferred_element_type=F32)
            w1g_s[:, c * LANES:(c + 1) * LANES] = blk[:, :LANES].astype(BF16)
            w1l_s[:, c * LANES:(c + 1) * LANES] = blk[:, LANES:].astype(BF16)
        w2_s[...] = w2_ref[0, 0].astype(BF16)

    x = x_ref[...]
    ug = jnp.dot(x, w1g_s[...], preferred_element_type=F32) + b1g_ref[0]
    ul = jnp.dot(x, w1l_s[...], preferred_element_type=F32) + b1l_ref[0]
    glu = jnp.minimum(ug, SWIGLU_LIMIT)
    lin = jnp.clip(ul, -SWIGLU_LIMIT, SWIGLU_LIMIT)
    act = glu * jax.nn.sigmoid(SWIGLU_ALPHA * glu) * (lin + 1.0)
    y = jnp.dot(act.astype(BF16), w2_s[...], preferred_element_type=F32) + b2_ref[0]
    o_ref[...] = y.astype(o_ref.dtype)


def _moe_experts(block_expert, xb, w1_all, b1g, b1l, w2_all, layer, b2, yb_prev, first_block, n_blocks_total):
    n_rows, d = xb.shape
    n_blocks = n_rows // MOE_BLOCK
    src = jnp.arange(MXU_DIM)[:, None]
    dst = jnp.arange(MXU_DIM)[None, :]
    perm = (src == jnp.where(dst < LANES, 2 * dst, 2 * (dst - LANES) + 1)).astype(BF16)
    bspec = lambda s: pl.BlockSpec((1,) + s, lambda i, be: (be[i], 0, 0))
    wspec = lambda s: pl.BlockSpec((1, 1) + s, lambda i, be: (layer, be[i], 0, 0))
    in_specs = [pl.BlockSpec((MOE_BLOCK, d), lambda i, be: (i, 0)),
                wspec((d, 2 * D_FF)), pl.BlockSpec((MXU_DIM, MXU_DIM), lambda i, be: (0, 0)),
                bspec((1, D_FF)), bspec((1, D_FF)), wspec((D_FF, d)), bspec((1, d))]
    args = [block_expert, xb, w1_all, perm, b1g, b1l, w2_all, b2]
    aliases = {}
    if yb_prev is not None:
        in_specs.append(pl.BlockSpec(memory_space=pl.ANY))
        aliases = {len(args): 0}
        args.append(yb_prev)
    grid_spec = pltpu.PrefetchScalarGridSpec(
        num_scalar_prefetch=1,
        grid=(n_blocks,),
        in_specs=in_specs,
        out_specs=pl.BlockSpec((MOE_BLOCK, d), lambda i, be: (first_block + i, 0)),
        scratch_shapes=[pltpu.VMEM((d, D_FF), BF16), pltpu.VMEM((d, D_FF), BF16), pltpu.VMEM((D_FF, d), BF16)],
    )
    return pl.pallas_call(
        _moe_kernel,
        grid_spec=grid_spec,
        out_shape=jax.ShapeDtypeStruct((n_blocks_total * MOE_BLOCK, d), BF16),
        input_output_aliases=aliases,
        compiler_params=_CP,
        name="moe_experts",
    )(*args)


def _rank_kernel(idx_ref, rank_ref, cnt_ref, carry_ref):
    i = pl.program_id(0)

    @pl.when(i == 0)
    def _():
        carry_ref[...] = jnp.zeros(carry_ref.shape, F32)

    idx = idx_ref[...]
    tm = idx.shape[0]
    lane = lax.broadcasted_iota(jnp.int32, idx.shape, 1)
    sel = [lane == idx[:, j:j + 1] for j in range(TOP_K)]
    onehot = sel[0].astype(F32)
    for j in range(1, TOP_K):
        onehot = onehot + sel[j].astype(F32)
    row = lax.broadcasted_iota(jnp.int32, (tm, tm), 0)
    col = lax.broadcasted_iota(jnp.int32, (tm, tm), 1)
    before = (row > col).astype(BF16)
    prefix = jnp.dot(before, onehot.astype(BF16), preferred_element_type=F32) + carry_ref[0:1, :]
    rank = jnp.zeros(idx.shape, F32)
    for j in range(TOP_K):
        rank = jnp.where(lane == j, jnp.sum(jnp.where(sel[j], prefix, 0.0), axis=-1, keepdims=True), rank)
    rank_ref[...] = rank.astype(jnp.int32)
    carry_ref[0:1, :] = carry_ref[0:1, :] + jnp.sum(onehot, axis=0, keepdims=True)
    cnt_ref[...] = carry_ref[...]


def _expert_ranks(idx, tm):
    t = idx.shape[0]
    rank, cnt = pl.pallas_call(
        _rank_kernel,
        grid=(t // tm,),
        in_specs=[_row(tm, LANES)],
        out_specs=[_row(tm, LANES), _full((8, LANES))],
        out_shape=[jax.ShapeDtypeStruct((t, LANES), jnp.int32), jax.ShapeDtypeStruct((8, LANES), F32)],
        scratch_shapes=[pltpu.VMEM((8, LANES), F32)],
        compiler_params=_CP,
        name="expert_ranks",
    )(idx)
    return rank[:, :TOP_K], cnt[0, :N_EXPERTS].astype(jnp.int32)


def _moe_ffn(h, idx, w1_all, b1, w2_all, b2, layer, tm):
    n_tok, d = h.shape
    n_assign = n_tok * TOP_K
    e_flat = idx[:, :TOP_K].reshape(n_assign)
    order = jnp.argsort(e_flat).astype(jnp.int32)
    rank, counts = _expert_ranks(idx, tm)
    starts = jnp.cumsum(counts) - counts
    padded = (counts + MOE_BLOCK - 1) // MOE_BLOCK * MOE_BLOCK
    pad_ends = jnp.cumsum(padded)
    pad_starts = pad_ends - padded
    n_blocks = -(-(n_assign + N_EXPERTS * (MOE_BLOCK - 1)) // MOE_BLOCK)
    n_rows = n_blocks * MOE_BLOCK
    block_start = jnp.arange(n_blocks, dtype=jnp.int32) * MOE_BLOCK
    block_expert = jnp.minimum(jnp.sum(pad_ends[None, :] <= block_start[:, None], axis=1, dtype=jnp.int32),
                               N_EXPERTS - 1)
    shift = pad_starts - starts
    pos = pad_starts[e_flat].reshape(n_tok, TOP_K) + rank
    r = jnp.arange(n_rows, dtype=jnp.int32)
    src = jnp.clip(r - jnp.repeat(shift[block_expert], MOE_BLOCK), 0, n_assign - 1)
    row_tok = order[src] // TOP_K
    assert n_blocks % MOE_PARTS == 0
    nbp = n_blocks // MOE_PARTS
    yb = None
    for part in range(MOE_PARTS):
        blocks = slice(part * nbp, (part + 1) * nbp)
        xb = h[row_tok[part * nbp * MOE_BLOCK:(part + 1) * nbp * MOE_BLOCK]]
        yb = _moe_experts(block_expert[blocks], xb, w1_all, b1[:, None, 0::2], b1[:, None, 1::2], w2_all, layer,
                          b2[:, None, :], yb, part * nbp, n_blocks)
    return yb, pos


def _combine_kernel(*refs, final):
    y_refs = refs[:TOP_K]
    gt_ref, x_ref, g5_ref = refs[TOP_K:TOP_K + 3]
    o_ref = refs[-1]
    gt = gt_ref[...]
    acc = y_refs[0][...].astype(F32) * gt[:, 0:1]
    for j in range(1, TOP_K):
        acc = acc + y_refs[j][...].astype(F32) * gt[:, j:j + 1]
    xn = x_ref[...] + g5_ref[0] * acc
    if final:
        fg_ref = refs[TOP_K + 3]
        xn = xn * lax.rsqrt(jnp.mean(xn * xn, axis=-1, keepdims=True) + NORM_EPS) * fg_ref[...]
    o_ref[...] = xn


def _moe_combine(yb, pos, gates, x2d, gate5, rows_per_vec, tm, final_g=None):
    t, d = x2d.shape
    ys = [yb[pos[:, j]] for j in range(TOP_K)]
    in_specs = [_row(tm, d)] * TOP_K + [_row(tm, LANES), _row(tm, d), _vec(d, rows_per_vec, tm)]
    args = ys + [gates, x2d, gate5]
    if final_g is not None:
        in_specs.append(_full((1, d)))
        args.append(final_g.reshape(1, d))
    return pl.pallas_call(
        functools.partial(_combine_kernel, final=final_g is not None),
        grid=(t // tm,),
        in_specs=in_specs,
        out_specs=_row(tm, d),
        out_shape=jax.ShapeDtypeStruct((t, d), F32),
        compiler_params=_CP,
        name="moe_combine",
    )(*args)


HALO = SUBLANES_BF16


def _odd_in_kernel(xm_ref, xp_ref, xn_ref, g_ref, shift_ref, scale_ref, wqkv_ref, wz_ref, wab_ref, cw_ref,
                   alog_ref, dtb_ref, q_ref, k_ref, v_ref, z_ref, gb_ref):
    i = pl.program_id(1)
    n_i = pl.num_programs(1)
    tm = xm_ref.shape[1]
    half = GDN_CONV // 2
    keep_prev = (i > 0).astype(BF16)
    keep_next = (i < n_i - 1).astype(BF16)
    h_main = _norm_mod(xm_ref[0], g_ref, shift_ref, scale_ref)
    h_ext = jnp.concatenate([_norm_mod(xp_ref[0], g_ref, shift_ref, scale_ref) * keep_prev, h_main,
                             _norm_mod(xn_ref[0], g_ref, shift_ref, scale_ref) * keep_next], axis=0)
    gw = MXU_DIM
    for cg in range(GDN_QKV_W // gw):
        cols = slice(cg * gw, (cg + 1) * gw)
        ext = jnp.dot(h_ext, wqkv_ref[:, cols], preferred_element_type=F32)
        acc = ext[HALO:HALO + tm] * cw_ref[half:half + 1, cols]
        for j in range(GDN_CONV):
            if j != half:
                sh = pltpu.roll(ext, (half - j) % (tm + 2 * HALO), 0)[HALO:HALO + tm]
                acc = acc + sh * cw_ref[j:j + 1, cols]
        y = acc * jax.nn.sigmoid(acc)
        for sub in range(gw // LANES):
            hd = cg * (gw // LANES) + sub
            yh = y[:, sub * LANES:(sub + 1) * LANES]
            if hd < 2 * GDN_HK:
                yh = yh * lax.rsqrt(jnp.sum(yh * yh, axis=-1, keepdims=True) + 1e-6)
                if hd < GDN_HK:
                    q_ref[0, :, hd * LANES:(hd + 1) * LANES] = yh * (GDN_DK ** -0.5)
                else:
                    k_ref[0, :, (hd - GDN_HK) * LANES:(hd - GDN_HK + 1) * LANES] = yh
            else:
                v_ref[0, :, (hd - 2 * GDN_HK) * LANES:(hd - 2 * GDN_HK + 1) * LANES] = yh
    z_ref[0] = jnp.dot(h_main, wz_ref[...], preferred_element_type=F32).astype(BF16)
    ab = jnp.dot(h_main, wab_ref[...], preferred_element_type=F32)
    xa = ab + dtb_ref[...]
    softplus = jnp.maximum(xa, 0.0) + jnp.log1p(jnp.exp(-jnp.abs(xa)))
    lane = lax.broadcasted_iota(jnp.int32, ab.shape, 1)
    gb_ref[0] = jnp.where(lane < 2 * GDN_HV, -jnp.exp(alog_ref[...]) * softplus, jax.nn.sigmoid(ab))


def _odd_in_stage(x3d, g, shift, scale, w_in, conv_w, a_log, dt_bias, tm):
    b, t, d = x3d.shape
    w_qkv = w_in[:, :GDN_QKV_W].astype(BF16)
    w_z = w_in[:, GDN_QKV_W:GDN_MAIN_W].astype(BF16)
    w_ab = jnp.zeros((d, LANES), BF16).at[:, :4 * GDN_HV].set(w_in[:, GDN_MAIN_W:].astype(BF16))
    nb = tm // HALO
    last = t // HALO - 1
    pad = lambda a: jnp.zeros((1, LANES), F32).at[0, :2 * GDN_HV].set(a.reshape(-1))
    per_vec = b // shift.shape[0]
    vec = pl.BlockSpec((1, 1, d), lambda bi, i: (bi // per_vec, 0, 0))
    seq = lambda n: pl.BlockSpec((1, tm, n), lambda bi, i: (bi, i, 0))
    full = lambda s: pl.BlockSpec(s, lambda bi, i: (0,) * len(s))
    f = lambda n, dt=F32: jax.ShapeDtypeStruct((b, t, n), dt)
    return pl.pallas_call(
        _odd_in_kernel,
        grid=(b, t // tm),
        in_specs=[seq(d),
                  pl.BlockSpec((1, HALO, d), lambda bi, i: (bi, jnp.maximum(i * nb - 1, 0), 0)),
                  pl.BlockSpec((1, HALO, d), lambda bi, i: (bi, jnp.minimum((i + 1) * nb, last), 0)),
                  full((1, d)), vec, vec, full(w_qkv.shape), full(w_z.shape), full(w_ab.shape),
                  full((GDN_CONV, GDN_QKV_W)), full((1, LANES)), full((1, LANES))],
        out_specs=[seq(GDN_QK_W), seq(GDN_QK_W), seq(GDN_V_W), seq(GDN_V_W), seq(LANES)],
        out_shape=[f(GDN_QK_W), f(GDN_QK_W), f(GDN_V_W), f(GDN_V_W, BF16), f(LANES)],
        compiler_params=_CP,
        name="odd_in_stage",
    )(x3d, x3d, x3d, g.reshape(1, d), shift, scale, w_qkv, w_z, w_ab, conv_w, pad(a_log), pad(dt_bias))


def _gdn_chunk_kernel(q_ref, k_ref, v_ref, gb_ref, s0_ref, *rest, reverse, g_lane, b_lane):
    o_ref, sfin_ref, s_ref = rest[-3:]
    add_ref = rest[0] if len(rest) == 4 else None
    c = pl.program_id(0)
    n_c = pl.num_programs(0)
    cs = GDN_CHUNK
    rep = GDN_HV // GDN_HK
    nb = q_ref.shape[0]

    @pl.when(c == 0)
    def _():
        s_ref[...] = s0_ref[...].reshape(s_ref.shape)

    row = lax.broadcasted_iota(jnp.int32, (cs, cs), 0)
    col = lax.broadcasted_iota(jnp.int32, (cs, cs), 1)
    incl = (row <= col) if reverse else (row >= col)
    strict = (row < col) if reverse else (row > col)
    eye = (row == col).astype(F32)
    incl_b = incl.astype(BF16)
    last = 0 if reverse else cs - 1
    tn = (((0,), (1,)), ((), ()))

    heads = range(GDN_HV)
    gc, gr, bc, qh, kh, vh = [], [], [], [], [], []
    for b in range(nb):
        gb = gb_ref[b]
        gb_hi = gb.astype(BF16)
        gb_lo = (gb - gb_hi.astype(F32)).astype(BF16)
        g_col = (jnp.dot(incl_b, gb_hi, preferred_element_type=F32)
                 + jnp.dot(incl_b, gb_lo, preferred_element_type=F32))
        g_row = (lax.dot_general(gb_hi, incl_b, tn, preferred_element_type=F32)
                 + lax.dot_general(gb_lo, incl_b, tn, preferred_element_type=F32))
        gc += [g_col[:, g_lane + h:g_lane + h + 1] for h in heads]
        gr += [g_row[g_lane + h:g_lane + h + 1, :] for h in heads]
        bc += [gb[:, b_lane + h:b_lane + h + 1] for h in heads]
        qh += [q_ref[b, :, h * GDN_DK:(h + 1) * GDN_DK] for h in range(GDN_HK)]
        kh += [k_ref[b, :, h * GDN_DK:(h + 1) * GDN_DK] for h in range(GDN_HK)]
        vh += [v_ref[b, :, h * GDN_DV:(h + 1) * GDN_DV] for h in heads]
    gc, gr, bc = jnp.stack(gc), jnp.stack(gr), jnp.stack(bc)
    qh, kh, vh = jnp.stack(qh), jnp.stack(kh), jnp.stack(vh)
    ge = gc[:, last:last + 1, :]
    kh_b = kh.astype(BF16)
    bnt = (((2,), (2,)), ((0,), (0,)))
    kk = lax.dot_general(kh_b, kh_b, bnt, preferred_element_type=F32)
    qk = lax.dot_general(qh.astype(BF16), kh_b, bnt, preferred_element_type=F32)
    kk, qk = jnp.repeat(kk, rep, axis=0), jnp.repeat(qk, rep, axis=0)
    qv, kv = jnp.repeat(qh, rep, axis=0), jnp.repeat(kh, rep, axis=0)

    decay = jnp.where(incl, jnp.exp(jnp.where(incl, gc - gr, 0.0)), 0.0)
    lm = jnp.where(strict, bc * kk * decay, 0.0)
    bmm = lambda a, b: jnp.einsum('hij,hjk->hik', a.astype(BF16), b.astype(BF16), preferred_element_type=F32)
    blk = lambda n: (row // n) == (col // n)
    l0 = jnp.where(blk(16), lm, 0.0)
    p = bmm(l0, l0)
    x = eye - l0
    for _ in range(2):
        xp = bmm(jnp.concatenate([x, p], axis=1), p)
        x = x + xp[:, :cs]
        p = xp[:, cs:]
    x = x + bmm(x, p)
    n = 32
    while n <= cs:
        off = jnp.where(blk(n) & ~blk(n // 2), lm, 0.0)
        x = x - bmm(bmm(x, off), x)
        n *= 2
    eg = jnp.exp(gc)
    uw = bmm(x, jnp.concatenate([vh * bc, kv * (bc * eg)], axis=2))
    u = uw[:, :, :GDN_DV]
    w = uw[:, :, GDN_DV:]
    qg = qv * eg
    intra = jnp.where(incl, qk * decay, 0.0)
    kt = kv * jnp.exp(ge - gc)
    s = s_ref[...]
    wq = bmm(jnp.concatenate([w, qg], axis=1), s)
    v_new = u - wq[:, :cs]
    o = wq[:, cs:] + bmm(intra, v_new)
    for b in range(nb):
        for h in heads:
            cols = slice(h * GDN_DV, (h + 1) * GDN_DV)
            oh = o[b * GDN_HV + h]
            if add_ref is not None:
                oh = oh + add_ref[b, :, cols]
            o_ref[b, :, cols] = oh.astype(o_ref.dtype)
    s_ref[...] = s * jnp.exp(ge) + jnp.einsum('hck,hcv->hkv', kt.astype(BF16), v_new.astype(BF16),
                                              preferred_element_type=F32)

    @pl.when(c == n_c - 1)
    def _():
        sfin_ref[...] = s_ref[...].reshape(sfin_ref.shape)


def _gdn_scan(q, k, v, gb, s0, reverse, add=None, out_dtype=F32):
    b, t, _ = q.shape
    assert t % GDN_CHUNK == 0 and GDN_CHUNK % 32 == 0
    n_c = t // GDN_CHUNK
    cm = (lambda ci: (0, n_c - 1 - ci, 0)) if reverse else (lambda ci: (0, ci, 0))
    d = 1 if reverse else 0
    smap = lambda ci: (0, 0, 0, 0)
    in_specs = [pl.BlockSpec((b, GDN_CHUNK, GDN_QK_W), cm), pl.BlockSpec((b, GDN_CHUNK, GDN_QK_W), cm),
                pl.BlockSpec((b, GDN_CHUNK, GDN_V_W), cm), pl.BlockSpec((b, GDN_CHUNK, LANES), cm),
                pl.BlockSpec((b, GDN_HV, GDN_DK, GDN_DV), smap)]
    args = [q, k, v, gb, s0]
    if add is not None:
        in_specs.append(pl.BlockSpec((b, GDN_CHUNK, GDN_V_W), cm))
        args.append(add)
    return pl.pallas_call(
        functools.partial(_gdn_chunk_kernel, reverse=reverse, g_lane=d * GDN_HV, b_lane=(2 + d) * GDN_HV),
        grid=(n_c,),
        in_specs=in_specs,
        out_specs=[pl.BlockSpec((b, GDN_CHUNK, GDN_V_W), cm), pl.BlockSpec((b, GDN_HV, GDN_DK, GDN_DV), smap)],
        out_shape=[jax.ShapeDtypeStruct((b, t, GDN_V_W), out_dtype),
                   jax.ShapeDtypeStruct((b, GDN_HV, GDN_DK, GDN_DV), F32)],
        scratch_shapes=[pltpu.VMEM((b * GDN_HV, GDN_DK, GDN_DV), F32)],
        compiler_params=_CP,
        name="gdn_scan_bwd" if reverse else "gdn_scan_fwd",
    )(*args)


def _odd_out_kernel(o_ref, z_ref, ng_ref, w_ref, x_ref, gate_ref, g_ref, shift_ref, scale_ref,
                    wr_ref, br_ref, xo_ref, h_ref, idx_ref, gates_ref):
    parts = []
    for h in range(GDN_HV):
        cols = slice(h * GDN_DV, (h + 1) * GDN_DV)
        o = o_ref[:, cols].astype(F32)
        z = z_ref[:, cols].astype(F32)
        o = o * lax.rsqrt(jnp.mean(o * o, axis=-1, keepdims=True) + NORM_EPS) * ng_ref[...]
        parts.append((o * (z * jax.nn.sigmoid(z))).astype(BF16))
    y = jnp.dot(jnp.concatenate(parts, axis=1), w_ref[...], preferred_element_type=F32)
    _residual_norm_route(y, x_ref, gate_ref, g_ref, shift_ref, scale_ref, wr_ref, br_ref,
                         xo_ref, h_ref, idx_ref, gates_ref)


def _odd_out_proj(o, z, norm_g, w, x2d, gate, g, shift, scale, w_router, b_router, rows_per_vec, tm):
    t, d = x2d.shape
    wr, br = _router_operands(w_router, b_router)
    vec = _vec(d, rows_per_vec, tm)
    out_specs, out_shape = _route_out(t, d, tm)
    return pl.pallas_call(
        _odd_out_kernel,
        grid=(t // tm,),
        in_specs=[_row(tm, GDN_V_W), _row(tm, GDN_V_W),
                  _full((1, GDN_DV)), _full(w.shape), _row(tm, d), vec, _full((1, d)), vec, vec,
                  _full((d, LANES)), _full((1, LANES))],
        out_specs=out_specs, out_shape=out_shape,
        compiler_params=_CP,
        name="odd_out_proj",
    )(o, z, norm_g.reshape(1, GDN_DV), w, x2d, gate, g.reshape(1, d), shift, scale, wr, br)


def _split_mod(mv, bsz):
    d = mv.shape[1] // N_MOD
    lat = [mv[:bsz, j * d:(j + 1) * d][:, None, :] for j in range(N_MOD)]
    ctx = [mv[bsz:bsz + 1, j * d:(j + 1) * d][:, None, :] for j in range(N_MOD)]
    return lat, ctx


def _even_layer(x2d, xc2d, mod, mod_c, norm1_g, norm2_g, w_in, w_out, lam_p, subln_g, lam_init, cos, sin,
                w_router, b_router, bsz, n_lat, n_ctx):
    d = x2d.shape[1]
    w_in_b = w_in.astype(BF16)
    w_out_b = w_out.astype(BF16)
    q_scale = DA_DH ** -0.5 * math.log2(math.e)
    f, q, k_all, v_all = _even_in_proj(x2d, norm1_g, mod[0], mod[1], w_in_b, cos, sin, bsz, n_lat, n_lat + n_ctx,
                                       q_scale, TM_PROJ)
    ones, zeros = jnp.ones((n_ctx, ROPE_AXIS_DIM), F32), jnp.zeros((n_ctx, ROPE_AXIS_DIM), F32)
    fc, qc, kc, vc = _even_in_proj(xc2d, norm1_g, mod_c[0], mod_c[1], w_in_b, ones, zeros, bsz, n_ctx, n_ctx,
                                   q_scale, n_ctx)
    lp = lam_p.astype(F32)
    lam = jnp.exp(jnp.sum(lp[0] * lp[1])) - jnp.exp(jnp.sum(lp[2] * lp[3])) + lam_init
    k_all = lax.dynamic_update_slice(k_all, kc, (0, n_lat, 0))
    v_all = lax.dynamic_update_slice(v_all, vc, (0, n_lat, 0))
    o = _diff_attention(lam, q, k_all, v_all, subln_g, 1.0 - lam_init, TQ_ATTN, TK_ATTN)
    oc = _diff_attention(lam, qc, kc, vc, subln_g, 1.0 - lam_init, n_ctx, n_ctx)
    fm = _fourier_mix(f.reshape(bsz, n_lat, FOURIER_W)).reshape(bsz * n_lat, FOURIER_W)
    fmc = _fourier_mix(fc.reshape(bsz, n_ctx, FOURIER_W)).reshape(bsz * n_ctx, FOURIER_W)
    lat = _even_out_proj(fm, o.reshape(bsz * n_lat, DA_W), w_out_b, x2d, mod[2], norm2_g, mod[3], mod[4],
                         w_router, b_router, n_lat, TM_PROJ)
    ctx = _even_out_proj(fmc, oc.reshape(bsz * n_ctx, DA_W), w_out_b, xc2d, mod_c[2], norm2_g, mod_c[3], mod_c[4],
                         w_router, b_router, bsz * n_ctx, n_ctx)
    return lat, ctx


def _odd_layer(x2d, xc2d, mod, mod_c, norm1_g, norm2_g, w_in, conv_w, a_log, dt_bias, norm_g, w_out,
               w_router, b_router, bsz, n_lat, n_ctx):
    d = x2d.shape[1]
    q, k, v, z, gb = _odd_in_stage(x2d.reshape(bsz, n_lat, d), norm1_g, mod[0], mod[1], w_in, conv_w, a_log,
                                   dt_bias, TM_PROJ)
    qc, kc, vc, _, gbc = _odd_in_stage(xc2d.reshape(bsz, n_ctx, d), norm1_g, mod_c[0], mod_c[1], w_in, conv_w,
                                       a_log, dt_bias, n_ctx)
    s0 = jnp.zeros((bsz, GDN_HV, GDN_DK, GDN_DV), F32)
    _, sc_f = _gdn_scan(qc, kc, vc, gbc, s0, False)
    o_f, _ = _gdn_scan(q, k, v, gb, sc_f, False)
    _, sc_b = _gdn_scan(qc, kc, vc, gbc, s0, True)
    o, _ = _gdn_scan(q, k, v, gb, sc_b, True, add=o_f, out_dtype=BF16)
    return _odd_out_proj(o.reshape(bsz * n_lat, GDN_V_W), z.reshape(bsz * n_lat, GDN_V_W), norm_g,
                         w_out.astype(BF16), x2d, mod[2], norm2_g, mod[3], mod[4], w_router, b_router, n_lat,
                         TM_PROJ)


def kernel(x, c, ctx, c_ctx, norm1_g, norm2_g, w_mod, b_mod, ev_w_in, ev_w_out, ev_lam, ev_subln_g,
           od_w_in, od_conv_w, od_a_log, od_dt_bias, od_norm_g, od_w_out,
           moe_w_router, moe_b_router, moe_w1, moe_b1, moe_w2, moe_b2, final_g):
    bsz, n_lat, d = x.shape
    n_ctx = ctx.shape[1]
    assert w_mod.shape[0] == 2, "kernel is written for one even (attention) and one odd (DeltaNet) layer"
    t_lat = bsz * n_lat
    cos, sin = _axial_rope_tables(n_lat // GRID_W)
    c_rows = jnp.zeros((8, d), F32).at[:bsz].set(c).at[bsz].set(c_ctx)
    x2d = x.reshape(t_lat, d)
    xc2d = ctx.reshape(bsz * n_ctx, d)

    mod, mod_c = _split_mod(_mod_vectors(c_rows, w_mod, b_mod, 0), bsz)
    (x2d, h2, idx, gates), (xc2d, h2c, idx_c, gates_c) = _even_layer(
        x2d, xc2d, mod, mod_c, norm1_g[0], norm2_g[0], ev_w_in[0], ev_w_out[0], ev_lam[0], ev_subln_g[0],
        _diff_lambda_init(0), cos, sin, moe_w_router[0], moe_b_router[0], bsz, n_lat, n_ctx)
    yb, pos = _moe_ffn(jnp.concatenate([h2, h2c], axis=0), jnp.concatenate([idx, idx_c], axis=0),
                       moe_w1, moe_b1[0], moe_w2, moe_b2[0], 0, TM_PROJ)
    x2d = _moe_combine(yb, pos[:t_lat], gates, x2d, mod[5], n_lat, TM_PROJ)
    xc2d = _moe_combine(yb, pos[t_lat:], gates_c, xc2d, mod_c[5], bsz * n_ctx, n_ctx)

    mod, mod_c = _split_mod(_mod_vectors(c_rows, w_mod, b_mod, 1), bsz)
    x2d, h2, idx, gates = _odd_layer(x2d, xc2d, mod, mod_c, norm1_g[1], norm2_g[1], od_w_in[0], od_conv_w[0],
                                     od_a_log[0], od_dt_bias[0], od_norm_g[0], od_w_out[0],
                                     moe_w_router[1], moe_b_router[1], bsz, n_lat, n_ctx)
    yb, pos = _moe_ffn(h2, idx, moe_w1, moe_b1[1], moe_w2, moe_b2[1], 1, TM_PROJ)
    return _moe_combine(yb, pos, gates, x2d, mod[5], n_lat, TM_PROJ, final_g=final_g).reshape(bsz, n_lat, d)
```

```python
import functools
import math

import jax
import jax.numpy as jnp
from jax import lax
from jax.experimental import pallas as pl
from jax.experimental.pallas import tpu as pltpu

D_MODEL = 1024
N_MOD = 6
NORM_EPS = 1e-6
GRID_W = 64

FOURIER_GROUPS = 4
FOURIER_GD = 64
FOURIER_W = FOURIER_GROUPS * FOURIER_GD
DA_HEADS = 6
DA_DH = 64
DA_VD = 2 * DA_DH
DA_W = DA_HEADS * DA_VD
ROPE_BASE = 10000.0
ROPE_AXIS_DIM = DA_DH // 2
SUBLN_EPS = 1e-5

GDN_HK = 8
GDN_HV = 16
GDN_DK = 128
GDN_DV = 128
GDN_QK_W = GDN_HK * GDN_DK
GDN_V_W = GDN_HV * GDN_DV
GDN_QKV_W = 2 * GDN_QK_W + GDN_V_W
GDN_MAIN_W = GDN_QKV_W + GDN_V_W
GDN_CONV = 5
GDN_CHUNK = 64

N_EXPERTS = 32
TOP_K = 4
D_FF = 1024
SWIGLU_LIMIT = 7.0
SWIGLU_ALPHA = 1.702
MOE_BLOCK = 512
MOE_FIRST_PART = 32

LANES = 128
SUBLANES_BF16 = 16
MXU_DIM = 256
VMEM_LIMIT = 56 * 1024 * 1024
BF16 = jnp.bfloat16
F32 = jnp.float32
HI = lax.Precision.HIGHEST

TM_PROJ = 512
TQ_ATTN, TK_ATTN = 512, 1280

_CP = pltpu.CompilerParams(vmem_limit_bytes=VMEM_LIMIT)


def _diff_lambda_init(layer_idx):
    return 0.8 - 0.6 * math.exp(-0.3 * layer_idx)


def _axial_rope_tables(rows):
    t = jnp.arange(rows * GRID_W, dtype=jnp.int32)
    row = (t // GRID_W).astype(F32)
    col = (t % GRID_W).astype(F32)
    inv = ROPE_BASE ** (-jnp.arange(0, ROPE_AXIS_DIM, 2, dtype=F32) / ROPE_AXIS_DIM)
    ang = jnp.concatenate([row[:, None] * inv, col[:, None] * inv], axis=-1)
    return jnp.cos(ang), jnp.sin(ang)


def _norm_mod(x, g_ref, shift_ref, scale_ref):
    h = x * lax.rsqrt(jnp.mean(x * x, axis=-1, keepdims=True) + NORM_EPS) * g_ref[...]
    return (h * (1.0 + scale_ref[0]) + shift_ref[0]).astype(BF16)


def _route(logits):
    lane = lax.broadcasted_iota(jnp.int32, logits.shape, 1)
    lane_f = lane.astype(F32)
    neg = jnp.float32(-jnp.inf)
    rest = jnp.where(lane < N_EXPERTS, logits, neg)
    idx = jnp.zeros(logits.shape, F32)
    val = jnp.full(logits.shape, neg, F32)
    for j in range(TOP_K):
        m = jnp.max(rest, axis=-1, keepdims=True)
        sel = jnp.min(jnp.where(rest == m, lane_f, float(LANES)), axis=-1, keepdims=True)
        idx = jnp.where(lane == j, sel, idx)
        val = jnp.where(lane == j, m, val)
        rest = jnp.where(lane_f == sel, neg, rest)
    e = jnp.exp(val - val[:, 0:1])
    return idx, e / jnp.sum(e, axis=-1, keepdims=True)


def _residual_norm_route(y, x_ref, gate_ref, g_ref, shift_ref, scale_ref, wr_ref, br_ref,
                         xo_ref, h_ref, idx_ref, gates_ref):
    xn = x_ref[...] + gate_ref[0] * y
    xo_ref[...] = xn
    h = _norm_mod(xn, g_ref, shift_ref, scale_ref)
    h_ref[...] = h
    idx, gates = _route(jnp.dot(h, wr_ref[...], preferred_element_type=F32) + br_ref[...])
    idx_ref[...] = idx.astype(jnp.int32)
    gates_ref[...] = gates


def _row(tm, n):
    return pl.BlockSpec((tm, n), lambda i: (i, 0))


def _full(shape):
    return pl.BlockSpec(shape, lambda i: (0,) * len(shape))


def _vec(d, rows_per_vec, tm):
    per = rows_per_vec // tm
    return pl.BlockSpec((1, 1, d), lambda i: (i // per, 0, 0))


def _router_operands(w_router, b_router):
    d = w_router.shape[0]
    wr = jnp.zeros((d, LANES), BF16).at[:, :N_EXPERTS].set(w_router.astype(BF16))
    br = jnp.zeros((1, LANES), F32).at[0, :N_EXPERTS].set(b_router)
    return wr, br


def _route_out(t, d, tm):
    specs = [_row(tm, d), _row(tm, d), _row(tm, LANES), _row(tm, LANES)]
    shapes = [jax.ShapeDtypeStruct((t, d), F32), jax.ShapeDtypeStruct((t, d), BF16),
              jax.ShapeDtypeStruct((t, LANES), jnp.int32), jax.ShapeDtypeStruct((t, LANES), F32)]
    return specs, shapes


def _mod_kernel(c_ref, w_ref, b_ref, o_ref):
    c = c_ref[...]
    s = (c * jax.nn.sigmoid(c)).astype(BF16)
    o_ref[...] = jnp.dot(s, w_ref[0].astype(BF16), preferred_element_type=F32) + b_ref[0]


def _mod_vectors(c_rows, w_mod, b_mod, layer):
    d = c_rows.shape[1]
    return pl.pallas_call(
        _mod_kernel,
        grid=(N_MOD,),
        in_specs=[pl.BlockSpec(c_rows.shape, lambda n: (0, 0)),
                  pl.BlockSpec((1, d, d), lambda n: (layer, 0, n)),
                  pl.BlockSpec((1, 1, d), lambda n: (layer, 0, n))],
        out_specs=pl.BlockSpec((c_rows.shape[0], d), lambda n: (0, n)),
        out_shape=jax.ShapeDtypeStruct((c_rows.shape[0], N_MOD * d), F32),
        compiler_params=_CP,
        name="mod_vectors",
    )(c_rows, w_mod, b_mod.reshape(b_mod.shape[0], 1, -1))


def _even_in_kernel(x_ref, g_ref, shift_ref, scale_ref, w_ref, rc_ref, rs_ref, f_ref, q_ref, k_ref, v_ref, *, q_scale):
    h = _norm_mod(x_ref[...], g_ref, shift_ref, scale_ref)
    p = jnp.dot(h, w_ref[...], preferred_element_type=F32)
    f_ref[...] = p[:, :FOURIER_W]
    rc, rs = rc_ref[...], rs_ref[...]
    lane = lax.broadcasted_iota(jnp.int32, rc.shape, 1)
    first = (lane % DA_DH) < ROPE_AXIS_DIM

    def rope(t):
        partner = jnp.where(first, pltpu.roll(t, LANES - ROPE_AXIS_DIM, 1), pltpu.roll(t, ROPE_AXIS_DIM, 1))
        return t * rc + partner * rs

    for hd in range(DA_HEADS):
        cq = slice(FOURIER_W + hd * DA_VD, FOURIER_W + (hd + 1) * DA_VD)
        ck = slice(FOURIER_W + DA_W + hd * DA_VD, FOURIER_W + DA_W + (hd + 1) * DA_VD)
        q_ref[0, :, hd * DA_VD:(hd + 1) * DA_VD] = (rope(p[:, cq]) * q_scale).astype(BF16)
        k_ref[0, :, hd * DA_VD:(hd + 1) * DA_VD] = rope(p[:, ck]).astype(BF16)
    v_ref[0] = p[:, FOURIER_W + 2 * DA_W:].astype(BF16)


def _even_in_proj(x2d, g, shift, scale, w, cos, sin, bsz, n_seq, n_keys, q_scale, tm):
    t, d = x2d.shape
    bpb = n_seq // tm
    rc = jnp.tile(cos, (1, LANES // ROPE_AXIS_DIM))
    rs = jnp.tile(jnp.concatenate([-sin, sin], axis=1), (1, LANES // DA_DH))
    tab = pl.BlockSpec((tm, LANES), lambda i: (i % bpb, 0))
    seq = pl.BlockSpec((1, tm, DA_W), lambda i: (i // bpb, i % bpb, 0))
    vec = _vec(d, n_seq * (bsz // shift.shape[0]), tm)
    return pl.pallas_call(
        functools.partial(_even_in_kernel, q_scale=q_scale),
        grid=(t // tm,),
        in_specs=[_row(tm, d), _full((1, d)), vec, vec, _full(w.shape), tab, tab],
        out_specs=[_row(tm, FOURIER_W), seq, seq, seq],
        out_shape=[jax.ShapeDtypeStruct((t, FOURIER_W), F32), jax.ShapeDtypeStruct((bsz, n_seq, DA_W), BF16),
                   jax.ShapeDtypeStruct((bsz, n_keys, DA_W), BF16), jax.ShapeDtypeStruct((bsz, n_keys, DA_W), BF16)],
        compiler_params=_CP,
        name="even_in_proj",
    )(x2d, g.reshape(1, d), shift, scale, w, rc, rs)


def _diff_attn_kernel(lam_ref, q_ref, k_ref, v_ref, g_ref, o_ref, qs_ref, s_ref, m_ref, acc_ref, *,
                      tk, n_sub, out_scale):
    tq = q_ref.shape[1]
    n_kv = k_ref.shape[1] // tk
    rb = 2 * tq // n_sub
    q = q_ref[0]
    lane = lax.broadcasted_iota(jnp.int32, q.shape, 1)
    zero = jnp.zeros_like(q)
    qs_ref[:tq] = jnp.where(lane < DA_DH, q, zero)
    qs_ref[tq:] = jnp.where(lane >= DA_DH, q, zero)
    m_ref[...] = jnp.full(m_ref.shape, -1e30, F32)
    acc_ref[...] = jnp.zeros(acc_ref.shape, F32)
    ones = jnp.ones((tk, LANES), BF16)

    def scores(i, slot):
        off = pl.multiple_of(i * tk, tk)
        k = k_ref[0, pl.ds(off, tk), :]
        s_ref[slot] = lax.dot_general(qs_ref[...], k, (((1,), (1,)), ((), ())), preferred_element_type=F32)

    def consume(i, slot):
        off = pl.multiple_of(i * tk, tk)
        v_ext = jnp.concatenate([v_ref[0, pl.ds(off, tk), :], ones], axis=1)
        for r in range(n_sub):
            rows = pl.ds(r * rb, rb)
            s = s_ref[slot, rows, :]
            m_prev = m_ref[rows, :]
            m_new = jnp.maximum(m_prev, jnp.max(s, axis=1, keepdims=True))
            alpha = jnp.exp2(m_prev - m_new)
            p = jnp.exp2(s - jnp.tile(m_new, (1, tk // LANES)))
            pv = jnp.dot(p.astype(BF16), v_ext, preferred_element_type=F32)
            acc_ref[rows, :] = acc_ref[rows, :] * jnp.tile(alpha, (1, 2)) + pv
            m_ref[rows, :] = m_new

    scores(0, 0)

    def body(j, carry):
        scores(2 * j + 1, 1)
        consume(2 * j, 0)
        scores(2 * j + 2, 0)
        consume(2 * j + 1, 1)
        return carry

    lax.fori_loop(0, (n_kv - 1) // 2, body, 0)
    consume(n_kv - 1, 0)
    acc = acc_ref[...]
    o1 = acc[:tq, :LANES] / acc[:tq, LANES:]
    o2 = acc[tq:, :LANES] / acc[tq:, LANES:]
    o = o1 - lam_ref[0] * o2
    ms = jnp.mean(o * o, axis=-1, keepdims=True)
    o = o * lax.rsqrt(ms + SUBLN_EPS) * g_ref[...] * out_scale
    o_ref[0] = o.astype(o_ref.dtype)


def _diff_attention(lam, q, k_all, v_all, subln_g, out_scale, tq, tk, n_sub=2):
    b, n, _ = q.shape
    nk = k_all.shape[1]
    assert n % tq == 0 and nk % tk == 0 and tk % MXU_DIM == 0 and (nk // tk) % 2 == 1
    grid_spec = pltpu.PrefetchScalarGridSpec(
        num_scalar_prefetch=1,
        grid=(b, DA_HEADS, n // tq),
        in_specs=[pl.BlockSpec((1, tq, DA_VD), lambda bi, hi, qi, lam_r: (bi, qi, hi)),
                  pl.BlockSpec((1, nk, DA_VD), lambda bi, hi, qi, lam_r: (bi, 0, hi)),
                  pl.BlockSpec((1, nk, DA_VD), lambda bi, hi, qi, lam_r: (bi, 0, hi)),
                  pl.BlockSpec((1, DA_VD), lambda bi, hi, qi, lam_r: (0, 0))],
        out_specs=pl.BlockSpec((1, tq, DA_VD), lambda bi, hi, qi, lam_r: (bi, qi, hi)),
        scratch_shapes=[pltpu.VMEM((2 * tq, LANES), BF16), pltpu.VMEM((2, 2 * tq, tk), F32),
                        pltpu.VMEM((2 * tq, LANES), F32), pltpu.VMEM((2 * tq, 2 * LANES), F32)],
    )
    return pl.pallas_call(
        functools.partial(_diff_attn_kernel, tk=tk, n_sub=n_sub, out_scale=out_scale),
        grid_spec=grid_spec,
        out_shape=jax.ShapeDtypeStruct((b, n, DA_W), BF16),
        compiler_params=_CP,
        name="diff_attention",
    )(lam.reshape(1), q, k_all, v_all, subln_g.reshape(1, DA_VD))


def _fft_stage1_kernel(x_ref, fch_ref, f1_ref, tc_ref, ts_ref, yr_ref, yi_ref, *, nb):
    w = FOURIER_W
    r = x_ref.shape[1]
    for bl in range(nb):
        cols = slice(bl * w, (bl + 1) * w)
        ab = jnp.dot(x_ref[0, :, cols], fch_ref[...], preferred_element_type=F32, precision=HI)
        z = jnp.concatenate([ab[:, :w], ab[:, w:]], axis=0)
        y = jnp.dot(f1_ref[...], z, preferred_element_type=F32, precision=HI)
        yr, yi = y[:r], y[r:]
        tc, ts = tc_ref[:, cols], ts_ref[:, cols]
        yr_ref[0, :, cols] = yr * tc + yi * ts
        yi_ref[0, :, cols] = yi * tc - yr * ts


def _fft_stage2_kernel(yr_ref, yi_ref, f2_ref, o_ref, *, nc):
    w = FOURIER_W
    for cl in range(nc):
        y = jnp.concatenate([yr_ref[0, cl], yi_ref[0, cl]], axis=0)
        o_ref[0, :, cl * w:(cl + 1) * w] = jnp.dot(f2_ref[...], y, preferred_element_type=F32, precision=HI)


def _fourier_mix(f, nb=8):
    bsz, n, w = f.shape
    r = math.isqrt(n)
    assert r * r == n and r % nb == 0 and w == FOURIER_W
    two_pi = 2.0 * math.pi
    k = jnp.arange(r, dtype=jnp.int32)
    ang = two_pi * ((k[:, None] * k[None, :]) % r).astype(F32) / r
    c1, s1 = jnp.cos(ang), jnp.sin(ang)
    kc = jnp.arange(FOURIER_GD, dtype=jnp.int32)
    angc = two_pi * ((kc[:, None] * kc[None, :]) % FOURIER_GD).astype(F32) / FOURIER_GD
    eye_g = jnp.eye(FOURIER_GROUPS, dtype=F32)
    fch = jnp.concatenate([jnp.kron(eye_g, jnp.cos(angc)), jnp.kron(eye_g, jnp.sin(angc))], axis=1)
    f1 = jnp.concatenate([jnp.concatenate([c1, -s1], axis=1), jnp.concatenate([-s1, -c1], axis=1)], axis=0)
    f2 = jnp.concatenate([c1, s1], axis=1) * (1.0 / math.sqrt(n * FOURIER_GD))
    angt = two_pi * ((k[:, None] * k[None, :]) % n).astype(F32) / n
    tc = jnp.repeat(jnp.cos(angt), w, axis=1)
    ts = jnp.repeat(jnp.sin(angt), w, axis=1)
    xv = f.reshape(bsz, r, r * w)
    blk = pl.BlockSpec((1, r, nb * w), lambda bi, j: (bi, 0, j))
    tab = pl.BlockSpec((r, nb * w), lambda bi, j: (0, j))
    full = lambda a: pl.BlockSpec(a.shape, lambda bi, j: (0, 0))
    yshape = jax.ShapeDtypeStruct((bsz, r, r * w), F32)
    yr, yi = pl.pallas_call(
        functools.partial(_fft_stage1_kernel, nb=nb),
        grid=(bsz, r // nb),
        in_specs=[blk, full(fch), full(f1), tab, tab],
        out_specs=[blk, blk],
        out_shape=[yshape, yshape],
        compiler_params=_CP,
        name="fft_stage1",
    )(xv, fch, f1, tc, ts)
    y4 = lambda a: a.reshape(bsz, r, r, w)
    yblk = pl.BlockSpec((1, nb, r, w), lambda bi, j: (bi, j, 0, 0))
    out = pl.pallas_call(
        functools.partial(_fft_stage2_kernel, nc=nb),
        grid=(bsz, r // nb),
        in_specs=[yblk, yblk, full(f2)],
        out_specs=blk,
        out_shape=yshape,
        compiler_params=_CP,
        name="fft_stage2",
    )(y4(yr), y4(yi), f2)
    return out.reshape(bsz, n, w)


def _even_out_kernel(fm_ref, o_ref, w_ref, x_ref, gate_ref, g_ref, shift_ref, scale_ref, wr_ref, br_ref,
                     xo_ref, h_ref, idx_ref, gates_ref):
    y = (jnp.dot(fm_ref[...].astype(BF16), w_ref[:FOURIER_W], preferred_element_type=F32)
         + jnp.dot(o_ref[...], w_ref[FOURIER_W:], preferred_element_type=F32))
    _residual_norm_route(y, x_ref, gate_ref, g_ref, shift_ref, scale_ref, wr_ref, br_ref,
                         xo_ref, h_ref, idx_ref, gates_ref)


def _even_out_proj(fm, o, w, x2d, gate, g, shift, scale, w_router, b_router, rows_per_vec, tm):
    t, d = x2d.shape
    wr, br = _router_operands(w_router, b_router)
    vec = _vec(d, rows_per_vec, tm)
    out_specs, out_shape = _route_out(t, d, tm)
    return pl.pallas_call(
        _even_out_kernel,
        grid=(t // tm,),
        in_specs=[_row(tm, FOURIER_W), _row(tm, DA_W), _full(w.shape), _row(tm, d), vec, _full((1, d)), vec, vec,
                  _full((d, LANES)), _full((1, LANES))],
        out_specs=out_specs, out_shape=out_shape,
        compiler_params=_CP,
        name="even_out_proj",
    )(fm, o, w, x2d, gate, g.reshape(1, d), shift, scale, wr, br)


def _moe_kernel(be_ref, x_ref, w1_ref, perm_ref, b1g_ref, b1l_ref, w2_ref, b2_ref, *rest):
    o_ref, w1g_s, w1l_s, w2_s = rest[-4:]
    i = pl.program_id(0)

    @pl.when((i == 0) | (be_ref[i] != be_ref[jnp.maximum(i - 1, 0)]))
    def _():
        for c in range(w1_ref.shape[3] // MXU_DIM):
            blk = jnp.dot(w1_ref[0, 0, :, c * MXU_DIM:(c + 1) * MXU_DIM].astype(BF16), perm_ref[...],
                          preferred_element_type=F32)
            w1g_s[:, c * LANES:(c + 1) * LANES] = blk[:, :LANES].astype(BF16)
            w1l_s[:, c * LANES:(c + 1) * LANES] = blk[:, LANES:].astype(BF16)
        w2_s[...] = w2_ref[0, 0].astype(BF16)

    x = x_ref[...]
    ug = jnp.dot(x, w1g_s[...], preferred_element_type=F32) + b1g_ref[0]
    ul = jnp.dot(x, w1l_s[...], preferred_element_type=F32) + b1l_ref[0]
    glu = jnp.minimum(ug, SWIGLU_LIMIT)
    lin = jnp.clip(ul, -SWIGLU_LIMIT, SWIGLU_LIMIT)
    act = glu * jax.nn.sigmoid(SWIGLU_ALPHA * glu) * (lin + 1.0)
    y = jnp.dot(act.astype(BF16), w2_s[...], preferred_element_type=F32) + b2_ref[0]
    o_ref[...] = y.astype(o_ref.dtype)


def _moe_experts(block_expert, xb, w1_all, b1g, b1l, w2_all, layer, b2, yb_prev, first_block, n_blocks_total):
    n_rows, d = xb.shape
    n_blocks = n_rows // MOE_BLOCK
    src = jnp.arange(MXU_DIM)[:, None]
    dst = jnp.arange(MXU_DIM)[None, :]
    perm = (src == jnp.where(dst < LANES, 2 * dst, 2 * (dst - LANES) + 1)).astype(BF16)
    bspec = lambda s: pl.BlockSpec((1,) + s, lambda i, be: (be[i], 0, 0))
    wspec = lambda s: pl.BlockSpec((1, 1) + s, lambda i, be: (layer, be[i], 0, 0))
    in_specs = [pl.BlockSpec((MOE_BLOCK, d), lambda i, be: (i, 0)),
                wspec((d, 2 * D_FF)), pl.BlockSpec((MXU_DIM, MXU_DIM), lambda i, be: (0, 0)),
                bspec((1, D_FF)), bspec((1, D_FF)), wspec((D_FF, d)), bspec((1, d))]
    args = [block_expert, xb, w1_all, perm, b1g, b1l, w2_all, b2]
    aliases = {}
    if yb_prev is not None:
        in_specs.append(pl.BlockSpec(memory_space=pl.ANY))
        aliases = {len(args): 0}
        args.append(yb_prev)
    grid_spec = pltpu.PrefetchScalarGridSpec(
        num_scalar_prefetch=1,
        grid=(n_blocks,),
        in_specs=in_specs,
        out_specs=pl.BlockSpec((MOE_BLOCK, d), lambda i, be: (first_block + i, 0)),
        scratch_shapes=[pltpu.VMEM((d, D_FF), BF16), pltpu.VMEM((d, D_FF), BF16), pltpu.VMEM((D_FF, d), BF16)],
    )
    return pl.pallas_call(
        _moe_kernel,
        grid_spec=grid_spec,
        out_shape=jax.ShapeDtypeStruct((n_blocks_total * MOE_BLOCK, d), BF16),
        input_output_aliases=aliases,
        compiler_params=_CP,
        name="moe_experts",
    )(*args)


def _rank_kernel(idx_ref, rank_ref, cnt_ref, carry_ref):
    i = pl.program_id(0)

    @pl.when(i == 0)
    def _():
        carry_ref[...] = jnp.zeros(carry_ref.shape, F32)

    idx = idx_ref[...]
    tm = idx.shape[0]
    lane = lax.broadcasted_iota(jnp.int32, idx.shape, 1)
    sel = [lane == idx[:, j:j + 1] for j in range(TOP_K)]
    onehot = sel[0].astype(F32)
    for j in range(1, TOP_K):
        onehot = onehot + sel[j].astype(F32)
    row = lax.broadcasted_iota(jnp.int32, (tm, tm), 0)
    col = lax.broadcasted_iota(jnp.int32, (tm, tm), 1)
    before = (row > col).astype(BF16)
    prefix = jnp.dot(before, onehot.astype(BF16), preferred_element_type=F32) + carry_ref[0:1, :]
    rank = jnp.zeros(idx.shape, F32)
    for j in range(TOP_K):
        rank = jnp.where(lane == j, jnp.sum(jnp.where(sel[j], prefix, 0.0), axis=-1, keepdims=True), rank)
    rank_ref[...] = rank.astype(jnp.int32)
    carry_ref[0:1, :] = carry_ref[0:1, :] + jnp.sum(onehot, axis=0, keepdims=True)
    cnt_ref[...] = carry_ref[...]


def _expert_ranks(idx, tm):
    t = idx.shape[0]
    rank, cnt = pl.pallas_call(
        _rank_kernel,
        grid=(t // tm,),
        in_specs=[_row(tm, LANES)],
        out_specs=[_row(tm, LANES), _full((8, LANES))],
        out_shape=[jax.ShapeDtypeStruct((t, LANES), jnp.int32), jax.ShapeDtypeStruct((8, LANES), F32)],
        scratch_shapes=[pltpu.VMEM((8, LANES), F32)],
        compiler_params=_CP,
        name="expert_ranks",
    )(idx)
    return rank[:, :TOP_K], cnt[0, :N_EXPERTS].astype(jnp.int32)


def _moe_parts(n_blocks):
    parts, first, size = [], 0, max(1, n_blocks // MOE_FIRST_PART)
    while first < n_blocks:
        if n_blocks - first < 2 * size:
            size = n_blocks - first
        parts.append((first, size))
        first += size
        size *= 2
    return parts


def _moe_ffn(h, idx, w1_all, b1, w2_all, b2, layer, tm):
    n_tok, d = h.shape
    n_assign = n_tok * TOP_K
    e_flat = idx[:, :TOP_K].reshape(n_assign)
    order = jnp.argsort(e_flat).astype(jnp.int32)
    rank, counts = _expert_ranks(idx, tm)
    starts = jnp.cumsum(counts) - counts
    padded = (counts + MOE_BLOCK - 1) // MOE_BLOCK * MOE_BLOCK
    pad_ends = jnp.cumsum(padded)
    pad_starts = pad_ends - padded
    n_blocks = -(-(n_assign + N_EXPERTS * (MOE_BLOCK - 1)) // MOE_BLOCK)
    n_rows = n_blocks * MOE_BLOCK
    block_start = jnp.arange(n_blocks, dtype=jnp.int32) * MOE_BLOCK
    block_expert = jnp.minimum(jnp.sum(pad_ends[None, :] <= block_start[:, None], axis=1, dtype=jnp.int32),
                               N_EXPERTS - 1)
    shift = pad_starts - starts
    pos = pad_starts[e_flat].reshape(n_tok, TOP_K) + rank
    r = jnp.arange(n_rows, dtype=jnp.int32)
    src = jnp.clip(r - jnp.repeat(shift[block_expert], MOE_BLOCK), 0, n_assign - 1)
    row_tok = order[src] // TOP_K
    yb = None
    for first, size in _moe_parts(n_blocks):
        xb = h[row_tok[first * MOE_BLOCK:(first + size) * MOE_BLOCK]]
        yb = _moe_experts(block_expert[first:first + size], xb, w1_all, b1[:, None, 0::2], b1[:, None, 1::2],
                          w2_all, layer, b2[:, None, :], yb, first, n_blocks)
    return yb, pos


def _combine_kernel(*refs, final):
    y_refs = refs[:TOP_K]
    gt_ref, x_ref, g5_ref = refs[TOP_K:TOP_K + 3]
    o_ref = refs[-1]
    gt = gt_ref[...]
    acc = y_refs[0][...].astype(F32) * gt[:, 0:1]
    for j in range(1, TOP_K):
        acc = acc + y_refs[j][...].astype(F32) * gt[:, j:j + 1]
    xn = x_ref[...] + g5_ref[0] * acc
    if final:
        fg_ref = refs[TOP_K + 3]
        xn = xn * lax.rsqrt(jnp.mean(xn * xn, axis=-1, keepdims=True) + NORM_EPS) * fg_ref[...]
    o_ref[...] = xn


def _moe_combine(yb, pos, gates, x2d, gate5, rows_per_vec, tm, final_g=None):
    t, d = x2d.shape
    ys = [yb[pos[:, j]] for j in range(TOP_K)]
    in_specs = [_row(tm, d)] * TOP_K + [_row(tm, LANES), _row(tm, d), _vec(d, rows_per_vec, tm)]
    args = ys + [gates, x2d, gate5]
    if final_g is not None:
        in_specs.append(_full((1, d)))
        args.append(final_g.reshape(1, d))
    return pl.pallas_call(
        functools.partial(_combine_kernel, final=final_g is not None),
        grid=(t // tm,),
        in_specs=in_specs,
        out_specs=_row(tm, d),
        out_shape=jax.ShapeDtypeStruct((t, d), F32),
        compiler_params=_CP,
        name="moe_combine",
    )(*args)


HALO = SUBLANES_BF16


def _odd_in_kernel(xm_ref, xp_ref, xn_ref, g_ref, shift_ref, scale_ref, wqkv_ref, wz_ref, wab_ref, cw_ref,
                   alog_ref, dtb_ref, q_ref, k_ref, v_ref, z_ref, gb_ref):
    i = pl.program_id(1)
    n_i = pl.num_programs(1)
    tm = xm_ref.shape[1]
    half = GDN_CONV // 2
    keep_prev = (i > 0).astype(BF16)
    keep_next = (i < n_i - 1).astype(BF16)
    h_main = _norm_mod(xm_ref[0], g_ref, shift_ref, scale_ref)
    h_ext = jnp.concatenate([_norm_mod(xp_ref[0], g_ref, shift_ref, scale_ref) * keep_prev, h_main,
                             _norm_mod(xn_ref[0], g_ref, shift_ref, scale_ref) * keep_next], axis=0)
    gw = MXU_DIM
    for cg in range(GDN_QKV_W // gw):
        cols = slice(cg * gw, (cg + 1) * gw)
        ext = jnp.dot(h_ext, wqkv_ref[:, cols], preferred_element_type=F32)
        acc = ext[HALO:HALO + tm] * cw_ref[half:half + 1, cols]
        for j in range(GDN_CONV):
            if j != half:
                sh = pltpu.roll(ext, (half - j) % (tm + 2 * HALO), 0)[HALO:HALO + tm]
                acc = acc + sh * cw_ref[j:j + 1, cols]
        y = acc * jax.nn.sigmoid(acc)
        for sub in range(gw // LANES):
            hd = cg * (gw // LANES) + sub
            yh = y[:, sub * LANES:(sub + 1) * LANES]
            if hd < 2 * GDN_HK:
                yh = yh * lax.rsqrt(jnp.sum(yh * yh, axis=-1, keepdims=True) + 1e-6)
                if hd < GDN_HK:
                    q_ref[0, :, hd * LANES:(hd + 1) * LANES] = yh * (GDN_DK ** -0.5)
                else:
                    k_ref[0, :, (hd - GDN_HK) * LANES:(hd - GDN_HK + 1) * LANES] = yh
            else:
                v_ref[0, :, (hd - 2 * GDN_HK) * LANES:(hd - 2 * GDN_HK + 1) * LANES] = yh
    z_ref[0] = jnp.dot(h_main, wz_ref[...], preferred_element_type=F32).astype(BF16)
    ab = jnp.dot(h_main, wab_ref[...], preferred_element_type=F32)
    xa = ab + dtb_ref[...]
    softplus = jnp.maximum(xa, 0.0) + jnp.log1p(jnp.exp(-jnp.abs(xa)))
    lane = lax.broadcasted_iota(jnp.int32, ab.shape, 1)
    gb_ref[0] = jnp.where(lane < 2 * GDN_HV, -jnp.exp(alog_ref[...]) * softplus, jax.nn.sigmoid(ab))


def _odd_in_stage(x3d, g, shift, scale, w_in, conv_w, a_log, dt_bias, tm):
    b, t, d = x3d.shape
    w_qkv = w_in[:, :GDN_QKV_W].astype(BF16)
    w_z = w_in[:, GDN_QKV_W:GDN_MAIN_W].astype(BF16)
    w_ab = jnp.zeros((d, LANES), BF16).at[:, :4 * GDN_HV].set(w_in[:, GDN_MAIN_W:].astype(BF16))
    nb = tm // HALO
    last = t // HALO - 1
    pad = lambda a: jnp.zeros((1, LANES), F32).at[0, :2 * GDN_HV].set(a.reshape(-1))
    per_vec = b // shift.shape[0]
    vec = pl.BlockSpec((1, 1, d), lambda bi, i: (bi // per_vec, 0, 0))
    seq = lambda n: pl.BlockSpec((1, tm, n), lambda bi, i: (bi, i, 0))
    full = lambda s: pl.BlockSpec(s, lambda bi, i: (0,) * len(s))
    f = lambda n, dt=F32: jax.ShapeDtypeStruct((b, t, n), dt)
    return pl.pallas_call(
        _odd_in_kernel,
        grid=(b, t // tm),
        in_specs=[seq(d),
                  pl.BlockSpec((1, HALO, d), lambda bi, i: (bi, jnp.maximum(i * nb - 1, 0), 0)),
                  pl.BlockSpec((1, HALO, d), lambda bi, i: (bi, jnp.minimum((i + 1) * nb, last), 0)),
                  full((1, d)), vec, vec, full(w_qkv.shape), full(w_z.shape), full(w_ab.shape),
                  full((GDN_CONV, GDN_QKV_W)), full((1, LANES)), full((1, LANES))],
        out_specs=[seq(GDN_QK_W), seq(GDN_QK_W), seq(GDN_V_W), seq(GDN_V_W), seq(LANES)],
        out_shape=[f(GDN_QK_W), f(GDN_QK_W), f(GDN_V_W), f(GDN_V_W, BF16), f(LANES)],
        compiler_params=_CP,
        name="odd_in_stage",
    )(x3d, x3d, x3d, g.reshape(1, d), shift, scale, w_qkv, w_z, w_ab, conv_w, pad(a_log), pad(dt_bias))


def _gdn_chunk_kernel(q_ref, k_ref, v_ref, gb_ref, s0_ref, *rest, reverse, g_lane, b_lane):
    o_ref, sfin_ref, s_ref = rest[-3:]
    add_ref = rest[0] if len(rest) == 4 else None
    c = pl.program_id(0)
    n_c = pl.num_programs(0)
    cs = GDN_CHUNK
    rep = GDN_HV // GDN_HK
    nb = q_ref.shape[0]

    @pl.when(c == 0)
    def _():
        s_ref[...] = s0_ref[...].reshape(s_ref.shape)

    row = lax.broadcasted_iota(jnp.int32, (cs, cs), 0)
    col = lax.broadcasted_iota(jnp.int32, (cs, cs), 1)
    incl = (row <= col) if reverse else (row >= col)
    strict = (row < col) if reverse else (row > col)
    eye = (row == col).astype(F32)
    incl_b = incl.astype(BF16)
    last = 0 if reverse else cs - 1
    tn = (((0,), (1,)), ((), ()))

    heads = range(GDN_HV)
    gc, gr, bc, qh, kh, vh = [], [], [], [], [], []
    for b in range(nb):
        gb = gb_ref[b]
        gb_hi = gb.astype(BF16)
        gb_lo = (gb - gb_hi.astype(F32)).astype(BF16)
        g_col = (jnp.dot(incl_b, gb_hi, preferred_element_type=F32)
                 + jnp.dot(incl_b, gb_lo, preferred_element_type=F32))
        g_row = (lax.dot_general(gb_hi, incl_b, tn, preferred_element_type=F32)
                 + lax.dot_general(gb_lo, incl_b, tn, preferred_element_type=F32))
        gc += [g_col[:, g_lane + h:g_lane + h + 1] for h in heads]
        gr += [g_row[g_lane + h:g_lane + h + 1, :] for h in heads]
        bc += [gb[:, b_lane + h:b_lane + h + 1] for h in heads]
        qh += [q_ref[b, :, h * GDN_DK:(h + 1) * GDN_DK] for h in range(GDN_HK)]
        kh += [k_ref[b, :, h * GDN_DK:(h + 1) * GDN_DK] for h in range(GDN_HK)]
        vh += [v_ref[b, :, h * GDN_DV:(h + 1) * GDN_DV] for h in heads]
    gc, gr, bc = jnp.stack(gc), jnp.stack(gr), jnp.stack(bc)
    qh, kh, vh = jnp.stack(qh), jnp.stack(kh), jnp.stack(vh)
    ge = gc[:, last:last + 1, :]
    kh_b = kh.astype(BF16)
    bnt = (((2,), (2,)), ((0,), (0,)))
    kk = lax.dot_general(kh_b, kh_b, bnt, preferred_element_type=F32)
    qk = lax.dot_general(qh.astype(BF16), kh_b, bnt, preferred_element_type=F32)
    kk, qk = jnp.repeat(kk, rep, axis=0), jnp.repeat(qk, rep, axis=0)
    qv, kv = jnp.repeat(qh, rep, axis=0), jnp.repeat(kh, rep, axis=0)

    decay = jnp.where(incl, jnp.exp(jnp.where(incl, gc - gr, 0.0)), 0.0)
    lm = jnp.where(strict, bc * kk * decay, 0.0)
    bmm = lambda a, b: jnp.einsum('hij,hjk->hik', a.astype(BF16), b.astype(BF16), preferred_element_type=F32)
    blk = lambda n: (row // n) == (col // n)
    l0 = jnp.where(blk(16), lm, 0.0)
    p = bmm(l0, l0)
    x = eye - l0
    for _ in range(2):
        xp = bmm(jnp.concatenate([x, p], axis=1), p)
        x = x + xp[:, :cs]
        p = xp[:, cs:]
    x = x + bmm(x, p)
    n = 32
    while n <= cs:
        off = jnp.where(blk(n) & ~blk(n // 2), lm, 0.0)
        x = x - bmm(bmm(x, off), x)
        n *= 2
    eg = jnp.exp(gc)
    uw = bmm(x, jnp.concatenate([vh * bc, kv * (bc * eg)], axis=2))
    u = uw[:, :, :GDN_DV]
    w = uw[:, :, GDN_DV:]
    qg = qv * eg
    intra = jnp.where(incl, qk * decay, 0.0)
    kt = kv * jnp.exp(ge - gc)
    s = s_ref[...]
    wq = bmm(jnp.concatenate([w, qg], axis=1), s)
    v_new = u - wq[:, :cs]
    o = wq[:, cs:] + bmm(intra, v_new)
    for b in range(nb):
        for h in heads:
            cols = slice(h * GDN_DV, (h + 1) * GDN_DV)
            oh = o[b * GDN_HV + h]
            if add_ref is not None:
                oh = oh + add_ref[b, :, cols]
            o_ref[b, :, cols] = oh.astype(o_ref.dtype)
    s_ref[...] = s * jnp.exp(ge) + jnp.einsum('hck,hcv->hkv', kt.astype(BF16), v_new.astype(BF16),
                                              preferred_element_type=F32)

    @pl.when(c == n_c - 1)
    def _():
        sfin_ref[...] = s_ref[...].reshape(sfin_ref.shape)


def _gdn_scan(q, k, v, gb, s0, reverse, add=None, out_dtype=F32):
    b, t, _ = q.shape
    assert t % GDN_CHUNK == 0 and GDN_CHUNK % 32 == 0
    n_c = t // GDN_CHUNK
    cm = (lambda ci: (0, n_c - 1 - ci, 0)) if reverse else (lambda ci: (0, ci, 0))
    d = 1 if reverse else 0
    smap = lambda ci: (0, 0, 0, 0)
    in_specs = [pl.BlockSpec((b, GDN_CHUNK, GDN_QK_W), cm), pl.BlockSpec((b, GDN_CHUNK, GDN_QK_W), cm),
                pl.BlockSpec((b, GDN_CHUNK, GDN_V_W), cm), pl.BlockSpec((b, GDN_CHUNK, LANES), cm),
                pl.BlockSpec((b, GDN_HV, GDN_DK, GDN_DV), smap)]
    args = [q, k, v, gb, s0]
    if add is not None:
        in_specs.append(pl.BlockSpec((b, GDN_CHUNK, GDN_V_W), cm))
        args.append(add)
    return pl.pallas_call(
        functools.partial(_gdn_chunk_kernel, reverse=reverse, g_lane=d * GDN_HV, b_lane=(2 + d) * GDN_HV),
        grid=(n_c,),
        in_specs=in_specs,
        out_specs=[pl.BlockSpec((b, GDN_CHUNK, GDN_V_W), cm), pl.BlockSpec((b, GDN_HV, GDN_DK, GDN_DV), smap)],
        out_shape=[jax.ShapeDtypeStruct((b, t, GDN_V_W), out_dtype),
                   jax.ShapeDtypeStruct((b, GDN_HV, GDN_DK, GDN_DV), F32)],
        scratch_shapes=[pltpu.VMEM((b * GDN_HV, GDN_DK, GDN_DV), F32)],
        compiler_params=_CP,
        name="gdn_scan_bwd" if reverse else "gdn_scan_fwd",
    )(*args)


def _odd_out_kernel(o_ref, z_ref, ng_ref, w_ref, x_ref, gate_ref, g_ref, shift_ref, scale_ref,
                    wr_ref, br_ref, xo_ref, h_ref, idx_ref, gates_ref):
    parts = []
    for h in range(GDN_HV):
        cols = slice(h * GDN_DV, (h + 1) * GDN_DV)
        o = o_ref[:, cols].astype(F32)
        z = z_ref[:, cols].astype(F32)
        o = o * lax.rsqrt(jnp.mean(o * o, axis=-1, keepdims=True) + NORM_EPS) * ng_ref[...]
        parts.append((o * (z * jax.nn.sigmoid(z))).astype(BF16))
    y = jnp.dot(jnp.concatenate(parts, axis=1), w_ref[...], preferred_element_type=F32)
    _residual_norm_route(y, x_ref, gate_ref, g_ref, shift_ref, scale_ref, wr_ref, br_ref,
                         xo_ref, h_ref, idx_ref, gates_ref)


def _odd_out_proj(o, z, norm_g, w, x2d, gate, g, shift, scale, w_router, b_router, rows_per_vec, tm):
    t, d = x2d.shape
    wr, br = _router_operands(w_router, b_router)
    vec = _vec(d, rows_per_vec, tm)
    out_specs, out_shape = _route_out(t, d, tm)
    return pl.pallas_call(
        _odd_out_kernel,
        grid=(t // tm,),
        in_specs=[_row(tm, GDN_V_W), _row(tm, GDN_V_W),
                  _full((1, GDN_DV)), _full(w.shape), _row(tm, d), vec, _full((1, d)), vec, vec,
                  _full((d, LANES)), _full((1, LANES))],
        out_specs=out_specs, out_shape=out_shape,
        compiler_params=_CP,
        name="odd_out_proj",
    )(o, z, norm_g.reshape(1, GDN_DV), w, x2d, gate, g.reshape(1, d), shift, scale, wr, br)


def _split_mod(mv, bsz):
    d = mv.shape[1] // N_MOD
    lat = [mv[:bsz, j * d:(j + 1) * d][:, None, :] for j in range(N_MOD)]
    ctx = [mv[bsz:bsz + 1, j * d:(j + 1) * d][:, None, :] for j in range(N_MOD)]
    return lat, ctx


def _even_layer(x2d, xc2d, mod, mod_c, norm1_g, norm2_g, w_in, w_out, lam_p, subln_g, lam_init, cos, sin,
                w_router, b_router, bsz, n_lat, n_ctx):
    d = x2d.shape[1]
    w_in_b = w_in.astype(BF16)
    w_out_b = w_out.astype(BF16)
    q_scale = DA_DH ** -0.5 * math.log2(math.e)
    f, q, k_all, v_all = _even_in_proj(x2d, norm1_g, mod[0], mod[1], w_in_b, cos, sin, bsz, n_lat, n_lat + n_ctx,
                                       q_scale, TM_PROJ)
    ones, zeros = jnp.ones((n_ctx, ROPE_AXIS_DIM), F32), jnp.zeros((n_ctx, ROPE_AXIS_DIM), F32)
    fc, qc, kc, vc = _even_in_proj(xc2d, norm1_g, mod_c[0], mod_c[1], w_in_b, ones, zeros, bsz, n_ctx, n_ctx,
                                   q_scale, n_ctx)
    lp = lam_p.astype(F32)
    lam = jnp.exp(jnp.sum(lp[0] * lp[1])) - jnp.exp(jnp.sum(lp[2] * lp[3])) + lam_init
    k_all = lax.dynamic_update_slice(k_all, kc, (0, n_lat, 0))
    v_all = lax.dynamic_update_slice(v_all, vc, (0, n_lat, 0))
    o = _diff_attention(lam, q, k_all, v_all, subln_g, 1.0 - lam_init, TQ_ATTN, TK_ATTN)
    oc = _diff_attention(lam, qc, kc, vc, subln_g, 1.0 - lam_init, n_ctx, n_ctx)
    fm = _fourier_mix(f.reshape(bsz, n_lat, FOURIER_W)).reshape(bsz * n_lat, FOURIER_W)
    fmc = _fourier_mix(fc.reshape(bsz, n_ctx, FOURIER_W)).reshape(bsz * n_ctx, FOURIER_W)
    lat = _even_out_proj(fm, o.reshape(bsz * n_lat, DA_W), w_out_b, x2d, mod[2], norm2_g, mod[3], mod[4],
                         w_router, b_router, n_lat, TM_PROJ)
    ctx = _even_out_proj(fmc, oc.reshape(bsz * n_ctx, DA_W), w_out_b, xc2d, mod_c[2], norm2_g, mod_c[3], mod_c[4],
                         w_router, b_router, bsz * n_ctx, n_ctx)
    return lat, ctx


def _odd_layer(x2d, xc2d, mod, mod_c, norm1_g, norm2_g, w_in, conv_w, a_log, dt_bias, norm_g, w_out,
               w_router, b_router, bsz, n_lat, n_ctx):
    d = x2d.shape[1]
    q, k, v, z, gb = _odd_in_stage(x2d.reshape(bsz, n_lat, d), norm1_g, mod[0], mod[1], w_in, conv_w, a_log,
                                   dt_bias, TM_PROJ)
    qc, kc, vc, _, gbc = _odd_in_stage(xc2d.reshape(bsz, n_ctx, d), norm1_g, mod_c[0], mod_c[1], w_in, conv_w,
                                       a_log, dt_bias, n_ctx)
    s0 = jnp.zeros((bsz, GDN_HV, GDN_DK, GDN_DV), F32)
    _, sc_f = _gdn_scan(qc, kc, vc, gbc, s0, False)
    o_f, _ = _gdn_scan(q, k, v, gb, sc_f, False)
    _, sc_b = _gdn_scan(qc, kc, vc, gbc, s0, True)
    o, _ = _gdn_scan(q, k, v, gb, sc_b, True, add=o_f, out_dtype=BF16)
    return _odd_out_proj(o.reshape(bsz * n_lat, GDN_V_W), z.reshape(bsz * n_lat, GDN_V_W), norm_g,
                         w_out.astype(BF16), x2d, mod[2], norm2_g, mod[3], mod[4], w_router, b_router, n_lat,
                         TM_PROJ)


def kernel(x, c, ctx, c_ctx, norm1_g, norm2_g, w_mod, b_mod, ev_w_in, ev_w_out, ev_lam, ev_subln_g,
           od_w_in, od_conv_w, od_a_log, od_dt_bias, od_norm_g, od_w_out,
           moe_w_router, moe_b_router, moe_w1, moe_b1, moe_w2, moe_b2, final_g):
    bsz, n_lat, d = x.shape
    n_ctx = ctx.shape[1]
    assert w_mod.shape[0] == 2, "kernel is written for one even (attention) and one odd (DeltaNet) layer"
    t_lat = bsz * n_lat
    cos, sin = _axial_rope_tables(n_lat // GRID_W)
    c_rows = jnp.zeros((8, d), F32).at[:bsz].set(c).at[bsz].set(c_ctx)
    x2d = x.reshape(t_lat, d)
    xc2d = ctx.reshape(bsz * n_ctx, d)

    mod, mod_c = _split_mod(_mod_vectors(c_rows, w_mod, b_mod, 0), bsz)
    (x2d, h2, idx, gates), (xc2d, h2c, idx_c, gates_c) = _even_layer(
        x2d, xc2d, mod, mod_c, norm1_g[0], norm2_g[0], ev_w_in[0], ev_w_out[0], ev_lam[0], ev_subln_g[0],
        _diff_lambda_init(0), cos, sin, moe_w_router[0], moe_b_router[0], bsz, n_lat, n_ctx)
    yb, pos = _moe_ffn(jnp.concatenate([h2, h2c], axis=0), jnp.concatenate([idx, idx_c], axis=0),
                       moe_w1, moe_b1[0], moe_w2, moe_b2[0], 0, TM_PROJ)
    x2d = _moe_combine(yb, pos[:t_lat], gates, x2d, mod[5], n_lat, TM_PROJ)
    xc2d = _moe_combine(yb, pos[t_lat:], gates_c, xc2d, mod_c[5], bsz * n_ctx, n_ctx)

    mod, mod_c = _split_mod(_mod_vectors(c_rows, w_mod, b_mod, 1), bsz)
    x2d, h2, idx, gates = _odd_layer(x2d, xc2d, mod, mod_c, norm1_g[1], norm2_g[1], od_w_in[0], od_conv_w[0],
                                     od_a_log[0], od_dt_bias[0], od_norm_g[0], od_w_out[0],
                                     moe_w_router[1], moe_b_router[1], bsz, n_lat, n_ctx)
    yb, pos = _moe_ffn(h2, idx, moe_w1, moe_b1[1], moe_w2, moe_b2[1], 1, TM_PROJ)
    return _moe_combine(yb, pos, gates, x2d, mod[5], n_lat, TM_PROJ, final_g=final_g).reshape(bsz, n_lat, d)
```

```python
import functools
import math

import jax
import jax.numpy as jnp
from jax import lax
from jax.experimental import pallas as pl
from jax.experimental.pallas import tpu as pltpu

D_MODEL = 1024
N_MOD = 6
NORM_EPS = 1e-6
GRID_W = 64

FOURIER_GROUPS = 4
FOURIER_GD = 64
FOURIER_W = FOURIER_GROUPS * FOURIER_GD
DA_HEADS = 6
DA_DH = 64
DA_VD = 2 * DA_DH
DA_W = DA_HEADS * DA_VD
ROPE_BASE = 10000.0
ROPE_AXIS_DIM = DA_DH // 2
SUBLN_EPS = 1e-5

GDN_HK = 8
GDN_HV = 16
GDN_DK = 128
GDN_DV = 128
GDN_QK_W = GDN_HK * GDN_DK
GDN_V_W = GDN_HV * GDN_DV
GDN_QKV_W = 2 * GDN_QK_W + GDN_V_W
GDN_MAIN_W = GDN_QKV_W + GDN_V_W
GDN_CONV = 5
GDN_CHUNK = 64

N_EXPERTS = 32
TOP_K = 4
D_FF = 1024
SWIGLU_LIMIT = 7.0
SWIGLU_ALPHA = 1.702
MOE_BLOCK = 512
MOE_FIRST_PART = 32

LANES = 128
SUBLANES_BF16 = 16
MXU_DIM = 256
VMEM_LIMIT = 56 * 1024 * 1024
BF16 = jnp.bfloat16
F32 = jnp.float32
HI = lax.Precision.HIGHEST

TM_PROJ = 512
TQ_ATTN, TK_ATTN = 512, 1280

_CP = pltpu.CompilerParams(vmem_limit_bytes=VMEM_LIMIT)


def _diff_lambda_init(layer_idx):
    return 0.8 - 0.6 * math.exp(-0.3 * layer_idx)


def _axial_rope_tables(rows):
    t = jnp.arange(rows * GRID_W, dtype=jnp.int32)
    row = (t // GRID_W).astype(F32)
    col = (t % GRID_W).astype(F32)
    inv = ROPE_BASE ** (-jnp.arange(0, ROPE_AXIS_DIM, 2, dtype=F32) / ROPE_AXIS_DIM)
    ang = jnp.concatenate([row[:, None] * inv, col[:, None] * inv], axis=-1)
    return jnp.cos(ang), jnp.sin(ang)


def _norm_mod(x, g_ref, shift_ref, scale_ref):
    h = x * lax.rsqrt(jnp.mean(x * x, axis=-1, keepdims=True) + NORM_EPS) * g_ref[...]
    return (h * (1.0 + scale_ref[0]) + shift_ref[0]).astype(BF16)


def _route(logits):
    lane = lax.broadcasted_iota(jnp.int32, logits.shape, 1)
    lane_f = lane.astype(F32)
    neg = jnp.float32(-jnp.inf)
    rest = jnp.where(lane < N_EXPERTS, logits, neg)
    idx = jnp.zeros(logits.shape, F32)
    val = jnp.full(logits.shape, neg, F32)
    for j in range(TOP_K):
        m = jnp.max(rest, axis=-1, keepdims=True)
        sel = jnp.min(jnp.where(rest == m, lane_f, float(LANES)), axis=-1, keepdims=True)
        idx = jnp.where(lane == j, sel, idx)
        val = jnp.where(lane == j, m, val)
        rest = jnp.where(lane_f == sel, neg, rest)
    e = jnp.exp(val - val[:, 0:1])
    return idx, e / jnp.sum(e, axis=-1, keepdims=True)


def _residual_norm_route(y, x_ref, gate_ref, g_ref, shift_ref, scale_ref, wr_ref, br_ref,
                         xo_ref, h_ref, idx_ref, gates_ref):
    xn = x_ref[...] + gate_ref[0] * y
    xo_ref[...] = xn
    h = _norm_mod(xn, g_ref, shift_ref, scale_ref)
    h_ref[...] = h
    idx, gates = _route(jnp.dot(h, wr_ref[...], preferred_element_type=F32) + br_ref[...])
    idx_ref[...] = idx.astype(jnp.int32)
    gates_ref[...] = gates


def _row(tm, n):
    return pl.BlockSpec((tm, n), lambda i: (i, 0))


def _full(shape):
    return pl.BlockSpec(shape, lambda i: (0,) * len(shape))


def _vec(d, rows_per_vec, tm):
    per = rows_per_vec // tm
    return pl.BlockSpec((1, 1, d), lambda i: (i // per, 0, 0))


def _router_operands(w_router, b_router):
    d = w_router.shape[0]
    wr = jnp.zeros((d, LANES), BF16).at[:, :N_EXPERTS].set(w_router.astype(BF16))
    br = jnp.zeros((1, LANES), F32).at[0, :N_EXPERTS].set(b_router)
    return wr, br


def _route_out(t, d, tm):
    specs = [_row(tm, d), _row(tm, d), _row(tm, LANES), _row(tm, LANES)]
    shapes = [jax.ShapeDtypeStruct((t, d), F32), jax.ShapeDtypeStruct((t, d), BF16),
              jax.ShapeDtypeStruct((t, LANES), jnp.int32), jax.ShapeDtypeStruct((t, LANES), F32)]
    return specs, shapes


def _mod_kernel(c_ref, w_ref, b_ref, o_ref):
    c = c_ref[...]
    s = (c * jax.nn.sigmoid(c)).astype(BF16)
    o_ref[...] = jnp.dot(s, w_ref[0].astype(BF16), preferred_element_type=F32) + b_ref[0]


def _mod_vectors(c_rows, w_mod, b_mod, layer):
    d = c_rows.shape[1]
    return pl.pallas_call(
        _mod_kernel,
        grid=(N_MOD,),
        in_specs=[pl.BlockSpec(c_rows.shape, lambda n: (0, 0)),
                  pl.BlockSpec((1, d, d), lambda n: (layer, 0, n)),
                  pl.BlockSpec((1, 1, d), lambda n: (layer, 0, n))],
        out_specs=pl.BlockSpec((c_rows.shape[0], d), lambda n: (0, n)),
        out_shape=jax.ShapeDtypeStruct((c_rows.shape[0], N_MOD * d), F32),
        compiler_params=_CP,
        name="mod_vectors",
    )(c_rows, w_mod, b_mod.reshape(b_mod.shape[0], 1, -1))


def _even_in_kernel(x_ref, g_ref, shift_ref, scale_ref, w_ref, rc_ref, rs_ref, f_ref, q_ref, k_ref, v_ref, *, q_scale):
    h = _norm_mod(x_ref[...], g_ref, shift_ref, scale_ref)
    p = jnp.dot(h, w_ref[...], preferred_element_type=F32)
    f_ref[...] = p[:, :FOURIER_W]
    rc, rs = rc_ref[...], rs_ref[...]
    lane = lax.broadcasted_iota(jnp.int32, rc.shape, 1)
    first = (lane % DA_DH) < ROPE_AXIS_DIM

    def rope(t):
        partner = jnp.where(first, pltpu.roll(t, LANES - ROPE_AXIS_DIM, 1), pltpu.roll(t, ROPE_AXIS_DIM, 1))
        return t * rc + partner * rs

    for hd in range(DA_HEADS):
        cq = slice(FOURIER_W + hd * DA_VD, FOURIER_W + (hd + 1) * DA_VD)
        ck = slice(FOURIER_W + DA_W + hd * DA_VD, FOURIER_W + DA_W + (hd + 1) * DA_VD)
        q_ref[0, :, hd * DA_VD:(hd + 1) * DA_VD] = (rope(p[:, cq]) * q_scale).astype(BF16)
        k_ref[0, :, hd * DA_VD:(hd + 1) * DA_VD] = rope(p[:, ck]).astype(BF16)
    v_ref[0] = p[:, FOURIER_W + 2 * DA_W:].astype(BF16)


def _even_in_proj(x2d, g, shift, scale, w, cos, sin, bsz, n_seq, n_keys, q_scale, tm):
    t, d = x2d.shape
    bpb = n_seq // tm
    rc = jnp.tile(cos, (1, LANES // ROPE_AXIS_DIM))
    rs = jnp.tile(jnp.concatenate([-sin, sin], axis=1), (1, LANES // DA_DH))
    tab = pl.BlockSpec((tm, LANES), lambda i: (i % bpb, 0))
    seq = pl.BlockSpec((1, tm, DA_W), lambda i: (i // bpb, i % bpb, 0))
    vec = _vec(d, n_seq * (bsz // shift.shape[0]), tm)
    return pl.pallas_call(
        functools.partial(_even_in_kernel, q_scale=q_scale),
        grid=(t // tm,),
        in_specs=[_row(tm, d), _full((1, d)), vec, vec, _full(w.shape), tab, tab],
        out_specs=[_row(tm, FOURIER_W), seq, seq, seq],
        out_shape=[jax.ShapeDtypeStruct((t, FOURIER_W), F32), jax.ShapeDtypeStruct((bsz, n_seq, DA_W), BF16),
                   jax.ShapeDtypeStruct((bsz, n_keys, DA_W), BF16), jax.ShapeDtypeStruct((bsz, n_keys, DA_W), BF16)],
        compiler_params=_CP,
        name="even_in_proj",
    )(x2d, g.reshape(1, d), shift, scale, w, rc, rs)


def _diff_attn_kernel(lam_ref, q_ref, k_ref, v_ref, g_ref, o_ref, qs_ref, s_ref, m_ref, acc_ref, *,
                      tk, n_sub, out_scale):
    tq = q_ref.shape[1]
    n_kv = k_ref.shape[1] // tk
    rb = 2 * tq // n_sub
    q = q_ref[0]
    lane = lax.broadcasted_iota(jnp.int32, q.shape, 1)
    zero = jnp.zeros_like(q)
    qs_ref[:tq] = jnp.where(lane < DA_DH, q, zero)
    qs_ref[tq:] = jnp.where(lane >= DA_DH, q, zero)
    m_ref[...] = jnp.full(m_ref.shape, -1e30, F32)
    acc_ref[...] = jnp.zeros(acc_ref.shape, F32)
    ones = jnp.ones((tk, LANES), BF16)

    def scores(i, slot):
        off = pl.multiple_of(i * tk, tk)
        k = k_ref[0, pl.ds(off, tk), :]
        s_ref[slot] = lax.dot_general(qs_ref[...], k, (((1,), (1,)), ((), ())), preferred_element_type=F32)

    def consume(i, slot):
        off = pl.multiple_of(i * tk, tk)
        v_ext = jnp.concatenate([v_ref[0, pl.ds(off, tk), :], ones], axis=1)
        for r in range(n_sub):
            rows = pl.ds(r * rb, rb)
            s = s_ref[slot, rows, :]
            m_prev = m_ref[rows, :]
            m_new = jnp.maximum(m_prev, jnp.max(s, axis=1, keepdims=True))
            alpha = jnp.exp2(m_prev - m_new)
            p = jnp.exp2(s - jnp.tile(m_new, (1, tk // LANES)))
            pv = jnp.dot(p.astype(BF16), v_ext, preferred_element_type=F32)
            acc_ref[rows, :] = acc_ref[rows, :] * jnp.tile(alpha, (1, 2)) + pv
            m_ref[rows, :] = m_new

    scores(0, 0)

    def body(j, carry):
        scores(2 * j + 1, 1)
        consume(2 * j, 0)
        scores(2 * j + 2, 0)
        consume(2 * j + 1, 1)
        return carry

    lax.fori_loop(0, (n_kv - 1) // 2, body, 0)
    consume(n_kv - 1, 0)
    acc = acc_ref[...]
    o1 = acc[:tq, :LANES] / acc[:tq, LANES:]
    o2 = acc[tq:, :LANES] / acc[tq:, LANES:]
    o = o1 - lam_ref[0] * o2
    ms = jnp.mean(o * o, axis=-1, keepdims=True)
    o = o * lax.rsqrt(ms + SUBLN_EPS) * g_ref[...] * out_scale
    o_ref[0] = o.astype(o_ref.dtype)


def _diff_attention(lam, q, k_all, v_all, subln_g, out_scale, tq, tk, n_sub=2):
    b, n, _ = q.shape
    nk = k_all.shape[1]
    assert n % tq == 0 and nk % tk == 0 and tk % MXU_DIM == 0 and (nk // tk) % 2 == 1
    grid_spec = pltpu.PrefetchScalarGridSpec(
        num_scalar_prefetch=1,
        grid=(b, DA_HEADS, n // tq),
        in_specs=[pl.BlockSpec((1, tq, DA_VD), lambda bi, hi, qi, lam_r: (bi, qi, hi)),
                  pl.BlockSpec((1, nk, DA_VD), lambda bi, hi, qi, lam_r: (bi, 0, hi)),
                  pl.BlockSpec((1, nk, DA_VD), lambda bi, hi, qi, lam_r: (bi, 0, hi)),
                  pl.BlockSpec((1, DA_VD), lambda bi, hi, qi, lam_r: (0, 0))],
        out_specs=pl.BlockSpec((1, tq, DA_VD), lambda bi, hi, qi, lam_r: (bi, qi, hi)),
        scratch_shapes=[pltpu.VMEM((2 * tq, LANES), BF16), pltpu.VMEM((2, 2 * tq, tk), F32),
                        pltpu.VMEM((2 * tq, LANES), F32), pltpu.VMEM((2 * tq, 2 * LANES), F32)],
    )
    return pl.pallas_call(
        functools.partial(_diff_attn_kernel, tk=tk, n_sub=n_sub, out_scale=out_scale),
        grid_spec=grid_spec,
        out_shape=jax.ShapeDtypeStruct((b, n, DA_W), BF16),
        compiler_params=_CP,
        name="diff_attention",
    )(lam.reshape(1), q, k_all, v_all, subln_g.reshape(1, DA_VD))


def _fft_stage1_kernel(x_ref, fch_ref, f1_ref, tc_ref, ts_ref, yr_ref, yi_ref, *, nb):
    w = FOURIER_W
    r = x_ref.shape[1]
    for bl in range(nb):
        cols = slice(bl * w, (bl + 1) * w)
        ab = jnp.dot(x_ref[0, :, cols], fch_ref[...], preferred_element_type=F32, precision=HI)
        z = jnp.concatenate([ab[:, :w], ab[:, w:]], axis=0)
        y = jnp.dot(f1_ref[...], z, preferred_element_type=F32, precision=HI)
        yr, yi = y[:r], y[r:]
        tc, ts = tc_ref[:, cols], ts_ref[:, cols]
        yr_ref[0, :, cols] = yr * tc + yi * ts
        yi_ref[0, :, cols] = yi * tc - yr * ts


def _fft_stage2_kernel(yr_ref, yi_ref, f2_ref, o_ref, *, nc):
    w = FOURIER_W
    for cl in range(nc):
        y = jnp.concatenate([yr_ref[0, cl], yi_ref[0, cl]], axis=0)
        o_ref[0, :, cl * w:(cl + 1) * w] = jnp.dot(f2_ref[...], y, preferred_element_type=F32, precision=HI)


def _fourier_mix(f, nb=8):
    bsz, n, w = f.shape
    r = math.isqrt(n)
    assert r * r == n and r % nb == 0 and w == FOURIER_W
    two_pi = 2.0 * math.pi
    k = jnp.arange(r, dtype=jnp.int32)
    ang = two_pi * ((k[:, None] * k[None, :]) % r).astype(F32) / r
    c1, s1 = jnp.cos(ang), jnp.sin(ang)
    kc = jnp.arange(FOURIER_GD, dtype=jnp.int32)
    angc = two_pi * ((kc[:, None] * kc[None, :]) % FOURIER_GD).astype(F32) / FOURIER_GD
    eye_g = jnp.eye(FOURIER_GROUPS, dtype=F32)
    fch = jnp.concatenate([jnp.kron(eye_g, jnp.cos(angc)), jnp.kron(eye_g, jnp.sin(angc))], axis=1)
    f1 = jnp.concatenate([jnp.concatenate([c1, -s1], axis=1), jnp.concatenate([-s1, -c1], axis=1)], axis=0)
    f2 = jnp.concatenate([c1, s1], axis=1) * (1.0 / math.sqrt(n * FOURIER_GD))
    angt = two_pi * ((k[:, None] * k[None, :]) % n).astype(F32) / n
    tc = jnp.repeat(jnp.cos(angt), w, axis=1)
    ts = jnp.repeat(jnp.sin(angt), w, axis=1)
    xv = f.reshape(bsz, r, r * w)
    blk = pl.BlockSpec((1, r, nb * w), lambda bi, j: (bi, 0, j))
    tab = pl.BlockSpec((r, nb * w), lambda bi, j: (0, j))
    full = lambda a: pl.BlockSpec(a.shape, lambda bi, j: (0, 0))
    yshape = jax.ShapeDtypeStruct((bsz, r, r * w), F32)
    yr, yi = pl.pallas_call(
        functools.partial(_fft_stage1_kernel, nb=nb),
        grid=(bsz, r // nb),
        in_specs=[blk, full(fch), full(f1), tab, tab],
        out_specs=[blk, blk],
        out_shape=[yshape, yshape],
        compiler_params=_CP,
        name="fft_stage1",
    )(xv, fch, f1, tc, ts)
    y4 = lambda a: a.reshape(bsz, r, r, w)
    yblk = pl.BlockSpec((1, nb, r, w), lambda bi, j: (bi, j, 0, 0))
    out = pl.pallas_call(
        functools.partial(_fft_stage2_kernel, nc=nb),
        grid=(bsz, r // nb),
        in_specs=[yblk, yblk, full(f2)],
        out_specs=blk,
        out_shape=yshape,
        compiler_params=_CP,
        name="fft_stage2",
    )(y4(yr), y4(yi), f2)
    return out.reshape(bsz, n, w)


def _even_out_kernel(fm_ref, o_ref, w_ref, x_ref, gate_ref, g_ref, shift_ref, scale_ref, wr_ref, br_ref,
                     xo_ref, h_ref, idx_ref, gates_ref):
    y = (jnp.dot(fm_ref[...].astype(BF16), w_ref[:FOURIER_W], preferred_element_type=F32)
         + jnp.dot(o_ref[...], w_ref[FOURIER_W:], preferred_element_type=F32))
    _residual_norm_route(y, x_ref, gate_ref, g_ref, shift_ref, scale_ref, wr_ref, br_ref,
                         xo_ref, h_ref, idx_ref, gates_ref)


def _even_out_proj(fm, o, w, x2d, gate, g, shift, scale, w_router, b_router, rows_per_vec, tm):
    t, d = x2d.shape
    wr, br = _router_operands(w_router, b_router)
    vec = _vec(d, rows_per_vec, tm)
    out_specs, out_shape = _route_out(t, d, tm)
    return pl.pallas_call(
        _even_out_kernel,
        grid=(t // tm,),
        in_specs=[_row(tm, FOURIER_W), _row(tm, DA_W), _full(w.shape), _row(tm, d), vec, _full((1, d)), vec, vec,
                  _full((d, LANES)), _full((1, LANES))],
        out_specs=out_specs, out_shape=out_shape,
        compiler_params=_CP,
        name="even_out_proj",
    )(fm, o, w, x2d, gate, g.reshape(1, d), shift, scale, wr, br)


def _moe_kernel(be_ref, x_ref, w1_ref, perm_ref, b1g_ref, b1l_ref, w2_ref, b2_ref, *rest):
    o_ref, w1g_s, w1l_s, w2_s = rest[-4:]
    i = pl.program_id(0)

    @pl.when((i == 0) | (be_ref[i] != be_ref[jnp.maximum(i - 1, 0)]))
    def _():
        for c in range(w1_ref.shape[3] // MXU_DIM):
            blk = jnp.dot(w1_ref[0, 0, :, c * MXU_DIM:(c + 1) * MXU_DIM].astype(BF16), perm_ref[...],
                          preferred_element_type=F32)
            w1g_s[:, c * LANES:(c + 1) * LANES] = blk[:, :LANES].astype(BF16)
            w1l_s[:, c * LANES:(c + 1) * LANES] = blk[:, LANES:].astype(BF16)
        w2_s[...] = w2_ref[0, 0].astype(BF16)

    x = x_ref[...]
    ug = jnp.dot(x, w1g_s[...], preferred_element_type=F32) + b1g_ref[0]
    ul = jnp.dot(x, w1l_s[...], preferred_element_type=F32) + b1l_ref[0]
    glu = jnp.minimum(ug, SWIGLU_LIMIT)
    lin = jnp.clip(ul, -SWIGLU_LIMIT, SWIGLU_LIMIT)
    act = glu * jax.nn.sigmoid(SWIGLU_ALPHA * glu) * (lin + 1.0)
    y = jnp.dot(act.astype(BF16), w2_s[...], preferred_element_type=F32) + b2_ref[0]
    o_ref[...] = y.astype(o_ref.dtype)


def _moe_experts(block_expert, xb, w1_all, b1g, b1l, w2_all, layer, b2, yb_prev, first_block, n_blocks_total):
    n_rows, d = xb.shape
    n_blocks = n_rows // MOE_BLOCK
    src = jnp.arange(MXU_DIM)[:, None]
    dst = jnp.arange(MXU_DIM)[None, :]
    perm = (src == jnp.where(dst < LANES, 2 * dst, 2 * (dst - LANES) + 1)).astype(BF16)
    bspec = lambda s: pl.BlockSpec((1,) + s, lambda i, be: (be[i], 0, 0))
    wspec = lambda s: pl.BlockSpec((1, 1) + s, lambda i, be: (layer, be[i], 0, 0))
    in_specs = [pl.BlockSpec((MOE_BLOCK, d), lambda i, be: (i, 0)),
                wspec((d, 2 * D_FF)), pl.BlockSpec((MXU_DIM, MXU_DIM), lambda i, be: (0, 0)),
                bspec((1, D_FF)), bspec((1, D_FF)), wspec((D_FF, d)), bspec((1, d))]
    args = [block_expert, xb, w1_all, perm, b1g, b1l, w2_all, b2]
    aliases = {}
    if yb_prev is not None:
        in_specs.append(pl.BlockSpec(memory_space=pl.ANY))
        aliases = {len(args): 0}
        args.append(yb_prev)
    grid_spec = pltpu.PrefetchScalarGridSpec(
        num_scalar_prefetch=1,
        grid=(n_blocks,),
        in_specs=in_specs,
        out_specs=pl.BlockSpec((MOE_BLOCK, d), lambda i, be: (first_block + i, 0)),
        scratch_shapes=[pltpu.VMEM((d, D_FF), BF16), pltpu.VMEM((d, D_FF), BF16), pltpu.VMEM((D_FF, d), BF16)],
    )
    return pl.pallas_call(
        _moe_kernel,
        grid_spec=grid_spec,
        out_shape=jax.ShapeDtypeStruct((n_blocks_total * MOE_BLOCK, d), BF16),
        input_output_aliases=aliases,
        compiler_params=_CP,
        name="moe_experts",
    )(*args)


def _rank_kernel(idx_ref, rank_ref, cnt_ref, carry_ref):
    i = pl.program_id(0)

    @pl.when(i == 0)
    def _():
        carry_ref[...] = jnp.zeros(carry_ref.shape, F32)

    idx = idx_ref[...]
    tm = idx.shape[0]
    lane = lax.broadcasted_iota(jnp.int32, idx.shape, 1)
    sel = [lane == idx[:, j:j + 1] for j in range(TOP_K)]
    onehot = sel[0].astype(F32)
    for j in range(1, TOP_K):
        onehot = onehot + sel[j].astype(F32)
    row = lax.broadcasted_iota(jnp.int32, (tm, tm), 0)
    col = lax.broadcasted_iota(jnp.int32, (tm, tm), 1)
    before = (row > col).astype(BF16)
    prefix = jnp.dot(before, onehot.astype(BF16), preferred_element_type=F32) + carry_ref[0:1, :]
    rank = jnp.zeros(idx.shape, F32)
    for j in range(TOP_K):
        rank = jnp.where(lane == j, jnp.sum(jnp.where(sel[j], prefix, 0.0), axis=-1, keepdims=True), rank)
    rank_ref[...] = rank.astype(jnp.int32)
    carry_ref[0:1, :] = carry_ref[0:1, :] + jnp.sum(onehot, axis=0, keepdims=True)
    cnt_ref[...] = carry_ref[...]


def _expert_ranks(idx, tm):
    t = idx.shape[0]
    rank, cnt = pl.pallas_call(
        _rank_kernel,
        grid=(t // tm,),
        in_specs=[_row(tm, LANES)],
        out_specs=[_row(tm, LANES), _full((8, LANES))],
        out_shape=[jax.ShapeDtypeStruct((t, LANES), jnp.int32), jax.ShapeDtypeStruct((8, LANES), F32)],
        scratch_shapes=[pltpu.VMEM((8, LANES), F32)],
        compiler_params=_CP,
        name="expert_ranks",
    )(idx)
    return rank, cnt[0, :N_EXPERTS].astype(jnp.int32)


def _pos_kernel(idx_ref, rank_ref, ps_ref, pos_ref):
    idx = idx_ref[...]
    lane = lax.broadcasted_iota(jnp.int32, idx.shape, 1)
    pos = rank_ref[...].astype(F32)
    for j in range(TOP_K):
        start = jnp.sum(jnp.where(lane == idx[:, j:j + 1], ps_ref[...], 0.0), axis=-1, keepdims=True)
        pos = jnp.where(lane == j, pos + start, pos)
    pos_ref[...] = pos.astype(jnp.int32)


def _expert_rows(idx, rank, pad_starts, tm):
    t = idx.shape[0]
    ps = jnp.zeros((1, LANES), F32).at[0, :N_EXPERTS].set(pad_starts.astype(F32))
    pos = pl.pallas_call(
        _pos_kernel,
        grid=(t // tm,),
        in_specs=[_row(tm, LANES), _row(tm, LANES), _full((1, LANES))],
        out_specs=_row(tm, LANES),
        out_shape=jax.ShapeDtypeStruct((t, LANES), jnp.int32),
        compiler_params=_CP,
        name="expert_rows",
    )(idx, rank, ps)
    return pos[:, :TOP_K]


def _moe_parts(n_blocks):
    parts, first, size = [], 0, max(1, n_blocks // MOE_FIRST_PART)
    while first < n_blocks:
        if n_blocks - first < 2 * size:
            size = n_blocks - first
        parts.append((first, size))
        first += size
        size *= 2
    return parts


def _moe_ffn(h, idx, w1_all, b1, w2_all, b2, layer, tm):
    n_tok, d = h.shape
    n_assign = n_tok * TOP_K
    e_flat = idx[:, :TOP_K].reshape(n_assign)
    order = jnp.argsort(e_flat).astype(jnp.int32)
    rank, counts = _expert_ranks(idx, tm)
    starts = jnp.cumsum(counts) - counts
    padded = (counts + MOE_BLOCK - 1) // MOE_BLOCK * MOE_BLOCK
    pad_ends = jnp.cumsum(padded)
    pad_starts = pad_ends - padded
    n_blocks = -(-(n_assign + N_EXPERTS * (MOE_BLOCK - 1)) // MOE_BLOCK)
    n_rows = n_blocks * MOE_BLOCK
    block_start = jnp.arange(n_blocks, dtype=jnp.int32) * MOE_BLOCK
    block_expert = jnp.minimum(jnp.sum(pad_ends[None, :] <= block_start[:, None], axis=1, dtype=jnp.int32),
                               N_EXPERTS - 1)
    shift = pad_starts - starts
    pos = _expert_rows(idx, rank, pad_starts, tm)
    r = jnp.arange(n_rows, dtype=jnp.int32)
    src = jnp.clip(r - jnp.repeat(shift[block_expert], MOE_BLOCK), 0, n_assign - 1)
    row_tok = order[src] // TOP_K
    yb = None
    for first, size in _moe_parts(n_blocks):
        xb = h[row_tok[first * MOE_BLOCK:(first + size) * MOE_BLOCK]]
        yb = _moe_experts(block_expert[first:first + size], xb, w1_all, b1[:, None, 0::2], b1[:, None, 1::2],
                          w2_all, layer, b2[:, None, :], yb, first, n_blocks)
    return yb, pos


def _combine_kernel(*refs, final):
    y_refs = refs[:TOP_K]
    gt_ref, x_ref, g5_ref = refs[TOP_K:TOP_K + 3]
    o_ref = refs[-1]
    gt = gt_ref[...]
    acc = y_refs[0][...].astype(F32) * gt[:, 0:1]
    for j in range(1, TOP_K):
        acc = acc + y_refs[j][...].astype(F32) * gt[:, j:j + 1]
    xn = x_ref[...] + g5_ref[0] * acc
    if final:
        fg_ref = refs[TOP_K + 3]
        xn = xn * lax.rsqrt(jnp.mean(xn * xn, axis=-1, keepdims=True) + NORM_EPS) * fg_ref[...]
    o_ref[...] = xn


def _moe_combine(yb, pos, gates, x2d, gate5, rows_per_vec, tm, final_g=None):
    t, d = x2d.shape
    ys = [yb[pos[:, j]] for j in range(TOP_K)]
    in_specs = [_row(tm, d)] * TOP_K + [_row(tm, LANES), _row(tm, d), _vec(d, rows_per_vec, tm)]
    args = ys + [gates, x2d, gate5]
    if final_g is not None:
        in_specs.append(_full((1, d)))
        args.append(final_g.reshape(1, d))
    return pl.pallas_call(
        functools.partial(_combine_kernel, final=final_g is not None),
        grid=(t // tm,),
        in_specs=in_specs,
        out_specs=_row(tm, d),
        out_shape=jax.ShapeDtypeStruct((t, d), F32),
        compiler_params=_CP,
        name="moe_combine",
    )(*args)


HALO = SUBLANES_BF16


def _odd_in_kernel(xm_ref, xp_ref, xn_ref, g_ref, shift_ref, scale_ref, wqkv_ref, wz_ref, wab_ref, cw_ref,
                   alog_ref, dtb_ref, q_ref, k_ref, v_ref, z_ref, gb_ref):
    i = pl.program_id(1)
    n_i = pl.num_programs(1)
    tm = xm_ref.shape[1]
    half = GDN_CONV // 2
    keep_prev = (i > 0).astype(BF16)
    keep_next = (i < n_i - 1).astype(BF16)
    h_main = _norm_mod(xm_ref[0], g_ref, shift_ref, scale_ref)
    h_ext = jnp.concatenate([_norm_mod(xp_ref[0], g_ref, shift_ref, scale_ref) * keep_prev, h_main,
                             _norm_mod(xn_ref[0], g_ref, shift_ref, scale_ref) * keep_next], axis=0)
    gw = MXU_DIM
    for cg in range(GDN_QKV_W // gw):
        cols = slice(cg * gw, (cg + 1) * gw)
        ext = jnp.dot(h_ext, wqkv_ref[:, cols], preferred_element_type=F32)
        acc = ext[HALO:HALO + tm] * cw_ref[half:half + 1, cols]
        for j in range(GDN_CONV):
            if j != half:
                sh = pltpu.roll(ext, (half - j) % (tm + 2 * HALO), 0)[HALO:HALO + tm]
                acc = acc + sh * cw_ref[j:j + 1, cols]
        y = acc * jax.nn.sigmoid(acc)
        for sub in range(gw // LANES):
            hd = cg * (gw // LANES) + sub
            yh = y[:, sub * LANES:(sub + 1) * LANES]
            if hd < 2 * GDN_HK:
                yh = yh * lax.rsqrt(jnp.sum(yh * yh, axis=-1, keepdims=True) + 1e-6)
                if hd < GDN_HK:
                    q_ref[0, :, hd * LANES:(hd + 1) * LANES] = yh * (GDN_DK ** -0.5)
                else:
                    k_ref[0, :, (hd - GDN_HK) * LANES:(hd - GDN_HK + 1) * LANES] = yh
            else:
                v_ref[0, :, (hd - 2 * GDN_HK) * LANES:(hd - 2 * GDN_HK + 1) * LANES] = yh
    z_ref[0] = jnp.dot(h_main, wz_ref[...], preferred_element_type=F32).astype(BF16)
    ab = jnp.dot(h_main, wab_ref[...], preferred_element_type=F32)
    xa = ab + dtb_ref[...]
    softplus = jnp.maximum(xa, 0.0) + jnp.log1p(jnp.exp(-jnp.abs(xa)))
    lane = lax.broadcasted_iota(jnp.int32, ab.shape, 1)
    gb_ref[0] = jnp.where(lane < 2 * GDN_HV, -jnp.exp(alog_ref[...]) * softplus, jax.nn.sigmoid(ab))


def _odd_in_stage(x3d, g, shift, scale, w_in, conv_w, a_log, dt_bias, tm):
    b, t, d = x3d.shape
    w_qkv = w_in[:, :GDN_QKV_W].astype(BF16)
    w_z = w_in[:, GDN_QKV_W:GDN_MAIN_W].astype(BF16)
    w_ab = jnp.zeros((d, LANES), BF16).at[:, :4 * GDN_HV].set(w_in[:, GDN_MAIN_W:].astype(BF16))
    nb = tm // HALO
    last = t // HALO - 1
    pad = lambda a: jnp.zeros((1, LANES), F32).at[0, :2 * GDN_HV].set(a.reshape(-1))
    per_vec = b // shift.shape[0]
    vec = pl.BlockSpec((1, 1, d), lambda bi, i: (bi // per_vec, 0, 0))
    seq = lambda n: pl.BlockSpec((1, tm, n), lambda bi, i: (bi, i, 0))
    full = lambda s: pl.BlockSpec(s, lambda bi, i: (0,) * len(s))
    f = lambda n, dt=F32: jax.ShapeDtypeStruct((b, t, n), dt)
    return pl.pallas_call(
        _odd_in_kernel,
        grid=(b, t // tm),
        in_specs=[seq(d),
                  pl.BlockSpec((1, HALO, d), lambda bi, i: (bi, jnp.maximum(i * nb - 1, 0), 0)),
                  pl.BlockSpec((1, HALO, d), lambda bi, i: (bi, jnp.minimum((i + 1) * nb, last), 0)),
                  full((1, d)), vec, vec, full(w_qkv.shape), full(w_z.shape), full(w_ab.shape),
                  full((GDN_CONV, GDN_QKV_W)), full((1, LANES)), full((1, LANES))],
        out_specs=[seq(GDN_QK_W), seq(GDN_QK_W), seq(GDN_V_W), seq(GDN_V_W), seq(LANES)],
        out_shape=[f(GDN_QK_W), f(GDN_QK_W), f(GDN_V_W), f(GDN_V_W, BF16), f(LANES)],
        compiler_params=_CP,
        name="odd_in_stage",
    )(x3d, x3d, x3d, g.reshape(1, d), shift, scale, w_qkv, w_z, w_ab, conv_w, pad(a_log), pad(dt_bias))


def _gdn_chunk_kernel(q_ref, k_ref, v_ref, gb_ref, s0_ref, *rest, reverse, g_lane, b_lane):
    o_ref, sfin_ref, s_ref = rest[-3:]
    add_ref = rest[0] if len(rest) == 4 else None
    c = pl.program_id(0)
    n_c = pl.num_programs(0)
    cs = GDN_CHUNK
    rep = GDN_HV // GDN_HK
    nb = q_ref.shape[0]

    @pl.when(c == 0)
    def _():
        s_ref[...] = s0_ref[...].reshape(s_ref.shape)

    row = lax.broadcasted_iota(jnp.int32, (cs, cs), 0)
    col = lax.broadcasted_iota(jnp.int32, (cs, cs), 1)
    incl = (row <= col) if reverse else (row >= col)
    strict = (row < col) if reverse else (row > col)
    eye = (row == col).astype(F32)
    incl_b = incl.astype(BF16)
    last = 0 if reverse else cs - 1
    tn = (((0,), (1,)), ((), ()))

    heads = range(GDN_HV)
    gc, gr, bc, qh, kh, vh = [], [], [], [], [], []
    for b in range(nb):
        gb = gb_ref[b]
        gb_hi = gb.astype(BF16)
        gb_lo = (gb - gb_hi.astype(F32)).astype(BF16)
        g_col = (jnp.dot(incl_b, gb_hi, preferred_element_type=F32)
                 + jnp.dot(incl_b, gb_lo, preferred_element_type=F32))
        g_row = (lax.dot_general(gb_hi, incl_b, tn, preferred_element_type=F32)
                 + lax.dot_general(gb_lo, incl_b, tn, preferred_element_type=F32))
        gc += [g_col[:, g_lane + h:g_lane + h + 1] for h in heads]
        gr += [g_row[g_lane + h:g_lane + h + 1, :] for h in heads]
        bc += [gb[:, b_lane + h:b_lane + h + 1] for h in heads]
        qh += [q_ref[b, :, h * GDN_DK:(h + 1) * GDN_DK] for h in range(GDN_HK)]
        kh += [k_ref[b, :, h * GDN_DK:(h + 1) * GDN_DK] for h in range(GDN_HK)]
        vh += [v_ref[b, :, h * GDN_DV:(h + 1) * GDN_DV] for h in heads]
    gc, gr, bc = jnp.stack(gc), jnp.stack(gr), jnp.stack(bc)
    qh, kh, vh = jnp.stack(qh), jnp.stack(kh), jnp.stack(vh)
    ge = gc[:, last:last + 1, :]
    kh_b = kh.astype(BF16)
    bnt = (((2,), (2,)), ((0,), (0,)))
    kk = lax.dot_general(kh_b, kh_b, bnt, preferred_element_type=F32)
    qk = lax.dot_general(qh.astype(BF16), kh_b, bnt, preferred_element_type=F32)
    kk, qk = jnp.repeat(kk, rep, axis=0), jnp.repeat(qk, rep, axis=0)
    qv, kv = jnp.repeat(qh, rep, axis=0), jnp.repeat(kh, rep, axis=0)

    decay = jnp.where(incl, jnp.exp(jnp.where(incl, gc - gr, 0.0)), 0.0)
    lm = jnp.where(strict, bc * kk * decay, 0.0)
    bmm = lambda a, b: jnp.einsum('hij,hjk->hik', a.astype(BF16), b.astype(BF16), preferred_element_type=F32)
    blk = lambda n: (row // n) == (col // n)
    l0 = jnp.where(blk(16), lm, 0.0)
    p = bmm(l0, l0)
    x = eye - l0
    for _ in range(2):
        xp = bmm(jnp.concatenate([x, p], axis=1), p)
        x = x + xp[:, :cs]
        p = xp[:, cs:]
    x = x + bmm(x, p)
    n = 32
    while n <= cs:
        off = jnp.where(blk(n) & ~blk(n // 2), lm, 0.0)
        x = x - bmm(bmm(x, off), x)
        n *= 2
    eg = jnp.exp(gc)
    uw = bmm(x, jnp.concatenate([vh * bc, kv * (bc * eg)], axis=2))
    u = uw[:, :, :GDN_DV]
    w = uw[:, :, GDN_DV:]
    qg = qv * eg
    intra = jnp.where(incl, qk * decay, 0.0)
    kt = kv * jnp.exp(ge - gc)
    s = s_ref[...]
    wq = bmm(jnp.concatenate([w, qg], axis=1), s)
    v_new = u - wq[:, :cs]
    o = wq[:, cs:] + bmm(intra, v_new)
    for b in range(nb):
        for h in heads:
            cols = slice(h * GDN_DV, (h + 1) * GDN_DV)
            oh = o[b * GDN_HV + h]
            if add_ref is not None:
                oh = oh + add_ref[b, :, cols]
            o_ref[b, :, cols] = oh.astype(o_ref.dtype)
    s_ref[...] = s * jnp.exp(ge) + jnp.einsum('hck,hcv->hkv', kt.astype(BF16), v_new.astype(BF16),
                                              preferred_element_type=F32)

    @pl.when(c == n_c - 1)
    def _():
        sfin_ref[...] = s_ref[...].reshape(sfin_ref.shape)


def _gdn_scan(q, k, v, gb, s0, reverse, add=None, out_dtype=F32):
    b, t, _ = q.shape
    assert t % GDN_CHUNK == 0 and GDN_CHUNK % 32 == 0
    n_c = t // GDN_CHUNK
    cm = (lambda ci: (0, n_c - 1 - ci, 0)) if reverse else (lambda ci: (0, ci, 0))
    d = 1 if reverse else 0
    smap = lambda ci: (0, 0, 0, 0)
    in_specs = [pl.BlockSpec((b, GDN_CHUNK, GDN_QK_W), cm), pl.BlockSpec((b, GDN_CHUNK, GDN_QK_W), cm),
                pl.BlockSpec((b, GDN_CHUNK, GDN_V_W), cm), pl.BlockSpec((b, GDN_CHUNK, LANES), cm),
                pl.BlockSpec((b, GDN_HV, GDN_DK, GDN_DV), smap)]
    args = [q, k, v, gb, s0]
    if add is not None:
        in_specs.append(pl.BlockSpec((b, GDN_CHUNK, GDN_V_W), cm))
        args.append(add)
    return pl.pallas_call(
        functools.partial(_gdn_chunk_kernel, reverse=reverse, g_lane=d * GDN_HV, b_lane=(2 + d) * GDN_HV),
        grid=(n_c,),
        in_specs=in_specs,
        out_specs=[pl.BlockSpec((b, GDN_CHUNK, GDN_V_W), cm), pl.BlockSpec((b, GDN_HV, GDN_DK, GDN_DV), smap)],
        out_shape=[jax.ShapeDtypeStruct((b, t, GDN_V_W), out_dtype),
                   jax.ShapeDtypeStruct((b, GDN_HV, GDN_DK, GDN_DV), F32)],
        scratch_shapes=[pltpu.VMEM((b * GDN_HV, GDN_DK, GDN_DV), F32)],
        compiler_params=_CP,
        name="gdn_scan_bwd" if reverse else "gdn_scan_fwd",
    )(*args)


def _odd_out_kernel(o_ref, z_ref, ng_ref, w_ref, x_ref, gate_ref, g_ref, shift_ref, scale_ref,
                    wr_ref, br_ref, xo_ref, h_ref, idx_ref, gates_ref):
    parts = []
    for h in range(GDN_HV):
        cols = slice(h * GDN_DV, (h + 1) * GDN_DV)
        o = o_ref[:, cols].astype(F32)
        z = z_ref[:, cols].astype(F32)
        o = o * lax.rsqrt(jnp.mean(o * o, axis=-1, keepdims=True) + NORM_EPS) * ng_ref[...]
        parts.append((o * (z * jax.nn.sigmoid(z))).astype(BF16))
    y = jnp.dot(jnp.concatenate(parts, axis=1), w_ref[...], preferred_element_type=F32)
    _residual_norm_route(y, x_ref, gate_ref, g_ref, shift_ref, scale_ref, wr_ref, br_ref,
                         xo_ref, h_ref, idx_ref, gates_ref)


def _odd_out_proj(o, z, norm_g, w, x2d, gate, g, shift, scale, w_router, b_router, rows_per_vec, tm):
    t, d = x2d.shape
    wr, br = _router_operands(w_router, b_router)
    vec = _vec(d, rows_per_vec, tm)
    out_specs, out_shape = _route_out(t, d, tm)
    return pl.pallas_call(
        _odd_out_kernel,
        grid=(t // tm,),
        in_specs=[_row(tm, GDN_V_W), _row(tm, GDN_V_W),
                  _full((1, GDN_DV)), _full(w.shape), _row(tm, d), vec, _full((1, d)), vec, vec,
                  _full((d, LANES)), _full((1, LANES))],
        out_specs=out_specs, out_shape=out_shape,
        compiler_params=_CP,
        name="odd_out_proj",
    )(o, z, norm_g.reshape(1, GDN_DV), w, x2d, gate, g.reshape(1, d), shift, scale, wr, br)


def _split_mod(mv, bsz):
    d = mv.shape[1] // N_MOD
    lat = [mv[:bsz, j * d:(j + 1) * d][:, None, :] for j in range(N_MOD)]
    ctx = [mv[bsz:bsz + 1, j * d:(j + 1) * d][:, None, :] for j in range(N_MOD)]
    return lat, ctx


def _even_layer(x2d, xc2d, mod, mod_c, norm1_g, norm2_g, w_in, w_out, lam_p, subln_g, lam_init, cos, sin,
                w_router, b_router, bsz, n_lat, n_ctx):
    d = x2d.shape[1]
    w_in_b = w_in.astype(BF16)
    w_out_b = w_out.astype(BF16)
    q_scale = DA_DH ** -0.5 * math.log2(math.e)
    f, q, k_all, v_all = _even_in_proj(x2d, norm1_g, mod[0], mod[1], w_in_b, cos, sin, bsz, n_lat, n_lat + n_ctx,
                                       q_scale, TM_PROJ)
    ones, zeros = jnp.ones((n_ctx, ROPE_AXIS_DIM), F32), jnp.zeros((n_ctx, ROPE_AXIS_DIM), F32)
    fc, qc, kc, vc = _even_in_proj(xc2d, norm1_g, mod_c[0], mod_c[1], w_in_b, ones, zeros, bsz, n_ctx, n_ctx,
                                   q_scale, n_ctx)
    lp = lam_p.astype(F32)
    lam = jnp.exp(jnp.sum(lp[0] * lp[1])) - jnp.exp(jnp.sum(lp[2] * lp[3])) + lam_init
    k_all = lax.dynamic_update_slice(k_all, kc, (0, n_lat, 0))
    v_all = lax.dynamic_update_slice(v_all, vc, (0, n_lat, 0))
    o = _diff_attention(lam, q, k_all, v_all, subln_g, 1.0 - lam_init, TQ_ATTN, TK_ATTN)
    oc = _diff_attention(lam, qc, kc, vc, subln_g, 1.0 - lam_init, n_ctx, n_ctx)
    fm = _fourier_mix(f.reshape(bsz, n_lat, FOURIER_W)).reshape(bsz * n_lat, FOURIER_W)
    fmc = _fourier_mix(fc.reshape(bsz, n_ctx, FOURIER_W)).reshape(bsz * n_ctx, FOURIER_W)
    lat = _even_out_proj(fm, o.reshape(bsz * n_lat, DA_W), w_out_b, x2d, mod[2], norm2_g, mod[3], mod[4],
                         w_router, b_router, n_lat, TM_PROJ)
    ctx = _even_out_proj(fmc, oc.reshape(bsz * n_ctx, DA_W), w_out_b, xc2d, mod_c[2], norm2_g, mod_c[3], mod_c[4],
                         w_router, b_router, bsz * n_ctx, n_ctx)
    return lat, ctx


def _odd_layer(x2d, xc2d, mod, mod_c, norm1_g, norm2_g, w_in, conv_w, a_log, dt_bias, norm_g, w_out,
               w_router, b_router, bsz, n_lat, n_ctx):
    d = x2d.shape[1]
    q, k, v, z, gb = _odd_in_stage(x2d.reshape(bsz, n_lat, d), norm1_g, mod[0], mod[1], w_in, conv_w, a_log,
                                   dt_bias, TM_PROJ)
    qc, kc, vc, _, gbc = _odd_in_stage(xc2d.reshape(bsz, n_ctx, d), norm1_g, mod_c[0], mod_c[1], w_in, conv_w,
                                       a_log, dt_bias, n_ctx)
    s0 = jnp.zeros((bsz, GDN_HV, GDN_DK, GDN_DV), F32)
    _, sc_f = _gdn_scan(qc, kc, vc, gbc, s0, False)
    o_f, _ = _gdn_scan(q, k, v, gb, sc_f, False)
    _, sc_b = _gdn_scan(qc, kc, vc, gbc, s0, True)
    o, _ = _gdn_scan(q, k, v, gb, sc_b, True, add=o_f, out_dtype=BF16)
    return _odd_out_proj(o.reshape(bsz * n_lat, GDN_V_W), z.reshape(bsz * n_lat, GDN_V_W), norm_g,
                         w_out.astype(BF16), x2d, mod[2], norm2_g, mod[3], mod[4], w_router, b_router, n_lat,
                         TM_PROJ)


def kernel(x, c, ctx, c_ctx, norm1_g, norm2_g, w_mod, b_mod, ev_w_in, ev_w_out, ev_lam, ev_subln_g,
           od_w_in, od_conv_w, od_a_log, od_dt_bias, od_norm_g, od_w_out,
           moe_w_router, moe_b_router, moe_w1, moe_b1, moe_w2, moe_b2, final_g):
    bsz, n_lat, d = x.shape
    n_ctx = ctx.shape[1]
    assert w_mod.shape[0] == 2, "kernel is written for one even (attention) and one odd (DeltaNet) layer"
    t_lat = bsz * n_lat
    cos, sin = _axial_rope_tables(n_lat // GRID_W)
    c_rows = jnp.zeros((8, d), F32).at[:bsz].set(c).at[bsz].set(c_ctx)
    x2d = x.reshape(t_lat, d)
    xc2d = ctx.reshape(bsz * n_ctx, d)

    mod, mod_c = _split_mod(_mod_vectors(c_rows, w_mod, b_mod, 0), bsz)
    (x2d, h2, idx, gates), (xc2d, h2c, idx_c, gates_c) = _even_layer(
        x2d, xc2d, mod, mod_c, norm1_g[0], norm2_g[0], ev_w_in[0], ev_w_out[0], ev_lam[0], ev_subln_g[0],
        _diff_lambda_init(0), cos, sin, moe_w_router[0], moe_b_router[0], bsz, n_lat, n_ctx)
    yb, pos = _moe_ffn(jnp.concatenate([h2, h2c], axis=0), jnp.concatenate([idx, idx_c], axis=0),
                       moe_w1, moe_b1[0], moe_w2, moe_b2[0], 0, TM_PROJ)
    x2d = _moe_combine(yb, pos[:t_lat], gates, x2d, mod[5], n_lat, TM_PROJ)
    xc2d = _moe_combine(yb, pos[t_lat:], gates_c, xc2d, mod_c[5], bsz * n_ctx, n_ctx)

    mod, mod_c = _split_mod(_mod_vectors(c_rows, w_mod, b_mod, 1), bsz)
    x2d, h2, idx, gates = _odd_layer(x2d, xc2d, mod, mod_c, norm1_g[1], norm2_g[1], od_w_in[0], od_conv_w[0],
                                     od_a_log[0], od_dt_bias[0], od_norm_g[0], od_w_out[0],
                                     moe_w_router[1], moe_b_router[1], bsz, n_lat, n_ctx)
    yb, pos = _moe_ffn(h2, idx, moe_w1, moe_b1[1], moe_w2, moe_b2[1], 1, TM_PROJ)
    return _moe_combine(yb, pos, gates, x2d, mod[5], n_lat, TM_PROJ, final_g=final_g).reshape(bsz, n_lat, d)
```

```python
import functools
import math

import jax
import jax.numpy as jnp
from jax import lax
from jax.experimental import pallas as pl
from jax.experimental.pallas import tpu as pltpu

D_MODEL = 1024
N_MOD = 6
NORM_EPS = 1e-6
GRID_W = 64

FOURIER_GROUPS = 4
FOURIER_GD = 64
FOURIER_W = FOURIER_GROUPS * FOURIER_GD
DA_HEADS = 6
DA_DH = 64
DA_VD = 2 * DA_DH
DA_W = DA_HEADS * DA_VD
ROPE_BASE = 10000.0
ROPE_AXIS_DIM = DA_DH // 2
SUBLN_EPS = 1e-5

GDN_HK = 8
GDN_HV = 16
GDN_DK = 128
GDN_DV = 128
GDN_QK_W = GDN_HK * GDN_DK
GDN_V_W = GDN_HV * GDN_DV
GDN_QKV_W = 2 * GDN_QK_W + GDN_V_W
GDN_MAIN_W = GDN_QKV_W + GDN_V_W
GDN_CONV = 5
GDN_CHUNK = 64

N_EXPERTS = 32
TOP_K = 4
D_FF = 1024
SWIGLU_LIMIT = 7.0
SWIGLU_ALPHA = 1.702
MOE_BLOCK = 512
MOE_FIRST_PART = 32

LANES = 128
SUBLANES_BF16 = 16
MXU_DIM = 256
VMEM_LIMIT = 56 * 1024 * 1024
BF16 = jnp.bfloat16
F32 = jnp.float32
HI = lax.Precision.HIGHEST

TM_PROJ = 512
TQ_ATTN, TK_ATTN = 512, 1280

_CP = pltpu.CompilerParams(vmem_limit_bytes=VMEM_LIMIT)


def _diff_lambda_init(layer_idx):
    return 0.8 - 0.6 * math.exp(-0.3 * layer_idx)


def _axial_rope_tables(rows):
    t = jnp.arange(rows * GRID_W, dtype=jnp.int32)
    row = (t // GRID_W).astype(F32)
    col = (t % GRID_W).astype(F32)
    inv = ROPE_BASE ** (-jnp.arange(0, ROPE_AXIS_DIM, 2, dtype=F32) / ROPE_AXIS_DIM)
    ang = jnp.concatenate([row[:, None] * inv, col[:, None] * inv], axis=-1)
    return jnp.cos(ang), jnp.sin(ang)


def _norm_mod(x, g_ref, shift_ref, scale_ref):
    h = x * lax.rsqrt(jnp.mean(x * x, axis=-1, keepdims=True) + NORM_EPS) * g_ref[...]
    return (h * (1.0 + scale_ref[0]) + shift_ref[0]).astype(BF16)


def _route(logits):
    lane = lax.broadcasted_iota(jnp.int32, logits.shape, 1)
    lane_f = lane.astype(F32)
    neg = jnp.float32(-jnp.inf)
    rest = jnp.where(lane < N_EXPERTS, logits, neg)
    idx = jnp.zeros(logits.shape, F32)
    val = jnp.full(logits.shape, neg, F32)
    for j in range(TOP_K):
        m = jnp.max(rest, axis=-1, keepdims=True)
        sel = jnp.min(jnp.where(rest == m, lane_f, float(LANES)), axis=-1, keepdims=True)
        idx = jnp.where(lane == j, sel, idx)
        val = jnp.where(lane == j, m, val)
        rest = jnp.where(lane_f == sel, neg, rest)
    e = jnp.exp(val - val[:, 0:1])
    return idx, e / jnp.sum(e, axis=-1, keepdims=True)


def _residual_norm_route(y, x_ref, gate_ref, g_ref, shift_ref, scale_ref, wr_ref, br_ref,
                         xo_ref, h_ref, idx_ref, gates_ref):
    xn = x_ref[...] + gate_ref[0] * y
    xo_ref[...] = xn
    h = _norm_mod(xn, g_ref, shift_ref, scale_ref)
    h_ref[...] = h
    idx, gates = _route(jnp.dot(h, wr_ref[...], preferred_element_type=F32) + br_ref[...])
    idx_ref[...] = idx.astype(jnp.int32)
    gates_ref[...] = gates


def _row(tm, n):
    return pl.BlockSpec((tm, n), lambda i: (i, 0))


def _full(shape):
    return pl.BlockSpec(shape, lambda i: (0,) * len(shape))


def _vec(d, rows_per_vec, tm):
    per = rows_per_vec // tm
    return pl.BlockSpec((1, 1, d), lambda i: (i // per, 0, 0))


def _router_operands(w_router, b_router):
    d = w_router.shape[0]
    wr = jnp.zeros((d, LANES), BF16).at[:, :N_EXPERTS].set(w_router.astype(BF16))
    br = jnp.zeros((1, LANES), F32).at[0, :N_EXPERTS].set(b_router)
    return wr, br


def _route_out(t, d, tm):
    specs = [_row(tm, d), _row(tm, d), _row(tm, LANES), _row(tm, LANES)]
    shapes = [jax.ShapeDtypeStruct((t, d), F32), jax.ShapeDtypeStruct((t, d), BF16),
              jax.ShapeDtypeStruct((t, LANES), jnp.int32), jax.ShapeDtypeStruct((t, LANES), F32)]
    return specs, shapes


def _mod_kernel(c_ref, w_ref, b_ref, o_ref):
    c = c_ref[...]
    s = (c * jax.nn.sigmoid(c)).astype(BF16)
    o_ref[...] = jnp.dot(s, w_ref[0].astype(BF16), preferred_element_type=F32) + b_ref[0]


def _mod_vectors(c_rows, w_mod, b_mod, layer):
    d = c_rows.shape[1]
    return pl.pallas_call(
        _mod_kernel,
        grid=(N_MOD,),
        in_specs=[pl.BlockSpec(c_rows.shape, lambda n: (0, 0)),
                  pl.BlockSpec((1, d, d), lambda n: (layer, 0, n)),
                  pl.BlockSpec((1, 1, d), lambda n: (layer, 0, n))],
        out_specs=pl.BlockSpec((c_rows.shape[0], d), lambda n: (0, n)),
        out_shape=jax.ShapeDtypeStruct((c_rows.shape[0], N_MOD * d), F32),
        compiler_params=_CP,
        name="mod_vectors",
    )(c_rows, w_mod, b_mod.reshape(b_mod.shape[0], 1, -1))


def _even_in_kernel(x_ref, g_ref, shift_ref, scale_ref, w_ref, rc_ref, rs_ref, f_ref, q_ref, k_ref, v_ref, *, q_scale):
    h = _norm_mod(x_ref[...], g_ref, shift_ref, scale_ref)
    p = jnp.dot(h, w_ref[...], preferred_element_type=F32)
    f_ref[...] = p[:, :FOURIER_W]
    rc, rs = rc_ref[...], rs_ref[...]
    lane = lax.broadcasted_iota(jnp.int32, rc.shape, 1)
    first = (lane % DA_DH) < ROPE_AXIS_DIM

    def rope(t):
        partner = jnp.where(first, pltpu.roll(t, LANES - ROPE_AXIS_DIM, 1), pltpu.roll(t, ROPE_AXIS_DIM, 1))
        return t * rc + partner * rs

    for hd in range(DA_HEADS):
        cq = slice(FOURIER_W + hd * DA_VD, FOURIER_W + (hd + 1) * DA_VD)
        ck = slice(FOURIER_W + DA_W + hd * DA_VD, FOURIER_W + DA_W + (hd + 1) * DA_VD)
        q_ref[0, :, hd * DA_VD:(hd + 1) * DA_VD] = (rope(p[:, cq]) * q_scale).astype(BF16)
        k_ref[0, :, hd * DA_VD:(hd + 1) * DA_VD] = rope(p[:, ck]).astype(BF16)
    v_ref[0] = p[:, FOURIER_W + 2 * DA_W:].astype(BF16)


def _even_in_proj(x2d, g, shift, scale, w, cos, sin, bsz, n_seq, n_keys, q_scale, tm):
    t, d = x2d.shape
    bpb = n_seq // tm
    rc = jnp.tile(cos, (1, LANES // ROPE_AXIS_DIM))
    rs = jnp.tile(jnp.concatenate([-sin, sin], axis=1), (1, LANES // DA_DH))
    tab = pl.BlockSpec((tm, LANES), lambda i: (i % bpb, 0))
    seq = pl.BlockSpec((1, tm, DA_W), lambda i: (i // bpb, i % bpb, 0))
    vec = _vec(d, n_seq * (bsz // shift.shape[0]), tm)
    return pl.pallas_call(
        functools.partial(_even_in_kernel, q_scale=q_scale),
        grid=(t // tm,),
        in_specs=[_row(tm, d), _full((1, d)), vec, vec, _full(w.shape), tab, tab],
        out_specs=[_row(tm, FOURIER_W), seq, seq, seq],
        out_shape=[jax.ShapeDtypeStruct((t, FOURIER_W), F32), jax.ShapeDtypeStruct((bsz, n_seq, DA_W), BF16),
                   jax.ShapeDtypeStruct((bsz, n_keys, DA_W), BF16), jax.ShapeDtypeStruct((bsz, n_keys, DA_W), BF16)],
        compiler_params=_CP,
        name="even_in_proj",
    )(x2d, g.reshape(1, d), shift, scale, w, rc, rs)


def _diff_attn_kernel(lam_ref, q_ref, k_ref, v_ref, g_ref, o_ref, qs_ref, s_ref, m_ref, acc_ref, *,
                      tk, n_sub, out_scale):
    tq = q_ref.shape[1]
    n_kv = k_ref.shape[1] // tk
    rb = 2 * tq // n_sub
    q = q_ref[0]
    lane = lax.broadcasted_iota(jnp.int32, q.shape, 1)
    zero = jnp.zeros_like(q)
    qs_ref[:tq] = jnp.where(lane < DA_DH, q, zero)
    qs_ref[tq:] = jnp.where(lane >= DA_DH, q, zero)
    m_ref[...] = jnp.full(m_ref.shape, -1e30, F32)
    acc_ref[...] = jnp.zeros(acc_ref.shape, F32)
    ones = jnp.ones((tk, LANES), BF16)

    def scores(i, slot):
        off = pl.multiple_of(i * tk, tk)
        k = k_ref[0, pl.ds(off, tk), :]
        s_ref[slot] = lax.dot_general(qs_ref[...], k, (((1,), (1,)), ((), ())), preferred_element_type=F32)

    def consume(i, slot):
        off = pl.multiple_of(i * tk, tk)
        v_ext = jnp.concatenate([v_ref[0, pl.ds(off, tk), :], ones], axis=1)
        for r in range(n_sub):
            rows = pl.ds(r * rb, rb)
            s = s_ref[slot, rows, :]
            m_prev = m_ref[rows, :]
            m_new = jnp.maximum(m_prev, jnp.max(s, axis=1, keepdims=True))
            alpha = jnp.exp2(m_prev - m_new)
            p = jnp.exp2(s - jnp.tile(m_new, (1, tk // LANES)))
            pv = jnp.dot(p.astype(BF16), v_ext, preferred_element_type=F32)
            acc_ref[rows, :] = acc_ref[rows, :] * jnp.tile(alpha, (1, 2)) + pv
            m_ref[rows, :] = m_new

    scores(0, 0)

    def body(j, carry):
        scores(2 * j + 1, 1)
        consume(2 * j, 0)
        scores(2 * j + 2, 0)
        consume(2 * j + 1, 1)
        return carry

    lax.fori_loop(0, (n_kv - 1) // 2, body, 0)
    consume(n_kv - 1, 0)
    acc = acc_ref[...]
    o1 = acc[:tq, :LANES] / acc[:tq, LANES:]
    o2 = acc[tq:, :LANES] / acc[tq:, LANES:]
    o = o1 - lam_ref[0] * o2
    ms = jnp.mean(o * o, axis=-1, keepdims=True)
    o = o * lax.rsqrt(ms + SUBLN_EPS) * g_ref[...] * out_scale
    o_ref[0] = o.astype(o_ref.dtype)


def _diff_attention(lam, q, k_all, v_all, subln_g, out_scale, tq, tk, n_sub=2):
    b, n, _ = q.shape
    nk = k_all.shape[1]
    assert n % tq == 0 and nk % tk == 0 and tk % MXU_DIM == 0 and (nk // tk) % 2 == 1
    grid_spec = pltpu.PrefetchScalarGridSpec(
        num_scalar_prefetch=1,
        grid=(b, DA_HEADS, n // tq),
        in_specs=[pl.BlockSpec((1, tq, DA_VD), lambda bi, hi, qi, lam_r: (bi, qi, hi)),
                  pl.BlockSpec((1, nk, DA_VD), lambda bi, hi, qi, lam_r: (bi, 0, hi)),
                  pl.BlockSpec((1, nk, DA_VD), lambda bi, hi, qi, lam_r: (bi, 0, hi)),
                  pl.BlockSpec((1, DA_VD), lambda bi, hi, qi, lam_r: (0, 0))],
        out_specs=pl.BlockSpec((1, tq, DA_VD), lambda bi, hi, qi, lam_r: (bi, qi, hi)),
        scratch_shapes=[pltpu.VMEM((2 * tq, LANES), BF16), pltpu.VMEM((2, 2 * tq, tk), F32),
                        pltpu.VMEM((2 * tq, LANES), F32), pltpu.VMEM((2 * tq, 2 * LANES), F32)],
    )
    return pl.pallas_call(
        functools.partial(_diff_attn_kernel, tk=tk, n_sub=n_sub, out_scale=out_scale),
        grid_spec=grid_spec,
        out_shape=jax.ShapeDtypeStruct((b, n, DA_W), BF16),
        compiler_params=_CP,
        name="diff_attention",
    )(lam.reshape(1), q, k_all, v_all, subln_g.reshape(1, DA_VD))


def _fft_stage1_kernel(x_ref, fch_ref, f1_ref, tc_ref, ts_ref, yr_ref, yi_ref, *, nb):
    w = FOURIER_W
    r = x_ref.shape[1]
    for bl in range(nb):
        cols = slice(bl * w, (bl + 1) * w)
        ab = jnp.dot(x_ref[0, :, cols], fch_ref[...], preferred_element_type=F32, precision=HI)
        z = jnp.concatenate([ab[:, :w], ab[:, w:]], axis=0)
        y = jnp.dot(f1_ref[...], z, preferred_element_type=F32, precision=HI)
        yr, yi = y[:r], y[r:]
        tc, ts = tc_ref[:, cols], ts_ref[:, cols]
        yr_ref[0, :, cols] = yr * tc + yi * ts
        yi_ref[0, :, cols] = yi * tc - yr * ts


def _fft_stage2_kernel(yr_ref, yi_ref, f2_ref, o_ref, *, nc):
    w = FOURIER_W
    for cl in range(nc):
        y = jnp.concatenate([yr_ref[0, cl], yi_ref[0, cl]], axis=0)
        o_ref[0, :, cl * w:(cl + 1) * w] = jnp.dot(f2_ref[...], y, preferred_element_type=F32, precision=HI)


def _fourier_mix(f, nb=8):
    bsz, n, w = f.shape
    r = math.isqrt(n)
    assert r * r == n and r % nb == 0 and w == FOURIER_W
    two_pi = 2.0 * math.pi
    k = jnp.arange(r, dtype=jnp.int32)
    ang = two_pi * ((k[:, None] * k[None, :]) % r).astype(F32) / r
    c1, s1 = jnp.cos(ang), jnp.sin(ang)
    kc = jnp.arange(FOURIER_GD, dtype=jnp.int32)
    angc = two_pi * ((kc[:, None] * kc[None, :]) % FOURIER_GD).astype(F32) / FOURIER_GD
    eye_g = jnp.eye(FOURIER_GROUPS, dtype=F32)
    fch = jnp.concatenate([jnp.kron(eye_g, jnp.cos(angc)), jnp.kron(eye_g, jnp.sin(angc))], axis=1)
    f1 = jnp.concatenate([jnp.concatenate([c1, -s1], axis=1), jnp.concatenate([-s1, -c1], axis=1)], axis=0)
    f2 = jnp.concatenate([c1, s1], axis=1) * (1.0 / math.sqrt(n * FOURIER_GD))
    angt = two_pi * ((k[:, None] * k[None, :]) % n).astype(F32) / n
    tc = jnp.repeat(jnp.cos(angt), w, axis=1)
    ts = jnp.repeat(jnp.sin(angt), w, axis=1)
    xv = f.reshape(bsz, r, r * w)
    blk = pl.BlockSpec((1, r, nb * w), lambda bi, j: (bi, 0, j))
    tab = pl.BlockSpec((r, nb * w), lambda bi, j: (0, j))
    full = lambda a: pl.BlockSpec(a.shape, lambda bi, j: (0, 0))
    yshape = jax.ShapeDtypeStruct((bsz, r, r * w), F32)
    yr, yi = pl.pallas_call(
        functools.partial(_fft_stage1_kernel, nb=nb),
        grid=(bsz, r // nb),
        in_specs=[blk, full(fch), full(f1), tab, tab],
        out_specs=[blk, blk],
        out_shape=[yshape, yshape],
        compiler_params=_CP,
        name="fft_stage1",
    )(xv, fch, f1, tc, ts)
    y4 = lambda a: a.reshape(bsz, r, r, w)
    yblk = pl.BlockSpec((1, nb, r, w), lambda bi, j: (bi, j, 0, 0))
    out = pl.pallas_call(
        functools.partial(_fft_stage2_kernel, nc=nb),
        grid=(bsz, r // nb),
        in_specs=[yblk, yblk, full(f2)],
        out_specs=blk,
        out_shape=yshape,
        compiler_params=_CP,
        name="fft_stage2",
    )(y4(yr), y4(yi), f2)
    return out.reshape(bsz, n, w)


def _even_out_kernel(fm_ref, o_ref, w_ref, x_ref, gate_ref, g_ref, shift_ref, scale_ref, wr_ref, br_ref,
                     xo_ref, h_ref, idx_ref, gates_ref):
    y = (jnp.dot(fm_ref[...].astype(BF16), w_ref[:FOURIER_W], preferred_element_type=F32)
         + jnp.dot(o_ref[...], w_ref[FOURIER_W:], preferred_element_type=F32))
    _residual_norm_route(y, x_ref, gate_ref, g_ref, shift_ref, scale_ref, wr_ref, br_ref,
                         xo_ref, h_ref, idx_ref, gates_ref)


def _even_out_proj(fm, o, w, x2d, gate, g, shift, scale, w_router, b_router, rows_per_vec, tm):
    t, d = x2d.shape
    wr, br = _router_operands(w_router, b_router)
    vec = _vec(d, rows_per_vec, tm)
    out_specs, out_shape = _route_out(t, d, tm)
    return pl.pallas_call(
        _even_out_kernel,
        grid=(t // tm,),
        in_specs=[_row(tm, FOURIER_W), _row(tm, DA_W), _full(w.shape), _row(tm, d), vec, _full((1, d)), vec, vec,
                  _full((d, LANES)), _full((1, LANES))],
        out_specs=out_specs, out_shape=out_shape,
        compiler_params=_CP,
        name="even_out_proj",
    )(fm, o, w, x2d, gate, g.reshape(1, d), shift, scale, wr, br)


def _moe_kernel(be_ref, x_ref, w1_ref, perm_ref, b1g_ref, b1l_ref, w2_ref, b2_ref, *rest):
    o_ref, w1g_s, w1l_s, w2_s = rest[-4:]
    i = pl.program_id(0)

    @pl.when((i == 0) | (be_ref[i] != be_ref[jnp.maximum(i - 1, 0)]))
    def _():
        for c in range(w1_ref.shape[3] // MXU_DIM):
            blk = jnp.dot(w1_ref[0, 0, :, c * MXU_DIM:(c + 1) * MXU_DIM].astype(BF16), perm_ref[...],
                          preferred_element_type=F32)
            w1g_s[:, c * LANES:(c + 1) * LANES] = blk[:, :LANES].astype(BF16)
            w1l_s[:, c * LANES:(c + 1) * LANES] = blk[:, LANES:].astype(BF16)
        w2_s[...] = w2_ref[0, 0].astype(BF16)

    x = x_ref[...]
    ug = jnp.dot(x, w1g_s[...], preferred_element_type=F32) + b1g_ref[0]
    ul = jnp.dot(x, w1l_s[...], preferred_element_type=F32) + b1l_ref[0]
    glu = jnp.minimum(ug, SWIGLU_LIMIT)
    lin = jnp.clip(ul, -SWIGLU_LIMIT, SWIGLU_LIMIT)
    act = glu * jax.nn.sigmoid(SWIGLU_ALPHA * glu) * (lin + 1.0)
    y = jnp.dot(act.astype(BF16), w2_s[...], preferred_element_type=F32) + b2_ref[0]
    o_ref[...] = y.astype(o_ref.dtype)


def _moe_experts(block_expert, xb, w1_all, b1g, b1l, w2_all, layer, b2, yb_prev, first_block, n_blocks_total):
    n_rows, d = xb.shape
    n_blocks = n_rows // MOE_BLOCK
    src = jnp.arange(MXU_DIM)[:, None]
    dst = jnp.arange(MXU_DIM)[None, :]
    perm = (src == jnp.where(dst < LANES, 2 * dst, 2 * (dst - LANES) + 1)).astype(BF16)
    bspec = lambda s: pl.BlockSpec((1,) + s, lambda i, be: (be[i], 0, 0))
    wspec = lambda s: pl.BlockSpec((1, 1) + s, lambda i, be: (layer, be[i], 0, 0))
    in_specs = [pl.BlockSpec((MOE_BLOCK, d), lambda i, be: (i, 0)),
                wspec((d, 2 * D_FF)), pl.BlockSpec((MXU_DIM, MXU_DIM), lambda i, be: (0, 0)),
                bspec((1, D_FF)), bspec((1, D_FF)), wspec((D_FF, d)), bspec((1, d))]
    args = [block_expert, xb, w1_all, perm, b1g, b1l, w2_all, b2]
    aliases = {}
    if yb_prev is not None:
        in_specs.append(pl.BlockSpec(memory_space=pl.ANY))
        aliases = {len(args): 0}
        args.append(yb_prev)
    grid_spec = pltpu.PrefetchScalarGridSpec(
        num_scalar_prefetch=1,
        grid=(n_blocks,),
        in_specs=in_specs,
        out_specs=pl.BlockSpec((MOE_BLOCK, d), lambda i, be: (first_block + i, 0)),
        scratch_shapes=[pltpu.VMEM((d, D_FF), BF16), pltpu.VMEM((d, D_FF), BF16), pltpu.VMEM((D_FF, d), BF16)],
    )
    return pl.pallas_call(
        _moe_kernel,
        grid_spec=grid_spec,
        out_shape=jax.ShapeDtypeStruct((n_blocks_total * MOE_BLOCK, d), BF16),
        input_output_aliases=aliases,
        compiler_params=_CP,
        name="moe_experts",
    )(*args)


def _rank_kernel(idx_ref, rank_ref, cnt_ref, carry_ref):
    i = pl.program_id(0)

    @pl.when(i == 0)
    def _():
        carry_ref[...] = jnp.zeros(carry_ref.shape, F32)

    idx = idx_ref[...]
    tm = idx.shape[0]
    lane = lax.broadcasted_iota(jnp.int32, idx.shape, 1)
    sel = [lane == idx[:, j:j + 1] for j in range(TOP_K)]
    onehot = sel[0].astype(F32)
    for j in range(1, TOP_K):
        onehot = onehot + sel[j].astype(F32)
    row = lax.broadcasted_iota(jnp.int32, (tm, tm), 0)
    col = lax.broadcasted_iota(jnp.int32, (tm, tm), 1)
    before = (row > col).astype(BF16)
    prefix = jnp.dot(before, onehot.astype(BF16), preferred_element_type=F32) + carry_ref[0:1, :]
    rank = jnp.zeros(idx.shape, F32)
    for j in range(TOP_K):
        rank = jnp.where(lane == j, jnp.sum(jnp.where(sel[j], prefix, 0.0), axis=-1, keepdims=True), rank)
    rank_ref[...] = rank.astype(jnp.int32)
    carry_ref[0:1, :] = carry_ref[0:1, :] + jnp.sum(onehot, axis=0, keepdims=True)
    cnt_ref[...] = carry_ref[...]


def _expert_ranks(idx, tm):
    t = idx.shape[0]
    rank, cnt = pl.pallas_call(
        _rank_kernel,
        grid=(t // tm,),
        in_specs=[_row(tm, LANES)],
        out_specs=[_row(tm, LANES), _full((8, LANES))],
        out_shape=[jax.ShapeDtypeStruct((t, LANES), jnp.int32), jax.ShapeDtypeStruct((8, LANES), F32)],
        scratch_shapes=[pltpu.VMEM((8, LANES), F32)],
        compiler_params=_CP,
        name="expert_ranks",
    )(idx)
    return rank, cnt[0, :N_EXPERTS].astype(jnp.int32)


def _pos_kernel(idx_ref, rank_ref, ps_ref, pos_ref):
    idx = idx_ref[...]
    lane = lax.broadcasted_iota(jnp.int32, idx.shape, 1)
    pos = rank_ref[...].astype(F32)
    for j in range(TOP_K):
        start = jnp.sum(jnp.where(lane == idx[:, j:j + 1], ps_ref[...], 0.0), axis=-1, keepdims=True)
        pos = jnp.where(lane == j, pos + start, pos)
    pos_ref[...] = pos.astype(jnp.int32)


def _expert_rows(idx, rank, pad_starts, tm):
    t = idx.shape[0]
    ps = jnp.zeros((1, LANES), F32).at[0, :N_EXPERTS].set(pad_starts.astype(F32))
    pos = pl.pallas_call(
        _pos_kernel,
        grid=(t // tm,),
        in_specs=[_row(tm, LANES), _row(tm, LANES), _full((1, LANES))],
        out_specs=_row(tm, LANES),
        out_shape=jax.ShapeDtypeStruct((t, LANES), jnp.int32),
        compiler_params=_CP,
        name="expert_rows",
    )(idx, rank, ps)
    return pos[:, :TOP_K]


def _moe_parts(n_blocks):
    parts, first, size = [], 0, max(1, n_blocks // MOE_FIRST_PART)
    while first < n_blocks:
        if n_blocks - first < 2 * size:
            size = n_blocks - first
        parts.append((first, size))
        first += size
        size *= 2
    return parts


def _moe_ffn(h, idx, w1_all, b1, w2_all, b2, layer, tm):
    n_tok, d = h.shape
    n_assign = n_tok * TOP_K
    e_flat = idx[:, :TOP_K].reshape(n_assign)
    shift_bits = (n_assign - 1).bit_length()
    assert N_EXPERTS << shift_bits < 2 ** 31
    packed = jnp.sort((e_flat << shift_bits) | jnp.arange(n_assign, dtype=jnp.int32))
    order = packed & ((1 << shift_bits) - 1)
    rank, counts = _expert_ranks(idx, tm)
    starts = jnp.cumsum(counts) - counts
    padded = (counts + MOE_BLOCK - 1) // MOE_BLOCK * MOE_BLOCK
    pad_ends = jnp.cumsum(padded)
    pad_starts = pad_ends - padded
    n_blocks = -(-(n_assign + N_EXPERTS * (MOE_BLOCK - 1)) // MOE_BLOCK)
    n_rows = n_blocks * MOE_BLOCK
    block_start = jnp.arange(n_blocks, dtype=jnp.int32) * MOE_BLOCK
    block_expert = jnp.minimum(jnp.sum(pad_ends[None, :] <= block_start[:, None], axis=1, dtype=jnp.int32),
                               N_EXPERTS - 1)
    shift = pad_starts - starts
    pos = _expert_rows(idx, rank, pad_starts, tm)
    r = jnp.arange(n_rows, dtype=jnp.int32)
    src = jnp.clip(r - jnp.repeat(shift[block_expert], MOE_BLOCK), 0, n_assign - 1)
    row_tok = order[src] // TOP_K
    yb = None
    for first, size in _moe_parts(n_blocks):
        xb = h[row_tok[first * MOE_BLOCK:(first + size) * MOE_BLOCK]]
        yb = _moe_experts(block_expert[first:first + size], xb, w1_all, b1[:, None, 0::2], b1[:, None, 1::2],
                          w2_all, layer, b2[:, None, :], yb, first, n_blocks)
    return yb, pos


def _combine_kernel(*refs, final):
    y_refs = refs[:TOP_K]
    gt_ref, x_ref, g5_ref = refs[TOP_K:TOP_K + 3]
    o_ref = refs[-1]
    gt = gt_ref[...]
    acc = y_refs[0][...].astype(F32) * gt[:, 0:1]
    for j in range(1, TOP_K):
        acc = acc + y_refs[j][...].astype(F32) * gt[:, j:j + 1]
    xn = x_ref[...] + g5_ref[0] * acc
    if final:
        fg_ref = refs[TOP_K + 3]
        xn = xn * lax.rsqrt(jnp.mean(xn * xn, axis=-1, keepdims=True) + NORM_EPS) * fg_ref[...]
    o_ref[...] = xn


def _moe_combine(yb, pos, gates, x2d, gate5, rows_per_vec, tm, final_g=None):
    t, d = x2d.shape
    ys = [yb[pos[:, j]] for j in range(TOP_K)]
    in_specs = [_row(tm, d)] * TOP_K + [_row(tm, LANES), _row(tm, d), _vec(d, rows_per_vec, tm)]
    args = ys + [gates, x2d, gate5]
    if final_g is not None:
        in_specs.append(_full((1, d)))
        args.append(final_g.reshape(1, d))
    return pl.pallas_call(
        functools.partial(_combine_kernel, final=final_g is not None),
        grid=(t // tm,),
        in_specs=in_specs,
        out_specs=_row(tm, d),
        out_shape=jax.ShapeDtypeStruct((t, d), F32),
        compiler_params=_CP,
        name="moe_combine",
    )(*args)


HALO = SUBLANES_BF16


def _odd_in_kernel(xm_ref, xp_ref, xn_ref, g_ref, shift_ref, scale_ref, wqkv_ref, wz_ref, wab_ref, cw_ref,
                   alog_ref, dtb_ref, q_ref, k_ref, v_ref, z_ref, gb_ref):
    i = pl.program_id(1)
    n_i = pl.num_programs(1)
    tm = xm_ref.shape[1]
    half = GDN_CONV // 2
    keep_prev = (i > 0).astype(BF16)
    keep_next = (i < n_i - 1).astype(BF16)
    h_main = _norm_mod(xm_ref[0], g_ref, shift_ref, scale_ref)
    h_ext = jnp.concatenate([_norm_mod(xp_ref[0], g_ref, shift_ref, scale_ref) * keep_prev, h_main,
                             _norm_mod(xn_ref[0], g_ref, shift_ref, scale_ref) * keep_next], axis=0)
    gw = MXU_DIM
    for cg in range(GDN_QKV_W // gw):
        cols = slice(cg * gw, (cg + 1) * gw)
        ext = jnp.dot(h_ext, wqkv_ref[:, cols], preferred_element_type=F32)
        acc = ext[HALO:HALO + tm] * cw_ref[half:half + 1, cols]
        for j in range(GDN_CONV):
            if j != half:
                sh = pltpu.roll(ext, (half - j) % (tm + 2 * HALO), 0)[HALO:HALO + tm]
                acc = acc + sh * cw_ref[j:j + 1, cols]
        y = acc * jax.nn.sigmoid(acc)
        for sub in range(gw // LANES):
            hd = cg * (gw // LANES) + sub
            yh = y[:, sub * LANES:(sub + 1) * LANES]
            if hd < 2 * GDN_HK:
                yh = yh * lax.rsqrt(jnp.sum(yh * yh, axis=-1, keepdims=True) + 1e-6)
                if hd < GDN_HK:
                    q_ref[0, :, hd * LANES:(hd + 1) * LANES] = yh * (GDN_DK ** -0.5)
                else:
                    k_ref[0, :, (hd - GDN_HK) * LANES:(hd - GDN_HK + 1) * LANES] = yh
            else:
                v_ref[0, :, (hd - 2 * GDN_HK) * LANES:(hd - 2 * GDN_HK + 1) * LANES] = yh
    z_ref[0] = jnp.dot(h_main, wz_ref[...], preferred_element_type=F32).astype(BF16)
    ab = jnp.dot(h_main, wab_ref[...], preferred_element_type=F32)
    xa = ab + dtb_ref[...]
    softplus = jnp.maximum(xa, 0.0) + jnp.log1p(jnp.exp(-jnp.abs(xa)))
    lane = lax.broadcasted_iota(jnp.int32, ab.shape, 1)
    gb_ref[0] = jnp.where(lane < 2 * GDN_HV, -jnp.exp(alog_ref[...]) * softplus, jax.nn.sigmoid(ab))


def _odd_in_stage(x3d, g, shift, scale, w_in, conv_w, a_log, dt_bias, tm):
    b, t, d = x3d.shape
    w_qkv = w_in[:, :GDN_QKV_W].astype(BF16)
    w_z = w_in[:, GDN_QKV_W:GDN_MAIN_W].astype(BF16)
    w_ab = jnp.zeros((d, LANES), BF16).at[:, :4 * GDN_HV].set(w_in[:, GDN_MAIN_W:].astype(BF16))
    nb = tm // HALO
    last = t // HALO - 1
    pad = lambda a: jnp.zeros((1, LANES), F32).at[0, :2 * GDN_HV].set(a.reshape(-1))
    per_vec = b // shift.shape[0]
    vec = pl.BlockSpec((1, 1, d), lambda bi, i: (bi // per_vec, 0, 0))
    seq = lambda n: pl.BlockSpec((1, tm, n), lambda bi, i: (bi, i, 0))
    full = lambda s: pl.BlockSpec(s, lambda bi, i: (0,) * len(s))
    f = lambda n, dt=F32: jax.ShapeDtypeStruct((b, t, n), dt)
    return pl.pallas_call(
        _odd_in_kernel,
        grid=(b, t // tm),
        in_specs=[seq(d),
                  pl.BlockSpec((1, HALO, d), lambda bi, i: (bi, jnp.maximum(i * nb - 1, 0), 0)),
                  pl.BlockSpec((1, HALO, d), lambda bi, i: (bi, jnp.minimum((i + 1) * nb, last), 0)),
                  full((1, d)), vec, vec, full(w_qkv.shape), full(w_z.shape), full(w_ab.shape),
                  full((GDN_CONV, GDN_QKV_W)), full((1, LANES)), full((1, LANES))],
        out_specs=[seq(GDN_QK_W), seq(GDN_QK_W), seq(GDN_V_W), seq(GDN_V_W), seq(LANES)],
        out_shape=[f(GDN_QK_W), f(GDN_QK_W), f(GDN_V_W), f(GDN_V_W, BF16), f(LANES)],
        compiler_params=_CP,
        name="odd_in_stage",
    )(x3d, x3d, x3d, g.reshape(1, d), shift, scale, w_qkv, w_z, w_ab, conv_w, pad(a_log), pad(dt_bias))


def _gdn_chunk_kernel(q_ref, k_ref, v_ref, gb_ref, s0_ref, *rest, reverse, g_lane, b_lane):
    o_ref, sfin_ref, s_ref = rest[-3:]
    add_ref = rest[0] if len(rest) == 4 else None
    c = pl.program_id(0)
    n_c = pl.num_programs(0)
    cs = GDN_CHUNK
    rep = GDN_HV // GDN_HK
    nb = q_ref.shape[0]

    @pl.when(c == 0)
    def _():
        s_ref[...] = s0_ref[...].reshape(s_ref.shape)

    row = lax.broadcasted_iota(jnp.int32, (cs, cs), 0)
    col = lax.broadcasted_iota(jnp.int32, (cs, cs), 1)
    incl = (row <= col) if reverse else (row >= col)
    strict = (row < col) if reverse else (row > col)
    eye = (row == col).astype(F32)
    incl_b = incl.astype(BF16)
    last = 0 if reverse else cs - 1
    tn = (((0,), (1,)), ((), ()))

    heads = range(GDN_HV)
    gc, gr, bc, qh, kh, vh = [], [], [], [], [], []
    for b in range(nb):
        gb = gb_ref[b]
        gb_hi = gb.astype(BF16)
        gb_lo = (gb - gb_hi.astype(F32)).astype(BF16)
        g_col = (jnp.dot(incl_b, gb_hi, preferred_element_type=F32)
                 + jnp.dot(incl_b, gb_lo, preferred_element_type=F32))
        g_row = (lax.dot_general(gb_hi, incl_b, tn, preferred_element_type=F32)
                 + lax.dot_general(gb_lo, incl_b, tn, preferred_element_type=F32))
        gc += [g_col[:, g_lane + h:g_lane + h + 1] for h in heads]
        gr += [g_row[g_lane + h:g_lane + h + 1, :] for h in heads]
        bc += [gb[:, b_lane + h:b_lane + h + 1] for h in heads]
        qh += [q_ref[b, :, h * GDN_DK:(h + 1) * GDN_DK] for h in range(GDN_HK)]
        kh += [k_ref[b, :, h * GDN_DK:(h + 1) * GDN_DK] for h in range(GDN_HK)]
        vh += [v_ref[b, :, h * GDN_DV:(h + 1) * GDN_DV] for h in heads]
    gc, gr, bc = jnp.stack(gc), jnp.stack(gr), jnp.stack(bc)
    qh, kh, vh = jnp.stack(qh), jnp.stack(kh), jnp.stack(vh)
    ge = gc[:, last:last + 1, :]
    kh_b = kh.astype(BF16)
    bnt = (((2,), (2,)), ((0,), (0,)))
    kk = lax.dot_general(kh_b, kh_b, bnt, preferred_element_type=F32)
    qk = lax.dot_general(qh.astype(BF16), kh_b, bnt, preferred_element_type=F32)
    kk, qk = jnp.repeat(kk, rep, axis=0), jnp.repeat(qk, rep, axis=0)
    qv, kv = jnp.repeat(qh, rep, axis=0), jnp.repeat(kh, rep, axis=0)

    decay = jnp.where(incl, jnp.exp(jnp.where(incl, gc - gr, 0.0)), 0.0)
    lm = jnp.where(strict, bc * kk * decay, 0.0)
    bmm = lambda a, b: jnp.einsum('hij,hjk->hik', a.astype(BF16), b.astype(BF16), preferred_element_type=F32)
    blk = lambda n: (row // n) == (col // n)
    l0 = jnp.where(blk(16), lm, 0.0)
    p = bmm(l0, l0)
    x = eye - l0
    for _ in range(2):
        xp = bmm(jnp.concatenate([x, p], axis=1), p)
        x = x + xp[:, :cs]
        p = xp[:, cs:]
    x = x + bmm(x, p)
    n = 32
    while n <= cs:
        off = jnp.where(blk(n) & ~blk(n // 2), lm, 0.0)
        x = x - bmm(bmm(x, off), x)
        n *= 2
    eg = jnp.exp(gc)
    uw = bmm(x, jnp.concatenate([vh * bc, kv * (bc * eg)], axis=2))
    u = uw[:, :, :GDN_DV]
    w = uw[:, :, GDN_DV:]
    qg = qv * eg
    intra = jnp.where(incl, qk * decay, 0.0)
    kt = kv * jnp.exp(ge - gc)
    s = s_ref[...]
    wq = bmm(jnp.concatenate([w, qg], axis=1), s)
    v_new = u - wq[:, :cs]
    o = wq[:, cs:] + bmm(intra, v_new)
    for b in range(nb):
        for h in heads:
            cols = slice(h * GDN_DV, (h + 1) * GDN_DV)
            oh = o[b * GDN_HV + h]
            if add_ref is not None:
                oh = oh + add_ref[b, :, cols]
            o_ref[b, :, cols] = oh.astype(o_ref.dtype)
    s_ref[...] = s * jnp.exp(ge) + jnp.einsum('hck,hcv->hkv', kt.astype(BF16), v_new.astype(BF16),
                                              preferred_element_type=F32)

    @pl.when(c == n_c - 1)
    def _():
        sfin_ref[...] = s_ref[...].reshape(sfin_ref.shape)


def _gdn_scan(q, k, v, gb, s0, reverse, add=None, out_dtype=F32):
    b, t, _ = q.shape
    assert t % GDN_CHUNK == 0 and GDN_CHUNK % 32 == 0
    n_c = t // GDN_CHUNK
    cm = (lambda ci: (0, n_c - 1 - ci, 0)) if reverse else (lambda ci: (0, ci, 0))
    d = 1 if reverse else 0
    smap = lambda ci: (0, 0, 0, 0)
    in_specs = [pl.BlockSpec((b, GDN_CHUNK, GDN_QK_W), cm), pl.BlockSpec((b, GDN_CHUNK, GDN_QK_W), cm),
                pl.BlockSpec((b, GDN_CHUNK, GDN_V_W), cm), pl.BlockSpec((b, GDN_CHUNK, LANES), cm),
                pl.BlockSpec((b, GDN_HV, GDN_DK, GDN_DV), smap)]
    args = [q, k, v, gb, s0]
    if add is not None:
        in_specs.append(pl.BlockSpec((b, GDN_CHUNK, GDN_V_W), cm))
        args.append(add)
    return pl.pallas_call(
        functools.partial(_gdn_chunk_kernel, reverse=reverse, g_lane=d * GDN_HV, b_lane=(2 + d) * GDN_HV),
        grid=(n_c,),
        in_specs=in_specs,
        out_specs=[pl.BlockSpec((b, GDN_CHUNK, GDN_V_W), cm), pl.BlockSpec((b, GDN_HV, GDN_DK, GDN_DV), smap)],
        out_shape=[jax.ShapeDtypeStruct((b, t, GDN_V_W), out_dtype),
                   jax.ShapeDtypeStruct((b, GDN_HV, GDN_DK, GDN_DV), F32)],
        scratch_shapes=[pltpu.VMEM((b * GDN_HV, GDN_DK, GDN_DV), F32)],
        compiler_params=_CP,
        name="gdn_scan_bwd" if reverse else "gdn_scan_fwd",
    )(*args)


def _odd_out_kernel(o_ref, z_ref, ng_ref, w_ref, x_ref, gate_ref, g_ref, shift_ref, scale_ref,
                    wr_ref, br_ref, xo_ref, h_ref, idx_ref, gates_ref):
    parts = []
    for h in range(GDN_HV):
        cols = slice(h * GDN_DV, (h + 1) * GDN_DV)
        o = o_ref[:, cols].astype(F32)
        z = z_ref[:, cols].astype(F32)
        o = o * lax.rsqrt(jnp.mean(o * o, axis=-1, keepdims=True) + NORM_EPS) * ng_ref[...]
        parts.append((o * (z * jax.nn.sigmoid(z))).astype(BF16))
    y = jnp.dot(jnp.concatenate(parts, axis=1), w_ref[...], preferred_element_type=F32)
    _residual_norm_route(y, x_ref, gate_ref, g_ref, shift_ref, scale_ref, wr_ref, br_ref,
                         xo_ref, h_ref, idx_ref, gates_ref)


def _odd_out_proj(o, z, norm_g, w, x2d, gate, g, shift, scale, w_router, b_router, rows_per_vec, tm):
    t, d = x2d.shape
    wr, br = _router_operands(w_router, b_router)
    vec = _vec(d, rows_per_vec, tm)
    out_specs, out_shape = _route_out(t, d, tm)
    return pl.pallas_call(
        _odd_out_kernel,
        grid=(t // tm,),
        in_specs=[_row(tm, GDN_V_W), _row(tm, GDN_V_W),
                  _full((1, GDN_DV)), _full(w.shape), _row(tm, d), vec, _full((1, d)), vec, vec,
                  _full((d, LANES)), _full((1, LANES))],
        out_specs=out_specs, out_shape=out_shape,
        compiler_params=_CP,
        name="odd_out_proj",
    )(o, z, norm_g.reshape(1, GDN_DV), w, x2d, gate, g.reshape(1, d), shift, scale, wr, br)


def _split_mod(mv, bsz):
    d = mv.shape[1] // N_MOD
    lat = [mv[:bsz, j * d:(j + 1) * d][:, None, :] for j in range(N_MOD)]
    ctx = [mv[bsz:bsz + 1, j * d:(j + 1) * d][:, None, :] for j in range(N_MOD)]
    return lat, ctx


def _even_layer(x2d, xc2d, mod, mod_c, norm1_g, norm2_g, w_in, w_out, lam_p, subln_g, lam_init, cos, sin,
                w_router, b_router, bsz, n_lat, n_ctx):
    d = x2d.shape[1]
    w_in_b = w_in.astype(BF16)
    w_out_b = w_out.astype(BF16)
    q_scale = DA_DH ** -0.5 * math.log2(math.e)
    f, q, k_all, v_all = _even_in_proj(x2d, norm1_g, mod[0], mod[1], w_in_b, cos, sin, bsz, n_lat, n_lat + n_ctx,
                                       q_scale, TM_PROJ)
    ones, zeros = jnp.ones((n_ctx, ROPE_AXIS_DIM), F32), jnp.zeros((n_ctx, ROPE_AXIS_DIM), F32)
    fc, qc, kc, vc = _even_in_proj(xc2d, norm1_g, mod_c[0], mod_c[1], w_in_b, ones, zeros, bsz, n_ctx, n_ctx,
                                   q_scale, n_ctx)
    lp = lam_p.astype(F32)
    lam = jnp.exp(jnp.sum(lp[0] * lp[1])) - jnp.exp(jnp.sum(lp[2] * lp[3])) + lam_init
    k_all = lax.dynamic_update_slice(k_all, kc, (0, n_lat, 0))
    v_all = lax.dynamic_update_slice(v_all, vc, (0, n_lat, 0))
    o = _diff_attention(lam, q, k_all, v_all, subln_g, 1.0 - lam_init, TQ_ATTN, TK_ATTN)
    oc = _diff_attention(lam, qc, kc, vc, subln_g, 1.0 - lam_init, n_ctx, n_ctx)
    fm = _fourier_mix(f.reshape(bsz, n_lat, FOURIER_W)).reshape(bsz * n_lat, FOURIER_W)
    fmc = _fourier_mix(fc.reshape(bsz, n_ctx, FOURIER_W)).reshape(bsz * n_ctx, FOURIER_W)
    lat = _even_out_proj(fm, o.reshape(bsz * n_lat, DA_W), w_out_b, x2d, mod[2], norm2_g, mod[3], mod[4],
                         w_router, b_router, n_lat, TM_PROJ)
    ctx = _even_out_proj(fmc, oc.reshape(bsz * n_ctx, DA_W), w_out_b, xc2d, mod_c[2], norm2_g, mod_c[3], mod_c[4],
                         w_router, b_router, bsz * n_ctx, n_ctx)
    return lat, ctx


def _odd_layer(x2d, xc2d, mod, mod_c, norm1_g, norm2_g, w_in, conv_w, a_log, dt_bias, norm_g, w_out,
               w_router, b_router, bsz, n_lat, n_ctx):
    d = x2d.shape[1]
    q, k, v, z, gb = _odd_in_stage(x2d.reshape(bsz, n_lat, d), norm1_g, mod[0], mod[1], w_in, conv_w, a_log,
                                   dt_bias, TM_PROJ)
    qc, kc, vc, _, gbc = _odd_in_stage(xc2d.reshape(bsz, n_ctx, d), norm1_g, mod_c[0], mod_c[1], w_in, conv_w,
                                       a_log, dt_bias, n_ctx)
    s0 = jnp.zeros((bsz, GDN_HV, GDN_DK, GDN_DV), F32)
    _, sc_f = _gdn_scan(qc, kc, vc, gbc, s0, False)
    o_f, _ = _gdn_scan(q, k, v, gb, sc_f, False)
    _, sc_b = _gdn_scan(qc, kc, vc, gbc, s0, True)
    o, _ = _gdn_scan(q, k, v, gb, sc_b, True, add=o_f, out_dtype=BF16)
    return _odd_out_proj(o.reshape(bsz * n_lat, GDN_V_W), z.reshape(bsz * n_lat, GDN_V_W), norm_g,
                         w_out.astype(BF16), x2d, mod[2], norm2_g, mod[3], mod[4], w_router, b_router, n_lat,
                         TM_PROJ)


def kernel(x, c, ctx, c_ctx, norm1_g, norm2_g, w_mod, b_mod, ev_w_in, ev_w_out, ev_lam, ev_subln_g,
           od_w_in, od_conv_w, od_a_log, od_dt_bias, od_norm_g, od_w_out,
           moe_w_router, moe_b_router, moe_w1, moe_b1, moe_w2, moe_b2, final_g):
    bsz, n_lat, d = x.shape
    n_ctx = ctx.shape[1]
    assert w_mod.shape[0] == 2, "kernel is written for one even (attention) and one odd (DeltaNet) layer"
    t_lat = bsz * n_lat
    cos, sin = _axial_rope_tables(n_lat // GRID_W)
    c_rows = jnp.zeros((8, d), F32).at[:bsz].set(c).at[bsz].set(c_ctx)
    x2d = x.reshape(t_lat, d)
    xc2d = ctx.reshape(bsz * n_ctx, d)

    mod, mod_c = _split_mod(_mod_vectors(c_rows, w_mod, b_mod, 0), bsz)
    (x2d, h2, idx, gates), (xc2d, h2c, idx_c, gates_c) = _even_layer(
        x2d, xc2d, mod, mod_c, norm1_g[0], norm2_g[0], ev_w_in[0], ev_w_out[0], ev_lam[0], ev_subln_g[0],
        _diff_lambda_init(0), cos, sin, moe_w_router[0], moe_b_router[0], bsz, n_lat, n_ctx)
    yb, pos = _moe_ffn(jnp.concatenate([h2, h2c], axis=0), jnp.concatenate([idx, idx_c], axis=0),
                       moe_w1, moe_b1[0], moe_w2, moe_b2[0], 0, TM_PROJ)
    x2d = _moe_combine(yb, pos[:t_lat], gates, x2d, mod[5], n_lat, TM_PROJ)
    xc2d = _moe_combine(yb, pos[t_lat:], gates_c, xc2d, mod_c[5], bsz * n_ctx, n_ctx)

    mod, mod_c = _split_mod(_mod_vectors(c_rows, w_mod, b_mod, 1), bsz)
    x2d, h2, idx, gates = _odd_layer(x2d, xc2d, mod, mod_c, norm1_g[1], norm2_g[1], od_w_in[0], od_conv_w[0],
                                     od_a_log[0], od_dt_bias[0], od_norm_g[0], od_w_out[0],
                                     moe_w_router[1], moe_b_router[1], bsz, n_lat, n_ctx)
    yb, pos = _moe_ffn(h2, idx, moe_w1, moe_b1[1], moe_w2, moe_b2[1], 1, TM_PROJ)
    return _moe_combine(yb, pos, gates, x2d, mod[5], n_lat, TM_PROJ, final_g=final_g).reshape(bsz, n_lat, d)
```

```python
import functools
import math

import jax
import jax.numpy as jnp
from jax import lax
from jax.experimental import pallas as pl
from jax.experimental.pallas import tpu as pltpu

D_MODEL = 1024
N_MOD = 6
NORM_EPS = 1e-6
GRID_W = 64

FOURIER_GROUPS = 4
FOURIER_GD = 64
FOURIER_W = FOURIER_GROUPS * FOURIER_GD
DA_HEADS = 6
DA_DH = 64
DA_VD = 2 * DA_DH
DA_W = DA_HEADS * DA_VD
ROPE_BASE = 10000.0
ROPE_AXIS_DIM = DA_DH // 2
SUBLN_EPS = 1e-5

GDN_HK = 8
GDN_HV = 16
GDN_DK = 128
GDN_DV = 128
GDN_QK_W = GDN_HK * GDN_DK
GDN_V_W = GDN_HV * GDN_DV
GDN_QKV_W = 2 * GDN_QK_W + GDN_V_W
GDN_MAIN_W = GDN_QKV_W + GDN_V_W
GDN_CONV = 5
GDN_CHUNK = 64

N_EXPERTS = 32
TOP_K = 4
D_FF = 1024
SWIGLU_LIMIT = 7.0
SWIGLU_ALPHA = 1.702
MOE_BLOCK = 512
MOE_FIRST_PART = 32

LANES = 128
SUBLANES_BF16 = 16
MXU_DIM = 256
VMEM_LIMIT = 56 * 1024 * 1024
BF16 = jnp.bfloat16
F32 = jnp.float32
HI = lax.Precision.HIGHEST

TM_PROJ = 512
TM_ELEMENTWISE = 4096
TQ_ATTN, TK_ATTN = 512, 1280

_CP = pltpu.CompilerParams(vmem_limit_bytes=VMEM_LIMIT)


def _diff_lambda_init(layer_idx):
    return 0.8 - 0.6 * math.exp(-0.3 * layer_idx)


def _axial_rope_tables(rows):
    t = jnp.arange(rows * GRID_W, dtype=jnp.int32)
    row = (t // GRID_W).astype(F32)
    col = (t % GRID_W).astype(F32)
    inv = ROPE_BASE ** (-jnp.arange(0, ROPE_AXIS_DIM, 2, dtype=F32) / ROPE_AXIS_DIM)
    ang = jnp.concatenate([row[:, None] * inv, col[:, None] * inv], axis=-1)
    return jnp.cos(ang), jnp.sin(ang)


def _norm_mod(x, g_ref, shift_ref, scale_ref):
    h = x * lax.rsqrt(jnp.mean(x * x, axis=-1, keepdims=True) + NORM_EPS) * g_ref[...]
    return (h * (1.0 + scale_ref[0]) + shift_ref[0]).astype(BF16)


def _route(logits):
    lane = lax.broadcasted_iota(jnp.int32, logits.shape, 1)
    lane_f = lane.astype(F32)
    neg = jnp.float32(-jnp.inf)
    rest = jnp.where(lane < N_EXPERTS, logits, neg)
    idx = jnp.zeros(logits.shape, F32)
    val = jnp.full(logits.shape, neg, F32)
    for j in range(TOP_K):
        m = jnp.max(rest, axis=-1, keepdims=True)
        sel = jnp.min(jnp.where(rest == m, lane_f, float(LANES)), axis=-1, keepdims=True)
        idx = jnp.where(lane == j, sel, idx)
        val = jnp.where(lane == j, m, val)
        rest = jnp.where(lane_f == sel, neg, rest)
    e = jnp.exp(val - val[:, 0:1])
    return idx, e / jnp.sum(e, axis=-1, keepdims=True)


def _residual_norm_route(y, x_ref, gate_ref, g_ref, shift_ref, scale_ref, wr_ref, br_ref,
                         xo_ref, h_ref, idx_ref, gates_ref):
    xn = x_ref[...] + gate_ref[0] * y
    xo_ref[...] = xn
    h = _norm_mod(xn, g_ref, shift_ref, scale_ref)
    h_ref[...] = h
    idx, gates = _route(jnp.dot(h, wr_ref[...], preferred_element_type=F32) + br_ref[...])
    idx_ref[...] = idx.astype(jnp.int32)
    gates_ref[...] = gates


def _row(tm, n):
    return pl.BlockSpec((tm, n), lambda i: (i, 0))


def _full(shape):
    return pl.BlockSpec(shape, lambda i: (0,) * len(shape))


def _vec(d, rows_per_vec, tm):
    per = rows_per_vec // tm
    return pl.BlockSpec((1, 1, d), lambda i: (i // per, 0, 0))


def _router_operands(w_router, b_router):
    d = w_router.shape[0]
    wr = jnp.zeros((d, LANES), BF16).at[:, :N_EXPERTS].set(w_router.astype(BF16))
    br = jnp.zeros((1, LANES), F32).at[0, :N_EXPERTS].set(b_router)
    return wr, br


def _route_out(t, d, tm):
    specs = [_row(tm, d), _row(tm, d), _row(tm, LANES), _row(tm, LANES)]
    shapes = [jax.ShapeDtypeStruct((t, d), F32), jax.ShapeDtypeStruct((t, d), BF16),
              jax.ShapeDtypeStruct((t, LANES), jnp.int32), jax.ShapeDtypeStruct((t, LANES), F32)]
    return specs, shapes


def _mod_kernel(c_ref, w_ref, b_ref, o_ref):
    c = c_ref[...]
    s = (c * jax.nn.sigmoid(c)).astype(BF16)
    o_ref[...] = jnp.dot(s, w_ref[0].astype(BF16), preferred_element_type=F32) + b_ref[0]


def _mod_vectors(c_rows, w_mod, b_mod, layer):
    d = c_rows.shape[1]
    return pl.pallas_call(
        _mod_kernel,
        grid=(N_MOD,),
        in_specs=[pl.BlockSpec(c_rows.shape, lambda n: (0, 0)),
                  pl.BlockSpec((1, d, d), lambda n: (layer, 0, n)),
                  pl.BlockSpec((1, 1, d), lambda n: (layer, 0, n))],
        out_specs=pl.BlockSpec((c_rows.shape[0], d), lambda n: (0, n)),
        out_shape=jax.ShapeDtypeStruct((c_rows.shape[0], N_MOD * d), F32),
        compiler_params=_CP,
        name="mod_vectors",
    )(c_rows, w_mod, b_mod.reshape(b_mod.shape[0], 1, -1))


def _even_in_kernel(x_ref, g_ref, shift_ref, scale_ref, w_ref, rc_ref, rs_ref, f_ref, q_ref, k_ref, v_ref, *, q_scale):
    h = _norm_mod(x_ref[...], g_ref, shift_ref, scale_ref)
    p = jnp.dot(h, w_ref[...], preferred_element_type=F32)
    f_ref[...] = p[:, :FOURIER_W]
    rc, rs = rc_ref[...], rs_ref[...]
    lane = lax.broadcasted_iota(jnp.int32, rc.shape, 1)
    first = (lane % DA_DH) < ROPE_AXIS_DIM

    def rope(t):
        partner = jnp.where(first, pltpu.roll(t, LANES - ROPE_AXIS_DIM, 1), pltpu.roll(t, ROPE_AXIS_DIM, 1))
        return t * rc + partner * rs

    for hd in range(DA_HEADS):
        cq = slice(FOURIER_W + hd * DA_VD, FOURIER_W + (hd + 1) * DA_VD)
        ck = slice(FOURIER_W + DA_W + hd * DA_VD, FOURIER_W + DA_W + (hd + 1) * DA_VD)
        q_ref[0, :, hd * DA_VD:(hd + 1) * DA_VD] = (rope(p[:, cq]) * q_scale).astype(BF16)
        k_ref[0, :, hd * DA_VD:(hd + 1) * DA_VD] = rope(p[:, ck]).astype(BF16)
    v_ref[0] = p[:, FOURIER_W + 2 * DA_W:].astype(BF16)


def _even_in_proj(x2d, g, shift, scale, w, cos, sin, bsz, n_seq, n_keys, q_scale, tm):
    t, d = x2d.shape
    bpb = n_seq // tm
    rc = jnp.tile(cos, (1, LANES // ROPE_AXIS_DIM))
    rs = jnp.tile(jnp.concatenate([-sin, sin], axis=1), (1, LANES // DA_DH))
    tab = pl.BlockSpec((tm, LANES), lambda i: (i % bpb, 0))
    seq = pl.BlockSpec((1, tm, DA_W), lambda i: (i // bpb, i % bpb, 0))
    vec = _vec(d, n_seq * (bsz // shift.shape[0]), tm)
    return pl.pallas_call(
        functools.partial(_even_in_kernel, q_scale=q_scale),
        grid=(t // tm,),
        in_specs=[_row(tm, d), _full((1, d)), vec, vec, _full(w.shape), tab, tab],
        out_specs=[_row(tm, FOURIER_W), seq, seq, seq],
        out_shape=[jax.ShapeDtypeStruct((t, FOURIER_W), F32), jax.ShapeDtypeStruct((bsz, n_seq, DA_W), BF16),
                   jax.ShapeDtypeStruct((bsz, n_keys, DA_W), BF16), jax.ShapeDtypeStruct((bsz, n_keys, DA_W), BF16)],
        compiler_params=_CP,
        name="even_in_proj",
    )(x2d, g.reshape(1, d), shift, scale, w, rc, rs)


def _diff_attn_kernel(lam_ref, q_ref, k_ref, v_ref, g_ref, o_ref, qs_ref, s_ref, m_ref, acc_ref, *,
                      tk, n_sub, out_scale):
    tq = q_ref.shape[1]
    n_kv = k_ref.shape[1] // tk
    rb = 2 * tq // n_sub
    q = q_ref[0]
    lane = lax.broadcasted_iota(jnp.int32, q.shape, 1)
    zero = jnp.zeros_like(q)
    qs_ref[:tq] = jnp.where(lane < DA_DH, q, zero)
    qs_ref[tq:] = jnp.where(lane >= DA_DH, q, zero)
    m_ref[...] = jnp.full(m_ref.shape, -1e30, F32)
    acc_ref[...] = jnp.zeros(acc_ref.shape, F32)
    ones = jnp.ones((tk, LANES), BF16)

    def scores(i, slot):
        off = pl.multiple_of(i * tk, tk)
        k = k_ref[0, pl.ds(off, tk), :]
        s_ref[slot] = lax.dot_general(qs_ref[...], k, (((1,), (1,)), ((), ())), preferred_element_type=F32)

    def consume(i, slot):
        off = pl.multiple_of(i * tk, tk)
        v_ext = jnp.concatenate([v_ref[0, pl.ds(off, tk), :], ones], axis=1)
        for r in range(n_sub):
            rows = pl.ds(r * rb, rb)
            s = s_ref[slot, rows, :]
            m_prev = m_ref[rows, :]
            m_new = jnp.maximum(m_prev, jnp.max(s, axis=1, keepdims=True))
            alpha = jnp.exp2(m_prev - m_new)
            p = jnp.exp2(s - jnp.tile(m_new, (1, tk // LANES)))
            pv = jnp.dot(p.astype(BF16), v_ext, preferred_element_type=F32)
            acc_ref[rows, :] = acc_ref[rows, :] * jnp.tile(alpha, (1, 2)) + pv
            m_ref[rows, :] = m_new

    scores(0, 0)

    def body(j, carry):
        scores(2 * j + 1, 1)
        consume(2 * j, 0)
        scores(2 * j + 2, 0)
        consume(2 * j + 1, 1)
        return carry

    lax.fori_loop(0, (n_kv - 1) // 2, body, 0)
    consume(n_kv - 1, 0)
    acc = acc_ref[...]
    o1 = acc[:tq, :LANES] / acc[:tq, LANES:]
    o2 = acc[tq:, :LANES] / acc[tq:, LANES:]
    o = o1 - lam_ref[0] * o2
    ms = jnp.mean(o * o, axis=-1, keepdims=True)
    o = o * lax.rsqrt(ms + SUBLN_EPS) * g_ref[...] * out_scale
    o_ref[0] = o.astype(o_ref.dtype)


def _diff_attention(lam, q, k_all, v_all, subln_g, out_scale, tq, tk, n_sub=2):
    b, n, _ = q.shape
    nk = k_all.shape[1]
    assert n % tq == 0 and nk % tk == 0 and tk % MXU_DIM == 0 and (nk // tk) % 2 == 1
    grid_spec = pltpu.PrefetchScalarGridSpec(
        num_scalar_prefetch=1,
        grid=(b, DA_HEADS, n // tq),
        in_specs=[pl.BlockSpec((1, tq, DA_VD), lambda bi, hi, qi, lam_r: (bi, qi, hi)),
                  pl.BlockSpec((1, nk, DA_VD), lambda bi, hi, qi, lam_r: (bi, 0, hi)),
                  pl.BlockSpec((1, nk, DA_VD), lambda bi, hi, qi, lam_r: (bi, 0, hi)),
                  pl.BlockSpec((1, DA_VD), lambda bi, hi, qi, lam_r: (0, 0))],
        out_specs=pl.BlockSpec((1, tq, DA_VD), lambda bi, hi, qi, lam_r: (bi, qi, hi)),
        scratch_shapes=[pltpu.VMEM((2 * tq, LANES), BF16), pltpu.VMEM((2, 2 * tq, tk), F32),
                        pltpu.VMEM((2 * tq, LANES), F32), pltpu.VMEM((2 * tq, 2 * LANES), F32)],
    )
    return pl.pallas_call(
        functools.partial(_diff_attn_kernel, tk=tk, n_sub=n_sub, out_scale=out_scale),
        grid_spec=grid_spec,
        out_shape=jax.ShapeDtypeStruct((b, n, DA_W), BF16),
        compiler_params=_CP,
        name="diff_attention",
    )(lam.reshape(1), q, k_all, v_all, subln_g.reshape(1, DA_VD))


def _fft_stage1_kernel(x_ref, fch_ref, f1_ref, tc_ref, ts_ref, yr_ref, yi_ref, *, nb):
    w = FOURIER_W
    r = x_ref.shape[1]
    for bl in range(nb):
        cols = slice(bl * w, (bl + 1) * w)
        ab = jnp.dot(x_ref[0, :, cols], fch_ref[...], preferred_element_type=F32, precision=HI)
        z = jnp.concatenate([ab[:, :w], ab[:, w:]], axis=0)
        y = jnp.dot(f1_ref[...], z, preferred_element_type=F32, precision=HI)
        yr, yi = y[:r], y[r:]
        tc, ts = tc_ref[:, cols], ts_ref[:, cols]
        yr_ref[0, :, cols] = yr * tc + yi * ts
        yi_ref[0, :, cols] = yi * tc - yr * ts


def _fft_stage2_kernel(yr_ref, yi_ref, f2_ref, o_ref, *, nc):
    w = FOURIER_W
    for cl in range(nc):
        y = jnp.concatenate([yr_ref[0, cl], yi_ref[0, cl]], axis=0)
        o_ref[0, :, cl * w:(cl + 1) * w] = jnp.dot(f2_ref[...], y, preferred_element_type=F32, precision=HI)


def _fourier_mix(f, nb=8):
    bsz, n, w = f.shape
    r = math.isqrt(n)
    assert r * r == n and r % nb == 0 and w == FOURIER_W
    two_pi = 2.0 * math.pi
    k = jnp.arange(r, dtype=jnp.int32)
    ang = two_pi * ((k[:, None] * k[None, :]) % r).astype(F32) / r
    c1, s1 = jnp.cos(ang), jnp.sin(ang)
    kc = jnp.arange(FOURIER_GD, dtype=jnp.int32)
    angc = two_pi * ((kc[:, None] * kc[None, :]) % FOURIER_GD).astype(F32) / FOURIER_GD
    eye_g = jnp.eye(FOURIER_GROUPS, dtype=F32)
    fch = jnp.concatenate([jnp.kron(eye_g, jnp.cos(angc)), jnp.kron(eye_g, jnp.sin(angc))], axis=1)
    f1 = jnp.concatenate([jnp.concatenate([c1, -s1], axis=1), jnp.concatenate([-s1, -c1], axis=1)], axis=0)
    f2 = jnp.concatenate([c1, s1], axis=1) * (1.0 / math.sqrt(n * FOURIER_GD))
    angt = two_pi * ((k[:, None] * k[None, :]) % n).astype(F32) / n
    tc = jnp.repeat(jnp.cos(angt), w, axis=1)
    ts = jnp.repeat(jnp.sin(angt), w, axis=1)
    xv = f.reshape(bsz, r, r * w)
    blk = pl.BlockSpec((1, r, nb * w), lambda bi, j: (bi, 0, j))
    tab = pl.BlockSpec((r, nb * w), lambda bi, j: (0, j))
    full = lambda a: pl.BlockSpec(a.shape, lambda bi, j: (0, 0))
    yshape = jax.ShapeDtypeStruct((bsz, r, r * w), F32)
    yr, yi = pl.pallas_call(
        functools.partial(_fft_stage1_kernel, nb=nb),
        grid=(bsz, r // nb),
        in_specs=[blk, full(fch), full(f1), tab, tab],
        out_specs=[blk, blk],
        out_shape=[yshape, yshape],
        compiler_params=_CP,
        name="fft_stage1",
    )(xv, fch, f1, tc, ts)
    y4 = lambda a: a.reshape(bsz, r, r, w)
    yblk = pl.BlockSpec((1, nb, r, w), lambda bi, j: (bi, j, 0, 0))
    out = pl.pallas_call(
        functools.partial(_fft_stage2_kernel, nc=nb),
        grid=(bsz, r // nb),
        in_specs=[yblk, yblk, full(f2)],
        out_specs=blk,
        out_shape=yshape,
        compiler_params=_CP,
        name="fft_stage2",
    )(y4(yr), y4(yi), f2)
    return out.reshape(bsz, n, w)


def _even_out_kernel(fm_ref, o_ref, w_ref, x_ref, gate_ref, g_ref, shift_ref, scale_ref, wr_ref, br_ref,
                     xo_ref, h_ref, idx_ref, gates_ref):
    y = (jnp.dot(fm_ref[...].astype(BF16), w_ref[:FOURIER_W], preferred_element_type=F32)
         + jnp.dot(o_ref[...], w_ref[FOURIER_W:], preferred_element_type=F32))
    _residual_norm_route(y, x_ref, gate_ref, g_ref, shift_ref, scale_ref, wr_ref, br_ref,
                         xo_ref, h_ref, idx_ref, gates_ref)


def _even_out_proj(fm, o, w, x2d, gate, g, shift, scale, w_router, b_router, rows_per_vec, tm):
    t, d = x2d.shape
    wr, br = _router_operands(w_router, b_router)
    vec = _vec(d, rows_per_vec, tm)
    out_specs, out_shape = _route_out(t, d, tm)
    return pl.pallas_call(
        _even_out_kernel,
        grid=(t // tm,),
        in_specs=[_row(tm, FOURIER_W), _row(tm, DA_W), _full(w.shape), _row(tm, d), vec, _full((1, d)), vec, vec,
                  _full((d, LANES)), _full((1, LANES))],
        out_specs=out_specs, out_shape=out_shape,
        compiler_params=_CP,
        name="even_out_proj",
    )(fm, o, w, x2d, gate, g.reshape(1, d), shift, scale, wr, br)


def _moe_kernel(be_ref, x_ref, w1_ref, perm_ref, b1g_ref, b1l_ref, w2_ref, b2_ref, *rest):
    o_ref, w1g_s, w1l_s, w2_s = rest[-4:]
    i = pl.program_id(0)

    @pl.when((i == 0) | (be_ref[i] != be_ref[jnp.maximum(i - 1, 0)]))
    def _():
        for c in range(w1_ref.shape[3] // MXU_DIM):
            blk = jnp.dot(w1_ref[0, 0, :, c * MXU_DIM:(c + 1) * MXU_DIM].astype(BF16), perm_ref[...],
                          preferred_element_type=F32)
            w1g_s[:, c * LANES:(c + 1) * LANES] = blk[:, :LANES].astype(BF16)
            w1l_s[:, c * LANES:(c + 1) * LANES] = blk[:, LANES:].astype(BF16)
        w2_s[...] = w2_ref[0, 0].astype(BF16)

    x = x_ref[...]
    ug = jnp.dot(x, w1g_s[...], preferred_element_type=F32) + b1g_ref[0]
    ul = jnp.dot(x, w1l_s[...], preferred_element_type=F32) + b1l_ref[0]
    glu = jnp.minimum(ug, SWIGLU_LIMIT)
    lin = jnp.clip(ul, -SWIGLU_LIMIT, SWIGLU_LIMIT)
    act = glu * jax.nn.sigmoid(SWIGLU_ALPHA * glu) * (lin + 1.0)
    y = jnp.dot(act.astype(BF16), w2_s[...], preferred_element_type=F32) + b2_ref[0]
    o_ref[...] = y.astype(o_ref.dtype)


def _moe_experts(block_expert, xb, w1_all, b1g, b1l, w2_all, layer, b2, yb_prev, first_block, n_blocks_total):
    n_rows, d = xb.shape
    n_blocks = n_rows // MOE_BLOCK
    src = jnp.arange(MXU_DIM)[:, None]
    dst = jnp.arange(MXU_DIM)[None, :]
    perm = (src == jnp.where(dst < LANES, 2 * dst, 2 * (dst - LANES) + 1)).astype(BF16)
    bspec = lambda s: pl.BlockSpec((1,) + s, lambda i, be: (be[i], 0, 0))
    wspec = lambda s: pl.BlockSpec((1, 1) + s, lambda i, be: (layer, be[i], 0, 0))
    in_specs = [pl.BlockSpec((MOE_BLOCK, d), lambda i, be: (i, 0)),
                wspec((d, 2 * D_FF)), pl.BlockSpec((MXU_DIM, MXU_DIM), lambda i, be: (0, 0)),
                bspec((1, D_FF)), bspec((1, D_FF)), wspec((D_FF, d)), bspec((1, d))]
    args = [block_expert, xb, w1_all, perm, b1g, b1l, w2_all, b2]
    aliases = {}
    if yb_prev is not None:
        in_specs.append(pl.BlockSpec(memory_space=pl.ANY))
        aliases = {len(args): 0}
        args.append(yb_prev)
    grid_spec = pltpu.PrefetchScalarGridSpec(
        num_scalar_prefetch=1,
        grid=(n_blocks,),
        in_specs=in_specs,
        out_specs=pl.BlockSpec((MOE_BLOCK, d), lambda i, be: (first_block + i, 0)),
        scratch_shapes=[pltpu.VMEM((d, D_FF), BF16), pltpu.VMEM((d, D_FF), BF16), pltpu.VMEM((D_FF, d), BF16)],
    )
    return pl.pallas_call(
        _moe_kernel,
        grid_spec=grid_spec,
        out_shape=jax.ShapeDtypeStruct((n_blocks_total * MOE_BLOCK, d), BF16),
        input_output_aliases=aliases,
        compiler_params=_CP,
        name="moe_experts",
    )(*args)


def _rank_kernel(idx_ref, rank_ref, cnt_ref, carry_ref):
    i = pl.program_id(0)

    @pl.when(i == 0)
    def _():
        carry_ref[...] = jnp.zeros(carry_ref.shape, F32)

    idx = idx_ref[...]
    tm = idx.shape[0]
    lane = lax.broadcasted_iota(jnp.int32, idx.shape, 1)
    sel = [lane == idx[:, j:j + 1] for j in range(TOP_K)]
    onehot = sel[0].astype(F32)
    for j in range(1, TOP_K):
        onehot = onehot + sel[j].astype(F32)
    row = lax.broadcasted_iota(jnp.int32, (tm, tm), 0)
    col = lax.broadcasted_iota(jnp.int32, (tm, tm), 1)
    before = (row > col).astype(BF16)
    prefix = jnp.dot(before, onehot.astype(BF16), preferred_element_type=F32) + carry_ref[0:1, :]
    rank = jnp.zeros(idx.shape, F32)
    for j in range(TOP_K):
        rank = jnp.where(lane == j, jnp.sum(jnp.where(sel[j], prefix, 0.0), axis=-1, keepdims=True), rank)
    rank_ref[...] = rank.astype(jnp.int32)
    carry_ref[0:1, :] = carry_ref[0:1, :] + jnp.sum(onehot, axis=0, keepdims=True)
    cnt_ref[...] = carry_ref[...]


def _expert_ranks(idx, tm):
    t = idx.shape[0]
    rank, cnt = pl.pallas_call(
        _rank_kernel,
        grid=(t // tm,),
        in_specs=[_row(tm, LANES)],
        out_specs=[_row(tm, LANES), _full((8, LANES))],
        out_shape=[jax.ShapeDtypeStruct((t, LANES), jnp.int32), jax.ShapeDtypeStruct((8, LANES), F32)],
        scratch_shapes=[pltpu.VMEM((8, LANES), F32)],
        compiler_params=_CP,
        name="expert_ranks",
    )(idx)
    return rank, cnt[0, :N_EXPERTS].astype(jnp.int32)


def _pos_kernel(idx_ref, rank_ref, ps_ref, pos_ref):
    idx = idx_ref[...]
    lane = lax.broadcasted_iota(jnp.int32, idx.shape, 1)
    pos = rank_ref[...].astype(F32)
    for j in range(TOP_K):
        start = jnp.sum(jnp.where(lane == idx[:, j:j + 1], ps_ref[...], 0.0), axis=-1, keepdims=True)
        pos = jnp.where(lane == j, pos + start, pos)
    pos_ref[...] = pos.astype(jnp.int32)


def _expert_rows(idx, rank, pad_starts):
    t = idx.shape[0]
    tm = max(m for m in range(8, TM_ELEMENTWISE + 1, 8) if t % m == 0)
    ps = jnp.zeros((1, LANES), F32).at[0, :N_EXPERTS].set(pad_starts.astype(F32))
    pos = pl.pallas_call(
        _pos_kernel,
        grid=(t // tm,),
        in_specs=[_row(tm, LANES), _row(tm, LANES), _full((1, LANES))],
        out_specs=_row(tm, LANES),
        out_shape=jax.ShapeDtypeStruct((t, LANES), jnp.int32),
        compiler_params=_CP,
        name="expert_rows",
    )(idx, rank, ps)
    return pos[:, :TOP_K]


def _moe_parts(n_blocks):
    parts, first, size = [], 0, max(1, n_blocks // MOE_FIRST_PART)
    while first < n_blocks:
        if n_blocks - first < 2 * size:
            size = n_blocks - first
        parts.append((first, size))
        first += size
        size *= 2
    return parts


def _moe_ffn(h, idx, w1_all, b1, w2_all, b2, layer, tm):
    n_tok, d = h.shape
    n_assign = n_tok * TOP_K
    e_flat = idx[:, :TOP_K].reshape(n_assign)
    shift_bits = (n_assign - 1).bit_length()
    assert N_EXPERTS << shift_bits < 2 ** 31
    packed = jnp.sort((e_flat << shift_bits) | jnp.arange(n_assign, dtype=jnp.int32))
    order = packed & ((1 << shift_bits) - 1)
    rank, counts = _expert_ranks(idx, tm)
    starts = jnp.cumsum(counts) - counts
    padded = (counts + MOE_BLOCK - 1) // MOE_BLOCK * MOE_BLOCK
    pad_ends = jnp.cumsum(padded)
    pad_starts = pad_ends - padded
    n_blocks = -(-(n_assign + N_EXPERTS * (MOE_BLOCK - 1)) // MOE_BLOCK)
    n_rows = n_blocks * MOE_BLOCK
    block_start = jnp.arange(n_blocks, dtype=jnp.int32) * MOE_BLOCK
    block_expert = jnp.minimum(jnp.sum(pad_ends[None, :] <= block_start[:, None], axis=1, dtype=jnp.int32),
                               N_EXPERTS - 1)
    shift = pad_starts - starts
    pos = _expert_rows(idx, rank, pad_starts)
    r = jnp.arange(n_rows, dtype=jnp.int32)
    src = jnp.clip(r - jnp.repeat(shift[block_expert], MOE_BLOCK), 0, n_assign - 1)
    row_tok = order[src] // TOP_K
    yb = None
    for first, size in _moe_parts(n_blocks):
        xb = h[row_tok[first * MOE_BLOCK:(first + size) * MOE_BLOCK]]
        yb = _moe_experts(block_expert[first:first + size], xb, w1_all, b1[:, None, 0::2], b1[:, None, 1::2],
                          w2_all, layer, b2[:, None, :], yb, first, n_blocks)
    return yb, pos


def _combine_kernel(*refs, final):
    y_refs = refs[:TOP_K]
    gt_ref, x_ref, g5_ref = refs[TOP_K:TOP_K + 3]
    o_ref = refs[-1]
    gt = gt_ref[...]
    acc = y_refs[0][...].astype(F32) * gt[:, 0:1]
    for j in range(1, TOP_K):
        acc = acc + y_refs[j][...].astype(F32) * gt[:, j:j + 1]
    xn = x_ref[...] + g5_ref[0] * acc
    if final:
        fg_ref = refs[TOP_K + 3]
        xn = xn * lax.rsqrt(jnp.mean(xn * xn, axis=-1, keepdims=True) + NORM_EPS) * fg_ref[...]
    o_ref[...] = xn


def _moe_combine(yb, pos, gates, x2d, gate5, rows_per_vec, tm, final_g=None):
    t, d = x2d.shape
    ys = [yb[pos[:, j]] for j in range(TOP_K)]
    in_specs = [_row(tm, d)] * TOP_K + [_row(tm, LANES), _row(tm, d), _vec(d, rows_per_vec, tm)]
    args = ys + [gates, x2d, gate5]
    if final_g is not None:
        in_specs.append(_full((1, d)))
        args.append(final_g.reshape(1, d))
    return pl.pallas_call(
        functools.partial(_combine_kernel, final=final_g is not None),
        grid=(t // tm,),
        in_specs=in_specs,
        out_specs=_row(tm, d),
        out_shape=jax.ShapeDtypeStruct((t, d), F32),
        compiler_params=_CP,
        name="moe_combine",
    )(*args)


HALO = SUBLANES_BF16


def _odd_in_kernel(xm_ref, xp_ref, xn_ref, g_ref, shift_ref, scale_ref, wqkv_ref, wz_ref, wab_ref, cw_ref,
                   alog_ref, dtb_ref, q_ref, k_ref, v_ref, z_ref, gb_ref):
    i = pl.program_id(1)
    n_i = pl.num_programs(1)
    tm = xm_ref.shape[1]
    half = GDN_CONV // 2
    keep_prev = (i > 0).astype(BF16)
    keep_next = (i < n_i - 1).astype(BF16)
    h_main = _norm_mod(xm_ref[0], g_ref, shift_ref, scale_ref)
    h_ext = jnp.concatenate([_norm_mod(xp_ref[0], g_ref, shift_ref, scale_ref) * keep_prev, h_main,
                             _norm_mod(xn_ref[0], g_ref, shift_ref, scale_ref) * keep_next], axis=0)
    gw = MXU_DIM
    for cg in range(GDN_QKV_W // gw):
        cols = slice(cg * gw, (cg + 1) * gw)
        ext = jnp.dot(h_ext, wqkv_ref[:, cols], preferred_element_type=F32)
        acc = ext[HALO:HALO + tm] * cw_ref[half:half + 1, cols]
        for j in range(GDN_CONV):
            if j != half:
                sh = pltpu.roll(ext, (half - j) % (tm + 2 * HALO), 0)[HALO:HALO + tm]
                acc = acc + sh * cw_ref[j:j + 1, cols]
        y = acc * jax.nn.sigmoid(acc)
        for sub in range(gw // LANES):
            hd = cg * (gw // LANES) + sub
            yh = y[:, sub * LANES:(sub + 1) * LANES]
            if hd < 2 * GDN_HK:
                yh = yh * lax.rsqrt(jnp.sum(yh * yh, axis=-1, keepdims=True) + 1e-6)
                if hd < GDN_HK:
                    q_ref[0, :, hd * LANES:(hd + 1) * LANES] = yh * (GDN_DK ** -0.5)
                else:
                    k_ref[0, :, (hd - GDN_HK) * LANES:(hd - GDN_HK + 1) * LANES] = yh
            else:
                v_ref[0, :, (hd - 2 * GDN_HK) * LANES:(hd - 2 * GDN_HK + 1) * LANES] = yh
    z_ref[0] = jnp.dot(h_main, wz_ref[...], preferred_element_type=F32).astype(BF16)
    ab = jnp.dot(h_main, wab_ref[...], preferred_element_type=F32)
    xa = ab + dtb_ref[...]
    softplus = jnp.maximum(xa, 0.0) + jnp.log1p(jnp.exp(-jnp.abs(xa)))
    lane = lax.broadcasted_iota(jnp.int32, ab.shape, 1)
    gb_ref[0] = jnp.where(lane < 2 * GDN_HV, -jnp.exp(alog_ref[...]) * softplus, jax.nn.sigmoid(ab))


def _odd_in_stage(x3d, g, shift, scale, w_in, conv_w, a_log, dt_bias, tm):
    b, t, d = x3d.shape
    w_qkv = w_in[:, :GDN_QKV_W].astype(BF16)
    w_z = w_in[:, GDN_QKV_W:GDN_MAIN_W].astype(BF16)
    w_ab = jnp.zeros((d, LANES), BF16).at[:, :4 * GDN_HV].set(w_in[:, GDN_MAIN_W:].astype(BF16))
    nb = tm // HALO
    last = t // HALO - 1
    pad = lambda a: jnp.zeros((1, LANES), F32).at[0, :2 * GDN_HV].set(a.reshape(-1))
    per_vec = b // shift.shape[0]
    vec = pl.BlockSpec((1, 1, d), lambda bi, i: (bi // per_vec, 0, 0))
    seq = lambda n: pl.BlockSpec((1, tm, n), lambda bi, i: (bi, i, 0))
    full = lambda s: pl.BlockSpec(s, lambda bi, i: (0,) * len(s))
    f = lambda n, dt=F32: jax.ShapeDtypeStruct((b, t, n), dt)
    return pl.pallas_call(
        _odd_in_kernel,
        grid=(b, t // tm),
        in_specs=[seq(d),
                  pl.BlockSpec((1, HALO, d), lambda bi, i: (bi, jnp.maximum(i * nb - 1, 0), 0)),
                  pl.BlockSpec((1, HALO, d), lambda bi, i: (bi, jnp.minimum((i + 1) * nb, last), 0)),
                  full((1, d)), vec, vec, full(w_qkv.shape), full(w_z.shape), full(w_ab.shape),
                  full((GDN_CONV, GDN_QKV_W)), full((1, LANES)), full((1, LANES))],
        out_specs=[seq(GDN_QK_W), seq(GDN_QK_W), seq(GDN_V_W), seq(GDN_V_W), seq(LANES)],
        out_shape=[f(GDN_QK_W), f(GDN_QK_W), f(GDN_V_W), f(GDN_V_W, BF16), f(LANES)],
        compiler_params=_CP,
        name="odd_in_stage",
    )(x3d, x3d, x3d, g.reshape(1, d), shift, scale, w_qkv, w_z, w_ab, conv_w, pad(a_log), pad(dt_bias))


def _gdn_chunk_kernel(q_ref, k_ref, v_ref, gb_ref, s0_ref, *rest, reverse, g_lane, b_lane):
    o_ref, sfin_ref, s_ref = rest[-3:]
    add_ref = rest[0] if len(rest) == 4 else None
    c = pl.program_id(0)
    n_c = pl.num_programs(0)
    cs = GDN_CHUNK
    rep = GDN_HV // GDN_HK
    nb = q_ref.shape[0]

    @pl.when(c == 0)
    def _():
        s_ref[...] = s0_ref[...].reshape(s_ref.shape)

    row = lax.broadcasted_iota(jnp.int32, (cs, cs), 0)
    col = lax.broadcasted_iota(jnp.int32, (cs, cs), 1)
    incl = (row <= col) if reverse else (row >= col)
    strict = (row < col) if reverse else (row > col)
    eye = (row == col).astype(F32)
    incl_b = incl.astype(BF16)
    last = 0 if reverse else cs - 1
    tn = (((0,), (1,)), ((), ()))

    heads = range(GDN_HV)
    gc, gr, bc, qh, kh, vh = [], [], [], [], [], []
    for b in range(nb):
        gb = gb_ref[b]
        gb_hi = gb.astype(BF16)
        gb_lo = (gb - gb_hi.astype(F32)).astype(BF16)
        g_col = (jnp.dot(incl_b, gb_hi, preferred_element_type=F32)
                 + jnp.dot(incl_b, gb_lo, preferred_element_type=F32))
        g_row = (lax.dot_general(gb_hi, incl_b, tn, preferred_element_type=F32)
                 + lax.dot_general(gb_lo, incl_b, tn, preferred_element_type=F32))
        gc += [g_col[:, g_lane + h:g_lane + h + 1] for h in heads]
        gr += [g_row[g_lane + h:g_lane + h + 1, :] for h in heads]
        bc += [gb[:, b_lane + h:b_lane + h + 1] for h in heads]
        qh += [q_ref[b, :, h * GDN_DK:(h + 1) * GDN_DK] for h in range(GDN_HK)]
        kh += [k_ref[b, :, h * GDN_DK:(h + 1) * GDN_DK] for h in range(GDN_HK)]
        vh += [v_ref[b, :, h * GDN_DV:(h + 1) * GDN_DV] for h in heads]
    gc, gr, bc = jnp.stack(gc), jnp.stack(gr), jnp.stack(bc)
    qh, kh, vh = jnp.stack(qh), jnp.stack(kh), jnp.stack(vh)
    ge = gc[:, last:last + 1, :]
    kh_b = kh.astype(BF16)
    bnt = (((2,), (2,)), ((0,), (0,)))
    kk = lax.dot_general(kh_b, kh_b, bnt, preferred_element_type=F32)
    qk = lax.dot_general(qh.astype(BF16), kh_b, bnt, preferred_element_type=F32)
    kk, qk = jnp.repeat(kk, rep, axis=0), jnp.repeat(qk, rep, axis=0)
    qv, kv = jnp.repeat(qh, rep, axis=0), jnp.repeat(kh, rep, axis=0)

    decay = jnp.where(incl, jnp.exp(jnp.where(incl, gc - gr, 0.0)), 0.0)
    lm = jnp.where(strict, bc * kk * decay, 0.0)
    bmm = lambda a, b: jnp.einsum('hij,hjk->hik', a.astype(BF16), b.astype(BF16), preferred_element_type=F32)
    blk = lambda n: (row // n) == (col // n)
    l0 = jnp.where(blk(16), lm, 0.0)
    p = bmm(l0, l0)
    x = eye - l0
    for _ in range(2):
        xp = bmm(jnp.concatenate([x, p], axis=1), p)
        x = x + xp[:, :cs]
        p = xp[:, cs:]
    x = x + bmm(x, p)
    n = 32
    while n <= cs:
        off = jnp.where(blk(n) & ~blk(n // 2), lm, 0.0)
        x = x - bmm(bmm(x, off), x)
        n *= 2
    eg = jnp.exp(gc)
    uw = bmm(x, jnp.concatenate([vh * bc, kv * (bc * eg)], axis=2))
    u = uw[:, :, :GDN_DV]
    w = uw[:, :, GDN_DV:]
    qg = qv * eg
    intra = jnp.where(incl, qk * decay, 0.0)
    kt = kv * jnp.exp(ge - gc)
    s = s_ref[...]
    wq = bmm(jnp.concatenate([w, qg], axis=1), s)
    v_new = u - wq[:, :cs]
    o = wq[:, cs:] + bmm(intra, v_new)
    for b in range(nb):
        for h in heads:
            cols = slice(h * GDN_DV, (h + 1) * GDN_DV)
            oh = o[b * GDN_HV + h]
            if add_ref is not None:
                oh = oh + add_ref[b, :, cols]
            o_ref[b, :, cols] = oh.astype(o_ref.dtype)
    s_ref[...] = s * jnp.exp(ge) + jnp.einsum('hck,hcv->hkv', kt.astype(BF16), v_new.astype(BF16),
                                              preferred_element_type=F32)

    @pl.when(c == n_c - 1)
    def _():
        sfin_ref[...] = s_ref[...].reshape(sfin_ref.shape)


def _gdn_scan(q, k, v, gb, s0, reverse, add=None, out_dtype=F32):
    b, t, _ = q.shape
    assert t % GDN_CHUNK == 0 and GDN_CHUNK % 32 == 0
    n_c = t // GDN_CHUNK
    cm = (lambda ci: (0, n_c - 1 - ci, 0)) if reverse else (lambda ci: (0, ci, 0))
    d = 1 if reverse else 0
    smap = lambda ci: (0, 0, 0, 0)
    in_specs = [pl.BlockSpec((b, GDN_CHUNK, GDN_QK_W), cm), pl.BlockSpec((b, GDN_CHUNK, GDN_QK_W), cm),
                pl.BlockSpec((b, GDN_CHUNK, GDN_V_W), cm), pl.BlockSpec((b, GDN_CHUNK, LANES), cm),
                pl.BlockSpec((b, GDN_HV, GDN_DK, GDN_DV), smap)]
    args = [q, k, v, gb, s0]
    if add is not None:
        in_specs.append(pl.BlockSpec((b, GDN_CHUNK, GDN_V_W), cm))
        args.append(add)
    return pl.pallas_call(
        functools.partial(_gdn_chunk_kernel, reverse=reverse, g_lane=d * GDN_HV, b_lane=(2 + d) * GDN_HV),
        grid=(n_c,),
        in_specs=in_specs,
        out_specs=[pl.BlockSpec((b, GDN_CHUNK, GDN_V_W), cm), pl.BlockSpec((b, GDN_HV, GDN_DK, GDN_DV), smap)],
        out_shape=[jax.ShapeDtypeStruct((b, t, GDN_V_W), out_dtype),
                   jax.ShapeDtypeStruct((b, GDN_HV, GDN_DK, GDN_DV), F32)],
        scratch_shapes=[pltpu.VMEM((b * GDN_HV, GDN_DK, GDN_DV), F32)],
        compiler_params=_CP,
        name="gdn_scan_bwd" if reverse else "gdn_scan_fwd",
    )(*args)


def _odd_out_kernel(o_ref, z_ref, ng_ref, w_ref, x_ref, gate_ref, g_ref, shift_ref, scale_ref,
                    wr_ref, br_ref, xo_ref, h_ref, idx_ref, gates_ref):
    parts = []
    for h in range(GDN_HV):
        cols = slice(h * GDN_DV, (h + 1) * GDN_DV)
        o = o_ref[:, cols].astype(F32)
        z = z_ref[:, cols].astype(F32)
        o = o * lax.rsqrt(jnp.mean(o * o, axis=-1, keepdims=True) + NORM_EPS) * ng_ref[...]
        parts.append((o * (z * jax.nn.sigmoid(z))).astype(BF16))
    y = jnp.dot(jnp.concatenate(parts, axis=1), w_ref[...], preferred_element_type=F32)
    _residual_norm_route(y, x_ref, gate_ref, g_ref, shift_ref, scale_ref, wr_ref, br_ref,
                         xo_ref, h_ref, idx_ref, gates_ref)


def _odd_out_proj(o, z, norm_g, w, x2d, gate, g, shift, scale, w_router, b_router, rows_per_vec, tm):
    t, d = x2d.shape
    wr, br = _router_operands(w_router, b_router)
    vec = _vec(d, rows_per_vec, tm)
    out_specs, out_shape = _route_out(t, d, tm)
    return pl.pallas_call(
        _odd_out_kernel,
        grid=(t // tm,),
        in_specs=[_row(tm, GDN_V_W), _row(tm, GDN_V_W),
                  _full((1, GDN_DV)), _full(w.shape), _row(tm, d), vec, _full((1, d)), vec, vec,
                  _full((d, LANES)), _full((1, LANES))],
        out_specs=out_specs, out_shape=out_shape,
        compiler_params=_CP,
        name="odd_out_proj",
    )(o, z, norm_g.reshape(1, GDN_DV), w, x2d, gate, g.reshape(1, d), shift, scale, wr, br)


def _split_mod(mv, bsz):
    d = mv.shape[1] // N_MOD
    lat = [mv[:bsz, j * d:(j + 1) * d][:, None, :] for j in range(N_MOD)]
    ctx = [mv[bsz:bsz + 1, j * d:(j + 1) * d][:, None, :] for j in range(N_MOD)]
    return lat, ctx


def _even_layer(x2d, xc2d, mod, mod_c, norm1_g, norm2_g, w_in, w_out, lam_p, subln_g, lam_init, cos, sin,
                w_router, b_router, bsz, n_lat, n_ctx):
    d = x2d.shape[1]
    w_in_b = w_in.astype(BF16)
    w_out_b = w_out.astype(BF16)
    q_scale = DA_DH ** -0.5 * math.log2(math.e)
    f, q, k_all, v_all = _even_in_proj(x2d, norm1_g, mod[0], mod[1], w_in_b, cos, sin, bsz, n_lat, n_lat + n_ctx,
                                       q_scale, TM_PROJ)
    ones, zeros = jnp.ones((n_ctx, ROPE_AXIS_DIM), F32), jnp.zeros((n_ctx, ROPE_AXIS_DIM), F32)
    fc, qc, kc, vc = _even_in_proj(xc2d, norm1_g, mod_c[0], mod_c[1], w_in_b, ones, zeros, bsz, n_ctx, n_ctx,
                                   q_scale, n_ctx)
    lp = lam_p.astype(F32)
    lam = jnp.exp(jnp.sum(lp[0] * lp[1])) - jnp.exp(jnp.sum(lp[2] * lp[3])) + lam_init
    k_all = lax.dynamic_update_slice(k_all, kc, (0, n_lat, 0))
    v_all = lax.dynamic_update_slice(v_all, vc, (0, n_lat, 0))
    o = _diff_attention(lam, q, k_all, v_all, subln_g, 1.0 - lam_init, TQ_ATTN, TK_ATTN)
    oc = _diff_attention(lam, qc, kc, vc, subln_g, 1.0 - lam_init, n_ctx, n_ctx)
    fm = _fourier_mix(f.reshape(bsz, n_lat, FOURIER_W)).reshape(bsz * n_lat, FOURIER_W)
    fmc = _fourier_mix(fc.reshape(bsz, n_ctx, FOURIER_W)).reshape(bsz * n_ctx, FOURIER_W)
    lat = _even_out_proj(fm, o.reshape(bsz * n_lat, DA_W), w_out_b, x2d, mod[2], norm2_g, mod[3], mod[4],
                         w_router, b_router, n_lat, TM_PROJ)
    ctx = _even_out_proj(fmc, oc.reshape(bsz * n_ctx, DA_W), w_out_b, xc2d, mod_c[2], norm2_g, mod_c[3], mod_c[4],
                         w_router, b_router, bsz * n_ctx, n_ctx)
    return lat, ctx


def _odd_layer(x2d, xc2d, mod, mod_c, norm1_g, norm2_g, w_in, conv_w, a_log, dt_bias, norm_g, w_out,
               w_router, b_router, bsz, n_lat, n_ctx):
    d = x2d.shape[1]
    q, k, v, z, gb = _odd_in_stage(x2d.reshape(bsz, n_lat, d), norm1_g, mod[0], mod[1], w_in, conv_w, a_log,
                                   dt_bias, TM_PROJ)
    qc, kc, vc, _, gbc = _odd_in_stage(xc2d.reshape(bsz, n_ctx, d), norm1_g, mod_c[0], mod_c[1], w_in, conv_w,
                                       a_log, dt_bias, n_ctx)
    s0 = jnp.zeros((bsz, GDN_HV, GDN_DK, GDN_DV), F32)
    _, sc_f = _gdn_scan(qc, kc, vc, gbc, s0, False)
    o_f, _ = _gdn_scan(q, k, v, gb, sc_f, False)
    _, sc_b = _gdn_scan(qc, kc, vc, gbc, s0, True)
    o, _ = _gdn_scan(q, k, v, gb, sc_b, True, add=o_f, out_dtype=BF16)
    return _odd_out_proj(o.reshape(bsz * n_lat, GDN_V_W), z.reshape(bsz * n_lat, GDN_V_W), norm_g,
                         w_out.astype(BF16), x2d, mod[2], norm2_g, mod[3], mod[4], w_router, b_router, n_lat,
                         TM_PROJ)


def kernel(x, c, ctx, c_ctx, norm1_g, norm2_g, w_mod, b_mod, ev_w_in, ev_w_out, ev_lam, ev_subln_g,
           od_w_in, od_conv_w, od_a_log, od_dt_bias, od_norm_g, od_w_out,
           moe_w_router, moe_b_router, moe_w1, moe_b1, moe_w2, moe_b2, final_g):
    bsz, n_lat, d = x.shape
    n_ctx = ctx.shape[1]
    assert w_mod.shape[0] == 2, "kernel is written for one even (attention) and one odd (DeltaNet) layer"
    t_lat = bsz * n_lat
    cos, sin = _axial_rope_tables(n_lat // GRID_W)
    c_rows = jnp.zeros((8, d), F32).at[:bsz].set(c).at[bsz].set(c_ctx)
    x2d = x.reshape(t_lat, d)
    xc2d = ctx.reshape(bsz * n_ctx, d)

    mod, mod_c = _split_mod(_mod_vectors(c_rows, w_mod, b_mod, 0), bsz)
    (x2d, h2, idx, gates), (xc2d, h2c, idx_c, gates_c) = _even_layer(
        x2d, xc2d, mod, mod_c, norm1_g[0], norm2_g[0], ev_w_in[0], ev_w_out[0], ev_lam[0], ev_subln_g[0],
        _diff_lambda_init(0), cos, sin, moe_w_router[0], moe_b_router[0], bsz, n_lat, n_ctx)
    yb, pos = _moe_ffn(jnp.concatenate([h2, h2c], axis=0), jnp.concatenate([idx, idx_c], axis=0),
                       moe_w1, moe_b1[0], moe_w2, moe_b2[0], 0, TM_PROJ)
    x2d = _moe_combine(yb, pos[:t_lat], gates, x2d, mod[5], n_lat, TM_PROJ)
    xc2d = _moe_combine(yb, pos[t_lat:], gates_c, xc2d, mod_c[5], bsz * n_ctx, n_ctx)

    mod, mod_c = _split_mod(_mod_vectors(c_rows, w_mod, b_mod, 1), bsz)
    x2d, h2, idx, gates = _odd_layer(x2d, xc2d, mod, mod_c, norm1_g[1], norm2_g[1], od_w_in[0], od_conv_w[0],
                                     od_a_log[0], od_dt_bias[0], od_norm_g[0], od_w_out[0],
                                     moe_w_router[1], moe_b_router[1], bsz, n_lat, n_ctx)
    yb, pos = _moe_ffn(h2, idx, moe_w1, moe_b1[1], moe_w2, moe_b2[1], 1, TM_PROJ)
    return _moe_combine(yb, pos, gates, x2d, mod[5], n_lat, TM_PROJ, final_g=final_g).reshape(bsz, n_lat, d)
```

```python
import functools
import math

import jax
import jax.numpy as jnp
from jax import lax
from jax.experimental import pallas as pl
from jax.experimental.pallas import tpu as pltpu

D_MODEL = 1024
N_MOD = 6
NORM_EPS = 1e-6
GRID_W = 64

FOURIER_GROUPS = 4
FOURIER_GD = 64
FOURIER_W = FOURIER_GROUPS * FOURIER_GD
DA_HEADS = 6
DA_DH = 64
DA_VD = 2 * DA_DH
DA_W = DA_HEADS * DA_VD
ROPE_BASE = 10000.0
ROPE_AXIS_DIM = DA_DH // 2
SUBLN_EPS = 1e-5

GDN_HK = 8
GDN_HV = 16
GDN_DK = 128
GDN_DV = 128
GDN_QK_W = GDN_HK * GDN_DK
GDN_V_W = GDN_HV * GDN_DV
GDN_QKV_W = 2 * GDN_QK_W + GDN_V_W
GDN_MAIN_W = GDN_QKV_W + GDN_V_W
GDN_CONV = 5
GDN_CHUNK = 64

N_EXPERTS = 32
TOP_K = 4
D_FF = 1024
SWIGLU_LIMIT = 7.0
SWIGLU_ALPHA = 1.702
MOE_BLOCK = 512
MOE_FIRST_PART = 32

LANES = 128
SUBLANES_BF16 = 16
MXU_DIM = 256
VMEM_LIMIT = 56 * 1024 * 1024
BF16 = jnp.bfloat16
F32 = jnp.float32
HI = lax.Precision.HIGHEST

TM_PROJ = 512
TM_ELEMENTWISE = 4096
TQ_ATTN, TK_ATTN = 512, 1280

_CP = pltpu.CompilerParams(vmem_limit_bytes=VMEM_LIMIT)


def _diff_lambda_init(layer_idx):
    return 0.8 - 0.6 * math.exp(-0.3 * layer_idx)


def _axial_rope_tables(rows):
    t = jnp.arange(rows * GRID_W, dtype=jnp.int32)
    row = (t // GRID_W).astype(F32)
    col = (t % GRID_W).astype(F32)
    inv = ROPE_BASE ** (-jnp.arange(0, ROPE_AXIS_DIM, 2, dtype=F32) / ROPE_AXIS_DIM)
    ang = jnp.concatenate([row[:, None] * inv, col[:, None] * inv], axis=-1)
    return jnp.cos(ang), jnp.sin(ang)


def _norm_mod(x, g_ref, shift_ref, scale_ref):
    h = x * lax.rsqrt(jnp.mean(x * x, axis=-1, keepdims=True) + NORM_EPS) * g_ref[...]
    return (h * (1.0 + scale_ref[0]) + shift_ref[0]).astype(BF16)


def _route(logits):
    lane = lax.broadcasted_iota(jnp.int32, logits.shape, 1)
    lane_f = lane.astype(F32)
    neg = jnp.float32(-jnp.inf)
    rest = jnp.where(lane < N_EXPERTS, logits, neg)
    idx = jnp.zeros(logits.shape, F32)
    val = jnp.full(logits.shape, neg, F32)
    for j in range(TOP_K):
        m = jnp.max(rest, axis=-1, keepdims=True)
        sel = jnp.min(jnp.where(rest == m, lane_f, float(LANES)), axis=-1, keepdims=True)
        idx = jnp.where(lane == j, sel, idx)
        val = jnp.where(lane == j, m, val)
        rest = jnp.where(lane_f == sel, neg, rest)
    e = jnp.exp(val - val[:, 0:1])
    return idx, e / jnp.sum(e, axis=-1, keepdims=True)


def _residual_norm_route(y, x_ref, gate_ref, g_ref, shift_ref, scale_ref, wr_ref, br_ref,
                         xo_ref, h_ref, idx_ref, gates_ref):
    xn = x_ref[...] + gate_ref[0] * y
    xo_ref[...] = xn
    h = _norm_mod(xn, g_ref, shift_ref, scale_ref)
    h_ref[...] = h
    idx, gates = _route(jnp.dot(h, wr_ref[...], preferred_element_type=F32) + br_ref[...])
    idx_ref[...] = idx.astype(jnp.int32)
    gates_ref[...] = gates


def _row(tm, n):
    return pl.BlockSpec((tm, n), lambda i: (i, 0))


def _full(shape):
    return pl.BlockSpec(shape, lambda i: (0,) * len(shape))


def _vec(d, rows_per_vec, tm):
    per = rows_per_vec // tm
    return pl.BlockSpec((1, 1, d), lambda i: (i // per, 0, 0))


def _router_operands(w_router, b_router):
    d = w_router.shape[0]
    wr = jnp.zeros((d, LANES), BF16).at[:, :N_EXPERTS].set(w_router.astype(BF16))
    br = jnp.zeros((1, LANES), F32).at[0, :N_EXPERTS].set(b_router)
    return wr, br


def _route_out(t, d, tm):
    specs = [_row(tm, d), _row(tm, d), _row(tm, LANES), _row(tm, LANES)]
    shapes = [jax.ShapeDtypeStruct((t, d), F32), jax.ShapeDtypeStruct((t, d), BF16),
              jax.ShapeDtypeStruct((t, LANES), jnp.int32), jax.ShapeDtypeStruct((t, LANES), F32)]
    return specs, shapes


def _mod_kernel(c_ref, w_ref, b_ref, o_ref):
    c = c_ref[...]
    s = (c * jax.nn.sigmoid(c)).astype(BF16)
    o_ref[...] = jnp.dot(s, w_ref[0].astype(BF16), preferred_element_type=F32) + b_ref[0]


def _mod_vectors(c_rows, w_mod, b_mod, layer):
    d = c_rows.shape[1]
    return pl.pallas_call(
        _mod_kernel,
        grid=(N_MOD,),
        in_specs=[pl.BlockSpec(c_rows.shape, lambda n: (0, 0)),
                  pl.BlockSpec((1, d, d), lambda n: (layer, 0, n)),
                  pl.BlockSpec((1, 1, d), lambda n: (layer, 0, n))],
        out_specs=pl.BlockSpec((c_rows.shape[0], d), lambda n: (0, n)),
        out_shape=jax.ShapeDtypeStruct((c_rows.shape[0], N_MOD * d), F32),
        compiler_params=_CP,
        name="mod_vectors",
    )(c_rows, w_mod, b_mod.reshape(b_mod.shape[0], 1, -1))


def _even_in_kernel(x_ref, g_ref, shift_ref, scale_ref, w_ref, rc_ref, rs_ref, f_ref, q_ref, k_ref, v_ref, *, q_scale):
    h = _norm_mod(x_ref[...], g_ref, shift_ref, scale_ref)
    p = jnp.dot(h, w_ref[...], preferred_element_type=F32)
    f_ref[...] = p[:, :FOURIER_W]
    rc, rs = rc_ref[...], rs_ref[...]
    lane = lax.broadcasted_iota(jnp.int32, rc.shape, 1)
    first = (lane % DA_DH) < ROPE_AXIS_DIM

    def rope(t):
        partner = jnp.where(first, pltpu.roll(t, LANES - ROPE_AXIS_DIM, 1), pltpu.roll(t, ROPE_AXIS_DIM, 1))
        return t * rc + partner * rs

    for hd in range(DA_HEADS):
        cq = slice(FOURIER_W + hd * DA_VD, FOURIER_W + (hd + 1) * DA_VD)
        ck = slice(FOURIER_W + DA_W + hd * DA_VD, FOURIER_W + DA_W + (hd + 1) * DA_VD)
        q_ref[0, :, hd * DA_VD:(hd + 1) * DA_VD] = (rope(p[:, cq]) * q_scale).astype(BF16)
        k_ref[0, :, hd * DA_VD:(hd + 1) * DA_VD] = rope(p[:, ck]).astype(BF16)
    v_ref[0] = p[:, FOURIER_W + 2 * DA_W:].astype(BF16)


def _even_in_proj(x2d, g, shift, scale, w, cos, sin, bsz, n_seq, n_keys, q_scale, tm):
    t, d = x2d.shape
    bpb = n_seq // tm
    rc = jnp.tile(cos, (1, LANES // ROPE_AXIS_DIM))
    rs = jnp.tile(jnp.concatenate([-sin, sin], axis=1), (1, LANES // DA_DH))
    tab = pl.BlockSpec((tm, LANES), lambda i: (i % bpb, 0))
    seq = pl.BlockSpec((1, tm, DA_W), lambda i: (i // bpb, i % bpb, 0))
    vec = _vec(d, n_seq * (bsz // shift.shape[0]), tm)
    return pl.pallas_call(
        functools.partial(_even_in_kernel, q_scale=q_scale),
        grid=(t // tm,),
        in_specs=[_row(tm, d), _full((1, d)), vec, vec, _full(w.shape), tab, tab],
        out_specs=[_row(tm, FOURIER_W), seq, seq, seq],
        out_shape=[jax.ShapeDtypeStruct((t, FOURIER_W), F32), jax.ShapeDtypeStruct((bsz, n_seq, DA_W), BF16),
                   jax.ShapeDtypeStruct((bsz, n_keys, DA_W), BF16), jax.ShapeDtypeStruct((bsz, n_keys, DA_W), BF16)],
        compiler_params=_CP,
        name="even_in_proj",
    )(x2d, g.reshape(1, d), shift, scale, w, rc, rs)


def _diff_attn_kernel(lam_ref, q_ref, k_ref, v_ref, g_ref, o_ref, qs_ref, s_ref, m_ref, acc_ref, *,
                      tk, n_sub, out_scale):
    tq = q_ref.shape[1]
    n_kv = k_ref.shape[1] // tk
    rb = 2 * tq // n_sub
    q = q_ref[0]
    lane = lax.broadcasted_iota(jnp.int32, q.shape, 1)
    zero = jnp.zeros_like(q)
    qs_ref[:tq] = jnp.where(lane < DA_DH, q, zero)
    qs_ref[tq:] = jnp.where(lane >= DA_DH, q, zero)
    m_ref[...] = jnp.full(m_ref.shape, -1e30, F32)
    acc_ref[...] = jnp.zeros(acc_ref.shape, F32)
    ones = jnp.ones((tk, LANES), BF16)

    def scores(i, slot):
        off = pl.multiple_of(i * tk, tk)
        k = k_ref[0, pl.ds(off, tk), :]
        s_ref[slot] = lax.dot_general(qs_ref[...], k, (((1,), (1,)), ((), ())), preferred_element_type=F32)

    def consume(i, slot):
        off = pl.multiple_of(i * tk, tk)
        v_ext = jnp.concatenate([v_ref[0, pl.ds(off, tk), :], ones], axis=1)
        for r in range(n_sub):
            rows = pl.ds(r * rb, rb)
            s = s_ref[slot, rows, :]
            m_prev = m_ref[rows, :]
            m_new = jnp.maximum(m_prev, jnp.max(s, axis=1, keepdims=True))
            alpha = jnp.exp2(m_prev - m_new)
            p = jnp.exp2(s - jnp.tile(m_new, (1, tk // LANES)))
            pv = jnp.dot(p.astype(BF16), v_ext, preferred_element_type=F32)
            acc_ref[rows, :] = acc_ref[rows, :] * jnp.tile(alpha, (1, 2)) + pv
            m_ref[rows, :] = m_new

    scores(0, 0)

    def body(j, carry):
        scores(2 * j + 1, 1)
        consume(2 * j, 0)
        scores(2 * j + 2, 0)
        consume(2 * j + 1, 1)
        return carry

    lax.fori_loop(0, (n_kv - 1) // 2, body, 0)
    consume(n_kv - 1, 0)
    acc = acc_ref[...]
    o1 = acc[:tq, :LANES] / acc[:tq, LANES:]
    o2 = acc[tq:, :LANES] / acc[tq:, LANES:]
    o = o1 - lam_ref[0] * o2
    ms = jnp.mean(o * o, axis=-1, keepdims=True)
    o = o * lax.rsqrt(ms + SUBLN_EPS) * g_ref[...] * out_scale
    o_ref[0] = o.astype(o_ref.dtype)


def _diff_attention(lam, q, k_all, v_all, subln_g, out_scale, tq, tk, n_sub=2):
    b, n, _ = q.shape
    nk = k_all.shape[1]
    assert n % tq == 0 and nk % tk == 0 and tk % MXU_DIM == 0 and (nk // tk) % 2 == 1
    grid_spec = pltpu.PrefetchScalarGridSpec(
        num_scalar_prefetch=1,
        grid=(b, DA_HEADS, n // tq),
        in_specs=[pl.BlockSpec((1, tq, DA_VD), lambda bi, hi, qi, lam_r: (bi, qi, hi)),
                  pl.BlockSpec((1, nk, DA_VD), lambda bi, hi, qi, lam_r: (bi, 0, hi)),
                  pl.BlockSpec((1, nk, DA_VD), lambda bi, hi, qi, lam_r: (bi, 0, hi)),
                  pl.BlockSpec((1, DA_VD), lambda bi, hi, qi, lam_r: (0, 0))],
        out_specs=pl.BlockSpec((1, tq, DA_VD), lambda bi, hi, qi, lam_r: (bi, qi, hi)),
        scratch_shapes=[pltpu.VMEM((2 * tq, LANES), BF16), pltpu.VMEM((2, 2 * tq, tk), F32),
                        pltpu.VMEM((2 * tq, LANES), F32), pltpu.VMEM((2 * tq, 2 * LANES), F32)],
    )
    return pl.pallas_call(
        functools.partial(_diff_attn_kernel, tk=tk, n_sub=n_sub, out_scale=out_scale),
        grid_spec=grid_spec,
        out_shape=jax.ShapeDtypeStruct((b, n, DA_W), BF16),
        compiler_params=_CP,
        name="diff_attention",
    )(lam.reshape(1), q, k_all, v_all, subln_g.reshape(1, DA_VD))


def _fft_stage1_kernel(x_ref, fch_ref, f1_ref, tc_ref, ts_ref, yr_ref, yi_ref, *, nb):
    w = FOURIER_W
    r = x_ref.shape[1]
    for bl in range(nb):
        cols = slice(bl * w, (bl + 1) * w)
        ab = jnp.dot(x_ref[0, :, cols], fch_ref[...], preferred_element_type=F32, precision=HI)
        z = jnp.concatenate([ab[:, :w], ab[:, w:]], axis=0)
        y = jnp.dot(f1_ref[...], z, preferred_element_type=F32, precision=HI)
        yr, yi = y[:r], y[r:]
        tc, ts = tc_ref[:, cols], ts_ref[:, cols]
        yr_ref[0, :, cols] = yr * tc + yi * ts
        yi_ref[0, :, cols] = yi * tc - yr * ts


def _fft_stage2_kernel(yr_ref, yi_ref, f2_ref, o_ref, *, nc):
    w = FOURIER_W
    for cl in range(nc):
        y = jnp.concatenate([yr_ref[0, cl], yi_ref[0, cl]], axis=0)
        o_ref[0, :, cl * w:(cl + 1) * w] = jnp.dot(f2_ref[...], y, preferred_element_type=F32, precision=HI)


def _fourier_mix(f, nb=8):
    bsz, n, w = f.shape
    r = math.isqrt(n)
    assert r * r == n and r % nb == 0 and w == FOURIER_W
    two_pi = 2.0 * math.pi
    k = jnp.arange(r, dtype=jnp.int32)
    ang = two_pi * ((k[:, None] * k[None, :]) % r).astype(F32) / r
    c1, s1 = jnp.cos(ang), jnp.sin(ang)
    kc = jnp.arange(FOURIER_GD, dtype=jnp.int32)
    angc = two_pi * ((kc[:, None] * kc[None, :]) % FOURIER_GD).astype(F32) / FOURIER_GD
    eye_g = jnp.eye(FOURIER_GROUPS, dtype=F32)
    fch = jnp.concatenate([jnp.kron(eye_g, jnp.cos(angc)), jnp.kron(eye_g, jnp.sin(angc))], axis=1)
    f1 = jnp.concatenate([jnp.concatenate([c1, -s1], axis=1), jnp.concatenate([-s1, -c1], axis=1)], axis=0)
    f2 = jnp.concatenate([c1, s1], axis=1) * (1.0 / math.sqrt(n * FOURIER_GD))
    angt = two_pi * ((k[:, None] * k[None, :]) % n).astype(F32) / n
    tc = jnp.repeat(jnp.cos(angt), w, axis=1)
    ts = jnp.repeat(jnp.sin(angt), w, axis=1)
    xv = f.reshape(bsz, r, r * w)
    blk = pl.BlockSpec((1, r, nb * w), lambda bi, j: (bi, 0, j))
    tab = pl.BlockSpec((r, nb * w), lambda bi, j: (0, j))
    full = lambda a: pl.BlockSpec(a.shape, lambda bi, j: (0, 0))
    yshape = jax.ShapeDtypeStruct((bsz, r, r * w), F32)
    yr, yi = pl.pallas_call(
        functools.partial(_fft_stage1_kernel, nb=nb),
        grid=(bsz, r // nb),
        in_specs=[blk, full(fch), full(f1), tab, tab],
        out_specs=[blk, blk],
        out_shape=[yshape, yshape],
        compiler_params=_CP,
        name="fft_stage1",
    )(xv, fch, f1, tc, ts)
    y4 = lambda a: a.reshape(bsz, r, r, w)
    yblk = pl.BlockSpec((1, nb, r, w), lambda bi, j: (bi, j, 0, 0))
    out = pl.pallas_call(
        functools.partial(_fft_stage2_kernel, nc=nb),
        grid=(bsz, r // nb),
        in_specs=[yblk, yblk, full(f2)],
        out_specs=blk,
        out_shape=yshape,
        compiler_params=_CP,
        name="fft_stage2",
    )(y4(yr), y4(yi), f2)
    return out.reshape(bsz, n, w)


def _even_out_kernel(fm_ref, o_ref, w_ref, x_ref, gate_ref, g_ref, shift_ref, scale_ref, wr_ref, br_ref,
                     xo_ref, h_ref, idx_ref, gates_ref):
    y = (jnp.dot(fm_ref[...].astype(BF16), w_ref[:FOURIER_W], preferred_element_type=F32)
         + jnp.dot(o_ref[...], w_ref[FOURIER_W:], preferred_element_type=F32))
    _residual_norm_route(y, x_ref, gate_ref, g_ref, shift_ref, scale_ref, wr_ref, br_ref,
                         xo_ref, h_ref, idx_ref, gates_ref)


def _even_out_proj(fm, o, w, x2d, gate, g, shift, scale, w_router, b_router, rows_per_vec, tm):
    t, d = x2d.shape
    wr, br = _router_operands(w_router, b_router)
    vec = _vec(d, rows_per_vec, tm)
    out_specs, out_shape = _route_out(t, d, tm)
    return pl.pallas_call(
        _even_out_kernel,
        grid=(t // tm,),
        in_specs=[_row(tm, FOURIER_W), _row(tm, DA_W), _full(w.shape), _row(tm, d), vec, _full((1, d)), vec, vec,
                  _full((d, LANES)), _full((1, LANES))],
        out_specs=out_specs, out_shape=out_shape,
        compiler_params=_CP,
        name="even_out_proj",
    )(fm, o, w, x2d, gate, g.reshape(1, d), shift, scale, wr, br)


def _moe_kernel(be_ref, x_ref, w1_ref, perm_ref, b1g_ref, b1l_ref, w2_ref, b2_ref, *rest):
    o_ref, w1g_s, w1l_s, w2_s = rest[-4:]
    i = pl.program_id(0)

    @pl.when((i == 0) | (be_ref[i] != be_ref[jnp.maximum(i - 1, 0)]))
    def _():
        for c in range(w1_ref.shape[3] // MXU_DIM):
            blk = jnp.dot(w1_ref[0, 0, :, c * MXU_DIM:(c + 1) * MXU_DIM].astype(BF16), perm_ref[...],
                          preferred_element_type=F32)
            w1g_s[:, c * LANES:(c + 1) * LANES] = blk[:, :LANES].astype(BF16)
            w1l_s[:, c * LANES:(c + 1) * LANES] = blk[:, LANES:].astype(BF16)
        w2_s[...] = w2_ref[0, 0].astype(BF16)

    x = x_ref[...]
    ug = jnp.dot(x, w1g_s[...], preferred_element_type=F32) + b1g_ref[0]
    ul = jnp.dot(x, w1l_s[...], preferred_element_type=F32) + b1l_ref[0]
    glu = jnp.minimum(ug, SWIGLU_LIMIT)
    lin = jnp.clip(ul, -SWIGLU_LIMIT, SWIGLU_LIMIT)
    act = glu * jax.nn.sigmoid(SWIGLU_ALPHA * glu) * (lin + 1.0)
    y = jnp.dot(act.astype(BF16), w2_s[...], preferred_element_type=F32) + b2_ref[0]
    o_ref[...] = y.astype(o_ref.dtype)


def _moe_experts(block_expert, xb, w1_all, b1g, b1l, w2_all, layer, b2, yb_prev, first_block, n_blocks_total):
    n_rows, d = xb.shape
    n_blocks = n_rows // MOE_BLOCK
    src = jnp.arange(MXU_DIM)[:, None]
    dst = jnp.arange(MXU_DIM)[None, :]
    perm = (src == jnp.where(dst < LANES, 2 * dst, 2 * (dst - LANES) + 1)).astype(BF16)
    bspec = lambda s: pl.BlockSpec((1,) + s, lambda i, be: (be[i], 0, 0))
    wspec = lambda s: pl.BlockSpec((1, 1) + s, lambda i, be: (layer, be[i], 0, 0))
    in_specs = [pl.BlockSpec((MOE_BLOCK, d), lambda i, be: (i, 0)),
                wspec((d, 2 * D_FF)), pl.BlockSpec((MXU_DIM, MXU_DIM), lambda i, be: (0, 0)),
                bspec((1, D_FF)), bspec((1, D_FF)), wspec((D_FF, d)), bspec((1, d))]
    args = [block_expert, xb, w1_all, perm, b1g, b1l, w2_all, b2]
    aliases = {}
    if yb_prev is not None:
        in_specs.append(pl.BlockSpec(memory_space=pl.ANY))
        aliases = {len(args): 0}
        args.append(yb_prev)
    grid_spec = pltpu.PrefetchScalarGridSpec(
        num_scalar_prefetch=1,
        grid=(n_blocks,),
        in_specs=in_specs,
        out_specs=pl.BlockSpec((MOE_BLOCK, d), lambda i, be: (first_block + i, 0)),
        scratch_shapes=[pltpu.VMEM((d, D_FF), BF16), pltpu.VMEM((d, D_FF), BF16), pltpu.VMEM((D_FF, d), BF16)],
    )
    return pl.pallas_call(
        _moe_kernel,
        grid_spec=grid_spec,
        out_shape=jax.ShapeDtypeStruct((n_blocks_total * MOE_BLOCK, d), BF16),
        input_output_aliases=aliases,
        compiler_params=_CP,
        name="moe_experts",
    )(*args)


def _rank_kernel(idx_ref, rank_ref, cnt_ref, carry_ref):
    i = pl.program_id(0)

    @pl.when(i == 0)
    def _():
        carry_ref[...] = jnp.zeros(carry_ref.shape, F32)

    idx = idx_ref[...]
    tm = idx.shape[0]
    lane = lax.broadcasted_iota(jnp.int32, idx.shape, 1)
    sel = [lane == idx[:, j:j + 1] for j in range(TOP_K)]
    onehot = sel[0].astype(F32)
    for j in range(1, TOP_K):
        onehot = onehot + sel[j].astype(F32)
    row = lax.broadcasted_iota(jnp.int32, (tm, tm), 0)
    col = lax.broadcasted_iota(jnp.int32, (tm, tm), 1)
    before = (row > col).astype(BF16)
    prefix = jnp.dot(before, onehot.astype(BF16), preferred_element_type=F32) + carry_ref[0:1, :]
    rank = jnp.zeros(idx.shape, F32)
    for j in range(TOP_K):
        rank = jnp.where(lane == j, jnp.sum(jnp.where(sel[j], prefix, 0.0), axis=-1, keepdims=True), rank)
    rank_ref[...] = rank.astype(jnp.int32)
    carry_ref[0:1, :] = carry_ref[0:1, :] + jnp.sum(onehot, axis=0, keepdims=True)
    cnt_ref[...] = carry_ref[...]


def _expert_ranks(idx, tm):
    t = idx.shape[0]
    rank, cnt = pl.pallas_call(
        _rank_kernel,
        grid=(t // tm,),
        in_specs=[_row(tm, LANES)],
        out_specs=[_row(tm, LANES), _full((8, LANES))],
        out_shape=[jax.ShapeDtypeStruct((t, LANES), jnp.int32), jax.ShapeDtypeStruct((8, LANES), F32)],
        scratch_shapes=[pltpu.VMEM((8, LANES), F32)],
        compiler_params=_CP,
        name="expert_ranks",
    )(idx)
    return rank, cnt[0, :N_EXPERTS].astype(jnp.int32)


def _pos_kernel(idx_ref, rank_ref, ps_ref, pos_ref):
    idx = idx_ref[...]
    lane = lax.broadcasted_iota(jnp.int32, idx.shape, 1)
    pos = rank_ref[...].astype(F32)
    for j in range(TOP_K):
        start = jnp.sum(jnp.where(lane == idx[:, j:j + 1], ps_ref[...], 0.0), axis=-1, keepdims=True)
        pos = jnp.where(lane == j, pos + start, pos)
    pos_ref[...] = jnp.transpose(pos)[:pos_ref.shape[0]].astype(jnp.int32)


def _expert_rows(idx, rank, pad_starts):
    t = idx.shape[0]
    tm = max(m for m in range(LANES, TM_ELEMENTWISE + 1, LANES) if t % m == 0)
    ps = jnp.zeros((1, LANES), F32).at[0, :N_EXPERTS].set(pad_starts.astype(F32))
    return pl.pallas_call(
        _pos_kernel,
        grid=(t // tm,),
        in_specs=[_row(tm, LANES), _row(tm, LANES), _full((1, LANES))],
        out_specs=pl.BlockSpec((8, tm), lambda i: (0, i)),
        out_shape=jax.ShapeDtypeStruct((8, t), jnp.int32),
        compiler_params=_CP,
        name="expert_rows",
    )(idx, rank, ps)


def _moe_parts(n_blocks):
    parts, first, size = [], 0, max(1, n_blocks // MOE_FIRST_PART)
    while first < n_blocks:
        if n_blocks - first < 2 * size:
            size = n_blocks - first
        parts.append((first, size))
        first += size
        size *= 2
    return parts


def _moe_ffn(h, idx, w1_all, b1, w2_all, b2, layer, tm):
    n_tok, d = h.shape
    n_assign = n_tok * TOP_K
    e_flat = idx[:, :TOP_K].reshape(n_assign)
    shift_bits = (n_assign - 1).bit_length()
    assert N_EXPERTS << shift_bits < 2 ** 31
    packed = jnp.sort((e_flat << shift_bits) | jnp.arange(n_assign, dtype=jnp.int32))
    order = packed & ((1 << shift_bits) - 1)
    rank, counts = _expert_ranks(idx, tm)
    starts = jnp.cumsum(counts) - counts
    padded = (counts + MOE_BLOCK - 1) // MOE_BLOCK * MOE_BLOCK
    pad_ends = jnp.cumsum(padded)
    pad_starts = pad_ends - padded
    n_blocks = -(-(n_assign + N_EXPERTS * (MOE_BLOCK - 1)) // MOE_BLOCK)
    n_rows = n_blocks * MOE_BLOCK
    block_start = jnp.arange(n_blocks, dtype=jnp.int32) * MOE_BLOCK
    block_expert = jnp.minimum(jnp.sum(pad_ends[None, :] <= block_start[:, None], axis=1, dtype=jnp.int32),
                               N_EXPERTS - 1)
    shift = pad_starts - starts
    pos = _expert_rows(idx, rank, pad_starts)
    r = jnp.arange(n_rows, dtype=jnp.int32)
    src = jnp.clip(r - jnp.repeat(shift[block_expert], MOE_BLOCK), 0, n_assign - 1)
    row_tok = order[src] // TOP_K
    yb = None
    for first, size in _moe_parts(n_blocks):
        xb = h[row_tok[first * MOE_BLOCK:(first + size) * MOE_BLOCK]]
        yb = _moe_experts(block_expert[first:first + size], xb, w1_all, b1[:, None, 0::2], b1[:, None, 1::2],
                          w2_all, layer, b2[:, None, :], yb, first, n_blocks)
    return yb, pos


def _combine_kernel(*refs, final):
    y_refs = refs[:TOP_K]
    gt_ref, x_ref, g5_ref = refs[TOP_K:TOP_K + 3]
    o_ref = refs[-1]
    gt = gt_ref[...]
    acc = y_refs[0][...].astype(F32) * gt[:, 0:1]
    for j in range(1, TOP_K):
        acc = acc + y_refs[j][...].astype(F32) * gt[:, j:j + 1]
    xn = x_ref[...] + g5_ref[0] * acc
    if final:
        fg_ref = refs[TOP_K + 3]
        xn = xn * lax.rsqrt(jnp.mean(xn * xn, axis=-1, keepdims=True) + NORM_EPS) * fg_ref[...]
    o_ref[...] = xn


def _moe_combine(yb, pos, gates, x2d, gate5, rows_per_vec, tm, final_g=None):
    t, d = x2d.shape
    ys = [yb[pos[j]] for j in range(TOP_K)]
    in_specs = [_row(tm, d)] * TOP_K + [_row(tm, LANES), _row(tm, d), _vec(d, rows_per_vec, tm)]
    args = ys + [gates, x2d, gate5]
    if final_g is not None:
        in_specs.append(_full((1, d)))
        args.append(final_g.reshape(1, d))
    return pl.pallas_call(
        functools.partial(_combine_kernel, final=final_g is not None),
        grid=(t // tm,),
        in_specs=in_specs,
        out_specs=_row(tm, d),
        out_shape=jax.ShapeDtypeStruct((t, d), F32),
        compiler_params=_CP,
        name="moe_combine",
    )(*args)


HALO = SUBLANES_BF16


def _odd_in_kernel(xm_ref, xp_ref, xn_ref, g_ref, shift_ref, scale_ref, wqkv_ref, wz_ref, wab_ref, cw_ref,
                   alog_ref, dtb_ref, q_ref, k_ref, v_ref, z_ref, gb_ref):
    i = pl.program_id(1)
    n_i = pl.num_programs(1)
    tm = xm_ref.shape[1]
    half = GDN_CONV // 2
    keep_prev = (i > 0).astype(BF16)
    keep_next = (i < n_i - 1).astype(BF16)
    h_main = _norm_mod(xm_ref[0], g_ref, shift_ref, scale_ref)
    h_ext = jnp.concatenate([_norm_mod(xp_ref[0], g_ref, shift_ref, scale_ref) * keep_prev, h_main,
                             _norm_mod(xn_ref[0], g_ref, shift_ref, scale_ref) * keep_next], axis=0)
    gw = MXU_DIM
    for cg in range(GDN_QKV_W // gw):
        cols = slice(cg * gw, (cg + 1) * gw)
        ext = jnp.dot(h_ext, wqkv_ref[:, cols], preferred_element_type=F32)
        acc = ext[HALO:HALO + tm] * cw_ref[half:half + 1, cols]
        for j in range(GDN_CONV):
            if j != half:
                sh = pltpu.roll(ext, (half - j) % (tm + 2 * HALO), 0)[HALO:HALO + tm]
                acc = acc + sh * cw_ref[j:j + 1, cols]
        y = acc * jax.nn.sigmoid(acc)
        for sub in range(gw // LANES):
            hd = cg * (gw // LANES) + sub
            yh = y[:, sub * LANES:(sub + 1) * LANES]
            if hd < 2 * GDN_HK:
                yh = yh * lax.rsqrt(jnp.sum(yh * yh, axis=-1, keepdims=True) + 1e-6)
                if hd < GDN_HK:
                    q_ref[0, :, hd * LANES:(hd + 1) * LANES] = yh * (GDN_DK ** -0.5)
                else:
                    k_ref[0, :, (hd - GDN_HK) * LANES:(hd - GDN_HK + 1) * LANES] = yh
            else:
                v_ref[0, :, (hd - 2 * GDN_HK) * LANES:(hd - 2 * GDN_HK + 1) * LANES] = yh
    z_ref[0] = jnp.dot(h_main, wz_ref[...], preferred_element_type=F32).astype(BF16)
    ab = jnp.dot(h_main, wab_ref[...], preferred_element_type=F32)
    xa = ab + dtb_ref[...]
    softplus = jnp.maximum(xa, 0.0) + jnp.log1p(jnp.exp(-jnp.abs(xa)))
    lane = lax.broadcasted_iota(jnp.int32, ab.shape, 1)
    gb_ref[0] = jnp.where(lane < 2 * GDN_HV, -jnp.exp(alog_ref[...]) * softplus, jax.nn.sigmoid(ab))


def _odd_in_stage(x3d, g, shift, scale, w_in, conv_w, a_log, dt_bias, tm):
    b, t, d = x3d.shape
    w_qkv = w_in[:, :GDN_QKV_W].astype(BF16)
    w_z = w_in[:, GDN_QKV_W:GDN_MAIN_W].astype(BF16)
    w_ab = jnp.zeros((d, LANES), BF16).at[:, :4 * GDN_HV].set(w_in[:, GDN_MAIN_W:].astype(BF16))
    nb = tm // HALO
    last = t // HALO - 1
    pad = lambda a: jnp.zeros((1, LANES), F32).at[0, :2 * GDN_HV].set(a.reshape(-1))
    per_vec = b // shift.shape[0]
    vec = pl.BlockSpec((1, 1, d), lambda bi, i: (bi // per_vec, 0, 0))
    seq = lambda n: pl.BlockSpec((1, tm, n), lambda bi, i: (bi, i, 0))
    full = lambda s: pl.BlockSpec(s, lambda bi, i: (0,) * len(s))
    f = lambda n, dt=F32: jax.ShapeDtypeStruct((b, t, n), dt)
    return pl.pallas_call(
        _odd_in_kernel,
        grid=(b, t // tm),
        in_specs=[seq(d),
                  pl.BlockSpec((1, HALO, d), lambda bi, i: (bi, jnp.maximum(i * nb - 1, 0), 0)),
                  pl.BlockSpec((1, HALO, d), lambda bi, i: (bi, jnp.minimum((i + 1) * nb, last), 0)),
                  full((1, d)), vec, vec, full(w_qkv.shape), full(w_z.shape), full(w_ab.shape),
                  full((GDN_CONV, GDN_QKV_W)), full((1, LANES)), full((1, LANES))],
        out_specs=[seq(GDN_QK_W), seq(GDN_QK_W), seq(GDN_V_W), seq(GDN_V_W), seq(LANES)],
        out_shape=[f(GDN_QK_W), f(GDN_QK_W), f(GDN_V_W), f(GDN_V_W, BF16), f(LANES)],
        compiler_params=_CP,
        name="odd_in_stage",
    )(x3d, x3d, x3d, g.reshape(1, d), shift, scale, w_qkv, w_z, w_ab, conv_w, pad(a_log), pad(dt_bias))


def _gdn_chunk_kernel(q_ref, k_ref, v_ref, gb_ref, s0_ref, *rest, reverse, g_lane, b_lane):
    o_ref, sfin_ref, s_ref = rest[-3:]
    add_ref = rest[0] if len(rest) == 4 else None
    c = pl.program_id(0)
    n_c = pl.num_programs(0)
    cs = GDN_CHUNK
    rep = GDN_HV // GDN_HK
    nb = q_ref.shape[0]

    @pl.when(c == 0)
    def _():
        s_ref[...] = s0_ref[...].reshape(s_ref.shape)

    row = lax.broadcasted_iota(jnp.int32, (cs, cs), 0)
    col = lax.broadcasted_iota(jnp.int32, (cs, cs), 1)
    incl = (row <= col) if reverse else (row >= col)
    strict = (row < col) if reverse else (row > col)
    eye = (row == col).astype(F32)
    incl_b = incl.astype(BF16)
    last = 0 if reverse else cs - 1
    tn = (((0,), (1,)), ((), ()))

    heads = range(GDN_HV)
    gc, gr, bc, qh, kh, vh = [], [], [], [], [], []
    for b in range(nb):
        gb = gb_ref[b]
        gb_hi = gb.astype(BF16)
        gb_lo = (gb - gb_hi.astype(F32)).astype(BF16)
        g_col = (jnp.dot(incl_b, gb_hi, preferred_element_type=F32)
                 + jnp.dot(incl_b, gb_lo, preferred_element_type=F32))
        g_row = (lax.dot_general(gb_hi, incl_b, tn, preferred_element_type=F32)
                 + lax.dot_general(gb_lo, incl_b, tn, preferred_element_type=F32))
        gc += [g_col[:, g_lane + h:g_lane + h + 1] for h in heads]
        gr += [g_row[g_lane + h:g_lane + h + 1, :] for h in heads]
        bc += [gb[:, b_lane + h:b_lane + h + 1] for h in heads]
        qh += [q_ref[b, :, h * GDN_DK:(h + 1) * GDN_DK] for h in range(GDN_HK)]
        kh += [k_ref[b, :, h * GDN_DK:(h + 1) * GDN_DK] for h in range(GDN_HK)]
        vh += [v_ref[b, :, h * GDN_DV:(h + 1) * GDN_DV] for h in heads]
    gc, gr, bc = jnp.stack(gc), jnp.stack(gr), jnp.stack(bc)
    qh, kh, vh = jnp.stack(qh), jnp.stack(kh), jnp.stack(vh)
    ge = gc[:, last:last + 1, :]
    kh_b = kh.astype(BF16)
    bnt = (((2,), (2,)), ((0,), (0,)))
    kk = lax.dot_general(kh_b, kh_b, bnt, preferred_element_type=F32)
    qk = lax.dot_general(qh.astype(BF16), kh_b, bnt, preferred_element_type=F32)
    kk, qk = jnp.repeat(kk, rep, axis=0), jnp.repeat(qk, rep, axis=0)
    qv, kv = jnp.repeat(qh, rep, axis=0), jnp.repeat(kh, rep, axis=0)

    decay = jnp.where(incl, jnp.exp(jnp.where(incl, gc - gr, 0.0)), 0.0)
    lm = jnp.where(strict, bc * kk * decay, 0.0)
    bmm = lambda a, b: jnp.einsum('hij,hjk->hik', a.astype(BF16), b.astype(BF16), preferred_element_type=F32)
    blk = lambda n: (row // n) == (col // n)
    l0 = jnp.where(blk(16), lm, 0.0)
    p = bmm(l0, l0)
    x = eye - l0
    for _ in range(2):
        xp = bmm(jnp.concatenate([x, p], axis=1), p)
        x = x + xp[:, :cs]
        p = xp[:, cs:]
    x = x + bmm(x, p)
    n = 32
    while n <= cs:
        off = jnp.where(blk(n) & ~blk(n // 2), lm, 0.0)
        x = x - bmm(bmm(x, off), x)
        n *= 2
    eg = jnp.exp(gc)
    uw = bmm(x, jnp.concatenate([vh * bc, kv * (bc * eg)], axis=2))
    u = uw[:, :, :GDN_DV]
    w = uw[:, :, GDN_DV:]
    qg = qv * eg
    intra = jnp.where(incl, qk * decay, 0.0)
    kt = kv * jnp.exp(ge - gc)
    s = s_ref[...]
    wq = bmm(jnp.concatenate([w, qg], axis=1), s)
    v_new = u - wq[:, :cs]
    o = wq[:, cs:] + bmm(intra, v_new)
    for b in range(nb):
        for h in heads:
            cols = slice(h * GDN_DV, (h + 1) * GDN_DV)
            oh = o[b * GDN_HV + h]
            if add_ref is not None:
                oh = oh + add_ref[b, :, cols]
            o_ref[b, :, cols] = oh.astype(o_ref.dtype)
    s_ref[...] = s * jnp.exp(ge) + jnp.einsum('hck,hcv->hkv', kt.astype(BF16), v_new.astype(BF16),
                                              preferred_element_type=F32)

    @pl.when(c == n_c - 1)
    def _():
        sfin_ref[...] = s_ref[...].reshape(sfin_ref.shape)


def _gdn_scan(q, k, v, gb, s0, reverse, add=None, out_dtype=F32):
    b, t, _ = q.shape
    assert t % GDN_CHUNK == 0 and GDN_CHUNK % 32 == 0
    n_c = t // GDN_CHUNK
    cm = (lambda ci: (0, n_c - 1 - ci, 0)) if reverse else (lambda ci: (0, ci, 0))
    d = 1 if reverse else 0
    smap = lambda ci: (0, 0, 0, 0)
    in_specs = [pl.BlockSpec((b, GDN_CHUNK, GDN_QK_W), cm), pl.BlockSpec((b, GDN_CHUNK, GDN_QK_W), cm),
                pl.BlockSpec((b, GDN_CHUNK, GDN_V_W), cm), pl.BlockSpec((b, GDN_CHUNK, LANES), cm),
                pl.BlockSpec((b, GDN_HV, GDN_DK, GDN_DV), smap)]
    args = [q, k, v, gb, s0]
    if add is not None:
        in_specs.append(pl.BlockSpec((b, GDN_CHUNK, GDN_V_W), cm))
        args.append(add)
    return pl.pallas_call(
        functools.partial(_gdn_chunk_kernel, reverse=reverse, g_lane=d * GDN_HV, b_lane=(2 + d) * GDN_HV),
        grid=(n_c,),
        in_specs=in_specs,
        out_specs=[pl.BlockSpec((b, GDN_CHUNK, GDN_V_W), cm), pl.BlockSpec((b, GDN_HV, GDN_DK, GDN_DV), smap)],
        out_shape=[jax.ShapeDtypeStruct((b, t, GDN_V_W), out_dtype),
                   jax.ShapeDtypeStruct((b, GDN_HV, GDN_DK, GDN_DV), F32)],
        scratch_shapes=[pltpu.VMEM((b * GDN_HV, GDN_DK, GDN_DV), F32)],
        compiler_params=_CP,
        name="gdn_scan_bwd" if reverse else "gdn_scan_fwd",
    )(*args)


def _odd_out_kernel(o_ref, z_ref, ng_ref, w_ref, x_ref, gate_ref, g_ref, shift_ref, scale_ref,
                    wr_ref, br_ref, xo_ref, h_ref, idx_ref, gates_ref):
    parts = []
    for h in range(GDN_HV):
        cols = slice(h * GDN_DV, (h + 1) * GDN_DV)
        o = o_ref[:, cols].astype(F32)
        z = z_ref[:, cols].astype(F32)
        o = o * lax.rsqrt(jnp.mean(o * o, axis=-1, keepdims=True) + NORM_EPS) * ng_ref[...]
        parts.append((o * (z * jax.nn.sigmoid(z))).astype(BF16))
    y = jnp.dot(jnp.concatenate(parts, axis=1), w_ref[...], preferred_element_type=F32)
    _residual_norm_route(y, x_ref, gate_ref, g_ref, shift_ref, scale_ref, wr_ref, br_ref,
                         xo_ref, h_ref, idx_ref, gates_ref)


def _odd_out_proj(o, z, norm_g, w, x2d, gate, g, shift, scale, w_router, b_router, rows_per_vec, tm):
    t, d = x2d.shape
    wr, br = _router_operands(w_router, b_router)
    vec = _vec(d, rows_per_vec, tm)
    out_specs, out_shape = _route_out(t, d, tm)
    return pl.pallas_call(
        _odd_out_kernel,
        grid=(t // tm,),
        in_specs=[_row(tm, GDN_V_W), _row(tm, GDN_V_W),
                  _full((1, GDN_DV)), _full(w.shape), _row(tm, d), vec, _full((1, d)), vec, vec,
                  _full((d, LANES)), _full((1, LANES))],
        out_specs=out_specs, out_shape=out_shape,
        compiler_params=_CP,
        name="odd_out_proj",
    )(o, z, norm_g.reshape(1, GDN_DV), w, x2d, gate, g.reshape(1, d), shift, scale, wr, br)


def _split_mod(mv, bsz):
    d = mv.shape[1] // N_MOD
    lat = [mv[:bsz, j * d:(j + 1) * d][:, None, :] for j in range(N_MOD)]
    ctx = [mv[bsz:bsz + 1, j * d:(j + 1) * d][:, None, :] for j in range(N_MOD)]
    return lat, ctx


def _even_layer(x2d, xc2d, mod, mod_c, norm1_g, norm2_g, w_in, w_out, lam_p, subln_g, lam_init, cos, sin,
                w_router, b_router, bsz, n_lat, n_ctx):
    d = x2d.shape[1]
    w_in_b = w_in.astype(BF16)
    w_out_b = w_out.astype(BF16)
    q_scale = DA_DH ** -0.5 * math.log2(math.e)
    f, q, k_all, v_all = _even_in_proj(x2d, norm1_g, mod[0], mod[1], w_in_b, cos, sin, bsz, n_lat, n_lat + n_ctx,
                                       q_scale, TM_PROJ)
    ones, zeros = jnp.ones((n_ctx, ROPE_AXIS_DIM), F32), jnp.zeros((n_ctx, ROPE_AXIS_DIM), F32)
    fc, qc, kc, vc = _even_in_proj(xc2d, norm1_g, mod_c[0], mod_c[1], w_in_b, ones, zeros, bsz, n_ctx, n_ctx,
                                   q_scale, n_ctx)
    lp = lam_p.astype(F32)
    lam = jnp.exp(jnp.sum(lp[0] * lp[1])) - jnp.exp(jnp.sum(lp[2] * lp[3])) + lam_init
    k_all = lax.dynamic_update_slice(k_all, kc, (0, n_lat, 0))
    v_all = lax.dynamic_update_slice(v_all, vc, (0, n_lat, 0))
    o = _diff_attention(lam, q, k_all, v_all, subln_g, 1.0 - lam_init, TQ_ATTN, TK_ATTN)
    oc = _diff_attention(lam, qc, kc, vc, subln_g, 1.0 - lam_init, n_ctx, n_ctx)
    fm = _fourier_mix(f.reshape(bsz, n_lat, FOURIER_W)).reshape(bsz * n_lat, FOURIER_W)
    fmc = _fourier_mix(fc.reshape(bsz, n_ctx, FOURIER_W)).reshape(bsz * n_ctx, FOURIER_W)
    lat = _even_out_proj(fm, o.reshape(bsz * n_lat, DA_W), w_out_b, x2d, mod[2], norm2_g, mod[3], mod[4],
                         w_router, b_router, n_lat, TM_PROJ)
    ctx = _even_out_proj(fmc, oc.reshape(bsz * n_ctx, DA_W), w_out_b, xc2d, mod_c[2], norm2_g, mod_c[3], mod_c[4],
                         w_router, b_router, bsz * n_ctx, n_ctx)
    return lat, ctx


def _odd_layer(x2d, xc2d, mod, mod_c, norm1_g, norm2_g, w_in, conv_w, a_log, dt_bias, norm_g, w_out,
               w_router, b_router, bsz, n_lat, n_ctx):
    d = x2d.shape[1]
    q, k, v, z, gb = _odd_in_stage(x2d.reshape(bsz, n_lat, d), norm1_g, mod[0], mod[1], w_in, conv_w, a_log,
                                   dt_bias, TM_PROJ)
    qc, kc, vc, _, gbc = _odd_in_stage(xc2d.reshape(bsz, n_ctx, d), norm1_g, mod_c[0], mod_c[1], w_in, conv_w,
                                       a_log, dt_bias, n_ctx)
    s0 = jnp.zeros((bsz, GDN_HV, GDN_DK, GDN_DV), F32)
    _, sc_f = _gdn_scan(qc, kc, vc, gbc, s0, False)
    o_f, _ = _gdn_scan(q, k, v, gb, sc_f, False)
    _, sc_b = _gdn_scan(qc, kc, vc, gbc, s0, True)
    o, _ = _gdn_scan(q, k, v, gb, sc_b, True, add=o_f, out_dtype=BF16)
    return _odd_out_proj(o.reshape(bsz * n_lat, GDN_V_W), z.reshape(bsz * n_lat, GDN_V_W), norm_g,
                         w_out.astype(BF16), x2d, mod[2], norm2_g, mod[3], mod[4], w_router, b_router, n_lat,
                         TM_PROJ)


def kernel(x, c, ctx, c_ctx, norm1_g, norm2_g, w_mod, b_mod, ev_w_in, ev_w_out, ev_lam, ev_subln_g,
           od_w_in, od_conv_w, od_a_log, od_dt_bias, od_norm_g, od_w_out,
           moe_w_router, moe_b_router, moe_w1, moe_b1, moe_w2, moe_b2, final_g):
    bsz, n_lat, d = x.shape
    n_ctx = ctx.shape[1]
    assert w_mod.shape[0] == 2, "kernel is written for one even (attention) and one odd (DeltaNet) layer"
    t_lat = bsz * n_lat
    cos, sin = _axial_rope_tables(n_lat // GRID_W)
    c_rows = jnp.zeros((8, d), F32).at[:bsz].set(c).at[bsz].set(c_ctx)
    x2d = x.reshape(t_lat, d)
    xc2d = ctx.reshape(bsz * n_ctx, d)

    mod, mod_c = _split_mod(_mod_vectors(c_rows, w_mod, b_mod, 0), bsz)
    (x2d, h2, idx, gates), (xc2d, h2c, idx_c, gates_c) = _even_layer(
        x2d, xc2d, mod, mod_c, norm1_g[0], norm2_g[0], ev_w_in[0], ev_w_out[0], ev_lam[0], ev_subln_g[0],
        _diff_lambda_init(0), cos, sin, moe_w_router[0], moe_b_router[0], bsz, n_lat, n_ctx)
    yb, pos = _moe_ffn(jnp.concatenate([h2, h2c], axis=0), jnp.concatenate([idx, idx_c], axis=0),
                       moe_w1, moe_b1[0], moe_w2, moe_b2[0], 0, TM_PROJ)
    x2d = _moe_combine(yb, pos[:, :t_lat], gates, x2d, mod[5], n_lat, TM_PROJ)
    xc2d = _moe_combine(yb, pos[:, t_lat:], gates_c, xc2d, mod_c[5], bsz * n_ctx, n_ctx)

    mod, mod_c = _split_mod(_mod_vectors(c_rows, w_mod, b_mod, 1), bsz)
    x2d, h2, idx, gates = _odd_layer(x2d, xc2d, mod, mod_c, norm1_g[1], norm2_g[1], od_w_in[0], od_conv_w[0],
                                     od_a_log[0], od_dt_bias[0], od_norm_g[0], od_w_out[0],
                                     moe_w_router[1], moe_b_router[1], bsz, n_lat, n_ctx)
    yb, pos = _moe_ffn(h2, idx, moe_w1, moe_b1[1], moe_w2, moe_b2[1], 1, TM_PROJ)
    return _moe_combine(yb, pos, gates, x2d, mod[5], n_lat, TM_PROJ, final_g=final_g).reshape(bsz, n_lat, d)
```

```python
import functools
import math

import jax
import jax.numpy as jnp
from jax import lax
from jax.experimental import pallas as pl
from jax.experimental.pallas import tpu as pltpu

N_MOD = 6
NORM_EPS = 1e-6
GRID_W = 64

FOURIER_GROUPS = 4
FOURIER_GD = 64
FOURIER_W = FOURIER_GROUPS * FOURIER_GD
DA_HEADS = 6
DA_DH = 64
DA_VD = 2 * DA_DH
DA_W = DA_HEADS * DA_VD
ROPE_BASE = 10000.0
ROPE_AXIS_DIM = DA_DH // 2
SUBLN_EPS = 1e-5

GDN_HK = 8
GDN_HV = 16
GDN_DK = 128
GDN_DV = 128
GDN_QK_W = GDN_HK * GDN_DK
GDN_V_W = GDN_HV * GDN_DV
GDN_QKV_W = 2 * GDN_QK_W + GDN_V_W
GDN_MAIN_W = GDN_QKV_W + GDN_V_W
GDN_CONV = 5
GDN_CHUNK = 64

N_EXPERTS = 32
TOP_K = 4
D_FF = 1024
SWIGLU_LIMIT = 7.0
SWIGLU_ALPHA = 1.702
MOE_BLOCK = 512
MOE_FIRST_PART = 32

LANES = 128
SUBLANES_BF16 = 16
MXU_DIM = 256
VMEM_LIMIT = 56 * 1024 * 1024
BF16 = jnp.bfloat16
F32 = jnp.float32
HI = lax.Precision.HIGHEST

TM_PROJ = 512
TM_ELEMENTWISE = 4096
TQ_ATTN, TK_ATTN = 512, 1280

_CP = pltpu.CompilerParams(vmem_limit_bytes=VMEM_LIMIT)


def _diff_lambda_init(layer_idx):
    return 0.8 - 0.6 * math.exp(-0.3 * layer_idx)


def _axial_rope_tables(rows):
    t = jnp.arange(rows * GRID_W, dtype=jnp.int32)
    row = (t // GRID_W).astype(F32)
    col = (t % GRID_W).astype(F32)
    inv = ROPE_BASE ** (-jnp.arange(0, ROPE_AXIS_DIM, 2, dtype=F32) / ROPE_AXIS_DIM)
    ang = jnp.concatenate([row[:, None] * inv, col[:, None] * inv], axis=-1)
    return jnp.cos(ang), jnp.sin(ang)


def _norm_mod(x, g_ref, shift_ref, scale_ref):
    h = x * lax.rsqrt(jnp.mean(x * x, axis=-1, keepdims=True) + NORM_EPS) * g_ref[...]
    return (h * (1.0 + scale_ref[0]) + shift_ref[0]).astype(BF16)


def _route(logits):
    lane = lax.broadcasted_iota(jnp.int32, logits.shape, 1)
    lane_f = lane.astype(F32)
    neg = jnp.float32(-jnp.inf)
    rest = jnp.where(lane < N_EXPERTS, logits, neg)
    idx = jnp.zeros(logits.shape, F32)
    val = jnp.full(logits.shape, neg, F32)
    for j in range(TOP_K):
        m = jnp.max(rest, axis=-1, keepdims=True)
        sel = jnp.min(jnp.where(rest == m, lane_f, float(LANES)), axis=-1, keepdims=True)
        idx = jnp.where(lane == j, sel, idx)
        val = jnp.where(lane == j, m, val)
        rest = jnp.where(lane_f == sel, neg, rest)
    e = jnp.exp(val - val[:, 0:1])
    return idx, e / jnp.sum(e, axis=-1, keepdims=True)


def _residual_norm_route(y, x_ref, gate_ref, g_ref, shift_ref, scale_ref, wr_ref, br_ref,
                         xo_ref, h_ref, idx_ref, gates_ref):
    xn = x_ref[...] + gate_ref[0] * y
    xo_ref[...] = xn
    h = _norm_mod(xn, g_ref, shift_ref, scale_ref)
    h_ref[...] = h
    idx, gates = _route(jnp.dot(h, wr_ref[...], preferred_element_type=F32) + br_ref[...])
    idx_ref[...] = idx.astype(jnp.int32)
    gates_ref[...] = gates


def _row(tm, n):
    return pl.BlockSpec((tm, n), lambda i: (i, 0))


def _full(shape):
    return pl.BlockSpec(shape, lambda i: (0,) * len(shape))


def _vec(d, rows_per_vec, tm):
    per = rows_per_vec // tm
    return pl.BlockSpec((1, 1, d), lambda i: (i // per, 0, 0))


def _router_operands(w_router, b_router):
    d = w_router.shape[0]
    wr = jnp.zeros((d, LANES), BF16).at[:, :N_EXPERTS].set(w_router.astype(BF16))
    br = jnp.zeros((1, LANES), F32).at[0, :N_EXPERTS].set(b_router)
    return wr, br


def _route_out(t, d, tm):
    specs = [_row(tm, d), _row(tm, d), _row(tm, LANES), _row(tm, LANES)]
    shapes = [jax.ShapeDtypeStruct((t, d), F32), jax.ShapeDtypeStruct((t, d), BF16),
              jax.ShapeDtypeStruct((t, LANES), jnp.int32), jax.ShapeDtypeStruct((t, LANES), F32)]
    return specs, shapes


def _mod_kernel(c_ref, w_ref, b_ref, o_ref):
    c = c_ref[...]
    s = (c * jax.nn.sigmoid(c)).astype(BF16)
    o_ref[...] = jnp.dot(s, w_ref[0].astype(BF16), preferred_element_type=F32) + b_ref[0]


def _mod_vectors(c_rows, w_mod, b_mod, layer):
    d = c_rows.shape[1]
    return pl.pallas_call(
        _mod_kernel,
        grid=(N_MOD,),
        in_specs=[pl.BlockSpec(c_rows.shape, lambda n: (0, 0)),
                  pl.BlockSpec((1, d, d), lambda n: (layer, 0, n)),
                  pl.BlockSpec((1, 1, d), lambda n: (layer, 0, n))],
        out_specs=pl.BlockSpec((c_rows.shape[0], d), lambda n: (0, n)),
        out_shape=jax.ShapeDtypeStruct((c_rows.shape[0], N_MOD * d), F32),
        compiler_params=_CP,
        name="mod_vectors",
    )(c_rows, w_mod, b_mod.reshape(b_mod.shape[0], 1, -1))


def _even_in_kernel(x_ref, g_ref, shift_ref, scale_ref, w_ref, rc_ref, rs_ref, f_ref, q_ref, k_ref, v_ref, *, q_scale):
    h = _norm_mod(x_ref[...], g_ref, shift_ref, scale_ref)
    p = jnp.dot(h, w_ref[...], preferred_element_type=F32)
    f_ref[...] = p[:, :FOURIER_W]
    rc, rs = rc_ref[...], rs_ref[...]
    lane = lax.broadcasted_iota(jnp.int32, rc.shape, 1)
    first = (lane % DA_DH) < ROPE_AXIS_DIM

    def rope(t):
        partner = jnp.where(first, pltpu.roll(t, LANES - ROPE_AXIS_DIM, 1), pltpu.roll(t, ROPE_AXIS_DIM, 1))
        return t * rc + partner * rs

    for hd in range(DA_HEADS):
        cq = slice(FOURIER_W + hd * DA_VD, FOURIER_W + (hd + 1) * DA_VD)
        ck = slice(FOURIER_W + DA_W + hd * DA_VD, FOURIER_W + DA_W + (hd + 1) * DA_VD)
        q_ref[0, :, hd * DA_VD:(hd + 1) * DA_VD] = (rope(p[:, cq]) * q_scale).astype(BF16)
        k_ref[0, :, hd * DA_VD:(hd + 1) * DA_VD] = rope(p[:, ck]).astype(BF16)
    v_ref[0] = p[:, FOURIER_W + 2 * DA_W:].astype(BF16)


def _even_in_proj(x2d, g, shift, scale, w, cos, sin, bsz, n_seq, n_keys, q_scale, tm):
    t, d = x2d.shape
    bpb = n_seq // tm
    rc = jnp.tile(cos, (1, LANES // ROPE_AXIS_DIM))
    rs = jnp.tile(jnp.concatenate([-sin, sin], axis=1), (1, LANES // DA_DH))
    tab = pl.BlockSpec((tm, LANES), lambda i: (i % bpb, 0))
    seq = pl.BlockSpec((1, tm, DA_W), lambda i: (i // bpb, i % bpb, 0))
    vec = _vec(d, n_seq * (bsz // shift.shape[0]), tm)
    return pl.pallas_call(
        functools.partial(_even_in_kernel, q_scale=q_scale),
        grid=(t // tm,),
        in_specs=[_row(tm, d), _full((1, d)), vec, vec, _full(w.shape), tab, tab],
        out_specs=[_row(tm, FOURIER_W), seq, seq, seq],
        out_shape=[jax.ShapeDtypeStruct((t, FOURIER_W), F32), jax.ShapeDtypeStruct((bsz, n_seq, DA_W), BF16),
                   jax.ShapeDtypeStruct((bsz, n_keys, DA_W), BF16), jax.ShapeDtypeStruct((bsz, n_keys, DA_W), BF16)],
        compiler_params=_CP,
        name="even_in_proj",
    )(x2d, g.reshape(1, d), shift, scale, w, rc, rs)


def _diff_attn_kernel(lam_ref, q_ref, k_ref, v_ref, g_ref, o_ref, qs_ref, s_ref, m_ref, acc_ref, *,
                      tk, n_sub, out_scale):
    tq = q_ref.shape[1]
    n_kv = k_ref.shape[1] // tk
    rb = 2 * tq // n_sub
    q = q_ref[0]
    lane = lax.broadcasted_iota(jnp.int32, q.shape, 1)
    zero = jnp.zeros_like(q)
    qs_ref[:tq] = jnp.where(lane < DA_DH, q, zero)
    qs_ref[tq:] = jnp.where(lane >= DA_DH, q, zero)
    m_ref[...] = jnp.full(m_ref.shape, -1e30, F32)
    acc_ref[...] = jnp.zeros(acc_ref.shape, F32)
    ones = jnp.ones((tk, LANES), BF16)

    def scores(i, slot):
        off = pl.multiple_of(i * tk, tk)
        k = k_ref[0, pl.ds(off, tk), :]
        s_ref[slot] = lax.dot_general(qs_ref[...], k, (((1,), (1,)), ((), ())), preferred_element_type=F32)

    def consume(i, slot):
        off = pl.multiple_of(i * tk, tk)
        v_ext = jnp.concatenate([v_ref[0, pl.ds(off, tk), :], ones], axis=1)
        for r in range(n_sub):
            rows = pl.ds(r * rb, rb)
            s = s_ref[slot, rows, :]
            m_prev = m_ref[rows, :]
            m_new = jnp.maximum(m_prev, jnp.max(s, axis=1, keepdims=True))
            alpha = jnp.exp2(m_prev - m_new)
            p = jnp.exp2(s - jnp.tile(m_new, (1, tk // LANES)))
            pv = jnp.dot(p.astype(BF16), v_ext, preferred_element_type=F32)
            acc_ref[rows, :] = acc_ref[rows, :] * jnp.tile(alpha, (1, 2)) + pv
            m_ref[rows, :] = m_new

    scores(0, 0)

    def body(j, carry):
        scores(2 * j + 1, 1)
        consume(2 * j, 0)
        scores(2 * j + 2, 0)
        consume(2 * j + 1, 1)
        return carry

    lax.fori_loop(0, (n_kv - 1) // 2, body, 0)
    consume(n_kv - 1, 0)
    acc = acc_ref[...]
    o1 = acc[:tq, :LANES] / acc[:tq, LANES:]
    o2 = acc[tq:, :LANES] / acc[tq:, LANES:]
    o = o1 - lam_ref[0] * o2
    ms = jnp.mean(o * o, axis=-1, keepdims=True)
    o = o * lax.rsqrt(ms + SUBLN_EPS) * g_ref[...] * out_scale
    o_ref[0] = o.astype(o_ref.dtype)


def _diff_attention(lam, q, k_all, v_all, subln_g, out_scale, tq, tk, n_sub=2):
    b, n, _ = q.shape
    nk = k_all.shape[1]
    assert n % tq == 0 and nk % tk == 0 and tk % MXU_DIM == 0 and (nk // tk) % 2 == 1
    grid_spec = pltpu.PrefetchScalarGridSpec(
        num_scalar_prefetch=1,
        grid=(b, DA_HEADS, n // tq),
        in_specs=[pl.BlockSpec((1, tq, DA_VD), lambda bi, hi, qi, lam_r: (bi, qi, hi)),
                  pl.BlockSpec((1, nk, DA_VD), lambda bi, hi, qi, lam_r: (bi, 0, hi)),
                  pl.BlockSpec((1, nk, DA_VD), lambda bi, hi, qi, lam_r: (bi, 0, hi)),
                  pl.BlockSpec((1, DA_VD), lambda bi, hi, qi, lam_r: (0, 0))],
        out_specs=pl.BlockSpec((1, tq, DA_VD), lambda bi, hi, qi, lam_r: (bi, qi, hi)),
        scratch_shapes=[pltpu.VMEM((2 * tq, LANES), BF16), pltpu.VMEM((2, 2 * tq, tk), F32),
                        pltpu.VMEM((2 * tq, LANES), F32), pltpu.VMEM((2 * tq, 2 * LANES), F32)],
    )
    return pl.pallas_call(
        functools.partial(_diff_attn_kernel, tk=tk, n_sub=n_sub, out_scale=out_scale),
        grid_spec=grid_spec,
        out_shape=jax.ShapeDtypeStruct((b, n, DA_W), BF16),
        compiler_params=_CP,
        name="diff_attention",
    )(lam.reshape(1), q, k_all, v_all, subln_g.reshape(1, DA_VD))


def _fft_stage1_kernel(x_ref, fch_ref, f1_ref, tc_ref, ts_ref, yr_ref, yi_ref, *, nb):
    w = FOURIER_W
    r = x_ref.shape[1]
    for bl in range(nb):
        cols = slice(bl * w, (bl + 1) * w)
        ab = jnp.dot(x_ref[0, :, cols], fch_ref[...], preferred_element_type=F32, precision=HI)
        z = jnp.concatenate([ab[:, :w], ab[:, w:]], axis=0)
        y = jnp.dot(f1_ref[...], z, preferred_element_type=F32, precision=HI)
        yr, yi = y[:r], y[r:]
        tc, ts = tc_ref[:, cols], ts_ref[:, cols]
        yr_ref[0, :, cols] = yr * tc + yi * ts
        yi_ref[0, :, cols] = yi * tc - yr * ts


def _fft_stage2_kernel(yr_ref, yi_ref, f2_ref, o_ref, *, nc):
    w = FOURIER_W
    for cl in range(nc):
        y = jnp.concatenate([yr_ref[0, cl], yi_ref[0, cl]], axis=0)
        o_ref[0, :, cl * w:(cl + 1) * w] = jnp.dot(f2_ref[...], y, preferred_element_type=F32, precision=HI)


def _fourier_mix(f, nb=8):
    bsz, n, w = f.shape
    r = math.isqrt(n)
    assert r * r == n and r % nb == 0 and w == FOURIER_W
    two_pi = 2.0 * math.pi
    k = jnp.arange(r, dtype=jnp.int32)
    ang = two_pi * ((k[:, None] * k[None, :]) % r).astype(F32) / r
    c1, s1 = jnp.cos(ang), jnp.sin(ang)
    kc = jnp.arange(FOURIER_GD, dtype=jnp.int32)
    angc = two_pi * ((kc[:, None] * kc[None, :]) % FOURIER_GD).astype(F32) / FOURIER_GD
    eye_g = jnp.eye(FOURIER_GROUPS, dtype=F32)
    fch = jnp.concatenate([jnp.kron(eye_g, jnp.cos(angc)), jnp.kron(eye_g, jnp.sin(angc))], axis=1)
    f1 = jnp.concatenate([jnp.concatenate([c1, -s1], axis=1), jnp.concatenate([-s1, -c1], axis=1)], axis=0)
    f2 = jnp.concatenate([c1, s1], axis=1) * (1.0 / math.sqrt(n * FOURIER_GD))
    angt = two_pi * ((k[:, None] * k[None, :]) % n).astype(F32) / n
    tc = jnp.repeat(jnp.cos(angt), w, axis=1)
    ts = jnp.repeat(jnp.sin(angt), w, axis=1)
    xv = f.reshape(bsz, r, r * w)
    blk = pl.BlockSpec((1, r, nb * w), lambda bi, j: (bi, 0, j))
    tab = pl.BlockSpec((r, nb * w), lambda bi, j: (0, j))
    full = lambda a: pl.BlockSpec(a.shape, lambda bi, j: (0, 0))
    yshape = jax.ShapeDtypeStruct((bsz, r, r * w), F32)
    yr, yi = pl.pallas_call(
        functools.partial(_fft_stage1_kernel, nb=nb),
        grid=(bsz, r // nb),
        in_specs=[blk, full(fch), full(f1), tab, tab],
        out_specs=[blk, blk],
        out_shape=[yshape, yshape],
        compiler_params=_CP,
        name="fft_stage1",
    )(xv, fch, f1, tc, ts)
    y4 = lambda a: a.reshape(bsz, r, r, w)
    yblk = pl.BlockSpec((1, nb, r, w), lambda bi, j: (bi, j, 0, 0))
    out = pl.pallas_call(
        functools.partial(_fft_stage2_kernel, nc=nb),
        grid=(bsz, r // nb),
        in_specs=[yblk, yblk, full(f2)],
        out_specs=blk,
        out_shape=yshape,
        compiler_params=_CP,
        name="fft_stage2",
    )(y4(yr), y4(yi), f2)
    return out.reshape(bsz, n, w)


def _even_out_kernel(fm_ref, o_ref, w_ref, x_ref, gate_ref, g_ref, shift_ref, scale_ref, wr_ref, br_ref,
                     xo_ref, h_ref, idx_ref, gates_ref):
    y = (jnp.dot(fm_ref[...].astype(BF16), w_ref[:FOURIER_W], preferred_element_type=F32)
         + jnp.dot(o_ref[...], w_ref[FOURIER_W:], preferred_element_type=F32))
    _residual_norm_route(y, x_ref, gate_ref, g_ref, shift_ref, scale_ref, wr_ref, br_ref,
                         xo_ref, h_ref, idx_ref, gates_ref)


def _even_out_proj(fm, o, w, x2d, gate, g, shift, scale, w_router, b_router, rows_per_vec, tm):
    t, d = x2d.shape
    wr, br = _router_operands(w_router, b_router)
    vec = _vec(d, rows_per_vec, tm)
    out_specs, out_shape = _route_out(t, d, tm)
    return pl.pallas_call(
        _even_out_kernel,
        grid=(t // tm,),
        in_specs=[_row(tm, FOURIER_W), _row(tm, DA_W), _full(w.shape), _row(tm, d), vec, _full((1, d)), vec, vec,
                  _full((d, LANES)), _full((1, LANES))],
        out_specs=out_specs, out_shape=out_shape,
        compiler_params=_CP,
        name="even_out_proj",
    )(fm, o, w, x2d, gate, g.reshape(1, d), shift, scale, wr, br)


def _moe_kernel(be_ref, x_ref, w1_ref, perm_ref, b1g_ref, b1l_ref, w2_ref, b2_ref, *rest):
    o_ref, w1g_s, w1l_s, w2_s = rest[-4:]
    i = pl.program_id(0)

    @pl.when((i == 0) | (be_ref[i] != be_ref[jnp.maximum(i - 1, 0)]))
    def _():
        for c in range(w1_ref.shape[3] // MXU_DIM):
            blk = jnp.dot(w1_ref[0, 0, :, c * MXU_DIM:(c + 1) * MXU_DIM].astype(BF16), perm_ref[...],
                          preferred_element_type=F32)
            w1g_s[:, c * LANES:(c + 1) * LANES] = blk[:, :LANES].astype(BF16)
            w1l_s[:, c * LANES:(c + 1) * LANES] = blk[:, LANES:].astype(BF16)
        w2_s[...] = w2_ref[0, 0].astype(BF16)

    x = x_ref[...]
    ug = jnp.dot(x, w1g_s[...], preferred_element_type=F32) + b1g_ref[0]
    ul = jnp.dot(x, w1l_s[...], preferred_element_type=F32) + b1l_ref[0]
    glu = jnp.minimum(ug, SWIGLU_LIMIT)
    lin = jnp.clip(ul, -SWIGLU_LIMIT, SWIGLU_LIMIT)
    act = glu * jax.nn.sigmoid(SWIGLU_ALPHA * glu) * (lin + 1.0)
    y = jnp.dot(act.astype(BF16), w2_s[...], preferred_element_type=F32) + b2_ref[0]
    o_ref[...] = y.astype(o_ref.dtype)


def _moe_experts(block_expert, xb, w1_all, b1g, b1l, w2_all, layer, b2, yb_prev, first_block, n_blocks_total):
    n_rows, d = xb.shape
    n_blocks = n_rows // MOE_BLOCK
    src = jnp.arange(MXU_DIM)[:, None]
    dst = jnp.arange(MXU_DIM)[None, :]
    perm = (src == jnp.where(dst < LANES, 2 * dst, 2 * (dst - LANES) + 1)).astype(BF16)
    bspec = lambda s: pl.BlockSpec((1,) + s, lambda i, be: (be[i], 0, 0))
    wspec = lambda s: pl.BlockSpec((1, 1) + s, lambda i, be: (layer, be[i], 0, 0))
    in_specs = [pl.BlockSpec((MOE_BLOCK, d), lambda i, be: (i, 0)),
                wspec((d, 2 * D_FF)), pl.BlockSpec((MXU_DIM, MXU_DIM), lambda i, be: (0, 0)),
                bspec((1, D_FF)), bspec((1, D_FF)), wspec((D_FF, d)), bspec((1, d))]
    args = [block_expert, xb, w1_all, perm, b1g, b1l, w2_all, b2]
    aliases = {}
    if yb_prev is not None:
        in_specs.append(pl.BlockSpec(memory_space=pl.ANY))
        aliases = {len(args): 0}
        args.append(yb_prev)
    grid_spec = pltpu.PrefetchScalarGridSpec(
        num_scalar_prefetch=1,
        grid=(n_blocks,),
        in_specs=in_specs,
        out_specs=pl.BlockSpec((MOE_BLOCK, d), lambda i, be: (first_block + i, 0)),
        scratch_shapes=[pltpu.VMEM((d, D_FF), BF16), pltpu.VMEM((d, D_FF), BF16), pltpu.VMEM((D_FF, d), BF16)],
    )
    return pl.pallas_call(
        _moe_kernel,
        grid_spec=grid_spec,
        out_shape=jax.ShapeDtypeStruct((n_blocks_total * MOE_BLOCK, d), BF16),
        input_output_aliases=aliases,
        compiler_params=_CP,
        name="moe_experts",
    )(*args)


def _rank_kernel(idx_ref, rank_ref, cnt_ref, carry_ref):
    i = pl.program_id(0)

    @pl.when(i == 0)
    def _():
        carry_ref[...] = jnp.zeros(carry_ref.shape, F32)

    idx = idx_ref[...]
    tm = idx.shape[0]
    lane = lax.broadcasted_iota(jnp.int32, idx.shape, 1)
    sel = [lane == idx[:, j:j + 1] for j in range(TOP_K)]
    onehot = sel[0].astype(F32)
    for j in range(1, TOP_K):
        onehot = onehot + sel[j].astype(F32)
    row = lax.broadcasted_iota(jnp.int32, (tm, tm), 0)
    col = lax.broadcasted_iota(jnp.int32, (tm, tm), 1)
    before = (row > col).astype(BF16)
    prefix = jnp.dot(before, onehot.astype(BF16), preferred_element_type=F32) + carry_ref[0:1, :]
    rank = jnp.zeros(idx.shape, F32)
    for j in range(TOP_K):
        rank = jnp.where(lane == j, jnp.sum(jnp.where(sel[j], prefix, 0.0), axis=-1, keepdims=True), rank)
    rank_ref[...] = rank.astype(jnp.int32)
    carry_ref[0:1, :] = carry_ref[0:1, :] + jnp.sum(onehot, axis=0, keepdims=True)
    cnt_ref[...] = carry_ref[...]


def _expert_ranks(idx, tm):
    t = idx.shape[0]
    rank, cnt = pl.pallas_call(
        _rank_kernel,
        grid=(t // tm,),
        in_specs=[_row(tm, LANES)],
        out_specs=[_row(tm, LANES), _full((8, LANES))],
        out_shape=[jax.ShapeDtypeStruct((t, LANES), jnp.int32), jax.ShapeDtypeStruct((8, LANES), F32)],
        scratch_shapes=[pltpu.VMEM((8, LANES), F32)],
        compiler_params=_CP,
        name="expert_ranks",
    )(idx)
    return rank, cnt[0, :N_EXPERTS].astype(jnp.int32)


def _pos_kernel(idx_ref, rank_ref, ps_ref, pos_ref):
    idx = idx_ref[...]
    lane = lax.broadcasted_iota(jnp.int32, idx.shape, 1)
    pos = rank_ref[...].astype(F32)
    for j in range(TOP_K):
        start = jnp.sum(jnp.where(lane == idx[:, j:j + 1], ps_ref[...], 0.0), axis=-1, keepdims=True)
        pos = jnp.where(lane == j, pos + start, pos)
    pos_ref[...] = jnp.transpose(pos)[:pos_ref.shape[0]].astype(jnp.int32)


def _expert_rows(idx, rank, pad_starts):
    t = idx.shape[0]
    tm = max(m for m in range(LANES, TM_ELEMENTWISE + 1, LANES) if t % m == 0)
    ps = jnp.zeros((1, LANES), F32).at[0, :N_EXPERTS].set(pad_starts.astype(F32))
    return pl.pallas_call(
        _pos_kernel,
        grid=(t // tm,),
        in_specs=[_row(tm, LANES), _row(tm, LANES), _full((1, LANES))],
        out_specs=pl.BlockSpec((8, tm), lambda i: (0, i)),
        out_shape=jax.ShapeDtypeStruct((8, t), jnp.int32),
        compiler_params=_CP,
        name="expert_rows",
    )(idx, rank, ps)


def _moe_parts(n_blocks):
    parts, first, size = [], 0, max(1, n_blocks // MOE_FIRST_PART)
    while first < n_blocks:
        if n_blocks - first < 2 * size:
            size = n_blocks - first
        parts.append((first, size))
        first += size
        size *= 2
    return parts


def _moe_ffn(h, idx, w1_all, b1, w2_all, b2, layer, tm):
    n_tok = h.shape[0]
    n_assign = n_tok * TOP_K
    e_flat = idx[:, :TOP_K].reshape(n_assign)
    shift_bits = (n_assign - 1).bit_length()
    assert N_EXPERTS << shift_bits < 2 ** 31
    packed = jnp.sort((e_flat << shift_bits) | jnp.arange(n_assign, dtype=jnp.int32))
    order = packed & ((1 << shift_bits) - 1)
    rank, counts = _expert_ranks(idx, tm)
    starts = jnp.cumsum(counts) - counts
    padded = (counts + MOE_BLOCK - 1) // MOE_BLOCK * MOE_BLOCK
    pad_ends = jnp.cumsum(padded)
    pad_starts = pad_ends - padded
    n_blocks = -(-(n_assign + N_EXPERTS * (MOE_BLOCK - 1)) // MOE_BLOCK)
    n_rows = n_blocks * MOE_BLOCK
    block_start = jnp.arange(n_blocks, dtype=jnp.int32) * MOE_BLOCK
    block_expert = jnp.minimum(jnp.sum(pad_ends[None, :] <= block_start[:, None], axis=1, dtype=jnp.int32),
                               N_EXPERTS - 1)
    shift = pad_starts - starts
    pos = _expert_rows(idx, rank, pad_starts)
    r = jnp.arange(n_rows, dtype=jnp.int32)
    src = jnp.clip(r - jnp.repeat(shift[block_expert], MOE_BLOCK), 0, n_assign - 1)
    row_tok = order[src] // TOP_K
    yb = None
    for first, size in _moe_parts(n_blocks):
        xb = h[row_tok[first * MOE_BLOCK:(first + size) * MOE_BLOCK]]
        yb = _moe_experts(block_expert[first:first + size], xb, w1_all, b1[:, None, 0::2], b1[:, None, 1::2],
                          w2_all, layer, b2[:, None, :], yb, first, n_blocks)
    return yb, pos


def _combine_kernel(*refs, final):
    y_refs = refs[:TOP_K]
    gt_ref, x_ref, g5_ref = refs[TOP_K:TOP_K + 3]
    o_ref = refs[-1]
    gt = gt_ref[...]
    acc = y_refs[0][...].astype(F32) * gt[:, 0:1]
    for j in range(1, TOP_K):
        acc = acc + y_refs[j][...].astype(F32) * gt[:, j:j + 1]
    xn = x_ref[...] + g5_ref[0] * acc
    if final:
        fg_ref = refs[TOP_K + 3]
        xn = xn * lax.rsqrt(jnp.mean(xn * xn, axis=-1, keepdims=True) + NORM_EPS) * fg_ref[...]
    o_ref[...] = xn


def _moe_combine(yb, pos, gates, x2d, gate5, rows_per_vec, tm, final_g=None):
    t, d = x2d.shape
    ys = [yb[pos[j]] for j in range(TOP_K)]
    in_specs = [_row(tm, d)] * TOP_K + [_row(tm, LANES), _row(tm, d), _vec(d, rows_per_vec, tm)]
    args = ys + [gates, x2d, gate5]
    if final_g is not None:
        in_specs.append(_full((1, d)))
        args.append(final_g.reshape(1, d))
    return pl.pallas_call(
        functools.partial(_combine_kernel, final=final_g is not None),
        grid=(t // tm,),
        in_specs=in_specs,
        out_specs=_row(tm, d),
        out_shape=jax.ShapeDtypeStruct((t, d), F32),
        compiler_params=_CP,
        name="moe_combine",
    )(*args)


HALO = SUBLANES_BF16


def _odd_in_kernel(xm_ref, xp_ref, xn_ref, g_ref, shift_ref, scale_ref, wqkv_ref, wz_ref, wab_ref, cw_ref,
                   alog_ref, dtb_ref, q_ref, k_ref, v_ref, z_ref, gb_ref):
    i = pl.program_id(1)
    n_i = pl.num_programs(1)
    tm = xm_ref.shape[1]
    half = GDN_CONV // 2
    keep_prev = (i > 0).astype(BF16)
    keep_next = (i < n_i - 1).astype(BF16)
    h_main = _norm_mod(xm_ref[0], g_ref, shift_ref, scale_ref)
    h_ext = jnp.concatenate([_norm_mod(xp_ref[0], g_ref, shift_ref, scale_ref) * keep_prev, h_main,
                             _norm_mod(xn_ref[0], g_ref, shift_ref, scale_ref) * keep_next], axis=0)
    gw = MXU_DIM
    for cg in range(GDN_QKV_W // gw):
        cols = slice(cg * gw, (cg + 1) * gw)
        ext = jnp.dot(h_ext, wqkv_ref[:, cols], preferred_element_type=F32)
        acc = ext[HALO:HALO + tm] * cw_ref[half:half + 1, cols]
        for j in range(GDN_CONV):
            if j != half:
                sh = pltpu.roll(ext, (half - j) % (tm + 2 * HALO), 0)[HALO:HALO + tm]
                acc = acc + sh * cw_ref[j:j + 1, cols]
        y = acc * jax.nn.sigmoid(acc)
        for sub in range(gw // LANES):
            hd = cg * (gw // LANES) + sub
            yh = y[:, sub * LANES:(sub + 1) * LANES]
            if hd < 2 * GDN_HK:
                yh = yh * lax.rsqrt(jnp.sum(yh * yh, axis=-1, keepdims=True) + 1e-6)
                if hd < GDN_HK:
                    q_ref[0, :, hd * LANES:(hd + 1) * LANES] = yh * (GDN_DK ** -0.5)
                else:
                    k_ref[0, :, (hd - GDN_HK) * LANES:(hd - GDN_HK + 1) * LANES] = yh
            else:
                v_ref[0, :, (hd - 2 * GDN_HK) * LANES:(hd - 2 * GDN_HK + 1) * LANES] = yh
    z_ref[0] = jnp.dot(h_main, wz_ref[...], preferred_element_type=F32).astype(BF16)
    ab = jnp.dot(h_main, wab_ref[...], preferred_element_type=F32)
    xa = ab + dtb_ref[...]
    softplus = jnp.maximum(xa, 0.0) + jnp.log1p(jnp.exp(-jnp.abs(xa)))
    lane = lax.broadcasted_iota(jnp.int32, ab.shape, 1)
    gb_ref[0] = jnp.where(lane < 2 * GDN_HV, -jnp.exp(alog_ref[...]) * softplus, jax.nn.sigmoid(ab))


def _odd_in_stage(x3d, g, shift, scale, w_in, conv_w, a_log, dt_bias, tm):
    b, t, d = x3d.shape
    w_qkv = w_in[:, :GDN_QKV_W].astype(BF16)
    w_z = w_in[:, GDN_QKV_W:GDN_MAIN_W].astype(BF16)
    w_ab = jnp.zeros((d, LANES), BF16).at[:, :4 * GDN_HV].set(w_in[:, GDN_MAIN_W:].astype(BF16))
    nb = tm // HALO
    last = t // HALO - 1
    pad = lambda a: jnp.zeros((1, LANES), F32).at[0, :2 * GDN_HV].set(a.reshape(-1))
    per_vec = b // shift.shape[0]
    vec = pl.BlockSpec((1, 1, d), lambda bi, i: (bi // per_vec, 0, 0))
    seq = lambda n: pl.BlockSpec((1, tm, n), lambda bi, i: (bi, i, 0))
    full = lambda s: pl.BlockSpec(s, lambda bi, i: (0,) * len(s))
    f = lambda n, dt=F32: jax.ShapeDtypeStruct((b, t, n), dt)
    return pl.pallas_call(
        _odd_in_kernel,
        grid=(b, t // tm),
        in_specs=[seq(d),
                  pl.BlockSpec((1, HALO, d), lambda bi, i: (bi, jnp.maximum(i * nb - 1, 0), 0)),
                  pl.BlockSpec((1, HALO, d), lambda bi, i: (bi, jnp.minimum((i + 1) * nb, last), 0)),
                  full((1, d)), vec, vec, full(w_qkv.shape), full(w_z.shape), full(w_ab.shape),
                  full((GDN_CONV, GDN_QKV_W)), full((1, LANES)), full((1, LANES))],
        out_specs=[seq(GDN_QK_W), seq(GDN_QK_W), seq(GDN_V_W), seq(GDN_V_W), seq(LANES)],
        out_shape=[f(GDN_QK_W), f(GDN_QK_W), f(GDN_V_W), f(GDN_V_W, BF16), f(LANES)],
        compiler_params=_CP,
        name="odd_in_stage",
    )(x3d, x3d, x3d, g.reshape(1, d), shift, scale, w_qkv, w_z, w_ab, conv_w, pad(a_log), pad(dt_bias))


def _gdn_chunk_kernel(q_ref, k_ref, v_ref, gb_ref, s0_ref, *rest, reverse, g_lane, b_lane):
    o_ref, sfin_ref, s_ref = rest[-3:]
    add_ref = rest[0] if len(rest) == 4 else None
    c = pl.program_id(0)
    n_c = pl.num_programs(0)
    cs = GDN_CHUNK
    rep = GDN_HV // GDN_HK
    nb = q_ref.shape[0]

    @pl.when(c == 0)
    def _():
        s_ref[...] = s0_ref[...].reshape(s_ref.shape)

    row = lax.broadcasted_iota(jnp.int32, (cs, cs), 0)
    col = lax.broadcasted_iota(jnp.int32, (cs, cs), 1)
    incl = (row <= col) if reverse else (row >= col)
    strict = (row < col) if reverse else (row > col)
    eye = (row == col).astype(F32)
    incl_b = incl.astype(BF16)
    last = 0 if reverse else cs - 1
    tn = (((0,), (1,)), ((), ()))

    heads = range(GDN_HV)
    gc, gr, bc, qh, kh, vh = [], [], [], [], [], []
    for b in range(nb):
        gb = gb_ref[b]
        gb_hi = gb.astype(BF16)
        gb_lo = (gb - gb_hi.astype(F32)).astype(BF16)
        g_col = (jnp.dot(incl_b, gb_hi, preferred_element_type=F32)
                 + jnp.dot(incl_b, gb_lo, preferred_element_type=F32))
        g_row = (lax.dot_general(gb_hi, incl_b, tn, preferred_element_type=F32)
                 + lax.dot_general(gb_lo, incl_b, tn, preferred_element_type=F32))
        gc += [g_col[:, g_lane + h:g_lane + h + 1] for h in heads]
        gr += [g_row[g_lane + h:g_lane + h + 1, :] for h in heads]
        bc += [gb[:, b_lane + h:b_lane + h + 1] for h in heads]
        qh += [q_ref[b, :, h * GDN_DK:(h + 1) * GDN_DK] for h in range(GDN_HK)]
        kh += [k_ref[b, :, h * GDN_DK:(h + 1) * GDN_DK] for h in range(GDN_HK)]
        vh += [v_ref[b, :, h * GDN_DV:(h + 1) * GDN_DV] for h in heads]
    gc, gr, bc = jnp.stack(gc), jnp.stack(gr), jnp.stack(bc)
    qh, kh, vh = jnp.stack(qh), jnp.stack(kh), jnp.stack(vh)
    ge = gc[:, last:last + 1, :]
    kh_b = kh.astype(BF16)
    bnt = (((2,), (2,)), ((0,), (0,)))
    kk = lax.dot_general(kh_b, kh_b, bnt, preferred_element_type=F32)
    qk = lax.dot_general(qh.astype(BF16), kh_b, bnt, preferred_element_type=F32)
    kk, qk = jnp.repeat(kk, rep, axis=0), jnp.repeat(qk, rep, axis=0)
    qv, kv = jnp.repeat(qh, rep, axis=0), jnp.repeat(kh, rep, axis=0)

    decay = jnp.where(incl, jnp.exp(jnp.where(incl, gc - gr, 0.0)), 0.0)
    lm = jnp.where(strict, bc * kk * decay, 0.0)
    bmm = lambda a, b: jnp.einsum('hij,hjk->hik', a.astype(BF16), b.astype(BF16), preferred_element_type=F32)
    blk = lambda n: (row // n) == (col // n)
    l0 = jnp.where(blk(16), lm, 0.0)
    p = bmm(l0, l0)
    x = eye - l0
    for _ in range(2):
        xp = bmm(jnp.concatenate([x, p], axis=1), p)
        x = x + xp[:, :cs]
        p = xp[:, cs:]
    x = x + bmm(x, p)
    n = 32
    while n <= cs:
        off = jnp.where(blk(n) & ~blk(n // 2), lm, 0.0)
        x = x - bmm(bmm(x, off), x)
        n *= 2
    eg = jnp.exp(gc)
    uw = bmm(x, jnp.concatenate([vh * bc, kv * (bc * eg)], axis=2))
    u = uw[:, :, :GDN_DV]
    w = uw[:, :, GDN_DV:]
    qg = qv * eg
    intra = jnp.where(incl, qk * decay, 0.0)
    kt = kv * jnp.exp(ge - gc)
    s = s_ref[...]
    wq = bmm(jnp.concatenate([w, qg], axis=1), s)
    v_new = u - wq[:, :cs]
    o = wq[:, cs:] + bmm(intra, v_new)
    for b in range(nb):
        for h in heads:
            cols = slice(h * GDN_DV, (h + 1) * GDN_DV)
            oh = o[b * GDN_HV + h]
            if add_ref is not None:
                oh = oh + add_ref[b, :, cols]
            o_ref[b, :, cols] = oh.astype(o_ref.dtype)
    s_ref[...] = s * jnp.exp(ge) + jnp.einsum('hck,hcv->hkv', kt.astype(BF16), v_new.astype(BF16),
                                              preferred_element_type=F32)

    @pl.when(c == n_c - 1)
    def _():
        sfin_ref[...] = s_ref[...].reshape(sfin_ref.shape)


def _gdn_scan(q, k, v, gb, s0, reverse, add=None, out_dtype=F32):
    b, t, _ = q.shape
    assert t % GDN_CHUNK == 0 and GDN_CHUNK % 32 == 0
    n_c = t // GDN_CHUNK
    cm = (lambda ci: (0, n_c - 1 - ci, 0)) if reverse else (lambda ci: (0, ci, 0))
    d = 1 if reverse else 0
    smap = lambda ci: (0, 0, 0, 0)
    in_specs = [pl.BlockSpec((b, GDN_CHUNK, GDN_QK_W), cm), pl.BlockSpec((b, GDN_CHUNK, GDN_QK_W), cm),
                pl.BlockSpec((b, GDN_CHUNK, GDN_V_W), cm), pl.BlockSpec((b, GDN_CHUNK, LANES), cm),
                pl.BlockSpec((b, GDN_HV, GDN_DK, GDN_DV), smap)]
    args = [q, k, v, gb, s0]
    if add is not None:
        in_specs.append(pl.BlockSpec((b, GDN_CHUNK, GDN_V_W), cm))
        args.append(add)
    return pl.pallas_call(
        functools.partial(_gdn_chunk_kernel, reverse=reverse, g_lane=d * GDN_HV, b_lane=(2 + d) * GDN_HV),
        grid=(n_c,),
        in_specs=in_specs,
        out_specs=[pl.BlockSpec((b, GDN_CHUNK, GDN_V_W), cm), pl.BlockSpec((b, GDN_HV, GDN_DK, GDN_DV), smap)],
        out_shape=[jax.ShapeDtypeStruct((b, t, GDN_V_W), out_dtype),
                   jax.ShapeDtypeStruct((b, GDN_HV, GDN_DK, GDN_DV), F32)],
        scratch_shapes=[pltpu.VMEM((b * GDN_HV, GDN_DK, GDN_DV), F32)],
        compiler_params=_CP,
        name="gdn_scan_bwd" if reverse else "gdn_scan_fwd",
    )(*args)


def _odd_out_kernel(o_ref, z_ref, ng_ref, w_ref, x_ref, gate_ref, g_ref, shift_ref, scale_ref,
                    wr_ref, br_ref, xo_ref, h_ref, idx_ref, gates_ref):
    parts = []
    for h in range(GDN_HV):
        cols = slice(h * GDN_DV, (h + 1) * GDN_DV)
        o = o_ref[:, cols].astype(F32)
        z = z_ref[:, cols].astype(F32)
        o = o * lax.rsqrt(jnp.mean(o * o, axis=-1, keepdims=True) + NORM_EPS) * ng_ref[...]
        parts.append((o * (z * jax.nn.sigmoid(z))).astype(BF16))
    y = jnp.dot(jnp.concatenate(parts, axis=1), w_ref[...], preferred_element_type=F32)
    _residual_norm_route(y, x_ref, gate_ref, g_ref, shift_ref, scale_ref, wr_ref, br_ref,
                         xo_ref, h_ref, idx_ref, gates_ref)


def _odd_out_proj(o, z, norm_g, w, x2d, gate, g, shift, scale, w_router, b_router, rows_per_vec, tm):
    t, d = x2d.shape
    wr, br = _router_operands(w_router, b_router)
    vec = _vec(d, rows_per_vec, tm)
    out_specs, out_shape = _route_out(t, d, tm)
    return pl.pallas_call(
        _odd_out_kernel,
        grid=(t // tm,),
        in_specs=[_row(tm, GDN_V_W), _row(tm, GDN_V_W),
                  _full((1, GDN_DV)), _full(w.shape), _row(tm, d), vec, _full((1, d)), vec, vec,
                  _full((d, LANES)), _full((1, LANES))],
        out_specs=out_specs, out_shape=out_shape,
        compiler_params=_CP,
        name="odd_out_proj",
    )(o, z, norm_g.reshape(1, GDN_DV), w, x2d, gate, g.reshape(1, d), shift, scale, wr, br)


def _split_mod(mv, bsz):
    d = mv.shape[1] // N_MOD
    lat = [mv[:bsz, j * d:(j + 1) * d][:, None, :] for j in range(N_MOD)]
    ctx = [mv[bsz:bsz + 1, j * d:(j + 1) * d][:, None, :] for j in range(N_MOD)]
    return lat, ctx


def _even_layer(x2d, xc2d, mod, mod_c, norm1_g, norm2_g, w_in, w_out, lam_p, subln_g, lam_init, cos, sin,
                w_router, b_router, bsz, n_lat, n_ctx):
    w_in_b = w_in.astype(BF16)
    w_out_b = w_out.astype(BF16)
    q_scale = DA_DH ** -0.5 * math.log2(math.e)
    f, q, k_all, v_all = _even_in_proj(x2d, norm1_g, mod[0], mod[1], w_in_b, cos, sin, bsz, n_lat, n_lat + n_ctx,
                                       q_scale, TM_PROJ)
    ones, zeros = jnp.ones((n_ctx, ROPE_AXIS_DIM), F32), jnp.zeros((n_ctx, ROPE_AXIS_DIM), F32)
    fc, qc, kc, vc = _even_in_proj(xc2d, norm1_g, mod_c[0], mod_c[1], w_in_b, ones, zeros, bsz, n_ctx, n_ctx,
                                   q_scale, n_ctx)
    lp = lam_p.astype(F32)
    lam = jnp.exp(jnp.sum(lp[0] * lp[1])) - jnp.exp(jnp.sum(lp[2] * lp[3])) + lam_init
    k_all = lax.dynamic_update_slice(k_all, kc, (0, n_lat, 0))
    v_all = lax.dynamic_update_slice(v_all, vc, (0, n_lat, 0))
    o = _diff_attention(lam, q, k_all, v_all, subln_g, 1.0 - lam_init, TQ_ATTN, TK_ATTN)
    oc = _diff_attention(lam, qc, kc, vc, subln_g, 1.0 - lam_init, n_ctx, n_ctx)
    fm = _fourier_mix(f.reshape(bsz, n_lat, FOURIER_W)).reshape(bsz * n_lat, FOURIER_W)
    fmc = _fourier_mix(fc.reshape(bsz, n_ctx, FOURIER_W)).reshape(bsz * n_ctx, FOURIER_W)
    lat = _even_out_proj(fm, o.reshape(bsz * n_lat, DA_W), w_out_b, x2d, mod[2], norm2_g, mod[3], mod[4],
                         w_router, b_router, n_lat, TM_PROJ)
    ctx = _even_out_proj(fmc, oc.reshape(bsz * n_ctx, DA_W), w_out_b, xc2d, mod_c[2], norm2_g, mod_c[3], mod_c[4],
                         w_router, b_router, bsz * n_ctx, n_ctx)
    return lat, ctx


def _odd_layer(x2d, xc2d, mod, mod_c, norm1_g, norm2_g, w_in, conv_w, a_log, dt_bias, norm_g, w_out,
               w_router, b_router, bsz, n_lat, n_ctx):
    d = x2d.shape[1]
    q, k, v, z, gb = _odd_in_stage(x2d.reshape(bsz, n_lat, d), norm1_g, mod[0], mod[1], w_in, conv_w, a_log,
                                   dt_bias, TM_PROJ)
    qc, kc, vc, _, gbc = _odd_in_stage(xc2d.reshape(bsz, n_ctx, d), norm1_g, mod_c[0], mod_c[1], w_in, conv_w,
                                       a_log, dt_bias, n_ctx)
    s0 = jnp.zeros((bsz, GDN_HV, GDN_DK, GDN_DV), F32)
    _, sc_f = _gdn_scan(qc, kc, vc, gbc, s0, False)
    o_f, _ = _gdn_scan(q, k, v, gb, sc_f, False)
    _, sc_b = _gdn_scan(qc, kc, vc, gbc, s0, True)
    o, _ = _gdn_scan(q, k, v, gb, sc_b, True, add=o_f, out_dtype=BF16)
    return _odd_out_proj(o.reshape(bsz * n_lat, GDN_V_W), z.reshape(bsz * n_lat, GDN_V_W), norm_g,
                         w_out.astype(BF16), x2d, mod[2], norm2_g, mod[3], mod[4], w_router, b_router, n_lat,
                         TM_PROJ)


def kernel(x, c, ctx, c_ctx, norm1_g, norm2_g, w_mod, b_mod, ev_w_in, ev_w_out, ev_lam, ev_subln_g,
           od_w_in, od_conv_w, od_a_log, od_dt_bias, od_norm_g, od_w_out,
           moe_w_router, moe_b_router, moe_w1, moe_b1, moe_w2, moe_b2, final_g):
    bsz, n_lat, d = x.shape
    n_ctx = ctx.shape[1]
    assert w_mod.shape[0] == 2, "kernel is written for one even (attention) and one odd (DeltaNet) layer"
    t_lat = bsz * n_lat
    cos, sin = _axial_rope_tables(n_lat // GRID_W)
    c_rows = jnp.zeros((8, d), F32).at[:bsz].set(c).at[bsz].set(c_ctx)
    x2d = x.reshape(t_lat, d)
    xc2d = ctx.reshape(bsz * n_ctx, d)

    mod, mod_c = _split_mod(_mod_vectors(c_rows, w_mod, b_mod, 0), bsz)
    (x2d, h2, idx, gates), (xc2d, h2c, idx_c, gates_c) = _even_layer(
        x2d, xc2d, mod, mod_c, norm1_g[0], norm2_g[0], ev_w_in[0], ev_w_out[0], ev_lam[0], ev_subln_g[0],
        _diff_lambda_init(0), cos, sin, moe_w_router[0], moe_b_router[0], bsz, n_lat, n_ctx)
    yb, pos = _moe_ffn(jnp.concatenate([h2, h2c], axis=0), jnp.concatenate([idx, idx_c], axis=0),
                       moe_w1, moe_b1[0], moe_w2, moe_b2[0], 0, TM_PROJ)
    x2d = _moe_combine(yb, pos[:, :t_lat], gates, x2d, mod[5], n_lat, TM_PROJ)
    xc2d = _moe_combine(yb, pos[:, t_lat:], gates_c, xc2d, mod_c[5], bsz * n_ctx, n_ctx)

    mod, mod_c = _split_mod(_mod_vectors(c_rows, w_mod, b_mod, 1), bsz)
    x2d, h2, idx, gates = _odd_layer(x2d, xc2d, mod, mod_c, norm1_g[1], norm2_g[1], od_w_in[0], od_conv_w[0],
                                     od_a_log[0], od_dt_bias[0], od_norm_g[0], od_w_out[0],
                                     moe_w_router[1], moe_b_router[1], bsz, n_lat, n_ctx)
    yb, pos = _moe_ffn(h2, idx, moe_w1, moe_b1[1], moe_w2, moe_b2[1], 1, TM_PROJ)
    return _moe_combine(yb, pos, gates, x2d, mod[5], n_lat, TM_PROJ, final_g=final_g).reshape(bsz, n_lat, d)
```
